```python
import jax, jax.numpy as jnp
from jax import lax
import numpy as np

D_MODEL = 1024
BATCH = 4
SEQ = 8192
DEPTH = 4

HEAD_DIM = 64
Q_BLOCK = 128
ROPE_THETA = 10000.0
EPS = 1e-6
NSA_HEADS = 8
NSA_KV_GROUPS = 2
NSA_CMP_LEN = 32
NSA_CMP_STRIDE = 16
NSA_CMP_HIDDEN = 128
NSA_SEL_LEN = 64
NSA_SEL_TOPK = 16
NSA_WINDOW = 512
FORCE_SCORE = 1e9
NSA_WIDTH = NSA_HEADS * HEAD_DIM
NSA_KV_WIDTH = NSA_KV_GROUPS * HEAD_DIM
FOX_HEADS = 8
FOX_WIDTH = FOX_HEADS * HEAD_DIM
MEM_LEN = 256
MEM_HEADS = 4
MEM_HEAD_DIM = 128
MEM_WIDTH = MEM_HEADS * MEM_HEAD_DIM
N_BRANCH = 3
IN_SPLITS = (NSA_WIDTH, 6 * NSA_KV_WIDTH, 3 * NSA_HEADS, NSA_WIDTH,
             3 * FOX_WIDTH, FOX_HEADS, FOX_WIDTH,
             MEM_WIDTH, MEM_WIDTH,
             N_BRANCH * D_MODEL)
IN_WIDTH = sum(IN_SPLITS)

kernel_name = 'hybrid_nsa_fox_memory_block'


def _rmsnorm(x, g):
    xf = x.astype(jnp.float32)
    y = xf * lax.rsqrt(jnp.mean(xf * xf, axis=-1, keepdims=True) + EPS)
    return (y * g.astype(jnp.float32)).astype(x.dtype)


def _rope(x, pos):
    half = x.shape[-1] // 2
    inv_freq = ROPE_THETA ** (-jnp.arange(half, dtype=jnp.float32) / half)
    ang = pos.astype(jnp.float32)[:, None] * inv_freq[None, :]
    cos = jnp.cos(ang)[None, :, None, :]
    sin = jnp.sin(ang)[None, :, None, :]
    xf = x.astype(jnp.float32)
    x1, x2 = xf[..., :half], xf[..., half:]
    return jnp.concatenate([x1 * cos - x2 * sin, x2 * cos + x1 * sin], axis=-1).astype(x.dtype)


def _masked_softmax(s, mask):
    s = jnp.where(mask, s.astype(jnp.float32), -jnp.inf)
    m = jnp.max(s, axis=-1, keepdims=True)
    m = jnp.where(jnp.isfinite(m), m, 0.0)
    e = jnp.exp(s - m)
    d = jnp.sum(e, axis=-1, keepdims=True)
    return e / jnp.where(d > 0.0, d, 1.0)


def _split(z, sizes):
    return jnp.split(z, np.cumsum(sizes)[:-1].tolist(), axis=-1)


def _compress(x, pe, w1, w2):
    B, T, G, dk = x.shape
    r = NSA_CMP_LEN // NSA_CMP_STRIDE
    n_chunk = T // NSA_CMP_STRIDE
    n_cmp = n_chunk - r + 1
    ch = x.reshape(B, n_chunk, NSA_CMP_STRIDE, G, dk)
    blk = jnp.concatenate([ch[:, i:i + n_cmp] for i in range(r)], axis=2)
    blk = blk + pe[None, None, :, None, :].astype(x.dtype)
    flat = blk.transpose(0, 1, 3, 2, 4).reshape(B, n_cmp, G, NSA_CMP_LEN * dk)
    return jax.nn.silu(flat @ w1) @ w2


def _overlap(n_cmp, n_slc):
    cs = np.arange(n_cmp)[:, None] * NSA_CMP_STRIDE
    ss = np.arange(n_slc)[None, :] * NSA_SEL_LEN
    ov = np.clip(np.minimum(cs + NSA_CMP_LEN, ss + NSA_SEL_LEN) - np.maximum(cs, ss), 0, None)
    return jnp.asarray(ov / NSA_CMP_LEN, dtype=jnp.float32)


def _nsa(q, kv, gate, pe_k, w1_k, w2_k, pe_v, w1_v, w2_v, q_g, k_g):
    B, T, _ = q.shape
    G, R, dk = NSA_KV_GROUPS, NSA_HEADS // NSA_KV_GROUPS, HEAD_DIM
    pos = jnp.arange(T)
    nqb = T // Q_BLOCK
    n_slc = T // NSA_SEL_LEN
    n_top = min(NSA_SEL_TOPK, n_slc)
    scale = dk ** -0.5
    qh = _rope(_rmsnorm(q.reshape(B, T, NSA_HEADS, dk), q_g), pos)
    k_c, v_c, k_s, v_s, k_w, v_w = [a.reshape(B, T, G, dk) for a in jnp.split(kv, 6, axis=-1)]
    k_c = _rmsnorm(_compress(_rope(k_c, pos), pe_k, w1_k, w2_k), k_g[0])
    v_c = _compress(v_c, pe_v, w1_v, w2_v)
    n_cmp = k_c.shape[1]
    k_s = _rope(_rmsnorm(k_s, k_g[1]), pos)
    k_w = _rope(_rmsnorm(k_w, k_g[2]), pos)
    ov = _overlap(n_cmp, n_slc)
    ks_blk = k_s.reshape(B, n_slc, NSA_SEL_LEN, G, dk).transpose(0, 3, 1, 2, 4)
    vs_blk = v_s.reshape(B, n_slc, NSA_SEL_LEN, G, dk).transpose(0, 3, 1, 2, 4)
    kw_pad = jnp.pad(k_w, ((0, 0), (NSA_WINDOW, 0), (0, 0), (0, 0)))
    vw_pad = jnp.pad(v_w, ((0, 0), (NSA_WINDOW, 0), (0, 0), (0, 0)))
    gather = jax.vmap(jax.vmap(lambda blocks, ix: blocks[ix]))
    cmp_end = jnp.arange(n_cmp) * NSA_CMP_STRIDE + NSA_CMP_LEN - 1
    jblk = jnp.arange(n_slc)
    qblocks = qh.reshape(B, nqb, Q_BLOCK, G, R, dk).transpose(1, 0, 2, 3, 4, 5)

    def block(args):
        qb, i = args
        t = i * Q_BLOCK + jnp.arange(Q_BLOCK)
        s = jnp.einsum('bqgrd,bngd->bgrqn', qb, k_c) * scale
        p_c = _masked_softmax(s, cmp_end[None, :] <= t[:, None])
        o_c = jnp.einsum('bgrqn,bngd->bqgrd', p_c.astype(v_c.dtype), v_c)
        imp = jnp.einsum('bgrqn,nj->bgqj', p_c, ov)
        cur = t // NSA_SEL_LEN
        forced = (jblk[None] == 0) | (jblk[None] == cur[:, None]) | (jblk[None] == cur[:, None] - 1)
        score = jnp.where(forced, FORCE_SCORE, imp)
        score = jnp.where(jblk[None] <= cur[:, None], score, -jnp.inf)
        _, idx = lax.top_k(score, n_top)
        sel_ok = idx <= cur[None, None, :, None]
        kg = gather(ks_blk, idx)
        vg = gather(vs_blk, idx)
        s = jnp.einsum('bqgrd,bgqnkd->bgrqnk', qb, kg) * scale
        kpos = idx[..., None] * NSA_SEL_LEN + jnp.arange(NSA_SEL_LEN)
        smask = sel_ok[..., None] & (kpos <= t[None, None, :, None, None])
        m_sel = n_top * NSA_SEL_LEN
        p_s = _masked_softmax(s.reshape(B, G, R, Q_BLOCK, m_sel),
                              smask[:, :, None].reshape(B, G, 1, Q_BLOCK, m_sel))
        o_s = jnp.einsum('bgrqm,bgqmd->bqgrd', p_s.astype(vg.dtype), vg.reshape(B, G, Q_BLOCK, m_sel, dk))
        kw = lax.dynamic_slice_in_dim(kw_pad, i * Q_BLOCK, Q_BLOCK + NSA_WINDOW, axis=1)
        vw = lax.dynamic_slice_in_dim(vw_pad, i * Q_BLOCK, Q_BLOCK + NSA_WINDOW, axis=1)
        spos = i * Q_BLOCK - NSA_WINDOW + jnp.arange(Q_BLOCK + NSA_WINDOW)
        diff = t[:, None] - spos[None, :]
        wmask = (diff >= 0) & (diff < NSA_WINDOW) & (spos[None, :] >= 0)
        s = jnp.einsum('bqgrd,bkgd->bgrqk', qb, kw) * scale
        p_w = _masked_softmax(s, wmask)
        o_w = jnp.einsum('bgrqk,bkgd->bqgrd', p_w.astype(vw.dtype), vw)
        return o_c, o_s, o_w

    o_c, o_s, o_w = lax.map(block, (qblocks, jnp.arange(nqb)))
    to_bthd = lambda o: o.transpose(1, 0, 2, 3, 4, 5).reshape(B, T, NSA_HEADS, dk)
    g = jax.nn.sigmoid(gate.astype(jnp.float32)).reshape(B, T, NSA_HEADS, 3, 1).astype(q.dtype)
    o = g[..., 0, :] * to_bthd(o_c) + g[..., 1, :] * to_bthd(o_s) + g[..., 2, :] * to_bthd(o_w)
    return o.reshape(B, T, NSA_WIDTH)


def _fox(qkv, f_logit, q_g, k_g, f_bias):
    B, T, _ = qkv.shape
    H, dk = FOX_HEADS, HEAD_DIM
    nqb = T // Q_BLOCK
    scale = dk ** -0.5
    q, k, v = [a.reshape(B, T, H, dk) for a in jnp.split(qkv, 3, axis=-1)]
    q = _rmsnorm(q, q_g)
    k = _rmsnorm(k, k_g)
    log_f = jax.nn.log_sigmoid(f_logit.astype(jnp.float32) + f_bias.astype(jnp.float32))
    c = jnp.cumsum(log_f, axis=1).transpose(0, 2, 1)
    qblocks = q.reshape(B, nqb, Q_BLOCK, H, dk).transpose(1, 0, 2, 3, 4)
    cblocks = c.reshape(B, H, nqb, Q_BLOCK).transpose(2, 0, 1, 3)
    kpos = jnp.arange(T)

    def block(args):
        qb, cq, i = args
        t = i * Q_BLOCK + jnp.arange(Q_BLOCK)
        s = jnp.einsum('bqhd,bkhd->bhqk', qb, k).astype(jnp.float32) * scale
        s = s + cq[..., None] - c[:, :, None, :]
        p = _masked_softmax(s, kpos[None, :] <= t[:, None])
        return jnp.einsum('bhqk,bkhd->bqhd', p.astype(v.dtype), v)

    o = lax.map(block, (qblocks, cblocks, jnp.arange(nqb)))
    return o.transpose(1, 0, 2, 3, 4).reshape(B, T, FOX_WIDTH)


def _memory_xattn(q, mem_kv, q_g, k_g):
    B, T, _ = q.shape
    M = mem_kv.shape[1]
    q = _rmsnorm(q.reshape(B, T, MEM_HEADS, MEM_HEAD_DIM), q_g)
    k, v = [a.reshape(B, M, MEM_HEADS, MEM_HEAD_DIM) for a in jnp.split(mem_kv, 2, axis=-1)]
    k = _rmsnorm(k, k_g)
    s = jnp.einsum('bthd,bmhd->bhtm', q, k).astype(jnp.float32) * (MEM_HEAD_DIM ** -0.5)
    p = jax.nn.softmax(s, axis=-1)
    return jnp.einsum('bhtm,bmhd->bthd', p.astype(v.dtype), v).reshape(B, T, MEM_WIDTH)


def setup_inputs(seed: int = 0) -> dict:
    key = jax.random.key(seed)
    ks = jax.random.split(key, 24)
    f32 = jnp.float32
    L = DEPTH

    def nrm(k, shape, scale):
        return jax.random.normal(k, shape, f32) * scale

    def gain(k, shape):
        return 1.0 + 0.05 * jax.random.normal(k, shape, f32)

    cmp_in = NSA_CMP_LEN * HEAD_DIM
    return {
        'x': nrm(ks[0], (BATCH, SEQ, D_MODEL), 1.0),
        'mem': nrm(ks[1], (BATCH, MEM_LEN, D_MODEL), 1.0),
        'norm_g': gain(ks[2], (L, D_MODEL)),
        'mem_norm_g': gain(ks[3], (L, D_MODEL)),
        'w_in': nrm(ks[4], (L, D_MODEL, IN_WIDTH), D_MODEL ** -0.5),
        'nsa_q_norm': gain(ks[5], (L, HEAD_DIM)),
        'nsa_k_norm': gain(ks[6], (L, 3, HEAD_DIM)),
        'cmp_pe_k': nrm(ks[7], (L, NSA_CMP_LEN, HEAD_DIM), 0.1),
        'cmp_w1_k': nrm(ks[8], (L, cmp_in, NSA_CMP_HIDDEN), cmp_in ** -0.5),
        'cmp_w2_k': nrm(ks[9], (L, NSA_CMP_HIDDEN, HEAD_DIM), NSA_CMP_HIDDEN ** -0.5),
        'cmp_pe_v': nrm(ks[10], (L, NSA_CMP_LEN, HEAD_DIM), 0.1),
        'cmp_w1_v': nrm(ks[11], (L, cmp_in, NSA_CMP_HIDDEN), cmp_in ** -0.5),
        'cmp_w2_v': nrm(ks[12], (L, NSA_CMP_HIDDEN, HEAD_DIM), NSA_CMP_HIDDEN ** -0.5),
        'fox_q_norm': gain(ks[13], (L, HEAD_DIM)),
        'fox_k_norm': gain(ks[14], (L, HEAD_DIM)),
        'fox_f_bias': jax.random.uniform(ks[15], (L, FOX_HEADS), f32, 1.0, 4.0),
        'mem_q_norm': gain(ks[16], (L, MEM_HEAD_DIM)),
        'mem_k_norm': gain(ks[17], (L, MEM_HEAD_DIM)),
        'w_mem_kv': nrm(ks[18], (L, D_MODEL, 2 * MEM_WIDTH), D_MODEL ** -0.5),
        'w_branch_a': nrm(ks[19], (L, NSA_WIDTH, D_MODEL), NSA_WIDTH ** -0.5),
        'w_branch_b': nrm(ks[20], (L, FOX_WIDTH, D_MODEL), FOX_WIDTH ** -0.5),
        'w_branch_m': nrm(ks[21], (L, MEM_WIDTH, D_MODEL), MEM_WIDTH ** -0.5),
        'w_out': nrm(ks[22], (L, D_MODEL, D_MODEL), D_MODEL ** -0.5),
    }


def reference(x, mem, norm_g, mem_norm_g, w_in, nsa_q_norm, nsa_k_norm,
              cmp_pe_k, cmp_w1_k, cmp_w2_k, cmp_pe_v, cmp_w1_v, cmp_w2_v,
              fox_q_norm, fox_k_norm, fox_f_bias, mem_q_norm, mem_k_norm, w_mem_kv,
              w_branch_a, w_branch_b, w_branch_m, w_out):
    B, T, D = x.shape
    for l in range(DEPTH):
        h = _rmsnorm(x, norm_g[l])
        z = h @ w_in[l]
        (nsa_q, nsa_kv, nsa_gate, nsa_silu, fox_qkv, fox_f, fox_silu,
         mem_q, mem_silu, merge_g) = _split(z, IN_SPLITS)
        o_a = _nsa(nsa_q, nsa_kv, nsa_gate, cmp_pe_k[l], cmp_w1_k[l], cmp_w2_k[l],
                   cmp_pe_v[l], cmp_w1_v[l], cmp_w2_v[l], nsa_q_norm[l], nsa_k_norm[l])
        o_a = o_a * jax.nn.silu(nsa_silu)
        o_b = _fox(fox_qkv, fox_f, fox_q_norm[l], fox_k_norm[l], fox_f_bias[l]) * jax.nn.silu(fox_silu)
        mem_kv = _rmsnorm(mem, mem_norm_g[l]) @ w_mem_kv[l]
        o_m = _memory_xattn(mem_q, mem_kv, mem_q_norm[l], mem_k_norm[l]) * jax.nn.silu(mem_silu)
        g = jax.nn.sigmoid(merge_g.astype(jnp.float32)).reshape(B, T, N_BRANCH, D).astype(x.dtype)
        u = (g[:, :, 0] * (o_a @ w_branch_a[l])
             + g[:, :, 1] * (o_b @ w_branch_b[l])
             + g[:, :, 2] * (o_m @ w_branch_m[l]))
        x = x + u @ w_out[l]
    return x
```

```python
import functools

import numpy as np
import jax
import jax.numpy as jnp
from jax import lax
from jax.experimental import pallas as pl
from jax.experimental.pallas import tpu as pltpu

F32 = jnp.float32
BF16 = jnp.bfloat16

D_MODEL = 1024
HEAD_DIM = 64
ROPE_THETA = 10000.0
EPS = 1e-6
NSA_HEADS = 8
NSA_KV_GROUPS = 2
NSA_GROUP_HEADS = NSA_HEADS // NSA_KV_GROUPS
NSA_CMP_LEN = 32
NSA_CMP_STRIDE = 16
NSA_CMP_HIDDEN = 128
NSA_SEL_LEN = 64
NSA_SEL_TOPK = 16
NSA_WINDOW = 512
FORCE_SCORE = 1e9
NSA_WIDTH = NSA_HEADS * HEAD_DIM
FOX_HEADS = 8
FOX_WIDTH = FOX_HEADS * HEAD_DIM
MEM_HEADS = 4
MEM_HEAD_DIM = 128
MEM_WIDTH = MEM_HEADS * MEM_HEAD_DIM

LANES = 128
VMEM_LIMIT = 48 * 1024 * 1024
NEG = -1e30
SEL_MASK = 32768.0
SEL_SHIFT = 6
NSLC_PAD = LANES

C_NSA_Q = 0
C_NSA_SILU = 512
C_FOX_Q = 1024
C_FOX_K = 1536
C_FOX_V = 2048
C_FOX_SILU = 2560
C_MEM_Q = 3072
C_MEM_SILU = 3584
C_MERGE = 4096
C_NSA_KV = 7168
C_SMALL = 7936
Z_WIDTH = 8064

TQ = 512
CMP_TQ = 256


def _cparams(sem):
    return pltpu.CompilerParams(dimension_semantics=sem, vmem_limit_bytes=VMEM_LIMIT)


def _norm_matmul_kernel(x_ref, g_ref, w_ref, o_ref):
    x = x_ref[...]
    ms = jnp.mean(x * x, axis=-1, keepdims=True)
    h = (x * lax.rsqrt(ms + EPS) * g_ref[...]).astype(BF16)
    o_ref[...] = jnp.dot(h, w_ref[...], preferred_element_type=F32)


def _norm_matmul(x2d, g, w, tm, tn):
    m, k = x2d.shape
    n = w.shape[1]
    return pl.pallas_call(
        _norm_matmul_kernel,
        grid=(n // tn, m // tm),
        in_specs=[pl.BlockSpec((tm, k), lambda j, i: (i, 0)),
                  pl.BlockSpec((1, k), lambda j, i: (0, 0)),
                  pl.BlockSpec((k, tn), lambda j, i: (0, j))],
        out_specs=pl.BlockSpec((tm, tn), lambda j, i: (i, j)),
        out_shape=jax.ShapeDtypeStruct((m, n), F32),
        compiler_params=_cparams(("parallel", "parallel")),
        name="norm_matmul",
    )(x2d, g.reshape(1, k), w)


def _matmul_kernel(x_ref, w_ref, o_ref):
    o_ref[...] = jnp.dot(x_ref[...], w_ref[...], preferred_element_type=F32)


def _matmul(x, w, tm):
    m, k = x.shape
    n = w.shape[1]
    return pl.pallas_call(
        _matmul_kernel,
        grid=(m // tm,),
        in_specs=[pl.BlockSpec((tm, k), lambda i: (i, 0)),
                  pl.BlockSpec((k, n), lambda i: (0, 0))],
        out_specs=pl.BlockSpec((tm, n), lambda i: (i, 0)),
        out_shape=jax.ShapeDtypeStruct((m, n), F32),
        compiler_params=_cparams(("parallel",)),
        name="cmp_matmul",
    )(x, w)


def _gate_column(gate_blk, lane, idx):
    col = jnp.sum(jnp.where(lane == idx, gate_blk, 0.0), axis=1, keepdims=True)
    return jax.nn.sigmoid(col)


def _flash_kernel(qi_tab, ki_tab, fl_tab, *refs, nheads, shared_kv, window, has_delta,
                  has_u, gate_branch, nblk):
    refs = list(refs)
    base_ref = refs.pop(0) if has_delta else None
    q_ref = refs.pop(0)
    u_ref = refs.pop(0) if has_u else None
    k_ref = refs.pop(0)
    v_ref = refs.pop(0)
    gate_ref = refs.pop(0) if gate_branch is not None else None
    o_ref = refs.pop(0)
    m_scr = refs.pop(0)
    l_scr = refs.pop(0)
    acc_scr = refs.pop(0)
    qcat_scr = refs.pop(0) if has_u else None

    b = pl.program_id(0)
    g = pl.program_id(1)
    step = pl.program_id(2)
    qi = qi_tab[step]
    ki = ki_tab[step]
    fl = fl_tab[step]
    tq = q_ref.shape[2]
    tk = k_ref.shape[2]

    @pl.when((fl & 1) == 1)
    def _init():
        m_scr[...] = jnp.full(m_scr.shape, NEG, F32)
        l_scr[...] = jnp.zeros(l_scr.shape, F32)
        acc_scr[...] = jnp.zeros(acc_scr.shape, F32)
        if has_u:
            for r in range(nheads):
                qcat_scr[r, :, :LANES] = q_ref[0, r]
                qcat_scr[r, :, LANES:] = u_ref[0, 0]

    row = lax.broadcasted_iota(jnp.int32, (tq, tk), 0)
    col = lax.broadcasted_iota(jnp.int32, (tq, tk), 1)
    dist = (qi - ki) * tq + row - col
    valid = dist >= 0
    if window is not None:
        valid = valid & (dist < window)
    lane = lax.broadcasted_iota(jnp.int32, (tq, LANES), 1)
    lo = lane < HEAD_DIM

    for p in range(nheads // 2):
        pvs, alphas = [], []
        for e in range(2):
            hh = 2 * p + e
            q = qcat_scr[hh] if has_u else q_ref[0, hh]
            k = k_ref[0, 0] if shared_kv else k_ref[0, hh]
            v = v_ref[0, 0] if shared_kv else v_ref[0]
            s = lax.dot_general(q, k, (((1,), (1,)), ((), ())), preferred_element_type=F32)
            s = jnp.where(valid, s, NEG)
            m_prev = m_scr[hh]
            m_cur = jnp.max(s, axis=1, keepdims=True)
            if has_delta:
                hrow = (b * FOX_HEADS + g * nheads + hh) * nblk
                delta = base_ref[hrow + qi] - base_ref[hrow + ki]
                m_cur = m_cur + delta
            m_new = jnp.maximum(m_prev, m_cur)
            alpha = jnp.exp(m_prev - m_new)
            shift = m_new[:, :1]
            if has_delta:
                shift = shift - delta
            pm = jnp.exp(s - shift)
            l_scr[hh] = alpha * l_scr[hh] + jnp.sum(pm, axis=1, keepdims=True)
            m_scr[hh] = m_new
            pvs.append(jnp.dot(pm.astype(BF16), v, preferred_element_type=F32))
            alphas.append(alpha)
        acc_scr[p] = (acc_scr[p] * jnp.where(lo, alphas[0], alphas[1])
                      + jnp.where(lo, pvs[0], pvs[1]))

    @pl.when((fl & 2) == 2)
    def _finish():
        for p in range(nheads // 2):
            inv = jnp.where(lo, 1.0 / l_scr[2 * p], 1.0 / l_scr[2 * p + 1])
            if gate_branch is not None:
                gate_blk = gate_ref[0]
                h0 = g * nheads + 2 * p
                ga = _gate_column(gate_blk, lane, 3 * h0 + gate_branch)
                gb = _gate_column(gate_blk, lane, 3 * (h0 + 1) + gate_branch)
                inv = inv * jnp.where(lo, ga, gb)
            o_ref[0, :, p * LANES:(p + 1) * LANES] = acc_scr[p] * inv


def _tile_tables(nq, window_tiles):
    qi, ki, fl = [], [], []
    for i in range(nq):
        lo = 0 if window_tiles is None else max(0, i - window_tiles)
        for j in range(lo, i + 1):
            qi.append(i)
            ki.append(j)
            fl.append((1 if j == lo else 0) | (2 if j == i else 0))
    return (jnp.asarray(qi, jnp.int32), jnp.asarray(ki, jnp.int32), jnp.asarray(fl, jnp.int32))


def _flash(q, k, v, *, batch, seq, nheads, shared_kv, window=None, base=None, u=None,
           z=None, gate_branch=None):
    nq = seq // TQ
    ngroups = 8 // nheads
    da = k.shape[-1]
    tabs = _tile_tables(nq, None if window is None else window // TQ)
    nsteps = int(tabs[0].shape[0])
    has_delta = base is not None
    has_u = u is not None

    in_specs, args = [], []
    if has_delta:
        in_specs.append(pl.BlockSpec(memory_space=pltpu.SMEM))
        args.append(base)
    in_specs.append(pl.BlockSpec((1, nheads, TQ, LANES), lambda b, g, s, qt, kt, ft: (b, g, qt[s], 0)))
    args.append(q)
    if has_u:
        in_specs.append(pl.BlockSpec((1, 1, TQ, LANES), lambda b, g, s, qt, kt, ft: (b, g, qt[s], 0)))
        args.append(u)
    if shared_kv:
        in_specs.append(pl.BlockSpec((1, 1, TQ, da), lambda b, g, s, qt, kt, ft: (b, g, kt[s], 0)))
        in_specs.append(pl.BlockSpec((1, 1, TQ, LANES), lambda b, g, s, qt, kt, ft: (b, g, kt[s], 0)))
    else:
        in_specs.append(pl.BlockSpec((1, nheads, TQ, da), lambda b, g, s, qt, kt, ft: (b, g, kt[s], 0)))
        in_specs.append(pl.BlockSpec((1, TQ, LANES), lambda b, g, s, qt, kt, ft: (b, kt[s], g)))
    args += [k, v]
    if gate_branch is not None:
        in_specs.append(pl.BlockSpec((1, TQ, LANES),
                                     lambda b, g, s, qt, kt, ft: (b, qt[s], C_SMALL // LANES)))
        args.append(z)

    out_w = LANES * (nheads // 2)
    scratch = [pltpu.VMEM((nheads, TQ, LANES), F32),
               pltpu.VMEM((nheads, TQ, LANES), F32),
               pltpu.VMEM((nheads // 2, TQ, LANES), F32)]
    if has_u:
        scratch.append(pltpu.VMEM((nheads, TQ, 2 * LANES), BF16))

    kern = functools.partial(_flash_kernel, nheads=nheads, shared_kv=shared_kv, window=window,
                             has_delta=has_delta, has_u=has_u, gate_branch=gate_branch, nblk=nq)
    return pl.pallas_call(
        kern,
        grid_spec=pltpu.PrefetchScalarGridSpec(
            num_scalar_prefetch=3,
            grid=(batch, ngroups, nsteps),
            in_specs=in_specs,
            out_specs=pl.BlockSpec((1, TQ, out_w), lambda b, g, s, qt, kt, ft: (b, qt[s], g)),
            scratch_shapes=scratch),
        out_shape=jax.ShapeDtypeStruct((batch, seq, 4 * LANES), F32),
        compiler_params=_cparams(("parallel", "parallel", "arbitrary")),
        name="flash_" + ("fox" if has_delta else ("sel" if has_u else "win")),
    )(*tabs, *args)


def _cmp_kernel(q_ref, kc_ref, vc_ref, ov_ref, gate_ref, o_ref, u_ref):
    g = pl.program_id(1)
    qi = pl.program_id(2)
    tq = q_ref.shape[2]
    ncmp = kc_ref.shape[2]
    nslc = ov_ref.shape[1]

    t = qi * tq + lax.broadcasted_iota(jnp.int32, (tq, ncmp), 0)
    n = lax.broadcasted_iota(jnp.int32, (tq, ncmp), 1)
    valid = (n * NSA_CMP_STRIDE + (NSA_CMP_LEN - 1)) <= t
    kc = kc_ref[0, 0]
    vc = vc_ref[0, 0]
    ov = ov_ref[...]
    lane = lax.broadcasted_iota(jnp.int32, (tq, LANES), 1)
    lo = lane < HEAD_DIM
    gate_blk = gate_ref[0]

    imp = jnp.zeros((tq, nslc), F32)
    outs = []
    for r in range(NSA_GROUP_HEADS):
        s = lax.dot_general(q_ref[0, r], kc, (((1,), (1,)), ((), ())), preferred_element_type=F32)
        s = jnp.where(valid, s, NEG)
        m = jnp.max(s, axis=1, keepdims=True)
        e = jnp.where(valid, jnp.exp(s - m), 0.0)
        d = jnp.sum(e, axis=1, keepdims=True)
        pb = (e / jnp.where(d > 0.0, d, 1.0)).astype(BF16)
        o = jnp.dot(pb, vc, preferred_element_type=F32)
        gcol = _gate_column(gate_blk, lane, 3 * (g * NSA_GROUP_HEADS + r))
        outs.append(o * gcol)
        imp = imp + jnp.dot(pb, ov, preferred_element_type=F32)
    for p in range(NSA_GROUP_HEADS // 2):
        o_ref[0, :, p * LANES:(p + 1) * LANES] = jnp.where(lo, outs[2 * p], outs[2 * p + 1])

    tpos = qi * tq + lax.broadcasted_iota(jnp.int32, (tq, nslc), 0)
    jblk = lax.broadcasted_iota(jnp.int32, (tq, nslc), 1)
    cur = jnp.right_shift(tpos, SEL_SHIFT)
    forced = (jblk == 0) | (jblk == cur) | (jblk == cur - 1)
    score = jnp.where(forced, FORCE_SCORE, imp)
    score = jnp.where(jblk <= cur, score, -jnp.inf)

    sc = score.T
    ji = lax.broadcasted_iota(jnp.int32, (nslc, tq), 0)
    ok_t = ji <= jnp.right_shift(qi * tq + lax.broadcasted_iota(jnp.int32, (nslc, tq), 1), SEL_SHIFT)
    jt = ji.astype(F32)
    for _ in range(NSA_SEL_TOPK):
        mx = jnp.max(sc, axis=0, keepdims=True)
        first = jnp.min(jnp.where(sc == mx, jt, float(nslc)), axis=0, keepdims=True)
        sc = jnp.where(jt == first, -jnp.inf, sc)
    unsel = jnp.where((sc == -jnp.inf) & ok_t, 0.0, 1.0)
    u_ref[0, 0] = unsel.T.astype(BF16)


def _cmp_select(qn, kc, vc, ov, z, *, batch, seq):
    ncmp = kc.shape[2]
    nslc = ov.shape[1]
    return pl.pallas_call(
        _cmp_kernel,
        grid=(batch, NSA_KV_GROUPS, seq // CMP_TQ),
        in_specs=[pl.BlockSpec((1, NSA_GROUP_HEADS, CMP_TQ, LANES), lambda b, g, i: (b, g, i, 0)),
                  pl.BlockSpec((1, 1, ncmp, LANES), lambda b, g, i: (b, g, 0, 0)),
                  pl.BlockSpec((1, 1, ncmp, LANES), lambda b, g, i: (b, g, 0, 0)),
                  pl.BlockSpec((ncmp, nslc), lambda b, g, i: (0, 0)),
                  pl.BlockSpec((1, CMP_TQ, LANES), lambda b, g, i: (b, i, C_SMALL // LANES))],
        out_specs=[pl.BlockSpec((1, CMP_TQ, 2 * LANES), lambda b, g, i: (b, i, g)),
                   pl.BlockSpec((1, 1, CMP_TQ, nslc), lambda b, g, i: (b, g, i, 0))],
        out_shape=[jax.ShapeDtypeStruct((batch, seq, NSA_WIDTH), F32),
                   jax.ShapeDtypeStruct((batch, NSA_KV_GROUPS, seq, nslc), BF16)],
        compiler_params=_cparams(("parallel", "parallel", "parallel")),
        name="cmp_select",
    )(qn, kc, vc, ov, z)


def _head_rms(x, g):
    return x * lax.rsqrt(jnp.mean(x * x, axis=-1, keepdims=True) + EPS) * g


def _mem_kernel(zq_ref, kv_ref, qg_ref, kg_ref, o_ref, kn_scr, vb_scr):
    @pl.when(pl.program_id(1) == 0)
    def _prep():
        for h in range(MEM_HEADS):
            kh = kv_ref[0, :, h * MEM_HEAD_DIM:(h + 1) * MEM_HEAD_DIM]
            kn_scr[h] = _head_rms(kh, kg_ref[...]).astype(BF16)
            vb_scr[h] = kv_ref[0, :, MEM_WIDTH + h * MEM_HEAD_DIM:
                               MEM_WIDTH + (h + 1) * MEM_HEAD_DIM].astype(BF16)

    for h in range(MEM_HEADS):
        sl = slice(h * MEM_HEAD_DIM, (h + 1) * MEM_HEAD_DIM)
        qh = (_head_rms(zq_ref[0, :, sl], qg_ref[...]) * (MEM_HEAD_DIM ** -0.5)).astype(BF16)
        s = lax.dot_general(qh, kn_scr[h], (((1,), (1,)), ((), ())), preferred_element_type=F32)
        m = jnp.max(s, axis=1, keepdims=True)
        e = jnp.exp(s - m)
        p = (e / jnp.sum(e, axis=1, keepdims=True)).astype(BF16)
        o_ref[0, :, sl] = jnp.dot(p, vb_scr[h], preferred_element_type=F32)


def _mem_attention(z, mem_kv, q_g, k_g, *, batch, seq):
    mlen = mem_kv.shape[1]
    tq = TQ
    return pl.pallas_call(
        _mem_kernel,
        grid=(batch, seq // tq),
        in_specs=[pl.BlockSpec((1, tq, MEM_WIDTH), lambda b, i: (b, i, C_MEM_Q // MEM_WIDTH)),
                  pl.BlockSpec((1, mlen, 2 * MEM_WIDTH), lambda b, i: (b, 0, 0)),
                  pl.BlockSpec((1, MEM_HEAD_DIM), lambda b, i: (0, 0)),
                  pl.BlockSpec((1, MEM_HEAD_DIM), lambda b, i: (0, 0))],
        out_specs=pl.BlockSpec((1, tq, MEM_WIDTH), lambda b, i: (b, i, 0)),
        out_shape=jax.ShapeDtypeStruct((batch, seq, MEM_WIDTH), F32),
        scratch_shapes=[pltpu.VMEM((MEM_HEADS, mlen, MEM_HEAD_DIM), BF16),
                        pltpu.VMEM((MEM_HEADS, mlen, MEM_HEAD_DIM), BF16)],
        compiler_params=_cparams(("parallel", "arbitrary")),
        name="mem_attention",
    )(z, mem_kv, q_g.reshape(1, MEM_HEAD_DIM), k_g.reshape(1, MEM_HEAD_DIM))


def _out_kernel(x_ref, oc_ref, os_ref, ow_ref, ob_ref, om_ref, sa_ref, sb_ref, sm_ref,
                g0_ref, g1_ref, g2_ref, wa_ref, wb_ref, wm_ref, wo_ref, y_ref):
    oa = (oc_ref[...] + os_ref[...] + ow_ref[...]) * jax.nn.silu(sa_ref[...])
    ob = ob_ref[...] * jax.nn.silu(sb_ref[...])
    om = om_ref[...] * jax.nn.silu(sm_ref[...])
    u = (jax.nn.sigmoid(g0_ref[...]) * jnp.dot(oa.astype(BF16), wa_ref[...], preferred_element_type=F32)
         + jax.nn.sigmoid(g1_ref[...]) * jnp.dot(ob.astype(BF16), wb_ref[...], preferred_element_type=F32)
         + jax.nn.sigmoid(g2_ref[...]) * jnp.dot(om.astype(BF16), wm_ref[...], preferred_element_type=F32))
    y_ref[...] = x_ref[...] + jnp.dot(u.astype(BF16), wo_ref[...], preferred_element_type=F32)


def _out_proj(x2d, o_c, o_s, o_w, o_b, o_m, z2d, wa, wb, wm, wo):
    m = x2d.shape[0]
    tm = 256
    w512 = 512
    row512 = lambda c: pl.BlockSpec((tm, w512), lambda i, c=c: (i, c))
    row1024 = lambda c: pl.BlockSpec((tm, D_MODEL), lambda i, c=c: (i, c))
    full = lambda shape: pl.BlockSpec(shape, lambda i: (0, 0))
    return pl.pallas_call(
        _out_kernel,
        grid=(m // tm,),
        in_specs=[row1024(0), row512(0), row512(0), row512(0), row512(0), row512(0),
                  row512(C_NSA_SILU // w512), row512(C_FOX_SILU // w512), row512(C_MEM_SILU // w512),
                  row1024(C_MERGE // D_MODEL), row1024(C_MERGE // D_MODEL + 1),
                  row1024(C_MERGE // D_MODEL + 2),
                  full((w512, D_MODEL)), full((w512, D_MODEL)), full((w512, D_MODEL)),
                  full((D_MODEL, D_MODEL))],
        out_specs=row1024(0),
        out_shape=jax.ShapeDtypeStruct((m, D_MODEL), F32),
        compiler_params=_cparams(("parallel",)),
        name="out_proj",
    )(x2d, o_c, o_s, o_w, o_b, o_m, z2d, z2d, z2d, z2d, z2d, z2d, wa, wb, wm, wo)


def _permute_w_in(w_in):
    o = np.cumsum([0, NSA_WIDTH, 6 * NSA_KV_GROUPS * HEAD_DIM, 3 * NSA_HEADS, NSA_WIDTH,
                   3 * FOX_WIDTH, FOX_HEADS, FOX_WIDTH, MEM_WIDTH, MEM_WIDTH, 3 * D_MODEL])
    nsa_q, nsa_kv, nsa_gate, nsa_silu, fox_qkv, fox_f, fox_silu, mem_q, mem_silu, merge = [
        w_in[..., o[i]:o[i + 1]] for i in range(10)]
    pad = jnp.zeros(w_in.shape[:2] + (LANES - 3 * NSA_HEADS - FOX_HEADS,), w_in.dtype)
    return jnp.concatenate([nsa_q, nsa_silu, fox_qkv, fox_silu, mem_q, mem_silu, merge, nsa_kv,
                            nsa_gate, fox_f, pad], axis=-1).astype(BF16)


def _rope(x, cos, sin):
    half = HEAD_DIM // 2
    c = cos[None, :, None, :]
    s = sin[None, :, None, :]
    x1, x2 = x[..., :half], x[..., half:]
    return jnp.concatenate([x1 * c - x2 * s, x2 * c + x1 * s], axis=-1)


def _rms(x, g):
    return x * lax.rsqrt(jnp.mean(x * x, axis=-1, keepdims=True) + EPS) * g


def _pad_lanes(x, width=LANES):
    return jnp.pad(x, [(0, 0)] * (x.ndim - 1) + [(0, width - x.shape[-1])])


def _overlap(n_cmp_pad, n_slc):
    cs = np.arange(n_cmp_pad)[:, None] * NSA_CMP_STRIDE
    ss = np.arange(NSLC_PAD)[None, :] * NSA_SEL_LEN
    ov = np.clip(np.minimum(cs + NSA_CMP_LEN, ss + NSA_SEL_LEN) - np.maximum(cs, ss), 0, None)
    ov = ov / NSA_CMP_LEN
    ov[-1] = 0.0
    ov[:, n_slc:] = 0.0
    return jnp.asarray(ov, BF16)


def _compress(x, pe, w1, w2, batch, seq):
    nchunk = seq // NSA_CMP_STRIDE
    half_in = NSA_CMP_STRIDE * HEAD_DIM
    xc = x.reshape(batch, nchunk, NSA_CMP_STRIDE, NSA_KV_GROUPS, HEAD_DIM)
    rows = batch * NSA_KV_GROUPS * nchunk
    xc = xc.transpose(0, 3, 1, 2, 4).reshape(rows, half_in).astype(BF16)
    wcat = jnp.concatenate([w1[:half_in], w1[half_in:]], axis=1).astype(BF16)
    hcat = _matmul(xc, wcat, min(512, rows)).reshape(batch, NSA_KV_GROUPS, nchunk, 2 * NSA_CMP_HIDDEN)
    pe_bias = jnp.dot(pe.reshape(1, -1), w1, precision=lax.Precision.HIGHEST)
    nxt = jnp.concatenate([hcat[:, :, 1:, NSA_CMP_HIDDEN:],
                           jnp.zeros_like(hcat[:, :, :1, NSA_CMP_HIDDEN:])], axis=2)
    hid = jax.nn.silu(hcat[..., :NSA_CMP_HIDDEN] + nxt + pe_bias).astype(BF16)
    w2p = _pad_lanes(w2).astype(BF16)
    out = _matmul(hid.reshape(rows, NSA_CMP_HIDDEN), w2p, min(512, rows))[:, :HEAD_DIM]
    return out.reshape(batch, NSA_KV_GROUPS, nchunk, HEAD_DIM)


def _split3(a):
    hi = a.astype(BF16)
    r = a - hi.astype(F32)
    mid = r.astype(BF16)
    lo = (r - mid.astype(F32)).astype(BF16)
    return hi, mid, lo


def _layer(x, mem, p, cos, sin, ov, sel_onehot):
    batch, seq, _ = x.shape
    m_rows = batch * seq
    z2d = _norm_matmul(x.reshape(m_rows, D_MODEL), p["norm_g"], p["w_in"], 512, Z_WIDTH // 3)
    z = z2d.reshape(batch, seq, Z_WIDTH)

    qh = _rope(_rms(z[..., C_NSA_Q:C_NSA_Q + NSA_WIDTH].reshape(batch, seq, NSA_HEADS, HEAD_DIM),
                    p["nsa_q_norm"]), cos, sin) * (HEAD_DIM ** -0.5)
    qn = _pad_lanes(qh.transpose(0, 2, 1, 3)).astype(BF16)
    kv = z[..., C_NSA_KV:C_NSA_KV + 6 * NSA_KV_GROUPS * HEAD_DIM]
    k_c, v_c, k_s, v_s, k_w, v_w = [a.reshape(batch, seq, NSA_KV_GROUPS, HEAD_DIM)
                                    for a in jnp.split(kv, 6, axis=-1)]
    kc = _rms(_compress(_rope(k_c, cos, sin), p["cmp_pe_k"], p["cmp_w1_k"], p["cmp_w2_k"], batch, seq),
              p["nsa_k_norm"][0])
    vc = _compress(v_c, p["cmp_pe_v"], p["cmp_w1_v"], p["cmp_w2_v"], batch, seq)
    kc = _pad_lanes(kc).astype(BF16)
    vc = jnp.concatenate([vc, vc], axis=-1).astype(BF16)
    ks = _rope(_rms(k_s, p["nsa_k_norm"][1]), cos, sin).transpose(0, 2, 1, 3)
    ks = jnp.concatenate([_pad_lanes(ks).astype(BF16),
                          jnp.broadcast_to(sel_onehot, (batch, NSA_KV_GROUPS) + sel_onehot.shape)], axis=-1)
    kw = _pad_lanes(_rope(_rms(k_w, p["nsa_k_norm"][2]), cos, sin).transpose(0, 2, 1, 3)).astype(BF16)
    dup = lambda a: jnp.concatenate([a, a], axis=-1).transpose(0, 2, 1, 3).astype(BF16)
    vs, vw = dup(v_s), dup(v_w)

    o_c, unsel = _cmp_select(qn, kc, vc, ov, z, batch=batch, seq=seq)
    o_s = _flash(qn, ks, vs, batch=batch, seq=seq, nheads=NSA_GROUP_HEADS, shared_kv=True,
                 u=unsel, z=z, gate_branch=1)
    o_w = _flash(qn, kw, vw, batch=batch, seq=seq, nheads=NSA_GROUP_HEADS, shared_kv=True,
                 window=NSA_WINDOW, z=z, gate_branch=2)

    fq = _rms(z[..., C_FOX_Q:C_FOX_Q + FOX_WIDTH].reshape(batch, seq, FOX_HEADS, HEAD_DIM),
              p["fox_q_norm"]) * (HEAD_DIM ** -0.5)
    fk = _rms(z[..., C_FOX_K:C_FOX_K + FOX_WIDTH].reshape(batch, seq, FOX_HEADS, HEAD_DIM),
              p["fox_k_norm"])
    fv = z[..., C_FOX_V:C_FOX_V + FOX_WIDTH].astype(BF16)
    f_logit = z[..., C_SMALL + 3 * NSA_HEADS:C_SMALL + 3 * NSA_HEADS + FOX_HEADS]
    log_f = jax.nn.log_sigmoid(f_logit + p["fox_f_bias"])
    nblk = seq // TQ
    a = jnp.cumsum(log_f.reshape(batch, nblk, TQ, FOX_HEADS), axis=2)
    tot = a[:, :, -1, :]
    base = jnp.cumsum(tot, axis=1) - tot
    base = base.transpose(0, 2, 1).reshape(-1)
    a = a.reshape(batch, seq, FOX_HEADS)
    ahi, amid, alo = _split3(a)
    one = jnp.ones_like(ahi)
    q_aug = jnp.stack([ahi, amid, alo, one, one, one], axis=-1)
    k_aug = jnp.stack([one, one, one, -ahi, -amid, -alo], axis=-1)
    fqa = _pad_lanes(jnp.concatenate([fq.astype(BF16), q_aug], axis=-1).transpose(0, 2, 1, 3))
    fka = _pad_lanes(jnp.concatenate([fk.astype(BF16), k_aug], axis=-1).transpose(0, 2, 1, 3))
    o_b = _flash(fqa, fka, fv, batch=batch, seq=seq, nheads=2, shared_kv=False, base=base)

    mlen = mem.shape[1]
    mem_kv = _norm_matmul(mem.reshape(batch * mlen, D_MODEL), p["mem_norm_g"], p["w_mem_kv"],
                          min(512, batch * mlen), 2 * MEM_WIDTH).reshape(batch, mlen, 2 * MEM_WIDTH)
    o_m = _mem_attention(z, mem_kv, p["mem_q_norm"], p["mem_k_norm"], batch=batch, seq=seq)

    y = _out_proj(x.reshape(m_rows, D_MODEL), o_c.reshape(m_rows, -1), o_s.reshape(m_rows, -1),
                  o_w.reshape(m_rows, -1), o_b.reshape(m_rows, -1), o_m.reshape(m_rows, -1), z2d,
                  p["w_branch_a"], p["w_branch_b"], p["w_branch_m"], p["w_out"])
    return y.reshape(batch, seq, D_MODEL)


def kernel(x, mem, norm_g, mem_norm_g, w_in, nsa_q_norm, nsa_k_norm, cmp_pe_k, cmp_w1_k, cmp_w2_k,
           cmp_pe_v, cmp_w1_v, cmp_w2_v, fox_q_norm, fox_k_norm, fox_f_bias, mem_q_norm, mem_k_norm,
           w_mem_kv, w_branch_a, w_branch_b, w_branch_m, w_out):
    batch, seq, _ = x.shape
    depth = w_in.shape[0]
    half = HEAD_DIM // 2
    inv_freq = ROPE_THETA ** (-jnp.arange(half, dtype=F32) / half)
    ang = jnp.arange(seq).astype(F32)[:, None] * inv_freq[None, :]
    cos, sin = jnp.cos(ang), jnp.sin(ang)
    n_slc = seq // NSA_SEL_LEN
    assert n_slc <= NSLC_PAD and seq % TQ == 0 and NSA_WINDOW == TQ
    ov = _overlap(seq // NSA_CMP_STRIDE, n_slc)
    sel_onehot = (-SEL_MASK * (jnp.arange(seq)[:, None] // NSA_SEL_LEN
                               == jnp.arange(NSLC_PAD)[None, :]).astype(F32)).astype(BF16)

    w_in_p = _permute_w_in(w_in)
    stacked = dict(norm_g=norm_g, mem_norm_g=mem_norm_g, w_in=w_in_p, nsa_q_norm=nsa_q_norm,
                   nsa_k_norm=nsa_k_norm, cmp_pe_k=cmp_pe_k, cmp_w1_k=cmp_w1_k, cmp_w2_k=cmp_w2_k,
                   cmp_pe_v=cmp_pe_v, cmp_w1_v=cmp_w1_v, cmp_w2_v=cmp_w2_v, fox_q_norm=fox_q_norm,
                   fox_k_norm=fox_k_norm, fox_f_bias=fox_f_bias, mem_q_norm=mem_q_norm,
                   mem_k_norm=mem_k_norm, w_mem_kv=w_mem_kv.astype(BF16),
                   w_branch_a=w_branch_a.astype(BF16), w_branch_b=w_branch_b.astype(BF16),
                   w_branch_m=w_branch_m.astype(BF16), w_out=w_out.astype(BF16))
    for l in range(depth):
        x = _layer(x, mem, {k: v[l] for k, v in stacked.items()}, cos, sin, ov, sel_onehot)
    return x
```

```python
import functools

import numpy as np
import jax
import jax.numpy as jnp
from jax import lax
from jax.experimental import pallas as pl
from jax.experimental.pallas import tpu as pltpu

F32 = jnp.float32
BF16 = jnp.bfloat16

D_MODEL = 1024
HEAD_DIM = 64
ROPE_THETA = 10000.0
EPS = 1e-6
NSA_HEADS = 8
NSA_KV_GROUPS = 2
NSA_GROUP_HEADS = NSA_HEADS // NSA_KV_GROUPS
NSA_CMP_LEN = 32
NSA_CMP_STRIDE = 16
NSA_CMP_HIDDEN = 128
NSA_SEL_LEN = 64
NSA_SEL_TOPK = 16
NSA_WINDOW = 512
FORCE_SCORE = 1e9
NSA_WIDTH = NSA_HEADS * HEAD_DIM
FOX_HEADS = 8
FOX_WIDTH = FOX_HEADS * HEAD_DIM
MEM_HEADS = 4
MEM_HEAD_DIM = 128
MEM_WIDTH = MEM_HEADS * MEM_HEAD_DIM

LANES = 128
VMEM_LIMIT = 48 * 1024 * 1024
NEG = -1e30
SEL_MASK = 32768.0
SEL_SHIFT = 6
NSLC_PAD = LANES
LOG2E = 1.4426950408889634
QK_SCALE = (HEAD_DIM ** -0.5) * LOG2E

C_NSA_Q = 0
C_NSA_SILU = 512
C_FOX_Q = 1024
C_FOX_K = 1536
C_FOX_V = 2048
C_FOX_SILU = 2560
C_MEM_Q = 3072
C_MEM_SILU = 3584
C_MERGE = 4096
C_NSA_KV = 7168
C_SMALL = 7936
Z_WIDTH = 8064
F_LANE = 3 * NSA_HEADS
AUG = HEAD_DIM

TQ = 512
CMP_TQ = 256


def _cparams(sem):
    return pltpu.CompilerParams(dimension_semantics=sem, vmem_limit_bytes=VMEM_LIMIT)


def _norm_matmul_kernel(x_ref, g_ref, w_ref, o_ref):
    x = x_ref[...]
    ms = jnp.mean(x * x, axis=-1, keepdims=True)
    h = (x * lax.rsqrt(ms + EPS) * g_ref[...]).astype(BF16)
    o_ref[...] = jnp.dot(h, w_ref[...], preferred_element_type=F32)


def _norm_matmul(x2d, g, w, tm, tn):
    m, k = x2d.shape
    n = w.shape[1]
    return pl.pallas_call(
        _norm_matmul_kernel,
        grid=(n // tn, m // tm),
        in_specs=[pl.BlockSpec((tm, k), lambda j, i: (i, 0)),
                  pl.BlockSpec((1, k), lambda j, i: (0, 0)),
                  pl.BlockSpec((k, tn), lambda j, i: (0, j))],
        out_specs=pl.BlockSpec((tm, tn), lambda j, i: (i, j)),
        out_shape=jax.ShapeDtypeStruct((m, n), F32),
        compiler_params=_cparams(("parallel", "parallel")),
        name="norm_matmul",
    )(x2d, g.reshape(1, k), w)


def _pair_rms(x, g2, lo):
    sq = x * x
    s_lo = jnp.sum(jnp.where(lo, sq, 0.0), axis=1, keepdims=True)
    s_hi = jnp.sum(jnp.where(lo, 0.0, sq), axis=1, keepdims=True)
    ms = jnp.where(lo, s_lo, s_hi) * (1.0 / HEAD_DIM)
    return x * lax.rsqrt(ms + EPS) * g2


def _pair_rope(y, cos4, sin_signed, first_half):
    rot = jnp.where(first_half, pltpu.roll(y, LANES - HEAD_DIM // 2, 1), pltpu.roll(y, HEAD_DIM // 2, 1))
    return y * cos4 + rot * sin_signed


def _head_a(y, lo):
    return jnp.where(lo, y, 0.0)


def _head_b(y, lo):
    return jnp.where(lo, pltpu.roll(y, HEAD_DIM, 1), 0.0)


def _split3_f32(c):
    hi = c.astype(BF16).astype(F32)
    r = c - hi
    mid = r.astype(BF16).astype(F32)
    return hi, mid, r - mid


def _prep_kernel(zq_ref, zc_ref, zs_ref, zw_ref, fq_ref, fk_ref, fv_ref, sm_ref, cos_ref, sin_ref,
                 tri_ref, gq_ref, gks_ref, gkw_ref, gfq_ref, gfk_ref, fb_ref,
                 qn_ref, kcr_ref, vcr_ref, ks_ref, vs_ref, kw_ref, vw_ref, fqa_ref, fka_ref, fvb_ref,
                 base_ref, run_scr):
    ti = pl.program_id(1)
    tt = zq_ref.shape[1]
    lane = lax.broadcasted_iota(jnp.int32, (tt, LANES), 1)
    lo = lane < HEAD_DIM
    first_half = (lane & (HEAD_DIM // 2)) == 0
    cos4 = cos_ref[...]
    sin_signed = sin_ref[...]

    for p in range(NSA_HEADS // 2):
        y = _pair_rope(_pair_rms(zq_ref[0, :, p * LANES:(p + 1) * LANES], gq_ref[...], lo),
                       cos4, sin_signed, first_half) * QK_SCALE
        qn_ref[0, 2 * p] = _head_a(y, lo).astype(BF16)
        qn_ref[0, 2 * p + 1] = _head_b(y, lo).astype(BF16)

    kcr_ref[0] = _pair_rope(zc_ref[0, :, :LANES], cos4, sin_signed, first_half).astype(BF16)
    vcr_ref[0] = zc_ref[0, :, LANES:].astype(BF16)

    sel_lane = jnp.right_shift(ti * tt + lax.broadcasted_iota(jnp.int32, (tt, LANES), 0), SEL_SHIFT)
    onehot = jnp.where(lane == sel_lane, -SEL_MASK, 0.0).astype(BF16)
    for z_ref, g_ref, k_ref, v_ref, with_onehot in ((zs_ref, gks_ref, ks_ref, vs_ref, True),
                                                    (zw_ref, gkw_ref, kw_ref, vw_ref, False)):
        y = _pair_rope(_pair_rms(z_ref[0, :, :LANES], g_ref[...], lo), cos4, sin_signed, first_half)
        v = z_ref[0, :, LANES:]
        vr = pltpu.roll(v, HEAD_DIM, 1)
        k_ref[0, 0, :, :LANES] = _head_a(y, lo).astype(BF16)
        k_ref[0, 1, :, :LANES] = _head_b(y, lo).astype(BF16)
        if with_onehot:
            k_ref[0, 0, :, LANES:] = onehot
            k_ref[0, 1, :, LANES:] = onehot
        v_ref[0, 0] = jnp.where(lo, v, vr).astype(BF16)
        v_ref[0, 1] = jnp.where(lo, vr, v).astype(BF16)

    @pl.when(ti == 0)
    def _reset():
        run_scr[...] = jnp.zeros(run_scr.shape, F32)

    xf = sm_ref[0] + fb_ref[...]
    log_f = (jnp.minimum(xf, 0.0) - jnp.log1p(jnp.exp(-jnp.abs(xf)))) * LOG2E
    tri = tri_ref[...]
    parts = _split3_f32(log_f)
    cum = sum(jnp.dot(tri, part.astype(BF16), preferred_element_type=F32) for part in parts)
    base_ref[0, 0] = run_scr[...]
    run_scr[...] = run_scr[...] + cum[tt - 1:tt, :]
    fvb_ref[0] = fv_ref[0].astype(BF16)

    ones_q = (lane >= AUG + 3) & (lane < AUG + 6)
    ones_k = (lane >= AUG) & (lane < AUG + 3)
    for p in range(FOX_HEADS // 2):
        sl = slice(p * LANES, (p + 1) * LANES)
        yq = _pair_rms(fq_ref[0, :, sl], gfq_ref[...], lo) * QK_SCALE
        yk = _pair_rms(fk_ref[0, :, sl], gfk_ref[...], lo)
        for e, split in ((0, _head_a), (1, _head_b)):
            h = 2 * p + e
            col = jnp.sum(jnp.where(lane == F_LANE + h, cum, 0.0), axis=1, keepdims=True)
            hi, mid, low = _split3_f32(col)
            aug_q = jnp.where(lane == AUG, hi, jnp.where(lane == AUG + 1, mid,
                              jnp.where(lane == AUG + 2, low, jnp.where(ones_q, 1.0, 0.0))))
            aug_k = jnp.where(lane == AUG + 3, -hi, jnp.where(lane == AUG + 4, -mid,
                              jnp.where(lane == AUG + 5, -low, jnp.where(ones_k, 1.0, 0.0))))
            fqa_ref[0, h] = (split(yq, lo) + aug_q).astype(BF16)
            fka_ref[0, h] = (split(yk, lo) + aug_k).astype(BF16)


def _prep(z, cos4, sin_signed, tri, gq, gks, gkw, gfq, gfk, fb, *, batch, seq):
    nblk = seq // TQ
    zspec = lambda w, c: pl.BlockSpec((1, TQ, w), lambda b, i, c=c: (b, i, c))
    row = pl.BlockSpec((1, LANES), lambda b, i: (0, 0))
    tab = pl.BlockSpec((TQ, LANES), lambda b, i: (i, 0))
    heads = lambda n, w: pl.BlockSpec((1, n, TQ, w), lambda b, i: (b, 0, i, 0))
    tok = lambda w: pl.BlockSpec((1, TQ, w), lambda b, i: (b, i, 0))
    bf = lambda *shape: jax.ShapeDtypeStruct(shape, BF16)
    return pl.pallas_call(
        _prep_kernel,
        grid=(batch, nblk),
        in_specs=[zspec(NSA_WIDTH, C_NSA_Q // NSA_WIDTH),
                  zspec(2 * LANES, C_NSA_KV // (2 * LANES)),
                  zspec(2 * LANES, C_NSA_KV // (2 * LANES) + 1),
                  zspec(2 * LANES, C_NSA_KV // (2 * LANES) + 2),
                  zspec(FOX_WIDTH, C_FOX_Q // FOX_WIDTH),
                  zspec(FOX_WIDTH, C_FOX_K // FOX_WIDTH),
                  zspec(FOX_WIDTH, C_FOX_V // FOX_WIDTH),
                  zspec(LANES, C_SMALL // LANES),
                  tab, tab, pl.BlockSpec((TQ, TQ), lambda b, i: (0, 0)),
                  row, row, row, row, row, row],
        out_specs=[heads(NSA_HEADS, LANES), tok(LANES), tok(LANES),
                   heads(NSA_KV_GROUPS, 2 * LANES), heads(NSA_KV_GROUPS, LANES),
                   heads(NSA_KV_GROUPS, LANES), heads(NSA_KV_GROUPS, LANES),
                   heads(FOX_HEADS, LANES), heads(FOX_HEADS, LANES), tok(FOX_WIDTH),
                   pl.BlockSpec((1, 1, 1, LANES), lambda b, i: (b, i, 0, 0))],
        out_shape=[bf(batch, NSA_HEADS, seq, LANES), bf(batch, seq, LANES), bf(batch, seq, LANES),
                   bf(batch, NSA_KV_GROUPS, seq, 2 * LANES), bf(batch, NSA_KV_GROUPS, seq, LANES),
                   bf(batch, NSA_KV_GROUPS, seq, LANES), bf(batch, NSA_KV_GROUPS, seq, LANES),
                   bf(batch, FOX_HEADS, seq, LANES), bf(batch, FOX_HEADS, seq, LANES),
                   bf(batch, seq, FOX_WIDTH),
                   jax.ShapeDtypeStruct((batch, nblk, 1, LANES), F32)],
        scratch_shapes=[pltpu.VMEM((1, LANES), F32)],
        compiler_params=_cparams(("parallel", "arbitrary")),
        name="prep",
    )(z, z, z, z, z, z, z, z, cos4, sin_signed, tri, gq, gks, gkw, gfq, gfk, fb)


def _compress_kernel(xk_ref, xv_ref, w1k_ref, w1v_ref, bk_ref, bv_ref, w2k_ref, w2v_ref, kg_ref,
                     kc_ref, vc_ref):
    nchunk = xk_ref.shape[1]
    for x_ref, w1_ref, b_ref, w2_ref, o_ref, is_key in ((xk_ref, w1k_ref, bk_ref, w2k_ref, kc_ref, True),
                                                        (xv_ref, w1v_ref, bv_ref, w2v_ref, vc_ref, False)):
        h = jnp.dot(x_ref[0], w1_ref[...], preferred_element_type=F32)
        for g in range(NSA_KV_GROUPS):
            c0 = g * 2 * NSA_CMP_HIDDEN
            top = h[:, c0:c0 + NSA_CMP_HIDDEN]
            bot = h[:, c0 + NSA_CMP_HIDDEN:c0 + 2 * NSA_CMP_HIDDEN]
            hid = top + pltpu.roll(bot, nchunk - 1, 0) + b_ref[...]
            act = (hid * jax.nn.sigmoid(hid)).astype(BF16)
            o = jnp.dot(act, w2_ref[...], preferred_element_type=F32)
            if is_key:
                ms = jnp.sum(o * o, axis=1, keepdims=True) * (1.0 / HEAD_DIM)
                o = o * lax.rsqrt(ms + EPS) * kg_ref[...]
            o_ref[0, g] = o.astype(BF16)


def _compress(xk, xv, w1k, w1v, bk, bv, w2k, w2v, kg, *, batch):
    nchunk, kin = xk.shape[1], xk.shape[2]
    xspec = pl.BlockSpec((1, nchunk, kin), lambda b: (b, 0, 0))
    full = lambda a: pl.BlockSpec(a.shape, lambda b: (0,) * a.ndim)
    ospec = pl.BlockSpec((1, NSA_KV_GROUPS, nchunk, LANES), lambda b: (b, 0, 0, 0))
    oshape = jax.ShapeDtypeStruct((batch, NSA_KV_GROUPS, nchunk, LANES), BF16)
    return pl.pallas_call(
        _compress_kernel,
        grid=(batch,),
        in_specs=[xspec, xspec, full(w1k), full(w1v), full(bk), full(bv), full(w2k), full(w2v), full(kg)],
        out_specs=[ospec, ospec],
        out_shape=[oshape, oshape],
        compiler_params=_cparams(("parallel",)),
        name="compress",
    )(xk, xv, w1k, w1v, bk, bv, w2k, w2v, kg)


def _gate_column(gate_blk, lane, idx):
    col = jnp.sum(jnp.where(lane == idx, gate_blk, 0.0), axis=1, keepdims=True)
    return jax.nn.sigmoid(col)


def _flash_kernel(qi_tab, ki_tab, fl_tab, *refs, nheads, shared_kv, window, has_delta,
                  has_u, gate_branch, nblk):
    refs = list(refs)
    base_ref = refs.pop(0) if has_delta else None
    q_ref = refs.pop(0)
    u_ref = refs.pop(0) if has_u else None
    k_ref = refs.pop(0)
    v_ref = refs.pop(0)
    gate_ref = refs.pop(0) if gate_branch is not None else None
    o_ref = refs.pop(0)
    m_scr = refs.pop(0)
    l_scr = refs.pop(0)
    acc_scr = refs.pop(0)
    qcat_scr = refs.pop(0) if has_u else None

    b = pl.program_id(0)
    g = pl.program_id(1)
    step = pl.program_id(2)
    qi = qi_tab[step]
    ki = ki_tab[step]
    fl = fl_tab[step]
    tq = q_ref.shape[2]
    tk = k_ref.shape[2]

    @pl.when((fl & 1) == 1)
    def _init():
        m_scr[...] = jnp.full(m_scr.shape, NEG, F32)
        l_scr[...] = jnp.zeros(l_scr.shape, F32)
        acc_scr[...] = jnp.zeros(acc_scr.shape, F32)
        if has_u:
            for r in range(nheads):
                qcat_scr[r, :, :LANES] = q_ref[0, r]
                qcat_scr[r, :, LANES:] = u_ref[0, 0]

    lane = lax.broadcasted_iota(jnp.int32, (tq, LANES), 1)
    lo = lane < HEAD_DIM

    def tile(diagonal):
        masked = diagonal or window is not None
        if masked:
            row = lax.broadcasted_iota(jnp.int32, (tq, tk), 0)
            col = lax.broadcasted_iota(jnp.int32, (tq, tk), 1)
            if diagonal:
                valid = row >= col
            else:
                valid = (tq + row - col) < window
        for p in range(nheads // 2):
            pvs, alphas = [], []
            for e in range(2):
                hh = 2 * p + e
                q = qcat_scr[hh] if has_u else q_ref[0, hh]
                k = k_ref[0, 0] if shared_kv else k_ref[0, hh]
                v = v_ref[0, 0] if shared_kv else v_ref[0, :, p * LANES:(p + 1) * LANES]
                s = lax.dot_general(q, k, (((1,), (1,)), ((), ())), preferred_element_type=F32)
                if masked:
                    s = jnp.where(valid, s, NEG)
                m_prev = m_scr[hh]
                m_cur = jnp.max(s, axis=1, keepdims=True)
                if has_delta:
                    hoff = F_LANE + g * nheads + hh
                    delta = (base_ref[(b * nblk + qi) * LANES + hoff]
                             - base_ref[(b * nblk + ki) * LANES + hoff])
                    m_cur = m_cur + delta
                m_new = jnp.maximum(m_prev, m_cur)
                alpha = jnp.exp2(m_prev - m_new)
                shift = m_new[:, :1]
                if has_delta:
                    shift = shift - delta
                pm = jnp.exp2(s - shift)
                l_scr[hh] = alpha * l_scr[hh] + jnp.sum(pm, axis=1, keepdims=True)
                m_scr[hh] = m_new
                pvs.append(jnp.dot(pm.astype(BF16), v, preferred_element_type=F32))
                alphas.append(alpha)
            acc_scr[p] = (acc_scr[p] * jnp.where(lo, alphas[0], alphas[1])
                          + jnp.where(lo, pvs[0], pvs[1]))

    pl.when(qi == ki)(lambda: tile(True))
    pl.when(qi != ki)(lambda: tile(False))

    @pl.when((fl & 2) == 2)
    def _finish():
        for p in range(nheads // 2):
            inv = jnp.where(lo, 1.0 / l_scr[2 * p], 1.0 / l_scr[2 * p + 1])
            if gate_branch is not None:
                gate_blk = gate_ref[0]
                h0 = g * nheads + 2 * p
                ga = _gate_column(gate_blk, lane, 3 * h0 + gate_branch)
                gb = _gate_column(gate_blk, lane, 3 * (h0 + 1) + gate_branch)
                inv = inv * jnp.where(lo, ga, gb)
            o_ref[0, :, p * LANES:(p + 1) * LANES] = acc_scr[p] * inv


def _tile_tables(nq, window_tiles):
    qi, ki, fl = [], [], []
    for i in range(nq):
        lo = 0 if window_tiles is None else max(0, i - window_tiles)
        for j in range(lo, i + 1):
            qi.append(i)
            ki.append(j)
            fl.append((1 if j == lo else 0) | (2 if j == i else 0))
    return (jnp.asarray(qi, jnp.int32), jnp.asarray(ki, jnp.int32), jnp.asarray(fl, jnp.int32))


def _flash(q, k, v, *, batch, seq, shared_kv, window=None, base=None, u=None, z=None,
           gate_branch=None):
    nheads = NSA_GROUP_HEADS
    nq = seq // TQ
    ngroups = 8 // nheads
    da = k.shape[-1]
    tabs = _tile_tables(nq, None if window is None else window // TQ)
    nsteps = int(tabs[0].shape[0])
    has_delta = base is not None
    has_u = u is not None
    out_w = LANES * (nheads // 2)

    in_specs, args = [], []
    if has_delta:
        in_specs.append(pl.BlockSpec(memory_space=pltpu.SMEM))
        args.append(base)
    in_specs.append(pl.BlockSpec((1, nheads, TQ, LANES), lambda b, g, s, qt, kt, ft: (b, g, qt[s], 0)))
    args.append(q)
    if has_u:
        in_specs.append(pl.BlockSpec((1, 1, TQ, LANES), lambda b, g, s, qt, kt, ft: (b, g, qt[s], 0)))
        args.append(u)
    if shared_kv:
        in_specs.append(pl.BlockSpec((1, 1, TQ, da), lambda b, g, s, qt, kt, ft: (b, g, kt[s], 0)))
        in_specs.append(pl.BlockSpec((1, 1, TQ, LANES), lambda b, g, s, qt, kt, ft: (b, g, kt[s], 0)))
    else:
        in_specs.append(pl.BlockSpec((1, nheads, TQ, da), lambda b, g, s, qt, kt, ft: (b, g, kt[s], 0)))
        in_specs.append(pl.BlockSpec((1, TQ, out_w), lambda b, g, s, qt, kt, ft: (b, kt[s], g)))
    args += [k, v]
    if gate_branch is not None:
        in_specs.append(pl.BlockSpec((1, TQ, LANES),
                                     lambda b, g, s, qt, kt, ft: (b, qt[s], C_SMALL // LANES)))
        args.append(z)

    scratch = [pltpu.VMEM((nheads, TQ, LANES), F32),
               pltpu.VMEM((nheads, TQ, LANES), F32),
               pltpu.VMEM((nheads // 2, TQ, LANES), F32)]
    if has_u:
        scratch.append(pltpu.VMEM((nheads, TQ, 2 * LANES), BF16))

    kern = functools.partial(_flash_kernel, nheads=nheads, shared_kv=shared_kv, window=window,
                             has_delta=has_delta, has_u=has_u, gate_branch=gate_branch, nblk=nq)
    return pl.pallas_call(
        kern,
        grid_spec=pltpu.PrefetchScalarGridSpec(
            num_scalar_prefetch=3,
            grid=(batch, ngroups, nsteps),
            in_specs=in_specs,
            out_specs=pl.BlockSpec((1, TQ, out_w), lambda b, g, s, qt, kt, ft: (b, qt[s], g)),
            scratch_shapes=scratch),
        out_shape=jax.ShapeDtypeStruct((batch, seq, 4 * LANES), F32),
        compiler_params=_cparams(("parallel", "parallel", "arbitrary")),
        name="flash_" + ("fox" if has_delta else ("sel" if has_u else "win")),
    )(*tabs, *args)


def _cmp_kernel(q_ref, kc_ref, vc_ref, ov_ref, gate_ref, o_ref, u_ref):
    g = pl.program_id(1)
    qi = pl.program_id(2)
    tq = q_ref.shape[2]
    ncmp = kc_ref.shape[2]
    nslc = ov_ref.shape[1]

    t = qi * tq + lax.broadcasted_iota(jnp.int32, (tq, ncmp), 0)
    n = lax.broadcasted_iota(jnp.int32, (tq, ncmp), 1)
    valid = (n * NSA_CMP_STRIDE + (NSA_CMP_LEN - 1)) <= t
    kc = kc_ref[0, 0]
    vc = vc_ref[0, 0]
    ov = ov_ref[...]
    lane = lax.broadcasted_iota(jnp.int32, (tq, LANES), 1)
    lo = lane < HEAD_DIM
    gate_blk = gate_ref[0]

    imp = jnp.zeros((tq, nslc), F32)
    outs = []
    for r in range(NSA_GROUP_HEADS):
        s = lax.dot_general(q_ref[0, r], kc, (((1,), (1,)), ((), ())), preferred_element_type=F32)
        s = jnp.where(valid, s, NEG)
        m = jnp.max(s, axis=1, keepdims=True)
        e = jnp.where(valid, jnp.exp2(s - m), 0.0)
        d = jnp.sum(e, axis=1, keepdims=True)
        pb = (e / jnp.where(d > 0.0, d, 1.0)).astype(BF16)
        o = jnp.dot(pb, vc, preferred_element_type=F32)
        gcol = _gate_column(gate_blk, lane, 3 * (g * NSA_GROUP_HEADS + r))
        outs.append(o * gcol)
        imp = imp + jnp.dot(pb, ov, preferred_element_type=F32)
    for p in range(NSA_GROUP_HEADS // 2):
        o_ref[0, :, p * LANES:(p + 1) * LANES] = jnp.where(lo, outs[2 * p], outs[2 * p + 1])

    tpos = qi * tq + lax.broadcasted_iota(jnp.int32, (tq, nslc), 0)
    jblk = lax.broadcasted_iota(jnp.int32, (tq, nslc), 1)
    cur = jnp.right_shift(tpos, SEL_SHIFT)
    forced = (jblk == 0) | (jblk == cur) | (jblk == cur - 1)
    score = jnp.where(forced, FORCE_SCORE, imp)
    score = jnp.where(jblk <= cur, score, -jnp.inf)

    sc = score.T
    ji = lax.broadcasted_iota(jnp.int32, (nslc, tq), 0)
    ok_t = ji <= jnp.right_shift(qi * tq + lax.broadcasted_iota(jnp.int32, (nslc, tq), 1), SEL_SHIFT)
    jt = ji.astype(F32)
    for _ in range(NSA_SEL_TOPK):
        mx = jnp.max(sc, axis=0, keepdims=True)
        first = jnp.min(jnp.where(sc == mx, jt, float(nslc)), axis=0, keepdims=True)
        sc = jnp.where(jt == first, -jnp.inf, sc)
    unsel = jnp.where((sc == -jnp.inf) & ok_t, 0.0, 1.0)
    u_ref[0, 0] = unsel.T.astype(BF16)


def _cmp_select(qn, kc, vc, ov, z, *, batch, seq):
    ncmp = kc.shape[2]
    nslc = ov.shape[1]
    return pl.pallas_call(
        _cmp_kernel,
        grid=(batch, NSA_KV_GROUPS, seq // CMP_TQ),
        in_specs=[pl.BlockSpec((1, NSA_GROUP_HEADS, CMP_TQ, LANES), lambda b, g, i: (b, g, i, 0)),
                  pl.BlockSpec((1, 1, ncmp, LANES), lambda b, g, i: (b, g, 0, 0)),
                  pl.BlockSpec((1, 1, ncmp, LANES), lambda b, g, i: (b, g, 0, 0)),
                  pl.BlockSpec((ncmp, nslc), lambda b, g, i: (0, 0)),
                  pl.BlockSpec((1, CMP_TQ, LANES), lambda b, g, i: (b, i, C_SMALL // LANES))],
        out_specs=[pl.BlockSpec((1, CMP_TQ, 2 * LANES), lambda b, g, i: (b, i, g)),
                   pl.BlockSpec((1, 1, CMP_TQ, nslc), lambda b, g, i: (b, g, i, 0))],
        out_shape=[jax.ShapeDtypeStruct((batch, seq, NSA_WIDTH), F32),
                   jax.ShapeDtypeStruct((batch, NSA_KV_GROUPS, seq, nslc), BF16)],
        compiler_params=_cparams(("parallel", "parallel", "parallel")),
        name="cmp_select",
    )(qn, kc, vc, ov, z)


def _head_rms(x, g):
    return x * lax.rsqrt(jnp.mean(x * x, axis=-1, keepdims=True) + EPS) * g


def _mem_kernel(zq_ref, kv_ref, qg_ref, kg_ref, o_ref, kn_scr, vb_scr):
    @pl.when(pl.program_id(1) == 0)
    def _prep():
        for h in range(MEM_HEADS):
            kh = kv_ref[0, :, h * MEM_HEAD_DIM:(h + 1) * MEM_HEAD_DIM]
            kn_scr[h] = _head_rms(kh, kg_ref[...]).astype(BF16)
            vb_scr[h] = kv_ref[0, :, MEM_WIDTH + h * MEM_HEAD_DIM:
                               MEM_WIDTH + (h + 1) * MEM_HEAD_DIM].astype(BF16)

    for h in range(MEM_HEADS):
        sl = slice(h * MEM_HEAD_DIM, (h + 1) * MEM_HEAD_DIM)
        qh = (_head_rms(zq_ref[0, :, sl], qg_ref[...]) * ((MEM_HEAD_DIM ** -0.5) * LOG2E)).astype(BF16)
        s = lax.dot_general(qh, kn_scr[h], (((1,), (1,)), ((), ())), preferred_element_type=F32)
        m = jnp.max(s, axis=1, keepdims=True)
        e = jnp.exp2(s - m)
        p = (e / jnp.sum(e, axis=1, keepdims=True)).astype(BF16)
        o_ref[0, :, sl] = jnp.dot(p, vb_scr[h], preferred_element_type=F32)


def _mem_attention(z, mem_kv, q_g, k_g, *, batch, seq):
    mlen = mem_kv.shape[1]
    tq = TQ
    return pl.pallas_call(
        _mem_kernel,
        grid=(batch, seq // tq),
        in_specs=[pl.BlockSpec((1, tq, MEM_WIDTH), lambda b, i: (b, i, C_MEM_Q // MEM_WIDTH)),
                  pl.BlockSpec((1, mlen, 2 * MEM_WIDTH), lambda b, i: (b, 0, 0)),
                  pl.BlockSpec((1, MEM_HEAD_DIM), lambda b, i: (0, 0)),
                  pl.BlockSpec((1, MEM_HEAD_DIM), lambda b, i: (0, 0))],
        out_specs=pl.BlockSpec((1, tq, MEM_WIDTH), lambda b, i: (b, i, 0)),
        out_shape=jax.ShapeDtypeStruct((batch, seq, MEM_WIDTH), F32),
        scratch_shapes=[pltpu.VMEM((MEM_HEADS, mlen, MEM_HEAD_DIM), BF16),
                        pltpu.VMEM((MEM_HEADS, mlen, MEM_HEAD_DIM), BF16)],
        compiler_params=_cparams(("parallel", "arbitrary")),
        name="mem_attention",
    )(z, mem_kv, q_g.reshape(1, MEM_HEAD_DIM), k_g.reshape(1, MEM_HEAD_DIM))


def _out_kernel(x_ref, oc_ref, os_ref, ow_ref, ob_ref, om_ref, sa_ref, sb_ref, sm_ref,
                g0_ref, g1_ref, g2_ref, wa_ref, wb_ref, wm_ref, wo_ref, y_ref):
    oa = (oc_ref[...] + os_ref[...] + ow_ref[...]) * jax.nn.silu(sa_ref[...])
    ob = ob_ref[...] * jax.nn.silu(sb_ref[...])
    om = om_ref[...] * jax.nn.silu(sm_ref[...])
    u = (jax.nn.sigmoid(g0_ref[...]) * jnp.dot(oa.astype(BF16), wa_ref[...], preferred_element_type=F32)
         + jax.nn.sigmoid(g1_ref[...]) * jnp.dot(ob.astype(BF16), wb_ref[...], preferred_element_type=F32)
         + jax.nn.sigmoid(g2_ref[...]) * jnp.dot(om.astype(BF16), wm_ref[...], preferred_element_type=F32))
    y_ref[...] = x_ref[...] + jnp.dot(u.astype(BF16), wo_ref[...], preferred_element_type=F32)


def _out_proj(x2d, o_c, o_s, o_w, o_b, o_m, z2d, wa, wb, wm, wo):
    m = x2d.shape[0]
    tm = 256
    w512 = 512
    row512 = lambda c: pl.BlockSpec((tm, w512), lambda i, c=c: (i, c))
    row1024 = lambda c: pl.BlockSpec((tm, D_MODEL), lambda i, c=c: (i, c))
    full = lambda shape: pl.BlockSpec(shape, lambda i: (0, 0))
    return pl.pallas_call(
        _out_kernel,
        grid=(m // tm,),
        in_specs=[row1024(0), row512(0), row512(0), row512(0), row512(0), row512(0),
                  row512(C_NSA_SILU // w512), row512(C_FOX_SILU // w512), row512(C_MEM_SILU // w512),
                  row1024(C_MERGE // D_MODEL), row1024(C_MERGE // D_MODEL + 1),
                  row1024(C_MERGE // D_MODEL + 2),
                  full((w512, D_MODEL)), full((w512, D_MODEL)), full((w512, D_MODEL)),
                  full((D_MODEL, D_MODEL))],
        out_specs=row1024(0),
        out_shape=jax.ShapeDtypeStruct((m, D_MODEL), F32),
        compiler_params=_cparams(("parallel",)),
        name="out_proj",
    )(x2d, o_c, o_s, o_w, o_b, o_m, z2d, z2d, z2d, z2d, z2d, z2d, wa, wb, wm, wo)


def _permute_w_in(w_in):
    o = np.cumsum([0, NSA_WIDTH, 6 * NSA_KV_GROUPS * HEAD_DIM, 3 * NSA_HEADS, NSA_WIDTH,
                   3 * FOX_WIDTH, FOX_HEADS, FOX_WIDTH, MEM_WIDTH, MEM_WIDTH, 3 * D_MODEL])
    nsa_q, nsa_kv, nsa_gate, nsa_silu, fox_qkv, fox_f, fox_silu, mem_q, mem_silu, merge = [
        w_in[..., o[i]:o[i + 1]] for i in range(10)]
    pad = jnp.zeros(w_in.shape[:2] + (LANES - 3 * NSA_HEADS - FOX_HEADS,), w_in.dtype)
    return jnp.concatenate([nsa_q, nsa_silu, fox_qkv, fox_silu, mem_q, mem_silu, merge, nsa_kv,
                            nsa_gate, fox_f, pad], axis=-1).astype(BF16)


def _pad_lanes(x, width=LANES):
    return jnp.pad(x, [(0, 0)] * (x.ndim - 1) + [(0, width - x.shape[-1])])


def _tile2(g):
    return jnp.concatenate([g, g], axis=-1)[:, None, :]


def _overlap(n_cmp_pad, n_slc):
    cs = np.arange(n_cmp_pad)[:, None] * NSA_CMP_STRIDE
    ss = np.arange(NSLC_PAD)[None, :] * NSA_SEL_LEN
    ov = np.clip(np.minimum(cs + NSA_CMP_LEN, ss + NSA_SEL_LEN) - np.maximum(cs, ss), 0, None)
    ov = ov / NSA_CMP_LEN
    ov[-1] = 0.0
    ov[:, n_slc:] = 0.0
    return jnp.asarray(ov, BF16)


def _compress_w1(w1):
    nl = w1.shape[0]
    halves = w1.reshape(nl, 2, NSA_CMP_STRIDE, 1, HEAD_DIM, NSA_CMP_HIDDEN)
    eye = jnp.eye(NSA_KV_GROUPS, dtype=w1.dtype)
    out = jnp.einsum("pg,zhldn->zlpdghn", eye, halves[:, :, :, 0])
    return out.reshape(nl, NSA_CMP_STRIDE * NSA_KV_GROUPS * HEAD_DIM,
                       NSA_KV_GROUPS * 2 * NSA_CMP_HIDDEN).astype(BF16)


def _layer(x, mem, p, consts):
    batch, seq, _ = x.shape
    m_rows = batch * seq
    cos4, sin_signed, tri, ov = consts
    z2d = _norm_matmul(x.reshape(m_rows, D_MODEL), p["norm_g"], p["w_in"], 512, Z_WIDTH // 3)
    z = z2d.reshape(batch, seq, Z_WIDTH)

    (qn, kcr, vcr, ks, vs, kw, vw, fqa, fka, fv, base) = _prep(
        z, cos4, sin_signed, tri, p["gq"], p["gks"], p["gkw"], p["gfq"], p["gfk"], p["fb"],
        batch=batch, seq=seq)

    nchunk = seq // NSA_CMP_STRIDE
    kc, vc = _compress(kcr.reshape(batch, nchunk, NSA_CMP_STRIDE * LANES),
                       vcr.reshape(batch, nchunk, NSA_CMP_STRIDE * LANES),
                       p["w1k"], p["w1v"], p["bk"], p["bv"], p["w2k"], p["w2v"], p["gkc"], batch=batch)

    o_c, unsel = _cmp_select(qn, kc, vc, ov, z, batch=batch, seq=seq)
    o_s = _flash(qn, ks, vs, batch=batch, seq=seq, shared_kv=True, u=unsel, z=z, gate_branch=1)
    o_w = _flash(qn, kw, vw, batch=batch, seq=seq, shared_kv=True, window=NSA_WINDOW, z=z, gate_branch=2)
    o_b = _flash(fqa, fka, fv, batch=batch, seq=seq, shared_kv=False, base=base.reshape(-1))

    mlen = mem.shape[1]
    mem_kv = _norm_matmul(mem.reshape(batch * mlen, D_MODEL), p["mem_norm_g"], p["w_mem_kv"],
                          min(512, batch * mlen), 2 * MEM_WIDTH).reshape(batch, mlen, 2 * MEM_WIDTH)
    o_m = _mem_attention(z, mem_kv, p["mem_q_norm"], p["mem_k_norm"], batch=batch, seq=seq)

    y = _out_proj(x.reshape(m_rows, D_MODEL), o_c.reshape(m_rows, -1), o_s.reshape(m_rows, -1),
                  o_w.reshape(m_rows, -1), o_b.reshape(m_rows, -1), o_m.reshape(m_rows, -1), z2d,
                  p["w_branch_a"], p["w_branch_b"], p["w_branch_m"], p["w_out"])
    return y.reshape(batch, seq, D_MODEL)


def kernel(x, mem, norm_g, mem_norm_g, w_in, nsa_q_norm, nsa_k_norm, cmp_pe_k, cmp_w1_k, cmp_w2_k,
           cmp_pe_v, cmp_w1_v, cmp_w2_v, fox_q_norm, fox_k_norm, fox_f_bias, mem_q_norm, mem_k_norm,
           w_mem_kv, w_branch_a, w_branch_b, w_branch_m, w_out):
    batch, seq, _ = x.shape
    depth = w_in.shape[0]
    n_slc = seq // NSA_SEL_LEN
    assert n_slc <= NSLC_PAD and seq % TQ == 0 and NSA_WINDOW == TQ

    half = HEAD_DIM // 2
    inv_freq = ROPE_THETA ** (-jnp.arange(half, dtype=F32) / half)
    ang = jnp.arange(seq).astype(F32)[:, None] * inv_freq[None, :]
    cos, sin = jnp.cos(ang), jnp.sin(ang)
    cos4 = jnp.concatenate([cos, cos, cos, cos], axis=-1)
    sin_signed = jnp.concatenate([-sin, sin, -sin, sin], axis=-1)
    tri = jnp.asarray(np.tril(np.ones((TQ, TQ), np.float32)), BF16)
    consts = (cos4, sin_signed, tri, _overlap(seq // NSA_CMP_STRIDE, n_slc))

    hp = lax.Precision.HIGHEST
    pe_bias = lambda pe, w1: jnp.einsum("lk,lkn->ln", pe.reshape(depth, -1), w1, precision=hp)[:, None, :]
    fb = jnp.zeros((depth, 1, LANES), F32).at[:, 0, F_LANE:F_LANE + FOX_HEADS].set(fox_f_bias)
    stacked = dict(
        norm_g=norm_g, mem_norm_g=mem_norm_g, w_in=_permute_w_in(w_in),
        gq=_tile2(nsa_q_norm), gks=_tile2(nsa_k_norm[:, 1]), gkw=_tile2(nsa_k_norm[:, 2]),
        gkc=_pad_lanes(nsa_k_norm[:, 0])[:, None, :], gfq=_tile2(fox_q_norm), gfk=_tile2(fox_k_norm),
        fb=fb, w1k=_compress_w1(cmp_w1_k), w1v=_compress_w1(cmp_w1_v),
        bk=pe_bias(cmp_pe_k, cmp_w1_k), bv=pe_bias(cmp_pe_v, cmp_w1_v),
        w2k=_pad_lanes(cmp_w2_k).astype(BF16),
        w2v=jnp.concatenate([cmp_w2_v, cmp_w2_v], axis=-1).astype(BF16),
        mem_q_norm=mem_q_norm, mem_k_norm=mem_k_norm, w_mem_kv=w_mem_kv.astype(BF16),
        w_branch_a=w_branch_a.astype(BF16), w_branch_b=w_branch_b.astype(BF16),
        w_branch_m=w_branch_m.astype(BF16), w_out=w_out.astype(BF16))
    for l in range(depth):
        x = _layer(x, mem, {k: v[l] for k, v in stacked.items()}, consts)
    return x
```

```python
import functools

import numpy as np
import jax
import jax.numpy as jnp
from jax import lax
from jax.experimental import pallas as pl
from jax.experimental.pallas import tpu as pltpu

F32 = jnp.float32
BF16 = jnp.bfloat16

D_MODEL = 1024
HEAD_DIM = 64
ROPE_THETA = 10000.0
EPS = 1e-6
NSA_HEADS = 8
NSA_KV_GROUPS = 2
NSA_GROUP_HEADS = NSA_HEADS // NSA_KV_GROUPS
NSA_CMP_LEN = 32
NSA_CMP_STRIDE = 16
NSA_CMP_HIDDEN = 128
NSA_SEL_LEN = 64
NSA_SEL_TOPK = 16
NSA_WINDOW = 512
FORCE_SCORE = 1e9
NSA_WIDTH = NSA_HEADS * HEAD_DIM
FOX_HEADS = 8
FOX_WIDTH = FOX_HEADS * HEAD_DIM
MEM_HEADS = 4
MEM_HEAD_DIM = 128
MEM_WIDTH = MEM_HEADS * MEM_HEAD_DIM

LANES = 128
VMEM_LIMIT = 48 * 1024 * 1024
NEG = -1e30
SEL_MASK = 32768.0
SEL_SHIFT = 6
NSLC_PAD = LANES
LOG2E = 1.4426950408889634
QK_SCALE = (HEAD_DIM ** -0.5) * LOG2E
BOUND_MARGIN = 1.02
MAX_FIXED_BOUND = 60.0

C_NSA_Q = 0
C_NSA_SILU = 512
C_FOX_Q = 1024
C_FOX_K = 1536
C_FOX_V = 2048
C_FOX_SILU = 2560
C_MEM_Q = 3072
C_MEM_SILU = 3584
C_MERGE = 4096
C_NSA_KV = 7168
C_SMALL = 7936
Z_WIDTH = 8064
F_LANE = 3 * NSA_HEADS
AUG = HEAD_DIM

TQ = 512
CMP_TQ = 256


def _cparams(sem):
    return pltpu.CompilerParams(dimension_semantics=sem, vmem_limit_bytes=VMEM_LIMIT)


def _norm_matmul_kernel(x_ref, g_ref, w_ref, o_ref):
    x = x_ref[...]
    ms = jnp.mean(x * x, axis=-1, keepdims=True)
    h = (x * lax.rsqrt(ms + EPS) * g_ref[...]).astype(BF16)
    o_ref[...] = jnp.dot(h, w_ref[...], preferred_element_type=F32)


def _norm_matmul(x2d, g, w, tm, tn):
    m, k = x2d.shape
    n = w.shape[1]
    return pl.pallas_call(
        _norm_matmul_kernel,
        grid=(n // tn, m // tm),
        in_specs=[pl.BlockSpec((tm, k), lambda j, i: (i, 0)),
                  pl.BlockSpec((1, k), lambda j, i: (0, 0)),
                  pl.BlockSpec((k, tn), lambda j, i: (0, j))],
        out_specs=pl.BlockSpec((tm, tn), lambda j, i: (i, j)),
        out_shape=jax.ShapeDtypeStruct((m, n), F32),
        compiler_params=_cparams(("parallel", "parallel")),
        name="norm_matmul",
    )(x2d, g.reshape(1, k), w)


def _pair_rms(x, g2, lo):
    sq = x * x
    s_lo = jnp.sum(jnp.where(lo, sq, 0.0), axis=1, keepdims=True)
    s_hi = jnp.sum(jnp.where(lo, 0.0, sq), axis=1, keepdims=True)
    ms = jnp.where(lo, s_lo, s_hi) * (1.0 / HEAD_DIM)
    return x * lax.rsqrt(ms + EPS) * g2


def _pair_rope(y, cos4, sin_signed, first_half):
    rot = jnp.where(first_half, pltpu.roll(y, LANES - HEAD_DIM // 2, 1), pltpu.roll(y, HEAD_DIM // 2, 1))
    return y * cos4 + rot * sin_signed


def _head_a(y, lo, tail=0.0):
    return jnp.where(lo, y, tail)


def _head_b(y, lo, tail=0.0):
    return jnp.where(lo, pltpu.roll(y, HEAD_DIM, 1), tail)


def _split3_f32(c):
    hi = c.astype(BF16).astype(F32)
    r = c - hi
    mid = r.astype(BF16).astype(F32)
    return hi, mid, r - mid


def _prep_kernel(zq_ref, zc_ref, zs_ref, zw_ref, fq_ref, fk_ref, fv_ref, sm_ref, cos_ref, sin_ref,
                 tri_ref, gq_ref, gks_ref, gkw_ref, gfq_ref, gfk_ref, fb_ref, ss_ref, sw_ref, sf_ref,
                 qn_ref, kcr_ref, vcr_ref, ks_ref, vs_ref, kw_ref, vw_ref, fqd_ref, fqo_ref, fka_ref,
                 fvb_ref, base_ref, run_scr):
    ti = pl.program_id(1)
    tt = zq_ref.shape[1]
    lane = lax.broadcasted_iota(jnp.int32, (tt, LANES), 1)
    lo = lane < HEAD_DIM
    first_half = (lane & (HEAD_DIM // 2)) == 0
    cos4 = cos_ref[...]
    sin_signed = sin_ref[...]
    one_at_aug = jnp.where(lane == AUG, 1.0, 0.0)

    for p in range(NSA_HEADS // 2):
        y = _pair_rope(_pair_rms(zq_ref[0, :, p * LANES:(p + 1) * LANES], gq_ref[...], lo),
                       cos4, sin_signed, first_half) * QK_SCALE
        qn_ref[0, 2 * p] = _head_a(y, lo, one_at_aug).astype(BF16)
        qn_ref[0, 2 * p + 1] = _head_b(y, lo, one_at_aug).astype(BF16)

    kcr_ref[0] = _pair_rope(zc_ref[0, :, :LANES], cos4, sin_signed, first_half).astype(BF16)
    vcr_ref[0] = zc_ref[0, :, LANES:].astype(BF16)

    sel_lane = jnp.right_shift(ti * tt + lax.broadcasted_iota(jnp.int32, (tt, LANES), 0), SEL_SHIFT)
    onehot = jnp.where(lane == sel_lane, -SEL_MASK, 0.0).astype(BF16)
    for z_ref, g_ref, k_ref, v_ref, bound_ref, with_onehot in (
            (zs_ref, gks_ref, ks_ref, vs_ref, ss_ref, True),
            (zw_ref, gkw_ref, kw_ref, vw_ref, sw_ref, False)):
        y = _pair_rope(_pair_rms(z_ref[0, :, :LANES], g_ref[...], lo), cos4, sin_signed, first_half)
        v = z_ref[0, :, LANES:]
        vr = pltpu.roll(v, HEAD_DIM, 1)
        neg_bound = jnp.where(lane == AUG, -bound_ref[...], 0.0)
        k_ref[0, 0, :, :LANES] = _head_a(y, lo, neg_bound).astype(BF16)
        k_ref[0, 1, :, :LANES] = _head_b(y, lo, neg_bound).astype(BF16)
        if with_onehot:
            k_ref[0, 0, :, LANES:] = onehot
            k_ref[0, 1, :, LANES:] = onehot
        v_ref[0, 0] = jnp.where(lo, v, vr).astype(BF16)
        v_ref[0, 1] = jnp.where(lo, vr, v).astype(BF16)

    @pl.when(ti == 0)
    def _reset():
        run_scr[...] = jnp.zeros(run_scr.shape, F32)

    xf = sm_ref[0] + fb_ref[...]
    log_f = (jnp.minimum(xf, 0.0) - jnp.log1p(jnp.exp(-jnp.abs(xf)))) * LOG2E
    tri = tri_ref[...]
    parts = _split3_f32(log_f)
    cum = sum(jnp.dot(tri, part.astype(BF16), preferred_element_type=F32) for part in parts)
    base_ref[0, 0] = run_scr[...]
    run_scr[...] = run_scr[...] + cum[tt - 1:tt, :]
    fvb_ref[0] = fv_ref[0].astype(BF16)

    def three(start, parts, rest):
        out = rest
        for i, part in enumerate(parts):
            out = jnp.where(lane == start + i, part, out)
        return out

    span = lambda a, b: jnp.where((lane >= a) & (lane < b), 1.0, 0.0)
    q_tail_diag = span(AUG + 3, AUG + 6) + span(AUG + 9, AUG + 10)
    q_tail_off = span(AUG + 6, AUG + 10)
    k_tail = span(AUG, AUG + 3) + jnp.where(lane == AUG + 9, -sf_ref[...], 0.0)
    for p in range(FOX_HEADS // 2):
        sl = slice(p * LANES, (p + 1) * LANES)
        yq = _pair_rms(fq_ref[0, :, sl], gfq_ref[...], lo) * QK_SCALE
        yk = _pair_rms(fk_ref[0, :, sl], gfk_ref[...], lo)
        for e, split in ((0, _head_a), (1, _head_b)):
            h = 2 * p + e
            col = jnp.sum(jnp.where(lane == F_LANE + h, cum, 0.0), axis=1, keepdims=True)
            a_start = _split3_f32(col)
            a_end = _split3_f32(col - col[tt - 1:tt, :])
            qh = split(yq, lo)
            fqd_ref[0, h] = (qh + three(AUG, a_start, q_tail_diag)).astype(BF16)
            fqo_ref[0, h] = (qh + three(AUG, a_start, q_tail_off)).astype(BF16)
            aug_k = three(AUG + 3, [-a for a in a_start], three(AUG + 6, [-a for a in a_end], k_tail))
            fka_ref[0, h] = (split(yk, lo) + aug_k).astype(BF16)


def _prep(z, cos4, sin_signed, tri, gq, gks, gkw, gfq, gfk, fb, s_sel, s_win, s_fox, *, batch, seq):
    nblk = seq // TQ
    zspec = lambda w, c: pl.BlockSpec((1, TQ, w), lambda b, i, c=c: (b, i, c))
    row = pl.BlockSpec((1, LANES), lambda b, i: (0, 0))
    tab = pl.BlockSpec((TQ, LANES), lambda b, i: (i, 0))
    heads = lambda n, w: pl.BlockSpec((1, n, TQ, w), lambda b, i: (b, 0, i, 0))
    tok = lambda w: pl.BlockSpec((1, TQ, w), lambda b, i: (b, i, 0))
    bf = lambda *shape: jax.ShapeDtypeStruct(shape, BF16)
    return pl.pallas_call(
        _prep_kernel,
        grid=(batch, nblk),
        in_specs=[zspec(NSA_WIDTH, C_NSA_Q // NSA_WIDTH),
                  zspec(2 * LANES, C_NSA_KV // (2 * LANES)),
                  zspec(2 * LANES, C_NSA_KV // (2 * LANES) + 1),
                  zspec(2 * LANES, C_NSA_KV // (2 * LANES) + 2),
                  zspec(FOX_WIDTH, C_FOX_Q // FOX_WIDTH),
                  zspec(FOX_WIDTH, C_FOX_K // FOX_WIDTH),
                  zspec(FOX_WIDTH, C_FOX_V // FOX_WIDTH),
                  zspec(LANES, C_SMALL // LANES),
                  tab, tab, pl.BlockSpec((TQ, TQ), lambda b, i: (0, 0)),
                  row, row, row, row, row, row, row, row, row],
        out_specs=[heads(NSA_HEADS, LANES), tok(LANES), tok(LANES),
                   heads(NSA_KV_GROUPS, 2 * LANES), heads(NSA_KV_GROUPS, LANES),
                   heads(NSA_KV_GROUPS, LANES), heads(NSA_KV_GROUPS, LANES),
                   heads(FOX_HEADS, LANES), heads(FOX_HEADS, LANES), heads(FOX_HEADS, LANES),
                   tok(FOX_WIDTH), pl.BlockSpec((1, 1, 1, LANES), lambda b, i: (b, i, 0, 0))],
        out_shape=[bf(batch, NSA_HEADS, seq, LANES), bf(batch, seq, LANES), bf(batch, seq, LANES),
                   bf(batch, NSA_KV_GROUPS, seq, 2 * LANES), bf(batch, NSA_KV_GROUPS, seq, LANES),
                   bf(batch, NSA_KV_GROUPS, seq, LANES), bf(batch, NSA_KV_GROUPS, seq, LANES),
                   bf(batch, FOX_HEADS, seq, LANES), bf(batch, FOX_HEADS, seq, LANES),
                   bf(batch, FOX_HEADS, seq, LANES), bf(batch, seq, FOX_WIDTH),
                   jax.ShapeDtypeStruct((batch, nblk, 1, LANES), F32)],
        scratch_shapes=[pltpu.VMEM((1, LANES), F32)],
        compiler_params=_cparams(("parallel", "arbitrary")),
        name="prep",
    )(z, z, z, z, z, z, z, z, cos4, sin_signed, tri, gq, gks, gkw, gfq, gfk, fb, s_sel, s_win, s_fox)


def _compress_kernel(xk_ref, xv_ref, w1k_ref, w1v_ref, bk_ref, bv_ref, w2k_ref, w2v_ref, kg_ref,
                     kc_ref, vc_ref):
    nchunk = xk_ref.shape[1]
    for x_ref, w1_ref, b_ref, w2_ref, o_ref, is_key in ((xk_ref, w1k_ref, bk_ref, w2k_ref, kc_ref, True),
                                                        (xv_ref, w1v_ref, bv_ref, w2v_ref, vc_ref, False)):
        h = jnp.dot(x_ref[0], w1_ref[...], preferred_element_type=F32)
        for g in range(NSA_KV_GROUPS):
            c0 = g * 2 * NSA_CMP_HIDDEN
            top = h[:, c0:c0 + NSA_CMP_HIDDEN]
            bot = h[:, c0 + NSA_CMP_HIDDEN:c0 + 2 * NSA_CMP_HIDDEN]
            hid = top + pltpu.roll(bot, nchunk - 1, 0) + b_ref[...]
            act = (hid * jax.nn.sigmoid(hid)).astype(BF16)
            o = jnp.dot(act, w2_ref[...], preferred_element_type=F32)
            if is_key:
                ms = jnp.sum(o * o, axis=1, keepdims=True) * (1.0 / HEAD_DIM)
                o = o * lax.rsqrt(ms + EPS) * kg_ref[...]
            o_ref[0, g] = o.astype(BF16)


def _compress(xk, xv, w1k, w1v, bk, bv, w2k, w2v, kg, *, batch):
    nchunk, kin = xk.shape[1], xk.shape[2]
    xspec = pl.BlockSpec((1, nchunk, kin), lambda b: (b, 0, 0))
    full = lambda a: pl.BlockSpec(a.shape, lambda b: (0,) * a.ndim)
    ospec = pl.BlockSpec((1, NSA_KV_GROUPS, nchunk, LANES), lambda b: (b, 0, 0, 0))
    oshape = jax.ShapeDtypeStruct((batch, NSA_KV_GROUPS, nchunk, LANES), BF16)
    return pl.pallas_call(
        _compress_kernel,
        grid=(batch,),
        in_specs=[xspec, xspec, full(w1k), full(w1v), full(bk), full(bv), full(w2k), full(w2v), full(kg)],
        out_specs=[ospec, ospec],
        out_shape=[oshape, oshape],
        compiler_params=_cparams(("parallel",)),
        name="compress",
    )(xk, xv, w1k, w1v, bk, bv, w2k, w2v, kg)


def _gate_column(gate_blk, lane, idx):
    col = jnp.sum(jnp.where(lane == idx, gate_blk, 0.0), axis=1, keepdims=True)
    return jax.nn.sigmoid(col)


def _flash_kernel(qi_tab, ki_tab, fl_tab, *refs, nheads, shared_kv, window, has_delta,
                  has_u, gate_branch, nblk, fixed_shift):
    refs = list(refs)
    base_ref = refs.pop(0) if has_delta else None
    q_ref = refs.pop(0)
    qoff_ref = refs.pop(0) if has_delta else None
    u_ref = refs.pop(0) if has_u else None
    k_ref = refs.pop(0)
    v_ref = refs.pop(0)
    gate_ref = refs.pop(0) if gate_branch is not None else None
    o_ref = refs.pop(0)
    m_scr = refs.pop(0)
    l_scr = refs.pop(0)
    acc_scr = refs.pop(0)
    qcat_scr = refs.pop(0) if has_u else None

    b = pl.program_id(0)
    g = pl.program_id(1)
    step = pl.program_id(2)
    qi = qi_tab[step]
    ki = ki_tab[step]
    fl = fl_tab[step]
    tq = q_ref.shape[2]
    tk = k_ref.shape[2]

    @pl.when((fl & 1) == 1)
    def _init():
        m_scr[...] = jnp.full(m_scr.shape, NEG, F32)
        l_scr[...] = jnp.zeros(l_scr.shape, F32)
        acc_scr[...] = jnp.zeros(acc_scr.shape, F32)
        if has_u:
            for r in range(nheads):
                qcat_scr[r, :, :LANES] = q_ref[0, r]
                qcat_scr[r, :, LANES:] = u_ref[0, 0]

    lane = lax.broadcasted_iota(jnp.int32, (tq, LANES), 1)
    lo = lane < HEAD_DIM

    def tile(diagonal):
        masked = diagonal or window is not None
        if masked:
            row = lax.broadcasted_iota(jnp.int32, (tq, tk), 0)
            col = lax.broadcasted_iota(jnp.int32, (tq, tk), 1)
            if diagonal:
                valid = row >= col
            else:
                valid = (tq + row - col) < window
        for p in range(nheads // 2):
            pvs, alphas = [], []
            for e in range(2):
                hh = 2 * p + e
                if has_u:
                    q = qcat_scr[hh]
                elif has_delta and not diagonal:
                    q = qoff_ref[0, hh]
                else:
                    q = q_ref[0, hh]
                k = k_ref[0, 0] if shared_kv else k_ref[0, hh]
                v = v_ref[0, 0] if shared_kv else v_ref[0, :, p * LANES:(p + 1) * LANES]
                s = lax.dot_general(q, k, (((1,), (1,)), ((), ())), preferred_element_type=F32)
                if masked:
                    s = jnp.where(valid, s, NEG)
                delta = None
                if has_delta and not diagonal:
                    hoff = F_LANE + g * nheads + hh
                    delta = (base_ref[(b * nblk + qi) * LANES + hoff]
                             - base_ref[(b * nblk + ki + 1) * LANES + hoff])
                if fixed_shift:
                    pm = jnp.exp2(s)
                    rowsum = pm[:, :LANES]
                    for j in range(1, tk // LANES):
                        rowsum = rowsum + pm[:, j * LANES:(j + 1) * LANES]
                    pv = jnp.dot(pm.astype(BF16), v, preferred_element_type=F32)
                    if delta is not None:
                        w = jnp.exp2(jnp.full((1, LANES), delta, F32))
                        rowsum = rowsum * w
                        pv = pv * w
                    l_scr[hh] = l_scr[hh] + rowsum
                    pvs.append(pv)
                else:
                    m_prev = m_scr[hh]
                    m_cur = jnp.max(s, axis=1, keepdims=True)
                    if delta is not None:
                        m_cur = m_cur + delta
                    m_new = jnp.maximum(m_prev, m_cur)
                    alpha = jnp.exp2(m_prev - m_new)
                    shift = m_new[:, :1]
                    if delta is not None:
                        shift = shift - delta
                    pm = jnp.exp2(s - shift)
                    l_scr[hh] = alpha * l_scr[hh] + jnp.sum(pm, axis=1, keepdims=True)
                    m_scr[hh] = m_new
                    pvs.append(jnp.dot(pm.astype(BF16), v, preferred_element_type=F32))
                    alphas.append(alpha)
            if fixed_shift:
                acc_scr[p] = acc_scr[p] + jnp.where(lo, pvs[0], pvs[1])
            else:
                acc_scr[p] = (acc_scr[p] * jnp.where(lo, alphas[0], alphas[1])
                              + jnp.where(lo, pvs[0], pvs[1]))

    pl.when(qi == ki)(lambda: tile(True))
    pl.when(qi != ki)(lambda: tile(False))

    @pl.when((fl & 2) == 2)
    def _finish():
        for p in range(nheads // 2):
            if fixed_shift:
                la = jnp.sum(l_scr[2 * p], axis=1, keepdims=True)
                lb = jnp.sum(l_scr[2 * p + 1], axis=1, keepdims=True)
            else:
                la, lb = l_scr[2 * p], l_scr[2 * p + 1]
            inv = jnp.where(lo, 1.0 / la, 1.0 / lb)
            if gate_branch is not None:
                gate_blk = gate_ref[0]
                h0 = g * nheads + 2 * p
                ga = _gate_column(gate_blk, lane, 3 * h0 + gate_branch)
                gb = _gate_column(gate_blk, lane, 3 * (h0 + 1) + gate_branch)
                inv = inv * jnp.where(lo, ga, gb)
            o_ref[0, :, p * LANES:(p + 1) * LANES] = acc_scr[p] * inv


def _tile_tables(nq, window_tiles):
    qi, ki, fl = [], [], []
    for i in range(nq):
        lo = 0 if window_tiles is None else max(0, i - window_tiles)
        for j in range(lo, i + 1):
            qi.append(i)
            ki.append(j)
            fl.append((1 if j == lo else 0) | (2 if j == i else 0))
    return (jnp.asarray(qi, jnp.int32), jnp.asarray(ki, jnp.int32), jnp.asarray(fl, jnp.int32))


def _flash(q, k, v, *, batch, seq, shared_kv, fixed_shift, window=None, base=None, q_off=None,
           u=None, z=None, gate_branch=None):
    nheads = NSA_GROUP_HEADS
    nq = seq // TQ
    ngroups = 8 // nheads
    da = k.shape[-1]
    tabs = _tile_tables(nq, None if window is None else window // TQ)
    nsteps = int(tabs[0].shape[0])
    has_delta = base is not None
    has_u = u is not None
    out_w = LANES * (nheads // 2)

    in_specs, args = [], []
    if has_delta:
        in_specs.append(pl.BlockSpec(memory_space=pltpu.SMEM))
        args.append(base)
    in_specs.append(pl.BlockSpec((1, nheads, TQ, LANES), lambda b, g, s, qt, kt, ft: (b, g, qt[s], 0)))
    args.append(q)
    if has_delta:
        in_specs.append(pl.BlockSpec((1, nheads, TQ, LANES), lambda b, g, s, qt, kt, ft: (b, g, qt[s], 0)))
        args.append(q_off)
    if has_u:
        in_specs.append(pl.BlockSpec((1, 1, TQ, LANES), lambda b, g, s, qt, kt, ft: (b, g, qt[s], 0)))
        args.append(u)
    if shared_kv:
        in_specs.append(pl.BlockSpec((1, 1, TQ, da), lambda b, g, s, qt, kt, ft: (b, g, kt[s], 0)))
        in_specs.append(pl.BlockSpec((1, 1, TQ, LANES), lambda b, g, s, qt, kt, ft: (b, g, kt[s], 0)))
    else:
        in_specs.append(pl.BlockSpec((1, nheads, TQ, da), lambda b, g, s, qt, kt, ft: (b, g, kt[s], 0)))
        in_specs.append(pl.BlockSpec((1, TQ, out_w), lambda b, g, s, qt, kt, ft: (b, kt[s], g)))
    args += [k, v]
    if gate_branch is not None:
        in_specs.append(pl.BlockSpec((1, TQ, LANES),
                                     lambda b, g, s, qt, kt, ft: (b, qt[s], C_SMALL // LANES)))
        args.append(z)

    scratch = [pltpu.VMEM((nheads, TQ, LANES), F32),
               pltpu.VMEM((nheads, TQ, LANES), F32),
               pltpu.VMEM((nheads // 2, TQ, LANES), F32)]
    if has_u:
        scratch.append(pltpu.VMEM((nheads, TQ, 2 * LANES), BF16))

    kern = functools.partial(_flash_kernel, nheads=nheads, shared_kv=shared_kv, window=window,
                             has_delta=has_delta, has_u=has_u, gate_branch=gate_branch, nblk=nq,
                             fixed_shift=fixed_shift)
    return pl.pallas_call(
        kern,
        grid_spec=pltpu.PrefetchScalarGridSpec(
            num_scalar_prefetch=3,
            grid=(batch, ngroups, nsteps),
            in_specs=in_specs,
            out_specs=pl.BlockSpec((1, TQ, out_w), lambda b, g, s, qt, kt, ft: (b, qt[s], g)),
            scratch_shapes=scratch),
        out_shape=jax.ShapeDtypeStruct((batch, seq, 4 * LANES), F32),
        compiler_params=_cparams(("parallel", "parallel", "arbitrary")),
        name="flash_" + ("fox" if has_delta else ("sel" if has_u else "win")),
    )(*tabs, *args)


def _cmp_kernel(q_ref, kc_ref, vc_ref, ov_ref, gate_ref, o_ref, u_ref):
    g = pl.program_id(1)
    qi = pl.program_id(2)
    tq = q_ref.shape[2]
    ncmp = kc_ref.shape[2]
    nslc = ov_ref.shape[1]

    t = qi * tq + lax.broadcasted_iota(jnp.int32, (tq, ncmp), 0)
    n = lax.broadcasted_iota(jnp.int32, (tq, ncmp), 1)
    valid = (n * NSA_CMP_STRIDE + (NSA_CMP_LEN - 1)) <= t
    kc = kc_ref[0, 0]
    vc = vc_ref[0, 0]
    ov = ov_ref[...]
    lane = lax.broadcasted_iota(jnp.int32, (tq, LANES), 1)
    lo = lane < HEAD_DIM
    gate_blk = gate_ref[0]

    imp = jnp.zeros((tq, nslc), F32)
    outs = []
    for r in range(NSA_GROUP_HEADS):
        s = lax.dot_general(q_ref[0, r], kc, (((1,), (1,)), ((), ())), preferred_element_type=F32)
        s = jnp.where(valid, s, NEG)
        m = jnp.max(s, axis=1, keepdims=True)
        e = jnp.where(valid, jnp.exp2(s - m), 0.0)
        d = jnp.sum(e, axis=1, keepdims=True)
        pb = (e / jnp.where(d > 0.0, d, 1.0)).astype(BF16)
        o = jnp.dot(pb, vc, preferred_element_type=F32)
        gcol = _gate_column(gate_blk, lane, 3 * (g * NSA_GROUP_HEADS + r))
        outs.append(o * gcol)
        imp = imp + jnp.dot(pb, ov, preferred_element_type=F32)
    for p in range(NSA_GROUP_HEADS // 2):
        o_ref[0, :, p * LANES:(p + 1) * LANES] = jnp.where(lo, outs[2 * p], outs[2 * p + 1])

    tpos = qi * tq + lax.broadcasted_iota(jnp.int32, (tq, nslc), 0)
    jblk = lax.broadcasted_iota(jnp.int32, (tq, nslc), 1)
    cur = jnp.right_shift(tpos, SEL_SHIFT)
    forced = (jblk == 0) | (jblk == cur) | (jblk == cur - 1)
    score = jnp.where(forced, FORCE_SCORE, imp)
    score = jnp.where(jblk <= cur, score, -jnp.inf)

    sc = score.T
    ji = lax.broadcasted_iota(jnp.int32, (nslc, tq), 0)
    ok_t = ji <= jnp.right_shift(qi * tq + lax.broadcasted_iota(jnp.int32, (nslc, tq), 1), SEL_SHIFT)
    jt = ji.astype(F32)
    for _ in range(NSA_SEL_TOPK):
        mx = jnp.max(sc, axis=0, keepdims=True)
        first = jnp.min(jnp.where(sc == mx, jt, float(nslc)), axis=0, keepdims=True)
        sc = jnp.where(jt == first, -jnp.inf, sc)
    unsel = jnp.where((sc == -jnp.inf) & ok_t, 0.0, 1.0)
    u_ref[0, 0] = unsel.T.astype(BF16)


def _cmp_select(qn, kc, vc, ov, z, *, batch, seq):
    ncmp = kc.shape[2]
    nslc = ov.shape[1]
    return pl.pallas_call(
        _cmp_kernel,
        grid=(batch, NSA_KV_GROUPS, seq // CMP_TQ),
        in_specs=[pl.BlockSpec((1, NSA_GROUP_HEADS, CMP_TQ, LANES), lambda b, g, i: (b, g, i, 0)),
                  pl.BlockSpec((1, 1, ncmp, LANES), lambda b, g, i: (b, g, 0, 0)),
                  pl.BlockSpec((1, 1, ncmp, LANES), lambda b, g, i: (b, g, 0, 0)),
                  pl.BlockSpec((ncmp, nslc), lambda b, g, i: (0, 0)),
                  pl.BlockSpec((1, CMP_TQ, LANES), lambda b, g, i: (b, i, C_SMALL // LANES))],
        out_specs=[pl.BlockSpec((1, CMP_TQ, 2 * LANES), lambda b, g, i: (b, i, g)),
                   pl.BlockSpec((1, 1, CMP_TQ, nslc), lambda b, g, i: (b, g, i, 0))],
        out_shape=[jax.ShapeDtypeStruct((batch, seq, NSA_WIDTH), F32),
                   jax.ShapeDtypeStruct((batch, NSA_KV_GROUPS, seq, nslc), BF16)],
        compiler_params=_cparams(("parallel", "parallel", "parallel")),
        name="cmp_select",
    )(qn, kc, vc, ov, z)


def _head_rms(x, g):
    return x * lax.rsqrt(jnp.mean(x * x, axis=-1, keepdims=True) + EPS) * g


def _mem_kernel(zq_ref, kv_ref, qg_ref, kg_ref, o_ref, kn_scr, vb_scr):
    @pl.when(pl.program_id(1) == 0)
    def _prep():
        for h in range(MEM_HEADS):
            kh = kv_ref[0, :, h * MEM_HEAD_DIM:(h + 1) * MEM_HEAD_DIM]
            kn_scr[h] = _head_rms(kh, kg_ref[...]).astype(BF16)
            vb_scr[h] = kv_ref[0, :, MEM_WIDTH + h * MEM_HEAD_DIM:
                               MEM_WIDTH + (h + 1) * MEM_HEAD_DIM].astype(BF16)

    for h in range(MEM_HEADS):
        sl = slice(h * MEM_HEAD_DIM, (h + 1) * MEM_HEAD_DIM)
        qh = (_head_rms(zq_ref[0, :, sl], qg_ref[...]) * ((MEM_HEAD_DIM ** -0.5) * LOG2E)).astype(BF16)
        s = lax.dot_general(qh, kn_scr[h], (((1,), (1,)), ((), ())), preferred_element_type=F32)
        m = jnp.max(s, axis=1, keepdims=True)
        e = jnp.exp2(s - m)
        p = (e / jnp.sum(e, axis=1, keepdims=True)).astype(BF16)
        o_ref[0, :, sl] = jnp.dot(p, vb_scr[h], preferred_element_type=F32)


def _mem_attention(z, mem_kv, q_g, k_g, *, batch, seq):
    mlen = mem_kv.shape[1]
    tq = TQ
    return pl.pallas_call(
        _mem_kernel,
        grid=(batch, seq // tq),
        in_specs=[pl.BlockSpec((1, tq, MEM_WIDTH), lambda b, i: (b, i, C_MEM_Q // MEM_WIDTH)),
                  pl.BlockSpec((1, mlen, 2 * MEM_WIDTH), lambda b, i: (b, 0, 0)),
                  pl.BlockSpec((1, MEM_HEAD_DIM), lambda b, i: (0, 0)),
                  pl.BlockSpec((1, MEM_HEAD_DIM), lambda b, i: (0, 0))],
        out_specs=pl.BlockSpec((1, tq, MEM_WIDTH), lambda b, i: (b, i, 0)),
        out_shape=jax.ShapeDtypeStruct((batch, seq, MEM_WIDTH), F32),
        scratch_shapes=[pltpu.VMEM((MEM_HEADS, mlen, MEM_HEAD_DIM), BF16),
                        pltpu.VMEM((MEM_HEADS, mlen, MEM_HEAD_DIM), BF16)],
        compiler_params=_cparams(("parallel", "arbitrary")),
        name="mem_attention",
    )(z, mem_kv, q_g.reshape(1, MEM_HEAD_DIM), k_g.reshape(1, MEM_HEAD_DIM))


def _out_kernel(x_ref, oc_ref, os_ref, ow_ref, ob_ref, om_ref, sa_ref, sb_ref, sm_ref,
                g0_ref, g1_ref, g2_ref, wa_ref, wb_ref, wm_ref, wo_ref, y_ref):
    oa = (oc_ref[...] + os_ref[...] + ow_ref[...]) * jax.nn.silu(sa_ref[...])
    ob = ob_ref[...] * jax.nn.silu(sb_ref[...])
    om = om_ref[...] * jax.nn.silu(sm_ref[...])
    u = (jax.nn.sigmoid(g0_ref[...]) * jnp.dot(oa.astype(BF16), wa_ref[...], preferred_element_type=F32)
         + jax.nn.sigmoid(g1_ref[...]) * jnp.dot(ob.astype(BF16), wb_ref[...], preferred_element_type=F32)
         + jax.nn.sigmoid(g2_ref[...]) * jnp.dot(om.astype(BF16), wm_ref[...], preferred_element_type=F32))
    y_ref[...] = x_ref[...] + jnp.dot(u.astype(BF16), wo_ref[...], preferred_element_type=F32)


def _out_proj(x2d, o_c, o_s, o_w, o_b, o_m, z2d, wa, wb, wm, wo):
    m = x2d.shape[0]
    tm = 256
    w512 = 512
    row512 = lambda c: pl.BlockSpec((tm, w512), lambda i, c=c: (i, c))
    row1024 = lambda c: pl.BlockSpec((tm, D_MODEL), lambda i, c=c: (i, c))
    full = lambda shape: pl.BlockSpec(shape, lambda i: (0, 0))
    return pl.pallas_call(
        _out_kernel,
        grid=(m // tm,),
        in_specs=[row1024(0), row512(0), row512(0), row512(0), row512(0), row512(0),
                  row512(C_NSA_SILU // w512), row512(C_FOX_SILU // w512), row512(C_MEM_SILU // w512),
                  row1024(C_MERGE // D_MODEL), row1024(C_MERGE // D_MODEL + 1),
                  row1024(C_MERGE // D_MODEL + 2),
                  full((w512, D_MODEL)), full((w512, D_MODEL)), full((w512, D_MODEL)),
                  full((D_MODEL, D_MODEL))],
        out_specs=row1024(0),
        out_shape=jax.ShapeDtypeStruct((m, D_MODEL), F32),
        compiler_params=_cparams(("parallel",)),
        name="out_proj",
    )(x2d, o_c, o_s, o_w, o_b, o_m, z2d, z2d, z2d, z2d, z2d, z2d, wa, wb, wm, wo)


def _permute_w_in(w_in):
    o = np.cumsum([0, NSA_WIDTH, 6 * NSA_KV_GROUPS * HEAD_DIM, 3 * NSA_HEADS, NSA_WIDTH,
                   3 * FOX_WIDTH, FOX_HEADS, FOX_WIDTH, MEM_WIDTH, MEM_WIDTH, 3 * D_MODEL])
    nsa_q, nsa_kv, nsa_gate, nsa_silu, fox_qkv, fox_f, fox_silu, mem_q, mem_silu, merge = [
        w_in[..., o[i]:o[i + 1]] for i in range(10)]
    pad = jnp.zeros(w_in.shape[:2] + (LANES - 3 * NSA_HEADS - FOX_HEADS,), w_in.dtype)
    return jnp.concatenate([nsa_q, nsa_silu, fox_qkv, fox_silu, mem_q, mem_silu, merge, nsa_kv,
                            nsa_gate, fox_f, pad], axis=-1).astype(BF16)


def _pad_lanes(x, width=LANES):
    return jnp.pad(x, [(0, 0)] * (x.ndim - 1) + [(0, width - x.shape[-1])])


def _tile2(g):
    return jnp.concatenate([g, g], axis=-1)[:, None, :]


def _overlap(n_cmp_pad, n_slc):
    cs = np.arange(n_cmp_pad)[:, None] * NSA_CMP_STRIDE
    ss = np.arange(NSLC_PAD)[None, :] * NSA_SEL_LEN
    ov = np.clip(np.minimum(cs + NSA_CMP_LEN, ss + NSA_SEL_LEN) - np.maximum(cs, ss), 0, None)
    ov = ov / NSA_CMP_LEN
    ov[-1] = 0.0
    ov[:, n_slc:] = 0.0
    return jnp.asarray(ov, BF16)


def _compress_w1(w1):
    nl = w1.shape[0]
    halves = w1.reshape(nl, 2, NSA_CMP_STRIDE, 1, HEAD_DIM, NSA_CMP_HIDDEN)
    eye = jnp.eye(NSA_KV_GROUPS, dtype=w1.dtype)
    out = jnp.einsum("pg,zhldn->zlpdghn", eye, halves[:, :, :, 0])
    return out.reshape(nl, NSA_CMP_STRIDE * NSA_KV_GROUPS * HEAD_DIM,
                       NSA_KV_GROUPS * 2 * NSA_CMP_HIDDEN).astype(BF16)


def _layer(x, mem, p, consts):
    batch, seq, _ = x.shape
    m_rows = batch * seq
    cos4, sin_signed, tri, ov = consts
    z2d = _norm_matmul(x.reshape(m_rows, D_MODEL), p["norm_g"], p["w_in"], 512, Z_WIDTH // 3)
    z = z2d.reshape(batch, seq, Z_WIDTH)

    (qn, kcr, vcr, ks, vs, kw, vw, fqd, fqo, fka, fv, base) = _prep(
        z, cos4, sin_signed, tri, p["gq"], p["gks"], p["gkw"], p["gfq"], p["gfk"], p["fb"],
        p["s_sel"], p["s_win"], p["s_fox"], batch=batch, seq=seq)

    nchunk = seq // NSA_CMP_STRIDE
    kc, vc = _compress(kcr.reshape(batch, nchunk, NSA_CMP_STRIDE * LANES),
                       vcr.reshape(batch, nchunk, NSA_CMP_STRIDE * LANES),
                       p["w1k"], p["w1v"], p["bk"], p["bv"], p["w2k"], p["w2v"], p["gkc"], batch=batch)

    o_c, unsel = _cmp_select(qn, kc, vc, ov, z, batch=batch, seq=seq)
    def attend(fixed_shift):
        o_s = _flash(qn, ks, vs, batch=batch, seq=seq, shared_kv=True, fixed_shift=fixed_shift,
                     u=unsel, z=z, gate_branch=1)
        o_w = _flash(qn, kw, vw, batch=batch, seq=seq, shared_kv=True, fixed_shift=fixed_shift,
                     window=NSA_WINDOW, z=z, gate_branch=2)
        o_b = _flash(fqd, fka, fv, batch=batch, seq=seq, shared_kv=False, fixed_shift=fixed_shift,
                     base=base.reshape(-1), q_off=fqo)
        return o_s, o_w, o_b

    o_s, o_w, o_b = lax.cond(p["bound_ok"], lambda: attend(True), lambda: attend(False))

    mlen = mem.shape[1]
    mem_kv = _norm_matmul(mem.reshape(batch * mlen, D_MODEL), p["mem_norm_g"], p["w_mem_kv"],
                          min(512, batch * mlen), 2 * MEM_WIDTH).reshape(batch, mlen, 2 * MEM_WIDTH)
    o_m = _mem_attention(z, mem_kv, p["mem_q_norm"], p["mem_k_norm"], batch=batch, seq=seq)

    y = _out_proj(x.reshape(m_rows, D_MODEL), o_c.reshape(m_rows, -1), o_s.reshape(m_rows, -1),
                  o_w.reshape(m_rows, -1), o_b.reshape(m_rows, -1), o_m.reshape(m_rows, -1), z2d,
                  p["w_branch_a"], p["w_branch_b"], p["w_branch_m"], p["w_out"])
    return y.reshape(batch, seq, D_MODEL)


def kernel(x, mem, norm_g, mem_norm_g, w_in, nsa_q_norm, nsa_k_norm, cmp_pe_k, cmp_w1_k, cmp_w2_k,
           cmp_pe_v, cmp_w1_v, cmp_w2_v, fox_q_norm, fox_k_norm, fox_f_bias, mem_q_norm, mem_k_norm,
           w_mem_kv, w_branch_a, w_branch_b, w_branch_m, w_out):
    batch, seq, _ = x.shape
    depth = w_in.shape[0]
    n_slc = seq // NSA_SEL_LEN
    assert n_slc <= NSLC_PAD and seq % TQ == 0 and NSA_WINDOW == TQ

    half = HEAD_DIM // 2
    inv_freq = ROPE_THETA ** (-jnp.arange(half, dtype=F32) / half)
    ang = jnp.arange(seq).astype(F32)[:, None] * inv_freq[None, :]
    cos, sin = jnp.cos(ang), jnp.sin(ang)
    cos4 = jnp.concatenate([cos, cos, cos, cos], axis=-1)
    sin_signed = jnp.concatenate([-sin, sin, -sin, sin], axis=-1)
    tri = jnp.asarray(np.tril(np.ones((TQ, TQ), np.float32)), BF16)
    consts = (cos4, sin_signed, tri, _overlap(seq // NSA_CMP_STRIDE, n_slc))

    hp = lax.Precision.HIGHEST
    pe_bias = lambda pe, w1: jnp.einsum("lk,lkn->ln", pe.reshape(depth, -1), w1, precision=hp)[:, None, :]
    fb = jnp.zeros((depth, 1, LANES), F32).at[:, 0, F_LANE:F_LANE + FOX_HEADS].set(fox_f_bias)

    def logit_bound(gq, gk):
        bound = (BOUND_MARGIN * HEAD_DIM * QK_SCALE
                 * jnp.max(jnp.abs(gq), axis=-1) * jnp.max(jnp.abs(gk), axis=-1))
        return bound.astype(BF16).astype(F32)

    bounds = [logit_bound(nsa_q_norm, nsa_k_norm[:, 1]), logit_bound(nsa_q_norm, nsa_k_norm[:, 2]),
              logit_bound(fox_q_norm, fox_k_norm)]
    bound_ok = jnp.maximum(jnp.maximum(bounds[0], bounds[1]), bounds[2]) <= MAX_FIXED_BOUND
    bound_row = lambda s: jnp.broadcast_to(s[:, None, None], (depth, 1, LANES))
    stacked = dict(
        s_sel=bound_row(bounds[0]), s_win=bound_row(bounds[1]), s_fox=bound_row(bounds[2]),
        bound_ok=bound_ok,
        norm_g=norm_g, mem_norm_g=mem_norm_g, w_in=_permute_w_in(w_in),
        gq=_tile2(nsa_q_norm), gks=_tile2(nsa_k_norm[:, 1]), gkw=_tile2(nsa_k_norm[:, 2]),
        gkc=_pad_lanes(nsa_k_norm[:, 0])[:, None, :], gfq=_tile2(fox_q_norm), gfk=_tile2(fox_k_norm),
        fb=fb, w1k=_compress_w1(cmp_w1_k), w1v=_compress_w1(cmp_w1_v),
        bk=pe_bias(cmp_pe_k, cmp_w1_k), bv=pe_bias(cmp_pe_v, cmp_w1_v),
        w2k=_pad_lanes(cmp_w2_k).astype(BF16),
        w2v=jnp.concatenate([cmp_w2_v, cmp_w2_v], axis=-1).astype(BF16),
        mem_q_norm=mem_q_norm, mem_k_norm=mem_k_norm, w_mem_kv=w_mem_kv.astype(BF16),
        w_branch_a=w_branch_a.astype(BF16), w_branch_b=w_branch_b.astype(BF16),
        w_branch_m=w_branch_m.astype(BF16), w_out=w_out.astype(BF16))
    for l in range(depth):
        x = _layer(x, mem, {k: v[l] for k, v in stacked.items()}, consts)
    return x
```

```python
import functools

import numpy as np
import jax
import jax.numpy as jnp
from jax import lax
from jax.experimental import pallas as pl
from jax.experimental.pallas import tpu as pltpu

F32 = jnp.float32
BF16 = jnp.bfloat16

D_MODEL = 1024
HEAD_DIM = 64
ROPE_THETA = 10000.0
EPS = 1e-6
NSA_HEADS = 8
NSA_KV_GROUPS = 2
NSA_GROUP_HEADS = NSA_HEADS // NSA_KV_GROUPS
NSA_CMP_LEN = 32
NSA_CMP_STRIDE = 16
NSA_CMP_HIDDEN = 128
NSA_SEL_LEN = 64
NSA_SEL_TOPK = 16
NSA_WINDOW = 512
FORCE_SCORE = 1e9
NSA_WIDTH = NSA_HEADS * HEAD_DIM
FOX_HEADS = 8
FOX_WIDTH = FOX_HEADS * HEAD_DIM
MEM_HEADS = 4
MEM_HEAD_DIM = 128
MEM_WIDTH = MEM_HEADS * MEM_HEAD_DIM

LANES = 128
VMEM_LIMIT = 48 * 1024 * 1024
NEG = -1e30
SEL_MASK = 32768.0
SEL_SHIFT = 6
NSLC_PAD = LANES
LOG2E = 1.4426950408889634
QK_SCALE = (HEAD_DIM ** -0.5) * LOG2E
BOUND_MARGIN = 1.02
MAX_FIXED_BOUND = 60.0

C_NSA_Q = 0
C_NSA_SILU = 512
C_FOX_Q = 1024
C_FOX_K = 1536
C_FOX_V = 2048
C_FOX_SILU = 2560
C_MEM_Q = 3072
C_MEM_SILU = 3584
C_MERGE = 4096
C_NSA_KV = 7168
C_SMALL = 7936
Z_WIDTH = 8064
F_LANE = 3 * NSA_HEADS
AUG = HEAD_DIM

TQ = 512
CMP_TQ = 256


def _cparams(sem):
    return pltpu.CompilerParams(dimension_semantics=sem, vmem_limit_bytes=VMEM_LIMIT)


def _norm_matmul_kernel(x_ref, g_ref, w_ref, o_ref):
    x = x_ref[...]
    ms = jnp.mean(x * x, axis=-1, keepdims=True)
    h = (x * lax.rsqrt(ms + EPS) * g_ref[...]).astype(BF16)
    o_ref[...] = jnp.dot(h, w_ref[...], preferred_element_type=F32)


def _norm_matmul(x2d, g, w, tm, tn):
    m, k = x2d.shape
    n = w.shape[1]
    return pl.pallas_call(
        _norm_matmul_kernel,
        grid=(n // tn, m // tm),
        in_specs=[pl.BlockSpec((tm, k), lambda j, i: (i, 0)),
                  pl.BlockSpec((1, k), lambda j, i: (0, 0)),
                  pl.BlockSpec((k, tn), lambda j, i: (0, j))],
        out_specs=pl.BlockSpec((tm, tn), lambda j, i: (i, j)),
        out_shape=jax.ShapeDtypeStruct((m, n), F32),
        compiler_params=_cparams(("parallel", "parallel")),
        name="norm_matmul",
    )(x2d, g.reshape(1, k), w)


def _pair_rms(x, g2, lo):
    sq = x * x
    s_lo = jnp.sum(jnp.where(lo, sq, 0.0), axis=1, keepdims=True)
    s_hi = jnp.sum(jnp.where(lo, 0.0, sq), axis=1, keepdims=True)
    ms = jnp.where(lo, s_lo, s_hi) * (1.0 / HEAD_DIM)
    return x * lax.rsqrt(ms + EPS) * g2


def _pair_rope(y, cos4, sin_signed, first_half):
    rot = jnp.where(first_half, pltpu.roll(y, LANES - HEAD_DIM // 2, 1), pltpu.roll(y, HEAD_DIM // 2, 1))
    return y * cos4 + rot * sin_signed


def _head_a(y, lo, tail=0.0):
    return jnp.where(lo, y, tail)


def _head_b(y, lo, tail=0.0):
    return jnp.where(lo, pltpu.roll(y, HEAD_DIM, 1), tail)


def _split3_f32(c):
    hi = c.astype(BF16).astype(F32)
    r = c - hi
    mid = r.astype(BF16).astype(F32)
    return hi, mid, r - mid


def _prep_kernel(zq_ref, zc_ref, zs_ref, zw_ref, fq_ref, fk_ref, fv_ref, sm_ref, cos_ref, sin_ref,
                 tri_ref, gq_ref, gks_ref, gkw_ref, gfq_ref, gfk_ref, fb_ref, ss_ref, sw_ref, sf_ref,
                 qn_ref, kcr_ref, vcr_ref, ks_ref, vs_ref, kw_ref, vw_ref, fqd_ref, fqo_ref, fka_ref,
                 fvb_ref, base_ref, run_scr):
    ti = pl.program_id(1)
    tt = zq_ref.shape[1]
    lane = lax.broadcasted_iota(jnp.int32, (tt, LANES), 1)
    lo = lane < HEAD_DIM
    first_half = (lane & (HEAD_DIM // 2)) == 0
    cos4 = cos_ref[...]
    sin_signed = sin_ref[...]
    one_at_aug = jnp.where(lane == AUG, 1.0, 0.0)

    for p in range(NSA_HEADS // 2):
        y = _pair_rope(_pair_rms(zq_ref[0, :, p * LANES:(p + 1) * LANES], gq_ref[...], lo),
                       cos4, sin_signed, first_half) * QK_SCALE
        qn_ref[0, 2 * p] = _head_a(y, lo, one_at_aug).astype(BF16)
        qn_ref[0, 2 * p + 1] = _head_b(y, lo, one_at_aug).astype(BF16)

    kcr_ref[0] = _pair_rope(zc_ref[0, :, :LANES], cos4, sin_signed, first_half).astype(BF16)
    vcr_ref[0] = zc_ref[0, :, LANES:].astype(BF16)

    sel_lane = jnp.right_shift(ti * tt + lax.broadcasted_iota(jnp.int32, (tt, LANES), 0), SEL_SHIFT)
    onehot = jnp.where(lane == sel_lane, -SEL_MASK, 0.0).astype(BF16)
    for z_ref, g_ref, k_ref, v_ref, bound_ref, with_onehot in (
            (zs_ref, gks_ref, ks_ref, vs_ref, ss_ref, True),
            (zw_ref, gkw_ref, kw_ref, vw_ref, sw_ref, False)):
        y = _pair_rope(_pair_rms(z_ref[0, :, :LANES], g_ref[...], lo), cos4, sin_signed, first_half)
        v = z_ref[0, :, LANES:]
        vr = pltpu.roll(v, HEAD_DIM, 1)
        neg_bound = jnp.where(lane == AUG, -bound_ref[...], 0.0)
        k_ref[0, 0, :, :LANES] = _head_a(y, lo, neg_bound).astype(BF16)
        k_ref[0, 1, :, :LANES] = _head_b(y, lo, neg_bound).astype(BF16)
        if with_onehot:
            k_ref[0, 0, :, LANES:] = onehot
            k_ref[0, 1, :, LANES:] = onehot
        v_ref[0, 0] = jnp.where(lo, v, vr).astype(BF16)
        v_ref[0, 1] = jnp.where(lo, vr, v).astype(BF16)

    @pl.when(ti == 0)
    def _reset():
        run_scr[...] = jnp.zeros(run_scr.shape, F32)

    xf = sm_ref[0] + fb_ref[...]
    log_f = (jnp.minimum(xf, 0.0) - jnp.log1p(jnp.exp(-jnp.abs(xf)))) * LOG2E
    tri = tri_ref[...]
    parts = _split3_f32(log_f)
    cum = sum(jnp.dot(tri, part.astype(BF16), preferred_element_type=F32) for part in parts)
    base_ref[0, 0] = run_scr[...]
    run_scr[...] = run_scr[...] + cum[tt - 1:tt, :]
    fvb_ref[0] = fv_ref[0].astype(BF16)

    def three(start, parts, rest):
        out = rest
        for i, part in enumerate(parts):
            out = jnp.where(lane == start + i, part, out)
        return out

    span = lambda a, b: jnp.where((lane >= a) & (lane < b), 1.0, 0.0)
    q_tail_diag = span(AUG + 3, AUG + 6) + span(AUG + 9, AUG + 10)
    q_tail_off = span(AUG + 6, AUG + 10)
    k_tail = span(AUG, AUG + 3) + jnp.where(lane == AUG + 9, -sf_ref[...], 0.0)
    for p in range(FOX_HEADS // 2):
        sl = slice(p * LANES, (p + 1) * LANES)
        yq = _pair_rms(fq_ref[0, :, sl], gfq_ref[...], lo) * QK_SCALE
        yk = _pair_rms(fk_ref[0, :, sl], gfk_ref[...], lo)
        for e, split in ((0, _head_a), (1, _head_b)):
            h = 2 * p + e
            col = jnp.sum(jnp.where(lane == F_LANE + h, cum, 0.0), axis=1, keepdims=True)
            a_start = _split3_f32(col)
            a_end = _split3_f32(col - col[tt - 1:tt, :])
            qh = split(yq, lo)
            fqd_ref[0, h] = (qh + three(AUG, a_start, q_tail_diag)).astype(BF16)
            fqo_ref[0, h] = (qh + three(AUG, a_start, q_tail_off)).astype(BF16)
            aug_k = three(AUG + 3, [-a for a in a_start], three(AUG + 6, [-a for a in a_end], k_tail))
            fka_ref[0, h] = (split(yk, lo) + aug_k).astype(BF16)


def _prep(z, cos4, sin_signed, tri, gq, gks, gkw, gfq, gfk, fb, s_sel, s_win, s_fox, *, batch, seq):
    nblk = seq // TQ
    zspec = lambda w, c: pl.BlockSpec((1, TQ, w), lambda b, i, c=c: (b, i, c))
    row = pl.BlockSpec((1, LANES), lambda b, i: (0, 0))
    tab = pl.BlockSpec((TQ, LANES), lambda b, i: (i, 0))
    heads = lambda n, w: pl.BlockSpec((1, n, TQ, w), lambda b, i: (b, 0, i, 0))
    tok = lambda w: pl.BlockSpec((1, TQ, w), lambda b, i: (b, i, 0))
    bf = lambda *shape: jax.ShapeDtypeStruct(shape, BF16)
    return pl.pallas_call(
        _prep_kernel,
        grid=(batch, nblk),
        in_specs=[zspec(NSA_WIDTH, C_NSA_Q // NSA_WIDTH),
                  zspec(2 * LANES, C_NSA_KV // (2 * LANES)),
                  zspec(2 * LANES, C_NSA_KV // (2 * LANES) + 1),
                  zspec(2 * LANES, C_NSA_KV // (2 * LANES) + 2),
                  zspec(FOX_WIDTH, C_FOX_Q // FOX_WIDTH),
                  zspec(FOX_WIDTH, C_FOX_K // FOX_WIDTH),
                  zspec(FOX_WIDTH, C_FOX_V // FOX_WIDTH),
                  zspec(LANES, C_SMALL // LANES),
                  tab, tab, pl.BlockSpec((TQ, TQ), lambda b, i: (0, 0)),
                  row, row, row, row, row, row, row, row, row],
        out_specs=[heads(NSA_HEADS, LANES), tok(LANES), tok(LANES),
                   heads(NSA_KV_GROUPS, 2 * LANES), heads(NSA_KV_GROUPS, LANES),
                   heads(NSA_KV_GROUPS, LANES), heads(NSA_KV_GROUPS, LANES),
                   heads(FOX_HEADS, LANES), heads(FOX_HEADS, LANES), heads(FOX_HEADS, LANES),
                   tok(FOX_WIDTH), pl.BlockSpec((1, 1, 1, LANES), lambda b, i: (b, i, 0, 0))],
        out_shape=[bf(batch, NSA_HEADS, seq, LANES), bf(batch, seq, LANES), bf(batch, seq, LANES),
                   bf(batch, NSA_KV_GROUPS, seq, 2 * LANES), bf(batch, NSA_KV_GROUPS, seq, LANES),
                   bf(batch, NSA_KV_GROUPS, seq, LANES), bf(batch, NSA_KV_GROUPS, seq, LANES),
                   bf(batch, FOX_HEADS, seq, LANES), bf(batch, FOX_HEADS, seq, LANES),
                   bf(batch, FOX_HEADS, seq, LANES), bf(batch, seq, FOX_WIDTH),
                   jax.ShapeDtypeStruct((batch, nblk, 1, LANES), F32)],
        scratch_shapes=[pltpu.VMEM((1, LANES), F32)],
        compiler_params=_cparams(("parallel", "arbitrary")),
        name="prep",
    )(z, z, z, z, z, z, z, z, cos4, sin_signed, tri, gq, gks, gkw, gfq, gfk, fb, s_sel, s_win, s_fox)


def _compress_kernel(xk_ref, xv_ref, w1k_ref, w1v_ref, bk_ref, bv_ref, w2k_ref, w2v_ref, kg_ref,
                     sc_ref, kc_ref, vc_ref):
    nchunk = xk_ref.shape[1]
    lane = lax.broadcasted_iota(jnp.int32, (nchunk, LANES), 1)
    for x_ref, w1_ref, b_ref, w2_ref, o_ref, is_key in ((xk_ref, w1k_ref, bk_ref, w2k_ref, kc_ref, True),
                                                        (xv_ref, w1v_ref, bv_ref, w2v_ref, vc_ref, False)):
        h = jnp.dot(x_ref[0], w1_ref[...], preferred_element_type=F32)
        for g in range(NSA_KV_GROUPS):
            c0 = g * 2 * NSA_CMP_HIDDEN
            top = h[:, c0:c0 + NSA_CMP_HIDDEN]
            bot = h[:, c0 + NSA_CMP_HIDDEN:c0 + 2 * NSA_CMP_HIDDEN]
            hid = top + pltpu.roll(bot, nchunk - 1, 0) + b_ref[...]
            act = (hid * jax.nn.sigmoid(hid)).astype(BF16)
            o = jnp.dot(act, w2_ref[...], preferred_element_type=F32)
            if is_key:
                ms = jnp.sum(o * o, axis=1, keepdims=True) * (1.0 / HEAD_DIM)
                o = o * lax.rsqrt(ms + EPS) * kg_ref[...]
                o = jnp.where(lane == AUG, -sc_ref[...], o)
            o_ref[0, g] = o.astype(BF16)


def _compress(xk, xv, w1k, w1v, bk, bv, w2k, w2v, kg, s_cmp, *, batch):
    nchunk, kin = xk.shape[1], xk.shape[2]
    xspec = pl.BlockSpec((1, nchunk, kin), lambda b: (b, 0, 0))
    full = lambda a: pl.BlockSpec(a.shape, lambda b: (0,) * a.ndim)
    ospec = pl.BlockSpec((1, NSA_KV_GROUPS, nchunk, LANES), lambda b: (b, 0, 0, 0))
    oshape = jax.ShapeDtypeStruct((batch, NSA_KV_GROUPS, nchunk, LANES), BF16)
    return pl.pallas_call(
        _compress_kernel,
        grid=(batch,),
        in_specs=[xspec, xspec, full(w1k), full(w1v), full(bk), full(bv), full(w2k), full(w2v), full(kg),
                  full(s_cmp)],
        out_specs=[ospec, ospec],
        out_shape=[oshape, oshape],
        compiler_params=_cparams(("parallel",)),
        name="compress",
    )(xk, xv, w1k, w1v, bk, bv, w2k, w2v, kg, s_cmp)


def _gate_column(gate_blk, lane, idx):
    col = jnp.sum(jnp.where(lane == idx, gate_blk, 0.0), axis=1, keepdims=True)
    return jax.nn.sigmoid(col)


def _flash_kernel(qi_tab, ki_tab, fl_tab, *refs, nheads, shared_kv, window, has_delta,
                  has_u, gate_branch, nblk, fixed_shift):
    refs = list(refs)
    base_ref = refs.pop(0) if has_delta else None
    q_ref = refs.pop(0)
    qoff_ref = refs.pop(0) if has_delta else None
    u_ref = refs.pop(0) if has_u else None
    k_ref = refs.pop(0)
    v_ref = refs.pop(0)
    gate_ref = refs.pop(0) if gate_branch is not None else None
    o_ref = refs.pop(0)
    m_scr = refs.pop(0)
    l_scr = refs.pop(0)
    acc_scr = refs.pop(0)
    qcat_scr = refs.pop(0) if has_u else None

    b = pl.program_id(0)
    g = pl.program_id(1)
    step = pl.program_id(2)
    qi = qi_tab[step]
    ki = ki_tab[step]
    fl = fl_tab[step]
    tq = q_ref.shape[2]
    tk = k_ref.shape[2]

    @pl.when((fl & 1) == 1)
    def _init():
        m_scr[...] = jnp.full(m_scr.shape, NEG, F32)
        l_scr[...] = jnp.zeros(l_scr.shape, F32)
        acc_scr[...] = jnp.zeros(acc_scr.shape, F32)
        if has_u:
            for r in range(nheads):
                qcat_scr[r, :, :LANES] = q_ref[0, r]
                qcat_scr[r, :, LANES:] = u_ref[0, r // NSA_GROUP_HEADS]

    lane = lax.broadcasted_iota(jnp.int32, (tq, LANES), 1)
    lo = lane < HEAD_DIM

    def tile(diagonal):
        masked = diagonal or window is not None
        if masked:
            row = lax.broadcasted_iota(jnp.int32, (tq, tk), 0)
            col = lax.broadcasted_iota(jnp.int32, (tq, tk), 1)
            if diagonal:
                valid = row >= col
            else:
                valid = (tq + row - col) < window
        for p in range(nheads // 2):
            pvs, alphas = [], []
            for e in range(2):
                hh = 2 * p + e
                if has_u:
                    q = qcat_scr[hh]
                elif has_delta and not diagonal:
                    q = qoff_ref[0, hh]
                else:
                    q = q_ref[0, hh]
                k = k_ref[0, hh // NSA_GROUP_HEADS] if shared_kv else k_ref[0, hh]
                v = v_ref[0, hh // NSA_GROUP_HEADS] if shared_kv else v_ref[0, :, p * LANES:(p + 1) * LANES]
                s = lax.dot_general(q, k, (((1,), (1,)), ((), ())), preferred_element_type=F32)
                if masked:
                    s = jnp.where(valid, s, NEG)
                delta = None
                if has_delta and not diagonal:
                    hoff = F_LANE + g * nheads + hh
                    delta = (base_ref[(b * nblk + qi) * LANES + hoff]
                             - base_ref[(b * nblk + ki + 1) * LANES + hoff])
                if fixed_shift:
                    pm = jnp.exp2(s)
                    rowsum = pm[:, :LANES]
                    for j in range(1, tk // LANES):
                        rowsum = rowsum + pm[:, j * LANES:(j + 1) * LANES]
                    pv = jnp.dot(pm.astype(BF16), v, preferred_element_type=F32)
                    if delta is not None:
                        w = jnp.exp2(jnp.full((1, LANES), delta, F32))
                        rowsum = rowsum * w
                        pv = pv * w
                    l_scr[hh] = l_scr[hh] + rowsum
                    pvs.append(pv)
                else:
                    m_prev = m_scr[hh]
                    m_cur = jnp.max(s, axis=1, keepdims=True)
                    if delta is not None:
                        m_cur = m_cur + delta
                    m_new = jnp.maximum(m_prev, m_cur)
                    alpha = jnp.exp2(m_prev - m_new)
                    shift = m_new[:, :1]
                    if delta is not None:
                        shift = shift - delta
                    pm = jnp.exp2(s - shift)
                    l_scr[hh] = alpha * l_scr[hh] + jnp.sum(pm, axis=1, keepdims=True)
                    m_scr[hh] = m_new
                    pvs.append(jnp.dot(pm.astype(BF16), v, preferred_element_type=F32))
                    alphas.append(alpha)
            if fixed_shift:
                acc_scr[p] = acc_scr[p] + jnp.where(lo, pvs[0], pvs[1])
            else:
                acc_scr[p] = (acc_scr[p] * jnp.where(lo, alphas[0], alphas[1])
                              + jnp.where(lo, pvs[0], pvs[1]))

    pl.when(qi == ki)(lambda: tile(True))
    pl.when(qi != ki)(lambda: tile(False))

    @pl.when((fl & 2) == 2)
    def _finish():
        for p in range(nheads // 2):
            if fixed_shift:
                la = jnp.sum(l_scr[2 * p], axis=1, keepdims=True)
                lb = jnp.sum(l_scr[2 * p + 1], axis=1, keepdims=True)
            else:
                la, lb = l_scr[2 * p], l_scr[2 * p + 1]
            inv = jnp.where(lo, 1.0 / la, 1.0 / lb)
            if gate_branch is not None:
                gate_blk = gate_ref[0]
                h0 = g * nheads + 2 * p
                ga = _gate_column(gate_blk, lane, 3 * h0 + gate_branch)
                gb = _gate_column(gate_blk, lane, 3 * (h0 + 1) + gate_branch)
                inv = inv * jnp.where(lo, ga, gb)
            o_ref[0, :, p * LANES:(p + 1) * LANES] = (acc_scr[p] * inv).astype(o_ref.dtype)


def _tile_tables(nq, window_tiles):
    qi, ki, fl = [], [], []
    for i in range(nq):
        lo = 0 if window_tiles is None else max(0, i - window_tiles)
        for j in range(lo, i + 1):
            qi.append(i)
            ki.append(j)
            fl.append((1 if j == lo else 0) | (2 if j == i else 0))
    return (jnp.asarray(qi, jnp.int32), jnp.asarray(ki, jnp.int32), jnp.asarray(fl, jnp.int32))


def _flash(q, k, v, *, batch, seq, shared_kv, fixed_shift, window=None, base=None, q_off=None,
           u=None, z=None, gate_branch=None):
    nheads = NSA_HEADS
    nq = seq // TQ
    ngroups = NSA_HEADS // nheads
    nkv = NSA_KV_GROUPS
    da = k.shape[-1]
    tabs = _tile_tables(nq, None if window is None else window // TQ)
    nsteps = int(tabs[0].shape[0])
    has_delta = base is not None
    has_u = u is not None
    out_w = LANES * (nheads // 2)

    in_specs, args = [], []
    if has_delta:
        in_specs.append(pl.BlockSpec(memory_space=pltpu.SMEM))
        args.append(base)
    in_specs.append(pl.BlockSpec((1, nheads, TQ, LANES), lambda b, g, s, qt, kt, ft: (b, g, qt[s], 0)))
    args.append(q)
    if has_delta:
        in_specs.append(pl.BlockSpec((1, nheads, TQ, LANES), lambda b, g, s, qt, kt, ft: (b, g, qt[s], 0)))
        args.append(q_off)
    if has_u:
        in_specs.append(pl.BlockSpec((1, nkv, TQ, LANES), lambda b, g, s, qt, kt, ft: (b, g, qt[s], 0)))
        args.append(u)
    if shared_kv:
        in_specs.append(pl.BlockSpec((1, nkv, TQ, da), lambda b, g, s, qt, kt, ft: (b, g, kt[s], 0)))
        in_specs.append(pl.BlockSpec((1, nkv, TQ, LANES), lambda b, g, s, qt, kt, ft: (b, g, kt[s], 0)))
    else:
        in_specs.append(pl.BlockSpec((1, nheads, TQ, da), lambda b, g, s, qt, kt, ft: (b, g, kt[s], 0)))
        in_specs.append(pl.BlockSpec((1, TQ, out_w), lambda b, g, s, qt, kt, ft: (b, kt[s], g)))
    args += [k, v]
    if gate_branch is not None:
        in_specs.append(pl.BlockSpec((1, TQ, LANES),
                                     lambda b, g, s, qt, kt, ft: (b, qt[s], C_SMALL // LANES)))
        args.append(z)

    scratch = [pltpu.VMEM((nheads, TQ, LANES), F32),
               pltpu.VMEM((nheads, TQ, LANES), F32),
               pltpu.VMEM((nheads // 2, TQ, LANES), F32)]
    if has_u:
        scratch.append(pltpu.VMEM((nheads, TQ, 2 * LANES), BF16))

    kern = functools.partial(_flash_kernel, nheads=nheads, shared_kv=shared_kv, window=window,
                             has_delta=has_delta, has_u=has_u, gate_branch=gate_branch, nblk=nq,
                             fixed_shift=fixed_shift)
    return pl.pallas_call(
        kern,
        grid_spec=pltpu.PrefetchScalarGridSpec(
            num_scalar_prefetch=3,
            grid=(batch, ngroups, nsteps),
            in_specs=in_specs,
            out_specs=pl.BlockSpec((1, TQ, out_w), lambda b, g, s, qt, kt, ft: (b, qt[s], g)),
            scratch_shapes=scratch),
        out_shape=jax.ShapeDtypeStruct((batch, seq, 4 * LANES), BF16),
        compiler_params=_cparams(("parallel", "parallel", "arbitrary")),
        name="flash_" + ("fox" if has_delta else ("sel" if has_u else "win")),
    )(*tabs, *args)


def _cmp_kernel(q_ref, kc_ref, vc_ref, ov_ref, gate_ref, o_ref, u_ref):
    g = pl.program_id(1)
    qi = pl.program_id(2)
    tq = q_ref.shape[2]
    ncmp = kc_ref.shape[2]
    nslc = ov_ref.shape[1]

    t = qi * tq + lax.broadcasted_iota(jnp.int32, (tq, ncmp), 0)
    n = lax.broadcasted_iota(jnp.int32, (tq, ncmp), 1)
    valid = (n * NSA_CMP_STRIDE + (NSA_CMP_LEN - 1)) <= t
    kc = kc_ref[0, 0]
    vc = vc_ref[0, 0]
    ov = ov_ref[...]
    lane = lax.broadcasted_iota(jnp.int32, (tq, LANES), 1)
    lo = lane < HEAD_DIM
    gate_blk = gate_ref[0]

    imp = jnp.zeros((tq, nslc), F32)
    outs = []
    for r in range(NSA_GROUP_HEADS):
        s = lax.dot_general(q_ref[0, r], kc, (((1,), (1,)), ((), ())), preferred_element_type=F32)
        s = jnp.where(valid, s, NEG)
        m = jnp.max(s, axis=1, keepdims=True)
        e = jnp.where(valid, jnp.exp2(s - m), 0.0)
        d = jnp.sum(e, axis=1, keepdims=True)
        pb = (e / jnp.where(d > 0.0, d, 1.0)).astype(BF16)
        o = jnp.dot(pb, vc, preferred_element_type=F32)
        gcol = _gate_column(gate_blk, lane, 3 * (g * NSA_GROUP_HEADS + r))
        outs.append(o * gcol)
        imp = imp + jnp.dot(pb, ov, preferred_element_type=F32)
    for p in range(NSA_GROUP_HEADS // 2):
        o_ref[0, :, p * LANES:(p + 1) * LANES] = jnp.where(lo, outs[2 * p], outs[2 * p + 1]).astype(o_ref.dtype)
    _write_unselected(u_ref, imp, qi, tq, nslc)


def _cmp_fixed_kernel(q_ref, kc_ref, vc_ref, ov_ref, gate_ref, o_ref, u_ref, d_scr, acc_scr, imp_scr,
                      *, chunk):
    g = pl.program_id(1)
    qi = pl.program_id(2)
    tq = q_ref.shape[2]
    ncmp = kc_ref.shape[2]
    nslc = ov_ref.shape[1]
    t0 = qi * tq
    d_scr[...] = jnp.zeros(d_scr.shape, F32)
    acc_scr[...] = jnp.zeros(acc_scr.shape, F32)
    imp_scr[...] = jnp.zeros(imp_scr.shape, F32)

    for c in range(ncmp // chunk):
        first_end = c * chunk * NSA_CMP_STRIDE + NSA_CMP_LEN - 1
        last_end = ((c + 1) * chunk - 1) * NSA_CMP_STRIDE + NSA_CMP_LEN - 1

        def body(masked, c=c):
            rows = slice(c * chunk, (c + 1) * chunk)
            kc = kc_ref[0, 0, rows, :]
            vo = jnp.concatenate([vc_ref[0, 0, rows, :], ov_ref[rows, :]], axis=1)
            if masked:
                t = t0 + lax.broadcasted_iota(jnp.int32, (tq, chunk), 0)
                n = c * chunk + lax.broadcasted_iota(jnp.int32, (tq, chunk), 1)
                valid = (n * NSA_CMP_STRIDE + (NSA_CMP_LEN - 1)) <= t
            for r in range(NSA_GROUP_HEADS):
                s = lax.dot_general(q_ref[0, r], kc, (((1,), (1,)), ((), ())), preferred_element_type=F32)
                if masked:
                    s = jnp.where(valid, s, NEG)
                e = jnp.exp2(s)
                part = e[:, :LANES]
                for j in range(1, chunk // LANES):
                    part = part + e[:, j * LANES:(j + 1) * LANES]
                d_scr[r] = d_scr[r] + part
                both = jnp.dot(e.astype(BF16), vo, preferred_element_type=F32)
                acc_scr[r] = acc_scr[r] + both[:, :LANES]
                imp_scr[r] = imp_scr[r] + both[:, LANES:]

        pl.when((first_end <= t0 + (tq - 1)) & (last_end > t0))(functools.partial(body, True))
        pl.when(last_end <= t0)(functools.partial(body, False))

    lane = lax.broadcasted_iota(jnp.int32, (tq, LANES), 1)
    lo = lane < HEAD_DIM
    gate_blk = gate_ref[0]
    imp = jnp.zeros((tq, nslc), F32)
    outs = []
    for r in range(NSA_GROUP_HEADS):
        d = jnp.sum(d_scr[r], axis=1, keepdims=True)
        inv = 1.0 / jnp.where(d > 0.0, d, 1.0)
        gcol = _gate_column(gate_blk, lane, 3 * (g * NSA_GROUP_HEADS + r))
        outs.append(acc_scr[r] * (inv * gcol))
        imp = imp + imp_scr[r] * inv
    for p in range(NSA_GROUP_HEADS // 2):
        o_ref[0, :, p * LANES:(p + 1) * LANES] = jnp.where(lo, outs[2 * p], outs[2 * p + 1]).astype(o_ref.dtype)
    _write_unselected(u_ref, imp, qi, tq, nslc)


N_FORCED = 3


def _write_unselected(u_ref, imp, qi, tq, nslc):
    tpos = qi * tq + lax.broadcasted_iota(jnp.int32, (tq, nslc), 0)
    jblk = lax.broadcasted_iota(jnp.int32, (tq, nslc), 1)
    cur = jnp.right_shift(tpos, SEL_SHIFT)
    forced = (jblk == 0) | (jblk == cur) | (jblk == cur - 1)
    visible = jblk <= cur
    score_t = jnp.where(visible & jnp.logical_not(forced), imp, -jnp.inf).T

    def run(nrows):
        sc = score_t[:nrows]
        jt = lax.broadcasted_iota(jnp.int32, (nrows, tq), 0).astype(F32)
        for _ in range(NSA_SEL_TOPK - N_FORCED):
            mx = jnp.max(sc, axis=0, keepdims=True)
            first = jnp.min(jnp.where(sc == mx, jt, float(nslc)), axis=0, keepdims=True)
            sc = jnp.where(jt == first, -jnp.inf, sc)
        taken = jnp.where(sc == -jnp.inf, 1.0, 0.0)
        if nrows < nslc:
            taken = jnp.concatenate([taken, jnp.zeros((nslc - nrows, tq), F32)], axis=0)
        chosen = visible & (forced | (taken.T > 0.5))
        u_ref[0, 0] = jnp.where(chosen, 0.0, 1.0).astype(BF16)

    half = nslc // 2
    last_block = jnp.right_shift(qi * tq + (tq - 1), SEL_SHIFT)
    pl.when(last_block < half)(functools.partial(run, half))
    pl.when(last_block >= half)(functools.partial(run, nslc))


def _cmp_select(qn, kc, vc, ov, z, *, batch, seq, fixed_shift):
    ncmp = kc.shape[2]
    nslc = ov.shape[1]
    if fixed_shift:
        chunk = 2 * LANES if ncmp % (2 * LANES) == 0 else ncmp
        kern = functools.partial(_cmp_fixed_kernel, chunk=chunk)
        scratch = [pltpu.VMEM((NSA_GROUP_HEADS, CMP_TQ, LANES), F32) for _ in range(3)]
    else:
        kern, scratch = _cmp_kernel, []
    return pl.pallas_call(
        kern,
        scratch_shapes=scratch,
        grid=(batch, NSA_KV_GROUPS, seq // CMP_TQ),
        in_specs=[pl.BlockSpec((1, NSA_GROUP_HEADS, CMP_TQ, LANES), lambda b, g, i: (b, g, i, 0)),
                  pl.BlockSpec((1, 1, ncmp, LANES), lambda b, g, i: (b, g, 0, 0)),
                  pl.BlockSpec((1, 1, ncmp, LANES), lambda b, g, i: (b, g, 0, 0)),
                  pl.BlockSpec((ncmp, nslc), lambda b, g, i: (0, 0)),
                  pl.BlockSpec((1, CMP_TQ, LANES), lambda b, g, i: (b, i, C_SMALL // LANES))],
        out_specs=[pl.BlockSpec((1, CMP_TQ, 2 * LANES), lambda b, g, i: (b, i, g)),
                   pl.BlockSpec((1, 1, CMP_TQ, nslc), lambda b, g, i: (b, g, i, 0))],
        out_shape=[jax.ShapeDtypeStruct((batch, seq, NSA_WIDTH), BF16),
                   jax.ShapeDtypeStruct((batch, NSA_KV_GROUPS, seq, nslc), BF16)],
        compiler_params=_cparams(("parallel", "parallel", "parallel")),
        name="cmp_select",
    )(qn, kc, vc, ov, z)


def _head_rms(x, g):
    return x * lax.rsqrt(jnp.mean(x * x, axis=-1, keepdims=True) + EPS) * g


def _mem_kernel(zq_ref, kv_ref, qg_ref, kg_ref, o_ref, kn_scr, vb_scr):
    @pl.when(pl.program_id(1) == 0)
    def _prep():
        for h in range(MEM_HEADS):
            kh = kv_ref[0, :, h * MEM_HEAD_DIM:(h + 1) * MEM_HEAD_DIM]
            kn_scr[h] = _head_rms(kh, kg_ref[...]).astype(BF16)
            vb_scr[h] = kv_ref[0, :, MEM_WIDTH + h * MEM_HEAD_DIM:
                               MEM_WIDTH + (h + 1) * MEM_HEAD_DIM].astype(BF16)

    for h in range(MEM_HEADS):
        sl = slice(h * MEM_HEAD_DIM, (h + 1) * MEM_HEAD_DIM)
        qh = (_head_rms(zq_ref[0, :, sl], qg_ref[...]) * ((MEM_HEAD_DIM ** -0.5) * LOG2E)).astype(BF16)
        s = lax.dot_general(qh, kn_scr[h], (((1,), (1,)), ((), ())), preferred_element_type=F32)
        m = jnp.max(s, axis=1, keepdims=True)
        e = jnp.exp2(s - m)
        p = (e / jnp.sum(e, axis=1, keepdims=True)).astype(BF16)
        o_ref[0, :, sl] = jnp.dot(p, vb_scr[h], preferred_element_type=F32).astype(o_ref.dtype)


def _mem_attention(z, mem_kv, q_g, k_g, *, batch, seq):
    mlen = mem_kv.shape[1]
    tq = TQ
    return pl.pallas_call(
        _mem_kernel,
        grid=(batch, seq // tq),
        in_specs=[pl.BlockSpec((1, tq, MEM_WIDTH), lambda b, i: (b, i, C_MEM_Q // MEM_WIDTH)),
                  pl.BlockSpec((1, mlen, 2 * MEM_WIDTH), lambda b, i: (b, 0, 0)),
                  pl.BlockSpec((1, MEM_HEAD_DIM), lambda b, i: (0, 0)),
                  pl.BlockSpec((1, MEM_HEAD_DIM), lambda b, i: (0, 0))],
        out_specs=pl.BlockSpec((1, tq, MEM_WIDTH), lambda b, i: (b, i, 0)),
        out_shape=jax.ShapeDtypeStruct((batch, seq, MEM_WIDTH), BF16),
        scratch_shapes=[pltpu.VMEM((MEM_HEADS, mlen, MEM_HEAD_DIM), BF16),
                        pltpu.VMEM((MEM_HEADS, mlen, MEM_HEAD_DIM), BF16)],
        compiler_params=_cparams(("parallel", "arbitrary")),
        name="mem_attention",
    )(z, mem_kv, q_g.reshape(1, MEM_HEAD_DIM), k_g.reshape(1, MEM_HEAD_DIM))


def _out_kernel(x_ref, oc_ref, os_ref, ow_ref, ob_ref, om_ref, sa_ref, sb_ref, sm_ref,
                g0_ref, g1_ref, g2_ref, wa_ref, wb_ref, wm_ref, wo_ref, y_ref):
    oa = ((oc_ref[...].astype(F32) + os_ref[...].astype(F32) + ow_ref[...].astype(F32))
          * jax.nn.silu(sa_ref[...]))
    ob = ob_ref[...].astype(F32) * jax.nn.silu(sb_ref[...])
    om = om_ref[...].astype(F32) * jax.nn.silu(sm_ref[...])
    u = (jax.nn.sigmoid(g0_ref[...]) * jnp.dot(oa.astype(BF16), wa_ref[...], preferred_element_type=F32)
         + jax.nn.sigmoid(g1_ref[...]) * jnp.dot(ob.astype(BF16), wb_ref[...], preferred_element_type=F32)
         + jax.nn.sigmoid(g2_ref[...]) * jnp.dot(om.astype(BF16), wm_ref[...], preferred_element_type=F32))
    y_ref[...] = x_ref[...] + jnp.dot(u.astype(BF16), wo_ref[...], preferred_element_type=F32)


def _out_proj(x2d, o_c, o_s, o_w, o_b, o_m, z2d, wa, wb, wm, wo):
    m = x2d.shape[0]
    tm = 256
    w512 = 512
    row512 = lambda c: pl.BlockSpec((tm, w512), lambda i, c=c: (i, c))
    row1024 = lambda c: pl.BlockSpec((tm, D_MODEL), lambda i, c=c: (i, c))
    full = lambda shape: pl.BlockSpec(shape, lambda i: (0, 0))
    return pl.pallas_call(
        _out_kernel,
        grid=(m // tm,),
        in_specs=[row1024(0), row512(0), row512(0), row512(0), row512(0), row512(0),
                  row512(C_NSA_SILU // w512), row512(C_FOX_SILU // w512), row512(C_MEM_SILU // w512),
                  row1024(C_MERGE // D_MODEL), row1024(C_MERGE // D_MODEL + 1),
                  row1024(C_MERGE // D_MODEL + 2),
                  full((w512, D_MODEL)), full((w512, D_MODEL)), full((w512, D_MODEL)),
                  full((D_MODEL, D_MODEL))],
        out_specs=row1024(0),
        out_shape=jax.ShapeDtypeStruct((m, D_MODEL), F32),
        compiler_params=_cparams(("parallel",)),
        name="out_proj",
    )(x2d, o_c, o_s, o_w, o_b, o_m, z2d, z2d, z2d, z2d, z2d, z2d, wa, wb, wm, wo)


def _permute_w_in(w_in):
    o = np.cumsum([0, NSA_WIDTH, 6 * NSA_KV_GROUPS * HEAD_DIM, 3 * NSA_HEADS, NSA_WIDTH,
                   3 * FOX_WIDTH, FOX_HEADS, FOX_WIDTH, MEM_WIDTH, MEM_WIDTH, 3 * D_MODEL])
    nsa_q, nsa_kv, nsa_gate, nsa_silu, fox_qkv, fox_f, fox_silu, mem_q, mem_silu, merge = [
        w_in[..., o[i]:o[i + 1]] for i in range(10)]
    pad = jnp.zeros(w_in.shape[:2] + (LANES - 3 * NSA_HEADS - FOX_HEADS,), w_in.dtype)
    return jnp.concatenate([nsa_q, nsa_silu, fox_qkv, fox_silu, mem_q, mem_silu, merge, nsa_kv,
                            nsa_gate, fox_f, pad], axis=-1).astype(BF16)


def _pad_lanes(x, width=LANES):
    return jnp.pad(x, [(0, 0)] * (x.ndim - 1) + [(0, width - x.shape[-1])])


def _tile2(g):
    return jnp.concatenate([g, g], axis=-1)[:, None, :]


def _overlap(n_cmp_pad, n_slc):
    cs = np.arange(n_cmp_pad)[:, None] * NSA_CMP_STRIDE
    ss = np.arange(NSLC_PAD)[None, :] * NSA_SEL_LEN
    ov = np.clip(np.minimum(cs + NSA_CMP_LEN, ss + NSA_SEL_LEN) - np.maximum(cs, ss), 0, None)
    ov = ov / NSA_CMP_LEN
    ov[-1] = 0.0
    ov[:, n_slc:] = 0.0
    return jnp.asarray(ov, BF16)


def _compress_w1(w1):
    nl = w1.shape[0]
    halves = w1.reshape(nl, 2, NSA_CMP_STRIDE, 1, HEAD_DIM, NSA_CMP_HIDDEN)
    eye = jnp.eye(NSA_KV_GROUPS, dtype=w1.dtype)
    out = jnp.einsum("pg,zhldn->zlpdghn", eye, halves[:, :, :, 0])
    return out.reshape(nl, NSA_CMP_STRIDE * NSA_KV_GROUPS * HEAD_DIM,
                       NSA_KV_GROUPS * 2 * NSA_CMP_HIDDEN).astype(BF16)


def _layer(x, mem, p, consts):
    batch, seq, _ = x.shape
    m_rows = batch * seq
    cos4, sin_signed, tri, ov = consts
    z2d = _norm_matmul(x.reshape(m_rows, D_MODEL), p["norm_g"], p["w_in"], 512, Z_WIDTH // 3)
    z = z2d.reshape(batch, seq, Z_WIDTH)

    (qn, kcr, vcr, ks, vs, kw, vw, fqd, fqo, fka, fv, base) = _prep(
        z, cos4, sin_signed, tri, p["gq"], p["gks"], p["gkw"], p["gfq"], p["gfk"], p["fb"],
        p["s_sel"], p["s_win"], p["s_fox"], batch=batch, seq=seq)

    nchunk = seq // NSA_CMP_STRIDE
    kc, vc = _compress(kcr.reshape(batch, nchunk, NSA_CMP_STRIDE * LANES),
                       vcr.reshape(batch, nchunk, NSA_CMP_STRIDE * LANES),
                       p["w1k"], p["w1v"], p["bk"], p["bv"], p["w2k"], p["w2v"], p["gkc"], p["s_cmp"],
                       batch=batch)

    def attend(fixed_shift):
        o_c, unsel = _cmp_select(qn, kc, vc, ov, z, batch=batch, seq=seq, fixed_shift=fixed_shift)
        o_s = _flash(qn, ks, vs, batch=batch, seq=seq, shared_kv=True, fixed_shift=fixed_shift,
                     u=unsel, z=z, gate_branch=1)
        o_w = _flash(qn, kw, vw, batch=batch, seq=seq, shared_kv=True, fixed_shift=fixed_shift,
                     window=NSA_WINDOW, z=z, gate_branch=2)
        o_b = _flash(fqd, fka, fv, batch=batch, seq=seq, shared_kv=False, fixed_shift=fixed_shift,
                     base=base.reshape(-1), q_off=fqo)
        return o_c, o_s, o_w, o_b

    o_c, o_s, o_w, o_b = lax.cond(p["bound_ok"], lambda: attend(True), lambda: attend(False))

    mlen = mem.shape[1]
    mem_kv = _norm_matmul(mem.reshape(batch * mlen, D_MODEL), p["mem_norm_g"], p["w_mem_kv"],
                          min(512, batch * mlen), 2 * MEM_WIDTH).reshape(batch, mlen, 2 * MEM_WIDTH)
    o_m = _mem_attention(z, mem_kv, p["mem_q_norm"], p["mem_k_norm"], batch=batch, seq=seq)

    y = _out_proj(x.reshape(m_rows, D_MODEL), o_c.reshape(m_rows, -1), o_s.reshape(m_rows, -1),
                  o_w.reshape(m_rows, -1), o_b.reshape(m_rows, -1), o_m.reshape(m_rows, -1), z2d,
                  p["w_branch_a"], p["w_branch_b"], p["w_branch_m"], p["w_out"])
    return y.reshape(batch, seq, D_MODEL)


def kernel(x, mem, norm_g, mem_norm_g, w_in, nsa_q_norm, nsa_k_norm, cmp_pe_k, cmp_w1_k, cmp_w2_k,
           cmp_pe_v, cmp_w1_v, cmp_w2_v, fox_q_norm, fox_k_norm, fox_f_bias, mem_q_norm, mem_k_norm,
           w_mem_kv, w_branch_a, w_branch_b, w_branch_m, w_out):
    batch, seq, _ = x.shape
    depth = w_in.shape[0]
    n_slc = seq // NSA_SEL_LEN
    assert n_slc <= NSLC_PAD and seq % TQ == 0 and NSA_WINDOW == TQ

    half = HEAD_DIM // 2
    inv_freq = ROPE_THETA ** (-jnp.arange(half, dtype=F32) / half)
    ang = jnp.arange(seq).astype(F32)[:, None] * inv_freq[None, :]
    cos, sin = jnp.cos(ang), jnp.sin(ang)
    cos4 = jnp.concatenate([cos, cos, cos, cos], axis=-1)
    sin_signed = jnp.concatenate([-sin, sin, -sin, sin], axis=-1)
    tri = jnp.asarray(np.tril(np.ones((TQ, TQ), np.float32)), BF16)
    consts = (cos4, sin_signed, tri, _overlap(seq // NSA_CMP_STRIDE, n_slc))

    hp = lax.Precision.HIGHEST
    pe_bias = lambda pe, w1: jnp.einsum("lk,lkn->ln", pe.reshape(depth, -1), w1, precision=hp)[:, None, :]
    fb = jnp.zeros((depth, 1, LANES), F32).at[:, 0, F_LANE:F_LANE + FOX_HEADS].set(fox_f_bias)

    def logit_bound(gq, gk):
        bound = (BOUND_MARGIN * HEAD_DIM * QK_SCALE
                 * jnp.max(jnp.abs(gq), axis=-1) * jnp.max(jnp.abs(gk), axis=-1))
        return bound.astype(BF16).astype(F32)

    bounds = [logit_bound(nsa_q_norm, nsa_k_norm[:, 1]), logit_bound(nsa_q_norm, nsa_k_norm[:, 2]),
              logit_bound(fox_q_norm, fox_k_norm), logit_bound(nsa_q_norm, nsa_k_norm[:, 0])]
    bound_ok = functools.reduce(jnp.maximum, bounds) <= MAX_FIXED_BOUND
    bound_row = lambda s: jnp.broadcast_to(s[:, None, None], (depth, 1, LANES))
    stacked = dict(
        s_sel=bound_row(bounds[0]), s_win=bound_row(bounds[1]), s_fox=bound_row(bounds[2]),
        s_cmp=bound_row(bounds[3]),
        bound_ok=bound_ok,
        norm_g=norm_g, mem_norm_g=mem_norm_g, w_in=_permute_w_in(w_in),
        gq=_tile2(nsa_q_norm), gks=_tile2(nsa_k_norm[:, 1]), gkw=_tile2(nsa_k_norm[:, 2]),
        gkc=_pad_lanes(nsa_k_norm[:, 0])[:, None, :], gfq=_tile2(fox_q_norm), gfk=_tile2(fox_k_norm),
        fb=fb, w1k=_compress_w1(cmp_w1_k), w1v=_compress_w1(cmp_w1_v),
        bk=pe_bias(cmp_pe_k, cmp_w1_k), bv=pe_bias(cmp_pe_v, cmp_w1_v),
        w2k=_pad_lanes(cmp_w2_k).astype(BF16),
        w2v=jnp.concatenate([cmp_w2_v, cmp_w2_v], axis=-1).astype(BF16),
        mem_q_norm=mem_q_norm, mem_k_norm=mem_k_norm, w_mem_kv=w_mem_kv.astype(BF16),
        w_branch_a=w_branch_a.astype(BF16), w_branch_b=w_branch_b.astype(BF16),
        w_branch_m=w_branch_m.astype(BF16), w_out=w_out.astype(BF16))
    for l in range(depth):
        x = _layer(x, mem, {k: v[l] for k, v in stacked.items()}, consts)
    return x
```

```python
import functools

import numpy as np
import jax
import jax.numpy as jnp
from jax import lax
from jax.experimental import pallas as pl
from jax.experimental.pallas import tpu as pltpu

F32 = jnp.float32
BF16 = jnp.bfloat16

D_MODEL = 1024
HEAD_DIM = 64
ROPE_THETA = 10000.0
EPS = 1e-6
NSA_HEADS = 8
NSA_KV_GROUPS = 2
NSA_GROUP_HEADS = NSA_HEADS // NSA_KV_GROUPS
NSA_CMP_LEN = 32
NSA_CMP_STRIDE = 16
NSA_CMP_HIDDEN = 128
NSA_SEL_LEN = 64
NSA_SEL_TOPK = 16
NSA_WINDOW = 512
FORCE_SCORE = 1e9
NSA_WIDTH = NSA_HEADS * HEAD_DIM
FOX_HEADS = 8
FOX_WIDTH = FOX_HEADS * HEAD_DIM
MEM_HEADS = 4
MEM_HEAD_DIM = 128
MEM_WIDTH = MEM_HEADS * MEM_HEAD_DIM

LANES = 128
VMEM_LIMIT = 48 * 1024 * 1024
NEG = -1e30
SEL_MASK = 32768.0
SEL_SHIFT = 6
NSLC_PAD = LANES
LOG2E = 1.4426950408889634
QK_SCALE = (HEAD_DIM ** -0.5) * LOG2E
BOUND_MARGIN = 1.02
MAX_FIXED_BOUND = 60.0
ZERO_WEIGHT_LOG2 = -160.0

C_NSA_Q = 0
C_NSA_SILU = 512
C_FOX_Q = 1024
C_FOX_K = 1536
C_FOX_V = 2048
C_FOX_SILU = 2560
C_MEM_Q = 3072
C_MEM_SILU = 3584
C_MERGE = 4096
C_NSA_KV = 7168
C_SMALL = 7936
Z_WIDTH = 8064
F_LANE = 3 * NSA_HEADS
AUG = HEAD_DIM

TQ = 512
CMP_TQ = 256


def _cparams(sem):
    return pltpu.CompilerParams(dimension_semantics=sem, vmem_limit_bytes=VMEM_LIMIT)


def _norm_matmul_kernel(x_ref, g_ref, w_ref, o_ref):
    x = x_ref[...]
    ms = jnp.mean(x * x, axis=-1, keepdims=True)
    h = (x * lax.rsqrt(ms + EPS) * g_ref[...]).astype(BF16)
    o_ref[...] = jnp.dot(h, w_ref[...], preferred_element_type=F32)


def _norm_matmul(x2d, g, w, tm, tn):
    m, k = x2d.shape
    n = w.shape[1]
    return pl.pallas_call(
        _norm_matmul_kernel,
        grid=(n // tn, m // tm),
        in_specs=[pl.BlockSpec((tm, k), lambda j, i: (i, 0)),
                  pl.BlockSpec((1, k), lambda j, i: (0, 0)),
                  pl.BlockSpec((k, tn), lambda j, i: (0, j))],
        out_specs=pl.BlockSpec((tm, tn), lambda j, i: (i, j)),
        out_shape=jax.ShapeDtypeStruct((m, n), F32),
        compiler_params=_cparams(("parallel", "parallel")),
        name="norm_matmul",
    )(x2d, g.reshape(1, k), w)


def _pair_rms(x, g2, head_ones):
    sq = x * x
    hi = sq.astype(BF16)
    low = (sq - hi.astype(F32)).astype(BF16)
    ssq = (jnp.dot(hi, head_ones, preferred_element_type=F32)
           + jnp.dot(low, head_ones, preferred_element_type=F32))
    return x * lax.rsqrt(ssq * (1.0 / HEAD_DIM) + EPS) * g2


def _pair_rope(y, cos4, sin_signed, first_half):
    rot = jnp.where(first_half, pltpu.roll(y, LANES - HEAD_DIM // 2, 1), pltpu.roll(y, HEAD_DIM // 2, 1))
    return y * cos4 + rot * sin_signed


def _head_a(y, lo, tail=0.0):
    return jnp.where(lo, y, tail)


def _head_b(y, lo, tail=0.0):
    return jnp.where(lo, pltpu.roll(y, HEAD_DIM, 1), tail)


def _split3_f32(c):
    hi = c.astype(BF16).astype(F32)
    r = c - hi
    mid = r.astype(BF16).astype(F32)
    return hi, mid, r - mid


def _prep_kernel(zq_ref, zc_ref, zs_ref, zw_ref, fq_ref, fk_ref, fv_ref, sm_ref, cos_ref, sin_ref,
                 tri_ref, ones_ref, pq_ref, pk_ref, gq_ref, gks_ref, gkw_ref, gfq_ref, gfk_ref, fb_ref,
                 ss_ref, sw_ref, sf_ref,
                 qn_ref, kcr_ref, vcr_ref, ks_ref, vs_ref, kw_ref, vw_ref, fqd_ref, fqo_ref, fka_ref,
                 fvb_ref, base_ref, run_scr):
    ti = pl.program_id(1)
    tt = zq_ref.shape[1]
    lane = lax.broadcasted_iota(jnp.int32, (tt, LANES), 1)
    lo = lane < HEAD_DIM
    first_half = (lane & (HEAD_DIM // 2)) == 0
    cos4 = cos_ref[...]
    sin_signed = sin_ref[...]
    head_ones = ones_ref[...]
    one_at_aug = jnp.where(lane == AUG, 1.0, 0.0)

    for p in range(NSA_HEADS // 2):
        y = _pair_rope(_pair_rms(zq_ref[0, :, p * LANES:(p + 1) * LANES], gq_ref[...], head_ones),
                       cos4, sin_signed, first_half) * QK_SCALE
        qn_ref[0, 2 * p] = _head_a(y, lo, one_at_aug).astype(BF16)
        qn_ref[0, 2 * p + 1] = _head_b(y, lo, one_at_aug).astype(BF16)

    kcr_ref[0] = _pair_rope(zc_ref[0, :, :LANES], cos4, sin_signed, first_half).astype(BF16)
    vcr_ref[0] = zc_ref[0, :, LANES:].astype(BF16)

    sel_lane = jnp.right_shift(ti * tt + lax.broadcasted_iota(jnp.int32, (tt, LANES), 0), SEL_SHIFT)
    onehot = jnp.where(lane == sel_lane, -SEL_MASK, 0.0).astype(BF16)
    for z_ref, g_ref, k_ref, v_ref, bound_ref, with_onehot in (
            (zs_ref, gks_ref, ks_ref, vs_ref, ss_ref, True),
            (zw_ref, gkw_ref, kw_ref, vw_ref, sw_ref, False)):
        y = _pair_rope(_pair_rms(z_ref[0, :, :LANES], g_ref[...], head_ones), cos4, sin_signed, first_half)
        v = z_ref[0, :, LANES:]
        vr = pltpu.roll(v, HEAD_DIM, 1)
        neg_bound = jnp.where(lane == AUG, -bound_ref[...], 0.0)
        k_ref[0, 0, :, :LANES] = _head_a(y, lo, neg_bound).astype(BF16)
        k_ref[0, 1, :, :LANES] = _head_b(y, lo, neg_bound).astype(BF16)
        if with_onehot:
            k_ref[0, 0, :, LANES:] = onehot
            k_ref[0, 1, :, LANES:] = onehot
        v_ref[0, 0] = jnp.where(lo, v, vr).astype(BF16)
        v_ref[0, 1] = jnp.where(lo, vr, v).astype(BF16)

    @pl.when(ti == 0)
    def _reset():
        run_scr[...] = jnp.zeros(run_scr.shape, F32)

    xf = sm_ref[0] + fb_ref[...]
    log_f = (jnp.minimum(xf, 0.0) - jnp.log1p(jnp.exp(-jnp.abs(xf)))) * LOG2E
    tri = tri_ref[...]
    parts = _split3_f32(log_f)
    cum = sum(jnp.dot(tri, part.astype(BF16), preferred_element_type=F32) for part in parts)
    base_ref[0, 0] = run_scr[...]
    run_scr[...] = run_scr[...] + cum[tt - 1:tt, :]
    fvb_ref[0] = fv_ref[0].astype(BF16)

    parts6 = _split3_f32(cum) + _split3_f32(cum - cum[tt - 1:tt, :])
    x6 = jnp.concatenate([part.astype(BF16) for part in parts6], axis=1)
    aug_q = jnp.dot(x6, pq_ref[...], preferred_element_type=F32)
    aug_k = jnp.dot(x6, pk_ref[...], preferred_element_type=F32)

    lane1 = lax.broadcasted_iota(jnp.int32, (1, LANES), 1)
    span = lambda a, b: jnp.where((lane1 >= a) & (lane1 < b), 1.0, 0.0)
    q_tail_diag = span(AUG + 3, AUG + 6) + span(AUG + 9, AUG + 10)
    q_tail_off = span(AUG + 6, AUG + 10)
    k_tail = span(AUG, AUG + 3) + jnp.where(lane1 == AUG + 9, -sf_ref[...], 0.0)
    for p in range(FOX_HEADS // 2):
        sl = slice(p * LANES, (p + 1) * LANES)
        yq = _pair_rms(fq_ref[0, :, sl], gfq_ref[...], head_ones) * QK_SCALE
        yk = _pair_rms(fk_ref[0, :, sl], gfk_ref[...], head_ones)
        for e, split in ((0, _head_a), (1, _head_b)):
            h = 2 * p + e
            hs = slice(h * LANES, (h + 1) * LANES)
            qh = split(yq, lo) + aug_q[:, hs]
            fqd_ref[0, h] = (qh + q_tail_diag).astype(BF16)
            fqo_ref[0, h] = (qh + q_tail_off).astype(BF16)
            fka_ref[0, h] = (split(yk, lo) + aug_k[:, hs] + k_tail).astype(BF16)


def _prep(z, cos4, sin_signed, tri, head_ones, perm_q, perm_k, gq, gks, gkw, gfq, gfk, fb,
          s_sel, s_win, s_fox, *, batch, seq):
    nblk = seq // TQ
    whole = lambda a: pl.BlockSpec(a.shape, lambda b, i: (0, 0))
    zspec = lambda w, c: pl.BlockSpec((1, TQ, w), lambda b, i, c=c: (b, i, c))
    row = pl.BlockSpec((1, LANES), lambda b, i: (0, 0))
    tab = pl.BlockSpec((TQ, LANES), lambda b, i: (i, 0))
    heads = lambda n, w: pl.BlockSpec((1, n, TQ, w), lambda b, i: (b, 0, i, 0))
    tok = lambda w: pl.BlockSpec((1, TQ, w), lambda b, i: (b, i, 0))
    bf = lambda *shape: jax.ShapeDtypeStruct(shape, BF16)
    return pl.pallas_call(
        _prep_kernel,
        grid=(batch, nblk),
        in_specs=[zspec(NSA_WIDTH, C_NSA_Q // NSA_WIDTH),
                  zspec(2 * LANES, C_NSA_KV // (2 * LANES)),
                  zspec(2 * LANES, C_NSA_KV // (2 * LANES) + 1),
                  zspec(2 * LANES, C_NSA_KV // (2 * LANES) + 2),
                  zspec(FOX_WIDTH, C_FOX_Q // FOX_WIDTH),
                  zspec(FOX_WIDTH, C_FOX_K // FOX_WIDTH),
                  zspec(FOX_WIDTH, C_FOX_V // FOX_WIDTH),
                  zspec(LANES, C_SMALL // LANES),
                  tab, tab, whole(tri), whole(head_ones), whole(perm_q), whole(perm_k),
                  row, row, row, row, row, row, row, row, row],
        out_specs=[heads(NSA_HEADS, LANES), tok(LANES), tok(LANES),
                   heads(NSA_KV_GROUPS, 2 * LANES), heads(NSA_KV_GROUPS, LANES),
                   heads(NSA_KV_GROUPS, LANES), heads(NSA_KV_GROUPS, LANES),
                   heads(FOX_HEADS, LANES), heads(FOX_HEADS, LANES), heads(FOX_HEADS, LANES),
                   tok(FOX_WIDTH), pl.BlockSpec((1, 1, 1, LANES), lambda b, i: (b, i, 0, 0))],
        out_shape=[bf(batch, NSA_HEADS, seq, LANES), bf(batch, seq, LANES), bf(batch, seq, LANES),
                   bf(batch, NSA_KV_GROUPS, seq, 2 * LANES), bf(batch, NSA_KV_GROUPS, seq, LANES),
                   bf(batch, NSA_KV_GROUPS, seq, LANES), bf(batch, NSA_KV_GROUPS, seq, LANES),
                   bf(batch, FOX_HEADS, seq, LANES), bf(batch, FOX_HEADS, seq, LANES),
                   bf(batch, FOX_HEADS, seq, LANES), bf(batch, seq, FOX_WIDTH),
                   jax.ShapeDtypeStruct((batch, nblk, 1, LANES), F32)],
        scratch_shapes=[pltpu.VMEM((1, LANES), F32)],
        compiler_params=_cparams(("parallel", "arbitrary")),
        name="prep",
    )(z, z, z, z, z, z, z, z, cos4, sin_signed, tri, head_ones, perm_q, perm_k,
      gq, gks, gkw, gfq, gfk, fb, s_sel, s_win, s_fox)


def _compress_kernel(xk_ref, xv_ref, w1k_ref, w1v_ref, bk_ref, bv_ref, w2k_ref, w2v_ref, kg_ref,
                     sc_ref, kc_ref, vc_ref):
    nchunk = xk_ref.shape[1]
    lane = lax.broadcasted_iota(jnp.int32, (nchunk, LANES), 1)
    for x_ref, w1_ref, b_ref, w2_ref, o_ref, is_key in ((xk_ref, w1k_ref, bk_ref, w2k_ref, kc_ref, True),
                                                        (xv_ref, w1v_ref, bv_ref, w2v_ref, vc_ref, False)):
        h = jnp.dot(x_ref[0], w1_ref[...], preferred_element_type=F32)
        for g in range(NSA_KV_GROUPS):
            c0 = g * 2 * NSA_CMP_HIDDEN
            top = h[:, c0:c0 + NSA_CMP_HIDDEN]
            bot = h[:, c0 + NSA_CMP_HIDDEN:c0 + 2 * NSA_CMP_HIDDEN]
            hid = top + pltpu.roll(bot, nchunk - 1, 0) + b_ref[...]
            act = (hid * jax.nn.sigmoid(hid)).astype(BF16)
            o = jnp.dot(act, w2_ref[...], preferred_element_type=F32)
            if is_key:
                ms = jnp.sum(o * o, axis=1, keepdims=True) * (1.0 / HEAD_DIM)
                o = o * lax.rsqrt(ms + EPS) * kg_ref[...]
                o = jnp.where(lane == AUG, -sc_ref[...], o)
            o_ref[0, g] = o.astype(BF16)


def _compress(xk, xv, w1k, w1v, bk, bv, w2k, w2v, kg, s_cmp, *, batch):
    nchunk, kin = xk.shape[1], xk.shape[2]
    xspec = pl.BlockSpec((1, nchunk, kin), lambda b: (b, 0, 0))
    full = lambda a: pl.BlockSpec(a.shape, lambda b: (0,) * a.ndim)
    ospec = pl.BlockSpec((1, NSA_KV_GROUPS, nchunk, LANES), lambda b: (b, 0, 0, 0))
    oshape = jax.ShapeDtypeStruct((batch, NSA_KV_GROUPS, nchunk, LANES), BF16)
    return pl.pallas_call(
        _compress_kernel,
        grid=(batch,),
        in_specs=[xspec, xspec, full(w1k), full(w1v), full(bk), full(bv), full(w2k), full(w2v), full(kg),
                  full(s_cmp)],
        out_specs=[ospec, ospec],
        out_shape=[oshape, oshape],
        compiler_params=_cparams(("parallel",)),
        name="compress",
    )(xk, xv, w1k, w1v, bk, bv, w2k, w2v, kg, s_cmp)


def _gate_column(gate_blk, lane, idx):
    col = jnp.sum(jnp.where(lane == idx, gate_blk, 0.0), axis=1, keepdims=True)
    return jax.nn.sigmoid(col)


def _flash_kernel(qi_tab, ki_tab, fl_tab, *refs, nheads, shared_kv, window, has_delta,
                  has_u, gate_branch, nblk, fixed_shift):
    refs = list(refs)
    base_ref = refs.pop(0) if has_delta else None
    q_ref = refs.pop(0)
    qoff_ref = refs.pop(0) if has_delta else None
    u_ref = refs.pop(0) if has_u else None
    k_ref = refs.pop(0)
    v_ref = refs.pop(0)
    gate_ref = refs.pop(0) if gate_branch is not None else None
    o_ref = refs.pop(0)
    m_scr = refs.pop(0)
    l_scr = refs.pop(0)
    acc_scr = refs.pop(0)
    qcat_scr = refs.pop(0) if has_u else None

    b = pl.program_id(0)
    g = pl.program_id(1)
    step = pl.program_id(2)
    qi = qi_tab[step]
    ki = ki_tab[step]
    fl = fl_tab[step]
    tq = q_ref.shape[2]
    tk = k_ref.shape[2]

    @pl.when((fl & 1) == 1)
    def _init():
        m_scr[...] = jnp.full(m_scr.shape, NEG, F32)
        l_scr[...] = jnp.zeros(l_scr.shape, F32)
        acc_scr[...] = jnp.zeros(acc_scr.shape, F32)
        if has_u:
            for r in range(nheads):
                qcat_scr[r, :, :LANES] = q_ref[0, r]
                qcat_scr[r, :, LANES:] = u_ref[0, r // NSA_GROUP_HEADS]

    lane = lax.broadcasted_iota(jnp.int32, (tq, LANES), 1)
    lo = lane < HEAD_DIM

    def tile(diagonal):
        masked = diagonal or window is not None
        if masked:
            row = lax.broadcasted_iota(jnp.int32, (tq, tk), 0)
            col = lax.broadcasted_iota(jnp.int32, (tq, tk), 1)
            if diagonal:
                valid = row >= col
            else:
                valid = (tq + row - col) < window
        for p in range(nheads // 2):
            pvs, alphas = [], []
            for e in range(2):
                hh = 2 * p + e
                if has_u:
                    q = qcat_scr[hh]
                elif has_delta and not diagonal:
                    q = qoff_ref[0, hh]
                else:
                    q = q_ref[0, hh]
                k = k_ref[0, hh // NSA_GROUP_HEADS] if shared_kv else k_ref[0, hh]
                v = v_ref[0, hh // NSA_GROUP_HEADS] if shared_kv else v_ref[0, :, p * LANES:(p + 1) * LANES]
                s = lax.dot_general(q, k, (((1,), (1,)), ((), ())), preferred_element_type=F32)
                if masked:
                    s = jnp.where(valid, s, NEG)
                delta = None
                if has_delta and not diagonal:
                    hoff = F_LANE + g * nheads + hh
                    delta = (base_ref[(b * nblk + qi) * LANES + hoff]
                             - base_ref[(b * nblk + ki + 1) * LANES + hoff])
                if fixed_shift:
                    pm = jnp.exp2(s)
                    rowsum = pm[:, :LANES]
                    for j in range(1, tk // LANES):
                        rowsum = rowsum + pm[:, j * LANES:(j + 1) * LANES]
                    pv = jnp.dot(pm.astype(BF16), v, preferred_element_type=F32)
                    if delta is not None:
                        w = jnp.exp2(jnp.full((1, LANES), delta, F32))
                        rowsum = rowsum * w
                        pv = pv * w
                    l_scr[hh] = l_scr[hh] + rowsum
                    pvs.append(pv)
                else:
                    m_prev = m_scr[hh]
                    m_cur = jnp.max(s, axis=1, keepdims=True)
                    if delta is not None:
                        m_cur = m_cur + delta
                    m_new = jnp.maximum(m_prev, m_cur)
                    alpha = jnp.exp2(m_prev - m_new)
                    shift = m_new[:, :1]
                    if delta is not None:
                        shift = shift - delta
                    pm = jnp.exp2(s - shift)
                    l_scr[hh] = alpha * l_scr[hh] + jnp.sum(pm, axis=1, keepdims=True)
                    m_scr[hh] = m_new
                    pvs.append(jnp.dot(pm.astype(BF16), v, preferred_element_type=F32))
                    alphas.append(alpha)
            if fixed_shift:
                acc_scr[p] = acc_scr[p] + jnp.where(lo, pvs[0], pvs[1])
            else:
                acc_scr[p] = (acc_scr[p] * jnp.where(lo, alphas[0], alphas[1])
                              + jnp.where(lo, pvs[0], pvs[1]))

    pl.when(qi == ki)(lambda: tile(True))
    if fixed_shift and has_delta:
        k_next = jnp.minimum(ki + 1, qi)
        top = None
        for hh in range(nheads):
            hoff = F_LANE + g * nheads + hh
            delta = (base_ref[(b * nblk + qi) * LANES + hoff]
                     - base_ref[(b * nblk + k_next) * LANES + hoff])
            top = delta if top is None else jnp.maximum(top, delta)
        pl.when((qi != ki) & (top > ZERO_WEIGHT_LOG2))(lambda: tile(False))
    else:
        pl.when(qi != ki)(lambda: tile(False))

    @pl.when((fl & 2) == 2)
    def _finish():
        for p in range(nheads // 2):
            if fixed_shift:
                la = jnp.sum(l_scr[2 * p], axis=1, keepdims=True)
                lb = jnp.sum(l_scr[2 * p + 1], axis=1, keepdims=True)
            else:
                la, lb = l_scr[2 * p], l_scr[2 * p + 1]
            inv = jnp.where(lo, 1.0 / la, 1.0 / lb)
            if gate_branch is not None:
                gate_blk = gate_ref[0]
                h0 = g * nheads + 2 * p
                ga = _gate_column(gate_blk, lane, 3 * h0 + gate_branch)
                gb = _gate_column(gate_blk, lane, 3 * (h0 + 1) + gate_branch)
                inv = inv * jnp.where(lo, ga, gb)
            o_ref[0, :, p * LANES:(p + 1) * LANES] = (acc_scr[p] * inv).astype(o_ref.dtype)


def _tile_tables(nq, window_tiles):
    qi, ki, fl = [], [], []
    for i in range(nq):
        lo = 0 if window_tiles is None else max(0, i - window_tiles)
        for j in range(lo, i + 1):
            qi.append(i)
            ki.append(j)
            fl.append((1 if j == lo else 0) | (2 if j == i else 0))
    return (jnp.asarray(qi, jnp.int32), jnp.asarray(ki, jnp.int32), jnp.asarray(fl, jnp.int32))


def _flash(q, k, v, *, batch, seq, shared_kv, fixed_shift, window=None, base=None, q_off=None,
           u=None, z=None, gate_branch=None):
    nheads = NSA_HEADS
    nq = seq // TQ
    ngroups = NSA_HEADS // nheads
    nkv = NSA_KV_GROUPS
    da = k.shape[-1]
    tabs = _tile_tables(nq, None if window is None else window // TQ)
    nsteps = int(tabs[0].shape[0])
    has_delta = base is not None
    has_u = u is not None
    out_w = LANES * (nheads // 2)

    in_specs, args = [], []
    if has_delta:
        in_specs.append(pl.BlockSpec(memory_space=pltpu.SMEM))
        args.append(base)
    in_specs.append(pl.BlockSpec((1, nheads, TQ, LANES), lambda b, g, s, qt, kt, ft: (b, g, qt[s], 0)))
    args.append(q)
    if has_delta:
        in_specs.append(pl.BlockSpec((1, nheads, TQ, LANES), lambda b, g, s, qt, kt, ft: (b, g, qt[s], 0)))
        args.append(q_off)
    if has_u:
        in_specs.append(pl.BlockSpec((1, nkv, TQ, LANES), lambda b, g, s, qt, kt, ft: (b, g, qt[s], 0)))
        args.append(u)
    if shared_kv:
        in_specs.append(pl.BlockSpec((1, nkv, TQ, da), lambda b, g, s, qt, kt, ft: (b, g, kt[s], 0)))
        in_specs.append(pl.BlockSpec((1, nkv, TQ, LANES), lambda b, g, s, qt, kt, ft: (b, g, kt[s], 0)))
    else:
        in_specs.append(pl.BlockSpec((1, nheads, TQ, da), lambda b, g, s, qt, kt, ft: (b, g, kt[s], 0)))
        in_specs.append(pl.BlockSpec((1, TQ, out_w), lambda b, g, s, qt, kt, ft: (b, kt[s], g)))
    args += [k, v]
    if gate_branch is not None:
        in_specs.append(pl.BlockSpec((1, TQ, LANES),
                                     lambda b, g, s, qt, kt, ft: (b, qt[s], C_SMALL // LANES)))
        args.append(z)

    scratch = [pltpu.VMEM((nheads, TQ, LANES), F32),
               pltpu.VMEM((nheads, TQ, LANES), F32),
               pltpu.VMEM((nheads // 2, TQ, LANES), F32)]
    if has_u:
        scratch.append(pltpu.VMEM((nheads, TQ, 2 * LANES), BF16))

    kern = functools.partial(_flash_kernel, nheads=nheads, shared_kv=shared_kv, window=window,
                             has_delta=has_delta, has_u=has_u, gate_branch=gate_branch, nblk=nq,
                             fixed_shift=fixed_shift)
    return pl.pallas_call(
        kern,
        grid_spec=pltpu.PrefetchScalarGridSpec(
            num_scalar_prefetch=3,
            grid=(batch, ngroups, nsteps),
            in_specs=in_specs,
            out_specs=pl.BlockSpec((1, TQ, out_w), lambda b, g, s, qt, kt, ft: (b, qt[s], g)),
            scratch_shapes=scratch),
        out_shape=jax.ShapeDtypeStruct((batch, seq, 4 * LANES), BF16),
        compiler_params=_cparams(("parallel", "parallel", "arbitrary")),
        name="flash_" + ("fox" if has_delta else ("sel" if has_u else "win")),
    )(*tabs, *args)


def _cmp_kernel(q_ref, kc_ref, vc_ref, ov_ref, gate_ref, o_ref, u_ref):
    g = pl.program_id(1)
    qi = pl.program_id(2)
    tq = q_ref.shape[2]
    ncmp = kc_ref.shape[2]
    nslc = ov_ref.shape[1]

    t = qi * tq + lax.broadcasted_iota(jnp.int32, (tq, ncmp), 0)
    n = lax.broadcasted_iota(jnp.int32, (tq, ncmp), 1)
    valid = (n * NSA_CMP_STRIDE + (NSA_CMP_LEN - 1)) <= t
    kc = kc_ref[0, 0]
    vc = vc_ref[0, 0]
    ov = ov_ref[...]
    lane = lax.broadcasted_iota(jnp.int32, (tq, LANES), 1)
    lo = lane < HEAD_DIM
    gate_blk = gate_ref[0]

    imp = jnp.zeros((tq, nslc), F32)
    outs = []
    for r in range(NSA_GROUP_HEADS):
        s = lax.dot_general(q_ref[0, r], kc, (((1,), (1,)), ((), ())), preferred_element_type=F32)
        s = jnp.where(valid, s, NEG)
        m = jnp.max(s, axis=1, keepdims=True)
        e = jnp.where(valid, jnp.exp2(s - m), 0.0)
        d = jnp.sum(e, axis=1, keepdims=True)
        pb = (e / jnp.where(d > 0.0, d, 1.0)).astype(BF16)
        o = jnp.dot(pb, vc, preferred_element_type=F32)
        gcol = _gate_column(gate_blk, lane, 3 * (g * NSA_GROUP_HEADS + r))
        outs.append(o * gcol)
        imp = imp + jnp.dot(pb, ov, preferred_element_type=F32)
    for p in range(NSA_GROUP_HEADS // 2):
        o_ref[0, :, p * LANES:(p + 1) * LANES] = jnp.where(lo, outs[2 * p], outs[2 * p + 1]).astype(o_ref.dtype)
    _write_unselected(u_ref, imp, qi, tq, nslc)


def _cmp_fixed_kernel(q_ref, kc_ref, vc_ref, ov_ref, gate_ref, o_ref, u_ref, d_scr, acc_scr, imp_scr,
                      *, chunk):
    g = pl.program_id(1)
    qi = pl.program_id(2)
    tq = q_ref.shape[2]
    ncmp = kc_ref.shape[2]
    nslc = ov_ref.shape[1]
    t0 = qi * tq
    d_scr[...] = jnp.zeros(d_scr.shape, F32)
    acc_scr[...] = jnp.zeros(acc_scr.shape, F32)
    imp_scr[...] = jnp.zeros(imp_scr.shape, F32)

    for c in range(ncmp // chunk):
        first_end = c * chunk * NSA_CMP_STRIDE + NSA_CMP_LEN - 1
        last_end = ((c + 1) * chunk - 1) * NSA_CMP_STRIDE + NSA_CMP_LEN - 1

        def body(masked, c=c):
            rows = slice(c * chunk, (c + 1) * chunk)
            kc = kc_ref[0, 0, rows, :]
            vo = jnp.concatenate([vc_ref[0, 0, rows, :], ov_ref[rows, :]], axis=1)
            if masked:
                t = t0 + lax.broadcasted_iota(jnp.int32, (tq, chunk), 0)
                n = c * chunk + lax.broadcasted_iota(jnp.int32, (tq, chunk), 1)
                valid = (n * NSA_CMP_STRIDE + (NSA_CMP_LEN - 1)) <= t
            for r in range(NSA_GROUP_HEADS):
                s = lax.dot_general(q_ref[0, r], kc, (((1,), (1,)), ((), ())), preferred_element_type=F32)
                if masked:
                    s = jnp.where(valid, s, NEG)
                e = jnp.exp2(s)
                part = e[:, :LANES]
                for j in range(1, chunk // LANES):
                    part = part + e[:, j * LANES:(j + 1) * LANES]
                d_scr[r] = d_scr[r] + part
                both = jnp.dot(e.astype(BF16), vo, preferred_element_type=F32)
                acc_scr[r] = acc_scr[r] + both[:, :LANES]
                imp_scr[r] = imp_scr[r] + both[:, LANES:]

        pl.when((first_end <= t0 + (tq - 1)) & (last_end > t0))(functools.partial(body, True))
        pl.when(last_end <= t0)(functools.partial(body, False))

    lane = lax.broadcasted_iota(jnp.int32, (tq, LANES), 1)
    lo = lane < HEAD_DIM
    gate_blk = gate_ref[0]
    imp = jnp.zeros((tq, nslc), F32)
    outs = []
    for r in range(NSA_GROUP_HEADS):
        d = jnp.sum(d_scr[r], axis=1, keepdims=True)
        inv = 1.0 / jnp.where(d > 0.0, d, 1.0)
        gcol = _gate_column(gate_blk, lane, 3 * (g * NSA_GROUP_HEADS + r))
        outs.append(acc_scr[r] * (inv * gcol))
        imp = imp + imp_scr[r] * inv
    for p in range(NSA_GROUP_HEADS // 2):
        o_ref[0, :, p * LANES:(p + 1) * LANES] = jnp.where(lo, outs[2 * p], outs[2 * p + 1]).astype(o_ref.dtype)
    _write_unselected(u_ref, imp, qi, tq, nslc)


N_FORCED = 3


def _write_unselected(u_ref, imp, qi, tq, nslc):
    tpos = qi * tq + lax.broadcasted_iota(jnp.int32, (tq, nslc), 0)
    jblk = lax.broadcasted_iota(jnp.int32, (tq, nslc), 1)
    cur = jnp.right_shift(tpos, SEL_SHIFT)
    forced = (jblk == 0) | (jblk == cur) | (jblk == cur - 1)
    visible = jblk <= cur
    score_t = jnp.where(visible & jnp.logical_not(forced), imp, -jnp.inf).T

    def run(nrows):
        sc = score_t[:nrows]
        jt = lax.broadcasted_iota(jnp.int32, (nrows, tq), 0).astype(F32)
        for _ in range(NSA_SEL_TOPK - N_FORCED):
            mx = jnp.max(sc, axis=0, keepdims=True)
            first = jnp.min(jnp.where(sc == mx, jt, float(nslc)), axis=0, keepdims=True)
            sc = jnp.where(jt == first, -jnp.inf, sc)
        taken = jnp.where(sc == -jnp.inf, 1.0, 0.0)
        if nrows < nslc:
            taken = jnp.concatenate([taken, jnp.zeros((nslc - nrows, tq), F32)], axis=0)
        chosen = visible & (forced | (taken.T > 0.5))
        u_ref[0, 0] = jnp.where(chosen, 0.0, 1.0).astype(BF16)

    half = nslc // 2
    last_block = jnp.right_shift(qi * tq + (tq - 1), SEL_SHIFT)
    pl.when(last_block < half)(functools.partial(run, half))
    pl.when(last_block >= half)(functools.partial(run, nslc))


def _cmp_select(qn, kc, vc, ov, z, *, batch, seq, fixed_shift):
    ncmp = kc.shape[2]
    nslc = ov.shape[1]
    if fixed_shift:
        chunk = 2 * LANES if ncmp % (2 * LANES) == 0 else ncmp
        kern = functools.partial(_cmp_fixed_kernel, chunk=chunk)
        scratch = [pltpu.VMEM((NSA_GROUP_HEADS, CMP_TQ, LANES), F32) for _ in range(3)]
    else:
        kern, scratch = _cmp_kernel, []
    return pl.pallas_call(
        kern,
        scratch_shapes=scratch,
        grid=(batch, NSA_KV_GROUPS, seq // CMP_TQ),
        in_specs=[pl.BlockSpec((1, NSA_GROUP_HEADS, CMP_TQ, LANES), lambda b, g, i: (b, g, i, 0)),
                  pl.BlockSpec((1, 1, ncmp, LANES), lambda b, g, i: (b, g, 0, 0)),
                  pl.BlockSpec((1, 1, ncmp, LANES), lambda b, g, i: (b, g, 0, 0)),
                  pl.BlockSpec((ncmp, nslc), lambda b, g, i: (0, 0)),
                  pl.BlockSpec((1, CMP_TQ, LANES), lambda b, g, i: (b, i, C_SMALL // LANES))],
        out_specs=[pl.BlockSpec((1, CMP_TQ, 2 * LANES), lambda b, g, i: (b, i, g)),
                   pl.BlockSpec((1, 1, CMP_TQ, nslc), lambda b, g, i: (b, g, i, 0))],
        out_shape=[jax.ShapeDtypeStruct((batch, seq, NSA_WIDTH), BF16),
                   jax.ShapeDtypeStruct((batch, NSA_KV_GROUPS, seq, nslc), BF16)],
        compiler_params=_cparams(("parallel", "parallel", "parallel")),
        name="cmp_select",
    )(qn, kc, vc, ov, z)


def _head_rms(x, g):
    return x * lax.rsqrt(jnp.mean(x * x, axis=-1, keepdims=True) + EPS) * g


def _mem_kernel(zq_ref, kv_ref, qg_ref, kg_ref, o_ref, kn_scr, vb_scr):
    @pl.when(pl.program_id(1) == 0)
    def _prep():
        for h in range(MEM_HEADS):
            kh = kv_ref[0, :, h * MEM_HEAD_DIM:(h + 1) * MEM_HEAD_DIM]
            kn_scr[h] = _head_rms(kh, kg_ref[...]).astype(BF16)
            vb_scr[h] = kv_ref[0, :, MEM_WIDTH + h * MEM_HEAD_DIM:
                               MEM_WIDTH + (h + 1) * MEM_HEAD_DIM].astype(BF16)

    for h in range(MEM_HEADS):
        sl = slice(h * MEM_HEAD_DIM, (h + 1) * MEM_HEAD_DIM)
        qh = (_head_rms(zq_ref[0, :, sl], qg_ref[...]) * ((MEM_HEAD_DIM ** -0.5) * LOG2E)).astype(BF16)
        s = lax.dot_general(qh, kn_scr[h], (((1,), (1,)), ((), ())), preferred_element_type=F32)
        m = jnp.max(s, axis=1, keepdims=True)
        e = jnp.exp2(s - m)
        p = (e / jnp.sum(e, axis=1, keepdims=True)).astype(BF16)
        o_ref[0, :, sl] = jnp.dot(p, vb_scr[h], preferred_element_type=F32).astype(o_ref.dtype)


def _mem_attention(z, mem_kv, q_g, k_g, *, batch, seq):
    mlen = mem_kv.shape[1]
    tq = TQ
    return pl.pallas_call(
        _mem_kernel,
        grid=(batch, seq // tq),
        in_specs=[pl.BlockSpec((1, tq, MEM_WIDTH), lambda b, i: (b, i, C_MEM_Q // MEM_WIDTH)),
                  pl.BlockSpec((1, mlen, 2 * MEM_WIDTH), lambda b, i: (b, 0, 0)),
                  pl.BlockSpec((1, MEM_HEAD_DIM), lambda b, i: (0, 0)),
                  pl.BlockSpec((1, MEM_HEAD_DIM), lambda b, i: (0, 0))],
        out_specs=pl.BlockSpec((1, tq, MEM_WIDTH), lambda b, i: (b, i, 0)),
        out_shape=jax.ShapeDtypeStruct((batch, seq, MEM_WIDTH), BF16),
        scratch_shapes=[pltpu.VMEM((MEM_HEADS, mlen, MEM_HEAD_DIM), BF16),
                        pltpu.VMEM((MEM_HEADS, mlen, MEM_HEAD_DIM), BF16)],
        compiler_params=_cparams(("parallel", "arbitrary")),
        name="mem_attention",
    )(z, mem_kv, q_g.reshape(1, MEM_HEAD_DIM), k_g.reshape(1, MEM_HEAD_DIM))


def _out_kernel(x_ref, oc_ref, os_ref, ow_ref, ob_ref, om_ref, sa_ref, sb_ref, sm_ref,
                g0_ref, g1_ref, g2_ref, wa_ref, wb_ref, wm_ref, wo_ref, y_ref):
    oa = ((oc_ref[...].astype(F32) + os_ref[...].astype(F32) + ow_ref[...].astype(F32))
          * jax.nn.silu(sa_ref[...]))
    ob = ob_ref[...].astype(F32) * jax.nn.silu(sb_ref[...])
    om = om_ref[...].astype(F32) * jax.nn.silu(sm_ref[...])
    u = (jax.nn.sigmoid(g0_ref[...]) * jnp.dot(oa.astype(BF16), wa_ref[...], preferred_element_type=F32)
         + jax.nn.sigmoid(g1_ref[...]) * jnp.dot(ob.astype(BF16), wb_ref[...], preferred_element_type=F32)
         + jax.nn.sigmoid(g2_ref[...]) * jnp.dot(om.astype(BF16), wm_ref[...], preferred_element_type=F32))
    y_ref[...] = x_ref[...] + jnp.dot(u.astype(BF16), wo_ref[...], preferred_element_type=F32)


def _out_proj(x2d, o_c, o_s, o_w, o_b, o_m, z2d, wa, wb, wm, wo):
    m = x2d.shape[0]
    tm = 256
    w512 = 512
    row512 = lambda c: pl.BlockSpec((tm, w512), lambda i, c=c: (i, c))
    row1024 = lambda c: pl.BlockSpec((tm, D_MODEL), lambda i, c=c: (i, c))
    full = lambda shape: pl.BlockSpec(shape, lambda i: (0, 0))
    return pl.pallas_call(
        _out_kernel,
        grid=(m // tm,),
        in_specs=[row1024(0), row512(0), row512(0), row512(0), row512(0), row512(0),
                  row512(C_NSA_SILU // w512), row512(C_FOX_SILU // w512), row512(C_MEM_SILU // w512),
                  row1024(C_MERGE // D_MODEL), row1024(C_MERGE // D_MODEL + 1),
                  row1024(C_MERGE // D_MODEL + 2),
                  full((w512, D_MODEL)), full((w512, D_MODEL)), full((w512, D_MODEL)),
                  full((D_MODEL, D_MODEL))],
        out_specs=row1024(0),
        out_shape=jax.ShapeDtypeStruct((m, D_MODEL), F32),
        compiler_params=_cparams(("parallel",)),
        name="out_proj",
    )(x2d, o_c, o_s, o_w, o_b, o_m, z2d, z2d, z2d, z2d, z2d, z2d, wa, wb, wm, wo)


def _permute_w_in(w_in):
    o = np.cumsum([0, NSA_WIDTH, 6 * NSA_KV_GROUPS * HEAD_DIM, 3 * NSA_HEADS, NSA_WIDTH,
                   3 * FOX_WIDTH, FOX_HEADS, FOX_WIDTH, MEM_WIDTH, MEM_WIDTH, 3 * D_MODEL])
    nsa_q, nsa_kv, nsa_gate, nsa_silu, fox_qkv, fox_f, fox_silu, mem_q, mem_silu, merge = [
        w_in[..., o[i]:o[i + 1]] for i in range(10)]
    pad = jnp.zeros(w_in.shape[:2] + (LANES - 3 * NSA_HEADS - FOX_HEADS,), w_in.dtype)
    return jnp.concatenate([nsa_q, nsa_silu, fox_qkv, fox_silu, mem_q, mem_silu, merge, nsa_kv,
                            nsa_gate, fox_f, pad], axis=-1).astype(BF16)


def _pad_lanes(x, width=LANES):
    return jnp.pad(x, [(0, 0)] * (x.ndim - 1) + [(0, width - x.shape[-1])])


def _tile2(g):
    return jnp.concatenate([g, g], axis=-1)[:, None, :]


def _overlap(n_cmp_pad, n_slc):
    cs = np.arange(n_cmp_pad)[:, None] * NSA_CMP_STRIDE
    ss = np.arange(NSLC_PAD)[None, :] * NSA_SEL_LEN
    ov = np.clip(np.minimum(cs + NSA_CMP_LEN, ss + NSA_SEL_LEN) - np.maximum(cs, ss), 0, None)
    ov = ov / NSA_CMP_LEN
    ov[-1] = 0.0
    ov[:, n_slc:] = 0.0
    return jnp.asarray(ov, BF16)


def _aug_permutations():
    pq = np.zeros((6 * LANES, FOX_HEADS * LANES), np.float32)
    pk = np.zeros((6 * LANES, FOX_HEADS * LANES), np.float32)
    for h in range(FOX_HEADS):
        for i in range(3):
            pq[i * LANES + F_LANE + h, h * LANES + AUG + i] = 1.0
            pk[i * LANES + F_LANE + h, h * LANES + AUG + 3 + i] = -1.0
            pk[(3 + i) * LANES + F_LANE + h, h * LANES + AUG + 6 + i] = -1.0
    return jnp.asarray(pq, BF16), jnp.asarray(pk, BF16)


def _compress_w1(w1):
    nl = w1.shape[0]
    halves = w1.reshape(nl, 2, NSA_CMP_STRIDE, 1, HEAD_DIM, NSA_CMP_HIDDEN)
    eye = jnp.eye(NSA_KV_GROUPS, dtype=w1.dtype)
    out = jnp.einsum("pg,zhldn->zlpdghn", eye, halves[:, :, :, 0])
    return out.reshape(nl, NSA_CMP_STRIDE * NSA_KV_GROUPS * HEAD_DIM,
                       NSA_KV_GROUPS * 2 * NSA_CMP_HIDDEN).astype(BF16)


def _layer(x, mem, p, consts):
    batch, seq, _ = x.shape
    m_rows = batch * seq
    cos4, sin_signed, tri, head_ones, perm_q, perm_k, ov = consts
    z2d = _norm_matmul(x.reshape(m_rows, D_MODEL), p["norm_g"], p["w_in"], 512, Z_WIDTH // 3)
    z = z2d.reshape(batch, seq, Z_WIDTH)

    (qn, kcr, vcr, ks, vs, kw, vw, fqd, fqo, fka, fv, base) = _prep(
        z, cos4, sin_signed, tri, head_ones, perm_q, perm_k,
        p["gq"], p["gks"], p["gkw"], p["gfq"], p["gfk"], p["fb"],
        p["s_sel"], p["s_win"], p["s_fox"], batch=batch, seq=seq)

    nchunk = seq // NSA_CMP_STRIDE
    kc, vc = _compress(kcr.reshape(batch, nchunk, NSA_CMP_STRIDE * LANES),
                       vcr.reshape(batch, nchunk, NSA_CMP_STRIDE * LANES),
                       p["w1k"], p["w1v"], p["bk"], p["bv"], p["w2k"], p["w2v"], p["gkc"], p["s_cmp"],
                       batch=batch)

    def attend(fixed_shift):
        o_c, unsel = _cmp_select(qn, kc, vc, ov, z, batch=batch, seq=seq, fixed_shift=fixed_shift)
        o_s = _flash(qn, ks, vs, batch=batch, seq=seq, shared_kv=True, fixed_shift=fixed_shift,
                     u=unsel, z=z, gate_branch=1)
        o_w = _flash(qn, kw, vw, batch=batch, seq=seq, shared_kv=True, fixed_shift=fixed_shift,
                     window=NSA_WINDOW, z=z, gate_branch=2)
        o_b = _flash(fqd, fka, fv, batch=batch, seq=seq, shared_kv=False, fixed_shift=fixed_shift,
                     base=base.reshape(-1), q_off=fqo)
        return o_c, o_s, o_w, o_b

    o_c, o_s, o_w, o_b = lax.cond(p["bound_ok"], lambda: attend(True), lambda: attend(False))

    mlen = mem.shape[1]
    mem_kv = _norm_matmul(mem.reshape(batch * mlen, D_MODEL), p["mem_norm_g"], p["w_mem_kv"],
                          min(512, batch * mlen), 2 * MEM_WIDTH).reshape(batch, mlen, 2 * MEM_WIDTH)
    o_m = _mem_attention(z, mem_kv, p["mem_q_norm"], p["mem_k_norm"], batch=batch, seq=seq)

    y = _out_proj(x.reshape(m_rows, D_MODEL), o_c.reshape(m_rows, -1), o_s.reshape(m_rows, -1),
                  o_w.reshape(m_rows, -1), o_b.reshape(m_rows, -1), o_m.reshape(m_rows, -1), z2d,
                  p["w_branch_a"], p["w_branch_b"], p["w_branch_m"], p["w_out"])
    return y.reshape(batch, seq, D_MODEL)


def kernel(x, mem, norm_g, mem_norm_g, w_in, nsa_q_norm, nsa_k_norm, cmp_pe_k, cmp_w1_k, cmp_w2_k,
           cmp_pe_v, cmp_w1_v, cmp_w2_v, fox_q_norm, fox_k_norm, fox_f_bias, mem_q_norm, mem_k_norm,
           w_mem_kv, w_branch_a, w_branch_b, w_branch_m, w_out):
    batch, seq, _ = x.shape
    depth = w_in.shape[0]
    n_slc = seq // NSA_SEL_LEN
    assert n_slc <= NSLC_PAD and seq % TQ == 0 and NSA_WINDOW == TQ

    half = HEAD_DIM // 2
    inv_freq = ROPE_THETA ** (-jnp.arange(half, dtype=F32) / half)
    ang = jnp.arange(seq).astype(F32)[:, None] * inv_freq[None, :]
    cos, sin = jnp.cos(ang), jnp.sin(ang)
    cos4 = jnp.concatenate([cos, cos, cos, cos], axis=-1)
    sin_signed = jnp.concatenate([-sin, sin, -sin, sin], axis=-1)
    tri = jnp.asarray(np.tril(np.ones((TQ, TQ), np.float32)), BF16)
    head_ones = jnp.asarray(np.kron(np.eye(2), np.ones((HEAD_DIM, HEAD_DIM))), BF16)
    consts = (cos4, sin_signed, tri, head_ones, *_aug_permutations(),
              _overlap(seq // NSA_CMP_STRIDE, n_slc))

    hp = lax.Precision.HIGHEST
    pe_bias = lambda pe, w1: jnp.einsum("lk,lkn->ln", pe.reshape(depth, -1), w1, precision=hp)[:, None, :]
    fb = jnp.zeros((depth, 1, LANES), F32).at[:, 0, F_LANE:F_LANE + FOX_HEADS].set(fox_f_bias)

    def logit_bound(gq, gk):
        bound = (BOUND_MARGIN * HEAD_DIM * QK_SCALE
                 * jnp.max(jnp.abs(gq), axis=-1) * jnp.max(jnp.abs(gk), axis=-1))
        return bound.astype(BF16).astype(F32)

    bounds = [logit_bound(nsa_q_norm, nsa_k_norm[:, 1]), logit_bound(nsa_q_norm, nsa_k_norm[:, 2]),
              logit_bound(fox_q_norm, fox_k_norm), logit_bound(nsa_q_norm, nsa_k_norm[:, 0])]
    bound_ok = functools.reduce(jnp.maximum, bounds) <= MAX_FIXED_BOUND
    bound_row = lambda s: jnp.broadcast_to(s[:, None, None], (depth, 1, LANES))
    stacked = dict(
        s_sel=bound_row(bounds[0]), s_win=bound_row(bounds[1]), s_fox=bound_row(bounds[2]),
        s_cmp=bound_row(bounds[3]),
        bound_ok=bound_ok,
        norm_g=norm_g, mem_norm_g=mem_norm_g, w_in=_permute_w_in(w_in),
        gq=_tile2(nsa_q_norm), gks=_tile2(nsa_k_norm[:, 1]), gkw=_tile2(nsa_k_norm[:, 2]),
        gkc=_pad_lanes(nsa_k_norm[:, 0])[:, None, :], gfq=_tile2(fox_q_norm), gfk=_tile2(fox_k_norm),
        fb=fb, w1k=_compress_w1(cmp_w1_k), w1v=_compress_w1(cmp_w1_v),
        bk=pe_bias(cmp_pe_k, cmp_w1_k), bv=pe_bias(cmp_pe_v, cmp_w1_v),
        w2k=_pad_lanes(cmp_w2_k).astype(BF16),
        w2v=jnp.concatenate([cmp_w2_v, cmp_w2_v], axis=-1).astype(BF16),
        mem_q_norm=mem_q_norm, mem_k_norm=mem_k_norm, w_mem_kv=w_mem_kv.astype(BF16),
        w_branch_a=w_branch_a.astype(BF16), w_branch_b=w_branch_b.astype(BF16),
        w_branch_m=w_branch_m.astype(BF16), w_out=w_out.astype(BF16))
    for l in range(depth):
        x = _layer(x, mem, {k: v[l] for k, v in stacked.items()}, consts)
    return x
```

```python
import functools

import numpy as np
import jax
import jax.numpy as jnp
from jax import lax
from jax.experimental import pallas as pl
from jax.experimental.pallas import tpu as pltpu

F32 = jnp.float32
BF16 = jnp.bfloat16

D_MODEL = 1024
HEAD_DIM = 64
ROPE_THETA = 10000.0
EPS = 1e-6
NSA_HEADS = 8
NSA_KV_GROUPS = 2
NSA_GROUP_HEADS = NSA_HEADS // NSA_KV_GROUPS
NSA_CMP_LEN = 32
NSA_CMP_STRIDE = 16
NSA_CMP_HIDDEN = 128
NSA_SEL_LEN = 64
NSA_SEL_TOPK = 16
NSA_WINDOW = 512
FORCE_SCORE = 1e9
NSA_WIDTH = NSA_HEADS * HEAD_DIM
FOX_HEADS = 8
FOX_WIDTH = FOX_HEADS * HEAD_DIM
MEM_HEADS = 4
MEM_HEAD_DIM = 128
MEM_WIDTH = MEM_HEADS * MEM_HEAD_DIM

LANES = 128
VMEM_LIMIT = 48 * 1024 * 1024
NEG = -1e30
SEL_MASK = 32768.0
SEL_SHIFT = 6
NSLC_PAD = LANES
LOG2E = 1.4426950408889634
QK_SCALE = (HEAD_DIM ** -0.5) * LOG2E
BOUND_MARGIN = 1.02
MAX_FIXED_BOUND = 60.0
ZERO_WEIGHT_LOG2 = -160.0

C_NSA_Q = 0
C_NSA_SILU = 512
C_FOX_Q = 1024
C_FOX_K = 1536
C_FOX_V = 2048
C_FOX_SILU = 2560
C_MEM_Q = 3072
C_MEM_SILU = 3584
C_MERGE = 4096
C_NSA_KV = 7168
C_SMALL = 7936
Z_WIDTH = 8064
F_LANE = 3 * NSA_HEADS
AUG = HEAD_DIM

TQ = 512
CMP_TQ = 256


def _cparams(sem):
    return pltpu.CompilerParams(dimension_semantics=sem, vmem_limit_bytes=VMEM_LIMIT)


def _norm_matmul_kernel(x_ref, g_ref, w_ref, o_ref):
    x = x_ref[...]
    ms = jnp.mean(x * x, axis=-1, keepdims=True)
    h = (x * lax.rsqrt(ms + EPS) * g_ref[...]).astype(BF16)
    o_ref[...] = jnp.dot(h, w_ref[...], preferred_element_type=F32)


def _norm_matmul(x2d, g, w, tm, tn):
    m, k = x2d.shape
    n = w.shape[1]
    return pl.pallas_call(
        _norm_matmul_kernel,
        grid=(n // tn, m // tm),
        in_specs=[pl.BlockSpec((tm, k), lambda j, i: (i, 0)),
                  pl.BlockSpec((1, k), lambda j, i: (0, 0)),
                  pl.BlockSpec((k, tn), lambda j, i: (0, j))],
        out_specs=pl.BlockSpec((tm, tn), lambda j, i: (i, j)),
        out_shape=jax.ShapeDtypeStruct((m, n), F32),
        compiler_params=_cparams(("parallel", "parallel")),
        name="norm_matmul",
    )(x2d, g.reshape(1, k), w)


def _pair_rms(x, g2, head_ones):
    sq = x * x
    hi = sq.astype(BF16)
    low = (sq - hi.astype(F32)).astype(BF16)
    ssq = (jnp.dot(hi, head_ones, preferred_element_type=F32)
           + jnp.dot(low, head_ones, preferred_element_type=F32))
    return x * lax.rsqrt(ssq * (1.0 / HEAD_DIM) + EPS) * g2


def _pair_rope(y, cos4, sin_signed, first_half):
    rot = jnp.where(first_half, pltpu.roll(y, LANES - HEAD_DIM // 2, 1), pltpu.roll(y, HEAD_DIM // 2, 1))
    return y * cos4 + rot * sin_signed


def _head_a(y, lo, tail=0.0):
    return jnp.where(lo, y, tail)


def _head_b(y, lo, tail=0.0):
    return jnp.where(lo, pltpu.roll(y, HEAD_DIM, 1), tail)


def _split3_f32(c):
    hi = c.astype(BF16).astype(F32)
    r = c - hi
    mid = r.astype(BF16).astype(F32)
    return hi, mid, r - mid


def _prep_kernel(zq_ref, zc_ref, zs_ref, zw_ref, fq_ref, fk_ref, fv_ref, sm_ref, cos_ref, sin_ref,
                 tri_ref, ones_ref, pq_ref, pk_ref, gq_ref, gks_ref, gkw_ref, gfq_ref, gfk_ref, fb_ref,
                 ss_ref, sw_ref, sf_ref,
                 qn_ref, kcr_ref, vcr_ref, ks_ref, vs_ref, kw_ref, vw_ref, fqd_ref, fqo_ref, fka_ref,
                 fvb_ref, base_ref, run_scr):
    ti = pl.program_id(1)
    tt = zq_ref.shape[1]
    lane = lax.broadcasted_iota(jnp.int32, (tt, LANES), 1)
    lo = lane < HEAD_DIM
    first_half = (lane & (HEAD_DIM // 2)) == 0
    cos4 = cos_ref[...]
    sin_signed = sin_ref[...]
    head_ones = ones_ref[...]
    one_at_aug = jnp.where(lane == AUG, 1.0, 0.0)

    for p in range(NSA_HEADS // 2):
        y = _pair_rope(_pair_rms(zq_ref[0, :, p * LANES:(p + 1) * LANES], gq_ref[...], head_ones),
                       cos4, sin_signed, first_half) * QK_SCALE
        qn_ref[0, 2 * p] = _head_a(y, lo, one_at_aug).astype(BF16)
        qn_ref[0, 2 * p + 1] = _head_b(y, lo, one_at_aug).astype(BF16)

    kcr_ref[0] = _pair_rope(zc_ref[0, :, :LANES], cos4, sin_signed, first_half).astype(BF16)
    vcr_ref[0] = zc_ref[0, :, LANES:].astype(BF16)

    sel_lane = jnp.right_shift(ti * tt + lax.broadcasted_iota(jnp.int32, (tt, LANES), 0), SEL_SHIFT)
    onehot = jnp.where(lane == sel_lane, -SEL_MASK, 0.0).astype(BF16)
    for z_ref, g_ref, k_ref, v_ref, bound_ref, with_onehot in (
            (zs_ref, gks_ref, ks_ref, vs_ref, ss_ref, True),
            (zw_ref, gkw_ref, kw_ref, vw_ref, sw_ref, False)):
        y = _pair_rope(_pair_rms(z_ref[0, :, :LANES], g_ref[...], head_ones), cos4, sin_signed, first_half)
        v = z_ref[0, :, LANES:]
        vr = pltpu.roll(v, HEAD_DIM, 1)
        neg_bound = jnp.where(lane == AUG, -bound_ref[...], 0.0)
        k_ref[0, 0, :, :LANES] = _head_a(y, lo, neg_bound).astype(BF16)
        k_ref[0, 1, :, :LANES] = _head_b(y, lo, neg_bound).astype(BF16)
        if with_onehot:
            k_ref[0, 0, :, LANES:] = onehot
            k_ref[0, 1, :, LANES:] = onehot
        v_ref[0, 0] = jnp.where(lo, v, vr).astype(BF16)
        v_ref[0, 1] = jnp.where(lo, vr, v).astype(BF16)

    @pl.when(ti == 0)
    def _reset():
        run_scr[...] = jnp.zeros(run_scr.shape, F32)

    xf = sm_ref[0] + fb_ref[...]
    log_f = (jnp.minimum(xf, 0.0) - jnp.log1p(jnp.exp(-jnp.abs(xf)))) * LOG2E
    tri = tri_ref[...]
    parts = _split3_f32(log_f)
    cum = sum(jnp.dot(tri, part.astype(BF16), preferred_element_type=F32) for part in parts)
    base_ref[0, 0] = run_scr[...]
    run_scr[...] = run_scr[...] + cum[tt - 1:tt, :]
    fvb_ref[0] = fv_ref[0].astype(BF16)

    parts6 = _split3_f32(cum) + _split3_f32(cum - cum[tt - 1:tt, :])
    x6 = jnp.concatenate([part.astype(BF16) for part in parts6], axis=1)
    aug_q = jnp.dot(x6, pq_ref[...], preferred_element_type=F32)
    aug_k = jnp.dot(x6, pk_ref[...], preferred_element_type=F32)

    lane1 = lax.broadcasted_iota(jnp.int32, (1, LANES), 1)
    span = lambda a, b: jnp.where((lane1 >= a) & (lane1 < b), 1.0, 0.0)
    q_tail_diag = span(AUG + 3, AUG + 6) + span(AUG + 9, AUG + 10)
    q_tail_off = span(AUG + 6, AUG + 10)
    k_tail = span(AUG, AUG + 3) + jnp.where(lane1 == AUG + 9, -sf_ref[...], 0.0)
    for p in range(FOX_HEADS // 2):
        sl = slice(p * LANES, (p + 1) * LANES)
        yq = _pair_rms(fq_ref[0, :, sl], gfq_ref[...], head_ones) * QK_SCALE
        yk = _pair_rms(fk_ref[0, :, sl], gfk_ref[...], head_ones)
        for e, split in ((0, _head_a), (1, _head_b)):
            h = 2 * p + e
            hs = slice(h * LANES, (h + 1) * LANES)
            qh = split(yq, lo) + aug_q[:, hs]
            fqd_ref[0, h] = (qh + q_tail_diag).astype(BF16)
            fqo_ref[0, h] = (qh + q_tail_off).astype(BF16)
            fka_ref[0, h] = (split(yk, lo) + aug_k[:, hs] + k_tail).astype(BF16)


def _prep(z, cos4, sin_signed, tri, head_ones, perm_q, perm_k, gq, gks, gkw, gfq, gfk, fb,
          s_sel, s_win, s_fox, *, batch, seq):
    nblk = seq // TQ
    whole = lambda a: pl.BlockSpec(a.shape, lambda b, i: (0, 0))
    zspec = lambda w, c: pl.BlockSpec((1, TQ, w), lambda b, i, c=c: (b, i, c))
    row = pl.BlockSpec((1, LANES), lambda b, i: (0, 0))
    tab = pl.BlockSpec((TQ, LANES), lambda b, i: (i, 0))
    heads = lambda n, w: pl.BlockSpec((1, n, TQ, w), lambda b, i: (b, 0, i, 0))
    tok = lambda w: pl.BlockSpec((1, TQ, w), lambda b, i: (b, i, 0))
    bf = lambda *shape: jax.ShapeDtypeStruct(shape, BF16)
    return pl.pallas_call(
        _prep_kernel,
        grid=(batch, nblk),
        in_specs=[zspec(NSA_WIDTH, C_NSA_Q // NSA_WIDTH),
                  zspec(2 * LANES, C_NSA_KV // (2 * LANES)),
                  zspec(2 * LANES, C_NSA_KV // (2 * LANES) + 1),
                  zspec(2 * LANES, C_NSA_KV // (2 * LANES) + 2),
                  zspec(FOX_WIDTH, C_FOX_Q // FOX_WIDTH),
                  zspec(FOX_WIDTH, C_FOX_K // FOX_WIDTH),
                  zspec(FOX_WIDTH, C_FOX_V // FOX_WIDTH),
                  zspec(LANES, C_SMALL // LANES),
                  tab, tab, whole(tri), whole(head_ones), whole(perm_q), whole(perm_k),
                  row, row, row, row, row, row, row, row, row],
        out_specs=[heads(NSA_HEADS, LANES), tok(LANES), tok(LANES),
                   heads(NSA_KV_GROUPS, 2 * LANES), heads(NSA_KV_GROUPS, LANES),
                   heads(NSA_KV_GROUPS, LANES), heads(NSA_KV_GROUPS, LANES),
                   heads(FOX_HEADS, LANES), heads(FOX_HEADS, LANES), heads(FOX_HEADS, LANES),
                   tok(FOX_WIDTH), pl.BlockSpec((1, 1, 1, LANES), lambda b, i: (b, i, 0, 0))],
        out_shape=[bf(batch, NSA_HEADS, seq, LANES), bf(batch, seq, LANES), bf(batch, seq, LANES),
                   bf(batch, NSA_KV_GROUPS, seq, 2 * LANES), bf(batch, NSA_KV_GROUPS, seq, LANES),
                   bf(batch, NSA_KV_GROUPS, seq, LANES), bf(batch, NSA_KV_GROUPS, seq, LANES),
                   bf(batch, FOX_HEADS, seq, LANES), bf(batch, FOX_HEADS, seq, LANES),
                   bf(batch, FOX_HEADS, seq, LANES), bf(batch, seq, FOX_WIDTH),
                   jax.ShapeDtypeStruct((batch, nblk, 1, LANES), F32)],
        scratch_shapes=[pltpu.VMEM((1, LANES), F32)],
        compiler_params=_cparams(("parallel", "arbitrary")),
        name="prep",
    )(z, z, z, z, z, z, z, z, cos4, sin_signed, tri, head_ones, perm_q, perm_k,
      gq, gks, gkw, gfq, gfk, fb, s_sel, s_win, s_fox)


def _compress_kernel(xk_ref, xv_ref, w1k_ref, w1v_ref, bk_ref, bv_ref, w2k_ref, w2v_ref, kg_ref,
                     sc_ref, kc_ref, vc_ref):
    nchunk = xk_ref.shape[1]
    lane = lax.broadcasted_iota(jnp.int32, (nchunk, LANES), 1)
    for x_ref, w1_ref, b_ref, w2_ref, o_ref, is_key in ((xk_ref, w1k_ref, bk_ref, w2k_ref, kc_ref, True),
                                                        (xv_ref, w1v_ref, bv_ref, w2v_ref, vc_ref, False)):
        h = jnp.dot(x_ref[0], w1_ref[...], preferred_element_type=F32)
        for g in range(NSA_KV_GROUPS):
            c0 = g * 2 * NSA_CMP_HIDDEN
            top = h[:, c0:c0 + NSA_CMP_HIDDEN]
            bot = h[:, c0 + NSA_CMP_HIDDEN:c0 + 2 * NSA_CMP_HIDDEN]
            hid = top + pltpu.roll(bot, nchunk - 1, 0) + b_ref[...]
            act = (hid * jax.nn.sigmoid(hid)).astype(BF16)
            o = jnp.dot(act, w2_ref[...], preferred_element_type=F32)
            if is_key:
                ms = jnp.sum(o * o, axis=1, keepdims=True) * (1.0 / HEAD_DIM)
                o = o * lax.rsqrt(ms + EPS) * kg_ref[...]
                o = jnp.where(lane == AUG, -sc_ref[...], o)
            o_ref[0, g] = o.astype(BF16)


def _compress(xk, xv, w1k, w1v, bk, bv, w2k, w2v, kg, s_cmp, *, batch):
    nchunk, kin = xk.shape[1], xk.shape[2]
    xspec = pl.BlockSpec((1, nchunk, kin), lambda b: (b, 0, 0))
    full = lambda a: pl.BlockSpec(a.shape, lambda b: (0,) * a.ndim)
    ospec = pl.BlockSpec((1, NSA_KV_GROUPS, nchunk, LANES), lambda b: (b, 0, 0, 0))
    oshape = jax.ShapeDtypeStruct((batch, NSA_KV_GROUPS, nchunk, LANES), BF16)
    return pl.pallas_call(
        _compress_kernel,
        grid=(batch,),
        in_specs=[xspec, xspec, full(w1k), full(w1v), full(bk), full(bv), full(w2k), full(w2v), full(kg),
                  full(s_cmp)],
        out_specs=[ospec, ospec],
        out_shape=[oshape, oshape],
        compiler_params=_cparams(("parallel",)),
        name="compress",
    )(xk, xv, w1k, w1v, bk, bv, w2k, w2v, kg, s_cmp)


def _gate_column(gate_blk, lane, idx):
    col = jnp.sum(jnp.where(lane == idx, gate_blk, 0.0), axis=1, keepdims=True)
    return jax.nn.sigmoid(col)


def _flash_kernel(qi_tab, ki_tab, fl_tab, *refs, nheads, shared_kv, window, has_delta,
                  has_u, gate_branch, nblk, fixed_shift):
    refs = list(refs)
    base_ref = refs.pop(0) if has_delta else None
    q_ref = refs.pop(0)
    qoff_ref = refs.pop(0) if has_delta else None
    u_ref = refs.pop(0) if has_u else None
    k_ref = refs.pop(0)
    v_ref = refs.pop(0)
    gate_ref = refs.pop(0) if gate_branch is not None else None
    o_ref = refs.pop(0)
    m_scr = refs.pop(0)
    l_scr = refs.pop(0)
    acc_scr = refs.pop(0)
    qcat_scr = refs.pop(0) if has_u else None

    b = pl.program_id(0)
    g = pl.program_id(1)
    step = b * pl.num_programs(2) + pl.program_id(2)
    qi = qi_tab[step]
    ki = ki_tab[step]
    fl = fl_tab[step]
    active = (fl & 4) == 4
    tq = q_ref.shape[2]
    tk = k_ref.shape[2]

    @pl.when((fl & 1) == 1)
    def _init():
        m_scr[...] = jnp.full(m_scr.shape, NEG, F32)
        l_scr[...] = jnp.zeros(l_scr.shape, F32)
        acc_scr[...] = jnp.zeros(acc_scr.shape, F32)
        if has_u:
            for r in range(nheads):
                qcat_scr[r, :, :LANES] = q_ref[0, r]
                qcat_scr[r, :, LANES:] = u_ref[0, r // NSA_GROUP_HEADS]

    lane = lax.broadcasted_iota(jnp.int32, (tq, LANES), 1)
    lo = lane < HEAD_DIM

    def tile(diagonal):
        masked = diagonal or window is not None
        if masked:
            row = lax.broadcasted_iota(jnp.int32, (tq, tk), 0)
            col = lax.broadcasted_iota(jnp.int32, (tq, tk), 1)
            if diagonal:
                valid = row >= col
            else:
                valid = (tq + row - col) < window
        for p in range(nheads // 2):
            pvs, alphas = [], []
            for e in range(2):
                hh = 2 * p + e
                if has_u:
                    q = qcat_scr[hh]
                elif has_delta and not diagonal:
                    q = qoff_ref[0, hh]
                else:
                    q = q_ref[0, hh]
                k = k_ref[0, hh // NSA_GROUP_HEADS] if shared_kv else k_ref[0, hh]
                v = v_ref[0, hh // NSA_GROUP_HEADS] if shared_kv else v_ref[0, :, p * LANES:(p + 1) * LANES]
                s = lax.dot_general(q, k, (((1,), (1,)), ((), ())), preferred_element_type=F32)
                if masked:
                    s = jnp.where(valid, s, NEG)
                delta = None
                if has_delta and not diagonal:
                    hoff = F_LANE + g * nheads + hh
                    delta = (base_ref[(b * nblk + qi) * LANES + hoff]
                             - base_ref[(b * nblk + ki + 1) * LANES + hoff])
                if fixed_shift:
                    pm = jnp.exp2(s)
                    rowsum = pm[:, :LANES]
                    for j in range(1, tk // LANES):
                        rowsum = rowsum + pm[:, j * LANES:(j + 1) * LANES]
                    pv = jnp.dot(pm.astype(BF16), v, preferred_element_type=F32)
                    if delta is not None:
                        w = jnp.exp2(jnp.full((1, LANES), delta, F32))
                        rowsum = rowsum * w
                        pv = pv * w
                    l_scr[hh] = l_scr[hh] + rowsum
                    pvs.append(pv)
                else:
                    m_prev = m_scr[hh]
                    m_cur = jnp.max(s, axis=1, keepdims=True)
                    if delta is not None:
                        m_cur = m_cur + delta
                    m_new = jnp.maximum(m_prev, m_cur)
                    alpha = jnp.exp2(m_prev - m_new)
                    shift = m_new[:, :1]
                    if delta is not None:
                        shift = shift - delta
                    pm = jnp.exp2(s - shift)
                    l_scr[hh] = alpha * l_scr[hh] + jnp.sum(pm, axis=1, keepdims=True)
                    m_scr[hh] = m_new
                    pvs.append(jnp.dot(pm.astype(BF16), v, preferred_element_type=F32))
                    alphas.append(alpha)
            if fixed_shift:
                acc_scr[p] = acc_scr[p] + jnp.where(lo, pvs[0], pvs[1])
            else:
                acc_scr[p] = (acc_scr[p] * jnp.where(lo, alphas[0], alphas[1])
                              + jnp.where(lo, pvs[0], pvs[1]))

    pl.when((qi == ki) & active)(lambda: tile(True))
    pl.when((qi != ki) & active)(lambda: tile(False))

    @pl.when((fl & 2) == 2)
    def _finish():
        for p in range(nheads // 2):
            if fixed_shift:
                la = jnp.sum(l_scr[2 * p], axis=1, keepdims=True)
                lb = jnp.sum(l_scr[2 * p + 1], axis=1, keepdims=True)
            else:
                la, lb = l_scr[2 * p], l_scr[2 * p + 1]
            inv = jnp.where(lo, 1.0 / la, 1.0 / lb)
            if gate_branch is not None:
                gate_blk = gate_ref[0]
                h0 = g * nheads + 2 * p
                ga = _gate_column(gate_blk, lane, 3 * h0 + gate_branch)
                gb = _gate_column(gate_blk, lane, 3 * (h0 + 1) + gate_branch)
                inv = inv * jnp.where(lo, ga, gb)
            o_ref[0, :, p * LANES:(p + 1) * LANES] = (acc_scr[p] * inv).astype(o_ref.dtype)


def _tile_tables(nq, window_tiles, batch):
    qi, ki, fl = [], [], []
    for i in range(nq):
        lo = 0 if window_tiles is None else max(0, i - window_tiles)
        for j in range(lo, i + 1):
            qi.append(i)
            ki.append(j)
            fl.append((1 if j == lo else 0) | (2 if j == i else 0) | 4)
    rows = lambda a: jnp.tile(jnp.asarray(a, jnp.int32), batch)
    return rows(qi), rows(ki), rows(fl)


def _decayed_tile_tables(nq, batch, base):
    qi = np.repeat(np.arange(nq), np.arange(1, nq + 1))
    ki = np.concatenate([np.arange(i + 1) for i in range(nq)])
    nsteps = qi.shape[0]
    diag = jnp.asarray(qi == ki)
    gate = base.reshape(batch, nq, LANES)[:, :, F_LANE:F_LANE + FOX_HEADS]
    delta = gate[:, qi, :] - gate[:, np.minimum(ki + 1, nq - 1), :]
    active = diag[None, :] | (jnp.max(delta, axis=-1) > ZERO_WEIGHT_LOG2)
    same_q = jnp.asarray(qi[:, None] == np.arange(nq)[None, :])
    first_ki = jnp.min(jnp.where(active[:, :, None] & same_q[None], ki[None, :, None], nq), axis=1)
    first = active & (jnp.asarray(ki)[None, :] == first_ki[:, qi])
    flags = first.astype(jnp.int32) + 2 * diag[None, :].astype(jnp.int32) + 4 * active.astype(jnp.int32)
    order = jnp.argsort(jnp.logical_not(active), axis=1, stable=True)
    n_active = jnp.sum(active, axis=1, keepdims=True)
    slot = jnp.arange(nsteps)[None, :]
    src = jnp.take_along_axis(order, jnp.minimum(slot, n_active - 1), axis=1)
    kept = slot < n_active
    take = lambda a: jnp.asarray(a, jnp.int32)[src].reshape(-1)
    fl = jnp.where(kept, jnp.take_along_axis(flags, src, axis=1), 0).reshape(-1)
    return take(qi), take(ki), fl.astype(jnp.int32)


def _flash(q, k, v, *, batch, seq, shared_kv, fixed_shift, window=None, base=None, q_off=None,
           u=None, z=None, gate_branch=None):
    nheads = NSA_HEADS
    nq = seq // TQ
    ngroups = NSA_HEADS // nheads
    nkv = NSA_KV_GROUPS
    da = k.shape[-1]
    has_delta = base is not None
    has_u = u is not None
    out_w = LANES * (nheads // 2)
    if has_delta and fixed_shift:
        tabs = _decayed_tile_tables(nq, batch, base)
    else:
        tabs = _tile_tables(nq, None if window is None else window // TQ, batch)
    nsteps = int(tabs[0].shape[0]) // batch

    by_q_heads = lambda b, g, s, qt, kt, ft: (b, g, qt[b * nsteps + s], 0)
    by_k_heads = lambda b, g, s, qt, kt, ft: (b, g, kt[b * nsteps + s], 0)
    by_q_tokens = lambda col: (lambda b, g, s, qt, kt, ft: (b, qt[b * nsteps + s], g if col is None else col))
    by_k_tokens = lambda b, g, s, qt, kt, ft: (b, kt[b * nsteps + s], g)

    in_specs, args = [], []
    if has_delta:
        in_specs.append(pl.BlockSpec(memory_space=pltpu.SMEM))
        args.append(base)
    in_specs.append(pl.BlockSpec((1, nheads, TQ, LANES), by_q_heads))
    args.append(q)
    if has_delta:
        in_specs.append(pl.BlockSpec((1, nheads, TQ, LANES), by_q_heads))
        args.append(q_off)
    if has_u:
        in_specs.append(pl.BlockSpec((1, nkv, TQ, LANES), by_q_heads))
        args.append(u)
    if shared_kv:
        in_specs.append(pl.BlockSpec((1, nkv, TQ, da), by_k_heads))
        in_specs.append(pl.BlockSpec((1, nkv, TQ, LANES), by_k_heads))
    else:
        in_specs.append(pl.BlockSpec((1, nheads, TQ, da), by_k_heads))
        in_specs.append(pl.BlockSpec((1, TQ, out_w), by_k_tokens))
    args += [k, v]
    if gate_branch is not None:
        in_specs.append(pl.BlockSpec((1, TQ, LANES), by_q_tokens(C_SMALL // LANES)))
        args.append(z)

    scratch = [pltpu.VMEM((nheads, TQ, LANES), F32),
               pltpu.VMEM((nheads, TQ, LANES), F32),
               pltpu.VMEM((nheads // 2, TQ, LANES), F32)]
    if has_u:
        scratch.append(pltpu.VMEM((nheads, TQ, 2 * LANES), BF16))

    kern = functools.partial(_flash_kernel, nheads=nheads, shared_kv=shared_kv, window=window,
                             has_delta=has_delta, has_u=has_u, gate_branch=gate_branch, nblk=nq,
                             fixed_shift=fixed_shift)
    return pl.pallas_call(
        kern,
        grid_spec=pltpu.PrefetchScalarGridSpec(
            num_scalar_prefetch=3,
            grid=(batch, ngroups, nsteps),
            in_specs=in_specs,
            out_specs=pl.BlockSpec((1, TQ, out_w), by_q_tokens(None)),
            scratch_shapes=scratch),
        out_shape=jax.ShapeDtypeStruct((batch, seq, 4 * LANES), BF16),
        compiler_params=_cparams(("parallel", "parallel", "arbitrary")),
        name="flash_" + ("fox" if has_delta else ("sel" if has_u else "win")),
    )(*tabs, *args)


def _cmp_kernel(q_ref, kc_ref, vc_ref, ov_ref, gate_ref, o_ref, u_ref):
    g = pl.program_id(1)
    qi = pl.program_id(2)
    tq = q_ref.shape[2]
    ncmp = kc_ref.shape[2]
    nslc = ov_ref.shape[1]

    t = qi * tq + lax.broadcasted_iota(jnp.int32, (tq, ncmp), 0)
    n = lax.broadcasted_iota(jnp.int32, (tq, ncmp), 1)
    valid = (n * NSA_CMP_STRIDE + (NSA_CMP_LEN - 1)) <= t
    kc = kc_ref[0, 0]
    vc = vc_ref[0, 0]
    ov = ov_ref[...]
    lane = lax.broadcasted_iota(jnp.int32, (tq, LANES), 1)
    lo = lane < HEAD_DIM
    gate_blk = gate_ref[0]

    imp = jnp.zeros((tq, nslc), F32)
    outs = []
    for r in range(NSA_GROUP_HEADS):
        s = lax.dot_general(q_ref[0, r], kc, (((1,), (1,)), ((), ())), preferred_element_type=F32)
        s = jnp.where(valid, s, NEG)
        m = jnp.max(s, axis=1, keepdims=True)
        e = jnp.where(valid, jnp.exp2(s - m), 0.0)
        d = jnp.sum(e, axis=1, keepdims=True)
        pb = (e / jnp.where(d > 0.0, d, 1.0)).astype(BF16)
        o = jnp.dot(pb, vc, preferred_element_type=F32)
        gcol = _gate_column(gate_blk, lane, 3 * (g * NSA_GROUP_HEADS + r))
        outs.append(o * gcol)
        imp = imp + jnp.dot(pb, ov, preferred_element_type=F32)
    for p in range(NSA_GROUP_HEADS // 2):
        o_ref[0, :, p * LANES:(p + 1) * LANES] = jnp.where(lo, outs[2 * p], outs[2 * p + 1]).astype(o_ref.dtype)
    _write_unselected(u_ref, imp, qi, tq, nslc)


def _cmp_fixed_kernel(q_ref, kc_ref, vc_ref, ov_ref, gate_ref, o_ref, u_ref, d_scr, acc_scr, imp_scr,
                      *, chunk):
    g = pl.program_id(1)
    qi = pl.program_id(2)
    tq = q_ref.shape[2]
    ncmp = kc_ref.shape[2]
    nslc = ov_ref.shape[1]
    t0 = qi * tq
    d_scr[...] = jnp.zeros(d_scr.shape, F32)
    acc_scr[...] = jnp.zeros(acc_scr.shape, F32)
    imp_scr[...] = jnp.zeros(imp_scr.shape, F32)

    for c in range(ncmp // chunk):
        first_end = c * chunk * NSA_CMP_STRIDE + NSA_CMP_LEN - 1
        last_end = ((c + 1) * chunk - 1) * NSA_CMP_STRIDE + NSA_CMP_LEN - 1

        def body(masked, c=c):
            rows = slice(c * chunk, (c + 1) * chunk)
            kc = kc_ref[0, 0, rows, :]
            vo = jnp.concatenate([vc_ref[0, 0, rows, :], ov_ref[rows, :]], axis=1)
            if masked:
                t = t0 + lax.broadcasted_iota(jnp.int32, (tq, chunk), 0)
                n = c * chunk + lax.broadcasted_iota(jnp.int32, (tq, chunk), 1)
                valid = (n * NSA_CMP_STRIDE + (NSA_CMP_LEN - 1)) <= t
            for r in range(NSA_GROUP_HEADS):
                s = lax.dot_general(q_ref[0, r], kc, (((1,), (1,)), ((), ())), preferred_element_type=F32)
                if masked:
                    s = jnp.where(valid, s, NEG)
                e = jnp.exp2(s)
                part = e[:, :LANES]
                for j in range(1, chunk // LANES):
                    part = part + e[:, j * LANES:(j + 1) * LANES]
                d_scr[r] = d_scr[r] + part
                both = jnp.dot(e.astype(BF16), vo, preferred_element_type=F32)
                acc_scr[r] = acc_scr[r] + both[:, :LANES]
                imp_scr[r] = imp_scr[r] + both[:, LANES:]

        pl.when((first_end <= t0 + (tq - 1)) & (last_end > t0))(functools.partial(body, True))
        pl.when(last_end <= t0)(functools.partial(body, False))

    lane = lax.broadcasted_iota(jnp.int32, (tq, LANES), 1)
    lo = lane < HEAD_DIM
    gate_blk = gate_ref[0]
    imp = jnp.zeros((tq, nslc), F32)
    outs = []
    for r in range(NSA_GROUP_HEADS):
        d = jnp.sum(d_scr[r], axis=1, keepdims=True)
        inv = 1.0 / jnp.where(d > 0.0, d, 1.0)
        gcol = _gate_column(gate_blk, lane, 3 * (g * NSA_GROUP_HEADS + r))
        outs.append(acc_scr[r] * (inv * gcol))
        imp = imp + imp_scr[r] * inv
    for p in range(NSA_GROUP_HEADS // 2):
        o_ref[0, :, p * LANES:(p + 1) * LANES] = jnp.where(lo, outs[2 * p], outs[2 * p + 1]).astype(o_ref.dtype)
    _write_unselected(u_ref, imp, qi, tq, nslc)


N_FORCED = 3


def _write_unselected(u_ref, imp, qi, tq, nslc):
    tpos = qi * tq + lax.broadcasted_iota(jnp.int32, (tq, nslc), 0)
    jblk = lax.broadcasted_iota(jnp.int32, (tq, nslc), 1)
    cur = jnp.right_shift(tpos, SEL_SHIFT)
    forced = (jblk == 0) | (jblk == cur) | (jblk == cur - 1)
    visible = jblk <= cur
    score_t = jnp.where(visible & jnp.logical_not(forced), imp, -jnp.inf).T

    def run(nrows):
        sc = score_t[:nrows]
        jt = lax.broadcasted_iota(jnp.int32, (nrows, tq), 0).astype(F32)
        for _ in range(NSA_SEL_TOPK - N_FORCED):
            mx = jnp.max(sc, axis=0, keepdims=True)
            first = jnp.min(jnp.where(sc == mx, jt, float(nslc)), axis=0, keepdims=True)
            sc = jnp.where(jt == first, -jnp.inf, sc)
        taken = jnp.where(sc == -jnp.inf, 1.0, 0.0)
        if nrows < nslc:
            taken = jnp.concatenate([taken, jnp.zeros((nslc - nrows, tq), F32)], axis=0)
        chosen = visible & (forced | (taken.T > 0.5))
        u_ref[0, 0] = jnp.where(chosen, 0.0, 1.0).astype(BF16)

    half = nslc // 2
    last_block = jnp.right_shift(qi * tq + (tq - 1), SEL_SHIFT)
    pl.when(last_block < half)(functools.partial(run, half))
    pl.when(last_block >= half)(functools.partial(run, nslc))


def _cmp_select(qn, kc, vc, ov, z, *, batch, seq, fixed_shift):
    ncmp = kc.shape[2]
    nslc = ov.shape[1]
    if fixed_shift:
        chunk = 2 * LANES if ncmp % (2 * LANES) == 0 else ncmp
        kern = functools.partial(_cmp_fixed_kernel, chunk=chunk)
        scratch = [pltpu.VMEM((NSA_GROUP_HEADS, CMP_TQ, LANES), F32) for _ in range(3)]
    else:
        kern, scratch = _cmp_kernel, []
    return pl.pallas_call(
        kern,
        scratch_shapes=scratch,
        grid=(batch, NSA_KV_GROUPS, seq // CMP_TQ),
        in_specs=[pl.BlockSpec((1, NSA_GROUP_HEADS, CMP_TQ, LANES), lambda b, g, i: (b, g, i, 0)),
                  pl.BlockSpec((1, 1, ncmp, LANES), lambda b, g, i: (b, g, 0, 0)),
                  pl.BlockSpec((1, 1, ncmp, LANES), lambda b, g, i: (b, g, 0, 0)),
                  pl.BlockSpec((ncmp, nslc), lambda b, g, i: (0, 0)),
                  pl.BlockSpec((1, CMP_TQ, LANES), lambda b, g, i: (b, i, C_SMALL // LANES))],
        out_specs=[pl.BlockSpec((1, CMP_TQ, 2 * LANES), lambda b, g, i: (b, i, g)),
                   pl.BlockSpec((1, 1, CMP_TQ, nslc), lambda b, g, i: (b, g, i, 0))],
        out_shape=[jax.ShapeDtypeStruct((batch, seq, NSA_WIDTH), BF16),
                   jax.ShapeDtypeStruct((batch, NSA_KV_GROUPS, seq, nslc), BF16)],
        compiler_params=_cparams(("parallel", "parallel", "parallel")),
        name="cmp_select",
    )(qn, kc, vc, ov, z)


def _head_rms(x, g):
    return x * lax.rsqrt(jnp.mean(x * x, axis=-1, keepdims=True) + EPS) * g


def _mem_kernel(zq_ref, kv_ref, qg_ref, kg_ref, o_ref, kn_scr, vb_scr):
    @pl.when(pl.program_id(1) == 0)
    def _prep():
        for h in range(MEM_HEADS):
            kh = kv_ref[0, :, h * MEM_HEAD_DIM:(h + 1) * MEM_HEAD_DIM]
            kn_scr[h] = _head_rms(kh, kg_ref[...]).astype(BF16)
            vb_scr[h] = kv_ref[0, :, MEM_WIDTH + h * MEM_HEAD_DIM:
                               MEM_WIDTH + (h + 1) * MEM_HEAD_DIM].astype(BF16)

    for h in range(MEM_HEADS):
        sl = slice(h * MEM_HEAD_DIM, (h + 1) * MEM_HEAD_DIM)
        qh = (_head_rms(zq_ref[0, :, sl], qg_ref[...]) * ((MEM_HEAD_DIM ** -0.5) * LOG2E)).astype(BF16)
        s = lax.dot_general(qh, kn_scr[h], (((1,), (1,)), ((), ())), preferred_element_type=F32)
        m = jnp.max(s, axis=1, keepdims=True)
        e = jnp.exp2(s - m)
        p = (e / jnp.sum(e, axis=1, keepdims=True)).astype(BF16)
        o_ref[0, :, sl] = jnp.dot(p, vb_scr[h], preferred_element_type=F32).astype(o_ref.dtype)


def _mem_attention(z, mem_kv, q_g, k_g, *, batch, seq):
    mlen = mem_kv.shape[1]
    tq = TQ
    return pl.pallas_call(
        _mem_kernel,
        grid=(batch, seq // tq),
        in_specs=[pl.BlockSpec((1, tq, MEM_WIDTH), lambda b, i: (b, i, C_MEM_Q // MEM_WIDTH)),
                  pl.BlockSpec((1, mlen, 2 * MEM_WIDTH), lambda b, i: (b, 0, 0)),
                  pl.BlockSpec((1, MEM_HEAD_DIM), lambda b, i: (0, 0)),
                  pl.BlockSpec((1, MEM_HEAD_DIM), lambda b, i: (0, 0))],
        out_specs=pl.BlockSpec((1, tq, MEM_WIDTH), lambda b, i: (b, i, 0)),
        out_shape=jax.ShapeDtypeStruct((batch, seq, MEM_WIDTH), BF16),
        scratch_shapes=[pltpu.VMEM((MEM_HEADS, mlen, MEM_HEAD_DIM), BF16),
                        pltpu.VMEM((MEM_HEADS, mlen, MEM_HEAD_DIM), BF16)],
        compiler_params=_cparams(("parallel", "arbitrary")),
        name="mem_attention",
    )(z, mem_kv, q_g.reshape(1, MEM_HEAD_DIM), k_g.reshape(1, MEM_HEAD_DIM))


def _out_kernel(x_ref, oc_ref, os_ref, ow_ref, ob_ref, om_ref, sa_ref, sb_ref, sm_ref,
                g0_ref, g1_ref, g2_ref, wa_ref, wb_ref, wm_ref, wo_ref, y_ref):
    oa = ((oc_ref[...].astype(F32) + os_ref[...].astype(F32) + ow_ref[...].astype(F32))
          * jax.nn.silu(sa_ref[...]))
    ob = ob_ref[...].astype(F32) * jax.nn.silu(sb_ref[...])
    om = om_ref[...].astype(F32) * jax.nn.silu(sm_ref[...])
    u = (jax.nn.sigmoid(g0_ref[...]) * jnp.dot(oa.astype(BF16), wa_ref[...], preferred_element_type=F32)
         + jax.nn.sigmoid(g1_ref[...]) * jnp.dot(ob.astype(BF16), wb_ref[...], preferred_element_type=F32)
         + jax.nn.sigmoid(g2_ref[...]) * jnp.dot(om.astype(BF16), wm_ref[...], preferred_element_type=F32))
    y_ref[...] = x_ref[...] + jnp.dot(u.astype(BF16), wo_ref[...], preferred_element_type=F32)


def _out_proj(x2d, o_c, o_s, o_w, o_b, o_m, z2d, wa, wb, wm, wo):
    m = x2d.shape[0]
    tm = 256
    w512 = 512
    row512 = lambda c: pl.BlockSpec((tm, w512), lambda i, c=c: (i, c))
    row1024 = lambda c: pl.BlockSpec((tm, D_MODEL), lambda i, c=c: (i, c))
    full = lambda shape: pl.BlockSpec(shape, lambda i: (0, 0))
    return pl.pallas_call(
        _out_kernel,
        grid=(m // tm,),
        in_specs=[row1024(0), row512(0), row512(0), row512(0), row512(0), row512(0),
                  row512(C_NSA_SILU // w512), row512(C_FOX_SILU // w512), row512(C_MEM_SILU // w512),
                  row1024(C_MERGE // D_MODEL), row1024(C_MERGE // D_MODEL + 1),
                  row1024(C_MERGE // D_MODEL + 2),
                  full((w512, D_MODEL)), full((w512, D_MODEL)), full((w512, D_MODEL)),
                  full((D_MODEL, D_MODEL))],
        out_specs=row1024(0),
        out_shape=jax.ShapeDtypeStruct((m, D_MODEL), F32),
        compiler_params=_cparams(("parallel",)),
        name="out_proj",
    )(x2d, o_c, o_s, o_w, o_b, o_m, z2d, z2d, z2d, z2d, z2d, z2d, wa, wb, wm, wo)


def _permute_w_in(w_in):
    o = np.cumsum([0, NSA_WIDTH, 6 * NSA_KV_GROUPS * HEAD_DIM, 3 * NSA_HEADS, NSA_WIDTH,
                   3 * FOX_WIDTH, FOX_HEADS, FOX_WIDTH, MEM_WIDTH, MEM_WIDTH, 3 * D_MODEL])
    nsa_q, nsa_kv, nsa_gate, nsa_silu, fox_qkv, fox_f, fox_silu, mem_q, mem_silu, merge = [
        w_in[..., o[i]:o[i + 1]] for i in range(10)]
    pad = jnp.zeros(w_in.shape[:2] + (LANES - 3 * NSA_HEADS - FOX_HEADS,), w_in.dtype)
    return jnp.concatenate([nsa_q, nsa_silu, fox_qkv, fox_silu, mem_q, mem_silu, merge, nsa_kv,
                            nsa_gate, fox_f, pad], axis=-1).astype(BF16)


def _pad_lanes(x, width=LANES):
    return jnp.pad(x, [(0, 0)] * (x.ndim - 1) + [(0, width - x.shape[-1])])


def _tile2(g):
    return jnp.concatenate([g, g], axis=-1)[:, None, :]


def _overlap(n_cmp_pad, n_slc):
    cs = np.arange(n_cmp_pad)[:, None] * NSA_CMP_STRIDE
    ss = np.arange(NSLC_PAD)[None, :] * NSA_SEL_LEN
    ov = np.clip(np.minimum(cs + NSA_CMP_LEN, ss + NSA_SEL_LEN) - np.maximum(cs, ss), 0, None)
    ov = ov / NSA_CMP_LEN
    ov[-1] = 0.0
    ov[:, n_slc:] = 0.0
    return jnp.asarray(ov, BF16)


def _aug_permutations():
    pq = np.zeros((6 * LANES, FOX_HEADS * LANES), np.float32)
    pk = np.zeros((6 * LANES, FOX_HEADS * LANES), np.float32)
    for h in range(FOX_HEADS):
        for i in range(3):
            pq[i * LANES + F_LANE + h, h * LANES + AUG + i] = 1.0
            pk[i * LANES + F_LANE + h, h * LANES + AUG + 3 + i] = -1.0
            pk[(3 + i) * LANES + F_LANE + h, h * LANES + AUG + 6 + i] = -1.0
    return jnp.asarray(pq, BF16), jnp.asarray(pk, BF16)


def _compress_w1(w1):
    nl = w1.shape[0]
    halves = w1.reshape(nl, 2, NSA_CMP_STRIDE, 1, HEAD_DIM, NSA_CMP_HIDDEN)
    eye = jnp.eye(NSA_KV_GROUPS, dtype=w1.dtype)
    out = jnp.einsum("pg,zhldn->zlpdghn", eye, halves[:, :, :, 0])
    return out.reshape(nl, NSA_CMP_STRIDE * NSA_KV_GROUPS * HEAD_DIM,
                       NSA_KV_GROUPS * 2 * NSA_CMP_HIDDEN).astype(BF16)


def _layer(x, mem, p, consts):
    batch, seq, _ = x.shape
    m_rows = batch * seq
    cos4, sin_signed, tri, head_ones, perm_q, perm_k, ov = consts
    z2d = _norm_matmul(x.reshape(m_rows, D_MODEL), p["norm_g"], p["w_in"], 512, Z_WIDTH // 3)
    z = z2d.reshape(batch, seq, Z_WIDTH)

    (qn, kcr, vcr, ks, vs, kw, vw, fqd, fqo, fka, fv, base) = _prep(
        z, cos4, sin_signed, tri, head_ones, perm_q, perm_k,
        p["gq"], p["gks"], p["gkw"], p["gfq"], p["gfk"], p["fb"],
        p["s_sel"], p["s_win"], p["s_fox"], batch=batch, seq=seq)

    nchunk = seq // NSA_CMP_STRIDE
    kc, vc = _compress(kcr.reshape(batch, nchunk, NSA_CMP_STRIDE * LANES),
                       vcr.reshape(batch, nchunk, NSA_CMP_STRIDE * LANES),
                       p["w1k"], p["w1v"], p["bk"], p["bv"], p["w2k"], p["w2v"], p["gkc"], p["s_cmp"],
                       batch=batch)

    def attend(fixed_shift):
        o_c, unsel = _cmp_select(qn, kc, vc, ov, z, batch=batch, seq=seq, fixed_shift=fixed_shift)
        o_s = _flash(qn, ks, vs, batch=batch, seq=seq, shared_kv=True, fixed_shift=fixed_shift,
                     u=unsel, z=z, gate_branch=1)
        o_w = _flash(qn, kw, vw, batch=batch, seq=seq, shared_kv=True, fixed_shift=fixed_shift,
                     window=NSA_WINDOW, z=z, gate_branch=2)
        o_b = _flash(fqd, fka, fv, batch=batch, seq=seq, shared_kv=False, fixed_shift=fixed_shift,
                     base=base.reshape(-1), q_off=fqo)
        return o_c, o_s, o_w, o_b

    o_c, o_s, o_w, o_b = lax.cond(p["bound_ok"], lambda: attend(True), lambda: attend(False))

    mlen = mem.shape[1]
    mem_kv = _norm_matmul(mem.reshape(batch * mlen, D_MODEL), p["mem_norm_g"], p["w_mem_kv"],
                          min(512, batch * mlen), 2 * MEM_WIDTH).reshape(batch, mlen, 2 * MEM_WIDTH)
    o_m = _mem_attention(z, mem_kv, p["mem_q_norm"], p["mem_k_norm"], batch=batch, seq=seq)

    y = _out_proj(x.reshape(m_rows, D_MODEL), o_c.reshape(m_rows, -1), o_s.reshape(m_rows, -1),
                  o_w.reshape(m_rows, -1), o_b.reshape(m_rows, -1), o_m.reshape(m_rows, -1), z2d,
                  p["w_branch_a"], p["w_branch_b"], p["w_branch_m"], p["w_out"])
    return y.reshape(batch, seq, D_MODEL)


def kernel(x, mem, norm_g, mem_norm_g, w_in, nsa_q_norm, nsa_k_norm, cmp_pe_k, cmp_w1_k, cmp_w2_k,
           cmp_pe_v, cmp_w1_v, cmp_w2_v, fox_q_norm, fox_k_norm, fox_f_bias, mem_q_norm, mem_k_norm,
           w_mem_kv, w_branch_a, w_branch_b, w_branch_m, w_out):
    batch, seq, _ = x.shape
    depth = w_in.shape[0]
    n_slc = seq // NSA_SEL_LEN
    assert n_slc <= NSLC_PAD and seq % TQ == 0 and NSA_WINDOW == TQ

    half = HEAD_DIM // 2
    inv_freq = ROPE_THETA ** (-jnp.arange(half, dtype=F32) / half)
    ang = jnp.arange(seq).astype(F32)[:, None] * inv_freq[None, :]
    cos, sin = jnp.cos(ang), jnp.sin(ang)
    cos4 = jnp.concatenate([cos, cos, cos, cos], axis=-1)
    sin_signed = jnp.concatenate([-sin, sin, -sin, sin], axis=-1)
    tri = jnp.asarray(np.tril(np.ones((TQ, TQ), np.float32)), BF16)
    head_ones = jnp.asarray(np.kron(np.eye(2), np.ones((HEAD_DIM, HEAD_DIM))), BF16)
    consts = (cos4, sin_signed, tri, head_ones, *_aug_permutations(),
              _overlap(seq // NSA_CMP_STRIDE, n_slc))

    hp = lax.Precision.HIGHEST
    pe_bias = lambda pe, w1: jnp.einsum("lk,lkn->ln", pe.reshape(depth, -1), w1, precision=hp)[:, None, :]
    fb = jnp.zeros((depth, 1, LANES), F32).at[:, 0, F_LANE:F_LANE + FOX_HEADS].set(fox_f_bias)

    def logit_bound(gq, gk):
        bound = (BOUND_MARGIN * HEAD_DIM * QK_SCALE
                 * jnp.max(jnp.abs(gq), axis=-1) * jnp.max(jnp.abs(gk), axis=-1))
        return bound.astype(BF16).astype(F32)

    bounds = [logit_bound(nsa_q_norm, nsa_k_norm[:, 1]), logit_bound(nsa_q_norm, nsa_k_norm[:, 2]),
              logit_bound(fox_q_norm, fox_k_norm), logit_bound(nsa_q_norm, nsa_k_norm[:, 0])]
    bound_ok = functools.reduce(jnp.maximum, bounds) <= MAX_FIXED_BOUND
    bound_row = lambda s: jnp.broadcast_to(s[:, None, None], (depth, 1, LANES))
    stacked = dict(
        s_sel=bound_row(bounds[0]), s_win=bound_row(bounds[1]), s_fox=bound_row(bounds[2]),
        s_cmp=bound_row(bounds[3]),
        bound_ok=bound_ok,
        norm_g=norm_g, mem_norm_g=mem_norm_g, w_in=_permute_w_in(w_in),
        gq=_tile2(nsa_q_norm), gks=_tile2(nsa_k_norm[:, 1]), gkw=_tile2(nsa_k_norm[:, 2]),
        gkc=_pad_lanes(nsa_k_norm[:, 0])[:, None, :], gfq=_tile2(fox_q_norm), gfk=_tile2(fox_k_norm),
        fb=fb, w1k=_compress_w1(cmp_w1_k), w1v=_compress_w1(cmp_w1_v),
        bk=pe_bias(cmp_pe_k, cmp_w1_k), bv=pe_bias(cmp_pe_v, cmp_w1_v),
        w2k=_pad_lanes(cmp_w2_k).astype(BF16),
        w2v=jnp.concatenate([cmp_w2_v, cmp_w2_v], axis=-1).astype(BF16),
        mem_q_norm=mem_q_norm, mem_k_norm=mem_k_norm, w_mem_kv=w_mem_kv.astype(BF16),
        w_branch_a=w_branch_a.astype(BF16), w_branch_b=w_branch_b.astype(BF16),
        w_branch_m=w_branch_m.astype(BF16), w_out=w_out.astype(BF16))
    for l in range(depth):
        x = _layer(x, mem, {k: v[l] for k, v in stacked.items()}, consts)
    return x
```

```python
import functools

import numpy as np
import jax
import jax.numpy as jnp
from jax import lax
from jax.experimental import pallas as pl
from jax.experimental.pallas import tpu as pltpu

F32 = jnp.float32
BF16 = jnp.bfloat16

D_MODEL = 1024
HEAD_DIM = 64
ROPE_THETA = 10000.0
EPS = 1e-6
NSA_HEADS = 8
NSA_KV_GROUPS = 2
NSA_GROUP_HEADS = NSA_HEADS // NSA_KV_GROUPS
NSA_CMP_LEN = 32
NSA_CMP_STRIDE = 16
NSA_CMP_HIDDEN = 128
NSA_SEL_LEN = 64
NSA_SEL_TOPK = 16
NSA_WINDOW = 512
FORCE_SCORE = 1e9
NSA_WIDTH = NSA_HEADS * HEAD_DIM
FOX_HEADS = 8
FOX_WIDTH = FOX_HEADS * HEAD_DIM
MEM_HEADS = 4
MEM_HEAD_DIM = 128
MEM_WIDTH = MEM_HEADS * MEM_HEAD_DIM

LANES = 128
VMEM_LIMIT = 48 * 1024 * 1024
NEG = -1e30
SEL_MASK = 32768.0
SEL_SHIFT = 6
NSLC_PAD = LANES
LOG2E = 1.4426950408889634
QK_SCALE = (HEAD_DIM ** -0.5) * LOG2E
BOUND_MARGIN = 1.02
MAX_FIXED_BOUND = 60.0
ZERO_WEIGHT_LOG2 = -160.0

C_NSA_Q = 0
C_NSA_SILU = 512
C_FOX_Q = 1024
C_FOX_K = 1536
C_FOX_V = 2048
C_FOX_SILU = 2560
C_MEM_Q = 3072
C_MEM_SILU = 3584
C_MERGE = 4096
C_NSA_KV = 7168
C_SMALL = 7936
Z_WIDTH = 8064
F_LANE = 3 * NSA_HEADS
AUG = HEAD_DIM

TQ = 512
CMP_TQ = 256


def _cparams(sem):
    return pltpu.CompilerParams(dimension_semantics=sem, vmem_limit_bytes=VMEM_LIMIT)


def _norm_matmul_kernel(x_ref, g_ref, w_ref, o_ref):
    x = x_ref[...]
    ms = jnp.mean(x * x, axis=-1, keepdims=True)
    h = (x * lax.rsqrt(ms + EPS) * g_ref[...]).astype(BF16)
    o_ref[...] = jnp.dot(h, w_ref[...], preferred_element_type=F32).astype(o_ref.dtype)


def _norm_matmul(x2d, g, w, tm, tn, out_dtype):
    m, k = x2d.shape
    n = w.shape[1]
    return pl.pallas_call(
        _norm_matmul_kernel,
        grid=(n // tn, m // tm),
        in_specs=[pl.BlockSpec((tm, k), lambda j, i: (i, 0)),
                  pl.BlockSpec((1, k), lambda j, i: (0, 0)),
                  pl.BlockSpec((k, tn), lambda j, i: (0, j))],
        out_specs=pl.BlockSpec((tm, tn), lambda j, i: (i, j)),
        out_shape=jax.ShapeDtypeStruct((m, n), out_dtype),
        compiler_params=_cparams(("parallel", "parallel")),
        name="norm_matmul",
    )(x2d, g.reshape(1, k), w)


def _pair_rms(x, g2, head_ones):
    sq = x * x
    hi = sq.astype(BF16)
    low = (sq - hi.astype(F32)).astype(BF16)
    ssq = (jnp.dot(hi, head_ones, preferred_element_type=F32)
           + jnp.dot(low, head_ones, preferred_element_type=F32))
    return x * lax.rsqrt(ssq * (1.0 / HEAD_DIM) + EPS) * g2


def _pair_rope(y, cos4, sin_signed, first_half):
    rot = jnp.where(first_half, pltpu.roll(y, LANES - HEAD_DIM // 2, 1), pltpu.roll(y, HEAD_DIM // 2, 1))
    return y * cos4 + rot * sin_signed


def _head_a(y, lo, tail=0.0):
    return jnp.where(lo, y, tail)


def _head_b(y, lo, tail=0.0):
    return jnp.where(lo, pltpu.roll(y, HEAD_DIM, 1), tail)


def _split3_f32(c):
    hi = c.astype(BF16).astype(F32)
    r = c - hi
    mid = r.astype(BF16).astype(F32)
    return hi, mid, r - mid


def _prep_kernel(zq_ref, zc_ref, zs_ref, zw_ref, fq_ref, fk_ref, sm_ref, cos_ref, sin_ref,
                 tri_ref, ones_ref, pq_ref, pk_ref, gq_ref, gks_ref, gkw_ref, gfq_ref, gfk_ref, fb_ref,
                 ss_ref, sw_ref, sf_ref,
                 qn_ref, kcr_ref, vcr_ref, ks_ref, vs_ref, kw_ref, vw_ref, fqd_ref, fqo_ref, fka_ref,
                 base_ref, run_scr):
    ti = pl.program_id(1)
    tt = zq_ref.shape[1]
    lane = lax.broadcasted_iota(jnp.int32, (tt, LANES), 1)
    lo = lane < HEAD_DIM
    first_half = (lane & (HEAD_DIM // 2)) == 0
    cos4 = cos_ref[...]
    sin_signed = sin_ref[...]
    head_ones = ones_ref[...]
    one_at_aug = jnp.where(lane == AUG, 1.0, 0.0)

    for p in range(NSA_HEADS // 2):
        y = _pair_rope(_pair_rms(zq_ref[0, :, p * LANES:(p + 1) * LANES].astype(F32), gq_ref[...], head_ones),
                       cos4, sin_signed, first_half) * QK_SCALE
        qn_ref[0, 2 * p] = _head_a(y, lo, one_at_aug).astype(BF16)
        qn_ref[0, 2 * p + 1] = _head_b(y, lo, one_at_aug).astype(BF16)

    kcr_ref[0] = _pair_rope(zc_ref[0, :, :LANES].astype(F32), cos4, sin_signed, first_half).astype(BF16)
    vcr_ref[0] = zc_ref[0, :, LANES:].astype(BF16)

    sel_lane = jnp.right_shift(ti * tt + lax.broadcasted_iota(jnp.int32, (tt, LANES), 0), SEL_SHIFT)
    onehot = jnp.where(lane == sel_lane, -SEL_MASK, 0.0).astype(BF16)
    for z_ref, g_ref, k_ref, v_ref, bound_ref, with_onehot in (
            (zs_ref, gks_ref, ks_ref, vs_ref, ss_ref, True),
            (zw_ref, gkw_ref, kw_ref, vw_ref, sw_ref, False)):
        y = _pair_rope(_pair_rms(z_ref[0, :, :LANES].astype(F32), g_ref[...], head_ones),
                       cos4, sin_signed, first_half)
        v = z_ref[0, :, LANES:].astype(F32)
        vr = pltpu.roll(v, HEAD_DIM, 1)
        neg_bound = jnp.where(lane == AUG, -bound_ref[...], 0.0)
        k_ref[0, 0, :, :LANES] = _head_a(y, lo, neg_bound).astype(BF16)
        k_ref[0, 1, :, :LANES] = _head_b(y, lo, neg_bound).astype(BF16)
        if with_onehot:
            k_ref[0, 0, :, LANES:] = onehot
            k_ref[0, 1, :, LANES:] = onehot
        v_ref[0, 0] = jnp.where(lo, v, vr).astype(BF16)
        v_ref[0, 1] = jnp.where(lo, vr, v).astype(BF16)

    @pl.when(ti == 0)
    def _reset():
        run_scr[...] = jnp.zeros(run_scr.shape, F32)

    xf = sm_ref[0].astype(F32) + fb_ref[...]
    log_f = (jnp.minimum(xf, 0.0) - jnp.log1p(jnp.exp(-jnp.abs(xf)))) * LOG2E
    tri = tri_ref[...]
    parts = _split3_f32(log_f)
    cum = sum(jnp.dot(tri, part.astype(BF16), preferred_element_type=F32) for part in parts)
    base_ref[0, 0] = run_scr[...]
    run_scr[...] = run_scr[...] + cum[tt - 1:tt, :]

    parts6 = _split3_f32(cum) + _split3_f32(cum - cum[tt - 1:tt, :])
    x6 = jnp.concatenate([part.astype(BF16) for part in parts6], axis=1)
    aug_q = jnp.dot(x6, pq_ref[...], preferred_element_type=F32)
    aug_k = jnp.dot(x6, pk_ref[...], preferred_element_type=F32)

    lane1 = lax.broadcasted_iota(jnp.int32, (1, LANES), 1)
    span = lambda a, b: jnp.where((lane1 >= a) & (lane1 < b), 1.0, 0.0)
    q_tail_diag = span(AUG + 3, AUG + 6) + span(AUG + 9, AUG + 10)
    q_tail_off = span(AUG + 6, AUG + 10)
    k_tail = span(AUG, AUG + 3) + jnp.where(lane1 == AUG + 9, -sf_ref[...], 0.0)
    for p in range(FOX_HEADS // 2):
        sl = slice(p * LANES, (p + 1) * LANES)
        yq = _pair_rms(fq_ref[0, :, sl].astype(F32), gfq_ref[...], head_ones) * QK_SCALE
        yk = _pair_rms(fk_ref[0, :, sl].astype(F32), gfk_ref[...], head_ones)
        for e, split in ((0, _head_a), (1, _head_b)):
            h = 2 * p + e
            hs = slice(h * LANES, (h + 1) * LANES)
            qh = split(yq, lo) + aug_q[:, hs]
            fqd_ref[0, h] = (qh + q_tail_diag).astype(BF16)
            fqo_ref[0, h] = (qh + q_tail_off).astype(BF16)
            fka_ref[0, h] = (split(yk, lo) + aug_k[:, hs] + k_tail).astype(BF16)


def _prep(z, cos4, sin_signed, tri, head_ones, perm_q, perm_k, gq, gks, gkw, gfq, gfk, fb,
          s_sel, s_win, s_fox, *, batch, seq):
    nblk = seq // TQ
    whole = lambda a: pl.BlockSpec(a.shape, lambda b, i: (0, 0))
    zspec = lambda w, c: pl.BlockSpec((1, TQ, w), lambda b, i, c=c: (b, i, c))
    row = pl.BlockSpec((1, LANES), lambda b, i: (0, 0))
    tab = pl.BlockSpec((TQ, LANES), lambda b, i: (i, 0))
    heads = lambda n, w: pl.BlockSpec((1, n, TQ, w), lambda b, i: (b, 0, i, 0))
    tok = lambda w: pl.BlockSpec((1, TQ, w), lambda b, i: (b, i, 0))
    bf = lambda *shape: jax.ShapeDtypeStruct(shape, BF16)
    return pl.pallas_call(
        _prep_kernel,
        grid=(batch, nblk),
        in_specs=[zspec(NSA_WIDTH, C_NSA_Q // NSA_WIDTH),
                  zspec(2 * LANES, C_NSA_KV // (2 * LANES)),
                  zspec(2 * LANES, C_NSA_KV // (2 * LANES) + 1),
                  zspec(2 * LANES, C_NSA_KV // (2 * LANES) + 2),
                  zspec(FOX_WIDTH, C_FOX_Q // FOX_WIDTH),
                  zspec(FOX_WIDTH, C_FOX_K // FOX_WIDTH),
                  zspec(LANES, C_SMALL // LANES),
                  tab, tab, whole(tri), whole(head_ones), whole(perm_q), whole(perm_k),
                  row, row, row, row, row, row, row, row, row],
        out_specs=[heads(NSA_HEADS, LANES), tok(LANES), tok(LANES),
                   heads(NSA_KV_GROUPS, 2 * LANES), heads(NSA_KV_GROUPS, LANES),
                   heads(NSA_KV_GROUPS, LANES), heads(NSA_KV_GROUPS, LANES),
                   heads(FOX_HEADS, LANES), heads(FOX_HEADS, LANES), heads(FOX_HEADS, LANES),
                   pl.BlockSpec((1, 1, 1, LANES), lambda b, i: (b, i, 0, 0))],
        out_shape=[bf(batch, NSA_HEADS, seq, LANES), bf(batch, seq, LANES), bf(batch, seq, LANES),
                   bf(batch, NSA_KV_GROUPS, seq, 2 * LANES), bf(batch, NSA_KV_GROUPS, seq, LANES),
                   bf(batch, NSA_KV_GROUPS, seq, LANES), bf(batch, NSA_KV_GROUPS, seq, LANES),
                   bf(batch, FOX_HEADS, seq, LANES), bf(batch, FOX_HEADS, seq, LANES),
                   bf(batch, FOX_HEADS, seq, LANES),
                   jax.ShapeDtypeStruct((batch, nblk, 1, LANES), F32)],
        scratch_shapes=[pltpu.VMEM((1, LANES), F32)],
        compiler_params=_cparams(("parallel", "arbitrary")),
        name="prep",
    )(z, z, z, z, z, z, z, cos4, sin_signed, tri, head_ones, perm_q, perm_k,
      gq, gks, gkw, gfq, gfk, fb, s_sel, s_win, s_fox)


def _compress_kernel(xk_ref, xv_ref, w1k_ref, w1v_ref, bk_ref, bv_ref, w2k_ref, w2v_ref, kg_ref,
                     sc_ref, kc_ref, vc_ref):
    nchunk = xk_ref.shape[1]
    lane = lax.broadcasted_iota(jnp.int32, (nchunk, LANES), 1)
    for x_ref, w1_ref, b_ref, w2_ref, o_ref, is_key in ((xk_ref, w1k_ref, bk_ref, w2k_ref, kc_ref, True),
                                                        (xv_ref, w1v_ref, bv_ref, w2v_ref, vc_ref, False)):
        h = jnp.dot(x_ref[0], w1_ref[...], preferred_element_type=F32)
        for g in range(NSA_KV_GROUPS):
            c0 = g * 2 * NSA_CMP_HIDDEN
            top = h[:, c0:c0 + NSA_CMP_HIDDEN]
            bot = h[:, c0 + NSA_CMP_HIDDEN:c0 + 2 * NSA_CMP_HIDDEN]
            hid = top + pltpu.roll(bot, nchunk - 1, 0) + b_ref[...]
            act = (hid * jax.nn.sigmoid(hid)).astype(BF16)
            o = jnp.dot(act, w2_ref[...], preferred_element_type=F32)
            if is_key:
                ms = jnp.sum(o * o, axis=1, keepdims=True) * (1.0 / HEAD_DIM)
                o = o * lax.rsqrt(ms + EPS) * kg_ref[...]
                o = jnp.where(lane == AUG, -sc_ref[...], o)
            o_ref[0, g] = o.astype(BF16)


def _compress(xk, xv, w1k, w1v, bk, bv, w2k, w2v, kg, s_cmp, *, batch):
    nchunk, kin = xk.shape[1], xk.shape[2]
    xspec = pl.BlockSpec((1, nchunk, kin), lambda b: (b, 0, 0))
    full = lambda a: pl.BlockSpec(a.shape, lambda b: (0,) * a.ndim)
    ospec = pl.BlockSpec((1, NSA_KV_GROUPS, nchunk, LANES), lambda b: (b, 0, 0, 0))
    oshape = jax.ShapeDtypeStruct((batch, NSA_KV_GROUPS, nchunk, LANES), BF16)
    return pl.pallas_call(
        _compress_kernel,
        grid=(batch,),
        in_specs=[xspec, xspec, full(w1k), full(w1v), full(bk), full(bv), full(w2k), full(w2v), full(kg),
                  full(s_cmp)],
        out_specs=[ospec, ospec],
        out_shape=[oshape, oshape],
        compiler_params=_cparams(("parallel",)),
        name="compress",
    )(xk, xv, w1k, w1v, bk, bv, w2k, w2v, kg, s_cmp)


def _gate_column(gate_blk, lane, idx):
    col = jnp.sum(jnp.where(lane == idx, gate_blk, 0.0), axis=1, keepdims=True)
    return jax.nn.sigmoid(col)


def _flash_kernel(qi_tab, ki_tab, fl_tab, *refs, nheads, shared_kv, window, has_delta,
                  has_u, gate_branch, nblk, fixed_shift):
    refs = list(refs)
    base_ref = refs.pop(0) if has_delta else None
    q_ref = refs.pop(0)
    qoff_ref = refs.pop(0) if has_delta else None
    u_ref = refs.pop(0) if has_u else None
    k_ref = refs.pop(0)
    v_ref = refs.pop(0)
    gate_ref = refs.pop(0) if gate_branch is not None else None
    o_ref = refs.pop(0)
    m_scr = refs.pop(0)
    l_scr = refs.pop(0)
    acc_scr = refs.pop(0)
    qcat_scr = refs.pop(0) if has_u else None

    b = pl.program_id(0)
    g = pl.program_id(1)
    step = b * pl.num_programs(2) + pl.program_id(2)
    qi = qi_tab[step]
    ki = ki_tab[step]
    fl = fl_tab[step]
    active = (fl & 4) == 4
    tq = q_ref.shape[2]
    tk = k_ref.shape[2]

    @pl.when((fl & 1) == 1)
    def _init():
        m_scr[...] = jnp.full(m_scr.shape, NEG, F32)
        l_scr[...] = jnp.zeros(l_scr.shape, F32)
        acc_scr[...] = jnp.zeros(acc_scr.shape, F32)
        if has_u:
            for r in range(nheads):
                qcat_scr[r, :, :LANES] = q_ref[0, r]
                qcat_scr[r, :, LANES:] = u_ref[0, r // NSA_GROUP_HEADS]

    lane = lax.broadcasted_iota(jnp.int32, (tq, LANES), 1)
    lo = lane < HEAD_DIM

    def tile(diagonal):
        masked = diagonal or window is not None
        if masked:
            row = lax.broadcasted_iota(jnp.int32, (tq, tk), 0)
            col = lax.broadcasted_iota(jnp.int32, (tq, tk), 1)
            if diagonal:
                valid = row >= col
            else:
                valid = (tq + row - col) < window
        for p in range(nheads // 2):
            pvs, alphas = [], []
            for e in range(2):
                hh = 2 * p + e
                if has_u:
                    q = qcat_scr[hh]
                elif has_delta and not diagonal:
                    q = qoff_ref[0, hh]
                else:
                    q = q_ref[0, hh]
                k = k_ref[0, hh // NSA_GROUP_HEADS] if shared_kv else k_ref[0, hh]
                v = v_ref[0, hh // NSA_GROUP_HEADS] if shared_kv else v_ref[0, :, p * LANES:(p + 1) * LANES]
                s = lax.dot_general(q, k, (((1,), (1,)), ((), ())), preferred_element_type=F32)
                if masked:
                    s = jnp.where(valid, s, NEG)
                delta = None
                if has_delta and not diagonal:
                    hoff = F_LANE + g * nheads + hh
                    delta = (base_ref[(b * nblk + qi) * LANES + hoff]
                             - base_ref[(b * nblk + ki + 1) * LANES + hoff])
                if fixed_shift:
                    pm = jnp.exp2(s)
                    rowsum = pm[:, :LANES]
                    for j in range(1, tk // LANES):
                        rowsum = rowsum + pm[:, j * LANES:(j + 1) * LANES]
                    pv = jnp.dot(pm.astype(BF16), v, preferred_element_type=F32)
                    if delta is not None:
                        w = jnp.exp2(jnp.full((1, LANES), delta, F32))
                        rowsum = rowsum * w
                        pv = pv * w
                    l_scr[hh] = l_scr[hh] + rowsum
                    pvs.append(pv)
                else:
                    m_prev = m_scr[hh]
                    m_cur = jnp.max(s, axis=1, keepdims=True)
                    if delta is not None:
                        m_cur = m_cur + delta
                    m_new = jnp.maximum(m_prev, m_cur)
                    alpha = jnp.exp2(m_prev - m_new)
                    shift = m_new[:, :1]
                    if delta is not None:
                        shift = shift - delta
                    pm = jnp.exp2(s - shift)
                    l_scr[hh] = alpha * l_scr[hh] + jnp.sum(pm, axis=1, keepdims=True)
                    m_scr[hh] = m_new
                    pvs.append(jnp.dot(pm.astype(BF16), v, preferred_element_type=F32))
                    alphas.append(alpha)
            if fixed_shift:
                acc_scr[p] = acc_scr[p] + jnp.where(lo, pvs[0], pvs[1])
            else:
                acc_scr[p] = (acc_scr[p] * jnp.where(lo, alphas[0], alphas[1])
                              + jnp.where(lo, pvs[0], pvs[1]))

    pl.when((qi == ki) & active)(lambda: tile(True))
    pl.when((qi != ki) & active)(lambda: tile(False))

    @pl.when((fl & 2) == 2)
    def _finish():
        for p in range(nheads // 2):
            if fixed_shift:
                la = jnp.sum(l_scr[2 * p], axis=1, keepdims=True)
                lb = jnp.sum(l_scr[2 * p + 1], axis=1, keepdims=True)
            else:
                la, lb = l_scr[2 * p], l_scr[2 * p + 1]
            inv = jnp.where(lo, 1.0 / la, 1.0 / lb)
            if gate_branch is not None:
                gate_blk = gate_ref[0].astype(F32)
                h0 = g * nheads + 2 * p
                ga = _gate_column(gate_blk, lane, 3 * h0 + gate_branch)
                gb = _gate_column(gate_blk, lane, 3 * (h0 + 1) + gate_branch)
                inv = inv * jnp.where(lo, ga, gb)
            o_ref[0, :, p * LANES:(p + 1) * LANES] = (acc_scr[p] * inv).astype(o_ref.dtype)


def _tile_tables(nq, window_tiles, batch):
    qi, ki, fl = [], [], []
    for i in range(nq):
        lo = 0 if window_tiles is None else max(0, i - window_tiles)
        for j in range(lo, i + 1):
            qi.append(i)
            ki.append(j)
            fl.append((1 if j == lo else 0) | (2 if j == i else 0) | 4)
    rows = lambda a: jnp.tile(jnp.asarray(a, jnp.int32), batch)
    return rows(qi), rows(ki), rows(fl)


def _decayed_tile_tables(nq, batch, base):
    qi = np.repeat(np.arange(nq), np.arange(1, nq + 1))
    ki = np.concatenate([np.arange(i + 1) for i in range(nq)])
    nsteps = qi.shape[0]
    diag = jnp.asarray(qi == ki)
    gate = base.reshape(batch, nq, LANES)[:, :, F_LANE:F_LANE + FOX_HEADS]
    delta = gate[:, qi, :] - gate[:, np.minimum(ki + 1, nq - 1), :]
    active = diag[None, :] | (jnp.max(delta, axis=-1) > ZERO_WEIGHT_LOG2)
    same_q = jnp.asarray(qi[:, None] == np.arange(nq)[None, :])
    first_ki = jnp.min(jnp.where(active[:, :, None] & same_q[None], ki[None, :, None], nq), axis=1)
    first = active & (jnp.asarray(ki)[None, :] == first_ki[:, qi])
    flags = first.astype(jnp.int32) + 2 * diag[None, :].astype(jnp.int32) + 4 * active.astype(jnp.int32)
    order = jnp.argsort(jnp.logical_not(active), axis=1, stable=True)
    n_active = jnp.sum(active, axis=1, keepdims=True)
    slot = jnp.arange(nsteps)[None, :]
    src = jnp.take_along_axis(order, jnp.minimum(slot, n_active - 1), axis=1)
    kept = slot < n_active
    take = lambda a: jnp.asarray(a, jnp.int32)[src].reshape(-1)
    fl = jnp.where(kept, jnp.take_along_axis(flags, src, axis=1), 0).reshape(-1)
    return take(qi), take(ki), fl.astype(jnp.int32)


def _flash(q, k, v, *, batch, seq, shared_kv, fixed_shift, window=None, base=None, q_off=None,
           u=None, z=None, gate_branch=None, v_col=0):
    nheads = NSA_HEADS
    nq = seq // TQ
    ngroups = NSA_HEADS // nheads
    nkv = NSA_KV_GROUPS
    da = k.shape[-1]
    has_delta = base is not None
    has_u = u is not None
    out_w = LANES * (nheads // 2)
    if has_delta and fixed_shift:
        tabs = _decayed_tile_tables(nq, batch, base)
    else:
        tabs = _tile_tables(nq, None if window is None else window // TQ, batch)
    nsteps = int(tabs[0].shape[0]) // batch

    by_q_heads = lambda b, g, s, qt, kt, ft: (b, g, qt[b * nsteps + s], 0)
    by_k_heads = lambda b, g, s, qt, kt, ft: (b, g, kt[b * nsteps + s], 0)
    by_q_tokens = lambda col: (lambda b, g, s, qt, kt, ft: (b, qt[b * nsteps + s], g if col is None else col))
    by_k_tokens = lambda b, g, s, qt, kt, ft: (b, kt[b * nsteps + s], g + v_col)

    in_specs, args = [], []
    if has_delta:
        in_specs.append(pl.BlockSpec(memory_space=pltpu.SMEM))
        args.append(base)
    in_specs.append(pl.BlockSpec((1, nheads, TQ, LANES), by_q_heads))
    args.append(q)
    if has_delta:
        in_specs.append(pl.BlockSpec((1, nheads, TQ, LANES), by_q_heads))
        args.append(q_off)
    if has_u:
        in_specs.append(pl.BlockSpec((1, nkv, TQ, LANES), by_q_heads))
        args.append(u)
    if shared_kv:
        in_specs.append(pl.BlockSpec((1, nkv, TQ, da), by_k_heads))
        in_specs.append(pl.BlockSpec((1, nkv, TQ, LANES), by_k_heads))
    else:
        in_specs.append(pl.BlockSpec((1, nheads, TQ, da), by_k_heads))
        in_specs.append(pl.BlockSpec((1, TQ, out_w), by_k_tokens))
    args += [k, v]
    if gate_branch is not None:
        in_specs.append(pl.BlockSpec((1, TQ, LANES), by_q_tokens(C_SMALL // LANES)))
        args.append(z)

    scratch = [pltpu.VMEM((nheads, TQ, LANES), F32),
               pltpu.VMEM((nheads, TQ, LANES), F32),
               pltpu.VMEM((nheads // 2, TQ, LANES), F32)]
    if has_u:
        scratch.append(pltpu.VMEM((nheads, TQ, 2 * LANES), BF16))

    kern = functools.partial(_flash_kernel, nheads=nheads, shared_kv=shared_kv, window=window,
                             has_delta=has_delta, has_u=has_u, gate_branch=gate_branch, nblk=nq,
                             fixed_shift=fixed_shift)
    return pl.pallas_call(
        kern,
        grid_spec=pltpu.PrefetchScalarGridSpec(
            num_scalar_prefetch=3,
            grid=(batch, ngroups, nsteps),
            in_specs=in_specs,
            out_specs=pl.BlockSpec((1, TQ, out_w), by_q_tokens(None)),
            scratch_shapes=scratch),
        out_shape=jax.ShapeDtypeStruct((batch, seq, 4 * LANES), BF16),
        compiler_params=_cparams(("parallel", "parallel", "arbitrary")),
        name="flash_" + ("fox" if has_delta else ("sel" if has_u else "win")),
    )(*tabs, *args)


def _cmp_kernel(q_ref, kc_ref, vc_ref, ov_ref, gate_ref, o_ref, u_ref):
    g = pl.program_id(1)
    qi = pl.program_id(2)
    tq = q_ref.shape[2]
    ncmp = kc_ref.shape[2]
    nslc = ov_ref.shape[1]

    t = qi * tq + lax.broadcasted_iota(jnp.int32, (tq, ncmp), 0)
    n = lax.broadcasted_iota(jnp.int32, (tq, ncmp), 1)
    valid = (n * NSA_CMP_STRIDE + (NSA_CMP_LEN - 1)) <= t
    kc = kc_ref[0, 0]
    vc = vc_ref[0, 0]
    ov = ov_ref[...]
    lane = lax.broadcasted_iota(jnp.int32, (tq, LANES), 1)
    lo = lane < HEAD_DIM
    gate_blk = gate_ref[0].astype(F32)

    imp = jnp.zeros((tq, nslc), F32)
    outs = []
    for r in range(NSA_GROUP_HEADS):
        s = lax.dot_general(q_ref[0, r], kc, (((1,), (1,)), ((), ())), preferred_element_type=F32)
        s = jnp.where(valid, s, NEG)
        m = jnp.max(s, axis=1, keepdims=True)
        e = jnp.where(valid, jnp.exp2(s - m), 0.0)
        d = jnp.sum(e, axis=1, keepdims=True)
        pb = (e / jnp.where(d > 0.0, d, 1.0)).astype(BF16)
        o = jnp.dot(pb, vc, preferred_element_type=F32)
        gcol = _gate_column(gate_blk, lane, 3 * (g * NSA_GROUP_HEADS + r))
        outs.append(o * gcol)
        imp = imp + jnp.dot(pb, ov, preferred_element_type=F32)
    for p in range(NSA_GROUP_HEADS // 2):
        o_ref[0, :, p * LANES:(p + 1) * LANES] = jnp.where(lo, outs[2 * p], outs[2 * p + 1]).astype(o_ref.dtype)
    _write_unselected(u_ref, imp, qi, tq, nslc)


def _cmp_fixed_kernel(q_ref, kc_ref, vc_ref, ov_ref, gate_ref, o_ref, u_ref, d_scr, acc_scr, imp_scr,
                      *, chunk):
    g = pl.program_id(1)
    qi = pl.program_id(2)
    tq = q_ref.shape[2]
    ncmp = kc_ref.shape[2]
    nslc = ov_ref.shape[1]
    t0 = qi * tq
    d_scr[...] = jnp.zeros(d_scr.shape, F32)
    acc_scr[...] = jnp.zeros(acc_scr.shape, F32)
    imp_scr[...] = jnp.zeros(imp_scr.shape, F32)

    for c in range(ncmp // chunk):
        first_end = c * chunk * NSA_CMP_STRIDE + NSA_CMP_LEN - 1
        last_end = ((c + 1) * chunk - 1) * NSA_CMP_STRIDE + NSA_CMP_LEN - 1

        def body(masked, c=c):
            rows = slice(c * chunk, (c + 1) * chunk)
            kc = kc_ref[0, 0, rows, :]
            vo = jnp.concatenate([vc_ref[0, 0, rows, :], ov_ref[rows, :]], axis=1)
            if masked:
                t = t0 + lax.broadcasted_iota(jnp.int32, (tq, chunk), 0)
                n = c * chunk + lax.broadcasted_iota(jnp.int32, (tq, chunk), 1)
                valid = (n * NSA_CMP_STRIDE + (NSA_CMP_LEN - 1)) <= t
            for r in range(NSA_GROUP_HEADS):
                s = lax.dot_general(q_ref[0, r], kc, (((1,), (1,)), ((), ())), preferred_element_type=F32)
                if masked:
                    s = jnp.where(valid, s, NEG)
                e = jnp.exp2(s)
                part = e[:, :LANES]
                for j in range(1, chunk // LANES):
                    part = part + e[:, j * LANES:(j + 1) * LANES]
                d_scr[r] = d_scr[r] + part
                both = jnp.dot(e.astype(BF16), vo, preferred_element_type=F32)
                acc_scr[r] = acc_scr[r] + both[:, :LANES]
                imp_scr[r] = imp_scr[r] + both[:, LANES:]

        pl.when((first_end <= t0 + (tq - 1)) & (last_end > t0))(functools.partial(body, True))
        pl.when(last_end <= t0)(functools.partial(body, False))

    lane = lax.broadcasted_iota(jnp.int32, (tq, LANES), 1)
    lo = lane < HEAD_DIM
    gate_blk = gate_ref[0].astype(F32)
    imp = jnp.zeros((tq, nslc), F32)
    outs = []
    for r in range(NSA_GROUP_HEADS):
        d = jnp.sum(d_scr[r], axis=1, keepdims=True)
        inv = 1.0 / jnp.where(d > 0.0, d, 1.0)
        gcol = _gate_column(gate_blk, lane, 3 * (g * NSA_GROUP_HEADS + r))
        outs.append(acc_scr[r] * (inv * gcol))
        imp = imp + imp_scr[r] * inv
    for p in range(NSA_GROUP_HEADS // 2):
        o_ref[0, :, p * LANES:(p + 1) * LANES] = jnp.where(lo, outs[2 * p], outs[2 * p + 1]).astype(o_ref.dtype)
    _write_unselected(u_ref, imp, qi, tq, nslc)


N_FORCED = 3


def _write_unselected(u_ref, imp, qi, tq, nslc):
    tpos = qi * tq + lax.broadcasted_iota(jnp.int32, (tq, nslc), 0)
    jblk = lax.broadcasted_iota(jnp.int32, (tq, nslc), 1)
    cur = jnp.right_shift(tpos, SEL_SHIFT)
    forced = (jblk == 0) | (jblk == cur) | (jblk == cur - 1)
    visible = jblk <= cur
    score_t = jnp.where(visible & jnp.logical_not(forced), imp, -jnp.inf).T

    def run(nrows):
        sc = score_t[:nrows]
        jt = lax.broadcasted_iota(jnp.int32, (nrows, tq), 0).astype(F32)
        for _ in range(NSA_SEL_TOPK - N_FORCED):
            mx = jnp.max(sc, axis=0, keepdims=True)
            first = jnp.min(jnp.where(sc == mx, jt, float(nslc)), axis=0, keepdims=True)
            sc = jnp.where(jt == first, -jnp.inf, sc)
        taken = jnp.where(sc == -jnp.inf, 1.0, 0.0)
        if nrows < nslc:
            taken = jnp.concatenate([taken, jnp.zeros((nslc - nrows, tq), F32)], axis=0)
        chosen = visible & (forced | (taken.T > 0.5))
        u_ref[0, 0] = jnp.where(chosen, 0.0, 1.0).astype(BF16)

    half = nslc // 2
    last_block = jnp.right_shift(qi * tq + (tq - 1), SEL_SHIFT)
    pl.when(last_block < half)(functools.partial(run, half))
    pl.when(last_block >= half)(functools.partial(run, nslc))


def _cmp_select(qn, kc, vc, ov, z, *, batch, seq, fixed_shift):
    ncmp = kc.shape[2]
    nslc = ov.shape[1]
    if fixed_shift:
        chunk = 2 * LANES if ncmp % (2 * LANES) == 0 else ncmp
        kern = functools.partial(_cmp_fixed_kernel, chunk=chunk)
        scratch = [pltpu.VMEM((NSA_GROUP_HEADS, CMP_TQ, LANES), F32) for _ in range(3)]
    else:
        kern, scratch = _cmp_kernel, []
    return pl.pallas_call(
        kern,
        scratch_shapes=scratch,
        grid=(batch, NSA_KV_GROUPS, seq // CMP_TQ),
        in_specs=[pl.BlockSpec((1, NSA_GROUP_HEADS, CMP_TQ, LANES), lambda b, g, i: (b, g, i, 0)),
                  pl.BlockSpec((1, 1, ncmp, LANES), lambda b, g, i: (b, g, 0, 0)),
                  pl.BlockSpec((1, 1, ncmp, LANES), lambda b, g, i: (b, g, 0, 0)),
                  pl.BlockSpec((ncmp, nslc), lambda b, g, i: (0, 0)),
                  pl.BlockSpec((1, CMP_TQ, LANES), lambda b, g, i: (b, i, C_SMALL // LANES))],
        out_specs=[pl.BlockSpec((1, CMP_TQ, 2 * LANES), lambda b, g, i: (b, i, g)),
                   pl.BlockSpec((1, 1, CMP_TQ, nslc), lambda b, g, i: (b, g, i, 0))],
        out_shape=[jax.ShapeDtypeStruct((batch, seq, NSA_WIDTH), BF16),
                   jax.ShapeDtypeStruct((batch, NSA_KV_GROUPS, seq, nslc), BF16)],
        compiler_params=_cparams(("parallel", "parallel", "parallel")),
        name="cmp_select",
    )(qn, kc, vc, ov, z)


def _head_rms(x, g):
    return x * lax.rsqrt(jnp.mean(x * x, axis=-1, keepdims=True) + EPS) * g


def _mem_kernel(zq_ref, kv_ref, qg_ref, kg_ref, o_ref, kn_scr, vb_scr):
    @pl.when(pl.program_id(1) == 0)
    def _prep():
        for h in range(MEM_HEADS):
            kh = kv_ref[0, :, h * MEM_HEAD_DIM:(h + 1) * MEM_HEAD_DIM]
            kn_scr[h] = _head_rms(kh, kg_ref[...]).astype(BF16)
            vb_scr[h] = kv_ref[0, :, MEM_WIDTH + h * MEM_HEAD_DIM:
                               MEM_WIDTH + (h + 1) * MEM_HEAD_DIM].astype(BF16)

    for h in range(MEM_HEADS):
        sl = slice(h * MEM_HEAD_DIM, (h + 1) * MEM_HEAD_DIM)
        qh = (_head_rms(zq_ref[0, :, sl].astype(F32), qg_ref[...])
              * ((MEM_HEAD_DIM ** -0.5) * LOG2E)).astype(BF16)
        s = lax.dot_general(qh, kn_scr[h], (((1,), (1,)), ((), ())), preferred_element_type=F32)
        m = jnp.max(s, axis=1, keepdims=True)
        e = jnp.exp2(s - m)
        p = (e / jnp.sum(e, axis=1, keepdims=True)).astype(BF16)
        o_ref[0, :, sl] = jnp.dot(p, vb_scr[h], preferred_element_type=F32).astype(o_ref.dtype)


def _mem_attention(z, mem_kv, q_g, k_g, *, batch, seq):
    mlen = mem_kv.shape[1]
    tq = TQ
    return pl.pallas_call(
        _mem_kernel,
        grid=(batch, seq // tq),
        in_specs=[pl.BlockSpec((1, tq, MEM_WIDTH), lambda b, i: (b, i, C_MEM_Q // MEM_WIDTH)),
                  pl.BlockSpec((1, mlen, 2 * MEM_WIDTH), lambda b, i: (b, 0, 0)),
                  pl.BlockSpec((1, MEM_HEAD_DIM), lambda b, i: (0, 0)),
                  pl.BlockSpec((1, MEM_HEAD_DIM), lambda b, i: (0, 0))],
        out_specs=pl.BlockSpec((1, tq, MEM_WIDTH), lambda b, i: (b, i, 0)),
        out_shape=jax.ShapeDtypeStruct((batch, seq, MEM_WIDTH), BF16),
        scratch_shapes=[pltpu.VMEM((MEM_HEADS, mlen, MEM_HEAD_DIM), BF16),
                        pltpu.VMEM((MEM_HEADS, mlen, MEM_HEAD_DIM), BF16)],
        compiler_params=_cparams(("parallel", "arbitrary")),
        name="mem_attention",
    )(z, mem_kv, q_g.reshape(1, MEM_HEAD_DIM), k_g.reshape(1, MEM_HEAD_DIM))


def _out_kernel(x_ref, oc_ref, os_ref, ow_ref, ob_ref, om_ref, sa_ref, sb_ref, sm_ref,
                g0_ref, g1_ref, g2_ref, wa_ref, wb_ref, wm_ref, wo_ref, y_ref):
    oa = ((oc_ref[...].astype(F32) + os_ref[...].astype(F32) + ow_ref[...].astype(F32))
          * jax.nn.silu(sa_ref[...].astype(F32)))
    ob = ob_ref[...].astype(F32) * jax.nn.silu(sb_ref[...].astype(F32))
    om = om_ref[...].astype(F32) * jax.nn.silu(sm_ref[...].astype(F32))
    gate = lambda ref: jax.nn.sigmoid(ref[...].astype(F32))
    u = (gate(g0_ref) * jnp.dot(oa.astype(BF16), wa_ref[...], preferred_element_type=F32)
         + gate(g1_ref) * jnp.dot(ob.astype(BF16), wb_ref[...], preferred_element_type=F32)
         + gate(g2_ref) * jnp.dot(om.astype(BF16), wm_ref[...], preferred_element_type=F32))
    y_ref[...] = x_ref[...] + jnp.dot(u.astype(BF16), wo_ref[...], preferred_element_type=F32)


def _out_proj(x2d, o_c, o_s, o_w, o_b, o_m, z2d, wa, wb, wm, wo):
    m = x2d.shape[0]
    tm = 256
    w512 = 512
    row512 = lambda c: pl.BlockSpec((tm, w512), lambda i, c=c: (i, c))
    row1024 = lambda c: pl.BlockSpec((tm, D_MODEL), lambda i, c=c: (i, c))
    full = lambda shape: pl.BlockSpec(shape, lambda i: (0, 0))
    return pl.pallas_call(
        _out_kernel,
        grid=(m // tm,),
        in_specs=[row1024(0), row512(0), row512(0), row512(0), row512(0), row512(0),
                  row512(C_NSA_SILU // w512), row512(C_FOX_SILU // w512), row512(C_MEM_SILU // w512),
                  row1024(C_MERGE // D_MODEL), row1024(C_MERGE // D_MODEL + 1),
                  row1024(C_MERGE // D_MODEL + 2),
                  full((w512, D_MODEL)), full((w512, D_MODEL)), full((w512, D_MODEL)),
                  full((D_MODEL, D_MODEL))],
        out_specs=row1024(0),
        out_shape=jax.ShapeDtypeStruct((m, D_MODEL), F32),
        compiler_params=_cparams(("parallel",)),
        name="out_proj",
    )(x2d, o_c, o_s, o_w, o_b, o_m, z2d, z2d, z2d, z2d, z2d, z2d, wa, wb, wm, wo)


def _permute_w_in(w_in):
    o = np.cumsum([0, NSA_WIDTH, 6 * NSA_KV_GROUPS * HEAD_DIM, 3 * NSA_HEADS, NSA_WIDTH,
                   3 * FOX_WIDTH, FOX_HEADS, FOX_WIDTH, MEM_WIDTH, MEM_WIDTH, 3 * D_MODEL])
    nsa_q, nsa_kv, nsa_gate, nsa_silu, fox_qkv, fox_f, fox_silu, mem_q, mem_silu, merge = [
        w_in[..., o[i]:o[i + 1]] for i in range(10)]
    pad = jnp.zeros(w_in.shape[:2] + (LANES - 3 * NSA_HEADS - FOX_HEADS,), w_in.dtype)
    return jnp.concatenate([nsa_q, nsa_silu, fox_qkv, fox_silu, mem_q, mem_silu, merge, nsa_kv,
                            nsa_gate, fox_f, pad], axis=-1).astype(BF16)


def _pad_lanes(x, width=LANES):
    return jnp.pad(x, [(0, 0)] * (x.ndim - 1) + [(0, width - x.shape[-1])])


def _tile2(g):
    return jnp.concatenate([g, g], axis=-1)[:, None, :]


def _overlap(n_cmp_pad, n_slc):
    cs = np.arange(n_cmp_pad)[:, None] * NSA_CMP_STRIDE
    ss = np.arange(NSLC_PAD)[None, :] * NSA_SEL_LEN
    ov = np.clip(np.minimum(cs + NSA_CMP_LEN, ss + NSA_SEL_LEN) - np.maximum(cs, ss), 0, None)
    ov = ov / NSA_CMP_LEN
    ov[-1] = 0.0
    ov[:, n_slc:] = 0.0
    return jnp.asarray(ov, BF16)


def _aug_permutations():
    pq = np.zeros((6 * LANES, FOX_HEADS * LANES), np.float32)
    pk = np.zeros((6 * LANES, FOX_HEADS * LANES), np.float32)
    for h in range(FOX_HEADS):
        for i in range(3):
            pq[i * LANES + F_LANE + h, h * LANES + AUG + i] = 1.0
            pk[i * LANES + F_LANE + h, h * LANES + AUG + 3 + i] = -1.0
            pk[(3 + i) * LANES + F_LANE + h, h * LANES + AUG + 6 + i] = -1.0
    return jnp.asarray(pq, BF16), jnp.asarray(pk, BF16)


def _compress_w1(w1):
    nl = w1.shape[0]
    halves = w1.reshape(nl, 2, NSA_CMP_STRIDE, 1, HEAD_DIM, NSA_CMP_HIDDEN)
    eye = jnp.eye(NSA_KV_GROUPS, dtype=w1.dtype)
    out = jnp.einsum("pg,zhldn->zlpdghn", eye, halves[:, :, :, 0])
    return out.reshape(nl, NSA_CMP_STRIDE * NSA_KV_GROUPS * HEAD_DIM,
                       NSA_KV_GROUPS * 2 * NSA_CMP_HIDDEN).astype(BF16)


def _layer(x, mem, p, consts):
    batch, seq, _ = x.shape
    m_rows = batch * seq
    cos4, sin_signed, tri, head_ones, perm_q, perm_k, ov = consts
    z2d = _norm_matmul(x.reshape(m_rows, D_MODEL), p["norm_g"], p["w_in"], 512, Z_WIDTH // 3, BF16)
    z = z2d.reshape(batch, seq, Z_WIDTH)

    (qn, kcr, vcr, ks, vs, kw, vw, fqd, fqo, fka, base) = _prep(
        z, cos4, sin_signed, tri, head_ones, perm_q, perm_k,
        p["gq"], p["gks"], p["gkw"], p["gfq"], p["gfk"], p["fb"],
        p["s_sel"], p["s_win"], p["s_fox"], batch=batch, seq=seq)

    nchunk = seq // NSA_CMP_STRIDE
    kc, vc = _compress(kcr.reshape(batch, nchunk, NSA_CMP_STRIDE * LANES),
                       vcr.reshape(batch, nchunk, NSA_CMP_STRIDE * LANES),
                       p["w1k"], p["w1v"], p["bk"], p["bv"], p["w2k"], p["w2v"], p["gkc"], p["s_cmp"],
                       batch=batch)

    def attend(fixed_shift):
        o_c, unsel = _cmp_select(qn, kc, vc, ov, z, batch=batch, seq=seq, fixed_shift=fixed_shift)
        o_s = _flash(qn, ks, vs, batch=batch, seq=seq, shared_kv=True, fixed_shift=fixed_shift,
                     u=unsel, z=z, gate_branch=1)
        o_w = _flash(qn, kw, vw, batch=batch, seq=seq, shared_kv=True, fixed_shift=fixed_shift,
                     window=NSA_WINDOW, z=z, gate_branch=2)
        o_b = _flash(fqd, fka, z, batch=batch, seq=seq, shared_kv=False, fixed_shift=fixed_shift,
                     base=base.reshape(-1), q_off=fqo, v_col=C_FOX_V // FOX_WIDTH)
        return o_c, o_s, o_w, o_b

    o_c, o_s, o_w, o_b = lax.cond(p["bound_ok"], lambda: attend(True), lambda: attend(False))

    mlen = mem.shape[1]
    mem_kv = _norm_matmul(mem.reshape(batch * mlen, D_MODEL), p["mem_norm_g"], p["w_mem_kv"],
                          min(512, batch * mlen), 2 * MEM_WIDTH, F32).reshape(batch, mlen, 2 * MEM_WIDTH)
    o_m = _mem_attention(z, mem_kv, p["mem_q_norm"], p["mem_k_norm"], batch=batch, seq=seq)

    y = _out_proj(x.reshape(m_rows, D_MODEL), o_c.reshape(m_rows, -1), o_s.reshape(m_rows, -1),
                  o_w.reshape(m_rows, -1), o_b.reshape(m_rows, -1), o_m.reshape(m_rows, -1), z2d,
                  p["w_branch_a"], p["w_branch_b"], p["w_branch_m"], p["w_out"])
    return y.reshape(batch, seq, D_MODEL)


def kernel(x, mem, norm_g, mem_norm_g, w_in, nsa_q_norm, nsa_k_norm, cmp_pe_k, cmp_w1_k, cmp_w2_k,
           cmp_pe_v, cmp_w1_v, cmp_w2_v, fox_q_norm, fox_k_norm, fox_f_bias, mem_q_norm, mem_k_norm,
           w_mem_kv, w_branch_a, w_branch_b, w_branch_m, w_out):
    batch, seq, _ = x.shape
    depth = w_in.shape[0]
    n_slc = seq // NSA_SEL_LEN
    assert n_slc <= NSLC_PAD and seq % TQ == 0 and NSA_WINDOW == TQ

    half = HEAD_DIM // 2
    inv_freq = ROPE_THETA ** (-jnp.arange(half, dtype=F32) / half)
    ang = jnp.arange(seq).astype(F32)[:, None] * inv_freq[None, :]
    cos, sin = jnp.cos(ang), jnp.sin(ang)
    cos4 = jnp.concatenate([cos, cos, cos, cos], axis=-1)
    sin_signed = jnp.concatenate([-sin, sin, -sin, sin], axis=-1)
    tri = jnp.asarray(np.tril(np.ones((TQ, TQ), np.float32)), BF16)
    head_ones = jnp.asarray(np.kron(np.eye(2), np.ones((HEAD_DIM, HEAD_DIM))), BF16)
    consts = (cos4, sin_signed, tri, head_ones, *_aug_permutations(),
              _overlap(seq // NSA_CMP_STRIDE, n_slc))

    hp = lax.Precision.HIGHEST
    pe_bias = lambda pe, w1: jnp.einsum("lk,lkn->ln", pe.reshape(depth, -1), w1, precision=hp)[:, None, :]
    fb = jnp.zeros((depth, 1, LANES), F32).at[:, 0, F_LANE:F_LANE + FOX_HEADS].set(fox_f_bias)

    def logit_bound(gq, gk):
        bound = (BOUND_MARGIN * HEAD_DIM * QK_SCALE
                 * jnp.max(jnp.abs(gq), axis=-1) * jnp.max(jnp.abs(gk), axis=-1))
        return bound.astype(BF16).astype(F32)

    bounds = [logit_bound(nsa_q_norm, nsa_k_norm[:, 1]), logit_bound(nsa_q_norm, nsa_k_norm[:, 2]),
              logit_bound(fox_q_norm, fox_k_norm), logit_bound(nsa_q_norm, nsa_k_norm[:, 0])]
    bound_ok = functools.reduce(jnp.maximum, bounds) <= MAX_FIXED_BOUND
    bound_row = lambda s: jnp.broadcast_to(s[:, None, None], (depth, 1, LANES))
    stacked = dict(
        s_sel=bound_row(bounds[0]), s_win=bound_row(bounds[1]), s_fox=bound_row(bounds[2]),
        s_cmp=bound_row(bounds[3]),
        bound_ok=bound_ok,
        norm_g=norm_g, mem_norm_g=mem_norm_g, w_in=_permute_w_in(w_in),
        gq=_tile2(nsa_q_norm), gks=_tile2(nsa_k_norm[:, 1]), gkw=_tile2(nsa_k_norm[:, 2]),
        gkc=_pad_lanes(nsa_k_norm[:, 0])[:, None, :], gfq=_tile2(fox_q_norm), gfk=_tile2(fox_k_norm),
        fb=fb, w1k=_compress_w1(cmp_w1_k), w1v=_compress_w1(cmp_w1_v),
        bk=pe_bias(cmp_pe_k, cmp_w1_k), bv=pe_bias(cmp_pe_v, cmp_w1_v),
        w2k=_pad_lanes(cmp_w2_k).astype(BF16),
        w2v=jnp.concatenate([cmp_w2_v, cmp_w2_v], axis=-1).astype(BF16),
        mem_q_norm=mem_q_norm, mem_k_norm=mem_k_norm, w_mem_kv=w_mem_kv.astype(BF16),
        w_branch_a=w_branch_a.astype(BF16), w_branch_b=w_branch_b.astype(BF16),
        w_branch_m=w_branch_m.astype(BF16), w_out=w_out.astype(BF16))
    for l in range(depth):
        x = _layer(x, mem, {k: v[l] for k, v in stacked.items()}, consts)
    return x
```

```python
import functools

import numpy as np
import jax
import jax.numpy as jnp
from jax import lax
from jax.experimental import pallas as pl
from jax.experimental.pallas import tpu as pltpu

F32 = jnp.float32
BF16 = jnp.bfloat16

D_MODEL = 1024
HEAD_DIM = 64
ROPE_THETA = 10000.0
EPS = 1e-6
NSA_HEADS = 8
NSA_KV_GROUPS = 2
NSA_GROUP_HEADS = NSA_HEADS // NSA_KV_GROUPS
NSA_CMP_LEN = 32
NSA_CMP_STRIDE = 16
NSA_CMP_HIDDEN = 128
NSA_SEL_LEN = 64
NSA_SEL_TOPK = 16
NSA_WINDOW = 512
FORCE_SCORE = 1e9
NSA_WIDTH = NSA_HEADS * HEAD_DIM
FOX_HEADS = 8
FOX_WIDTH = FOX_HEADS * HEAD_DIM
MEM_HEADS = 4
MEM_HEAD_DIM = 128
MEM_WIDTH = MEM_HEADS * MEM_HEAD_DIM

LANES = 128
VMEM_LIMIT = 48 * 1024 * 1024
NEG = -1e30
SEL_MASK = 32768.0
SEL_SHIFT = 6
NSLC_PAD = LANES
LOG2E = 1.4426950408889634
QK_SCALE = (HEAD_DIM ** -0.5) * LOG2E
BOUND_MARGIN = 1.02
MAX_FIXED_BOUND = 60.0
ZERO_WEIGHT_LOG2 = -160.0

C_NSA_Q = 0
C_NSA_SILU = 512
C_FOX_Q = 1024
C_FOX_K = 1536
C_FOX_V = 2048
C_FOX_SILU = 2560
C_MEM_Q = 3072
C_MEM_SILU = 3584
C_MERGE = 4096
C_NSA_KV = 7168
C_SMALL = 7936
Z_WIDTH = 8064
F_LANE = 3 * NSA_HEADS
AUG = HEAD_DIM

TQ = 512
CMP_TQ = 512


def _cparams(sem):
    return pltpu.CompilerParams(dimension_semantics=sem, vmem_limit_bytes=VMEM_LIMIT)


def _norm_matmul_kernel(x_ref, g_ref, w_ref, o_ref):
    x = x_ref[...]
    ms = jnp.mean(x * x, axis=-1, keepdims=True)
    h = (x * lax.rsqrt(ms + EPS) * g_ref[...]).astype(BF16)
    o_ref[...] = jnp.dot(h, w_ref[...], preferred_element_type=F32).astype(o_ref.dtype)


def _norm_matmul(x2d, g, w, tm, tn, out_dtype):
    m, k = x2d.shape
    n = w.shape[1]
    return pl.pallas_call(
        _norm_matmul_kernel,
        grid=(n // tn, m // tm),
        in_specs=[pl.BlockSpec((tm, k), lambda j, i: (i, 0)),
                  pl.BlockSpec((1, k), lambda j, i: (0, 0)),
                  pl.BlockSpec((k, tn), lambda j, i: (0, j))],
        out_specs=pl.BlockSpec((tm, tn), lambda j, i: (i, j)),
        out_shape=jax.ShapeDtypeStruct((m, n), out_dtype),
        compiler_params=_cparams(("parallel", "parallel")),
        name="norm_matmul",
    )(x2d, g.reshape(1, k), w)


def _pair_rms(x, g2, head_ones):
    sq = x * x
    hi = sq.astype(BF16)
    low = (sq - hi.astype(F32)).astype(BF16)
    ssq = (jnp.dot(hi, head_ones, preferred_element_type=F32)
           + jnp.dot(low, head_ones, preferred_element_type=F32))
    return x * lax.rsqrt(ssq * (1.0 / HEAD_DIM) + EPS) * g2


def _pair_rope(y, cos4, sin_signed, first_half):
    rot = jnp.where(first_half, pltpu.roll(y, LANES - HEAD_DIM // 2, 1), pltpu.roll(y, HEAD_DIM // 2, 1))
    return y * cos4 + rot * sin_signed


def _head_a(y, lo, tail=0.0):
    return jnp.where(lo, y, tail)


def _head_b(y, lo, tail=0.0):
    return jnp.where(lo, pltpu.roll(y, HEAD_DIM, 1), tail)


def _split3_f32(c):
    hi = c.astype(BF16).astype(F32)
    r = c - hi
    mid = r.astype(BF16).astype(F32)
    return hi, mid, r - mid


def _prep_kernel(zq_ref, zc_ref, zs_ref, zw_ref, fq_ref, fk_ref, sm_ref, cos_ref, sin_ref,
                 tri_ref, ones_ref, pq_ref, pk_ref, gq_ref, gks_ref, gkw_ref, gfq_ref, gfk_ref, fb_ref,
                 ss_ref, sw_ref, sf_ref,
                 qn_ref, kcr_ref, vcr_ref, ks_ref, vs_ref, kw_ref, vw_ref, fqd_ref, fqo_ref, fka_ref,
                 base_ref, run_scr):
    ti = pl.program_id(1)
    tt = zq_ref.shape[1]
    lane = lax.broadcasted_iota(jnp.int32, (tt, LANES), 1)
    lo = lane < HEAD_DIM
    first_half = (lane & (HEAD_DIM // 2)) == 0
    cos4 = cos_ref[...]
    sin_signed = sin_ref[...]
    head_ones = ones_ref[...]
    one_at_aug = jnp.where(lane == AUG, 1.0, 0.0)

    for p in range(NSA_HEADS // 2):
        y = _pair_rope(_pair_rms(zq_ref[0, :, p * LANES:(p + 1) * LANES].astype(F32), gq_ref[...], head_ones),
                       cos4, sin_signed, first_half) * QK_SCALE
        qn_ref[0, 2 * p] = _head_a(y, lo, one_at_aug).astype(BF16)
        qn_ref[0, 2 * p + 1] = _head_b(y, lo, one_at_aug).astype(BF16)

    kcr_ref[0] = _pair_rope(zc_ref[0, :, :LANES].astype(F32), cos4, sin_signed, first_half).astype(BF16)
    vcr_ref[0] = zc_ref[0, :, LANES:].astype(BF16)

    sel_lane = jnp.right_shift(ti * tt + lax.broadcasted_iota(jnp.int32, (tt, LANES), 0), SEL_SHIFT)
    onehot = jnp.where(lane == sel_lane, -SEL_MASK, 0.0).astype(BF16)
    for z_ref, g_ref, k_ref, v_ref, bound_ref, with_onehot in (
            (zs_ref, gks_ref, ks_ref, vs_ref, ss_ref, True),
            (zw_ref, gkw_ref, kw_ref, vw_ref, sw_ref, False)):
        y = _pair_rope(_pair_rms(z_ref[0, :, :LANES].astype(F32), g_ref[...], head_ones),
                       cos4, sin_signed, first_half)
        v = z_ref[0, :, LANES:].astype(F32)
        vr = pltpu.roll(v, HEAD_DIM, 1)
        neg_bound = jnp.where(lane == AUG, -bound_ref[...], 0.0)
        k_ref[0, 0, :, :LANES] = _head_a(y, lo, neg_bound).astype(BF16)
        k_ref[0, 1, :, :LANES] = _head_b(y, lo, neg_bound).astype(BF16)
        if with_onehot:
            k_ref[0, 0, :, LANES:] = onehot
            k_ref[0, 1, :, LANES:] = onehot
        v_ref[0, 0] = jnp.where(lo, v, vr).astype(BF16)
        v_ref[0, 1] = jnp.where(lo, vr, v).astype(BF16)

    @pl.when(ti == 0)
    def _reset():
        run_scr[...] = jnp.zeros(run_scr.shape, F32)

    xf = sm_ref[0].astype(F32) + fb_ref[...]
    log_f = (jnp.minimum(xf, 0.0) - jnp.log1p(jnp.exp(-jnp.abs(xf)))) * LOG2E
    tri = tri_ref[...]
    parts = _split3_f32(log_f)
    cum = sum(jnp.dot(tri, part.astype(BF16), preferred_element_type=F32) for part in parts)
    base_ref[0, 0] = run_scr[...]
    run_scr[...] = run_scr[...] + cum[tt - 1:tt, :]

    parts6 = _split3_f32(cum) + _split3_f32(cum - cum[tt - 1:tt, :])
    x6 = jnp.concatenate([part.astype(BF16) for part in parts6], axis=1)
    aug_q = jnp.dot(x6, pq_ref[...], preferred_element_type=F32)
    aug_k = jnp.dot(x6, pk_ref[...], preferred_element_type=F32)

    lane1 = lax.broadcasted_iota(jnp.int32, (1, LANES), 1)
    span = lambda a, b: jnp.where((lane1 >= a) & (lane1 < b), 1.0, 0.0)
    q_tail_diag = span(AUG + 3, AUG + 6) + span(AUG + 9, AUG + 10)
    q_tail_off = span(AUG + 6, AUG + 10)
    k_tail = span(AUG, AUG + 3) + jnp.where(lane1 == AUG + 9, -sf_ref[...], 0.0)
    for p in range(FOX_HEADS // 2):
        sl = slice(p * LANES, (p + 1) * LANES)
        yq = _pair_rms(fq_ref[0, :, sl].astype(F32), gfq_ref[...], head_ones) * QK_SCALE
        yk = _pair_rms(fk_ref[0, :, sl].astype(F32), gfk_ref[...], head_ones)
        for e, split in ((0, _head_a), (1, _head_b)):
            h = 2 * p + e
            hs = slice(h * LANES, (h + 1) * LANES)
            qh = split(yq, lo) + aug_q[:, hs]
            fqd_ref[0, h] = (qh + q_tail_diag).astype(BF16)
            fqo_ref[0, h] = (qh + q_tail_off).astype(BF16)
            fka_ref[0, h] = (split(yk, lo) + aug_k[:, hs] + k_tail).astype(BF16)


def _prep(z, cos4, sin_signed, tri, head_ones, perm_q, perm_k, gq, gks, gkw, gfq, gfk, fb,
          s_sel, s_win, s_fox, *, batch, seq):
    nblk = seq // TQ
    whole = lambda a: pl.BlockSpec(a.shape, lambda b, i: (0, 0))
    zspec = lambda w, c: pl.BlockSpec((1, TQ, w), lambda b, i, c=c: (b, i, c))
    row = pl.BlockSpec((1, LANES), lambda b, i: (0, 0))
    tab = pl.BlockSpec((TQ, LANES), lambda b, i: (i, 0))
    heads = lambda n, w: pl.BlockSpec((1, n, TQ, w), lambda b, i: (b, 0, i, 0))
    tok = lambda w: pl.BlockSpec((1, TQ, w), lambda b, i: (b, i, 0))
    bf = lambda *shape: jax.ShapeDtypeStruct(shape, BF16)
    return pl.pallas_call(
        _prep_kernel,
        grid=(batch, nblk),
        in_specs=[zspec(NSA_WIDTH, C_NSA_Q // NSA_WIDTH),
                  zspec(2 * LANES, C_NSA_KV // (2 * LANES)),
                  zspec(2 * LANES, C_NSA_KV // (2 * LANES) + 1),
                  zspec(2 * LANES, C_NSA_KV // (2 * LANES) + 2),
                  zspec(FOX_WIDTH, C_FOX_Q // FOX_WIDTH),
                  zspec(FOX_WIDTH, C_FOX_K // FOX_WIDTH),
                  zspec(LANES, C_SMALL // LANES),
                  tab, tab, whole(tri), whole(head_ones), whole(perm_q), whole(perm_k),
                  row, row, row, row, row, row, row, row, row],
        out_specs=[heads(NSA_HEADS, LANES), tok(LANES), tok(LANES),
                   heads(NSA_KV_GROUPS, 2 * LANES), heads(NSA_KV_GROUPS, LANES),
                   heads(NSA_KV_GROUPS, LANES), heads(NSA_KV_GROUPS, LANES),
                   heads(FOX_HEADS, LANES), heads(FOX_HEADS, LANES), heads(FOX_HEADS, LANES),
                   pl.BlockSpec((1, 1, 1, LANES), lambda b, i: (b, i, 0, 0))],
        out_shape=[bf(batch, NSA_HEADS, seq, LANES), bf(batch, seq, LANES), bf(batch, seq, LANES),
                   bf(batch, NSA_KV_GROUPS, seq, 2 * LANES), bf(batch, NSA_KV_GROUPS, seq, LANES),
                   bf(batch, NSA_KV_GROUPS, seq, LANES), bf(batch, NSA_KV_GROUPS, seq, LANES),
                   bf(batch, FOX_HEADS, seq, LANES), bf(batch, FOX_HEADS, seq, LANES),
                   bf(batch, FOX_HEADS, seq, LANES),
                   jax.ShapeDtypeStruct((batch, nblk, 1, LANES), F32)],
        scratch_shapes=[pltpu.VMEM((1, LANES), F32)],
        compiler_params=_cparams(("parallel", "arbitrary")),
        name="prep",
    )(z, z, z, z, z, z, z, cos4, sin_signed, tri, head_ones, perm_q, perm_k,
      gq, gks, gkw, gfq, gfk, fb, s_sel, s_win, s_fox)


def _compress_kernel(xk_ref, xv_ref, w1k_ref, w1v_ref, bk_ref, bv_ref, w2k_ref, w2v_ref, kg_ref,
                     sc_ref, kc_ref, vc_ref):
    nchunk = xk_ref.shape[1]
    lane = lax.broadcasted_iota(jnp.int32, (nchunk, LANES), 1)
    for x_ref, w1_ref, b_ref, w2_ref, o_ref, is_key in ((xk_ref, w1k_ref, bk_ref, w2k_ref, kc_ref, True),
                                                        (xv_ref, w1v_ref, bv_ref, w2v_ref, vc_ref, False)):
        h = jnp.dot(x_ref[0], w1_ref[...], preferred_element_type=F32)
        for g in range(NSA_KV_GROUPS):
            c0 = g * 2 * NSA_CMP_HIDDEN
            top = h[:, c0:c0 + NSA_CMP_HIDDEN]
            bot = h[:, c0 + NSA_CMP_HIDDEN:c0 + 2 * NSA_CMP_HIDDEN]
            hid = top + pltpu.roll(bot, nchunk - 1, 0) + b_ref[...]
            act = (hid * jax.nn.sigmoid(hid)).astype(BF16)
            o = jnp.dot(act, w2_ref[...], preferred_element_type=F32)
            if is_key:
                ms = jnp.sum(o * o, axis=1, keepdims=True) * (1.0 / HEAD_DIM)
                o = o * lax.rsqrt(ms + EPS) * kg_ref[...]
                o = jnp.where(lane == AUG, -sc_ref[...], o)
            o_ref[0, g] = o.astype(BF16)


def _compress(xk, xv, w1k, w1v, bk, bv, w2k, w2v, kg, s_cmp, *, batch):
    nchunk, kin = xk.shape[1], xk.shape[2]
    xspec = pl.BlockSpec((1, nchunk, kin), lambda b: (b, 0, 0))
    full = lambda a: pl.BlockSpec(a.shape, lambda b: (0,) * a.ndim)
    ospec = pl.BlockSpec((1, NSA_KV_GROUPS, nchunk, LANES), lambda b: (b, 0, 0, 0))
    oshape = jax.ShapeDtypeStruct((batch, NSA_KV_GROUPS, nchunk, LANES), BF16)
    return pl.pallas_call(
        _compress_kernel,
        grid=(batch,),
        in_specs=[xspec, xspec, full(w1k), full(w1v), full(bk), full(bv), full(w2k), full(w2v), full(kg),
                  full(s_cmp)],
        out_specs=[ospec, ospec],
        out_shape=[oshape, oshape],
        compiler_params=_cparams(("parallel",)),
        name="compress",
    )(xk, xv, w1k, w1v, bk, bv, w2k, w2v, kg, s_cmp)


def _gate_column(gate_blk, lane, idx):
    col = jnp.sum(jnp.where(lane == idx, gate_blk, 0.0), axis=1, keepdims=True)
    return jax.nn.sigmoid(col)


def _flash_kernel(qi_tab, ki_tab, fl_tab, *refs, nheads, shared_kv, window, has_delta,
                  has_u, gate_branch, nblk, fixed_shift):
    refs = list(refs)
    base_ref = refs.pop(0) if has_delta else None
    q_ref = refs.pop(0)
    qoff_ref = refs.pop(0) if has_delta else None
    u_ref = refs.pop(0) if has_u else None
    k_ref = refs.pop(0)
    v_ref = refs.pop(0)
    gate_ref = refs.pop(0) if gate_branch is not None else None
    o_ref = refs.pop(0)
    m_scr = refs.pop(0)
    l_scr = refs.pop(0)
    acc_scr = refs.pop(0)
    qcat_scr = refs.pop(0) if has_u else None

    b = pl.program_id(0)
    g = pl.program_id(1)
    step = b * pl.num_programs(2) + pl.program_id(2)
    qi = qi_tab[step]
    ki = ki_tab[step]
    fl = fl_tab[step]
    active = (fl & 4) == 4
    tq = q_ref.shape[2]
    tk = k_ref.shape[2]

    @pl.when((fl & 1) == 1)
    def _init():
        m_scr[...] = jnp.full(m_scr.shape, NEG, F32)
        l_scr[...] = jnp.zeros(l_scr.shape, F32)
        acc_scr[...] = jnp.zeros(acc_scr.shape, F32)
        if has_u:
            for r in range(nheads):
                qcat_scr[r, :, :LANES] = q_ref[0, r]
                qcat_scr[r, :, LANES:] = u_ref[0, r // NSA_GROUP_HEADS]

    lane = lax.broadcasted_iota(jnp.int32, (tq, LANES), 1)
    lo = lane < HEAD_DIM

    def tile(diagonal):
        masked = diagonal or window is not None
        if masked:
            row = lax.broadcasted_iota(jnp.int32, (tq, tk), 0)
            col = lax.broadcasted_iota(jnp.int32, (tq, tk), 1)
            if diagonal:
                valid = row >= col
            else:
                valid = (tq + row - col) < window
        for p in range(nheads // 2):
            pvs, alphas = [], []
            for e in range(2):
                hh = 2 * p + e
                if has_u:
                    q = qcat_scr[hh]
                elif has_delta and not diagonal:
                    q = qoff_ref[0, hh]
                else:
                    q = q_ref[0, hh]
                k = k_ref[0, hh // NSA_GROUP_HEADS] if shared_kv else k_ref[0, hh]
                v = v_ref[0, hh // NSA_GROUP_HEADS] if shared_kv else v_ref[0, :, p * LANES:(p + 1) * LANES]
                s = lax.dot_general(q, k, (((1,), (1,)), ((), ())), preferred_element_type=F32)
                if masked:
                    s = jnp.where(valid, s, NEG)
                delta = None
                if has_delta and not diagonal:
                    hoff = F_LANE + g * nheads + hh
                    delta = (base_ref[(b * nblk + qi) * LANES + hoff]
                             - base_ref[(b * nblk + ki + 1) * LANES + hoff])
                if fixed_shift:
                    pm = jnp.exp2(s)
                    rowsum = pm[:, :LANES]
                    for j in range(1, tk // LANES):
                        rowsum = rowsum + pm[:, j * LANES:(j + 1) * LANES]
                    pv = jnp.dot(pm.astype(BF16), v, preferred_element_type=F32)
                    if delta is not None:
                        w = jnp.exp2(jnp.full((1, LANES), delta, F32))
                        rowsum = rowsum * w
                        pv = pv * w
                    l_scr[hh] = l_scr[hh] + rowsum
                    pvs.append(pv)
                else:
                    m_prev = m_scr[hh]
                    m_cur = jnp.max(s, axis=1, keepdims=True)
                    if delta is not None:
                        m_cur = m_cur + delta
                    m_new = jnp.maximum(m_prev, m_cur)
                    alpha = jnp.exp2(m_prev - m_new)
                    shift = m_new[:, :1]
                    if delta is not None:
                        shift = shift - delta
                    pm = jnp.exp2(s - shift)
                    l_scr[hh] = alpha * l_scr[hh] + jnp.sum(pm, axis=1, keepdims=True)
                    m_scr[hh] = m_new
                    pvs.append(jnp.dot(pm.astype(BF16), v, preferred_element_type=F32))
                    alphas.append(alpha)
            if fixed_shift:
                acc_scr[p] = acc_scr[p] + jnp.where(lo, pvs[0], pvs[1])
            else:
                acc_scr[p] = (acc_scr[p] * jnp.where(lo, alphas[0], alphas[1])
                              + jnp.where(lo, pvs[0], pvs[1]))

    pl.when((qi == ki) & active)(lambda: tile(True))
    pl.when((qi != ki) & active)(lambda: tile(False))

    @pl.when((fl & 2) == 2)
    def _finish():
        for p in range(nheads // 2):
            if fixed_shift:
                la = jnp.sum(l_scr[2 * p], axis=1, keepdims=True)
                lb = jnp.sum(l_scr[2 * p + 1], axis=1, keepdims=True)
            else:
                la, lb = l_scr[2 * p], l_scr[2 * p + 1]
            inv = jnp.where(lo, 1.0 / la, 1.0 / lb)
            if gate_branch is not None:
                gate_blk = gate_ref[0].astype(F32)
                h0 = g * nheads + 2 * p
                ga = _gate_column(gate_blk, lane, 3 * h0 + gate_branch)
                gb = _gate_column(gate_blk, lane, 3 * (h0 + 1) + gate_branch)
                inv = inv * jnp.where(lo, ga, gb)
            o_ref[0, :, p * LANES:(p + 1) * LANES] = (acc_scr[p] * inv).astype(o_ref.dtype)


def _tile_tables(nq, window_tiles, batch):
    qi, ki, fl = [], [], []
    for i in range(nq):
        lo = 0 if window_tiles is None else max(0, i - window_tiles)
        for j in range(lo, i + 1):
            qi.append(i)
            ki.append(j)
            fl.append((1 if j == lo else 0) | (2 if j == i else 0) | 4)
    rows = lambda a: jnp.tile(jnp.asarray(a, jnp.int32), batch)
    return rows(qi), rows(ki), rows(fl)


def _decayed_tile_tables(nq, batch, base):
    qi = np.repeat(np.arange(nq), np.arange(1, nq + 1))
    ki = np.concatenate([np.arange(i + 1) for i in range(nq)])
    nsteps = qi.shape[0]
    diag = jnp.asarray(qi == ki)
    gate = base.reshape(batch, nq, LANES)[:, :, F_LANE:F_LANE + FOX_HEADS]
    delta = gate[:, qi, :] - gate[:, np.minimum(ki + 1, nq - 1), :]
    active = diag[None, :] | (jnp.max(delta, axis=-1) > ZERO_WEIGHT_LOG2)
    same_q = jnp.asarray(qi[:, None] == np.arange(nq)[None, :])
    first_ki = jnp.min(jnp.where(active[:, :, None] & same_q[None], ki[None, :, None], nq), axis=1)
    first = active & (jnp.asarray(ki)[None, :] == first_ki[:, qi])
    flags = first.astype(jnp.int32) + 2 * diag[None, :].astype(jnp.int32) + 4 * active.astype(jnp.int32)
    order = jnp.argsort(jnp.logical_not(active), axis=1, stable=True)
    n_active = jnp.sum(active, axis=1, keepdims=True)
    slot = jnp.arange(nsteps)[None, :]
    src = jnp.take_along_axis(order, jnp.minimum(slot, n_active - 1), axis=1)
    kept = slot < n_active
    take = lambda a: jnp.asarray(a, jnp.int32)[src].reshape(-1)
    fl = jnp.where(kept, jnp.take_along_axis(flags, src, axis=1), 0).reshape(-1)
    return take(qi), take(ki), fl.astype(jnp.int32)


def _flash(q, k, v, *, batch, seq, shared_kv, fixed_shift, window=None, base=None, q_off=None,
           u=None, z=None, gate_branch=None, v_col=0):
    nheads = NSA_HEADS
    nq = seq // TQ
    ngroups = NSA_HEADS // nheads
    nkv = NSA_KV_GROUPS
    da = k.shape[-1]
    has_delta = base is not None
    has_u = u is not None
    out_w = LANES * (nheads // 2)
    if has_delta and fixed_shift:
        tabs = _decayed_tile_tables(nq, batch, base)
    else:
        tabs = _tile_tables(nq, None if window is None else window // TQ, batch)
    nsteps = int(tabs[0].shape[0]) // batch

    by_q_heads = lambda b, g, s, qt, kt, ft: (b, g, qt[b * nsteps + s], 0)
    by_k_heads = lambda b, g, s, qt, kt, ft: (b, g, kt[b * nsteps + s], 0)
    by_q_tokens = lambda col: (lambda b, g, s, qt, kt, ft: (b, qt[b * nsteps + s], g if col is None else col))
    by_k_tokens = lambda b, g, s, qt, kt, ft: (b, kt[b * nsteps + s], g + v_col)

    in_specs, args = [], []
    if has_delta:
        in_specs.append(pl.BlockSpec(memory_space=pltpu.SMEM))
        args.append(base)
    in_specs.append(pl.BlockSpec((1, nheads, TQ, LANES), by_q_heads))
    args.append(q)
    if has_delta:
        in_specs.append(pl.BlockSpec((1, nheads, TQ, LANES), by_q_heads))
        args.append(q_off)
    if has_u:
        in_specs.append(pl.BlockSpec((1, nkv, TQ, LANES), by_q_heads))
        args.append(u)
    if shared_kv:
        in_specs.append(pl.BlockSpec((1, nkv, TQ, da), by_k_heads))
        in_specs.append(pl.BlockSpec((1, nkv, TQ, LANES), by_k_heads))
    else:
        in_specs.append(pl.BlockSpec((1, nheads, TQ, da), by_k_heads))
        in_specs.append(pl.BlockSpec((1, TQ, out_w), by_k_tokens))
    args += [k, v]
    if gate_branch is not None:
        in_specs.append(pl.BlockSpec((1, TQ, LANES), by_q_tokens(C_SMALL // LANES)))
        args.append(z)

    scratch = [pltpu.VMEM((nheads, TQ, LANES), F32),
               pltpu.VMEM((nheads, TQ, LANES), F32),
               pltpu.VMEM((nheads // 2, TQ, LANES), F32)]
    if has_u:
        scratch.append(pltpu.VMEM((nheads, TQ, 2 * LANES), BF16))

    kern = functools.partial(_flash_kernel, nheads=nheads, shared_kv=shared_kv, window=window,
                             has_delta=has_delta, has_u=has_u, gate_branch=gate_branch, nblk=nq,
                             fixed_shift=fixed_shift)
    return pl.pallas_call(
        kern,
        grid_spec=pltpu.PrefetchScalarGridSpec(
            num_scalar_prefetch=3,
            grid=(batch, ngroups, nsteps),
            in_specs=in_specs,
            out_specs=pl.BlockSpec((1, TQ, out_w), by_q_tokens(None)),
            scratch_shapes=scratch),
        out_shape=jax.ShapeDtypeStruct((batch, seq, 4 * LANES), BF16),
        compiler_params=_cparams(("parallel", "parallel", "arbitrary")),
        name="flash_" + ("fox" if has_delta else ("sel" if has_u else "win")),
    )(*tabs, *args)


def _cmp_kernel(q_ref, kc_ref, vc_ref, ov_ref, gate_ref, o_ref, u_ref):
    g = pl.program_id(1)
    qi = pl.program_id(2)
    tq = q_ref.shape[2]
    ncmp = kc_ref.shape[2]
    nslc = ov_ref.shape[1]

    t = qi * tq + lax.broadcasted_iota(jnp.int32, (tq, ncmp), 0)
    n = lax.broadcasted_iota(jnp.int32, (tq, ncmp), 1)
    valid = (n * NSA_CMP_STRIDE + (NSA_CMP_LEN - 1)) <= t
    kc = kc_ref[0, 0]
    vc = vc_ref[0, 0]
    ov = ov_ref[...]
    lane = lax.broadcasted_iota(jnp.int32, (tq, LANES), 1)
    lo = lane < HEAD_DIM
    gate_blk = gate_ref[0].astype(F32)

    imp = jnp.zeros((tq, nslc), F32)
    outs = []
    for r in range(NSA_GROUP_HEADS):
        s = lax.dot_general(q_ref[0, r], kc, (((1,), (1,)), ((), ())), preferred_element_type=F32)
        s = jnp.where(valid, s, NEG)
        m = jnp.max(s, axis=1, keepdims=True)
        e = jnp.where(valid, jnp.exp2(s - m), 0.0)
        d = jnp.sum(e, axis=1, keepdims=True)
        pb = (e / jnp.where(d > 0.0, d, 1.0)).astype(BF16)
        o = jnp.dot(pb, vc, preferred_element_type=F32)
        gcol = _gate_column(gate_blk, lane, 3 * (g * NSA_GROUP_HEADS + r))
        outs.append(o * gcol)
        imp = imp + jnp.dot(pb, ov, preferred_element_type=F32)
    for p in range(NSA_GROUP_HEADS // 2):
        o_ref[0, :, p * LANES:(p + 1) * LANES] = jnp.where(lo, outs[2 * p], outs[2 * p + 1]).astype(o_ref.dtype)
    _write_unselected(u_ref, imp, qi, tq, nslc)


def _cmp_fixed_kernel(q_ref, kc_ref, vc_ref, ov_ref, gate_ref, o_ref, u_ref, d_scr, acc_scr, imp_scr,
                      *, chunk):
    g = pl.program_id(1)
    qi = pl.program_id(2)
    tq = q_ref.shape[2]
    ncmp = kc_ref.shape[2]
    nslc = ov_ref.shape[1]
    t0 = qi * tq
    d_scr[...] = jnp.zeros(d_scr.shape, F32)
    acc_scr[...] = jnp.zeros(acc_scr.shape, F32)
    imp_scr[...] = jnp.zeros(imp_scr.shape, F32)

    for c in range(ncmp // chunk):
        first_end = c * chunk * NSA_CMP_STRIDE + NSA_CMP_LEN - 1
        last_end = ((c + 1) * chunk - 1) * NSA_CMP_STRIDE + NSA_CMP_LEN - 1

        def body(masked, c=c):
            rows = slice(c * chunk, (c + 1) * chunk)
            kc = kc_ref[0, 0, rows, :]
            vo = jnp.concatenate([vc_ref[0, 0, rows, :], ov_ref[rows, :]], axis=1)
            if masked:
                t = t0 + lax.broadcasted_iota(jnp.int32, (tq, chunk), 0)
                n = c * chunk + lax.broadcasted_iota(jnp.int32, (tq, chunk), 1)
                valid = (n * NSA_CMP_STRIDE + (NSA_CMP_LEN - 1)) <= t
            for r in range(NSA_GROUP_HEADS):
                s = lax.dot_general(q_ref[0, r], kc, (((1,), (1,)), ((), ())), preferred_element_type=F32)
                if masked:
                    s = jnp.where(valid, s, NEG)
                e = jnp.exp2(s)
                part = e[:, :LANES]
                for j in range(1, chunk // LANES):
                    part = part + e[:, j * LANES:(j + 1) * LANES]
                d_scr[r] = d_scr[r] + part
                both = jnp.dot(e.astype(BF16), vo, preferred_element_type=F32)
                acc_scr[r] = acc_scr[r] + both[:, :LANES]
                imp_scr[r] = imp_scr[r] + both[:, LANES:]

        pl.when((first_end <= t0 + (tq - 1)) & (last_end > t0))(functools.partial(body, True))
        pl.when(last_end <= t0)(functools.partial(body, False))

    lane = lax.broadcasted_iota(jnp.int32, (tq, LANES), 1)
    lo = lane < HEAD_DIM
    gate_blk = gate_ref[0].astype(F32)
    imp = jnp.zeros((tq, nslc), F32)
    outs = []
    for r in range(NSA_GROUP_HEADS):
        d = jnp.sum(d_scr[r], axis=1, keepdims=True)
        inv = 1.0 / jnp.where(d > 0.0, d, 1.0)
        gcol = _gate_column(gate_blk, lane, 3 * (g * NSA_GROUP_HEADS + r))
        outs.append(acc_scr[r] * (inv * gcol))
        imp = imp + imp_scr[r] * inv
    for p in range(NSA_GROUP_HEADS // 2):
        o_ref[0, :, p * LANES:(p + 1) * LANES] = jnp.where(lo, outs[2 * p], outs[2 * p + 1]).astype(o_ref.dtype)
    _write_unselected(u_ref, imp, qi, tq, nslc)


N_FORCED = 3


def _write_unselected(u_ref, imp, qi, tq, nslc):
    tpos = qi * tq + lax.broadcasted_iota(jnp.int32, (tq, nslc), 0)
    jblk = lax.broadcasted_iota(jnp.int32, (tq, nslc), 1)
    cur = jnp.right_shift(tpos, SEL_SHIFT)
    forced = (jblk == 0) | (jblk == cur) | (jblk == cur - 1)
    visible = jblk <= cur
    score_t = jnp.where(visible & jnp.logical_not(forced), imp, -jnp.inf).T

    def run(nrows):
        sc = score_t[:nrows]
        jt = lax.broadcasted_iota(jnp.int32, (nrows, tq), 0).astype(F32)
        for _ in range(NSA_SEL_TOPK - N_FORCED):
            mx = jnp.max(sc, axis=0, keepdims=True)
            first = jnp.min(jnp.where(sc == mx, jt, float(nslc)), axis=0, keepdims=True)
            sc = jnp.where(jt == first, -jnp.inf, sc)
        taken = jnp.where(sc == -jnp.inf, 1.0, 0.0)
        if nrows < nslc:
            taken = jnp.concatenate([taken, jnp.zeros((nslc - nrows, tq), F32)], axis=0)
        chosen = visible & (forced | (taken.T > 0.5))
        u_ref[0, 0] = jnp.where(chosen, 0.0, 1.0).astype(BF16)

    half = nslc // 2
    last_block = jnp.right_shift(qi * tq + (tq - 1), SEL_SHIFT)
    pl.when(last_block < half)(functools.partial(run, half))
    pl.when(last_block >= half)(functools.partial(run, nslc))


def _cmp_select(qn, kc, vc, ov, z, *, batch, seq, fixed_shift):
    ncmp = kc.shape[2]
    nslc = ov.shape[1]
    if fixed_shift:
        chunk = 2 * LANES if ncmp % (2 * LANES) == 0 else ncmp
        kern = functools.partial(_cmp_fixed_kernel, chunk=chunk)
        scratch = [pltpu.VMEM((NSA_GROUP_HEADS, CMP_TQ, LANES), F32) for _ in range(3)]
    else:
        kern, scratch = _cmp_kernel, []
    return pl.pallas_call(
        kern,
        scratch_shapes=scratch,
        grid=(batch, NSA_KV_GROUPS, seq // CMP_TQ),
        in_specs=[pl.BlockSpec((1, NSA_GROUP_HEADS, CMP_TQ, LANES), lambda b, g, i: (b, g, i, 0)),
                  pl.BlockSpec((1, 1, ncmp, LANES), lambda b, g, i: (b, g, 0, 0)),
                  pl.BlockSpec((1, 1, ncmp, LANES), lambda b, g, i: (b, g, 0, 0)),
                  pl.BlockSpec((ncmp, nslc), lambda b, g, i: (0, 0)),
                  pl.BlockSpec((1, CMP_TQ, LANES), lambda b, g, i: (b, i, C_SMALL // LANES))],
        out_specs=[pl.BlockSpec((1, CMP_TQ, 2 * LANES), lambda b, g, i: (b, i, g)),
                   pl.BlockSpec((1, 1, CMP_TQ, nslc), lambda b, g, i: (b, g, i, 0))],
        out_shape=[jax.ShapeDtypeStruct((batch, seq, NSA_WIDTH), BF16),
                   jax.ShapeDtypeStruct((batch, NSA_KV_GROUPS, seq, nslc), BF16)],
        compiler_params=_cparams(("parallel", "parallel", "parallel")),
        name="cmp_select",
    )(qn, kc, vc, ov, z)


def _head_rms(x, g):
    return x * lax.rsqrt(jnp.mean(x * x, axis=-1, keepdims=True) + EPS) * g


def _mem_kernel(zq_ref, kv_ref, qg_ref, kg_ref, o_ref, kn_scr, vb_scr):
    @pl.when(pl.program_id(1) == 0)
    def _prep():
        for h in range(MEM_HEADS):
            kh = kv_ref[0, :, h * MEM_HEAD_DIM:(h + 1) * MEM_HEAD_DIM]
            kn_scr[h] = _head_rms(kh, kg_ref[...]).astype(BF16)
            vb_scr[h] = kv_ref[0, :, MEM_WIDTH + h * MEM_HEAD_DIM:
                               MEM_WIDTH + (h + 1) * MEM_HEAD_DIM].astype(BF16)

    for h in range(MEM_HEADS):
        sl = slice(h * MEM_HEAD_DIM, (h + 1) * MEM_HEAD_DIM)
        qh = (_head_rms(zq_ref[0, :, sl].astype(F32), qg_ref[...])
              * ((MEM_HEAD_DIM ** -0.5) * LOG2E)).astype(BF16)
        s = lax.dot_general(qh, kn_scr[h], (((1,), (1,)), ((), ())), preferred_element_type=F32)
        m = jnp.max(s, axis=1, keepdims=True)
        e = jnp.exp2(s - m)
        p = (e / jnp.sum(e, axis=1, keepdims=True)).astype(BF16)
        o_ref[0, :, sl] = jnp.dot(p, vb_scr[h], preferred_element_type=F32).astype(o_ref.dtype)


def _mem_attention(z, mem_kv, q_g, k_g, *, batch, seq):
    mlen = mem_kv.shape[1]
    tq = TQ
    return pl.pallas_call(
        _mem_kernel,
        grid=(batch, seq // tq),
        in_specs=[pl.BlockSpec((1, tq, MEM_WIDTH), lambda b, i: (b, i, C_MEM_Q // MEM_WIDTH)),
                  pl.BlockSpec((1, mlen, 2 * MEM_WIDTH), lambda b, i: (b, 0, 0)),
                  pl.BlockSpec((1, MEM_HEAD_DIM), lambda b, i: (0, 0)),
                  pl.BlockSpec((1, MEM_HEAD_DIM), lambda b, i: (0, 0))],
        out_specs=pl.BlockSpec((1, tq, MEM_WIDTH), lambda b, i: (b, i, 0)),
        out_shape=jax.ShapeDtypeStruct((batch, seq, MEM_WIDTH), BF16),
        scratch_shapes=[pltpu.VMEM((MEM_HEADS, mlen, MEM_HEAD_DIM), BF16),
                        pltpu.VMEM((MEM_HEADS, mlen, MEM_HEAD_DIM), BF16)],
        compiler_params=_cparams(("parallel", "arbitrary")),
        name="mem_attention",
    )(z, mem_kv, q_g.reshape(1, MEM_HEAD_DIM), k_g.reshape(1, MEM_HEAD_DIM))


def _out_kernel(x_ref, oc_ref, os_ref, ow_ref, ob_ref, om_ref, sa_ref, sb_ref, sm_ref,
                g0_ref, g1_ref, g2_ref, wa_ref, wb_ref, wm_ref, wo_ref, y_ref):
    oa = ((oc_ref[...].astype(F32) + os_ref[...].astype(F32) + ow_ref[...].astype(F32))
          * jax.nn.silu(sa_ref[...].astype(F32)))
    ob = ob_ref[...].astype(F32) * jax.nn.silu(sb_ref[...].astype(F32))
    om = om_ref[...].astype(F32) * jax.nn.silu(sm_ref[...].astype(F32))
    gate = lambda ref: jax.nn.sigmoid(ref[...].astype(F32))
    u = (gate(g0_ref) * jnp.dot(oa.astype(BF16), wa_ref[...], preferred_element_type=F32)
         + gate(g1_ref) * jnp.dot(ob.astype(BF16), wb_ref[...], preferred_element_type=F32)
         + gate(g2_ref) * jnp.dot(om.astype(BF16), wm_ref[...], preferred_element_type=F32))
    y_ref[...] = x_ref[...] + jnp.dot(u.astype(BF16), wo_ref[...], preferred_element_type=F32)


def _out_proj(x2d, o_c, o_s, o_w, o_b, o_m, z2d, wa, wb, wm, wo):
    m = x2d.shape[0]
    tm = 256
    w512 = 512
    row512 = lambda c: pl.BlockSpec((tm, w512), lambda i, c=c: (i, c))
    row1024 = lambda c: pl.BlockSpec((tm, D_MODEL), lambda i, c=c: (i, c))
    full = lambda shape: pl.BlockSpec(shape, lambda i: (0, 0))
    return pl.pallas_call(
        _out_kernel,
        grid=(m // tm,),
        in_specs=[row1024(0), row512(0), row512(0), row512(0), row512(0), row512(0),
                  row512(C_NSA_SILU // w512), row512(C_FOX_SILU // w512), row512(C_MEM_SILU // w512),
                  row1024(C_MERGE // D_MODEL), row1024(C_MERGE // D_MODEL + 1),
                  row1024(C_MERGE // D_MODEL + 2),
                  full((w512, D_MODEL)), full((w512, D_MODEL)), full((w512, D_MODEL)),
                  full((D_MODEL, D_MODEL))],
        out_specs=row1024(0),
        out_shape=jax.ShapeDtypeStruct((m, D_MODEL), F32),
        compiler_params=_cparams(("parallel",)),
        name="out_proj",
    )(x2d, o_c, o_s, o_w, o_b, o_m, z2d, z2d, z2d, z2d, z2d, z2d, wa, wb, wm, wo)


def _permute_w_in(w_in):
    o = np.cumsum([0, NSA_WIDTH, 6 * NSA_KV_GROUPS * HEAD_DIM, 3 * NSA_HEADS, NSA_WIDTH,
                   3 * FOX_WIDTH, FOX_HEADS, FOX_WIDTH, MEM_WIDTH, MEM_WIDTH, 3 * D_MODEL])
    nsa_q, nsa_kv, nsa_gate, nsa_silu, fox_qkv, fox_f, fox_silu, mem_q, mem_silu, merge = [
        w_in[..., o[i]:o[i + 1]] for i in range(10)]
    pad = jnp.zeros(w_in.shape[:2] + (LANES - 3 * NSA_HEADS - FOX_HEADS,), w_in.dtype)
    return jnp.concatenate([nsa_q, nsa_silu, fox_qkv, fox_silu, mem_q, mem_silu, merge, nsa_kv,
                            nsa_gate, fox_f, pad], axis=-1).astype(BF16)


def _pad_lanes(x, width=LANES):
    return jnp.pad(x, [(0, 0)] * (x.ndim - 1) + [(0, width - x.shape[-1])])


def _tile2(g):
    return jnp.concatenate([g, g], axis=-1)[:, None, :]


def _overlap(n_cmp_pad, n_slc):
    cs = np.arange(n_cmp_pad)[:, None] * NSA_CMP_STRIDE
    ss = np.arange(NSLC_PAD)[None, :] * NSA_SEL_LEN
    ov = np.clip(np.minimum(cs + NSA_CMP_LEN, ss + NSA_SEL_LEN) - np.maximum(cs, ss), 0, None)
    ov = ov / NSA_CMP_LEN
    ov[-1] = 0.0
    ov[:, n_slc:] = 0.0
    return jnp.asarray(ov, BF16)


def _aug_permutations():
    pq = np.zeros((6 * LANES, FOX_HEADS * LANES), np.float32)
    pk = np.zeros((6 * LANES, FOX_HEADS * LANES), np.float32)
    for h in range(FOX_HEADS):
        for i in range(3):
            pq[i * LANES + F_LANE + h, h * LANES + AUG + i] = 1.0
            pk[i * LANES + F_LANE + h, h * LANES + AUG + 3 + i] = -1.0
            pk[(3 + i) * LANES + F_LANE + h, h * LANES + AUG + 6 + i] = -1.0
    return jnp.asarray(pq, BF16), jnp.asarray(pk, BF16)


def _compress_w1(w1):
    nl = w1.shape[0]
    halves = w1.reshape(nl, 2, NSA_CMP_STRIDE, 1, HEAD_DIM, NSA_CMP_HIDDEN)
    eye = jnp.eye(NSA_KV_GROUPS, dtype=w1.dtype)
    out = jnp.einsum("pg,zhldn->zlpdghn", eye, halves[:, :, :, 0])
    return out.reshape(nl, NSA_CMP_STRIDE * NSA_KV_GROUPS * HEAD_DIM,
                       NSA_KV_GROUPS * 2 * NSA_CMP_HIDDEN).astype(BF16)


def _layer(x, mem, p, consts):
    batch, seq, _ = x.shape
    m_rows = batch * seq
    cos4, sin_signed, tri, head_ones, perm_q, perm_k, ov = consts
    z2d = _norm_matmul(x.reshape(m_rows, D_MODEL), p["norm_g"], p["w_in"], 512, Z_WIDTH // 3, BF16)
    z = z2d.reshape(batch, seq, Z_WIDTH)

    (qn, kcr, vcr, ks, vs, kw, vw, fqd, fqo, fka, base) = _prep(
        z, cos4, sin_signed, tri, head_ones, perm_q, perm_k,
        p["gq"], p["gks"], p["gkw"], p["gfq"], p["gfk"], p["fb"],
        p["s_sel"], p["s_win"], p["s_fox"], batch=batch, seq=seq)

    nchunk = seq // NSA_CMP_STRIDE
    kc, vc = _compress(kcr.reshape(batch, nchunk, NSA_CMP_STRIDE * LANES),
                       vcr.reshape(batch, nchunk, NSA_CMP_STRIDE * LANES),
                       p["w1k"], p["w1v"], p["bk"], p["bv"], p["w2k"], p["w2v"], p["gkc"], p["s_cmp"],
                       batch=batch)

    def attend(fixed_shift):
        o_c, unsel = _cmp_select(qn, kc, vc, ov, z, batch=batch, seq=seq, fixed_shift=fixed_shift)
        o_s = _flash(qn, ks, vs, batch=batch, seq=seq, shared_kv=True, fixed_shift=fixed_shift,
                     u=unsel, z=z, gate_branch=1)
        o_w = _flash(qn, kw, vw, batch=batch, seq=seq, shared_kv=True, fixed_shift=fixed_shift,
                     window=NSA_WINDOW, z=z, gate_branch=2)
        o_b = _flash(fqd, fka, z, batch=batch, seq=seq, shared_kv=False, fixed_shift=fixed_shift,
                     base=base.reshape(-1), q_off=fqo, v_col=C_FOX_V // FOX_WIDTH)
        return o_c, o_s, o_w, o_b

    o_c, o_s, o_w, o_b = lax.cond(p["bound_ok"], lambda: attend(True), lambda: attend(False))

    mlen = mem.shape[1]
    mem_kv = _norm_matmul(mem.reshape(batch * mlen, D_MODEL), p["mem_norm_g"], p["w_mem_kv"],
                          min(512, batch * mlen), 2 * MEM_WIDTH, F32).reshape(batch, mlen, 2 * MEM_WIDTH)
    o_m = _mem_attention(z, mem_kv, p["mem_q_norm"], p["mem_k_norm"], batch=batch, seq=seq)

    y = _out_proj(x.reshape(m_rows, D_MODEL), o_c.reshape(m_rows, -1), o_s.reshape(m_rows, -1),
                  o_w.reshape(m_rows, -1), o_b.reshape(m_rows, -1), o_m.reshape(m_rows, -1), z2d,
                  p["w_branch_a"], p["w_branch_b"], p["w_branch_m"], p["w_out"])
    return y.reshape(batch, seq, D_MODEL)


def kernel(x, mem, norm_g, mem_norm_g, w_in, nsa_q_norm, nsa_k_norm, cmp_pe_k, cmp_w1_k, cmp_w2_k,
           cmp_pe_v, cmp_w1_v, cmp_w2_v, fox_q_norm, fox_k_norm, fox_f_bias, mem_q_norm, mem_k_norm,
           w_mem_kv, w_branch_a, w_branch_b, w_branch_m, w_out):
    batch, seq, _ = x.shape
    depth = w_in.shape[0]
    n_slc = seq // NSA_SEL_LEN
    assert n_slc <= NSLC_PAD and seq % TQ == 0 and NSA_WINDOW == TQ

    half = HEAD_DIM // 2
    inv_freq = ROPE_THETA ** (-jnp.arange(half, dtype=F32) / half)
    ang = jnp.arange(seq).astype(F32)[:, None] * inv_freq[None, :]
    cos, sin = jnp.cos(ang), jnp.sin(ang)
    cos4 = jnp.concatenate([cos, cos, cos, cos], axis=-1)
    sin_signed = jnp.concatenate([-sin, sin, -sin, sin], axis=-1)
    tri = jnp.asarray(np.tril(np.ones((TQ, TQ), np.float32)), BF16)
    head_ones = jnp.asarray(np.kron(np.eye(2), np.ones((HEAD_DIM, HEAD_DIM))), BF16)
    consts = (cos4, sin_signed, tri, head_ones, *_aug_permutations(),
              _overlap(seq // NSA_CMP_STRIDE, n_slc))

    hp = lax.Precision.HIGHEST
    pe_bias = lambda pe, w1: jnp.einsum("lk,lkn->ln", pe.reshape(depth, -1), w1, precision=hp)[:, None, :]
    fb = jnp.zeros((depth, 1, LANES), F32).at[:, 0, F_LANE:F_LANE + FOX_HEADS].set(fox_f_bias)

    def logit_bound(gq, gk):
        bound = (BOUND_MARGIN * HEAD_DIM * QK_SCALE
                 * jnp.max(jnp.abs(gq), axis=-1) * jnp.max(jnp.abs(gk), axis=-1))
        return bound.astype(BF16).astype(F32)

    bounds = [logit_bound(nsa_q_norm, nsa_k_norm[:, 1]), logit_bound(nsa_q_norm, nsa_k_norm[:, 2]),
              logit_bound(fox_q_norm, fox_k_norm), logit_bound(nsa_q_norm, nsa_k_norm[:, 0])]
    bound_ok = functools.reduce(jnp.maximum, bounds) <= MAX_FIXED_BOUND
    bound_row = lambda s: jnp.broadcast_to(s[:, None, None], (depth, 1, LANES))
    stacked = dict(
        s_sel=bound_row(bounds[0]), s_win=bound_row(bounds[1]), s_fox=bound_row(bounds[2]),
        s_cmp=bound_row(bounds[3]),
        bound_ok=bound_ok,
        norm_g=norm_g, mem_norm_g=mem_norm_g, w_in=_permute_w_in(w_in),
        gq=_tile2(nsa_q_norm), gks=_tile2(nsa_k_norm[:, 1]), gkw=_tile2(nsa_k_norm[:, 2]),
        gkc=_pad_lanes(nsa_k_norm[:, 0])[:, None, :], gfq=_tile2(fox_q_norm), gfk=_tile2(fox_k_norm),
        fb=fb, w1k=_compress_w1(cmp_w1_k), w1v=_compress_w1(cmp_w1_v),
        bk=pe_bias(cmp_pe_k, cmp_w1_k), bv=pe_bias(cmp_pe_v, cmp_w1_v),
        w2k=_pad_lanes(cmp_w2_k).astype(BF16),
        w2v=jnp.concatenate([cmp_w2_v, cmp_w2_v], axis=-1).astype(BF16),
        mem_q_norm=mem_q_norm, mem_k_norm=mem_k_norm, w_mem_kv=w_mem_kv.astype(BF16),
        w_branch_a=w_branch_a.astype(BF16), w_branch_b=w_branch_b.astype(BF16),
        w_branch_m=w_branch_m.astype(BF16), w_out=w_out.astype(BF16))
    for l in range(depth):
        x = _layer(x, mem, {k: v[l] for k, v in stacked.items()}, consts)
    return x
```

```python
import functools

import numpy as np
import jax
import jax.numpy as jnp
from jax import lax
from jax.experimental import pallas as pl
from jax.experimental.pallas import tpu as pltpu

F32 = jnp.float32
BF16 = jnp.bfloat16

D_MODEL = 1024
HEAD_DIM = 64
ROPE_THETA = 10000.0
EPS = 1e-6
NSA_HEADS = 8
NSA_KV_GROUPS = 2
NSA_GROUP_HEADS = NSA_HEADS // NSA_KV_GROUPS
NSA_CMP_LEN = 32
NSA_CMP_STRIDE = 16
NSA_CMP_HIDDEN = 128
NSA_SEL_LEN = 64
NSA_SEL_TOPK = 16
NSA_WINDOW = 512
FORCE_SCORE = 1e9
NSA_WIDTH = NSA_HEADS * HEAD_DIM
FOX_HEADS = 8
FOX_WIDTH = FOX_HEADS * HEAD_DIM
MEM_HEADS = 4
MEM_HEAD_DIM = 128
MEM_WIDTH = MEM_HEADS * MEM_HEAD_DIM

LANES = 128
VMEM_LIMIT = 48 * 1024 * 1024
NEG = -1e30
SEL_MASK = 32768.0
SEL_SHIFT = 6
NSLC_PAD = LANES
LOG2E = 1.4426950408889634
QK_SCALE = (HEAD_DIM ** -0.5) * LOG2E
BOUND_MARGIN = 1.02
MAX_FIXED_BOUND = 60.0
ZERO_WEIGHT_LOG2 = -160.0

C_NSA_Q = 0
C_NSA_SILU = 512
C_FOX_Q = 1024
C_FOX_K = 1536
C_FOX_V = 2048
C_FOX_SILU = 2560
C_MEM_Q = 3072
C_MEM_SILU = 3584
C_MERGE = 4096
C_NSA_KV = 7168
C_SMALL = 7936
Z_WIDTH = 8064
F_LANE = 3 * NSA_HEADS
AUG = HEAD_DIM

TQ = 512
FLASH_ROWS = 2
CMP_TQ = 512


def _cparams(sem):
    return pltpu.CompilerParams(dimension_semantics=sem, vmem_limit_bytes=VMEM_LIMIT)


def _norm_matmul_kernel(x_ref, g_ref, w_ref, o_ref):
    x = x_ref[...]
    ms = jnp.mean(x * x, axis=-1, keepdims=True)
    h = (x * lax.rsqrt(ms + EPS) * g_ref[...]).astype(BF16)
    o_ref[...] = jnp.dot(h, w_ref[...], preferred_element_type=F32).astype(o_ref.dtype)


def _norm_matmul(x2d, g, w, tm, tn, out_dtype):
    m, k = x2d.shape
    n = w.shape[1]
    return pl.pallas_call(
        _norm_matmul_kernel,
        grid=(n // tn, m // tm),
        in_specs=[pl.BlockSpec((tm, k), lambda j, i: (i, 0)),
                  pl.BlockSpec((1, k), lambda j, i: (0, 0)),
                  pl.BlockSpec((k, tn), lambda j, i: (0, j))],
        out_specs=pl.BlockSpec((tm, tn), lambda j, i: (i, j)),
        out_shape=jax.ShapeDtypeStruct((m, n), out_dtype),
        compiler_params=_cparams(("parallel", "parallel")),
        name="norm_matmul",
    )(x2d, g.reshape(1, k), w)


def _pair_rms(x, g2, head_ones):
    sq = x * x
    hi = sq.astype(BF16)
    low = (sq - hi.astype(F32)).astype(BF16)
    ssq = (jnp.dot(hi, head_ones, preferred_element_type=F32)
           + jnp.dot(low, head_ones, preferred_element_type=F32))
    return x * lax.rsqrt(ssq * (1.0 / HEAD_DIM) + EPS) * g2


def _pair_rope(y, cos4, sin_signed, first_half):
    rot = jnp.where(first_half, pltpu.roll(y, LANES - HEAD_DIM // 2, 1), pltpu.roll(y, HEAD_DIM // 2, 1))
    return y * cos4 + rot * sin_signed


def _head_a(y, lo, tail=0.0):
    return jnp.where(lo, y, tail)


def _head_b(y, lo, tail=0.0):
    return jnp.where(lo, pltpu.roll(y, HEAD_DIM, 1), tail)


def _split3_f32(c):
    hi = c.astype(BF16).astype(F32)
    r = c - hi
    mid = r.astype(BF16).astype(F32)
    return hi, mid, r - mid


def _prep_kernel(zq_ref, zc_ref, zs_ref, zw_ref, fq_ref, fk_ref, sm_ref, cos_ref, sin_ref,
                 tri_ref, ones_ref, pq_ref, pk_ref, gq_ref, gks_ref, gkw_ref, gfq_ref, gfk_ref, fb_ref,
                 ss_ref, sw_ref, sf_ref,
                 qn_ref, kcr_ref, vcr_ref, ks_ref, vs_ref, kw_ref, vw_ref, fqd_ref, fqo_ref, fka_ref,
                 base_ref, run_scr):
    ti = pl.program_id(1)
    tt = zq_ref.shape[1]
    lane = lax.broadcasted_iota(jnp.int32, (tt, LANES), 1)
    lo = lane < HEAD_DIM
    first_half = (lane & (HEAD_DIM // 2)) == 0
    cos4 = cos_ref[...]
    sin_signed = sin_ref[...]
    head_ones = ones_ref[...]
    one_at_aug = jnp.where(lane == AUG, 1.0, 0.0)

    for p in range(NSA_HEADS // 2):
        y = _pair_rope(_pair_rms(zq_ref[0, :, p * LANES:(p + 1) * LANES].astype(F32), gq_ref[...], head_ones),
                       cos4, sin_signed, first_half) * QK_SCALE
        qn_ref[0, 2 * p] = _head_a(y, lo, one_at_aug).astype(BF16)
        qn_ref[0, 2 * p + 1] = _head_b(y, lo, one_at_aug).astype(BF16)

    kcr_ref[0] = _pair_rope(zc_ref[0, :, :LANES].astype(F32), cos4, sin_signed, first_half).astype(BF16)
    vcr_ref[0] = zc_ref[0, :, LANES:].astype(BF16)

    sel_lane = jnp.right_shift(ti * tt + lax.broadcasted_iota(jnp.int32, (tt, LANES), 0), SEL_SHIFT)
    onehot = jnp.where(lane == sel_lane, -SEL_MASK, 0.0).astype(BF16)
    for z_ref, g_ref, k_ref, v_ref, bound_ref, with_onehot in (
            (zs_ref, gks_ref, ks_ref, vs_ref, ss_ref, True),
            (zw_ref, gkw_ref, kw_ref, vw_ref, sw_ref, False)):
        y = _pair_rope(_pair_rms(z_ref[0, :, :LANES].astype(F32), g_ref[...], head_ones),
                       cos4, sin_signed, first_half)
        v = z_ref[0, :, LANES:].astype(F32)
        vr = pltpu.roll(v, HEAD_DIM, 1)
        neg_bound = jnp.where(lane == AUG, -bound_ref[...], 0.0)
        k_ref[0, 0, :, :LANES] = _head_a(y, lo, neg_bound).astype(BF16)
        k_ref[0, 1, :, :LANES] = _head_b(y, lo, neg_bound).astype(BF16)
        if with_onehot:
            k_ref[0, 0, :, LANES:] = onehot
            k_ref[0, 1, :, LANES:] = onehot
        v_ref[0, 0] = jnp.where(lo, v, vr).astype(BF16)
        v_ref[0, 1] = jnp.where(lo, vr, v).astype(BF16)

    @pl.when(ti == 0)
    def _reset():
        run_scr[...] = jnp.zeros(run_scr.shape, F32)

    xf = sm_ref[0].astype(F32) + fb_ref[...]
    log_f = (jnp.minimum(xf, 0.0) - jnp.log1p(jnp.exp(-jnp.abs(xf)))) * LOG2E
    tri = tri_ref[...]
    parts = _split3_f32(log_f)
    cum = sum(jnp.dot(tri, part.astype(BF16), preferred_element_type=F32) for part in parts)
    base_ref[0, 0] = run_scr[...]
    run_scr[...] = run_scr[...] + cum[tt - 1:tt, :]

    parts6 = _split3_f32(cum) + _split3_f32(cum - cum[tt - 1:tt, :])
    x6 = jnp.concatenate([part.astype(BF16) for part in parts6], axis=1)
    aug_q = jnp.dot(x6, pq_ref[...], preferred_element_type=F32)
    aug_k = jnp.dot(x6, pk_ref[...], preferred_element_type=F32)

    lane1 = lax.broadcasted_iota(jnp.int32, (1, LANES), 1)
    span = lambda a, b: jnp.where((lane1 >= a) & (lane1 < b), 1.0, 0.0)
    q_tail_diag = span(AUG + 3, AUG + 6) + span(AUG + 9, AUG + 10)
    q_tail_off = span(AUG + 6, AUG + 10)
    k_tail = span(AUG, AUG + 3) + jnp.where(lane1 == AUG + 9, -sf_ref[...], 0.0)
    for p in range(FOX_HEADS // 2):
        sl = slice(p * LANES, (p + 1) * LANES)
        yq = _pair_rms(fq_ref[0, :, sl].astype(F32), gfq_ref[...], head_ones) * QK_SCALE
        yk = _pair_rms(fk_ref[0, :, sl].astype(F32), gfk_ref[...], head_ones)
        for e, split in ((0, _head_a), (1, _head_b)):
            h = 2 * p + e
            hs = slice(h * LANES, (h + 1) * LANES)
            qh = split(yq, lo) + aug_q[:, hs]
            fqd_ref[0, h] = (qh + q_tail_diag).astype(BF16)
            fqo_ref[0, h] = (qh + q_tail_off).astype(BF16)
            fka_ref[0, h] = (split(yk, lo) + aug_k[:, hs] + k_tail).astype(BF16)


def _prep(z, cos4, sin_signed, tri, head_ones, perm_q, perm_k, gq, gks, gkw, gfq, gfk, fb,
          s_sel, s_win, s_fox, *, batch, seq):
    nblk = seq // TQ
    whole = lambda a: pl.BlockSpec(a.shape, lambda b, i: (0, 0))
    zspec = lambda w, c: pl.BlockSpec((1, TQ, w), lambda b, i, c=c: (b, i, c))
    row = pl.BlockSpec((1, LANES), lambda b, i: (0, 0))
    tab = pl.BlockSpec((TQ, LANES), lambda b, i: (i, 0))
    heads = lambda n, w: pl.BlockSpec((1, n, TQ, w), lambda b, i: (b, 0, i, 0))
    tok = lambda w: pl.BlockSpec((1, TQ, w), lambda b, i: (b, i, 0))
    bf = lambda *shape: jax.ShapeDtypeStruct(shape, BF16)
    return pl.pallas_call(
        _prep_kernel,
        grid=(batch, nblk),
        in_specs=[zspec(NSA_WIDTH, C_NSA_Q // NSA_WIDTH),
                  zspec(2 * LANES, C_NSA_KV // (2 * LANES)),
                  zspec(2 * LANES, C_NSA_KV // (2 * LANES) + 1),
                  zspec(2 * LANES, C_NSA_KV // (2 * LANES) + 2),
                  zspec(FOX_WIDTH, C_FOX_Q // FOX_WIDTH),
                  zspec(FOX_WIDTH, C_FOX_K // FOX_WIDTH),
                  zspec(LANES, C_SMALL // LANES),
                  tab, tab, whole(tri), whole(head_ones), whole(perm_q), whole(perm_k),
                  row, row, row, row, row, row, row, row, row],
        out_specs=[heads(NSA_HEADS, LANES), tok(LANES), tok(LANES),
                   heads(NSA_KV_GROUPS, 2 * LANES), heads(NSA_KV_GROUPS, LANES),
                   heads(NSA_KV_GROUPS, LANES), heads(NSA_KV_GROUPS, LANES),
                   heads(FOX_HEADS, LANES), heads(FOX_HEADS, LANES), heads(FOX_HEADS, LANES),
                   pl.BlockSpec((1, 1, 1, LANES), lambda b, i: (b, i, 0, 0))],
        out_shape=[bf(batch, NSA_HEADS, seq, LANES), bf(batch, seq, LANES), bf(batch, seq, LANES),
                   bf(batch, NSA_KV_GROUPS, seq, 2 * LANES), bf(batch, NSA_KV_GROUPS, seq, LANES),
                   bf(batch, NSA_KV_GROUPS, seq, LANES), bf(batch, NSA_KV_GROUPS, seq, LANES),
                   bf(batch, FOX_HEADS, seq, LANES), bf(batch, FOX_HEADS, seq, LANES),
                   bf(batch, FOX_HEADS, seq, LANES),
                   jax.ShapeDtypeStruct((batch, nblk, 1, LANES), F32)],
        scratch_shapes=[pltpu.VMEM((1, LANES), F32)],
        compiler_params=_cparams(("parallel", "arbitrary")),
        name="prep",
    )(z, z, z, z, z, z, z, cos4, sin_signed, tri, head_ones, perm_q, perm_k,
      gq, gks, gkw, gfq, gfk, fb, s_sel, s_win, s_fox)


def _compress_kernel(xk_ref, xv_ref, w1k_ref, w1v_ref, bk_ref, bv_ref, w2k_ref, w2v_ref, kg_ref,
                     sc_ref, kc_ref, vc_ref):
    nchunk = xk_ref.shape[1]
    lane = lax.broadcasted_iota(jnp.int32, (nchunk, LANES), 1)
    for x_ref, w1_ref, b_ref, w2_ref, o_ref, is_key in ((xk_ref, w1k_ref, bk_ref, w2k_ref, kc_ref, True),
                                                        (xv_ref, w1v_ref, bv_ref, w2v_ref, vc_ref, False)):
        h = jnp.dot(x_ref[0], w1_ref[...], preferred_element_type=F32)
        for g in range(NSA_KV_GROUPS):
            c0 = g * 2 * NSA_CMP_HIDDEN
            top = h[:, c0:c0 + NSA_CMP_HIDDEN]
            bot = h[:, c0 + NSA_CMP_HIDDEN:c0 + 2 * NSA_CMP_HIDDEN]
            hid = top + pltpu.roll(bot, nchunk - 1, 0) + b_ref[...]
            act = (hid * jax.nn.sigmoid(hid)).astype(BF16)
            o = jnp.dot(act, w2_ref[...], preferred_element_type=F32)
            if is_key:
                ms = jnp.sum(o * o, axis=1, keepdims=True) * (1.0 / HEAD_DIM)
                o = o * lax.rsqrt(ms + EPS) * kg_ref[...]
                o = jnp.where(lane == AUG, -sc_ref[...], o)
            o_ref[0, g] = o.astype(BF16)


def _compress(xk, xv, w1k, w1v, bk, bv, w2k, w2v, kg, s_cmp, *, batch):
    nchunk, kin = xk.shape[1], xk.shape[2]
    xspec = pl.BlockSpec((1, nchunk, kin), lambda b: (b, 0, 0))
    full = lambda a: pl.BlockSpec(a.shape, lambda b: (0,) * a.ndim)
    ospec = pl.BlockSpec((1, NSA_KV_GROUPS, nchunk, LANES), lambda b: (b, 0, 0, 0))
    oshape = jax.ShapeDtypeStruct((batch, NSA_KV_GROUPS, nchunk, LANES), BF16)
    return pl.pallas_call(
        _compress_kernel,
        grid=(batch,),
        in_specs=[xspec, xspec, full(w1k), full(w1v), full(bk), full(bv), full(w2k), full(w2v), full(kg),
                  full(s_cmp)],
        out_specs=[ospec, ospec],
        out_shape=[oshape, oshape],
        compiler_params=_cparams(("parallel",)),
        name="compress",
    )(xk, xv, w1k, w1v, bk, bv, w2k, w2v, kg, s_cmp)


def _gate_column(gate_blk, lane, idx):
    col = jnp.sum(jnp.where(lane == idx, gate_blk, 0.0), axis=1, keepdims=True)
    return jax.nn.sigmoid(col)


def _flash_kernel(qi_tab, ki_tab, fl_tab, *refs, nheads, shared_kv, window, has_delta,
                  has_u, gate_branch, nblk, fixed_shift, nb):
    refs = list(refs)
    base_ref = refs.pop(0) if has_delta else None
    q_ref = refs.pop(0)
    qoff_ref = refs.pop(0) if has_delta else None
    u_ref = refs.pop(0) if has_u else None
    k_ref = refs.pop(0)
    v_ref = refs.pop(0)
    gate_ref = refs.pop(0) if gate_branch is not None else None
    o_ref = refs.pop(0)
    m_scr = refs.pop(0)
    l_scr = refs.pop(0)
    acc_scr = refs.pop(0)
    qcat_scr = refs.pop(0) if has_u else None

    b = pl.program_id(0)
    g = pl.program_id(1)
    step = b * pl.num_programs(2) + pl.program_id(2)
    qi = qi_tab[step]
    ki = ki_tab[step]
    fl = fl_tab[step]
    active = (fl & 4) == 4
    tq = q_ref.shape[2]
    tk = k_ref.shape[2]

    @pl.when((fl & 1) == 1)
    def _init():
        m_scr[...] = jnp.full(m_scr.shape, NEG, F32)
        l_scr[...] = jnp.zeros(l_scr.shape, F32)
        acc_scr[...] = jnp.zeros(acc_scr.shape, F32)
        if has_u:
            for bb in range(nb):
                for r in range(nheads):
                    qcat_scr[bb * nheads + r, :, :LANES] = q_ref[bb, r]
                    qcat_scr[bb * nheads + r, :, LANES:] = u_ref[bb, r // NSA_GROUP_HEADS]

    lane = lax.broadcasted_iota(jnp.int32, (tq, LANES), 1)
    lo = lane < HEAD_DIM

    def tile(diagonal):
        masked = diagonal or window is not None
        if masked:
            row = lax.broadcasted_iota(jnp.int32, (tq, tk), 0)
            col = lax.broadcasted_iota(jnp.int32, (tq, tk), 1)
            if diagonal:
                valid = row >= col
            else:
                valid = (tq + row - col) < window
        for bb in range(nb):
            slot0, pair0, brow = bb * nheads, bb * (nheads // 2), b * nb + bb
            for p in range(nheads // 2):
                pvs, alphas = [], []
                for e in range(2):
                    hh = 2 * p + e
                    if has_u:
                        q = qcat_scr[slot0 + hh]
                    elif has_delta and not diagonal:
                        q = qoff_ref[bb, hh]
                    else:
                        q = q_ref[bb, hh]
                    k = k_ref[bb, hh // NSA_GROUP_HEADS] if shared_kv else k_ref[bb, hh]
                    v = (v_ref[bb, hh // NSA_GROUP_HEADS] if shared_kv
                         else v_ref[bb, :, p * LANES:(p + 1) * LANES])
                    s = lax.dot_general(q, k, (((1,), (1,)), ((), ())), preferred_element_type=F32)
                    if masked:
                        s = jnp.where(valid, s, NEG)
                    delta = None
                    if has_delta and not diagonal:
                        hoff = F_LANE + g * nheads + hh
                        delta = (base_ref[(brow * nblk + qi) * LANES + hoff]
                                 - base_ref[(brow * nblk + ki + 1) * LANES + hoff])
                    if fixed_shift:
                        pm = jnp.exp2(s)
                        rowsum = pm[:, :LANES]
                        for j in range(1, tk // LANES):
                            rowsum = rowsum + pm[:, j * LANES:(j + 1) * LANES]
                        pv = jnp.dot(pm.astype(BF16), v, preferred_element_type=F32)
                        if delta is not None:
                            w = jnp.exp2(jnp.full((1, LANES), delta, F32))
                            rowsum = rowsum * w
                            pv = pv * w
                        l_scr[slot0 + hh] = l_scr[slot0 + hh] + rowsum
                        pvs.append(pv)
                    else:
                        m_prev = m_scr[slot0 + hh]
                        m_cur = jnp.max(s, axis=1, keepdims=True)
                        if delta is not None:
                            m_cur = m_cur + delta
                        m_new = jnp.maximum(m_prev, m_cur)
                        alpha = jnp.exp2(m_prev - m_new)
                        shift = m_new[:, :1]
                        if delta is not None:
                            shift = shift - delta
                        pm = jnp.exp2(s - shift)
                        l_scr[slot0 + hh] = alpha * l_scr[slot0 + hh] + jnp.sum(pm, axis=1, keepdims=True)
                        m_scr[slot0 + hh] = m_new
                        pvs.append(jnp.dot(pm.astype(BF16), v, preferred_element_type=F32))
                        alphas.append(alpha)
                if fixed_shift:
                    acc_scr[pair0 + p] = acc_scr[pair0 + p] + jnp.where(lo, pvs[0], pvs[1])
                else:
                    acc_scr[pair0 + p] = (acc_scr[pair0 + p] * jnp.where(lo, alphas[0], alphas[1])
                                          + jnp.where(lo, pvs[0], pvs[1]))

    pl.when((qi == ki) & active)(lambda: tile(True))
    pl.when((qi != ki) & active)(lambda: tile(False))

    @pl.when((fl & 2) == 2)
    def _finish():
        for bb in range(nb):
            slot0, pair0 = bb * nheads, bb * (nheads // 2)
            for p in range(nheads // 2):
                if fixed_shift:
                    la = jnp.sum(l_scr[slot0 + 2 * p], axis=1, keepdims=True)
                    lb = jnp.sum(l_scr[slot0 + 2 * p + 1], axis=1, keepdims=True)
                else:
                    la, lb = l_scr[slot0 + 2 * p], l_scr[slot0 + 2 * p + 1]
                inv = jnp.where(lo, 1.0 / la, 1.0 / lb)
                if gate_branch is not None:
                    gate_blk = gate_ref[bb].astype(F32)
                    h0 = g * nheads + 2 * p
                    ga = _gate_column(gate_blk, lane, 3 * h0 + gate_branch)
                    gb = _gate_column(gate_blk, lane, 3 * (h0 + 1) + gate_branch)
                    inv = inv * jnp.where(lo, ga, gb)
                o_ref[bb, :, p * LANES:(p + 1) * LANES] = (acc_scr[pair0 + p] * inv).astype(o_ref.dtype)


def _tile_tables(nq, window_tiles, batch):
    qi, ki, fl = [], [], []
    for i in range(nq):
        lo = 0 if window_tiles is None else max(0, i - window_tiles)
        for j in range(lo, i + 1):
            qi.append(i)
            ki.append(j)
            fl.append((1 if j == lo else 0) | (2 if j == i else 0) | 4)
    rows = lambda a: jnp.tile(jnp.asarray(a, jnp.int32), batch)
    return rows(qi), rows(ki), rows(fl)


def _decayed_tile_tables(nq, batch, nb, base):
    qi = np.repeat(np.arange(nq), np.arange(1, nq + 1))
    ki = np.concatenate([np.arange(i + 1) for i in range(nq)])
    nsteps = qi.shape[0]
    diag = jnp.asarray(qi == ki)
    gate = base.reshape(batch, nq, LANES)[:, :, F_LANE:F_LANE + FOX_HEADS]
    delta = gate[:, qi, :] - gate[:, np.minimum(ki + 1, nq - 1), :]
    top = jnp.max(delta.reshape(batch // nb, nb, nsteps, FOX_HEADS), axis=(1, 3))
    active = diag[None, :] | (top > ZERO_WEIGHT_LOG2)
    same_q = jnp.asarray(qi[:, None] == np.arange(nq)[None, :])
    first_ki = jnp.min(jnp.where(active[:, :, None] & same_q[None], ki[None, :, None], nq), axis=1)
    first = active & (jnp.asarray(ki)[None, :] == first_ki[:, qi])
    flags = first.astype(jnp.int32) + 2 * diag[None, :].astype(jnp.int32) + 4 * active.astype(jnp.int32)
    order = jnp.argsort(jnp.logical_not(active), axis=1, stable=True)
    n_active = jnp.sum(active, axis=1, keepdims=True)
    slot = jnp.arange(nsteps)[None, :]
    src = jnp.take_along_axis(order, jnp.minimum(slot, n_active - 1), axis=1)
    kept = slot < n_active
    take = lambda a: jnp.asarray(a, jnp.int32)[src].reshape(-1)
    fl = jnp.where(kept, jnp.take_along_axis(flags, src, axis=1), 0).reshape(-1)
    return take(qi), take(ki), fl.astype(jnp.int32)


def _flash(q, k, v, *, batch, seq, shared_kv, fixed_shift, window=None, base=None, q_off=None,
           u=None, z=None, gate_branch=None, v_col=0):
    nheads = NSA_HEADS
    nb = FLASH_ROWS if batch % FLASH_ROWS == 0 else 1
    nq = seq // TQ
    ngroups = NSA_HEADS // nheads
    nkv = NSA_KV_GROUPS
    da = k.shape[-1]
    has_delta = base is not None
    has_u = u is not None
    out_w = LANES * (nheads // 2)
    if has_delta and fixed_shift:
        tabs = _decayed_tile_tables(nq, batch, nb, base)
    else:
        tabs = _tile_tables(nq, None if window is None else window // TQ, batch // nb)
    nsteps = int(tabs[0].shape[0]) // (batch // nb)

    by_q_heads = lambda b, g, s, qt, kt, ft: (b, g, qt[b * nsteps + s], 0)
    by_k_heads = lambda b, g, s, qt, kt, ft: (b, g, kt[b * nsteps + s], 0)
    by_q_tokens = lambda col: (lambda b, g, s, qt, kt, ft: (b, qt[b * nsteps + s], g if col is None else col))
    by_k_tokens = lambda b, g, s, qt, kt, ft: (b, kt[b * nsteps + s], g + v_col)

    in_specs, args = [], []
    if has_delta:
        in_specs.append(pl.BlockSpec(memory_space=pltpu.SMEM))
        args.append(base)
    in_specs.append(pl.BlockSpec((nb, nheads, TQ, LANES), by_q_heads))
    args.append(q)
    if has_delta:
        in_specs.append(pl.BlockSpec((nb, nheads, TQ, LANES), by_q_heads))
        args.append(q_off)
    if has_u:
        in_specs.append(pl.BlockSpec((nb, nkv, TQ, LANES), by_q_heads))
        args.append(u)
    if shared_kv:
        in_specs.append(pl.BlockSpec((nb, nkv, TQ, da), by_k_heads))
        in_specs.append(pl.BlockSpec((nb, nkv, TQ, LANES), by_k_heads))
    else:
        in_specs.append(pl.BlockSpec((nb, nheads, TQ, da), by_k_heads))
        in_specs.append(pl.BlockSpec((nb, TQ, out_w), by_k_tokens))
    args += [k, v]
    if gate_branch is not None:
        in_specs.append(pl.BlockSpec((nb, TQ, LANES), by_q_tokens(C_SMALL // LANES)))
        args.append(z)

    scratch = [pltpu.VMEM((nb * nheads, TQ, LANES), F32),
               pltpu.VMEM((nb * nheads, TQ, LANES), F32),
               pltpu.VMEM((nb * nheads // 2, TQ, LANES), F32)]
    if has_u:
        scratch.append(pltpu.VMEM((nb * nheads, TQ, 2 * LANES), BF16))

    kern = functools.partial(_flash_kernel, nheads=nheads, shared_kv=shared_kv, window=window,
                             has_delta=has_delta, has_u=has_u, gate_branch=gate_branch, nblk=nq,
                             fixed_shift=fixed_shift, nb=nb)
    return pl.pallas_call(
        kern,
        grid_spec=pltpu.PrefetchScalarGridSpec(
            num_scalar_prefetch=3,
            grid=(batch // nb, ngroups, nsteps),
            in_specs=in_specs,
            out_specs=pl.BlockSpec((nb, TQ, out_w), by_q_tokens(None)),
            scratch_shapes=scratch),
        out_shape=jax.ShapeDtypeStruct((batch, seq, 4 * LANES), BF16),
        compiler_params=_cparams(("parallel", "parallel", "arbitrary")),
        name="flash_" + ("fox" if has_delta else ("sel" if has_u else "win")),
    )(*tabs, *args)


def _cmp_kernel(q_ref, kc_ref, vc_ref, ov_ref, gate_ref, o_ref, u_ref):
    g = pl.program_id(1)
    qi = pl.program_id(2)
    tq = q_ref.shape[2]
    ncmp = kc_ref.shape[2]
    nslc = ov_ref.shape[1]

    t = qi * tq + lax.broadcasted_iota(jnp.int32, (tq, ncmp), 0)
    n = lax.broadcasted_iota(jnp.int32, (tq, ncmp), 1)
    valid = (n * NSA_CMP_STRIDE + (NSA_CMP_LEN - 1)) <= t
    kc = kc_ref[0, 0]
    vc = vc_ref[0, 0]
    ov = ov_ref[...]
    lane = lax.broadcasted_iota(jnp.int32, (tq, LANES), 1)
    lo = lane < HEAD_DIM
    gate_blk = gate_ref[0].astype(F32)

    imp = jnp.zeros((tq, nslc), F32)
    outs = []
    for r in range(NSA_GROUP_HEADS):
        s = lax.dot_general(q_ref[0, r], kc, (((1,), (1,)), ((), ())), preferred_element_type=F32)
        s = jnp.where(valid, s, NEG)
        m = jnp.max(s, axis=1, keepdims=True)
        e = jnp.where(valid, jnp.exp2(s - m), 0.0)
        d = jnp.sum(e, axis=1, keepdims=True)
        pb = (e / jnp.where(d > 0.0, d, 1.0)).astype(BF16)
        o = jnp.dot(pb, vc, preferred_element_type=F32)
        gcol = _gate_column(gate_blk, lane, 3 * (g * NSA_GROUP_HEADS + r))
        outs.append(o * gcol)
        imp = imp + jnp.dot(pb, ov, preferred_element_type=F32)
    for p in range(NSA_GROUP_HEADS // 2):
        o_ref[0, :, p * LANES:(p + 1) * LANES] = jnp.where(lo, outs[2 * p], outs[2 * p + 1]).astype(o_ref.dtype)
    _write_unselected(u_ref, imp, qi, tq, nslc)


def _cmp_fixed_kernel(q_ref, kc_ref, vc_ref, ov_ref, gate_ref, o_ref, u_ref, d_scr, acc_scr, imp_scr,
                      *, chunk):
    g = pl.program_id(1)
    qi = pl.program_id(2)
    tq = q_ref.shape[2]
    ncmp = kc_ref.shape[2]
    nslc = ov_ref.shape[1]
    t0 = qi * tq
    d_scr[...] = jnp.zeros(d_scr.shape, F32)
    acc_scr[...] = jnp.zeros(acc_scr.shape, F32)
    imp_scr[...] = jnp.zeros(imp_scr.shape, F32)

    for c in range(ncmp // chunk):
        first_end = c * chunk * NSA_CMP_STRIDE + NSA_CMP_LEN - 1
        last_end = ((c + 1) * chunk - 1) * NSA_CMP_STRIDE + NSA_CMP_LEN - 1

        def body(masked, c=c):
            rows = slice(c * chunk, (c + 1) * chunk)
            kc = kc_ref[0, 0, rows, :]
            vo = jnp.concatenate([vc_ref[0, 0, rows, :], ov_ref[rows, :]], axis=1)
            if masked:
                t = t0 + lax.broadcasted_iota(jnp.int32, (tq, chunk), 0)
                n = c * chunk + lax.broadcasted_iota(jnp.int32, (tq, chunk), 1)
                valid = (n * NSA_CMP_STRIDE + (NSA_CMP_LEN - 1)) <= t
            for r in range(NSA_GROUP_HEADS):
                s = lax.dot_general(q_ref[0, r], kc, (((1,), (1,)), ((), ())), preferred_element_type=F32)
                if masked:
                    s = jnp.where(valid, s, NEG)
                e = jnp.exp2(s)
                part = e[:, :LANES]
                for j in range(1, chunk // LANES):
                    part = part + e[:, j * LANES:(j + 1) * LANES]
                d_scr[r] = d_scr[r] + part
                both = jnp.dot(e.astype(BF16), vo, preferred_element_type=F32)
                acc_scr[r] = acc_scr[r] + both[:, :LANES]
                imp_scr[r] = imp_scr[r] + both[:, LANES:]

        pl.when((first_end <= t0 + (tq - 1)) & (last_end > t0))(functools.partial(body, True))
        pl.when(last_end <= t0)(functools.partial(body, False))

    lane = lax.broadcasted_iota(jnp.int32, (tq, LANES), 1)
    lo = lane < HEAD_DIM
    gate_blk = gate_ref[0].astype(F32)
    imp = jnp.zeros((tq, nslc), F32)
    outs = []
    for r in range(NSA_GROUP_HEADS):
        d = jnp.sum(d_scr[r], axis=1, keepdims=True)
        inv = 1.0 / jnp.where(d > 0.0, d, 1.0)
        gcol = _gate_column(gate_blk, lane, 3 * (g * NSA_GROUP_HEADS + r))
        outs.append(acc_scr[r] * (inv * gcol))
        imp = imp + imp_scr[r] * inv
    for p in range(NSA_GROUP_HEADS // 2):
        o_ref[0, :, p * LANES:(p + 1) * LANES] = jnp.where(lo, outs[2 * p], outs[2 * p + 1]).astype(o_ref.dtype)
    _write_unselected(u_ref, imp, qi, tq, nslc)


N_FORCED = 3


def _write_unselected(u_ref, imp, qi, tq, nslc):
    tpos = qi * tq + lax.broadcasted_iota(jnp.int32, (tq, nslc), 0)
    jblk = lax.broadcasted_iota(jnp.int32, (tq, nslc), 1)
    cur = jnp.right_shift(tpos, SEL_SHIFT)
    forced = (jblk == 0) | (jblk == cur) | (jblk == cur - 1)
    visible = jblk <= cur
    score_t = jnp.where(visible & jnp.logical_not(forced), imp, -jnp.inf).T

    def run(nrows):
        sc = score_t[:nrows]
        jt = lax.broadcasted_iota(jnp.int32, (nrows, tq), 0).astype(F32)
        for _ in range(NSA_SEL_TOPK - N_FORCED):
            mx = jnp.max(sc, axis=0, keepdims=True)
            first = jnp.min(jnp.where(sc == mx, jt, float(nslc)), axis=0, keepdims=True)
            sc = jnp.where(jt == first, -jnp.inf, sc)
        taken = jnp.where(sc == -jnp.inf, 1.0, 0.0)
        if nrows < nslc:
            taken = jnp.concatenate([taken, jnp.zeros((nslc - nrows, tq), F32)], axis=0)
        chosen = visible & (forced | (taken.T > 0.5))
        u_ref[0, 0] = jnp.where(chosen, 0.0, 1.0).astype(BF16)

    half = nslc // 2
    last_block = jnp.right_shift(qi * tq + (tq - 1), SEL_SHIFT)
    pl.when(last_block < half)(functools.partial(run, half))
    pl.when(last_block >= half)(functools.partial(run, nslc))


def _cmp_select(qn, kc, vc, ov, z, *, batch, seq, fixed_shift):
    ncmp = kc.shape[2]
    nslc = ov.shape[1]
    if fixed_shift:
        chunk = 2 * LANES if ncmp % (2 * LANES) == 0 else ncmp
        kern = functools.partial(_cmp_fixed_kernel, chunk=chunk)
        scratch = [pltpu.VMEM((NSA_GROUP_HEADS, CMP_TQ, LANES), F32) for _ in range(3)]
    else:
        kern, scratch = _cmp_kernel, []
    return pl.pallas_call(
        kern,
        scratch_shapes=scratch,
        grid=(batch, NSA_KV_GROUPS, seq // CMP_TQ),
        in_specs=[pl.BlockSpec((1, NSA_GROUP_HEADS, CMP_TQ, LANES), lambda b, g, i: (b, g, i, 0)),
                  pl.BlockSpec((1, 1, ncmp, LANES), lambda b, g, i: (b, g, 0, 0)),
                  pl.BlockSpec((1, 1, ncmp, LANES), lambda b, g, i: (b, g, 0, 0)),
                  pl.BlockSpec((ncmp, nslc), lambda b, g, i: (0, 0)),
                  pl.BlockSpec((1, CMP_TQ, LANES), lambda b, g, i: (b, i, C_SMALL // LANES))],
        out_specs=[pl.BlockSpec((1, CMP_TQ, 2 * LANES), lambda b, g, i: (b, i, g)),
                   pl.BlockSpec((1, 1, CMP_TQ, nslc), lambda b, g, i: (b, g, i, 0))],
        out_shape=[jax.ShapeDtypeStruct((batch, seq, NSA_WIDTH), BF16),
                   jax.ShapeDtypeStruct((batch, NSA_KV_GROUPS, seq, nslc), BF16)],
        compiler_params=_cparams(("parallel", "parallel", "parallel")),
        name="cmp_select",
    )(qn, kc, vc, ov, z)


def _head_rms(x, g):
    return x * lax.rsqrt(jnp.mean(x * x, axis=-1, keepdims=True) + EPS) * g


def _mem_kernel(zq_ref, kv_ref, qg_ref, kg_ref, o_ref, kn_scr, vb_scr):
    @pl.when(pl.program_id(1) == 0)
    def _prep():
        for h in range(MEM_HEADS):
            kh = kv_ref[0, :, h * MEM_HEAD_DIM:(h + 1) * MEM_HEAD_DIM]
            kn_scr[h] = _head_rms(kh, kg_ref[...]).astype(BF16)
            vb_scr[h] = kv_ref[0, :, MEM_WIDTH + h * MEM_HEAD_DIM:
                               MEM_WIDTH + (h + 1) * MEM_HEAD_DIM].astype(BF16)

    for h in range(MEM_HEADS):
        sl = slice(h * MEM_HEAD_DIM, (h + 1) * MEM_HEAD_DIM)
        qh = (_head_rms(zq_ref[0, :, sl].astype(F32), qg_ref[...])
              * ((MEM_HEAD_DIM ** -0.5) * LOG2E)).astype(BF16)
        s = lax.dot_general(qh, kn_scr[h], (((1,), (1,)), ((), ())), preferred_element_type=F32)
        m = jnp.max(s, axis=1, keepdims=True)
        e = jnp.exp2(s - m)
        p = (e / jnp.sum(e, axis=1, keepdims=True)).astype(BF16)
        o_ref[0, :, sl] = jnp.dot(p, vb_scr[h], preferred_element_type=F32).astype(o_ref.dtype)


def _mem_attention(z, mem_kv, q_g, k_g, *, batch, seq):
    mlen = mem_kv.shape[1]
    tq = TQ
    return pl.pallas_call(
        _mem_kernel,
        grid=(batch, seq // tq),
        in_specs=[pl.BlockSpec((1, tq, MEM_WIDTH), lambda b, i: (b, i, C_MEM_Q // MEM_WIDTH)),
                  pl.BlockSpec((1, mlen, 2 * MEM_WIDTH), lambda b, i: (b, 0, 0)),
                  pl.BlockSpec((1, MEM_HEAD_DIM), lambda b, i: (0, 0)),
                  pl.BlockSpec((1, MEM_HEAD_DIM), lambda b, i: (0, 0))],
        out_specs=pl.BlockSpec((1, tq, MEM_WIDTH), lambda b, i: (b, i, 0)),
        out_shape=jax.ShapeDtypeStruct((batch, seq, MEM_WIDTH), BF16),
        scratch_shapes=[pltpu.VMEM((MEM_HEADS, mlen, MEM_HEAD_DIM), BF16),
                        pltpu.VMEM((MEM_HEADS, mlen, MEM_HEAD_DIM), BF16)],
        compiler_params=_cparams(("parallel", "arbitrary")),
        name="mem_attention",
    )(z, mem_kv, q_g.reshape(1, MEM_HEAD_DIM), k_g.reshape(1, MEM_HEAD_DIM))


def _out_kernel(x_ref, oc_ref, os_ref, ow_ref, ob_ref, om_ref, sa_ref, sb_ref, sm_ref,
                g0_ref, g1_ref, g2_ref, wa_ref, wb_ref, wm_ref, wo_ref, y_ref):
    oa = ((oc_ref[...].astype(F32) + os_ref[...].astype(F32) + ow_ref[...].astype(F32))
          * jax.nn.silu(sa_ref[...].astype(F32)))
    ob = ob_ref[...].astype(F32) * jax.nn.silu(sb_ref[...].astype(F32))
    om = om_ref[...].astype(F32) * jax.nn.silu(sm_ref[...].astype(F32))
    gate = lambda ref: jax.nn.sigmoid(ref[...].astype(F32))
    u = (gate(g0_ref) * jnp.dot(oa.astype(BF16), wa_ref[...], preferred_element_type=F32)
         + gate(g1_ref) * jnp.dot(ob.astype(BF16), wb_ref[...], preferred_element_type=F32)
         + gate(g2_ref) * jnp.dot(om.astype(BF16), wm_ref[...], preferred_element_type=F32))
    y_ref[...] = x_ref[...] + jnp.dot(u.astype(BF16), wo_ref[...], preferred_element_type=F32)


def _out_proj(x2d, o_c, o_s, o_w, o_b, o_m, z2d, wa, wb, wm, wo):
    m = x2d.shape[0]
    tm = 256
    w512 = 512
    row512 = lambda c: pl.BlockSpec((tm, w512), lambda i, c=c: (i, c))
    row1024 = lambda c: pl.BlockSpec((tm, D_MODEL), lambda i, c=c: (i, c))
    full = lambda shape: pl.BlockSpec(shape, lambda i: (0, 0))
    return pl.pallas_call(
        _out_kernel,
        grid=(m // tm,),
        in_specs=[row1024(0), row512(0), row512(0), row512(0), row512(0), row512(0),
                  row512(C_NSA_SILU // w512), row512(C_FOX_SILU // w512), row512(C_MEM_SILU // w512),
                  row1024(C_MERGE // D_MODEL), row1024(C_MERGE // D_MODEL + 1),
                  row1024(C_MERGE // D_MODEL + 2),
                  full((w512, D_MODEL)), full((w512, D_MODEL)), full((w512, D_MODEL)),
                  full((D_MODEL, D_MODEL))],
        out_specs=row1024(0),
        out_shape=jax.ShapeDtypeStruct((m, D_MODEL), F32),
        compiler_params=_cparams(("parallel",)),
        name="out_proj",
    )(x2d, o_c, o_s, o_w, o_b, o_m, z2d, z2d, z2d, z2d, z2d, z2d, wa, wb, wm, wo)


def _permute_w_in(w_in):
    o = np.cumsum([0, NSA_WIDTH, 6 * NSA_KV_GROUPS * HEAD_DIM, 3 * NSA_HEADS, NSA_WIDTH,
                   3 * FOX_WIDTH, FOX_HEADS, FOX_WIDTH, MEM_WIDTH, MEM_WIDTH, 3 * D_MODEL])
    nsa_q, nsa_kv, nsa_gate, nsa_silu, fox_qkv, fox_f, fox_silu, mem_q, mem_silu, merge = [
        w_in[..., o[i]:o[i + 1]] for i in range(10)]
    pad = jnp.zeros(w_in.shape[:2] + (LANES - 3 * NSA_HEADS - FOX_HEADS,), w_in.dtype)
    return jnp.concatenate([nsa_q, nsa_silu, fox_qkv, fox_silu, mem_q, mem_silu, merge, nsa_kv,
                            nsa_gate, fox_f, pad], axis=-1).astype(BF16)


def _pad_lanes(x, width=LANES):
    return jnp.pad(x, [(0, 0)] * (x.ndim - 1) + [(0, width - x.shape[-1])])


def _tile2(g):
    return jnp.concatenate([g, g], axis=-1)[:, None, :]


def _overlap(n_cmp_pad, n_slc):
    cs = np.arange(n_cmp_pad)[:, None] * NSA_CMP_STRIDE
    ss = np.arange(NSLC_PAD)[None, :] * NSA_SEL_LEN
    ov = np.clip(np.minimum(cs + NSA_CMP_LEN, ss + NSA_SEL_LEN) - np.maximum(cs, ss), 0, None)
    ov = ov / NSA_CMP_LEN
    ov[-1] = 0.0
    ov[:, n_slc:] = 0.0
    return jnp.asarray(ov, BF16)


def _aug_permutations():
    pq = np.zeros((6 * LANES, FOX_HEADS * LANES), np.float32)
    pk = np.zeros((6 * LANES, FOX_HEADS * LANES), np.float32)
    for h in range(FOX_HEADS):
        for i in range(3):
            pq[i * LANES + F_LANE + h, h * LANES + AUG + i] = 1.0
            pk[i * LANES + F_LANE + h, h * LANES + AUG + 3 + i] = -1.0
            pk[(3 + i) * LANES + F_LANE + h, h * LANES + AUG + 6 + i] = -1.0
    return jnp.asarray(pq, BF16), jnp.asarray(pk, BF16)


def _compress_w1(w1):
    nl = w1.shape[0]
    halves = w1.reshape(nl, 2, NSA_CMP_STRIDE, 1, HEAD_DIM, NSA_CMP_HIDDEN)
    eye = jnp.eye(NSA_KV_GROUPS, dtype=w1.dtype)
    out = jnp.einsum("pg,zhldn->zlpdghn", eye, halves[:, :, :, 0])
    return out.reshape(nl, NSA_CMP_STRIDE * NSA_KV_GROUPS * HEAD_DIM,
                       NSA_KV_GROUPS * 2 * NSA_CMP_HIDDEN).astype(BF16)


def _layer(x, mem, p, consts):
    batch, seq, _ = x.shape
    m_rows = batch * seq
    cos4, sin_signed, tri, head_ones, perm_q, perm_k, ov = consts
    z2d = _norm_matmul(x.reshape(m_rows, D_MODEL), p["norm_g"], p["w_in"], 512, Z_WIDTH // 3, BF16)
    z = z2d.reshape(batch, seq, Z_WIDTH)

    (qn, kcr, vcr, ks, vs, kw, vw, fqd, fqo, fka, base) = _prep(
        z, cos4, sin_signed, tri, head_ones, perm_q, perm_k,
        p["gq"], p["gks"], p["gkw"], p["gfq"], p["gfk"], p["fb"],
        p["s_sel"], p["s_win"], p["s_fox"], batch=batch, seq=seq)

    nchunk = seq // NSA_CMP_STRIDE
    kc, vc = _compress(kcr.reshape(batch, nchunk, NSA_CMP_STRIDE * LANES),
                       vcr.reshape(batch, nchunk, NSA_CMP_STRIDE * LANES),
                       p["w1k"], p["w1v"], p["bk"], p["bv"], p["w2k"], p["w2v"], p["gkc"], p["s_cmp"],
                       batch=batch)

    def attend(fixed_shift):
        o_c, unsel = _cmp_select(qn, kc, vc, ov, z, batch=batch, seq=seq, fixed_shift=fixed_shift)
        o_s = _flash(qn, ks, vs, batch=batch, seq=seq, shared_kv=True, fixed_shift=fixed_shift,
                     u=unsel, z=z, gate_branch=1)
        o_w = _flash(qn, kw, vw, batch=batch, seq=seq, shared_kv=True, fixed_shift=fixed_shift,
                     window=NSA_WINDOW, z=z, gate_branch=2)
        o_b = _flash(fqd, fka, z, batch=batch, seq=seq, shared_kv=False, fixed_shift=fixed_shift,
                     base=base.reshape(-1), q_off=fqo, v_col=C_FOX_V // FOX_WIDTH)
        return o_c, o_s, o_w, o_b

    o_c, o_s, o_w, o_b = lax.cond(p["bound_ok"], lambda: attend(True), lambda: attend(False))

    mlen = mem.shape[1]
    mem_kv = _norm_matmul(mem.reshape(batch * mlen, D_MODEL), p["mem_norm_g"], p["w_mem_kv"],
                          min(512, batch * mlen), 2 * MEM_WIDTH, F32).reshape(batch, mlen, 2 * MEM_WIDTH)
    o_m = _mem_attention(z, mem_kv, p["mem_q_norm"], p["mem_k_norm"], batch=batch, seq=seq)

    y = _out_proj(x.reshape(m_rows, D_MODEL), o_c.reshape(m_rows, -1), o_s.reshape(m_rows, -1),
                  o_w.reshape(m_rows, -1), o_b.reshape(m_rows, -1), o_m.reshape(m_rows, -1), z2d,
                  p["w_branch_a"], p["w_branch_b"], p["w_branch_m"], p["w_out"])
    return y.reshape(batch, seq, D_MODEL)


def kernel(x, mem, norm_g, mem_norm_g, w_in, nsa_q_norm, nsa_k_norm, cmp_pe_k, cmp_w1_k, cmp_w2_k,
           cmp_pe_v, cmp_w1_v, cmp_w2_v, fox_q_norm, fox_k_norm, fox_f_bias, mem_q_norm, mem_k_norm,
           w_mem_kv, w_branch_a, w_branch_b, w_branch_m, w_out):
    batch, seq, _ = x.shape
    depth = w_in.shape[0]
    n_slc = seq // NSA_SEL_LEN
    assert n_slc <= NSLC_PAD and seq % TQ == 0 and NSA_WINDOW == TQ

    half = HEAD_DIM // 2
    inv_freq = ROPE_THETA ** (-jnp.arange(half, dtype=F32) / half)
    ang = jnp.arange(seq).astype(F32)[:, None] * inv_freq[None, :]
    cos, sin = jnp.cos(ang), jnp.sin(ang)
    cos4 = jnp.concatenate([cos, cos, cos, cos], axis=-1)
    sin_signed = jnp.concatenate([-sin, sin, -sin, sin], axis=-1)
    tri = jnp.asarray(np.tril(np.ones((TQ, TQ), np.float32)), BF16)
    head_ones = jnp.asarray(np.kron(np.eye(2), np.ones((HEAD_DIM, HEAD_DIM))), BF16)
    consts = (cos4, sin_signed, tri, head_ones, *_aug_permutations(),
              _overlap(seq // NSA_CMP_STRIDE, n_slc))

    hp = lax.Precision.HIGHEST
    pe_bias = lambda pe, w1: jnp.einsum("lk,lkn->ln", pe.reshape(depth, -1), w1, precision=hp)[:, None, :]
    fb = jnp.zeros((depth, 1, LANES), F32).at[:, 0, F_LANE:F_LANE + FOX_HEADS].set(fox_f_bias)

    def logit_bound(gq, gk):
        bound = (BOUND_MARGIN * HEAD_DIM * QK_SCALE
                 * jnp.max(jnp.abs(gq), axis=-1) * jnp.max(jnp.abs(gk), axis=-1))
        return bound.astype(BF16).astype(F32)

    bounds = [logit_bound(nsa_q_norm, nsa_k_norm[:, 1]), logit_bound(nsa_q_norm, nsa_k_norm[:, 2]),
              logit_bound(fox_q_norm, fox_k_norm), logit_bound(nsa_q_norm, nsa_k_norm[:, 0])]
    bound_ok = functools.reduce(jnp.maximum, bounds) <= MAX_FIXED_BOUND
    bound_row = lambda s: jnp.broadcast_to(s[:, None, None], (depth, 1, LANES))
    stacked = dict(
        s_sel=bound_row(bounds[0]), s_win=bound_row(bounds[1]), s_fox=bound_row(bounds[2]),
        s_cmp=bound_row(bounds[3]),
        bound_ok=bound_ok,
        norm_g=norm_g, mem_norm_g=mem_norm_g, w_in=_permute_w_in(w_in),
        gq=_tile2(nsa_q_norm), gks=_tile2(nsa_k_norm[:, 1]), gkw=_tile2(nsa_k_norm[:, 2]),
        gkc=_pad_lanes(nsa_k_norm[:, 0])[:, None, :], gfq=_tile2(fox_q_norm), gfk=_tile2(fox_k_norm),
        fb=fb, w1k=_compress_w1(cmp_w1_k), w1v=_compress_w1(cmp_w1_v),
        bk=pe_bias(cmp_pe_k, cmp_w1_k), bv=pe_bias(cmp_pe_v, cmp_w1_v),
        w2k=_pad_lanes(cmp_w2_k).astype(BF16),
        w2v=jnp.concatenate([cmp_w2_v, cmp_w2_v], axis=-1).astype(BF16),
        mem_q_norm=mem_q_norm, mem_k_norm=mem_k_norm, w_mem_kv=w_mem_kv.astype(BF16),
        w_branch_a=w_branch_a.astype(BF16), w_branch_b=w_branch_b.astype(BF16),
        w_branch_m=w_branch_m.astype(BF16), w_out=w_out.astype(BF16))
    for l in range(depth):
        x = _layer(x, mem, {k: v[l] for k, v in stacked.items()}, consts)
    return x
```

```python
import functools

import numpy as np
import jax
import jax.numpy as jnp
from jax import lax
from jax.experimental import pallas as pl
from jax.experimental.pallas import tpu as pltpu

F32 = jnp.float32
BF16 = jnp.bfloat16

D_MODEL = 1024
HEAD_DIM = 64
ROPE_THETA = 10000.0
EPS = 1e-6
NSA_HEADS = 8
NSA_KV_GROUPS = 2
NSA_GROUP_HEADS = NSA_HEADS // NSA_KV_GROUPS
NSA_CMP_LEN = 32
NSA_CMP_STRIDE = 16
NSA_CMP_HIDDEN = 128
NSA_SEL_LEN = 64
NSA_SEL_TOPK = 16
NSA_WINDOW = 512
FORCE_SCORE = 1e9
NSA_WIDTH = NSA_HEADS * HEAD_DIM
FOX_HEADS = 8
FOX_WIDTH = FOX_HEADS * HEAD_DIM
MEM_HEADS = 4
MEM_HEAD_DIM = 128
MEM_WIDTH = MEM_HEADS * MEM_HEAD_DIM

LANES = 128
VMEM_LIMIT = 56 * 1024 * 1024
NEG = -1e30
SEL_MASK = 32768.0
SEL_SHIFT = 6
NSLC_PAD = LANES
LOG2E = 1.4426950408889634
QK_SCALE = (HEAD_DIM ** -0.5) * LOG2E
BOUND_MARGIN = 1.02
MAX_FIXED_BOUND = 60.0
ZERO_WEIGHT_LOG2 = -160.0

C_NSA_Q = 0
C_NSA_SILU = 512
C_FOX_Q = 1024
C_FOX_K = 1536
C_FOX_V = 2048
C_FOX_SILU = 2560
C_MEM_Q = 3072
C_MEM_SILU = 3584
C_MERGE = 4096
C_NSA_KV = 7168
C_SMALL = 7936
Z_WIDTH = 8064
F_LANE = 3 * NSA_HEADS
AUG = HEAD_DIM

TQ = 512
FLASH_ROWS = 4
SAFE_FLASH_ROWS = 2
CMP_TQ = 512


def _cparams(sem):
    return pltpu.CompilerParams(dimension_semantics=sem, vmem_limit_bytes=VMEM_LIMIT)


def _norm_matmul_kernel(x_ref, g_ref, w_ref, o_ref):
    x = x_ref[...]
    ms = jnp.mean(x * x, axis=-1, keepdims=True)
    h = (x * lax.rsqrt(ms + EPS) * g_ref[...]).astype(BF16)
    o_ref[...] = jnp.dot(h, w_ref[...], preferred_element_type=F32).astype(o_ref.dtype)


def _norm_matmul(x2d, g, w, tm, tn, out_dtype):
    m, k = x2d.shape
    n = w.shape[1]
    return pl.pallas_call(
        _norm_matmul_kernel,
        grid=(n // tn, m // tm),
        in_specs=[pl.BlockSpec((tm, k), lambda j, i: (i, 0)),
                  pl.BlockSpec((1, k), lambda j, i: (0, 0)),
                  pl.BlockSpec((k, tn), lambda j, i: (0, j))],
        out_specs=pl.BlockSpec((tm, tn), lambda j, i: (i, j)),
        out_shape=jax.ShapeDtypeStruct((m, n), out_dtype),
        compiler_params=_cparams(("parallel", "parallel")),
        name="norm_matmul",
    )(x2d, g.reshape(1, k), w)


def _pair_rms(x, g2, head_ones):
    sq = x * x
    hi = sq.astype(BF16)
    low = (sq - hi.astype(F32)).astype(BF16)
    ssq = (jnp.dot(hi, head_ones, preferred_element_type=F32)
           + jnp.dot(low, head_ones, preferred_element_type=F32))
    return x * lax.rsqrt(ssq * (1.0 / HEAD_DIM) + EPS) * g2


def _pair_rope(y, cos4, sin_signed, first_half):
    rot = jnp.where(first_half, pltpu.roll(y, LANES - HEAD_DIM // 2, 1), pltpu.roll(y, HEAD_DIM // 2, 1))
    return y * cos4 + rot * sin_signed


def _head_a(y, lo, tail=0.0):
    return jnp.where(lo, y, tail)


def _head_b(y, lo, tail=0.0):
    return jnp.where(lo, pltpu.roll(y, HEAD_DIM, 1), tail)


def _split3_f32(c):
    hi = c.astype(BF16).astype(F32)
    r = c - hi
    mid = r.astype(BF16).astype(F32)
    return hi, mid, r - mid


def _prep_kernel(zq_ref, zc_ref, zs_ref, zw_ref, fq_ref, fk_ref, sm_ref, cos_ref, sin_ref,
                 tri_ref, ones_ref, pq_ref, pk_ref, gq_ref, gks_ref, gkw_ref, gfq_ref, gfk_ref, fb_ref,
                 ss_ref, sw_ref, sf_ref,
                 qn_ref, kcr_ref, vcr_ref, ks_ref, vs_ref, kw_ref, vw_ref, fqd_ref, fqo_ref, fka_ref,
                 base_ref, run_scr):
    ti = pl.program_id(1)
    tt = zq_ref.shape[1]
    lane = lax.broadcasted_iota(jnp.int32, (tt, LANES), 1)
    lo = lane < HEAD_DIM
    first_half = (lane & (HEAD_DIM // 2)) == 0
    cos4 = cos_ref[...]
    sin_signed = sin_ref[...]
    head_ones = ones_ref[...]
    one_at_aug = jnp.where(lane == AUG, 1.0, 0.0)

    for p in range(NSA_HEADS // 2):
        y = _pair_rope(_pair_rms(zq_ref[0, :, p * LANES:(p + 1) * LANES].astype(F32), gq_ref[...], head_ones),
                       cos4, sin_signed, first_half) * QK_SCALE
        qn_ref[0, 2 * p] = _head_a(y, lo, one_at_aug).astype(BF16)
        qn_ref[0, 2 * p + 1] = _head_b(y, lo, one_at_aug).astype(BF16)

    kcr_ref[0] = _pair_rope(zc_ref[0, :, :LANES].astype(F32), cos4, sin_signed, first_half).astype(BF16)
    vcr_ref[0] = zc_ref[0, :, LANES:].astype(BF16)

    sel_lane = jnp.right_shift(ti * tt + lax.broadcasted_iota(jnp.int32, (tt, LANES), 0), SEL_SHIFT)
    onehot = jnp.where(lane == sel_lane, -SEL_MASK, 0.0).astype(BF16)
    for z_ref, g_ref, k_ref, v_ref, bound_ref, with_onehot in (
            (zs_ref, gks_ref, ks_ref, vs_ref, ss_ref, True),
            (zw_ref, gkw_ref, kw_ref, vw_ref, sw_ref, False)):
        y = _pair_rope(_pair_rms(z_ref[0, :, :LANES].astype(F32), g_ref[...], head_ones),
                       cos4, sin_signed, first_half)
        v = z_ref[0, :, LANES:].astype(F32)
        vr = pltpu.roll(v, HEAD_DIM, 1)
        neg_bound = jnp.where(lane == AUG, -bound_ref[...], 0.0)
        k_ref[0, 0, :, :LANES] = _head_a(y, lo, neg_bound).astype(BF16)
        k_ref[0, 1, :, :LANES] = _head_b(y, lo, neg_bound).astype(BF16)
        if with_onehot:
            k_ref[0, 0, :, LANES:] = onehot
            k_ref[0, 1, :, LANES:] = onehot
        v_ref[0, 0] = jnp.where(lo, v, vr).astype(BF16)
        v_ref[0, 1] = jnp.where(lo, vr, v).astype(BF16)

    @pl.when(ti == 0)
    def _reset():
        run_scr[...] = jnp.zeros(run_scr.shape, F32)

    xf = sm_ref[0].astype(F32) + fb_ref[...]
    log_f = (jnp.minimum(xf, 0.0) - jnp.log1p(jnp.exp(-jnp.abs(xf)))) * LOG2E
    tri = tri_ref[...]
    parts = _split3_f32(log_f)
    cum = sum(jnp.dot(tri, part.astype(BF16), preferred_element_type=F32) for part in parts)
    base_ref[0, 0] = run_scr[...]
    run_scr[...] = run_scr[...] + cum[tt - 1:tt, :]

    parts6 = _split3_f32(cum) + _split3_f32(cum - cum[tt - 1:tt, :])
    x6 = jnp.concatenate([part.astype(BF16) for part in parts6], axis=1)
    aug_q = jnp.dot(x6, pq_ref[...], preferred_element_type=F32)
    aug_k = jnp.dot(x6, pk_ref[...], preferred_element_type=F32)

    lane1 = lax.broadcasted_iota(jnp.int32, (1, LANES), 1)
    span = lambda a, b: jnp.where((lane1 >= a) & (lane1 < b), 1.0, 0.0)
    q_tail_diag = span(AUG + 3, AUG + 6) + span(AUG + 9, AUG + 10)
    q_tail_off = span(AUG + 6, AUG + 10)
    k_tail = span(AUG, AUG + 3) + jnp.where(lane1 == AUG + 9, -sf_ref[...], 0.0)
    for p in range(FOX_HEADS // 2):
        sl = slice(p * LANES, (p + 1) * LANES)
        yq = _pair_rms(fq_ref[0, :, sl].astype(F32), gfq_ref[...], head_ones) * QK_SCALE
        yk = _pair_rms(fk_ref[0, :, sl].astype(F32), gfk_ref[...], head_ones)
        for e, split in ((0, _head_a), (1, _head_b)):
            h = 2 * p + e
            hs = slice(h * LANES, (h + 1) * LANES)
            qh = split(yq, lo) + aug_q[:, hs]
            fqd_ref[0, h] = (qh + q_tail_diag).astype(BF16)
            fqo_ref[0, h] = (qh + q_tail_off).astype(BF16)
            fka_ref[0, h] = (split(yk, lo) + aug_k[:, hs] + k_tail).astype(BF16)


def _prep(z, cos4, sin_signed, tri, head_ones, perm_q, perm_k, gq, gks, gkw, gfq, gfk, fb,
          s_sel, s_win, s_fox, *, batch, seq):
    nblk = seq // TQ
    whole = lambda a: pl.BlockSpec(a.shape, lambda b, i: (0, 0))
    zspec = lambda w, c: pl.BlockSpec((1, TQ, w), lambda b, i, c=c: (b, i, c))
    row = pl.BlockSpec((1, LANES), lambda b, i: (0, 0))
    tab = pl.BlockSpec((TQ, LANES), lambda b, i: (i, 0))
    heads = lambda n, w: pl.BlockSpec((1, n, TQ, w), lambda b, i: (b, 0, i, 0))
    tok = lambda w: pl.BlockSpec((1, TQ, w), lambda b, i: (b, i, 0))
    bf = lambda *shape: jax.ShapeDtypeStruct(shape, BF16)
    return pl.pallas_call(
        _prep_kernel,
        grid=(batch, nblk),
        in_specs=[zspec(NSA_WIDTH, C_NSA_Q // NSA_WIDTH),
                  zspec(2 * LANES, C_NSA_KV // (2 * LANES)),
                  zspec(2 * LANES, C_NSA_KV // (2 * LANES) + 1),
                  zspec(2 * LANES, C_NSA_KV // (2 * LANES) + 2),
                  zspec(FOX_WIDTH, C_FOX_Q // FOX_WIDTH),
                  zspec(FOX_WIDTH, C_FOX_K // FOX_WIDTH),
                  zspec(LANES, C_SMALL // LANES),
                  tab, tab, whole(tri), whole(head_ones), whole(perm_q), whole(perm_k),
                  row, row, row, row, row, row, row, row, row],
        out_specs=[heads(NSA_HEADS, LANES), tok(LANES), tok(LANES),
                   heads(NSA_KV_GROUPS, 2 * LANES), heads(NSA_KV_GROUPS, LANES),
                   heads(NSA_KV_GROUPS, LANES), heads(NSA_KV_GROUPS, LANES),
                   heads(FOX_HEADS, LANES), heads(FOX_HEADS, LANES), heads(FOX_HEADS, LANES),
                   pl.BlockSpec((1, 1, 1, LANES), lambda b, i: (b, i, 0, 0))],
        out_shape=[bf(batch, NSA_HEADS, seq, LANES), bf(batch, seq, LANES), bf(batch, seq, LANES),
                   bf(batch, NSA_KV_GROUPS, seq, 2 * LANES), bf(batch, NSA_KV_GROUPS, seq, LANES),
                   bf(batch, NSA_KV_GROUPS, seq, LANES), bf(batch, NSA_KV_GROUPS, seq, LANES),
                   bf(batch, FOX_HEADS, seq, LANES), bf(batch, FOX_HEADS, seq, LANES),
                   bf(batch, FOX_HEADS, seq, LANES),
                   jax.ShapeDtypeStruct((batch, nblk, 1, LANES), F32)],
        scratch_shapes=[pltpu.VMEM((1, LANES), F32)],
        compiler_params=_cparams(("parallel", "arbitrary")),
        name="prep",
    )(z, z, z, z, z, z, z, cos4, sin_signed, tri, head_ones, perm_q, perm_k,
      gq, gks, gkw, gfq, gfk, fb, s_sel, s_win, s_fox)


def _compress_kernel(xk_ref, xv_ref, w1k_ref, w1v_ref, bk_ref, bv_ref, w2k_ref, w2v_ref, kg_ref,
                     sc_ref, kc_ref, vc_ref):
    nchunk = xk_ref.shape[1]
    lane = lax.broadcasted_iota(jnp.int32, (nchunk, LANES), 1)
    for x_ref, w1_ref, b_ref, w2_ref, o_ref, is_key in ((xk_ref, w1k_ref, bk_ref, w2k_ref, kc_ref, True),
                                                        (xv_ref, w1v_ref, bv_ref, w2v_ref, vc_ref, False)):
        h = jnp.dot(x_ref[0], w1_ref[...], preferred_element_type=F32)
        for g in range(NSA_KV_GROUPS):
            c0 = g * 2 * NSA_CMP_HIDDEN
            top = h[:, c0:c0 + NSA_CMP_HIDDEN]
            bot = h[:, c0 + NSA_CMP_HIDDEN:c0 + 2 * NSA_CMP_HIDDEN]
            hid = top + pltpu.roll(bot, nchunk - 1, 0) + b_ref[...]
            act = (hid * jax.nn.sigmoid(hid)).astype(BF16)
            o = jnp.dot(act, w2_ref[...], preferred_element_type=F32)
            if is_key:
                ms = jnp.sum(o * o, axis=1, keepdims=True) * (1.0 / HEAD_DIM)
                o = o * lax.rsqrt(ms + EPS) * kg_ref[...]
                o = jnp.where(lane == AUG, -sc_ref[...], o)
            o_ref[0, g] = o.astype(BF16)


def _compress(xk, xv, w1k, w1v, bk, bv, w2k, w2v, kg, s_cmp, *, batch):
    nchunk, kin = xk.shape[1], xk.shape[2]
    xspec = pl.BlockSpec((1, nchunk, kin), lambda b: (b, 0, 0))
    full = lambda a: pl.BlockSpec(a.shape, lambda b: (0,) * a.ndim)
    ospec = pl.BlockSpec((1, NSA_KV_GROUPS, nchunk, LANES), lambda b: (b, 0, 0, 0))
    oshape = jax.ShapeDtypeStruct((batch, NSA_KV_GROUPS, nchunk, LANES), BF16)
    return pl.pallas_call(
        _compress_kernel,
        grid=(batch,),
        in_specs=[xspec, xspec, full(w1k), full(w1v), full(bk), full(bv), full(w2k), full(w2v), full(kg),
                  full(s_cmp)],
        out_specs=[ospec, ospec],
        out_shape=[oshape, oshape],
        compiler_params=_cparams(("parallel",)),
        name="compress",
    )(xk, xv, w1k, w1v, bk, bv, w2k, w2v, kg, s_cmp)


def _gate_column(gate_blk, lane, idx):
    col = jnp.sum(jnp.where(lane == idx, gate_blk, 0.0), axis=1, keepdims=True)
    return jax.nn.sigmoid(col)


def _flash_kernel(qi_tab, ki_tab, fl_tab, *refs, nheads, shared_kv, window, has_delta,
                  has_u, gate_branch, nblk, fixed_shift, nb):
    refs = list(refs)
    base_ref = refs.pop(0) if has_delta else None
    q_ref = refs.pop(0)
    qoff_ref = refs.pop(0) if has_delta else None
    u_ref = refs.pop(0) if has_u else None
    k_ref = refs.pop(0)
    v_ref = refs.pop(0)
    gate_ref = refs.pop(0) if gate_branch is not None else None
    o_ref = refs.pop(0)
    m_scr = refs.pop(0)
    l_scr = refs.pop(0)
    acc_scr = refs.pop(0)
    qcat_scr = refs.pop(0) if has_u else None

    b = pl.program_id(0)
    g = pl.program_id(1)
    step = b * pl.num_programs(2) + pl.program_id(2)
    qi = qi_tab[step]
    ki = ki_tab[step]
    fl = fl_tab[step]
    active = (fl & 4) == 4
    tq = q_ref.shape[2]
    tk = k_ref.shape[2]

    @pl.when((fl & 1) == 1)
    def _init():
        m_scr[...] = jnp.full(m_scr.shape, NEG, F32)
        l_scr[...] = jnp.zeros(l_scr.shape, F32)
        acc_scr[...] = jnp.zeros(acc_scr.shape, F32)
        if has_u:
            for bb in range(nb):
                for r in range(nheads):
                    qcat_scr[bb * nheads + r, :, :LANES] = q_ref[bb, r]
                    qcat_scr[bb * nheads + r, :, LANES:] = u_ref[bb, r // NSA_GROUP_HEADS]

    lane = lax.broadcasted_iota(jnp.int32, (tq, LANES), 1)
    lo = lane < HEAD_DIM

    def tile(diagonal):
        masked = diagonal or window is not None
        if masked:
            row = lax.broadcasted_iota(jnp.int32, (tq, tk), 0)
            col = lax.broadcasted_iota(jnp.int32, (tq, tk), 1)
            if diagonal:
                valid = row >= col
            else:
                valid = (tq + row - col) < window
        for bb in range(nb):
            slot0, pair0, brow = bb * nheads, bb * (nheads // 2), b * nb + bb
            for p in range(nheads // 2):
                pvs, alphas = [], []
                for e in range(2):
                    hh = 2 * p + e
                    if has_u:
                        q = qcat_scr[slot0 + hh]
                    elif has_delta and not diagonal:
                        q = qoff_ref[bb, hh]
                    else:
                        q = q_ref[bb, hh]
                    k = k_ref[bb, hh // NSA_GROUP_HEADS] if shared_kv else k_ref[bb, hh]
                    v = (v_ref[bb, hh // NSA_GROUP_HEADS] if shared_kv
                         else v_ref[bb, :, p * LANES:(p + 1) * LANES])
                    s = lax.dot_general(q, k, (((1,), (1,)), ((), ())), preferred_element_type=F32)
                    if masked:
                        s = jnp.where(valid, s, NEG)
                    delta = None
                    if has_delta and not diagonal:
                        hoff = F_LANE + g * nheads + hh
                        delta = (base_ref[(brow * nblk + qi) * LANES + hoff]
                                 - base_ref[(brow * nblk + ki + 1) * LANES + hoff])
                    if fixed_shift:
                        pm = jnp.exp2(s)
                        rowsum = pm[:, :LANES]
                        for j in range(1, tk // LANES):
                            rowsum = rowsum + pm[:, j * LANES:(j + 1) * LANES]
                        pv = jnp.dot(pm.astype(BF16), v, preferred_element_type=F32)
                        if delta is not None:
                            w = jnp.exp2(jnp.full((1, LANES), delta, F32))
                            rowsum = rowsum * w
                            pv = pv * w
                        l_scr[slot0 + hh] = l_scr[slot0 + hh] + rowsum
                        pvs.append(pv)
                    else:
                        m_prev = m_scr[slot0 + hh]
                        m_cur = jnp.max(s, axis=1, keepdims=True)
                        if delta is not None:
                            m_cur = m_cur + delta
                        m_new = jnp.maximum(m_prev, m_cur)
                        alpha = jnp.exp2(m_prev - m_new)
                        shift = m_new[:, :1]
                        if delta is not None:
                            shift = shift - delta
                        pm = jnp.exp2(s - shift)
                        l_scr[slot0 + hh] = alpha * l_scr[slot0 + hh] + jnp.sum(pm, axis=1, keepdims=True)
                        m_scr[slot0 + hh] = m_new
                        pvs.append(jnp.dot(pm.astype(BF16), v, preferred_element_type=F32))
                        alphas.append(alpha)
                if fixed_shift:
                    acc_scr[pair0 + p] = acc_scr[pair0 + p] + jnp.where(lo, pvs[0], pvs[1])
                else:
                    acc_scr[pair0 + p] = (acc_scr[pair0 + p] * jnp.where(lo, alphas[0], alphas[1])
                                          + jnp.where(lo, pvs[0], pvs[1]))

    pl.when((qi == ki) & active)(lambda: tile(True))
    pl.when((qi != ki) & active)(lambda: tile(False))

    @pl.when((fl & 2) == 2)
    def _finish():
        for bb in range(nb):
            slot0, pair0 = bb * nheads, bb * (nheads // 2)
            for p in range(nheads // 2):
                if fixed_shift:
                    la = jnp.sum(l_scr[slot0 + 2 * p], axis=1, keepdims=True)
                    lb = jnp.sum(l_scr[slot0 + 2 * p + 1], axis=1, keepdims=True)
                else:
                    la, lb = l_scr[slot0 + 2 * p], l_scr[slot0 + 2 * p + 1]
                inv = jnp.where(lo, 1.0 / la, 1.0 / lb)
                if gate_branch is not None:
                    gate_blk = gate_ref[bb].astype(F32)
                    h0 = g * nheads + 2 * p
                    ga = _gate_column(gate_blk, lane, 3 * h0 + gate_branch)
                    gb = _gate_column(gate_blk, lane, 3 * (h0 + 1) + gate_branch)
                    inv = inv * jnp.where(lo, ga, gb)
                o_ref[bb, :, p * LANES:(p + 1) * LANES] = (acc_scr[pair0 + p] * inv).astype(o_ref.dtype)


def _tile_tables(nq, window_tiles, batch):
    qi, ki, fl = [], [], []
    for i in range(nq):
        lo = 0 if window_tiles is None else max(0, i - window_tiles)
        for j in range(lo, i + 1):
            qi.append(i)
            ki.append(j)
            fl.append((1 if j == lo else 0) | (2 if j == i else 0) | 4)
    rows = lambda a: jnp.tile(jnp.asarray(a, jnp.int32), batch)
    return rows(qi), rows(ki), rows(fl)


def _decayed_tile_tables(nq, batch, nb, base):
    qi = np.repeat(np.arange(nq), np.arange(1, nq + 1))
    ki = np.concatenate([np.arange(i + 1) for i in range(nq)])
    nsteps = qi.shape[0]
    diag = jnp.asarray(qi == ki)
    gate = base.reshape(batch, nq, LANES)[:, :, F_LANE:F_LANE + FOX_HEADS]
    delta = gate[:, qi, :] - gate[:, np.minimum(ki + 1, nq - 1), :]
    top = jnp.max(delta.reshape(batch // nb, nb, nsteps, FOX_HEADS), axis=(1, 3))
    active = diag[None, :] | (top > ZERO_WEIGHT_LOG2)
    same_q = jnp.asarray(qi[:, None] == np.arange(nq)[None, :])
    first_ki = jnp.min(jnp.where(active[:, :, None] & same_q[None], ki[None, :, None], nq), axis=1)
    first = active & (jnp.asarray(ki)[None, :] == first_ki[:, qi])
    flags = first.astype(jnp.int32) + 2 * diag[None, :].astype(jnp.int32) + 4 * active.astype(jnp.int32)
    order = jnp.argsort(jnp.logical_not(active), axis=1, stable=True)
    n_active = jnp.sum(active, axis=1, keepdims=True)
    slot = jnp.arange(nsteps)[None, :]
    src = jnp.take_along_axis(order, jnp.minimum(slot, n_active - 1), axis=1)
    kept = slot < n_active
    take = lambda a: jnp.asarray(a, jnp.int32)[src].reshape(-1)
    fl = jnp.where(kept, jnp.take_along_axis(flags, src, axis=1), 0).reshape(-1)
    return take(qi), take(ki), fl.astype(jnp.int32)


def _flash(q, k, v, *, batch, seq, shared_kv, fixed_shift, window=None, base=None, q_off=None,
           u=None, z=None, gate_branch=None, v_col=0):
    nheads = NSA_HEADS
    nb = FLASH_ROWS if fixed_shift else SAFE_FLASH_ROWS
    nb = nb if batch % nb == 0 else 1
    nq = seq // TQ
    ngroups = NSA_HEADS // nheads
    nkv = NSA_KV_GROUPS
    da = k.shape[-1]
    has_delta = base is not None
    has_u = u is not None
    out_w = LANES * (nheads // 2)
    if has_delta and fixed_shift:
        tabs = _decayed_tile_tables(nq, batch, nb, base)
    else:
        tabs = _tile_tables(nq, None if window is None else window // TQ, batch // nb)
    nsteps = int(tabs[0].shape[0]) // (batch // nb)

    by_q_heads = lambda b, g, s, qt, kt, ft: (b, g, qt[b * nsteps + s], 0)
    by_k_heads = lambda b, g, s, qt, kt, ft: (b, g, kt[b * nsteps + s], 0)
    by_q_tokens = lambda col: (lambda b, g, s, qt, kt, ft: (b, qt[b * nsteps + s], g if col is None else col))
    by_k_tokens = lambda b, g, s, qt, kt, ft: (b, kt[b * nsteps + s], g + v_col)

    in_specs, args = [], []
    if has_delta:
        in_specs.append(pl.BlockSpec(memory_space=pltpu.SMEM))
        args.append(base)
    in_specs.append(pl.BlockSpec((nb, nheads, TQ, LANES), by_q_heads))
    args.append(q)
    if has_delta:
        in_specs.append(pl.BlockSpec((nb, nheads, TQ, LANES), by_q_heads))
        args.append(q_off)
    if has_u:
        in_specs.append(pl.BlockSpec((nb, nkv, TQ, LANES), by_q_heads))
        args.append(u)
    if shared_kv:
        in_specs.append(pl.BlockSpec((nb, nkv, TQ, da), by_k_heads))
        in_specs.append(pl.BlockSpec((nb, nkv, TQ, LANES), by_k_heads))
    else:
        in_specs.append(pl.BlockSpec((nb, nheads, TQ, da), by_k_heads))
        in_specs.append(pl.BlockSpec((nb, TQ, out_w), by_k_tokens))
    args += [k, v]
    if gate_branch is not None:
        in_specs.append(pl.BlockSpec((nb, TQ, LANES), by_q_tokens(C_SMALL // LANES)))
        args.append(z)

    m_rows = 8 if fixed_shift else TQ
    scratch = [pltpu.VMEM((nb * nheads, m_rows, LANES), F32),
               pltpu.VMEM((nb * nheads, TQ, LANES), F32),
               pltpu.VMEM((nb * nheads // 2, TQ, LANES), F32)]
    if has_u:
        scratch.append(pltpu.VMEM((nb * nheads, TQ, 2 * LANES), BF16))

    kern = functools.partial(_flash_kernel, nheads=nheads, shared_kv=shared_kv, window=window,
                             has_delta=has_delta, has_u=has_u, gate_branch=gate_branch, nblk=nq,
                             fixed_shift=fixed_shift, nb=nb)
    return pl.pallas_call(
        kern,
        grid_spec=pltpu.PrefetchScalarGridSpec(
            num_scalar_prefetch=3,
            grid=(batch // nb, ngroups, nsteps),
            in_specs=in_specs,
            out_specs=pl.BlockSpec((nb, TQ, out_w), by_q_tokens(None)),
            scratch_shapes=scratch),
        out_shape=jax.ShapeDtypeStruct((batch, seq, 4 * LANES), BF16),
        compiler_params=_cparams(("parallel", "parallel", "arbitrary")),
        name="flash_" + ("fox" if has_delta else ("sel" if has_u else "win")),
    )(*tabs, *args)


def _cmp_kernel(q_ref, kc_ref, vc_ref, ov_ref, gate_ref, o_ref, u_ref):
    g = pl.program_id(1)
    qi = pl.program_id(2)
    tq = q_ref.shape[2]
    ncmp = kc_ref.shape[2]
    nslc = ov_ref.shape[1]

    t = qi * tq + lax.broadcasted_iota(jnp.int32, (tq, ncmp), 0)
    n = lax.broadcasted_iota(jnp.int32, (tq, ncmp), 1)
    valid = (n * NSA_CMP_STRIDE + (NSA_CMP_LEN - 1)) <= t
    kc = kc_ref[0, 0]
    vc = vc_ref[0, 0]
    ov = ov_ref[...]
    lane = lax.broadcasted_iota(jnp.int32, (tq, LANES), 1)
    lo = lane < HEAD_DIM
    gate_blk = gate_ref[0].astype(F32)

    imp = jnp.zeros((tq, nslc), F32)
    outs = []
    for r in range(NSA_GROUP_HEADS):
        s = lax.dot_general(q_ref[0, r], kc, (((1,), (1,)), ((), ())), preferred_element_type=F32)
        s = jnp.where(valid, s, NEG)
        m = jnp.max(s, axis=1, keepdims=True)
        e = jnp.where(valid, jnp.exp2(s - m), 0.0)
        d = jnp.sum(e, axis=1, keepdims=True)
        pb = (e / jnp.where(d > 0.0, d, 1.0)).astype(BF16)
        o = jnp.dot(pb, vc, preferred_element_type=F32)
        gcol = _gate_column(gate_blk, lane, 3 * (g * NSA_GROUP_HEADS + r))
        outs.append(o * gcol)
        imp = imp + jnp.dot(pb, ov, preferred_element_type=F32)
    for p in range(NSA_GROUP_HEADS // 2):
        o_ref[0, :, p * LANES:(p + 1) * LANES] = jnp.where(lo, outs[2 * p], outs[2 * p + 1]).astype(o_ref.dtype)
    _write_unselected(u_ref, imp, qi, tq, nslc)


def _cmp_fixed_kernel(q_ref, kc_ref, vc_ref, ov_ref, gate_ref, o_ref, u_ref, d_scr, acc_scr, imp_scr,
                      *, chunk):
    g = pl.program_id(1)
    qi = pl.program_id(2)
    tq = q_ref.shape[2]
    ncmp = kc_ref.shape[2]
    nslc = ov_ref.shape[1]
    t0 = qi * tq
    d_scr[...] = jnp.zeros(d_scr.shape, F32)
    acc_scr[...] = jnp.zeros(acc_scr.shape, F32)
    imp_scr[...] = jnp.zeros(imp_scr.shape, F32)

    for c in range(ncmp // chunk):
        first_end = c * chunk * NSA_CMP_STRIDE + NSA_CMP_LEN - 1
        last_end = ((c + 1) * chunk - 1) * NSA_CMP_STRIDE + NSA_CMP_LEN - 1

        def body(masked, c=c):
            rows = slice(c * chunk, (c + 1) * chunk)
            kc = kc_ref[0, 0, rows, :]
            vo = jnp.concatenate([vc_ref[0, 0, rows, :], ov_ref[rows, :]], axis=1)
            if masked:
                t = t0 + lax.broadcasted_iota(jnp.int32, (tq, chunk), 0)
                n = c * chunk + lax.broadcasted_iota(jnp.int32, (tq, chunk), 1)
                valid = (n * NSA_CMP_STRIDE + (NSA_CMP_LEN - 1)) <= t
            for r in range(NSA_GROUP_HEADS):
                s = lax.dot_general(q_ref[0, r], kc, (((1,), (1,)), ((), ())), preferred_element_type=F32)
                if masked:
                    s = jnp.where(valid, s, NEG)
                e = jnp.exp2(s)
                part = e[:, :LANES]
                for j in range(1, chunk // LANES):
                    part = part + e[:, j * LANES:(j + 1) * LANES]
                d_scr[r] = d_scr[r] + part
                both = jnp.dot(e.astype(BF16), vo, preferred_element_type=F32)
                acc_scr[r] = acc_scr[r] + both[:, :LANES]
                imp_scr[r] = imp_scr[r] + both[:, LANES:]

        pl.when((first_end <= t0 + (tq - 1)) & (last_end > t0))(functools.partial(body, True))
        pl.when(last_end <= t0)(functools.partial(body, False))

    lane = lax.broadcasted_iota(jnp.int32, (tq, LANES), 1)
    lo = lane < HEAD_DIM
    gate_blk = gate_ref[0].astype(F32)
    imp = jnp.zeros((tq, nslc), F32)
    outs = []
    for r in range(NSA_GROUP_HEADS):
        d = jnp.sum(d_scr[r], axis=1, keepdims=True)
        inv = 1.0 / jnp.where(d > 0.0, d, 1.0)
        gcol = _gate_column(gate_blk, lane, 3 * (g * NSA_GROUP_HEADS + r))
        outs.append(acc_scr[r] * (inv * gcol))
        imp = imp + imp_scr[r] * inv
    for p in range(NSA_GROUP_HEADS // 2):
        o_ref[0, :, p * LANES:(p + 1) * LANES] = jnp.where(lo, outs[2 * p], outs[2 * p + 1]).astype(o_ref.dtype)
    _write_unselected(u_ref, imp, qi, tq, nslc)


N_FORCED = 3


def _write_unselected(u_ref, imp, qi, tq, nslc):
    tpos = qi * tq + lax.broadcasted_iota(jnp.int32, (tq, nslc), 0)
    jblk = lax.broadcasted_iota(jnp.int32, (tq, nslc), 1)
    cur = jnp.right_shift(tpos, SEL_SHIFT)
    forced = (jblk == 0) | (jblk == cur) | (jblk == cur - 1)
    visible = jblk <= cur
    score_t = jnp.where(visible & jnp.logical_not(forced), imp, -jnp.inf).T

    def run(nrows):
        sc = score_t[:nrows]
        jt = lax.broadcasted_iota(jnp.int32, (nrows, tq), 0).astype(F32)
        for _ in range(NSA_SEL_TOPK - N_FORCED):
            mx = jnp.max(sc, axis=0, keepdims=True)
            first = jnp.min(jnp.where(sc == mx, jt, float(nslc)), axis=0, keepdims=True)
            sc = jnp.where(jt == first, -jnp.inf, sc)
        taken = jnp.where(sc == -jnp.inf, 1.0, 0.0)
        if nrows < nslc:
            taken = jnp.concatenate([taken, jnp.zeros((nslc - nrows, tq), F32)], axis=0)
        chosen = visible & (forced | (taken.T > 0.5))
        u_ref[0, 0] = jnp.where(chosen, 0.0, 1.0).astype(BF16)

    half = nslc // 2
    last_block = jnp.right_shift(qi * tq + (tq - 1), SEL_SHIFT)
    pl.when(last_block < half)(functools.partial(run, half))
    pl.when(last_block >= half)(functools.partial(run, nslc))


def _cmp_select(qn, kc, vc, ov, z, *, batch, seq, fixed_shift):
    ncmp = kc.shape[2]
    nslc = ov.shape[1]
    if fixed_shift:
        chunk = 2 * LANES if ncmp % (2 * LANES) == 0 else ncmp
        kern = functools.partial(_cmp_fixed_kernel, chunk=chunk)
        scratch = [pltpu.VMEM((NSA_GROUP_HEADS, CMP_TQ, LANES), F32) for _ in range(3)]
    else:
        kern, scratch = _cmp_kernel, []
    return pl.pallas_call(
        kern,
        scratch_shapes=scratch,
        grid=(batch, NSA_KV_GROUPS, seq // CMP_TQ),
        in_specs=[pl.BlockSpec((1, NSA_GROUP_HEADS, CMP_TQ, LANES), lambda b, g, i: (b, g, i, 0)),
                  pl.BlockSpec((1, 1, ncmp, LANES), lambda b, g, i: (b, g, 0, 0)),
                  pl.BlockSpec((1, 1, ncmp, LANES), lambda b, g, i: (b, g, 0, 0)),
                  pl.BlockSpec((ncmp, nslc), lambda b, g, i: (0, 0)),
                  pl.BlockSpec((1, CMP_TQ, LANES), lambda b, g, i: (b, i, C_SMALL // LANES))],
        out_specs=[pl.BlockSpec((1, CMP_TQ, 2 * LANES), lambda b, g, i: (b, i, g)),
                   pl.BlockSpec((1, 1, CMP_TQ, nslc), lambda b, g, i: (b, g, i, 0))],
        out_shape=[jax.ShapeDtypeStruct((batch, seq, NSA_WIDTH), BF16),
                   jax.ShapeDtypeStruct((batch, NSA_KV_GROUPS, seq, nslc), BF16)],
        compiler_params=_cparams(("parallel", "parallel", "parallel")),
        name="cmp_select",
    )(qn, kc, vc, ov, z)


def _head_rms(x, g):
    return x * lax.rsqrt(jnp.mean(x * x, axis=-1, keepdims=True) + EPS) * g


def _mem_kernel(zq_ref, kv_ref, qg_ref, kg_ref, o_ref, kn_scr, vb_scr):
    @pl.when(pl.program_id(1) == 0)
    def _prep():
        for h in range(MEM_HEADS):
            kh = kv_ref[0, :, h * MEM_HEAD_DIM:(h + 1) * MEM_HEAD_DIM]
            kn_scr[h] = _head_rms(kh, kg_ref[...]).astype(BF16)
            vb_scr[h] = kv_ref[0, :, MEM_WIDTH + h * MEM_HEAD_DIM:
                               MEM_WIDTH + (h + 1) * MEM_HEAD_DIM].astype(BF16)

    for h in range(MEM_HEADS):
        sl = slice(h * MEM_HEAD_DIM, (h + 1) * MEM_HEAD_DIM)
        qh = (_head_rms(zq_ref[0, :, sl].astype(F32), qg_ref[...])
              * ((MEM_HEAD_DIM ** -0.5) * LOG2E)).astype(BF16)
        s = lax.dot_general(qh, kn_scr[h], (((1,), (1,)), ((), ())), preferred_element_type=F32)
        m = jnp.max(s, axis=1, keepdims=True)
        e = jnp.exp2(s - m)
        p = (e / jnp.sum(e, axis=1, keepdims=True)).astype(BF16)
        o_ref[0, :, sl] = jnp.dot(p, vb_scr[h], preferred_element_type=F32).astype(o_ref.dtype)


def _mem_attention(z, mem_kv, q_g, k_g, *, batch, seq):
    mlen = mem_kv.shape[1]
    tq = TQ
    return pl.pallas_call(
        _mem_kernel,
        grid=(batch, seq // tq),
        in_specs=[pl.BlockSpec((1, tq, MEM_WIDTH), lambda b, i: (b, i, C_MEM_Q // MEM_WIDTH)),
                  pl.BlockSpec((1, mlen, 2 * MEM_WIDTH), lambda b, i: (b, 0, 0)),
                  pl.BlockSpec((1, MEM_HEAD_DIM), lambda b, i: (0, 0)),
                  pl.BlockSpec((1, MEM_HEAD_DIM), lambda b, i: (0, 0))],
        out_specs=pl.BlockSpec((1, tq, MEM_WIDTH), lambda b, i: (b, i, 0)),
        out_shape=jax.ShapeDtypeStruct((batch, seq, MEM_WIDTH), BF16),
        scratch_shapes=[pltpu.VMEM((MEM_HEADS, mlen, MEM_HEAD_DIM), BF16),
                        pltpu.VMEM((MEM_HEADS, mlen, MEM_HEAD_DIM), BF16)],
        compiler_params=_cparams(("parallel", "arbitrary")),
        name="mem_attention",
    )(z, mem_kv, q_g.reshape(1, MEM_HEAD_DIM), k_g.reshape(1, MEM_HEAD_DIM))


def _out_kernel(x_ref, oc_ref, os_ref, ow_ref, ob_ref, om_ref, sa_ref, sb_ref, sm_ref,
                g0_ref, g1_ref, g2_ref, wa_ref, wb_ref, wm_ref, wo_ref, y_ref):
    oa = ((oc_ref[...].astype(F32) + os_ref[...].astype(F32) + ow_ref[...].astype(F32))
          * jax.nn.silu(sa_ref[...].astype(F32)))
    ob = ob_ref[...].astype(F32) * jax.nn.silu(sb_ref[...].astype(F32))
    om = om_ref[...].astype(F32) * jax.nn.silu(sm_ref[...].astype(F32))
    gate = lambda ref: jax.nn.sigmoid(ref[...].astype(F32))
    u = (gate(g0_ref) * jnp.dot(oa.astype(BF16), wa_ref[...], preferred_element_type=F32)
         + gate(g1_ref) * jnp.dot(ob.astype(BF16), wb_ref[...], preferred_element_type=F32)
         + gate(g2_ref) * jnp.dot(om.astype(BF16), wm_ref[...], preferred_element_type=F32))
    y_ref[...] = x_ref[...] + jnp.dot(u.astype(BF16), wo_ref[...], preferred_element_type=F32)


def _out_proj(x2d, o_c, o_s, o_w, o_b, o_m, z2d, wa, wb, wm, wo):
    m = x2d.shape[0]
    tm = 256
    w512 = 512
    row512 = lambda c: pl.BlockSpec((tm, w512), lambda i, c=c: (i, c))
    row1024 = lambda c: pl.BlockSpec((tm, D_MODEL), lambda i, c=c: (i, c))
    full = lambda shape: pl.BlockSpec(shape, lambda i: (0, 0))
    return pl.pallas_call(
        _out_kernel,
        grid=(m // tm,),
        in_specs=[row1024(0), row512(0), row512(0), row512(0), row512(0), row512(0),
                  row512(C_NSA_SILU // w512), row512(C_FOX_SILU // w512), row512(C_MEM_SILU // w512),
                  row1024(C_MERGE // D_MODEL), row1024(C_MERGE // D_MODEL + 1),
                  row1024(C_MERGE // D_MODEL + 2),
                  full((w512, D_MODEL)), full((w512, D_MODEL)), full((w512, D_MODEL)),
                  full((D_MODEL, D_MODEL))],
        out_specs=row1024(0),
        out_shape=jax.ShapeDtypeStruct((m, D_MODEL), F32),
        compiler_params=_cparams(("parallel",)),
        name="out_proj",
    )(x2d, o_c, o_s, o_w, o_b, o_m, z2d, z2d, z2d, z2d, z2d, z2d, wa, wb, wm, wo)


def _permute_w_in(w_in):
    o = np.cumsum([0, NSA_WIDTH, 6 * NSA_KV_GROUPS * HEAD_DIM, 3 * NSA_HEADS, NSA_WIDTH,
                   3 * FOX_WIDTH, FOX_HEADS, FOX_WIDTH, MEM_WIDTH, MEM_WIDTH, 3 * D_MODEL])
    nsa_q, nsa_kv, nsa_gate, nsa_silu, fox_qkv, fox_f, fox_silu, mem_q, mem_silu, merge = [
        w_in[..., o[i]:o[i + 1]] for i in range(10)]
    pad = jnp.zeros(w_in.shape[:2] + (LANES - 3 * NSA_HEADS - FOX_HEADS,), w_in.dtype)
    return jnp.concatenate([nsa_q, nsa_silu, fox_qkv, fox_silu, mem_q, mem_silu, merge, nsa_kv,
                            nsa_gate, fox_f, pad], axis=-1).astype(BF16)


def _pad_lanes(x, width=LANES):
    return jnp.pad(x, [(0, 0)] * (x.ndim - 1) + [(0, width - x.shape[-1])])


def _tile2(g):
    return jnp.concatenate([g, g], axis=-1)[:, None, :]


def _overlap(n_cmp_pad, n_slc):
    cs = np.arange(n_cmp_pad)[:, None] * NSA_CMP_STRIDE
    ss = np.arange(NSLC_PAD)[None, :] * NSA_SEL_LEN
    ov = np.clip(np.minimum(cs + NSA_CMP_LEN, ss + NSA_SEL_LEN) - np.maximum(cs, ss), 0, None)
    ov = ov / NSA_CMP_LEN
    ov[-1] = 0.0
    ov[:, n_slc:] = 0.0
    return jnp.asarray(ov, BF16)


def _aug_permutations():
    pq = np.zeros((6 * LANES, FOX_HEADS * LANES), np.float32)
    pk = np.zeros((6 * LANES, FOX_HEADS * LANES), np.float32)
    for h in range(FOX_HEADS):
        for i in range(3):
            pq[i * LANES + F_LANE + h, h * LANES + AUG + i] = 1.0
            pk[i * LANES + F_LANE + h, h * LANES + AUG + 3 + i] = -1.0
            pk[(3 + i) * LANES + F_LANE + h, h * LANES + AUG + 6 + i] = -1.0
    return jnp.asarray(pq, BF16), jnp.asarray(pk, BF16)


def _compress_w1(w1):
    nl = w1.shape[0]
    halves = w1.reshape(nl, 2, NSA_CMP_STRIDE, 1, HEAD_DIM, NSA_CMP_HIDDEN)
    eye = jnp.eye(NSA_KV_GROUPS, dtype=w1.dtype)
    out = jnp.einsum("pg,zhldn->zlpdghn", eye, halves[:, :, :, 0])
    return out.reshape(nl, NSA_CMP_STRIDE * NSA_KV_GROUPS * HEAD_DIM,
                       NSA_KV_GROUPS * 2 * NSA_CMP_HIDDEN).astype(BF16)


def _layer(x, mem, p, consts):
    batch, seq, _ = x.shape
    m_rows = batch * seq
    cos4, sin_signed, tri, head_ones, perm_q, perm_k, ov = consts
    z2d = _norm_matmul(x.reshape(m_rows, D_MODEL), p["norm_g"], p["w_in"], 512, Z_WIDTH // 3, BF16)
    z = z2d.reshape(batch, seq, Z_WIDTH)

    (qn, kcr, vcr, ks, vs, kw, vw, fqd, fqo, fka, base) = _prep(
        z, cos4, sin_signed, tri, head_ones, perm_q, perm_k,
        p["gq"], p["gks"], p["gkw"], p["gfq"], p["gfk"], p["fb"],
        p["s_sel"], p["s_win"], p["s_fox"], batch=batch, seq=seq)

    nchunk = seq // NSA_CMP_STRIDE
    kc, vc = _compress(kcr.reshape(batch, nchunk, NSA_CMP_STRIDE * LANES),
                       vcr.reshape(batch, nchunk, NSA_CMP_STRIDE * LANES),
                       p["w1k"], p["w1v"], p["bk"], p["bv"], p["w2k"], p["w2v"], p["gkc"], p["s_cmp"],
                       batch=batch)

    def attend(fixed_shift):
        o_c, unsel = _cmp_select(qn, kc, vc, ov, z, batch=batch, seq=seq, fixed_shift=fixed_shift)
        o_s = _flash(qn, ks, vs, batch=batch, seq=seq, shared_kv=True, fixed_shift=fixed_shift,
                     u=unsel, z=z, gate_branch=1)
        o_w = _flash(qn, kw, vw, batch=batch, seq=seq, shared_kv=True, fixed_shift=fixed_shift,
                     window=NSA_WINDOW, z=z, gate_branch=2)
        o_b = _flash(fqd, fka, z, batch=batch, seq=seq, shared_kv=False, fixed_shift=fixed_shift,
                     base=base.reshape(-1), q_off=fqo, v_col=C_FOX_V // FOX_WIDTH)
        return o_c, o_s, o_w, o_b

    o_c, o_s, o_w, o_b = lax.cond(p["bound_ok"], lambda: attend(True), lambda: attend(False))

    mlen = mem.shape[1]
    mem_kv = _norm_matmul(mem.reshape(batch * mlen, D_MODEL), p["mem_norm_g"], p["w_mem_kv"],
                          min(512, batch * mlen), 2 * MEM_WIDTH, F32).reshape(batch, mlen, 2 * MEM_WIDTH)
    o_m = _mem_attention(z, mem_kv, p["mem_q_norm"], p["mem_k_norm"], batch=batch, seq=seq)

    y = _out_proj(x.reshape(m_rows, D_MODEL), o_c.reshape(m_rows, -1), o_s.reshape(m_rows, -1),
                  o_w.reshape(m_rows, -1), o_b.reshape(m_rows, -1), o_m.reshape(m_rows, -1), z2d,
                  p["w_branch_a"], p["w_branch_b"], p["w_branch_m"], p["w_out"])
    return y.reshape(batch, seq, D_MODEL)


def kernel(x, mem, norm_g, mem_norm_g, w_in, nsa_q_norm, nsa_k_norm, cmp_pe_k, cmp_w1_k, cmp_w2_k,
           cmp_pe_v, cmp_w1_v, cmp_w2_v, fox_q_norm, fox_k_norm, fox_f_bias, mem_q_norm, mem_k_norm,
           w_mem_kv, w_branch_a, w_branch_b, w_branch_m, w_out):
    batch, seq, _ = x.shape
    depth = w_in.shape[0]
    n_slc = seq // NSA_SEL_LEN
    assert n_slc <= NSLC_PAD and seq % TQ == 0 and NSA_WINDOW == TQ

    half = HEAD_DIM // 2
    inv_freq = ROPE_THETA ** (-jnp.arange(half, dtype=F32) / half)
    ang = jnp.arange(seq).astype(F32)[:, None] * inv_freq[None, :]
    cos, sin = jnp.cos(ang), jnp.sin(ang)
    cos4 = jnp.concatenate([cos, cos, cos, cos], axis=-1)
    sin_signed = jnp.concatenate([-sin, sin, -sin, sin], axis=-1)
    tri = jnp.asarray(np.tril(np.ones((TQ, TQ), np.float32)), BF16)
    head_ones = jnp.asarray(np.kron(np.eye(2), np.ones((HEAD_DIM, HEAD_DIM))), BF16)
    consts = (cos4, sin_signed, tri, head_ones, *_aug_permutations(),
              _overlap(seq // NSA_CMP_STRIDE, n_slc))

    hp = lax.Precision.HIGHEST
    pe_bias = lambda pe, w1: jnp.einsum("lk,lkn->ln", pe.reshape(depth, -1), w1, precision=hp)[:, None, :]
    fb = jnp.zeros((depth, 1, LANES), F32).at[:, 0, F_LANE:F_LANE + FOX_HEADS].set(fox_f_bias)

    def logit_bound(gq, gk):
        bound = (BOUND_MARGIN * HEAD_DIM * QK_SCALE
                 * jnp.max(jnp.abs(gq), axis=-1) * jnp.max(jnp.abs(gk), axis=-1))
        return bound.astype(BF16).astype(F32)

    bounds = [logit_bound(nsa_q_norm, nsa_k_norm[:, 1]), logit_bound(nsa_q_norm, nsa_k_norm[:, 2]),
              logit_bound(fox_q_norm, fox_k_norm), logit_bound(nsa_q_norm, nsa_k_norm[:, 0])]
    bound_ok = functools.reduce(jnp.maximum, bounds) <= MAX_FIXED_BOUND
    bound_row = lambda s: jnp.broadcast_to(s[:, None, None], (depth, 1, LANES))
    stacked = dict(
        s_sel=bound_row(bounds[0]), s_win=bound_row(bounds[1]), s_fox=bound_row(bounds[2]),
        s_cmp=bound_row(bounds[3]),
        bound_ok=bound_ok,
        norm_g=norm_g, mem_norm_g=mem_norm_g, w_in=_permute_w_in(w_in),
        gq=_tile2(nsa_q_norm), gks=_tile2(nsa_k_norm[:, 1]), gkw=_tile2(nsa_k_norm[:, 2]),
        gkc=_pad_lanes(nsa_k_norm[:, 0])[:, None, :], gfq=_tile2(fox_q_norm), gfk=_tile2(fox_k_norm),
        fb=fb, w1k=_compress_w1(cmp_w1_k), w1v=_compress_w1(cmp_w1_v),
        bk=pe_bias(cmp_pe_k, cmp_w1_k), bv=pe_bias(cmp_pe_v, cmp_w1_v),
        w2k=_pad_lanes(cmp_w2_k).astype(BF16),
        w2v=jnp.concatenate([cmp_w2_v, cmp_w2_v], axis=-1).astype(BF16),
        mem_q_norm=mem_q_norm, mem_k_norm=mem_k_norm, w_mem_kv=w_mem_kv.astype(BF16),
        w_branch_a=w_branch_a.astype(BF16), w_branch_b=w_branch_b.astype(BF16),
        w_branch_m=w_branch_m.astype(BF16), w_out=w_out.astype(BF16))
    for l in range(depth):
        x = _layer(x, mem, {k: v[l] for k, v in stacked.items()}, consts)
    return x
```

```python
import functools

import numpy as np
import jax
import jax.numpy as jnp
from jax import lax
from jax.experimental import pallas as pl
from jax.experimental.pallas import tpu as pltpu

F32 = jnp.float32
BF16 = jnp.bfloat16

D_MODEL = 1024
HEAD_DIM = 64
ROPE_THETA = 10000.0
EPS = 1e-6
NSA_HEADS = 8
NSA_KV_GROUPS = 2
NSA_GROUP_HEADS = NSA_HEADS // NSA_KV_GROUPS
NSA_CMP_LEN = 32
NSA_CMP_STRIDE = 16
NSA_CMP_HIDDEN = 128
NSA_SEL_LEN = 64
NSA_SEL_TOPK = 16
NSA_WINDOW = 512
FORCE_SCORE = 1e9
NSA_WIDTH = NSA_HEADS * HEAD_DIM
FOX_HEADS = 8
FOX_WIDTH = FOX_HEADS * HEAD_DIM
MEM_HEADS = 4
MEM_HEAD_DIM = 128
MEM_WIDTH = MEM_HEADS * MEM_HEAD_DIM

LANES = 128
VMEM_LIMIT = 56 * 1024 * 1024
NEG = -1e30
SEL_MASK = 32768.0
SEL_SHIFT = 6
NSLC_PAD = LANES
LOG2E = 1.4426950408889634
QK_SCALE = (HEAD_DIM ** -0.5) * LOG2E
BOUND_MARGIN = 1.02
MAX_FIXED_BOUND = 60.0
ZERO_WEIGHT_LOG2 = -160.0

C_NSA_Q = 0
C_NSA_SILU = 512
C_FOX_Q = 1024
C_FOX_K = 1536
C_FOX_V = 2048
C_FOX_SILU = 2560
C_MEM_Q = 3072
C_MEM_SILU = 3584
C_MERGE = 4096
C_NSA_KV = 7168
C_SMALL = 7936
Z_WIDTH = 8064
F_LANE = 3 * NSA_HEADS
AUG = HEAD_DIM

TQ = 512
FLASH_ROWS = 4
SAFE_FLASH_ROWS = 2
CMP_TQ = 512


def _cparams(sem):
    return pltpu.CompilerParams(dimension_semantics=sem, vmem_limit_bytes=VMEM_LIMIT)


def _norm_matmul_kernel(x_ref, g_ref, w_ref, o_ref, *, tn):
    x = x_ref[...]
    ms = jnp.mean(x * x, axis=-1, keepdims=True)
    h = (x * lax.rsqrt(ms + EPS) * g_ref[...]).astype(BF16)
    for c0 in range(0, w_ref.shape[1], tn):
        o_ref[:, c0:c0 + tn] = jnp.dot(h, w_ref[:, c0:c0 + tn],
                                       preferred_element_type=F32).astype(o_ref.dtype)


def _norm_matmul(x2d, g, w, tm, tn, out_dtype):
    m, k = x2d.shape
    n = w.shape[1]
    return pl.pallas_call(
        functools.partial(_norm_matmul_kernel, tn=tn),
        grid=(m // tm,),
        in_specs=[pl.BlockSpec((tm, k), lambda i: (i, 0)),
                  pl.BlockSpec((1, k), lambda i: (0, 0)),
                  pl.BlockSpec((k, n), lambda i: (0, 0), pipeline_mode=pl.Buffered(1))],
        out_specs=pl.BlockSpec((tm, n), lambda i: (i, 0)),
        out_shape=jax.ShapeDtypeStruct((m, n), out_dtype),
        compiler_params=_cparams(("parallel",)),
        name="norm_matmul",
    )(x2d, g.reshape(1, k), w)


def _pair_rms(x, g2, head_ones):
    sq = x * x
    hi = sq.astype(BF16)
    low = (sq - hi.astype(F32)).astype(BF16)
    ssq = (jnp.dot(hi, head_ones, preferred_element_type=F32)
           + jnp.dot(low, head_ones, preferred_element_type=F32))
    return x * lax.rsqrt(ssq * (1.0 / HEAD_DIM) + EPS) * g2


def _pair_rope(y, cos4, sin_signed, first_half):
    rot = jnp.where(first_half, pltpu.roll(y, LANES - HEAD_DIM // 2, 1), pltpu.roll(y, HEAD_DIM // 2, 1))
    return y * cos4 + rot * sin_signed


def _head_a(y, lo, tail=0.0):
    return jnp.where(lo, y, tail)


def _head_b(y, lo, tail=0.0):
    return jnp.where(lo, pltpu.roll(y, HEAD_DIM, 1), tail)


def _split3_f32(c):
    hi = c.astype(BF16).astype(F32)
    r = c - hi
    mid = r.astype(BF16).astype(F32)
    return hi, mid, r - mid


def _prep_kernel(zq_ref, zc_ref, zs_ref, zw_ref, fq_ref, fk_ref, sm_ref, cos_ref, sin_ref,
                 tri_ref, ones_ref, pq_ref, pk_ref, gq_ref, gks_ref, gkw_ref, gfq_ref, gfk_ref, fb_ref,
                 ss_ref, sw_ref, sf_ref,
                 qn_ref, kcr_ref, vcr_ref, ks_ref, vs_ref, kw_ref, vw_ref, fqd_ref, fqo_ref, fka_ref,
                 base_ref, run_scr):
    ti = pl.program_id(1)
    tt = zq_ref.shape[1]
    lane = lax.broadcasted_iota(jnp.int32, (tt, LANES), 1)
    lo = lane < HEAD_DIM
    first_half = (lane & (HEAD_DIM // 2)) == 0
    cos4 = cos_ref[...]
    sin_signed = sin_ref[...]
    head_ones = ones_ref[...]
    one_at_aug = jnp.where(lane == AUG, 1.0, 0.0)

    for p in range(NSA_HEADS // 2):
        y = _pair_rope(_pair_rms(zq_ref[0, :, p * LANES:(p + 1) * LANES].astype(F32), gq_ref[...], head_ones),
                       cos4, sin_signed, first_half) * QK_SCALE
        qn_ref[0, 2 * p] = _head_a(y, lo, one_at_aug).astype(BF16)
        qn_ref[0, 2 * p + 1] = _head_b(y, lo, one_at_aug).astype(BF16)

    kcr_ref[0] = _pair_rope(zc_ref[0, :, :LANES].astype(F32), cos4, sin_signed, first_half).astype(BF16)
    vcr_ref[0] = zc_ref[0, :, LANES:].astype(BF16)

    sel_lane = jnp.right_shift(ti * tt + lax.broadcasted_iota(jnp.int32, (tt, LANES), 0), SEL_SHIFT)
    onehot = jnp.where(lane == sel_lane, -SEL_MASK, 0.0).astype(BF16)
    for z_ref, g_ref, k_ref, v_ref, bound_ref, with_onehot in (
            (zs_ref, gks_ref, ks_ref, vs_ref, ss_ref, True),
            (zw_ref, gkw_ref, kw_ref, vw_ref, sw_ref, False)):
        y = _pair_rope(_pair_rms(z_ref[0, :, :LANES].astype(F32), g_ref[...], head_ones),
                       cos4, sin_signed, first_half)
        v = z_ref[0, :, LANES:].astype(F32)
        vr = pltpu.roll(v, HEAD_DIM, 1)
        neg_bound = jnp.where(lane == AUG, -bound_ref[...], 0.0)
        k_ref[0, 0, :, :LANES] = _head_a(y, lo, neg_bound).astype(BF16)
        k_ref[0, 1, :, :LANES] = _head_b(y, lo, neg_bound).astype(BF16)
        if with_onehot:
            k_ref[0, 0, :, LANES:] = onehot
            k_ref[0, 1, :, LANES:] = onehot
        v_ref[0, 0] = jnp.where(lo, v, vr).astype(BF16)
        v_ref[0, 1] = jnp.where(lo, vr, v).astype(BF16)

    @pl.when(ti == 0)
    def _reset():
        run_scr[...] = jnp.zeros(run_scr.shape, F32)

    xf = sm_ref[0].astype(F32) + fb_ref[...]
    log_f = (jnp.minimum(xf, 0.0) - jnp.log1p(jnp.exp(-jnp.abs(xf)))) * LOG2E
    tri = tri_ref[...]
    parts = _split3_f32(log_f)
    cum = sum(jnp.dot(tri, part.astype(BF16), preferred_element_type=F32) for part in parts)
    base_ref[0, 0] = run_scr[...]
    run_scr[...] = run_scr[...] + cum[tt - 1:tt, :]

    parts6 = _split3_f32(cum) + _split3_f32(cum - cum[tt - 1:tt, :])
    x6 = jnp.concatenate([part.astype(BF16) for part in parts6], axis=1)
    aug_q = jnp.dot(x6, pq_ref[...], preferred_element_type=F32)
    aug_k = jnp.dot(x6, pk_ref[...], preferred_element_type=F32)

    lane1 = lax.broadcasted_iota(jnp.int32, (1, LANES), 1)
    span = lambda a, b: jnp.where((lane1 >= a) & (lane1 < b), 1.0, 0.0)
    q_tail_diag = span(AUG + 3, AUG + 6) + span(AUG + 9, AUG + 10)
    q_tail_off = span(AUG + 6, AUG + 10)
    k_tail = span(AUG, AUG + 3) + jnp.where(lane1 == AUG + 9, -sf_ref[...], 0.0)
    for p in range(FOX_HEADS // 2):
        sl = slice(p * LANES, (p + 1) * LANES)
        yq = _pair_rms(fq_ref[0, :, sl].astype(F32), gfq_ref[...], head_ones) * QK_SCALE
        yk = _pair_rms(fk_ref[0, :, sl].astype(F32), gfk_ref[...], head_ones)
        for e, split in ((0, _head_a), (1, _head_b)):
            h = 2 * p + e
            hs = slice(h * LANES, (h + 1) * LANES)
            qh = split(yq, lo) + aug_q[:, hs]
            fqd_ref[0, h] = (qh + q_tail_diag).astype(BF16)
            fqo_ref[0, h] = (qh + q_tail_off).astype(BF16)
            fka_ref[0, h] = (split(yk, lo) + aug_k[:, hs] + k_tail).astype(BF16)


def _prep(z, cos4, sin_signed, tri, head_ones, perm_q, perm_k, gq, gks, gkw, gfq, gfk, fb,
          s_sel, s_win, s_fox, *, batch, seq):
    nblk = seq // TQ
    whole = lambda a: pl.BlockSpec(a.shape, lambda b, i: (0, 0))
    zspec = lambda w, c: pl.BlockSpec((1, TQ, w), lambda b, i, c=c: (b, i, c))
    row = pl.BlockSpec((1, LANES), lambda b, i: (0, 0))
    tab = pl.BlockSpec((TQ, LANES), lambda b, i: (i, 0))
    heads = lambda n, w: pl.BlockSpec((1, n, TQ, w), lambda b, i: (b, 0, i, 0))
    tok = lambda w: pl.BlockSpec((1, TQ, w), lambda b, i: (b, i, 0))
    bf = lambda *shape: jax.ShapeDtypeStruct(shape, BF16)
    return pl.pallas_call(
        _prep_kernel,
        grid=(batch, nblk),
        in_specs=[zspec(NSA_WIDTH, C_NSA_Q // NSA_WIDTH),
                  zspec(2 * LANES, C_NSA_KV // (2 * LANES)),
                  zspec(2 * LANES, C_NSA_KV // (2 * LANES) + 1),
                  zspec(2 * LANES, C_NSA_KV // (2 * LANES) + 2),
                  zspec(FOX_WIDTH, C_FOX_Q // FOX_WIDTH),
                  zspec(FOX_WIDTH, C_FOX_K // FOX_WIDTH),
                  zspec(LANES, C_SMALL // LANES),
                  tab, tab, whole(tri), whole(head_ones), whole(perm_q), whole(perm_k),
                  row, row, row, row, row, row, row, row, row],
        out_specs=[heads(NSA_HEADS, LANES), tok(LANES), tok(LANES),
                   heads(NSA_KV_GROUPS, 2 * LANES), heads(NSA_KV_GROUPS, LANES),
                   heads(NSA_KV_GROUPS, LANES), heads(NSA_KV_GROUPS, LANES),
                   heads(FOX_HEADS, LANES), heads(FOX_HEADS, LANES), heads(FOX_HEADS, LANES),
                   pl.BlockSpec((1, 1, 1, LANES), lambda b, i: (b, i, 0, 0))],
        out_shape=[bf(batch, NSA_HEADS, seq, LANES), bf(batch, seq, LANES), bf(batch, seq, LANES),
                   bf(batch, NSA_KV_GROUPS, seq, 2 * LANES), bf(batch, NSA_KV_GROUPS, seq, LANES),
                   bf(batch, NSA_KV_GROUPS, seq, LANES), bf(batch, NSA_KV_GROUPS, seq, LANES),
                   bf(batch, FOX_HEADS, seq, LANES), bf(batch, FOX_HEADS, seq, LANES),
                   bf(batch, FOX_HEADS, seq, LANES),
                   jax.ShapeDtypeStruct((batch, nblk, 1, LANES), F32)],
        scratch_shapes=[pltpu.VMEM((1, LANES), F32)],
        compiler_params=_cparams(("parallel", "arbitrary")),
        name="prep",
    )(z, z, z, z, z, z, z, cos4, sin_signed, tri, head_ones, perm_q, perm_k,
      gq, gks, gkw, gfq, gfk, fb, s_sel, s_win, s_fox)


def _compress_kernel(xk_ref, xv_ref, w1k_ref, w1v_ref, bk_ref, bv_ref, w2k_ref, w2v_ref, kg_ref,
                     sc_ref, kc_ref, vc_ref):
    nchunk = xk_ref.shape[1]
    lane = lax.broadcasted_iota(jnp.int32, (nchunk, LANES), 1)
    for x_ref, w1_ref, b_ref, w2_ref, o_ref, is_key in ((xk_ref, w1k_ref, bk_ref, w2k_ref, kc_ref, True),
                                                        (xv_ref, w1v_ref, bv_ref, w2v_ref, vc_ref, False)):
        h = jnp.dot(x_ref[0], w1_ref[...], preferred_element_type=F32)
        for g in range(NSA_KV_GROUPS):
            c0 = g * 2 * NSA_CMP_HIDDEN
            top = h[:, c0:c0 + NSA_CMP_HIDDEN]
            bot = h[:, c0 + NSA_CMP_HIDDEN:c0 + 2 * NSA_CMP_HIDDEN]
            hid = top + pltpu.roll(bot, nchunk - 1, 0) + b_ref[...]
            act = (hid * jax.nn.sigmoid(hid)).astype(BF16)
            o = jnp.dot(act, w2_ref[...], preferred_element_type=F32)
            if is_key:
                ms = jnp.sum(o * o, axis=1, keepdims=True) * (1.0 / HEAD_DIM)
                o = o * lax.rsqrt(ms + EPS) * kg_ref[...]
                o = jnp.where(lane == AUG, -sc_ref[...], o)
            o_ref[0, g] = o.astype(BF16)


def _compress(xk, xv, w1k, w1v, bk, bv, w2k, w2v, kg, s_cmp, *, batch):
    nchunk, kin = xk.shape[1], xk.shape[2]
    xspec = pl.BlockSpec((1, nchunk, kin), lambda b: (b, 0, 0))
    full = lambda a: pl.BlockSpec(a.shape, lambda b: (0,) * a.ndim)
    ospec = pl.BlockSpec((1, NSA_KV_GROUPS, nchunk, LANES), lambda b: (b, 0, 0, 0))
    oshape = jax.ShapeDtypeStruct((batch, NSA_KV_GROUPS, nchunk, LANES), BF16)
    return pl.pallas_call(
        _compress_kernel,
        grid=(batch,),
        in_specs=[xspec, xspec, full(w1k), full(w1v), full(bk), full(bv), full(w2k), full(w2v), full(kg),
                  full(s_cmp)],
        out_specs=[ospec, ospec],
        out_shape=[oshape, oshape],
        compiler_params=_cparams(("parallel",)),
        name="compress",
    )(xk, xv, w1k, w1v, bk, bv, w2k, w2v, kg, s_cmp)


def _gate_column(gate_blk, lane, idx):
    col = jnp.sum(jnp.where(lane == idx, gate_blk, 0.0), axis=1, keepdims=True)
    return jax.nn.sigmoid(col)


def _flash_kernel(qi_tab, ki_tab, fl_tab, *refs, nheads, shared_kv, window, has_delta,
                  has_u, gate_branch, nblk, fixed_shift, nb):
    refs = list(refs)
    base_ref = refs.pop(0) if has_delta else None
    q_ref = refs.pop(0)
    qoff_ref = refs.pop(0) if has_delta else None
    u_ref = refs.pop(0) if has_u else None
    k_ref = refs.pop(0)
    v_ref = refs.pop(0)
    gate_ref = refs.pop(0) if gate_branch is not None else None
    o_ref = refs.pop(0)
    m_scr = refs.pop(0)
    l_scr = refs.pop(0)
    acc_scr = refs.pop(0)
    qcat_scr = refs.pop(0) if has_u else None

    b = pl.program_id(0)
    g = pl.program_id(1)
    step = b * pl.num_programs(2) + pl.program_id(2)
    qi = qi_tab[step]
    ki = ki_tab[step]
    fl = fl_tab[step]
    active = (fl & 4) == 4
    tq = q_ref.shape[2]
    tk = k_ref.shape[2]

    @pl.when((fl & 1) == 1)
    def _init():
        m_scr[...] = jnp.full(m_scr.shape, NEG, F32)
        l_scr[...] = jnp.zeros(l_scr.shape, F32)
        acc_scr[...] = jnp.zeros(acc_scr.shape, F32)
        if has_u:
            for bb in range(nb):
                for r in range(nheads):
                    qcat_scr[bb * nheads + r, :, :LANES] = q_ref[bb, r]
                    qcat_scr[bb * nheads + r, :, LANES:] = u_ref[bb, r // NSA_GROUP_HEADS]

    lane = lax.broadcasted_iota(jnp.int32, (tq, LANES), 1)
    lo = lane < HEAD_DIM

    def tile(diagonal):
        masked = diagonal or window is not None
        if masked:
            row = lax.broadcasted_iota(jnp.int32, (tq, tk), 0)
            col = lax.broadcasted_iota(jnp.int32, (tq, tk), 1)
            if diagonal:
                valid = row >= col
            else:
                valid = (tq + row - col) < window
        for bb in range(nb):
            slot0, pair0, brow = bb * nheads, bb * (nheads // 2), b * nb + bb
            for p in range(nheads // 2):
                pvs, alphas = [], []
                for e in range(2):
                    hh = 2 * p + e
                    if has_u:
                        q = qcat_scr[slot0 + hh]
                    elif has_delta and not diagonal:
                        q = qoff_ref[bb, hh]
                    else:
                        q = q_ref[bb, hh]
                    k = k_ref[bb, hh // NSA_GROUP_HEADS] if shared_kv else k_ref[bb, hh]
                    v = (v_ref[bb, hh // NSA_GROUP_HEADS] if shared_kv
                         else v_ref[bb, :, p * LANES:(p + 1) * LANES])
                    s = lax.dot_general(q, k, (((1,), (1,)), ((), ())), preferred_element_type=F32)
                    if masked:
                        s = jnp.where(valid, s, NEG)
                    delta = None
                    if has_delta and not diagonal:
                        hoff = F_LANE + g * nheads + hh
                        delta = (base_ref[(brow * nblk + qi) * LANES + hoff]
                                 - base_ref[(brow * nblk + ki + 1) * LANES + hoff])
                    if fixed_shift:
                        pm = jnp.exp2(s)
                        rowsum = pm[:, :LANES]
                        for j in range(1, tk // LANES):
                            rowsum = rowsum + pm[:, j * LANES:(j + 1) * LANES]
                        pv = jnp.dot(pm.astype(BF16), v, preferred_element_type=F32)
                        if delta is not None:
                            w = jnp.exp2(jnp.full((1, LANES), delta, F32))
                            rowsum = rowsum * w
                            pv = pv * w
                        l_scr[slot0 + hh] = l_scr[slot0 + hh] + rowsum
                        pvs.append(pv)
                    else:
                        m_prev = m_scr[slot0 + hh]
                        m_cur = jnp.max(s, axis=1, keepdims=True)
                        if delta is not None:
                            m_cur = m_cur + delta
                        m_new = jnp.maximum(m_prev, m_cur)
                        alpha = jnp.exp2(m_prev - m_new)
                        shift = m_new[:, :1]
                        if delta is not None:
                            shift = shift - delta
                        pm = jnp.exp2(s - shift)
                        l_scr[slot0 + hh] = alpha * l_scr[slot0 + hh] + jnp.sum(pm, axis=1, keepdims=True)
                        m_scr[slot0 + hh] = m_new
                        pvs.append(jnp.dot(pm.astype(BF16), v, preferred_element_type=F32))
                        alphas.append(alpha)
                if fixed_shift:
                    acc_scr[pair0 + p] = acc_scr[pair0 + p] + jnp.where(lo, pvs[0], pvs[1])
                else:
                    acc_scr[pair0 + p] = (acc_scr[pair0 + p] * jnp.where(lo, alphas[0], alphas[1])
                                          + jnp.where(lo, pvs[0], pvs[1]))

    pl.when((qi == ki) & active)(lambda: tile(True))
    pl.when((qi != ki) & active)(lambda: tile(False))

    @pl.when((fl & 2) == 2)
    def _finish():
        for bb in range(nb):
            slot0, pair0 = bb * nheads, bb * (nheads // 2)
            for p in range(nheads // 2):
                if fixed_shift:
                    la = jnp.sum(l_scr[slot0 + 2 * p], axis=1, keepdims=True)
                    lb = jnp.sum(l_scr[slot0 + 2 * p + 1], axis=1, keepdims=True)
                else:
                    la, lb = l_scr[slot0 + 2 * p], l_scr[slot0 + 2 * p + 1]
                inv = jnp.where(lo, 1.0 / la, 1.0 / lb)
                if gate_branch is not None:
                    gate_blk = gate_ref[bb].astype(F32)
                    h0 = g * nheads + 2 * p
                    ga = _gate_column(gate_blk, lane, 3 * h0 + gate_branch)
                    gb = _gate_column(gate_blk, lane, 3 * (h0 + 1) + gate_branch)
                    inv = inv * jnp.where(lo, ga, gb)
                o_ref[bb, :, p * LANES:(p + 1) * LANES] = (acc_scr[pair0 + p] * inv).astype(o_ref.dtype)


def _tile_tables(nq, window_tiles, batch):
    qi, ki, fl = [], [], []
    for i in range(nq):
        lo = 0 if window_tiles is None else max(0, i - window_tiles)
        for j in range(lo, i + 1):
            qi.append(i)
            ki.append(j)
            fl.append((1 if j == lo else 0) | (2 if j == i else 0) | 4)
    rows = lambda a: jnp.tile(jnp.asarray(a, jnp.int32), batch)
    return rows(qi), rows(ki), rows(fl)


def _decayed_tile_tables(nq, batch, nb, base):
    qi = np.repeat(np.arange(nq), np.arange(1, nq + 1))
    ki = np.concatenate([np.arange(i + 1) for i in range(nq)])
    nsteps = qi.shape[0]
    diag = jnp.asarray(qi == ki)
    gate = base.reshape(batch, nq, LANES)[:, :, F_LANE:F_LANE + FOX_HEADS]
    delta = gate[:, qi, :] - gate[:, np.minimum(ki + 1, nq - 1), :]
    top = jnp.max(delta.reshape(batch // nb, nb, nsteps, FOX_HEADS), axis=(1, 3))
    active = diag[None, :] | (top > ZERO_WEIGHT_LOG2)
    same_q = jnp.asarray(qi[:, None] == np.arange(nq)[None, :])
    first_ki = jnp.min(jnp.where(active[:, :, None] & same_q[None], ki[None, :, None], nq), axis=1)
    first = active & (jnp.asarray(ki)[None, :] == first_ki[:, qi])
    flags = first.astype(jnp.int32) + 2 * diag[None, :].astype(jnp.int32) + 4 * active.astype(jnp.int32)
    order = jnp.argsort(jnp.logical_not(active), axis=1, stable=True)
    n_active = jnp.sum(active, axis=1, keepdims=True)
    slot = jnp.arange(nsteps)[None, :]
    src = jnp.take_along_axis(order, jnp.minimum(slot, n_active - 1), axis=1)
    kept = slot < n_active
    take = lambda a: jnp.asarray(a, jnp.int32)[src].reshape(-1)
    fl = jnp.where(kept, jnp.take_along_axis(flags, src, axis=1), 0).reshape(-1)
    return take(qi), take(ki), fl.astype(jnp.int32)


def _flash(q, k, v, *, batch, seq, shared_kv, fixed_shift, window=None, base=None, q_off=None,
           u=None, z=None, gate_branch=None, v_col=0):
    nheads = NSA_HEADS
    nb = FLASH_ROWS if fixed_shift else SAFE_FLASH_ROWS
    nb = nb if batch % nb == 0 else 1
    nq = seq // TQ
    ngroups = NSA_HEADS // nheads
    nkv = NSA_KV_GROUPS
    da = k.shape[-1]
    has_delta = base is not None
    has_u = u is not None
    out_w = LANES * (nheads // 2)
    if has_delta and fixed_shift:
        tabs = _decayed_tile_tables(nq, batch, nb, base)
    else:
        tabs = _tile_tables(nq, None if window is None else window // TQ, batch // nb)
    nsteps = int(tabs[0].shape[0]) // (batch // nb)

    by_q_heads = lambda b, g, s, qt, kt, ft: (b, g, qt[b * nsteps + s], 0)
    by_k_heads = lambda b, g, s, qt, kt, ft: (b, g, kt[b * nsteps + s], 0)
    by_q_tokens = lambda col: (lambda b, g, s, qt, kt, ft: (b, qt[b * nsteps + s], g if col is None else col))
    by_k_tokens = lambda b, g, s, qt, kt, ft: (b, kt[b * nsteps + s], g + v_col)

    in_specs, args = [], []
    if has_delta:
        in_specs.append(pl.BlockSpec(memory_space=pltpu.SMEM))
        args.append(base)
    in_specs.append(pl.BlockSpec((nb, nheads, TQ, LANES), by_q_heads))
    args.append(q)
    if has_delta:
        in_specs.append(pl.BlockSpec((nb, nheads, TQ, LANES), by_q_heads))
        args.append(q_off)
    if has_u:
        in_specs.append(pl.BlockSpec((nb, nkv, TQ, LANES), by_q_heads))
        args.append(u)
    if shared_kv:
        in_specs.append(pl.BlockSpec((nb, nkv, TQ, da), by_k_heads))
        in_specs.append(pl.BlockSpec((nb, nkv, TQ, LANES), by_k_heads))
    else:
        in_specs.append(pl.BlockSpec((nb, nheads, TQ, da), by_k_heads))
        in_specs.append(pl.BlockSpec((nb, TQ, out_w), by_k_tokens))
    args += [k, v]
    if gate_branch is not None:
        in_specs.append(pl.BlockSpec((nb, TQ, LANES), by_q_tokens(C_SMALL // LANES)))
        args.append(z)

    m_rows = 8 if fixed_shift else TQ
    scratch = [pltpu.VMEM((nb * nheads, m_rows, LANES), F32),
               pltpu.VMEM((nb * nheads, TQ, LANES), F32),
               pltpu.VMEM((nb * nheads // 2, TQ, LANES), F32)]
    if has_u:
        scratch.append(pltpu.VMEM((nb * nheads, TQ, 2 * LANES), BF16))

    kern = functools.partial(_flash_kernel, nheads=nheads, shared_kv=shared_kv, window=window,
                             has_delta=has_delta, has_u=has_u, gate_branch=gate_branch, nblk=nq,
                             fixed_shift=fixed_shift, nb=nb)
    return pl.pallas_call(
        kern,
        grid_spec=pltpu.PrefetchScalarGridSpec(
            num_scalar_prefetch=3,
            grid=(batch // nb, ngroups, nsteps),
            in_specs=in_specs,
            out_specs=pl.BlockSpec((nb, TQ, out_w), by_q_tokens(None)),
            scratch_shapes=scratch),
        out_shape=jax.ShapeDtypeStruct((batch, seq, 4 * LANES), BF16),
        compiler_params=_cparams(("parallel", "parallel", "arbitrary")),
        name="flash_" + ("fox" if has_delta else ("sel" if has_u else "win")),
    )(*tabs, *args)


def _cmp_kernel(q_ref, kc_ref, vc_ref, ov_ref, gate_ref, o_ref, u_ref):
    g = pl.program_id(1)
    qi = pl.program_id(2)
    tq = q_ref.shape[2]
    ncmp = kc_ref.shape[2]
    nslc = ov_ref.shape[1]

    t = qi * tq + lax.broadcasted_iota(jnp.int32, (tq, ncmp), 0)
    n = lax.broadcasted_iota(jnp.int32, (tq, ncmp), 1)
    valid = (n * NSA_CMP_STRIDE + (NSA_CMP_LEN - 1)) <= t
    kc = kc_ref[0, 0]
    vc = vc_ref[0, 0]
    ov = ov_ref[...]
    lane = lax.broadcasted_iota(jnp.int32, (tq, LANES), 1)
    lo = lane < HEAD_DIM
    gate_blk = gate_ref[0].astype(F32)

    imp = jnp.zeros((tq, nslc), F32)
    outs = []
    for r in range(NSA_GROUP_HEADS):
        s = lax.dot_general(q_ref[0, r], kc, (((1,), (1,)), ((), ())), preferred_element_type=F32)
        s = jnp.where(valid, s, NEG)
        m = jnp.max(s, axis=1, keepdims=True)
        e = jnp.where(valid, jnp.exp2(s - m), 0.0)
        d = jnp.sum(e, axis=1, keepdims=True)
        pb = (e / jnp.where(d > 0.0, d, 1.0)).astype(BF16)
        o = jnp.dot(pb, vc, preferred_element_type=F32)
        gcol = _gate_column(gate_blk, lane, 3 * (g * NSA_GROUP_HEADS + r))
        outs.append(o * gcol)
        imp = imp + jnp.dot(pb, ov, preferred_element_type=F32)
    for p in range(NSA_GROUP_HEADS // 2):
        o_ref[0, :, p * LANES:(p + 1) * LANES] = jnp.where(lo, outs[2 * p], outs[2 * p + 1]).astype(o_ref.dtype)
    _write_unselected(u_ref, imp, qi, tq, nslc)


def _cmp_fixed_kernel(q_ref, kc_ref, vc_ref, ov_ref, gate_ref, o_ref, u_ref, d_scr, acc_scr, imp_scr,
                      *, chunk):
    g = pl.program_id(1)
    qi = pl.program_id(2)
    tq = q_ref.shape[2]
    ncmp = kc_ref.shape[2]
    nslc = ov_ref.shape[1]
    t0 = qi * tq
    d_scr[...] = jnp.zeros(d_scr.shape, F32)
    acc_scr[...] = jnp.zeros(acc_scr.shape, F32)
    imp_scr[...] = jnp.zeros(imp_scr.shape, F32)

    for c in range(ncmp // chunk):
        first_end = c * chunk * NSA_CMP_STRIDE + NSA_CMP_LEN - 1
        last_end = ((c + 1) * chunk - 1) * NSA_CMP_STRIDE + NSA_CMP_LEN - 1

        def body(masked, c=c):
            rows = slice(c * chunk, (c + 1) * chunk)
            kc = kc_ref[0, 0, rows, :]
            vo = jnp.concatenate([vc_ref[0, 0, rows, :], ov_ref[rows, :]], axis=1)
            if masked:
                t = t0 + lax.broadcasted_iota(jnp.int32, (tq, chunk), 0)
                n = c * chunk + lax.broadcasted_iota(jnp.int32, (tq, chunk), 1)
                valid = (n * NSA_CMP_STRIDE + (NSA_CMP_LEN - 1)) <= t
            for r in range(NSA_GROUP_HEADS):
                s = lax.dot_general(q_ref[0, r], kc, (((1,), (1,)), ((), ())), preferred_element_type=F32)
                if masked:
                    s = jnp.where(valid, s, NEG)
                e = jnp.exp2(s)
                part = e[:, :LANES]
                for j in range(1, chunk // LANES):
                    part = part + e[:, j * LANES:(j + 1) * LANES]
                d_scr[r] = d_scr[r] + part
                both = jnp.dot(e.astype(BF16), vo, preferred_element_type=F32)
                acc_scr[r] = acc_scr[r] + both[:, :LANES]
                imp_scr[r] = imp_scr[r] + both[:, LANES:]

        pl.when((first_end <= t0 + (tq - 1)) & (last_end > t0))(functools.partial(body, True))
        pl.when(last_end <= t0)(functools.partial(body, False))

    lane = lax.broadcasted_iota(jnp.int32, (tq, LANES), 1)
    lo = lane < HEAD_DIM
    gate_blk = gate_ref[0].astype(F32)
    imp = jnp.zeros((tq, nslc), F32)
    outs = []
    for r in range(NSA_GROUP_HEADS):
        d = jnp.sum(d_scr[r], axis=1, keepdims=True)
        inv = 1.0 / jnp.where(d > 0.0, d, 1.0)
        gcol = _gate_column(gate_blk, lane, 3 * (g * NSA_GROUP_HEADS + r))
        outs.append(acc_scr[r] * (inv * gcol))
        imp = imp + imp_scr[r] * inv
    for p in range(NSA_GROUP_HEADS // 2):
        o_ref[0, :, p * LANES:(p + 1) * LANES] = jnp.where(lo, outs[2 * p], outs[2 * p + 1]).astype(o_ref.dtype)
    _write_unselected(u_ref, imp, qi, tq, nslc)


N_FORCED = 3


def _write_unselected(u_ref, imp, qi, tq, nslc):
    tpos = qi * tq + lax.broadcasted_iota(jnp.int32, (tq, nslc), 0)
    jblk = lax.broadcasted_iota(jnp.int32, (tq, nslc), 1)
    cur = jnp.right_shift(tpos, SEL_SHIFT)
    forced = (jblk == 0) | (jblk == cur) | (jblk == cur - 1)
    visible = jblk <= cur
    score_t = jnp.where(visible & jnp.logical_not(forced), imp, -jnp.inf).T

    def run(nrows):
        sc = score_t[:nrows]
        jt = lax.broadcasted_iota(jnp.int32, (nrows, tq), 0).astype(F32)
        for _ in range(NSA_SEL_TOPK - N_FORCED):
            mx = jnp.max(sc, axis=0, keepdims=True)
            first = jnp.min(jnp.where(sc == mx, jt, float(nslc)), axis=0, keepdims=True)
            sc = jnp.where(jt == first, -jnp.inf, sc)
        taken = jnp.where(sc == -jnp.inf, 1.0, 0.0)
        if nrows < nslc:
            taken = jnp.concatenate([taken, jnp.zeros((nslc - nrows, tq), F32)], axis=0)
        chosen = visible & (forced | (taken.T > 0.5))
        u_ref[0, 0] = jnp.where(chosen, 0.0, 1.0).astype(BF16)

    half = nslc // 2
    last_block = jnp.right_shift(qi * tq + (tq - 1), SEL_SHIFT)
    pl.when(last_block < half)(functools.partial(run, half))
    pl.when(last_block >= half)(functools.partial(run, nslc))


def _cmp_select(qn, kc, vc, ov, z, *, batch, seq, fixed_shift):
    ncmp = kc.shape[2]
    nslc = ov.shape[1]
    if fixed_shift:
        chunk = 2 * LANES if ncmp % (2 * LANES) == 0 else ncmp
        kern = functools.partial(_cmp_fixed_kernel, chunk=chunk)
        scratch = [pltpu.VMEM((NSA_GROUP_HEADS, CMP_TQ, LANES), F32) for _ in range(3)]
    else:
        kern, scratch = _cmp_kernel, []
    return pl.pallas_call(
        kern,
        scratch_shapes=scratch,
        grid=(batch, NSA_KV_GROUPS, seq // CMP_TQ),
        in_specs=[pl.BlockSpec((1, NSA_GROUP_HEADS, CMP_TQ, LANES), lambda b, g, i: (b, g, i, 0)),
                  pl.BlockSpec((1, 1, ncmp, LANES), lambda b, g, i: (b, g, 0, 0)),
                  pl.BlockSpec((1, 1, ncmp, LANES), lambda b, g, i: (b, g, 0, 0)),
                  pl.BlockSpec((ncmp, nslc), lambda b, g, i: (0, 0)),
                  pl.BlockSpec((1, CMP_TQ, LANES), lambda b, g, i: (b, i, C_SMALL // LANES))],
        out_specs=[pl.BlockSpec((1, CMP_TQ, 2 * LANES), lambda b, g, i: (b, i, g)),
                   pl.BlockSpec((1, 1, CMP_TQ, nslc), lambda b, g, i: (b, g, i, 0))],
        out_shape=[jax.ShapeDtypeStruct((batch, seq, NSA_WIDTH), BF16),
                   jax.ShapeDtypeStruct((batch, NSA_KV_GROUPS, seq, nslc), BF16)],
        compiler_params=_cparams(("parallel", "parallel", "parallel")),
        name="cmp_select",
    )(qn, kc, vc, ov, z)


def _head_rms(x, g):
    return x * lax.rsqrt(jnp.mean(x * x, axis=-1, keepdims=True) + EPS) * g


def _mem_kernel(zq_ref, kv_ref, qg_ref, kg_ref, o_ref, kn_scr, vb_scr):
    @pl.when(pl.program_id(1) == 0)
    def _prep():
        for h in range(MEM_HEADS):
            kh = kv_ref[0, :, h * MEM_HEAD_DIM:(h + 1) * MEM_HEAD_DIM]
            kn_scr[h] = _head_rms(kh, kg_ref[...]).astype(BF16)
            vb_scr[h] = kv_ref[0, :, MEM_WIDTH + h * MEM_HEAD_DIM:
                               MEM_WIDTH + (h + 1) * MEM_HEAD_DIM].astype(BF16)

    for h in range(MEM_HEADS):
        sl = slice(h * MEM_HEAD_DIM, (h + 1) * MEM_HEAD_DIM)
        qh = (_head_rms(zq_ref[0, :, sl].astype(F32), qg_ref[...])
              * ((MEM_HEAD_DIM ** -0.5) * LOG2E)).astype(BF16)
        s = lax.dot_general(qh, kn_scr[h], (((1,), (1,)), ((), ())), preferred_element_type=F32)
        m = jnp.max(s, axis=1, keepdims=True)
        e = jnp.exp2(s - m)
        p = (e / jnp.sum(e, axis=1, keepdims=True)).astype(BF16)
        o_ref[0, :, sl] = jnp.dot(p, vb_scr[h], preferred_element_type=F32).astype(o_ref.dtype)


def _mem_attention(z, mem_kv, q_g, k_g, *, batch, seq):
    mlen = mem_kv.shape[1]
    tq = TQ
    return pl.pallas_call(
        _mem_kernel,
        grid=(batch, seq // tq),
        in_specs=[pl.BlockSpec((1, tq, MEM_WIDTH), lambda b, i: (b, i, C_MEM_Q // MEM_WIDTH)),
                  pl.BlockSpec((1, mlen, 2 * MEM_WIDTH), lambda b, i: (b, 0, 0)),
                  pl.BlockSpec((1, MEM_HEAD_DIM), lambda b, i: (0, 0)),
                  pl.BlockSpec((1, MEM_HEAD_DIM), lambda b, i: (0, 0))],
        out_specs=pl.BlockSpec((1, tq, MEM_WIDTH), lambda b, i: (b, i, 0)),
        out_shape=jax.ShapeDtypeStruct((batch, seq, MEM_WIDTH), BF16),
        scratch_shapes=[pltpu.VMEM((MEM_HEADS, mlen, MEM_HEAD_DIM), BF16),
                        pltpu.VMEM((MEM_HEADS, mlen, MEM_HEAD_DIM), BF16)],
        compiler_params=_cparams(("parallel", "arbitrary")),
        name="mem_attention",
    )(z, mem_kv, q_g.reshape(1, MEM_HEAD_DIM), k_g.reshape(1, MEM_HEAD_DIM))


def _out_kernel(x_ref, oc_ref, os_ref, ow_ref, ob_ref, om_ref, sa_ref, sb_ref, sm_ref,
                g0_ref, g1_ref, g2_ref, wa_ref, wb_ref, wm_ref, wo_ref, y_ref):
    oa = ((oc_ref[...].astype(F32) + os_ref[...].astype(F32) + ow_ref[...].astype(F32))
          * jax.nn.silu(sa_ref[...].astype(F32)))
    ob = ob_ref[...].astype(F32) * jax.nn.silu(sb_ref[...].astype(F32))
    om = om_ref[...].astype(F32) * jax.nn.silu(sm_ref[...].astype(F32))
    gate = lambda ref: jax.nn.sigmoid(ref[...].astype(F32))
    u = (gate(g0_ref) * jnp.dot(oa.astype(BF16), wa_ref[...], preferred_element_type=F32)
         + gate(g1_ref) * jnp.dot(ob.astype(BF16), wb_ref[...], preferred_element_type=F32)
         + gate(g2_ref) * jnp.dot(om.astype(BF16), wm_ref[...], preferred_element_type=F32))
    y_ref[...] = x_ref[...] + jnp.dot(u.astype(BF16), wo_ref[...], preferred_element_type=F32)


def _out_proj(x2d, o_c, o_s, o_w, o_b, o_m, z2d, wa, wb, wm, wo):
    m = x2d.shape[0]
    tm = 256
    w512 = 512
    row512 = lambda c: pl.BlockSpec((tm, w512), lambda i, c=c: (i, c))
    row1024 = lambda c: pl.BlockSpec((tm, D_MODEL), lambda i, c=c: (i, c))
    full = lambda shape: pl.BlockSpec(shape, lambda i: (0, 0))
    return pl.pallas_call(
        _out_kernel,
        grid=(m // tm,),
        in_specs=[row1024(0), row512(0), row512(0), row512(0), row512(0), row512(0),
                  row512(C_NSA_SILU // w512), row512(C_FOX_SILU // w512), row512(C_MEM_SILU // w512),
                  row1024(C_MERGE // D_MODEL), row1024(C_MERGE // D_MODEL + 1),
                  row1024(C_MERGE // D_MODEL + 2),
                  full((w512, D_MODEL)), full((w512, D_MODEL)), full((w512, D_MODEL)),
                  full((D_MODEL, D_MODEL))],
        out_specs=row1024(0),
        out_shape=jax.ShapeDtypeStruct((m, D_MODEL), F32),
        compiler_params=_cparams(("parallel",)),
        name="out_proj",
    )(x2d, o_c, o_s, o_w, o_b, o_m, z2d, z2d, z2d, z2d, z2d, z2d, wa, wb, wm, wo)


def _permute_w_in(w_in):
    o = np.cumsum([0, NSA_WIDTH, 6 * NSA_KV_GROUPS * HEAD_DIM, 3 * NSA_HEADS, NSA_WIDTH,
                   3 * FOX_WIDTH, FOX_HEADS, FOX_WIDTH, MEM_WIDTH, MEM_WIDTH, 3 * D_MODEL])
    nsa_q, nsa_kv, nsa_gate, nsa_silu, fox_qkv, fox_f, fox_silu, mem_q, mem_silu, merge = [
        w_in[..., o[i]:o[i + 1]] for i in range(10)]
    pad = jnp.zeros(w_in.shape[:2] + (LANES - 3 * NSA_HEADS - FOX_HEADS,), w_in.dtype)
    return jnp.concatenate([nsa_q, nsa_silu, fox_qkv, fox_silu, mem_q, mem_silu, merge, nsa_kv,
                            nsa_gate, fox_f, pad], axis=-1).astype(BF16)


def _pad_lanes(x, width=LANES):
    return jnp.pad(x, [(0, 0)] * (x.ndim - 1) + [(0, width - x.shape[-1])])


def _tile2(g):
    return jnp.concatenate([g, g], axis=-1)[:, None, :]


def _overlap(n_cmp_pad, n_slc):
    cs = np.arange(n_cmp_pad)[:, None] * NSA_CMP_STRIDE
    ss = np.arange(NSLC_PAD)[None, :] * NSA_SEL_LEN
    ov = np.clip(np.minimum(cs + NSA_CMP_LEN, ss + NSA_SEL_LEN) - np.maximum(cs, ss), 0, None)
    ov = ov / NSA_CMP_LEN
    ov[-1] = 0.0
    ov[:, n_slc:] = 0.0
    return jnp.asarray(ov, BF16)


def _aug_permutations():
    pq = np.zeros((6 * LANES, FOX_HEADS * LANES), np.float32)
    pk = np.zeros((6 * LANES, FOX_HEADS * LANES), np.float32)
    for h in range(FOX_HEADS):
        for i in range(3):
            pq[i * LANES + F_LANE + h, h * LANES + AUG + i] = 1.0
            pk[i * LANES + F_LANE + h, h * LANES + AUG + 3 + i] = -1.0
            pk[(3 + i) * LANES + F_LANE + h, h * LANES + AUG + 6 + i] = -1.0
    return jnp.asarray(pq, BF16), jnp.asarray(pk, BF16)


def _compress_w1(w1):
    nl = w1.shape[0]
    halves = w1.reshape(nl, 2, NSA_CMP_STRIDE, 1, HEAD_DIM, NSA_CMP_HIDDEN)
    eye = jnp.eye(NSA_KV_GROUPS, dtype=w1.dtype)
    out = jnp.einsum("pg,zhldn->zlpdghn", eye, halves[:, :, :, 0])
    return out.reshape(nl, NSA_CMP_STRIDE * NSA_KV_GROUPS * HEAD_DIM,
                       NSA_KV_GROUPS * 2 * NSA_CMP_HIDDEN).astype(BF16)


def _layer(x, mem, p, consts):
    batch, seq, _ = x.shape
    m_rows = batch * seq
    cos4, sin_signed, tri, head_ones, perm_q, perm_k, ov = consts
    z2d = _norm_matmul(x.reshape(m_rows, D_MODEL), p["norm_g"], p["w_in"], 512, Z_WIDTH // 3, BF16)
    z = z2d.reshape(batch, seq, Z_WIDTH)

    (qn, kcr, vcr, ks, vs, kw, vw, fqd, fqo, fka, base) = _prep(
        z, cos4, sin_signed, tri, head_ones, perm_q, perm_k,
        p["gq"], p["gks"], p["gkw"], p["gfq"], p["gfk"], p["fb"],
        p["s_sel"], p["s_win"], p["s_fox"], batch=batch, seq=seq)

    nchunk = seq // NSA_CMP_STRIDE
    kc, vc = _compress(kcr.reshape(batch, nchunk, NSA_CMP_STRIDE * LANES),
                       vcr.reshape(batch, nchunk, NSA_CMP_STRIDE * LANES),
                       p["w1k"], p["w1v"], p["bk"], p["bv"], p["w2k"], p["w2v"], p["gkc"], p["s_cmp"],
                       batch=batch)

    def attend(fixed_shift):
        o_c, unsel = _cmp_select(qn, kc, vc, ov, z, batch=batch, seq=seq, fixed_shift=fixed_shift)
        o_s = _flash(qn, ks, vs, batch=batch, seq=seq, shared_kv=True, fixed_shift=fixed_shift,
                     u=unsel, z=z, gate_branch=1)
        o_w = _flash(qn, kw, vw, batch=batch, seq=seq, shared_kv=True, fixed_shift=fixed_shift,
                     window=NSA_WINDOW, z=z, gate_branch=2)
        o_b = _flash(fqd, fka, z, batch=batch, seq=seq, shared_kv=False, fixed_shift=fixed_shift,
                     base=base.reshape(-1), q_off=fqo, v_col=C_FOX_V // FOX_WIDTH)
        return o_c, o_s, o_w, o_b

    o_c, o_s, o_w, o_b = lax.cond(p["bound_ok"], lambda: attend(True), lambda: attend(False))

    mlen = mem.shape[1]
    mem_kv = _norm_matmul(mem.reshape(batch * mlen, D_MODEL), p["mem_norm_g"], p["w_mem_kv"],
                          min(512, batch * mlen), 2 * MEM_WIDTH, F32).reshape(batch, mlen, 2 * MEM_WIDTH)
    o_m = _mem_attention(z, mem_kv, p["mem_q_norm"], p["mem_k_norm"], batch=batch, seq=seq)

    y = _out_proj(x.reshape(m_rows, D_MODEL), o_c.reshape(m_rows, -1), o_s.reshape(m_rows, -1),
                  o_w.reshape(m_rows, -1), o_b.reshape(m_rows, -1), o_m.reshape(m_rows, -1), z2d,
                  p["w_branch_a"], p["w_branch_b"], p["w_branch_m"], p["w_out"])
    return y.reshape(batch, seq, D_MODEL)


def kernel(x, mem, norm_g, mem_norm_g, w_in, nsa_q_norm, nsa_k_norm, cmp_pe_k, cmp_w1_k, cmp_w2_k,
           cmp_pe_v, cmp_w1_v, cmp_w2_v, fox_q_norm, fox_k_norm, fox_f_bias, mem_q_norm, mem_k_norm,
           w_mem_kv, w_branch_a, w_branch_b, w_branch_m, w_out):
    batch, seq, _ = x.shape
    depth = w_in.shape[0]
    n_slc = seq // NSA_SEL_LEN
    assert n_slc <= NSLC_PAD and seq % TQ == 0 and NSA_WINDOW == TQ

    half = HEAD_DIM // 2
    inv_freq = ROPE_THETA ** (-jnp.arange(half, dtype=F32) / half)
    ang = jnp.arange(seq).astype(F32)[:, None] * inv_freq[None, :]
    cos, sin = jnp.cos(ang), jnp.sin(ang)
    cos4 = jnp.concatenate([cos, cos, cos, cos], axis=-1)
    sin_signed = jnp.concatenate([-sin, sin, -sin, sin], axis=-1)
    tri = jnp.asarray(np.tril(np.ones((TQ, TQ), np.float32)), BF16)
    head_ones = jnp.asarray(np.kron(np.eye(2), np.ones((HEAD_DIM, HEAD_DIM))), BF16)
    consts = (cos4, sin_signed, tri, head_ones, *_aug_permutations(),
              _overlap(seq // NSA_CMP_STRIDE, n_slc))

    hp = lax.Precision.HIGHEST
    pe_bias = lambda pe, w1: jnp.einsum("lk,lkn->ln", pe.reshape(depth, -1), w1, precision=hp)[:, None, :]
    fb = jnp.zeros((depth, 1, LANES), F32).at[:, 0, F_LANE:F_LANE + FOX_HEADS].set(fox_f_bias)

    def logit_bound(gq, gk):
        bound = (BOUND_MARGIN * HEAD_DIM * QK_SCALE
                 * jnp.max(jnp.abs(gq), axis=-1) * jnp.max(jnp.abs(gk), axis=-1))
        return bound.astype(BF16).astype(F32)

    bounds = [logit_bound(nsa_q_norm, nsa_k_norm[:, 1]), logit_bound(nsa_q_norm, nsa_k_norm[:, 2]),
              logit_bound(fox_q_norm, fox_k_norm), logit_bound(nsa_q_norm, nsa_k_norm[:, 0])]
    bound_ok = functools.reduce(jnp.maximum, bounds) <= MAX_FIXED_BOUND
    bound_row = lambda s: jnp.broadcast_to(s[:, None, None], (depth, 1, LANES))
    stacked = dict(
        s_sel=bound_row(bounds[0]), s_win=bound_row(bounds[1]), s_fox=bound_row(bounds[2]),
        s_cmp=bound_row(bounds[3]),
        bound_ok=bound_ok,
        norm_g=norm_g, mem_norm_g=mem_norm_g, w_in=_permute_w_in(w_in),
        gq=_tile2(nsa_q_norm), gks=_tile2(nsa_k_norm[:, 1]), gkw=_tile2(nsa_k_norm[:, 2]),
        gkc=_pad_lanes(nsa_k_norm[:, 0])[:, None, :], gfq=_tile2(fox_q_norm), gfk=_tile2(fox_k_norm),
        fb=fb, w1k=_compress_w1(cmp_w1_k), w1v=_compress_w1(cmp_w1_v),
        bk=pe_bias(cmp_pe_k, cmp_w1_k), bv=pe_bias(cmp_pe_v, cmp_w1_v),
        w2k=_pad_lanes(cmp_w2_k).astype(BF16),
        w2v=jnp.concatenate([cmp_w2_v, cmp_w2_v], axis=-1).astype(BF16),
        mem_q_norm=mem_q_norm, mem_k_norm=mem_k_norm, w_mem_kv=w_mem_kv.astype(BF16),
        w_branch_a=w_branch_a.astype(BF16), w_branch_b=w_branch_b.astype(BF16),
        w_branch_m=w_branch_m.astype(BF16), w_out=w_out.astype(BF16))
    for l in range(depth):
        x = _layer(x, mem, {k: v[l] for k, v in stacked.items()}, consts)
    return x
```

```python
import functools

import numpy as np
import jax
import jax.numpy as jnp
from jax import lax
from jax.experimental import pallas as pl
from jax.experimental.pallas import tpu as pltpu

F32 = jnp.float32
BF16 = jnp.bfloat16

D_MODEL = 1024
HEAD_DIM = 64
ROPE_THETA = 10000.0
EPS = 1e-6
NSA_HEADS = 8
NSA_KV_GROUPS = 2
NSA_GROUP_HEADS = NSA_HEADS // NSA_KV_GROUPS
NSA_CMP_LEN = 32
NSA_CMP_STRIDE = 16
NSA_CMP_HIDDEN = 128
NSA_SEL_LEN = 64
NSA_SEL_TOPK = 16
NSA_WINDOW = 512
FORCE_SCORE = 1e9
NSA_WIDTH = NSA_HEADS * HEAD_DIM
FOX_HEADS = 8
FOX_WIDTH = FOX_HEADS * HEAD_DIM
MEM_HEADS = 4
MEM_HEAD_DIM = 128
MEM_WIDTH = MEM_HEADS * MEM_HEAD_DIM

LANES = 128
VMEM_LIMIT = 56 * 1024 * 1024
NEG = -1e30
SEL_MASK = 32768.0
SEL_SHIFT = 6
NSLC_PAD = LANES
LOG2E = 1.4426950408889634
QK_SCALE = (HEAD_DIM ** -0.5) * LOG2E
BOUND_MARGIN = 1.02
MAX_FIXED_BOUND = 60.0
ZERO_WEIGHT_LOG2 = -160.0

C_NSA_Q = 0
C_NSA_SILU = 512
C_FOX_Q = 1024
C_FOX_K = 1536
C_FOX_V = 2048
C_FOX_SILU = 2560
C_MEM_Q = 3072
C_MEM_SILU = 3584
C_MERGE = 4096
C_NSA_KV = 7168
C_SMALL = 7936
Z_WIDTH = 8064
F_LANE = 3 * NSA_HEADS
AUG = HEAD_DIM

TQ = 512
FLASH_ROWS = 4
SAFE_FLASH_ROWS = 2
CMP_TQ = 512


def _cparams(sem):
    return pltpu.CompilerParams(dimension_semantics=sem, vmem_limit_bytes=VMEM_LIMIT)


def _norm_matmul_kernel(x_ref, g_ref, w_ref, o_ref, *, tn):
    x = x_ref[...]
    ms = jnp.mean(x * x, axis=-1, keepdims=True)
    h = (x * lax.rsqrt(ms + EPS) * g_ref[...]).astype(BF16)
    for c0 in range(0, w_ref.shape[1], tn):
        o_ref[:, c0:c0 + tn] = jnp.dot(h, w_ref[:, c0:c0 + tn],
                                       preferred_element_type=F32).astype(o_ref.dtype)


def _norm_matmul(x2d, g, w, layer, tm, tn, out_dtype):
    m, k = x2d.shape
    n = w.shape[2]
    return pl.pallas_call(
        functools.partial(_norm_matmul_kernel, tn=tn),
        grid=(m // tm,),
        in_specs=[pl.BlockSpec((tm, k), lambda i: (i, 0)),
                  pl.BlockSpec((1, k), lambda i: (0, 0)),
                  pl.BlockSpec((None, k, n), lambda i: (layer, 0, 0), pipeline_mode=pl.Buffered(1))],
        out_specs=pl.BlockSpec((tm, n), lambda i: (i, 0)),
        out_shape=jax.ShapeDtypeStruct((m, n), out_dtype),
        compiler_params=_cparams(("parallel",)),
        name="norm_matmul",
    )(x2d, g.reshape(1, k), w)


def _pair_rms(x, g2, head_ones):
    sq = x * x
    hi = sq.astype(BF16)
    low = (sq - hi.astype(F32)).astype(BF16)
    ssq = (jnp.dot(hi, head_ones, preferred_element_type=F32)
           + jnp.dot(low, head_ones, preferred_element_type=F32))
    return x * lax.rsqrt(ssq * (1.0 / HEAD_DIM) + EPS) * g2


def _pair_rope(y, cos4, sin_signed, first_half):
    rot = jnp.where(first_half, pltpu.roll(y, LANES - HEAD_DIM // 2, 1), pltpu.roll(y, HEAD_DIM // 2, 1))
    return y * cos4 + rot * sin_signed


def _head_a(y, lo, tail=0.0):
    return jnp.where(lo, y, tail)


def _head_b(y, lo, tail=0.0):
    return jnp.where(lo, pltpu.roll(y, HEAD_DIM, 1), tail)


def _split3_f32(c):
    hi = c.astype(BF16).astype(F32)
    r = c - hi
    mid = r.astype(BF16).astype(F32)
    return hi, mid, r - mid


def _prep_kernel(zq_ref, zc_ref, zs_ref, zw_ref, fq_ref, fk_ref, sm_ref, cos_ref, sin_ref,
                 tri_ref, ones_ref, pq_ref, pk_ref, gq_ref, gks_ref, gkw_ref, gfq_ref, gfk_ref, fb_ref,
                 ss_ref, sw_ref, sf_ref,
                 qn_ref, kcr_ref, vcr_ref, ks_ref, vs_ref, kw_ref, vw_ref, fqd_ref, fqo_ref, fka_ref,
                 base_ref, run_scr):
    ti = pl.program_id(1)
    tt = zq_ref.shape[1]
    lane = lax.broadcasted_iota(jnp.int32, (tt, LANES), 1)
    lo = lane < HEAD_DIM
    first_half = (lane & (HEAD_DIM // 2)) == 0
    cos4 = cos_ref[...]
    sin_signed = sin_ref[...]
    head_ones = ones_ref[...]
    one_at_aug = jnp.where(lane == AUG, 1.0, 0.0)

    for p in range(NSA_HEADS // 2):
        y = _pair_rope(_pair_rms(zq_ref[0, :, p * LANES:(p + 1) * LANES].astype(F32), gq_ref[...], head_ones),
                       cos4, sin_signed, first_half) * QK_SCALE
        qn_ref[0, 2 * p] = _head_a(y, lo, one_at_aug).astype(BF16)
        qn_ref[0, 2 * p + 1] = _head_b(y, lo, one_at_aug).astype(BF16)

    kcr_ref[0] = _pair_rope(zc_ref[0, :, :LANES].astype(F32), cos4, sin_signed, first_half).astype(BF16)
    vcr_ref[0] = zc_ref[0, :, LANES:].astype(BF16)

    sel_lane = jnp.right_shift(ti * tt + lax.broadcasted_iota(jnp.int32, (tt, LANES), 0), SEL_SHIFT)
    onehot = jnp.where(lane == sel_lane, -SEL_MASK, 0.0).astype(BF16)
    for z_ref, g_ref, k_ref, v_ref, bound_ref, with_onehot in (
            (zs_ref, gks_ref, ks_ref, vs_ref, ss_ref, True),
            (zw_ref, gkw_ref, kw_ref, vw_ref, sw_ref, False)):
        y = _pair_rope(_pair_rms(z_ref[0, :, :LANES].astype(F32), g_ref[...], head_ones),
                       cos4, sin_signed, first_half)
        v = z_ref[0, :, LANES:].astype(F32)
        vr = pltpu.roll(v, HEAD_DIM, 1)
        neg_bound = jnp.where(lane == AUG, -bound_ref[...], 0.0)
        k_ref[0, 0, :, :LANES] = _head_a(y, lo, neg_bound).astype(BF16)
        k_ref[0, 1, :, :LANES] = _head_b(y, lo, neg_bound).astype(BF16)
        if with_onehot:
            k_ref[0, 0, :, LANES:] = onehot
            k_ref[0, 1, :, LANES:] = onehot
        v_ref[0, 0] = jnp.where(lo, v, vr).astype(BF16)
        v_ref[0, 1] = jnp.where(lo, vr, v).astype(BF16)

    @pl.when(ti == 0)
    def _reset():
        run_scr[...] = jnp.zeros(run_scr.shape, F32)

    xf = sm_ref[0].astype(F32) + fb_ref[...]
    log_f = (jnp.minimum(xf, 0.0) - jnp.log1p(jnp.exp(-jnp.abs(xf)))) * LOG2E
    tri = tri_ref[...]
    parts = _split3_f32(log_f)
    cum = sum(jnp.dot(tri, part.astype(BF16), preferred_element_type=F32) for part in parts)
    base_ref[0, 0] = run_scr[...]
    run_scr[...] = run_scr[...] + cum[tt - 1:tt, :]

    parts6 = _split3_f32(cum) + _split3_f32(cum - cum[tt - 1:tt, :])
    x6 = jnp.concatenate([part.astype(BF16) for part in parts6], axis=1)
    aug_q = jnp.dot(x6, pq_ref[...], preferred_element_type=F32)
    aug_k = jnp.dot(x6, pk_ref[...], preferred_element_type=F32)

    lane1 = lax.broadcasted_iota(jnp.int32, (1, LANES), 1)
    span = lambda a, b: jnp.where((lane1 >= a) & (lane1 < b), 1.0, 0.0)
    q_tail_diag = span(AUG + 3, AUG + 6) + span(AUG + 9, AUG + 10)
    q_tail_off = span(AUG + 6, AUG + 10)
    k_tail = span(AUG, AUG + 3) + jnp.where(lane1 == AUG + 9, -sf_ref[...], 0.0)
    for p in range(FOX_HEADS // 2):
        sl = slice(p * LANES, (p + 1) * LANES)
        yq = _pair_rms(fq_ref[0, :, sl].astype(F32), gfq_ref[...], head_ones) * QK_SCALE
        yk = _pair_rms(fk_ref[0, :, sl].astype(F32), gfk_ref[...], head_ones)
        for e, split in ((0, _head_a), (1, _head_b)):
            h = 2 * p + e
            hs = slice(h * LANES, (h + 1) * LANES)
            qh = split(yq, lo) + aug_q[:, hs]
            fqd_ref[0, h] = (qh + q_tail_diag).astype(BF16)
            fqo_ref[0, h] = (qh + q_tail_off).astype(BF16)
            fka_ref[0, h] = (split(yk, lo) + aug_k[:, hs] + k_tail).astype(BF16)


def _prep(z, cos4, sin_signed, tri, head_ones, perm_q, perm_k, gq, gks, gkw, gfq, gfk, fb,
          s_sel, s_win, s_fox, *, batch, seq):
    nblk = seq // TQ
    whole = lambda a: pl.BlockSpec(a.shape, lambda b, i: (0, 0))
    zspec = lambda w, c: pl.BlockSpec((1, TQ, w), lambda b, i, c=c: (b, i, c))
    row = pl.BlockSpec((1, LANES), lambda b, i: (0, 0))
    tab = pl.BlockSpec((TQ, LANES), lambda b, i: (i, 0))
    heads = lambda n, w: pl.BlockSpec((1, n, TQ, w), lambda b, i: (b, 0, i, 0))
    tok = lambda w: pl.BlockSpec((1, TQ, w), lambda b, i: (b, i, 0))
    bf = lambda *shape: jax.ShapeDtypeStruct(shape, BF16)
    return pl.pallas_call(
        _prep_kernel,
        grid=(batch, nblk),
        in_specs=[zspec(NSA_WIDTH, C_NSA_Q // NSA_WIDTH),
                  zspec(2 * LANES, C_NSA_KV // (2 * LANES)),
                  zspec(2 * LANES, C_NSA_KV // (2 * LANES) + 1),
                  zspec(2 * LANES, C_NSA_KV // (2 * LANES) + 2),
                  zspec(FOX_WIDTH, C_FOX_Q // FOX_WIDTH),
                  zspec(FOX_WIDTH, C_FOX_K // FOX_WIDTH),
                  zspec(LANES, C_SMALL // LANES),
                  tab, tab, whole(tri), whole(head_ones), whole(perm_q), whole(perm_k),
                  row, row, row, row, row, row, row, row, row],
        out_specs=[heads(NSA_HEADS, LANES), tok(LANES), tok(LANES),
                   heads(NSA_KV_GROUPS, 2 * LANES), heads(NSA_KV_GROUPS, LANES),
                   heads(NSA_KV_GROUPS, LANES), heads(NSA_KV_GROUPS, LANES),
                   heads(FOX_HEADS, LANES), heads(FOX_HEADS, LANES), heads(FOX_HEADS, LANES),
                   pl.BlockSpec((1, 1, 1, LANES), lambda b, i: (b, i, 0, 0))],
        out_shape=[bf(batch, NSA_HEADS, seq, LANES), bf(batch, seq, LANES), bf(batch, seq, LANES),
                   bf(batch, NSA_KV_GROUPS, seq, 2 * LANES), bf(batch, NSA_KV_GROUPS, seq, LANES),
                   bf(batch, NSA_KV_GROUPS, seq, LANES), bf(batch, NSA_KV_GROUPS, seq, LANES),
                   bf(batch, FOX_HEADS, seq, LANES), bf(batch, FOX_HEADS, seq, LANES),
                   bf(batch, FOX_HEADS, seq, LANES),
                   jax.ShapeDtypeStruct((batch, nblk, 1, LANES), F32)],
        scratch_shapes=[pltpu.VMEM((1, LANES), F32)],
        compiler_params=_cparams(("parallel", "arbitrary")),
        name="prep",
    )(z, z, z, z, z, z, z, cos4, sin_signed, tri, head_ones, perm_q, perm_k,
      gq, gks, gkw, gfq, gfk, fb, s_sel, s_win, s_fox)


def _compress_kernel(xk_ref, xv_ref, w1k_ref, w1v_ref, bk_ref, bv_ref, w2k_ref, w2v_ref, kg_ref,
                     sc_ref, kc_ref, vc_ref):
    nchunk = xk_ref.shape[1]
    lane = lax.broadcasted_iota(jnp.int32, (nchunk, LANES), 1)
    for x_ref, w1_ref, b_ref, w2_ref, o_ref, is_key in ((xk_ref, w1k_ref, bk_ref, w2k_ref, kc_ref, True),
                                                        (xv_ref, w1v_ref, bv_ref, w2v_ref, vc_ref, False)):
        h = jnp.dot(x_ref[0], w1_ref[...], preferred_element_type=F32)
        for g in range(NSA_KV_GROUPS):
            c0 = g * 2 * NSA_CMP_HIDDEN
            top = h[:, c0:c0 + NSA_CMP_HIDDEN]
            bot = h[:, c0 + NSA_CMP_HIDDEN:c0 + 2 * NSA_CMP_HIDDEN]
            hid = top + pltpu.roll(bot, nchunk - 1, 0) + b_ref[...]
            act = (hid * jax.nn.sigmoid(hid)).astype(BF16)
            o = jnp.dot(act, w2_ref[...], preferred_element_type=F32)
            if is_key:
                ms = jnp.sum(o * o, axis=1, keepdims=True) * (1.0 / HEAD_DIM)
                o = o * lax.rsqrt(ms + EPS) * kg_ref[...]
                o = jnp.where(lane == AUG, -sc_ref[...], o)
            o_ref[0, g] = o.astype(BF16)


def _compress(xk, xv, w1k, w1v, bk, bv, w2k, w2v, kg, s_cmp, *, batch):
    nchunk, kin = xk.shape[1], xk.shape[2]
    xspec = pl.BlockSpec((1, nchunk, kin), lambda b: (b, 0, 0))
    full = lambda a: pl.BlockSpec(a.shape, lambda b: (0,) * a.ndim)
    ospec = pl.BlockSpec((1, NSA_KV_GROUPS, nchunk, LANES), lambda b: (b, 0, 0, 0))
    oshape = jax.ShapeDtypeStruct((batch, NSA_KV_GROUPS, nchunk, LANES), BF16)
    return pl.pallas_call(
        _compress_kernel,
        grid=(batch,),
        in_specs=[xspec, xspec, full(w1k), full(w1v), full(bk), full(bv), full(w2k), full(w2v), full(kg),
                  full(s_cmp)],
        out_specs=[ospec, ospec],
        out_shape=[oshape, oshape],
        compiler_params=_cparams(("parallel",)),
        name="compress",
    )(xk, xv, w1k, w1v, bk, bv, w2k, w2v, kg, s_cmp)


def _gate_column(gate_blk, lane, idx):
    col = jnp.sum(jnp.where(lane == idx, gate_blk, 0.0), axis=1, keepdims=True)
    return jax.nn.sigmoid(col)


def _flash_kernel(qi_tab, ki_tab, fl_tab, *refs, nheads, shared_kv, window, has_delta,
                  has_u, gate_branch, nblk, fixed_shift, nb):
    refs = list(refs)
    base_ref = refs.pop(0) if has_delta else None
    q_ref = refs.pop(0)
    qoff_ref = refs.pop(0) if has_delta else None
    u_ref = refs.pop(0) if has_u else None
    k_ref = refs.pop(0)
    v_ref = refs.pop(0)
    gate_ref = refs.pop(0) if gate_branch is not None else None
    o_ref = refs.pop(0)
    m_scr = refs.pop(0)
    l_scr = refs.pop(0)
    acc_scr = refs.pop(0)
    qcat_scr = refs.pop(0) if has_u else None

    b = pl.program_id(0)
    g = pl.program_id(1)
    step = b * pl.num_programs(2) + pl.program_id(2)
    qi = qi_tab[step]
    ki = ki_tab[step]
    fl = fl_tab[step]
    active = (fl & 4) == 4
    tq = q_ref.shape[2]
    tk = k_ref.shape[2]

    @pl.when((fl & 1) == 1)
    def _init():
        m_scr[...] = jnp.full(m_scr.shape, NEG, F32)
        l_scr[...] = jnp.zeros(l_scr.shape, F32)
        acc_scr[...] = jnp.zeros(acc_scr.shape, F32)
        if has_u:
            for bb in range(nb):
                for r in range(nheads):
                    qcat_scr[bb * nheads + r, :, :LANES] = q_ref[bb, r]
                    qcat_scr[bb * nheads + r, :, LANES:] = u_ref[bb, r // NSA_GROUP_HEADS]

    lane = lax.broadcasted_iota(jnp.int32, (tq, LANES), 1)
    lo = lane < HEAD_DIM

    def tile(diagonal):
        masked = diagonal or window is not None
        if masked:
            row = lax.broadcasted_iota(jnp.int32, (tq, tk), 0)
            col = lax.broadcasted_iota(jnp.int32, (tq, tk), 1)
            if diagonal:
                valid = row >= col
            else:
                valid = (tq + row - col) < window
        for bb in range(nb):
            slot0, pair0, brow = bb * nheads, bb * (nheads // 2), b * nb + bb
            for p in range(nheads // 2):
                pvs, alphas = [], []
                for e in range(2):
                    hh = 2 * p + e
                    if has_u:
                        q = qcat_scr[slot0 + hh]
                    elif has_delta and not diagonal:
                        q = qoff_ref[bb, hh]
                    else:
                        q = q_ref[bb, hh]
                    k = k_ref[bb, hh // NSA_GROUP_HEADS] if shared_kv else k_ref[bb, hh]
                    v = (v_ref[bb, hh // NSA_GROUP_HEADS] if shared_kv
                         else v_ref[bb, :, p * LANES:(p + 1) * LANES])
                    s = lax.dot_general(q, k, (((1,), (1,)), ((), ())), preferred_element_type=F32)
                    if masked:
                        s = jnp.where(valid, s, NEG)
                    delta = None
                    if has_delta and not diagonal:
                        hoff = F_LANE + g * nheads + hh
                        delta = (base_ref[(brow * nblk + qi) * LANES + hoff]
                                 - base_ref[(brow * nblk + ki + 1) * LANES + hoff])
                    if fixed_shift:
                        pm = jnp.exp2(s)
                        rowsum = pm[:, :LANES]
                        for j in range(1, tk // LANES):
                            rowsum = rowsum + pm[:, j * LANES:(j + 1) * LANES]
                        pv = jnp.dot(pm.astype(BF16), v, preferred_element_type=F32)
                        if delta is not None:
                            w = jnp.exp2(jnp.full((1, LANES), delta, F32))
                            rowsum = rowsum * w
                            pv = pv * w
                        l_scr[slot0 + hh] = l_scr[slot0 + hh] + rowsum
                        pvs.append(pv)
                    else:
                        m_prev = m_scr[slot0 + hh]
                        m_cur = jnp.max(s, axis=1, keepdims=True)
                        if delta is not None:
                            m_cur = m_cur + delta
                        m_new = jnp.maximum(m_prev, m_cur)
                        alpha = jnp.exp2(m_prev - m_new)
                        shift = m_new[:, :1]
                        if delta is not None:
                            shift = shift - delta
                        pm = jnp.exp2(s - shift)
                        l_scr[slot0 + hh] = alpha * l_scr[slot0 + hh] + jnp.sum(pm, axis=1, keepdims=True)
                        m_scr[slot0 + hh] = m_new
                        pvs.append(jnp.dot(pm.astype(BF16), v, preferred_element_type=F32))
                        alphas.append(alpha)
                if fixed_shift:
                    acc_scr[pair0 + p] = acc_scr[pair0 + p] + jnp.where(lo, pvs[0], pvs[1])
                else:
                    acc_scr[pair0 + p] = (acc_scr[pair0 + p] * jnp.where(lo, alphas[0], alphas[1])
                                          + jnp.where(lo, pvs[0], pvs[1]))

    pl.when((qi == ki) & active)(lambda: tile(True))
    pl.when((qi != ki) & active)(lambda: tile(False))

    @pl.when((fl & 2) == 2)
    def _finish():
        for bb in range(nb):
            slot0, pair0 = bb * nheads, bb * (nheads // 2)
            for p in range(nheads // 2):
                if fixed_shift:
                    la = jnp.sum(l_scr[slot0 + 2 * p], axis=1, keepdims=True)
                    lb = jnp.sum(l_scr[slot0 + 2 * p + 1], axis=1, keepdims=True)
                else:
                    la, lb = l_scr[slot0 + 2 * p], l_scr[slot0 + 2 * p + 1]
                inv = jnp.where(lo, 1.0 / la, 1.0 / lb)
                if gate_branch is not None:
                    gate_blk = gate_ref[bb].astype(F32)
                    h0 = g * nheads + 2 * p
                    ga = _gate_column(gate_blk, lane, 3 * h0 + gate_branch)
                    gb = _gate_column(gate_blk, lane, 3 * (h0 + 1) + gate_branch)
                    inv = inv * jnp.where(lo, ga, gb)
                o_ref[bb, :, p * LANES:(p + 1) * LANES] = (acc_scr[pair0 + p] * inv).astype(o_ref.dtype)


def _tile_tables(nq, window_tiles, batch):
    qi, ki, fl = [], [], []
    for i in range(nq):
        lo = 0 if window_tiles is None else max(0, i - window_tiles)
        for j in range(lo, i + 1):
            qi.append(i)
            ki.append(j)
            fl.append((1 if j == lo else 0) | (2 if j == i else 0) | 4)
    rows = lambda a: jnp.tile(jnp.asarray(a, jnp.int32), batch)
    return rows(qi), rows(ki), rows(fl)


def _decayed_tile_tables(nq, batch, nb, base):
    qi = np.repeat(np.arange(nq), np.arange(1, nq + 1))
    ki = np.concatenate([np.arange(i + 1) for i in range(nq)])
    nsteps = qi.shape[0]
    diag = jnp.asarray(qi == ki)
    gate = base.reshape(batch, nq, LANES)[:, :, F_LANE:F_LANE + FOX_HEADS]
    delta = gate[:, qi, :] - gate[:, np.minimum(ki + 1, nq - 1), :]
    top = jnp.max(delta.reshape(batch // nb, nb, nsteps, FOX_HEADS), axis=(1, 3))
    active = diag[None, :] | (top > ZERO_WEIGHT_LOG2)
    same_q = jnp.asarray(qi[:, None] == np.arange(nq)[None, :])
    first_ki = jnp.min(jnp.where(active[:, :, None] & same_q[None], ki[None, :, None], nq), axis=1)
    first = active & (jnp.asarray(ki)[None, :] == first_ki[:, qi])
    flags = first.astype(jnp.int32) + 2 * diag[None, :].astype(jnp.int32) + 4 * active.astype(jnp.int32)
    order = jnp.argsort(jnp.logical_not(active), axis=1, stable=True)
    n_active = jnp.sum(active, axis=1, keepdims=True)
    slot = jnp.arange(nsteps)[None, :]
    src = jnp.take_along_axis(order, jnp.minimum(slot, n_active - 1), axis=1)
    kept = slot < n_active
    take = lambda a: jnp.asarray(a, jnp.int32)[src].reshape(-1)
    fl = jnp.where(kept, jnp.take_along_axis(flags, src, axis=1), 0).reshape(-1)
    return take(qi), take(ki), fl.astype(jnp.int32)


def _flash(q, k, v, *, batch, seq, shared_kv, fixed_shift, window=None, base=None, q_off=None,
           u=None, z=None, gate_branch=None, v_col=0):
    nheads = NSA_HEADS
    nb = FLASH_ROWS if fixed_shift else SAFE_FLASH_ROWS
    nb = nb if batch % nb == 0 else 1
    nq = seq // TQ
    ngroups = NSA_HEADS // nheads
    nkv = NSA_KV_GROUPS
    da = k.shape[-1]
    has_delta = base is not None
    has_u = u is not None
    out_w = LANES * (nheads // 2)
    if has_delta and fixed_shift:
        tabs = _decayed_tile_tables(nq, batch, nb, base)
    else:
        tabs = _tile_tables(nq, None if window is None else window // TQ, batch // nb)
    nsteps = int(tabs[0].shape[0]) // (batch // nb)

    by_q_heads = lambda b, g, s, qt, kt, ft: (b, g, qt[b * nsteps + s], 0)
    by_k_heads = lambda b, g, s, qt, kt, ft: (b, g, kt[b * nsteps + s], 0)
    by_q_tokens = lambda col: (lambda b, g, s, qt, kt, ft: (b, qt[b * nsteps + s], g if col is None else col))
    by_k_tokens = lambda b, g, s, qt, kt, ft: (b, kt[b * nsteps + s], g + v_col)

    in_specs, args = [], []
    if has_delta:
        in_specs.append(pl.BlockSpec(memory_space=pltpu.SMEM))
        args.append(base)
    in_specs.append(pl.BlockSpec((nb, nheads, TQ, LANES), by_q_heads))
    args.append(q)
    if has_delta:
        in_specs.append(pl.BlockSpec((nb, nheads, TQ, LANES), by_q_heads))
        args.append(q_off)
    if has_u:
        in_specs.append(pl.BlockSpec((nb, nkv, TQ, LANES), by_q_heads))
        args.append(u)
    if shared_kv:
        in_specs.append(pl.BlockSpec((nb, nkv, TQ, da), by_k_heads))
        in_specs.append(pl.BlockSpec((nb, nkv, TQ, LANES), by_k_heads))
    else:
        in_specs.append(pl.BlockSpec((nb, nheads, TQ, da), by_k_heads))
        in_specs.append(pl.BlockSpec((nb, TQ, out_w), by_k_tokens))
    args += [k, v]
    if gate_branch is not None:
        in_specs.append(pl.BlockSpec((nb, TQ, LANES), by_q_tokens(C_SMALL // LANES)))
        args.append(z)

    m_rows = 8 if fixed_shift else TQ
    scratch = [pltpu.VMEM((nb * nheads, m_rows, LANES), F32),
               pltpu.VMEM((nb * nheads, TQ, LANES), F32),
               pltpu.VMEM((nb * nheads // 2, TQ, LANES), F32)]
    if has_u:
        scratch.append(pltpu.VMEM((nb * nheads, TQ, 2 * LANES), BF16))

    kern = functools.partial(_flash_kernel, nheads=nheads, shared_kv=shared_kv, window=window,
                             has_delta=has_delta, has_u=has_u, gate_branch=gate_branch, nblk=nq,
                             fixed_shift=fixed_shift, nb=nb)
    return pl.pallas_call(
        kern,
        grid_spec=pltpu.PrefetchScalarGridSpec(
            num_scalar_prefetch=3,
            grid=(batch // nb, ngroups, nsteps),
            in_specs=in_specs,
            out_specs=pl.BlockSpec((nb, TQ, out_w), by_q_tokens(None)),
            scratch_shapes=scratch),
        out_shape=jax.ShapeDtypeStruct((batch, seq, 4 * LANES), BF16),
        compiler_params=_cparams(("parallel", "parallel", "arbitrary")),
        name="flash_" + ("fox" if has_delta else ("sel" if has_u else "win")),
    )(*tabs, *args)


def _cmp_kernel(q_ref, kc_ref, vc_ref, ov_ref, gate_ref, o_ref, u_ref):
    g = pl.program_id(1)
    qi = pl.program_id(2)
    tq = q_ref.shape[2]
    ncmp = kc_ref.shape[2]
    nslc = ov_ref.shape[1]

    t = qi * tq + lax.broadcasted_iota(jnp.int32, (tq, ncmp), 0)
    n = lax.broadcasted_iota(jnp.int32, (tq, ncmp), 1)
    valid = (n * NSA_CMP_STRIDE + (NSA_CMP_LEN - 1)) <= t
    kc = kc_ref[0, 0]
    vc = vc_ref[0, 0]
    ov = ov_ref[...]
    lane = lax.broadcasted_iota(jnp.int32, (tq, LANES), 1)
    lo = lane < HEAD_DIM
    gate_blk = gate_ref[0].astype(F32)

    imp = jnp.zeros((tq, nslc), F32)
    outs = []
    for r in range(NSA_GROUP_HEADS):
        s = lax.dot_general(q_ref[0, r], kc, (((1,), (1,)), ((), ())), preferred_element_type=F32)
        s = jnp.where(valid, s, NEG)
        m = jnp.max(s, axis=1, keepdims=True)
        e = jnp.where(valid, jnp.exp2(s - m), 0.0)
        d = jnp.sum(e, axis=1, keepdims=True)
        pb = (e / jnp.where(d > 0.0, d, 1.0)).astype(BF16)
        o = jnp.dot(pb, vc, preferred_element_type=F32)
        gcol = _gate_column(gate_blk, lane, 3 * (g * NSA_GROUP_HEADS + r))
        outs.append(o * gcol)
        imp = imp + jnp.dot(pb, ov, preferred_element_type=F32)
    for p in range(NSA_GROUP_HEADS // 2):
        o_ref[0, :, p * LANES:(p + 1) * LANES] = jnp.where(lo, outs[2 * p], outs[2 * p + 1]).astype(o_ref.dtype)
    _write_unselected(u_ref, imp, qi, tq, nslc)


def _cmp_fixed_kernel(q_ref, kc_ref, vc_ref, ov_ref, gate_ref, o_ref, u_ref, d_scr, acc_scr, imp_scr,
                      *, chunk):
    g = pl.program_id(1)
    qi = pl.program_id(2)
    tq = q_ref.shape[2]
    ncmp = kc_ref.shape[2]
    nslc = ov_ref.shape[1]
    t0 = qi * tq
    d_scr[...] = jnp.zeros(d_scr.shape, F32)
    acc_scr[...] = jnp.zeros(acc_scr.shape, F32)
    imp_scr[...] = jnp.zeros(imp_scr.shape, F32)

    for c in range(ncmp // chunk):
        first_end = c * chunk * NSA_CMP_STRIDE + NSA_CMP_LEN - 1
        last_end = ((c + 1) * chunk - 1) * NSA_CMP_STRIDE + NSA_CMP_LEN - 1

        def body(masked, c=c):
            rows = slice(c * chunk, (c + 1) * chunk)
            kc = kc_ref[0, 0, rows, :]
            vo = jnp.concatenate([vc_ref[0, 0, rows, :], ov_ref[rows, :]], axis=1)
            if masked:
                t = t0 + lax.broadcasted_iota(jnp.int32, (tq, chunk), 0)
                n = c * chunk + lax.broadcasted_iota(jnp.int32, (tq, chunk), 1)
                valid = (n * NSA_CMP_STRIDE + (NSA_CMP_LEN - 1)) <= t
            for r in range(NSA_GROUP_HEADS):
                s = lax.dot_general(q_ref[0, r], kc, (((1,), (1,)), ((), ())), preferred_element_type=F32)
                if masked:
                    s = jnp.where(valid, s, NEG)
                e = jnp.exp2(s)
                part = e[:, :LANES]
                for j in range(1, chunk // LANES):
                    part = part + e[:, j * LANES:(j + 1) * LANES]
                d_scr[r] = d_scr[r] + part
                both = jnp.dot(e.astype(BF16), vo, preferred_element_type=F32)
                acc_scr[r] = acc_scr[r] + both[:, :LANES]
                imp_scr[r] = imp_scr[r] + both[:, LANES:]

        pl.when((first_end <= t0 + (tq - 1)) & (last_end > t0))(functools.partial(body, True))
        pl.when(last_end <= t0)(functools.partial(body, False))

    lane = lax.broadcasted_iota(jnp.int32, (tq, LANES), 1)
    lo = lane < HEAD_DIM
    gate_blk = gate_ref[0].astype(F32)
    imp = jnp.zeros((tq, nslc), F32)
    outs = []
    for r in range(NSA_GROUP_HEADS):
        d = jnp.sum(d_scr[r], axis=1, keepdims=True)
        inv = 1.0 / jnp.where(d > 0.0, d, 1.0)
        gcol = _gate_column(gate_blk, lane, 3 * (g * NSA_GROUP_HEADS + r))
        outs.append(acc_scr[r] * (inv * gcol))
        imp = imp + imp_scr[r] * inv
    for p in range(NSA_GROUP_HEADS // 2):
        o_ref[0, :, p * LANES:(p + 1) * LANES] = jnp.where(lo, outs[2 * p], outs[2 * p + 1]).astype(o_ref.dtype)
    _write_unselected(u_ref, imp, qi, tq, nslc)


N_FORCED = 3


def _write_unselected(u_ref, imp, qi, tq, nslc):
    tpos = qi * tq + lax.broadcasted_iota(jnp.int32, (tq, nslc), 0)
    jblk = lax.broadcasted_iota(jnp.int32, (tq, nslc), 1)
    cur = jnp.right_shift(tpos, SEL_SHIFT)
    forced = (jblk == 0) | (jblk == cur) | (jblk == cur - 1)
    visible = jblk <= cur
    score_t = jnp.where(visible & jnp.logical_not(forced), imp, -jnp.inf).T

    def run(nrows):
        sc = score_t[:nrows]
        jt = lax.broadcasted_iota(jnp.int32, (nrows, tq), 0).astype(F32)
        for _ in range(NSA_SEL_TOPK - N_FORCED):
            mx = jnp.max(sc, axis=0, keepdims=True)
            first = jnp.min(jnp.where(sc == mx, jt, float(nslc)), axis=0, keepdims=True)
            sc = jnp.where(jt == first, -jnp.inf, sc)
        taken = jnp.where(sc == -jnp.inf, 1.0, 0.0)
        if nrows < nslc:
            taken = jnp.concatenate([taken, jnp.zeros((nslc - nrows, tq), F32)], axis=0)
        chosen = visible & (forced | (taken.T > 0.5))
        u_ref[0, 0] = jnp.where(chosen, 0.0, 1.0).astype(BF16)

    half = nslc // 2
    last_block = jnp.right_shift(qi * tq + (tq - 1), SEL_SHIFT)
    pl.when(last_block < half)(functools.partial(run, half))
    pl.when(last_block >= half)(functools.partial(run, nslc))


def _cmp_select(qn, kc, vc, ov, z, *, batch, seq, fixed_shift):
    ncmp = kc.shape[2]
    nslc = ov.shape[1]
    if fixed_shift:
        chunk = 2 * LANES if ncmp % (2 * LANES) == 0 else ncmp
        kern = functools.partial(_cmp_fixed_kernel, chunk=chunk)
        scratch = [pltpu.VMEM((NSA_GROUP_HEADS, CMP_TQ, LANES), F32) for _ in range(3)]
    else:
        kern, scratch = _cmp_kernel, []
    return pl.pallas_call(
        kern,
        scratch_shapes=scratch,
        grid=(batch, NSA_KV_GROUPS, seq // CMP_TQ),
        in_specs=[pl.BlockSpec((1, NSA_GROUP_HEADS, CMP_TQ, LANES), lambda b, g, i: (b, g, i, 0)),
                  pl.BlockSpec((1, 1, ncmp, LANES), lambda b, g, i: (b, g, 0, 0)),
                  pl.BlockSpec((1, 1, ncmp, LANES), lambda b, g, i: (b, g, 0, 0)),
                  pl.BlockSpec((ncmp, nslc), lambda b, g, i: (0, 0)),
                  pl.BlockSpec((1, CMP_TQ, LANES), lambda b, g, i: (b, i, C_SMALL // LANES))],
        out_specs=[pl.BlockSpec((1, CMP_TQ, 2 * LANES), lambda b, g, i: (b, i, g)),
                   pl.BlockSpec((1, 1, CMP_TQ, nslc), lambda b, g, i: (b, g, i, 0))],
        out_shape=[jax.ShapeDtypeStruct((batch, seq, NSA_WIDTH), BF16),
                   jax.ShapeDtypeStruct((batch, NSA_KV_GROUPS, seq, nslc), BF16)],
        compiler_params=_cparams(("parallel", "parallel", "parallel")),
        name="cmp_select",
    )(qn, kc, vc, ov, z)


def _head_rms(x, g):
    return x * lax.rsqrt(jnp.mean(x * x, axis=-1, keepdims=True) + EPS) * g


def _mem_kernel(zq_ref, kv_ref, qg_ref, kg_ref, o_ref, kn_scr, vb_scr):
    @pl.when(pl.program_id(1) == 0)
    def _prep():
        for h in range(MEM_HEADS):
            kh = kv_ref[0, :, h * MEM_HEAD_DIM:(h + 1) * MEM_HEAD_DIM]
            kn_scr[h] = _head_rms(kh, kg_ref[...]).astype(BF16)
            vb_scr[h] = kv_ref[0, :, MEM_WIDTH + h * MEM_HEAD_DIM:
                               MEM_WIDTH + (h + 1) * MEM_HEAD_DIM].astype(BF16)

    for h in range(MEM_HEADS):
        sl = slice(h * MEM_HEAD_DIM, (h + 1) * MEM_HEAD_DIM)
        qh = (_head_rms(zq_ref[0, :, sl].astype(F32), qg_ref[...])
              * ((MEM_HEAD_DIM ** -0.5) * LOG2E)).astype(BF16)
        s = lax.dot_general(qh, kn_scr[h], (((1,), (1,)), ((), ())), preferred_element_type=F32)
        m = jnp.max(s, axis=1, keepdims=True)
        e = jnp.exp2(s - m)
        p = (e / jnp.sum(e, axis=1, keepdims=True)).astype(BF16)
        o_ref[0, :, sl] = jnp.dot(p, vb_scr[h], preferred_element_type=F32).astype(o_ref.dtype)


def _mem_attention(z, mem_kv, q_g, k_g, *, batch, seq):
    mlen = mem_kv.shape[1]
    tq = TQ
    return pl.pallas_call(
        _mem_kernel,
        grid=(batch, seq // tq),
        in_specs=[pl.BlockSpec((1, tq, MEM_WIDTH), lambda b, i: (b, i, C_MEM_Q // MEM_WIDTH)),
                  pl.BlockSpec((1, mlen, 2 * MEM_WIDTH), lambda b, i: (b, 0, 0)),
                  pl.BlockSpec((1, MEM_HEAD_DIM), lambda b, i: (0, 0)),
                  pl.BlockSpec((1, MEM_HEAD_DIM), lambda b, i: (0, 0))],
        out_specs=pl.BlockSpec((1, tq, MEM_WIDTH), lambda b, i: (b, i, 0)),
        out_shape=jax.ShapeDtypeStruct((batch, seq, MEM_WIDTH), BF16),
        scratch_shapes=[pltpu.VMEM((MEM_HEADS, mlen, MEM_HEAD_DIM), BF16),
                        pltpu.VMEM((MEM_HEADS, mlen, MEM_HEAD_DIM), BF16)],
        compiler_params=_cparams(("parallel", "arbitrary")),
        name="mem_attention",
    )(z, mem_kv, q_g.reshape(1, MEM_HEAD_DIM), k_g.reshape(1, MEM_HEAD_DIM))


def _sigmoid(x):
    return 0.5 * jnp.tanh(0.5 * x) + 0.5


def _out_kernel(x_ref, oc_ref, os_ref, ow_ref, ob_ref, om_ref, sa_ref, sb_ref, sm_ref,
                g0_ref, g1_ref, g2_ref, wa_ref, wb_ref, wm_ref, wo_ref, y_ref):
    silu = lambda ref: (lambda x: x * _sigmoid(x))(ref[...].astype(F32))
    oa = (oc_ref[...].astype(F32) + os_ref[...].astype(F32) + ow_ref[...].astype(F32)) * silu(sa_ref)
    ob = ob_ref[...].astype(F32) * silu(sb_ref)
    om = om_ref[...].astype(F32) * silu(sm_ref)
    gate = lambda ref: _sigmoid(ref[...].astype(F32))
    u = (gate(g0_ref) * jnp.dot(oa.astype(BF16), wa_ref[...], preferred_element_type=F32)
         + gate(g1_ref) * jnp.dot(ob.astype(BF16), wb_ref[...], preferred_element_type=F32)
         + gate(g2_ref) * jnp.dot(om.astype(BF16), wm_ref[...], preferred_element_type=F32))
    y_ref[...] = x_ref[...] + jnp.dot(u.astype(BF16), wo_ref[...], preferred_element_type=F32)


def _out_proj(x2d, o_c, o_s, o_w, o_b, o_m, z2d, wa, wb, wm, wo, layer):
    m = x2d.shape[0]
    tm = 256
    w512 = 512
    row512 = lambda c: pl.BlockSpec((tm, w512), lambda i, c=c: (i, c))
    row1024 = lambda c: pl.BlockSpec((tm, D_MODEL), lambda i, c=c: (i, c))
    full = lambda shape: pl.BlockSpec((None,) + shape, lambda i: (layer, 0, 0))
    return pl.pallas_call(
        _out_kernel,
        grid=(m // tm,),
        in_specs=[row1024(0), row512(0), row512(0), row512(0), row512(0), row512(0),
                  row512(C_NSA_SILU // w512), row512(C_FOX_SILU // w512), row512(C_MEM_SILU // w512),
                  row1024(C_MERGE // D_MODEL), row1024(C_MERGE // D_MODEL + 1),
                  row1024(C_MERGE // D_MODEL + 2),
                  full((w512, D_MODEL)), full((w512, D_MODEL)), full((w512, D_MODEL)),
                  full((D_MODEL, D_MODEL))],
        out_specs=row1024(0),
        out_shape=jax.ShapeDtypeStruct((m, D_MODEL), F32),
        compiler_params=_cparams(("parallel",)),
        name="out_proj",
    )(x2d, o_c, o_s, o_w, o_b, o_m, z2d, z2d, z2d, z2d, z2d, z2d, wa, wb, wm, wo)


def _permute_w_in(w_in):
    o = np.cumsum([0, NSA_WIDTH, 6 * NSA_KV_GROUPS * HEAD_DIM, 3 * NSA_HEADS, NSA_WIDTH,
                   3 * FOX_WIDTH, FOX_HEADS, FOX_WIDTH, MEM_WIDTH, MEM_WIDTH, 3 * D_MODEL])
    nsa_q, nsa_kv, nsa_gate, nsa_silu, fox_qkv, fox_f, fox_silu, mem_q, mem_silu, merge = [
        w_in[..., o[i]:o[i + 1]] for i in range(10)]
    pad = jnp.zeros(w_in.shape[:2] + (LANES - 3 * NSA_HEADS - FOX_HEADS,), w_in.dtype)
    return jnp.concatenate([nsa_q, nsa_silu, fox_qkv, fox_silu, mem_q, mem_silu, merge, nsa_kv,
                            nsa_gate, fox_f, pad], axis=-1).astype(BF16)


def _pad_lanes(x, width=LANES):
    return jnp.pad(x, [(0, 0)] * (x.ndim - 1) + [(0, width - x.shape[-1])])


def _tile2(g):
    return jnp.concatenate([g, g], axis=-1)[:, None, :]


def _overlap(n_cmp_pad, n_slc):
    cs = np.arange(n_cmp_pad)[:, None] * NSA_CMP_STRIDE
    ss = np.arange(NSLC_PAD)[None, :] * NSA_SEL_LEN
    ov = np.clip(np.minimum(cs + NSA_CMP_LEN, ss + NSA_SEL_LEN) - np.maximum(cs, ss), 0, None)
    ov = ov / NSA_CMP_LEN
    ov[-1] = 0.0
    ov[:, n_slc:] = 0.0
    return jnp.asarray(ov, BF16)


def _aug_permutations():
    pq = np.zeros((6 * LANES, FOX_HEADS * LANES), np.float32)
    pk = np.zeros((6 * LANES, FOX_HEADS * LANES), np.float32)
    for h in range(FOX_HEADS):
        for i in range(3):
            pq[i * LANES + F_LANE + h, h * LANES + AUG + i] = 1.0
            pk[i * LANES + F_LANE + h, h * LANES + AUG + 3 + i] = -1.0
            pk[(3 + i) * LANES + F_LANE + h, h * LANES + AUG + 6 + i] = -1.0
    return jnp.asarray(pq, BF16), jnp.asarray(pk, BF16)


def _compress_w1(w1):
    nl = w1.shape[0]
    halves = w1.reshape(nl, 2, NSA_CMP_STRIDE, 1, HEAD_DIM, NSA_CMP_HIDDEN)
    eye = jnp.eye(NSA_KV_GROUPS, dtype=w1.dtype)
    out = jnp.einsum("pg,zhldn->zlpdghn", eye, halves[:, :, :, 0])
    return out.reshape(nl, NSA_CMP_STRIDE * NSA_KV_GROUPS * HEAD_DIM,
                       NSA_KV_GROUPS * 2 * NSA_CMP_HIDDEN).astype(BF16)


def _layer(x, mem, p, big, layer, consts):
    batch, seq, _ = x.shape
    m_rows = batch * seq
    cos4, sin_signed, tri, head_ones, perm_q, perm_k, ov = consts
    z2d = _norm_matmul(x.reshape(m_rows, D_MODEL), p["norm_g"], big["w_in"], layer, 512, Z_WIDTH // 3, BF16)
    z = z2d.reshape(batch, seq, Z_WIDTH)

    (qn, kcr, vcr, ks, vs, kw, vw, fqd, fqo, fka, base) = _prep(
        z, cos4, sin_signed, tri, head_ones, perm_q, perm_k,
        p["gq"], p["gks"], p["gkw"], p["gfq"], p["gfk"], p["fb"],
        p["s_sel"], p["s_win"], p["s_fox"], batch=batch, seq=seq)

    nchunk = seq // NSA_CMP_STRIDE
    kc, vc = _compress(kcr.reshape(batch, nchunk, NSA_CMP_STRIDE * LANES),
                       vcr.reshape(batch, nchunk, NSA_CMP_STRIDE * LANES),
                       p["w1k"], p["w1v"], p["bk"], p["bv"], p["w2k"], p["w2v"], p["gkc"], p["s_cmp"],
                       batch=batch)

    def attend(fixed_shift):
        o_c, unsel = _cmp_select(qn, kc, vc, ov, z, batch=batch, seq=seq, fixed_shift=fixed_shift)
        o_s = _flash(qn, ks, vs, batch=batch, seq=seq, shared_kv=True, fixed_shift=fixed_shift,
                     u=unsel, z=z, gate_branch=1)
        o_w = _flash(qn, kw, vw, batch=batch, seq=seq, shared_kv=True, fixed_shift=fixed_shift,
                     window=NSA_WINDOW, z=z, gate_branch=2)
        o_b = _flash(fqd, fka, z, batch=batch, seq=seq, shared_kv=False, fixed_shift=fixed_shift,
                     base=base.reshape(-1), q_off=fqo, v_col=C_FOX_V // FOX_WIDTH)
        return o_c, o_s, o_w, o_b

    o_c, o_s, o_w, o_b = lax.cond(p["bound_ok"], lambda: attend(True), lambda: attend(False))

    mlen = mem.shape[1]
    mem_kv = _norm_matmul(mem.reshape(batch * mlen, D_MODEL), p["mem_norm_g"], big["w_mem_kv"], layer,
                          min(512, batch * mlen), 2 * MEM_WIDTH, F32).reshape(batch, mlen, 2 * MEM_WIDTH)
    o_m = _mem_attention(z, mem_kv, p["mem_q_norm"], p["mem_k_norm"], batch=batch, seq=seq)

    y = _out_proj(x.reshape(m_rows, D_MODEL), o_c.reshape(m_rows, -1), o_s.reshape(m_rows, -1),
                  o_w.reshape(m_rows, -1), o_b.reshape(m_rows, -1), o_m.reshape(m_rows, -1), z2d,
                  big["w_branch_a"], big["w_branch_b"], big["w_branch_m"], big["w_out"], layer)
    return y.reshape(batch, seq, D_MODEL)


def kernel(x, mem, norm_g, mem_norm_g, w_in, nsa_q_norm, nsa_k_norm, cmp_pe_k, cmp_w1_k, cmp_w2_k,
           cmp_pe_v, cmp_w1_v, cmp_w2_v, fox_q_norm, fox_k_norm, fox_f_bias, mem_q_norm, mem_k_norm,
           w_mem_kv, w_branch_a, w_branch_b, w_branch_m, w_out):
    batch, seq, _ = x.shape
    depth = w_in.shape[0]
    n_slc = seq // NSA_SEL_LEN
    assert n_slc <= NSLC_PAD and seq % TQ == 0 and NSA_WINDOW == TQ

    half = HEAD_DIM // 2
    inv_freq = ROPE_THETA ** (-jnp.arange(half, dtype=F32) / half)
    ang = jnp.arange(seq).astype(F32)[:, None] * inv_freq[None, :]
    cos, sin = jnp.cos(ang), jnp.sin(ang)
    cos4 = jnp.concatenate([cos, cos, cos, cos], axis=-1)
    sin_signed = jnp.concatenate([-sin, sin, -sin, sin], axis=-1)
    tri = jnp.asarray(np.tril(np.ones((TQ, TQ), np.float32)), BF16)
    head_ones = jnp.asarray(np.kron(np.eye(2), np.ones((HEAD_DIM, HEAD_DIM))), BF16)
    consts = (cos4, sin_signed, tri, head_ones, *_aug_permutations(),
              _overlap(seq // NSA_CMP_STRIDE, n_slc))

    hp = lax.Precision.HIGHEST
    pe_bias = lambda pe, w1: jnp.einsum("lk,lkn->ln", pe.reshape(depth, -1), w1, precision=hp)[:, None, :]
    fb = jnp.zeros((depth, 1, LANES), F32).at[:, 0, F_LANE:F_LANE + FOX_HEADS].set(fox_f_bias)

    def logit_bound(gq, gk):
        bound = (BOUND_MARGIN * HEAD_DIM * QK_SCALE
                 * jnp.max(jnp.abs(gq), axis=-1) * jnp.max(jnp.abs(gk), axis=-1))
        return bound.astype(BF16).astype(F32)

    bounds = [logit_bound(nsa_q_norm, nsa_k_norm[:, 1]), logit_bound(nsa_q_norm, nsa_k_norm[:, 2]),
              logit_bound(fox_q_norm, fox_k_norm), logit_bound(nsa_q_norm, nsa_k_norm[:, 0])]
    bound_ok = functools.reduce(jnp.maximum, bounds) <= MAX_FIXED_BOUND
    bound_row = lambda s: jnp.broadcast_to(s[:, None, None], (depth, 1, LANES))
    stacked = dict(
        s_sel=bound_row(bounds[0]), s_win=bound_row(bounds[1]), s_fox=bound_row(bounds[2]),
        s_cmp=bound_row(bounds[3]),
        bound_ok=bound_ok,
        norm_g=norm_g, mem_norm_g=mem_norm_g,
        gq=_tile2(nsa_q_norm), gks=_tile2(nsa_k_norm[:, 1]), gkw=_tile2(nsa_k_norm[:, 2]),
        gkc=_pad_lanes(nsa_k_norm[:, 0])[:, None, :], gfq=_tile2(fox_q_norm), gfk=_tile2(fox_k_norm),
        fb=fb, w1k=_compress_w1(cmp_w1_k), w1v=_compress_w1(cmp_w1_v),
        bk=pe_bias(cmp_pe_k, cmp_w1_k), bv=pe_bias(cmp_pe_v, cmp_w1_v),
        w2k=_pad_lanes(cmp_w2_k).astype(BF16),
        w2v=jnp.concatenate([cmp_w2_v, cmp_w2_v], axis=-1).astype(BF16),
        mem_q_norm=mem_q_norm, mem_k_norm=mem_k_norm)
    big = dict(w_in=_permute_w_in(w_in), w_mem_kv=w_mem_kv.astype(BF16),
               w_branch_a=w_branch_a.astype(BF16), w_branch_b=w_branch_b.astype(BF16),
               w_branch_m=w_branch_m.astype(BF16), w_out=w_out.astype(BF16))
    for l in range(depth):
        x = _layer(x, mem, {k: v[l] for k, v in stacked.items()}, big, l, consts)
    return x
```

```python
import functools

import numpy as np
import jax
import jax.numpy as jnp
from jax import lax
from jax.experimental import pallas as pl
from jax.experimental.pallas import tpu as pltpu

F32 = jnp.float32
BF16 = jnp.bfloat16

D_MODEL = 1024
HEAD_DIM = 64
ROPE_THETA = 10000.0
EPS = 1e-6
NSA_HEADS = 8
NSA_KV_GROUPS = 2
NSA_GROUP_HEADS = NSA_HEADS // NSA_KV_GROUPS
NSA_CMP_LEN = 32
NSA_CMP_STRIDE = 16
NSA_CMP_HIDDEN = 128
NSA_SEL_LEN = 64
NSA_SEL_TOPK = 16
NSA_WINDOW = 512
FORCE_SCORE = 1e9
NSA_WIDTH = NSA_HEADS * HEAD_DIM
FOX_HEADS = 8
FOX_WIDTH = FOX_HEADS * HEAD_DIM
MEM_HEADS = 4
MEM_HEAD_DIM = 128
MEM_WIDTH = MEM_HEADS * MEM_HEAD_DIM

LANES = 128
VMEM_LIMIT = 56 * 1024 * 1024
NEG = -1e30
SEL_MASK = 32768.0
SEL_SHIFT = 6
NSLC_PAD = LANES
LOG2E = 1.4426950408889634
QK_SCALE = (HEAD_DIM ** -0.5) * LOG2E
BOUND_MARGIN = 1.02
MAX_FIXED_BOUND = 60.0
ZERO_WEIGHT_LOG2 = -160.0

C_NSA_Q = 0
C_NSA_SILU = 512
C_FOX_Q = 1024
C_FOX_K = 1536
C_FOX_V = 2048
C_FOX_SILU = 2560
C_MEM_Q = 3072
C_MEM_SILU = 3584
C_MERGE = 4096
C_NSA_KV = 7168
C_SMALL = 7936
Z_WIDTH = 8064
F_LANE = 3 * NSA_HEADS
AUG = HEAD_DIM

TQ = 512
FLASH_ROWS = 4
SAFE_FLASH_ROWS = 2
CMP_TQ = 512


def _cparams(sem):
    return pltpu.CompilerParams(dimension_semantics=sem, vmem_limit_bytes=VMEM_LIMIT)


def _norm_matmul_kernel(x_ref, g_ref, w_ref, o_ref, *, tn):
    x = x_ref[...]
    ms = jnp.mean(x * x, axis=-1, keepdims=True)
    h = (x * lax.rsqrt(ms + EPS) * g_ref[...]).astype(BF16)
    for c0 in range(0, w_ref.shape[1], tn):
        o_ref[:, c0:c0 + tn] = jnp.dot(h, w_ref[:, c0:c0 + tn],
                                       preferred_element_type=F32).astype(o_ref.dtype)


def _norm_matmul(x2d, g, w, layer, tm, tn, out_dtype):
    m, k = x2d.shape
    n = w.shape[2]
    return pl.pallas_call(
        functools.partial(_norm_matmul_kernel, tn=tn),
        grid=(m // tm,),
        in_specs=[pl.BlockSpec((tm, k), lambda i: (i, 0)),
                  pl.BlockSpec((1, k), lambda i: (0, 0)),
                  pl.BlockSpec((None, k, n), lambda i: (layer, 0, 0), pipeline_mode=pl.Buffered(1))],
        out_specs=pl.BlockSpec((tm, n), lambda i: (i, 0)),
        out_shape=jax.ShapeDtypeStruct((m, n), out_dtype),
        compiler_params=_cparams(("parallel",)),
        name="norm_matmul",
    )(x2d, g.reshape(1, k), w)


def _pair_rms(x, g2, head_ones):
    sq = x * x
    hi = sq.astype(BF16)
    low = (sq - hi.astype(F32)).astype(BF16)
    ssq = (jnp.dot(hi, head_ones, preferred_element_type=F32)
           + jnp.dot(low, head_ones, preferred_element_type=F32))
    return x * lax.rsqrt(ssq * (1.0 / HEAD_DIM) + EPS) * g2


def _pair_rope(y, cos4, sin_signed, first_half):
    rot = jnp.where(first_half, pltpu.roll(y, LANES - HEAD_DIM // 2, 1), pltpu.roll(y, HEAD_DIM // 2, 1))
    return y * cos4 + rot * sin_signed


def _head_a(y, lo, tail=0.0):
    return jnp.where(lo, y, tail)


def _head_b(y, lo, tail=0.0):
    return jnp.where(lo, pltpu.roll(y, HEAD_DIM, 1), tail)


def _split3_f32(c):
    hi = c.astype(BF16).astype(F32)
    r = c - hi
    mid = r.astype(BF16).astype(F32)
    return hi, mid, r - mid


def _prep_kernel(zq_ref, zc_ref, zs_ref, zw_ref, fq_ref, fk_ref, sm_ref, cos_ref, sin_ref,
                 tri_ref, ones_ref, pq_ref, pk_ref, gq_ref, gks_ref, gkw_ref, gfq_ref, gfk_ref, fb_ref,
                 ss_ref, sw_ref, sf_ref,
                 qn_ref, kcr_ref, vcr_ref, ks_ref, vs_ref, kw_ref, vw_ref, fqd_ref, fqo_ref, fka_ref,
                 base_ref, run_scr):
    ti = pl.program_id(1)
    tt = zq_ref.shape[1]
    lane = lax.broadcasted_iota(jnp.int32, (tt, LANES), 1)
    lo = lane < HEAD_DIM
    first_half = (lane & (HEAD_DIM // 2)) == 0
    cos4 = cos_ref[...]
    sin_signed = sin_ref[...]
    head_ones = ones_ref[...]
    one_at_aug = jnp.where(lane == AUG, 1.0, 0.0)

    for p in range(NSA_HEADS // 2):
        y = _pair_rope(_pair_rms(zq_ref[0, :, p * LANES:(p + 1) * LANES].astype(F32), gq_ref[...], head_ones),
                       cos4, sin_signed, first_half) * QK_SCALE
        qn_ref[0, 2 * p] = _head_a(y, lo, one_at_aug).astype(BF16)
        qn_ref[0, 2 * p + 1] = _head_b(y, lo, one_at_aug).astype(BF16)

    kcr_ref[0] = _pair_rope(zc_ref[0, :, :LANES].astype(F32), cos4, sin_signed, first_half).astype(BF16)
    vcr_ref[0] = zc_ref[0, :, LANES:].astype(BF16)

    sel_lane = jnp.right_shift(ti * tt + lax.broadcasted_iota(jnp.int32, (tt, LANES), 0), SEL_SHIFT)
    onehot = jnp.where(lane == sel_lane, -SEL_MASK, 0.0).astype(BF16)
    for z_ref, g_ref, k_ref, v_ref, bound_ref, with_onehot in (
            (zs_ref, gks_ref, ks_ref, vs_ref, ss_ref, True),
            (zw_ref, gkw_ref, kw_ref, vw_ref, sw_ref, False)):
        y = _pair_rope(_pair_rms(z_ref[0, :, :LANES].astype(F32), g_ref[...], head_ones),
                       cos4, sin_signed, first_half)
        v = z_ref[0, :, LANES:].astype(F32)
        vr = pltpu.roll(v, HEAD_DIM, 1)
        neg_bound = jnp.where(lane == AUG, -bound_ref[...], 0.0)
        k_ref[0, 0, :, :LANES] = _head_a(y, lo, neg_bound).astype(BF16)
        k_ref[0, 1, :, :LANES] = _head_b(y, lo, neg_bound).astype(BF16)
        if with_onehot:
            k_ref[0, 0, :, LANES:] = onehot
            k_ref[0, 1, :, LANES:] = onehot
        v_ref[0, 0] = jnp.where(lo, v, vr).astype(BF16)
        v_ref[0, 1] = jnp.where(lo, vr, v).astype(BF16)

    @pl.when(ti == 0)
    def _reset():
        run_scr[...] = jnp.zeros(run_scr.shape, F32)

    xf = sm_ref[0].astype(F32) + fb_ref[...]
    log_f = (jnp.minimum(xf, 0.0) - jnp.log1p(jnp.exp(-jnp.abs(xf)))) * LOG2E
    tri = tri_ref[...]
    parts = _split3_f32(log_f)
    cum = sum(jnp.dot(tri, part.astype(BF16), preferred_element_type=F32) for part in parts)
    base_ref[0, 0] = run_scr[...]
    run_scr[...] = run_scr[...] + cum[tt - 1:tt, :]

    parts6 = _split3_f32(cum) + _split3_f32(cum - cum[tt - 1:tt, :])
    x6 = jnp.concatenate([part.astype(BF16) for part in parts6], axis=1)
    aug_q = jnp.dot(x6, pq_ref[...], preferred_element_type=F32)
    aug_k = jnp.dot(x6, pk_ref[...], preferred_element_type=F32)

    lane1 = lax.broadcasted_iota(jnp.int32, (1, LANES), 1)
    span = lambda a, b: jnp.where((lane1 >= a) & (lane1 < b), 1.0, 0.0)
    q_tail_diag = span(AUG + 3, AUG + 6) + span(AUG + 9, AUG + 10)
    q_tail_off = span(AUG + 6, AUG + 10)
    k_tail = span(AUG, AUG + 3) + jnp.where(lane1 == AUG + 9, -sf_ref[...], 0.0)
    for p in range(FOX_HEADS // 2):
        sl = slice(p * LANES, (p + 1) * LANES)
        yq = _pair_rms(fq_ref[0, :, sl].astype(F32), gfq_ref[...], head_ones) * QK_SCALE
        yk = _pair_rms(fk_ref[0, :, sl].astype(F32), gfk_ref[...], head_ones)
        for e, split in ((0, _head_a), (1, _head_b)):
            h = 2 * p + e
            hs = slice(h * LANES, (h + 1) * LANES)
            qh = split(yq, lo) + aug_q[:, hs]
            fqd_ref[0, h] = (qh + q_tail_diag).astype(BF16)
            fqo_ref[0, h] = (qh + q_tail_off).astype(BF16)
            fka_ref[0, h] = (split(yk, lo) + aug_k[:, hs] + k_tail).astype(BF16)


def _prep(z, cos4, sin_signed, tri, head_ones, perm_q, perm_k, gq, gks, gkw, gfq, gfk, fb,
          s_sel, s_win, s_fox, *, batch, seq):
    nblk = seq // TQ
    whole = lambda a: pl.BlockSpec(a.shape, lambda b, i: (0, 0))
    zspec = lambda w, c: pl.BlockSpec((1, TQ, w), lambda b, i, c=c: (b, i, c))
    row = pl.BlockSpec((1, LANES), lambda b, i: (0, 0))
    tab = pl.BlockSpec((TQ, LANES), lambda b, i: (i, 0))
    heads = lambda n, w: pl.BlockSpec((1, n, TQ, w), lambda b, i: (b, 0, i, 0))
    tok = lambda w: pl.BlockSpec((1, TQ, w), lambda b, i: (b, i, 0))
    bf = lambda *shape: jax.ShapeDtypeStruct(shape, BF16)
    return pl.pallas_call(
        _prep_kernel,
        grid=(batch, nblk),
        in_specs=[zspec(NSA_WIDTH, C_NSA_Q // NSA_WIDTH),
                  zspec(2 * LANES, C_NSA_KV // (2 * LANES)),
                  zspec(2 * LANES, C_NSA_KV // (2 * LANES) + 1),
                  zspec(2 * LANES, C_NSA_KV // (2 * LANES) + 2),
                  zspec(FOX_WIDTH, C_FOX_Q // FOX_WIDTH),
                  zspec(FOX_WIDTH, C_FOX_K // FOX_WIDTH),
                  zspec(LANES, C_SMALL // LANES),
                  tab, tab, whole(tri), whole(head_ones), whole(perm_q), whole(perm_k),
                  row, row, row, row, row, row, row, row, row],
        out_specs=[heads(NSA_HEADS, LANES), tok(LANES), tok(LANES),
                   heads(NSA_KV_GROUPS, 2 * LANES), heads(NSA_KV_GROUPS, LANES),
                   heads(NSA_KV_GROUPS, LANES), heads(NSA_KV_GROUPS, LANES),
                   heads(FOX_HEADS, LANES), heads(FOX_HEADS, LANES), heads(FOX_HEADS, LANES),
                   pl.BlockSpec((1, 1, 1, LANES), lambda b, i: (b, i, 0, 0))],
        out_shape=[bf(batch, NSA_HEADS, seq, LANES), bf(batch, seq, LANES), bf(batch, seq, LANES),
                   bf(batch, NSA_KV_GROUPS, seq, 2 * LANES), bf(batch, NSA_KV_GROUPS, seq, LANES),
                   bf(batch, NSA_KV_GROUPS, seq, LANES), bf(batch, NSA_KV_GROUPS, seq, LANES),
                   bf(batch, FOX_HEADS, seq, LANES), bf(batch, FOX_HEADS, seq, LANES),
                   bf(batch, FOX_HEADS, seq, LANES),
                   jax.ShapeDtypeStruct((batch, nblk, 1, LANES), F32)],
        scratch_shapes=[pltpu.VMEM((1, LANES), F32)],
        compiler_params=_cparams(("parallel", "arbitrary")),
        name="prep",
    )(z, z, z, z, z, z, z, cos4, sin_signed, tri, head_ones, perm_q, perm_k,
      gq, gks, gkw, gfq, gfk, fb, s_sel, s_win, s_fox)


def _compress_kernel(xk_ref, xv_ref, w1k_ref, w1v_ref, bk_ref, bv_ref, w2k_ref, w2v_ref, kg_ref,
                     sc_ref, kc_ref, vc_ref):
    nchunk = xk_ref.shape[1]
    lane = lax.broadcasted_iota(jnp.int32, (nchunk, LANES), 1)
    for x_ref, w1_ref, b_ref, w2_ref, o_ref, is_key in ((xk_ref, w1k_ref, bk_ref, w2k_ref, kc_ref, True),
                                                        (xv_ref, w1v_ref, bv_ref, w2v_ref, vc_ref, False)):
        h = jnp.dot(x_ref[0], w1_ref[...], preferred_element_type=F32)
        for g in range(NSA_KV_GROUPS):
            c0 = g * 2 * NSA_CMP_HIDDEN
            top = h[:, c0:c0 + NSA_CMP_HIDDEN]
            bot = h[:, c0 + NSA_CMP_HIDDEN:c0 + 2 * NSA_CMP_HIDDEN]
            hid = top + pltpu.roll(bot, nchunk - 1, 0) + b_ref[...]
            act = (hid * jax.nn.sigmoid(hid)).astype(BF16)
            o = jnp.dot(act, w2_ref[...], preferred_element_type=F32)
            if is_key:
                ms = jnp.sum(o * o, axis=1, keepdims=True) * (1.0 / HEAD_DIM)
                o = o * lax.rsqrt(ms + EPS) * kg_ref[...]
                o = jnp.where(lane == AUG, -sc_ref[...], o)
            o_ref[0, g] = o.astype(BF16)


def _compress(xk, xv, w1k, w1v, bk, bv, w2k, w2v, kg, s_cmp, *, batch):
    nchunk, kin = xk.shape[1], xk.shape[2]
    xspec = pl.BlockSpec((1, nchunk, kin), lambda b: (b, 0, 0))
    full = lambda a: pl.BlockSpec(a.shape, lambda b: (0,) * a.ndim)
    ospec = pl.BlockSpec((1, NSA_KV_GROUPS, nchunk, LANES), lambda b: (b, 0, 0, 0))
    oshape = jax.ShapeDtypeStruct((batch, NSA_KV_GROUPS, nchunk, LANES), BF16)
    return pl.pallas_call(
        _compress_kernel,
        grid=(batch,),
        in_specs=[xspec, xspec, full(w1k), full(w1v), full(bk), full(bv), full(w2k), full(w2v), full(kg),
                  full(s_cmp)],
        out_specs=[ospec, ospec],
        out_shape=[oshape, oshape],
        compiler_params=_cparams(("parallel",)),
        name="compress",
    )(xk, xv, w1k, w1v, bk, bv, w2k, w2v, kg, s_cmp)


def _gate_column(gate_blk, lane, idx):
    col = jnp.sum(jnp.where(lane == idx, gate_blk, 0.0), axis=1, keepdims=True)
    return jax.nn.sigmoid(col)


def _flash_kernel(qi_tab, ki_tab, fl_tab, *refs, nheads, shared_kv, window, has_delta,
                  has_u, gate_branch, nblk, fixed_shift, nb):
    refs = list(refs)
    base_ref = refs.pop(0) if has_delta else None
    q_ref = refs.pop(0)
    qoff_ref = refs.pop(0) if has_delta else None
    u_ref = refs.pop(0) if has_u else None
    k_ref = refs.pop(0)
    v_ref = refs.pop(0)
    gate_ref = refs.pop(0) if gate_branch is not None else None
    o_ref = refs.pop(0)
    m_scr = refs.pop(0)
    l_scr = refs.pop(0)
    acc_scr = refs.pop(0)
    qcat_scr = refs.pop(0) if has_u else None

    b = pl.program_id(0)
    g = pl.program_id(1)
    step = b * pl.num_programs(2) + pl.program_id(2)
    qi = qi_tab[step]
    ki = ki_tab[step]
    fl = fl_tab[step]
    active = (fl & 4) == 4
    tq = q_ref.shape[2]
    tk = k_ref.shape[2]

    @pl.when((fl & 1) == 1)
    def _init():
        m_scr[...] = jnp.full(m_scr.shape, NEG, F32)
        l_scr[...] = jnp.zeros(l_scr.shape, F32)
        acc_scr[...] = jnp.zeros(acc_scr.shape, F32)
        if has_u:
            for bb in range(nb):
                for r in range(nheads):
                    qcat_scr[bb * nheads + r, :, :LANES] = q_ref[bb, r]
                    qcat_scr[bb * nheads + r, :, LANES:] = u_ref[bb, r // NSA_GROUP_HEADS]

    lane = lax.broadcasted_iota(jnp.int32, (tq, LANES), 1)
    lo = lane < HEAD_DIM

    def tile(diagonal):
        masked = diagonal or window is not None
        if masked:
            row = lax.broadcasted_iota(jnp.int32, (tq, tk), 0)
            col = lax.broadcasted_iota(jnp.int32, (tq, tk), 1)
            if diagonal:
                valid = row >= col
            else:
                valid = (tq + row - col) < window
        for bb in range(nb):
            slot0, pair0, brow = bb * nheads, bb * (nheads // 2), b * nb + bb
            for p in range(nheads // 2):
                pvs, alphas = [], []
                for e in range(2):
                    hh = 2 * p + e
                    if has_u:
                        q = qcat_scr[slot0 + hh]
                    elif has_delta and not diagonal:
                        q = qoff_ref[bb, hh]
                    else:
                        q = q_ref[bb, hh]
                    k = k_ref[bb, hh // NSA_GROUP_HEADS] if shared_kv else k_ref[bb, hh]
                    v = (v_ref[bb, hh // NSA_GROUP_HEADS] if shared_kv
                         else v_ref[bb, :, p * LANES:(p + 1) * LANES])
                    s = lax.dot_general(q, k, (((1,), (1,)), ((), ())), preferred_element_type=F32)
                    if masked:
                        s = jnp.where(valid, s, NEG)
                    delta = None
                    if has_delta and not diagonal:
                        hoff = F_LANE + g * nheads + hh
                        delta = (base_ref[(brow * nblk + qi) * LANES + hoff]
                                 - base_ref[(brow * nblk + ki + 1) * LANES + hoff])
                    if fixed_shift:
                        pm = jnp.exp2(s)
                        rowsum = pm[:, :LANES]
                        for j in range(1, tk // LANES):
                            rowsum = rowsum + pm[:, j * LANES:(j + 1) * LANES]
                        pv = jnp.dot(pm.astype(BF16), v, preferred_element_type=F32)
                        if delta is not None:
                            w = jnp.exp2(jnp.full((1, LANES), delta, F32))
                            rowsum = rowsum * w
                            pv = pv * w
                        l_scr[slot0 + hh] = l_scr[slot0 + hh] + rowsum
                        pvs.append(pv)
                    else:
                        m_prev = m_scr[slot0 + hh]
                        m_cur = jnp.max(s, axis=1, keepdims=True)
                        if delta is not None:
                            m_cur = m_cur + delta
                        m_new = jnp.maximum(m_prev, m_cur)
                        alpha = jnp.exp2(m_prev - m_new)
                        shift = m_new[:, :1]
                        if delta is not None:
                            shift = shift - delta
                        pm = jnp.exp2(s - shift)
                        l_scr[slot0 + hh] = alpha * l_scr[slot0 + hh] + jnp.sum(pm, axis=1, keepdims=True)
                        m_scr[slot0 + hh] = m_new
                        pvs.append(jnp.dot(pm.astype(BF16), v, preferred_element_type=F32))
                        alphas.append(alpha)
                if fixed_shift:
                    acc_scr[pair0 + p] = acc_scr[pair0 + p] + jnp.where(lo, pvs[0], pvs[1])
                else:
                    acc_scr[pair0 + p] = (acc_scr[pair0 + p] * jnp.where(lo, alphas[0], alphas[1])
                                          + jnp.where(lo, pvs[0], pvs[1]))

    pl.when((qi == ki) & active)(lambda: tile(True))
    pl.when((qi != ki) & active)(lambda: tile(False))

    @pl.when((fl & 2) == 2)
    def _finish():
        for bb in range(nb):
            slot0, pair0 = bb * nheads, bb * (nheads // 2)
            for p in range(nheads // 2):
                if fixed_shift:
                    la = jnp.sum(l_scr[slot0 + 2 * p], axis=1, keepdims=True)
                    lb = jnp.sum(l_scr[slot0 + 2 * p + 1], axis=1, keepdims=True)
                else:
                    la, lb = l_scr[slot0 + 2 * p], l_scr[slot0 + 2 * p + 1]
                inv = jnp.where(lo, 1.0 / la, 1.0 / lb)
                if gate_branch is not None:
                    h0 = g * nheads + 2 * p
                    src = lax.broadcasted_iota(jnp.int32, (LANES, LANES), 0)
                    dst_lo = lax.broadcasted_iota(jnp.int32, (LANES, LANES), 1) < HEAD_DIM
                    pick = jnp.where(src == jnp.where(dst_lo, 3 * h0 + gate_branch,
                                                      3 * (h0 + 1) + gate_branch), 1.0, 0.0)
                    logits = jnp.dot(gate_ref[bb].astype(BF16), pick.astype(BF16),
                                     preferred_element_type=F32)
                    inv = inv * jax.nn.sigmoid(logits)
                o_ref[bb, :, p * LANES:(p + 1) * LANES] = (acc_scr[pair0 + p] * inv).astype(o_ref.dtype)


def _tile_tables(nq, window_tiles, batch):
    qi, ki, fl = [], [], []
    for i in range(nq):
        lo = 0 if window_tiles is None else max(0, i - window_tiles)
        for j in range(lo, i + 1):
            qi.append(i)
            ki.append(j)
            fl.append((1 if j == lo else 0) | (2 if j == i else 0) | 4)
    rows = lambda a: jnp.tile(jnp.asarray(a, jnp.int32), batch)
    return rows(qi), rows(ki), rows(fl)


def _decayed_tile_tables(nq, batch, nb, base):
    qi = np.repeat(np.arange(nq), np.arange(1, nq + 1))
    ki = np.concatenate([np.arange(i + 1) for i in range(nq)])
    nsteps = qi.shape[0]
    diag = jnp.asarray(qi == ki)
    gate = base.reshape(batch, nq, LANES)[:, :, F_LANE:F_LANE + FOX_HEADS]
    delta = gate[:, qi, :] - gate[:, np.minimum(ki + 1, nq - 1), :]
    top = jnp.max(delta.reshape(batch // nb, nb, nsteps, FOX_HEADS), axis=(1, 3))
    active = diag[None, :] | (top > ZERO_WEIGHT_LOG2)
    same_q = jnp.asarray(qi[:, None] == np.arange(nq)[None, :])
    first_ki = jnp.min(jnp.where(active[:, :, None] & same_q[None], ki[None, :, None], nq), axis=1)
    first = active & (jnp.asarray(ki)[None, :] == first_ki[:, qi])
    flags = first.astype(jnp.int32) + 2 * diag[None, :].astype(jnp.int32) + 4 * active.astype(jnp.int32)
    order = jnp.argsort(jnp.logical_not(active), axis=1, stable=True)
    n_active = jnp.sum(active, axis=1, keepdims=True)
    slot = jnp.arange(nsteps)[None, :]
    src = jnp.take_along_axis(order, jnp.minimum(slot, n_active - 1), axis=1)
    kept = slot < n_active
    take = lambda a: jnp.asarray(a, jnp.int32)[src].reshape(-1)
    fl = jnp.where(kept, jnp.take_along_axis(flags, src, axis=1), 0).reshape(-1)
    return take(qi), take(ki), fl.astype(jnp.int32)


def _flash(q, k, v, *, batch, seq, shared_kv, fixed_shift, window=None, base=None, q_off=None,
           u=None, z=None, gate_branch=None, v_col=0):
    nheads = NSA_HEADS
    nb = FLASH_ROWS if fixed_shift else SAFE_FLASH_ROWS
    nb = nb if batch % nb == 0 else 1
    nq = seq // TQ
    ngroups = NSA_HEADS // nheads
    nkv = NSA_KV_GROUPS
    da = k.shape[-1]
    has_delta = base is not None
    has_u = u is not None
    out_w = LANES * (nheads // 2)
    if has_delta and fixed_shift:
        tabs = _decayed_tile_tables(nq, batch, nb, base)
    else:
        tabs = _tile_tables(nq, None if window is None else window // TQ, batch // nb)
    nsteps = int(tabs[0].shape[0]) // (batch // nb)

    by_q_heads = lambda b, g, s, qt, kt, ft: (b, g, qt[b * nsteps + s], 0)
    by_k_heads = lambda b, g, s, qt, kt, ft: (b, g, kt[b * nsteps + s], 0)
    by_q_tokens = lambda col: (lambda b, g, s, qt, kt, ft: (b, qt[b * nsteps + s], g if col is None else col))
    by_k_tokens = lambda b, g, s, qt, kt, ft: (b, kt[b * nsteps + s], g + v_col)

    in_specs, args = [], []
    if has_delta:
        in_specs.append(pl.BlockSpec(memory_space=pltpu.SMEM))
        args.append(base)
    in_specs.append(pl.BlockSpec((nb, nheads, TQ, LANES), by_q_heads))
    args.append(q)
    if has_delta:
        in_specs.append(pl.BlockSpec((nb, nheads, TQ, LANES), by_q_heads))
        args.append(q_off)
    if has_u:
        in_specs.append(pl.BlockSpec((nb, nkv, TQ, LANES), by_q_heads))
        args.append(u)
    if shared_kv:
        in_specs.append(pl.BlockSpec((nb, nkv, TQ, da), by_k_heads))
        in_specs.append(pl.BlockSpec((nb, nkv, TQ, LANES), by_k_heads))
    else:
        in_specs.append(pl.BlockSpec((nb, nheads, TQ, da), by_k_heads))
        in_specs.append(pl.BlockSpec((nb, TQ, out_w), by_k_tokens))
    args += [k, v]
    if gate_branch is not None:
        in_specs.append(pl.BlockSpec((nb, TQ, LANES), by_q_tokens(C_SMALL // LANES)))
        args.append(z)

    m_rows = 8 if fixed_shift else TQ
    scratch = [pltpu.VMEM((nb * nheads, m_rows, LANES), F32),
               pltpu.VMEM((nb * nheads, TQ, LANES), F32),
               pltpu.VMEM((nb * nheads // 2, TQ, LANES), F32)]
    if has_u:
        scratch.append(pltpu.VMEM((nb * nheads, TQ, 2 * LANES), BF16))

    kern = functools.partial(_flash_kernel, nheads=nheads, shared_kv=shared_kv, window=window,
                             has_delta=has_delta, has_u=has_u, gate_branch=gate_branch, nblk=nq,
                             fixed_shift=fixed_shift, nb=nb)
    return pl.pallas_call(
        kern,
        grid_spec=pltpu.PrefetchScalarGridSpec(
            num_scalar_prefetch=3,
            grid=(batch // nb, ngroups, nsteps),
            in_specs=in_specs,
            out_specs=pl.BlockSpec((nb, TQ, out_w), by_q_tokens(None)),
            scratch_shapes=scratch),
        out_shape=jax.ShapeDtypeStruct((batch, seq, 4 * LANES), BF16),
        compiler_params=_cparams(("parallel", "parallel", "arbitrary")),
        name="flash_" + ("fox" if has_delta else ("sel" if has_u else "win")),
    )(*tabs, *args)


def _cmp_kernel(q_ref, kc_ref, vc_ref, ov_ref, gate_ref, o_ref, u_ref):
    g = pl.program_id(1)
    qi = pl.program_id(2)
    tq = q_ref.shape[2]
    ncmp = kc_ref.shape[2]
    nslc = ov_ref.shape[1]

    t = qi * tq + lax.broadcasted_iota(jnp.int32, (tq, ncmp), 0)
    n = lax.broadcasted_iota(jnp.int32, (tq, ncmp), 1)
    valid = (n * NSA_CMP_STRIDE + (NSA_CMP_LEN - 1)) <= t
    kc = kc_ref[0, 0]
    vc = vc_ref[0, 0]
    ov = ov_ref[...]
    lane = lax.broadcasted_iota(jnp.int32, (tq, LANES), 1)
    lo = lane < HEAD_DIM
    gate_blk = gate_ref[0].astype(F32)

    imp = jnp.zeros((tq, nslc), F32)
    outs = []
    for r in range(NSA_GROUP_HEADS):
        s = lax.dot_general(q_ref[0, r], kc, (((1,), (1,)), ((), ())), preferred_element_type=F32)
        s = jnp.where(valid, s, NEG)
        m = jnp.max(s, axis=1, keepdims=True)
        e = jnp.where(valid, jnp.exp2(s - m), 0.0)
        d = jnp.sum(e, axis=1, keepdims=True)
        pb = (e / jnp.where(d > 0.0, d, 1.0)).astype(BF16)
        o = jnp.dot(pb, vc, preferred_element_type=F32)
        gcol = _gate_column(gate_blk, lane, 3 * (g * NSA_GROUP_HEADS + r))
        outs.append(o * gcol)
        imp = imp + jnp.dot(pb, ov, preferred_element_type=F32)
    for p in range(NSA_GROUP_HEADS // 2):
        o_ref[0, :, p * LANES:(p + 1) * LANES] = jnp.where(lo, outs[2 * p], outs[2 * p + 1]).astype(o_ref.dtype)
    _write_unselected(u_ref, imp, qi, tq, nslc)


def _cmp_fixed_kernel(q_ref, kc_ref, vc_ref, ov_ref, gate_ref, o_ref, u_ref, d_scr, acc_scr, imp_scr,
                      *, chunk):
    g = pl.program_id(1)
    qi = pl.program_id(2)
    tq = q_ref.shape[2]
    ncmp = kc_ref.shape[2]
    nslc = ov_ref.shape[1]
    t0 = qi * tq
    d_scr[...] = jnp.zeros(d_scr.shape, F32)
    acc_scr[...] = jnp.zeros(acc_scr.shape, F32)
    imp_scr[...] = jnp.zeros(imp_scr.shape, F32)

    for c in range(ncmp // chunk):
        first_end = c * chunk * NSA_CMP_STRIDE + NSA_CMP_LEN - 1
        last_end = ((c + 1) * chunk - 1) * NSA_CMP_STRIDE + NSA_CMP_LEN - 1

        def body(masked, c=c):
            rows = slice(c * chunk, (c + 1) * chunk)
            kc = kc_ref[0, 0, rows, :]
            vo = jnp.concatenate([vc_ref[0, 0, rows, :], ov_ref[rows, :]], axis=1)
            if masked:
                t = t0 + lax.broadcasted_iota(jnp.int32, (tq, chunk), 0)
                n = c * chunk + lax.broadcasted_iota(jnp.int32, (tq, chunk), 1)
                valid = (n * NSA_CMP_STRIDE + (NSA_CMP_LEN - 1)) <= t
            for r in range(NSA_GROUP_HEADS):
                s = lax.dot_general(q_ref[0, r], kc, (((1,), (1,)), ((), ())), preferred_element_type=F32)
                if masked:
                    s = jnp.where(valid, s, NEG)
                e = jnp.exp2(s)
                part = e[:, :LANES]
                for j in range(1, chunk // LANES):
                    part = part + e[:, j * LANES:(j + 1) * LANES]
                d_scr[r] = d_scr[r] + part
                both = jnp.dot(e.astype(BF16), vo, preferred_element_type=F32)
                acc_scr[r] = acc_scr[r] + both[:, :LANES]
                imp_scr[r] = imp_scr[r] + both[:, LANES:]

        pl.when((first_end <= t0 + (tq - 1)) & (last_end > t0))(functools.partial(body, True))
        pl.when(last_end <= t0)(functools.partial(body, False))

    lane = lax.broadcasted_iota(jnp.int32, (tq, LANES), 1)
    lo = lane < HEAD_DIM
    gate_blk = gate_ref[0].astype(F32)
    imp = jnp.zeros((tq, nslc), F32)
    outs = []
    for r in range(NSA_GROUP_HEADS):
        d = jnp.sum(d_scr[r], axis=1, keepdims=True)
        inv = 1.0 / jnp.where(d > 0.0, d, 1.0)
        gcol = _gate_column(gate_blk, lane, 3 * (g * NSA_GROUP_HEADS + r))
        outs.append(acc_scr[r] * (inv * gcol))
        imp = imp + imp_scr[r] * inv
    for p in range(NSA_GROUP_HEADS // 2):
        o_ref[0, :, p * LANES:(p + 1) * LANES] = jnp.where(lo, outs[2 * p], outs[2 * p + 1]).astype(o_ref.dtype)
    _write_unselected(u_ref, imp, qi, tq, nslc)


N_FORCED = 3


def _write_unselected(u_ref, imp, qi, tq, nslc):
    tpos = qi * tq + lax.broadcasted_iota(jnp.int32, (tq, nslc), 0)
    jblk = lax.broadcasted_iota(jnp.int32, (tq, nslc), 1)
    cur = jnp.right_shift(tpos, SEL_SHIFT)
    forced = (jblk == 0) | (jblk == cur) | (jblk == cur - 1)
    visible = jblk <= cur
    score_t = jnp.where(visible & jnp.logical_not(forced), imp, -jnp.inf).T

    def run(nrows):
        sc = score_t[:nrows]
        jt = lax.broadcasted_iota(jnp.int32, (nrows, tq), 0).astype(F32)
        for _ in range(NSA_SEL_TOPK - N_FORCED):
            mx = jnp.max(sc, axis=0, keepdims=True)
            first = jnp.min(jnp.where(sc == mx, jt, float(nslc)), axis=0, keepdims=True)
            sc = jnp.where(jt == first, -jnp.inf, sc)
        taken = jnp.where(sc == -jnp.inf, 1.0, 0.0)
        if nrows < nslc:
            taken = jnp.concatenate([taken, jnp.zeros((nslc - nrows, tq), F32)], axis=0)
        chosen = visible & (forced | (taken.T > 0.5))
        u_ref[0, 0] = jnp.where(chosen, 0.0, 1.0).astype(BF16)

    half = nslc // 2
    last_block = jnp.right_shift(qi * tq + (tq - 1), SEL_SHIFT)
    pl.when(last_block < half)(functools.partial(run, half))
    pl.when(last_block >= half)(functools.partial(run, nslc))


def _cmp_select(qn, kc, vc, ov, z, *, batch, seq, fixed_shift):
    ncmp = kc.shape[2]
    nslc = ov.shape[1]
    if fixed_shift:
        chunk = 2 * LANES if ncmp % (2 * LANES) == 0 else ncmp
        kern = functools.partial(_cmp_fixed_kernel, chunk=chunk)
        scratch = [pltpu.VMEM((NSA_GROUP_HEADS, CMP_TQ, LANES), F32) for _ in range(3)]
    else:
        kern, scratch = _cmp_kernel, []
    return pl.pallas_call(
        kern,
        scratch_shapes=scratch,
        grid=(batch, NSA_KV_GROUPS, seq // CMP_TQ),
        in_specs=[pl.BlockSpec((1, NSA_GROUP_HEADS, CMP_TQ, LANES), lambda b, g, i: (b, g, i, 0)),
                  pl.BlockSpec((1, 1, ncmp, LANES), lambda b, g, i: (b, g, 0, 0)),
                  pl.BlockSpec((1, 1, ncmp, LANES), lambda b, g, i: (b, g, 0, 0)),
                  pl.BlockSpec((ncmp, nslc), lambda b, g, i: (0, 0)),
                  pl.BlockSpec((1, CMP_TQ, LANES), lambda b, g, i: (b, i, C_SMALL // LANES))],
        out_specs=[pl.BlockSpec((1, CMP_TQ, 2 * LANES), lambda b, g, i: (b, i, g)),
                   pl.BlockSpec((1, 1, CMP_TQ, nslc), lambda b, g, i: (b, g, i, 0))],
        out_shape=[jax.ShapeDtypeStruct((batch, seq, NSA_WIDTH), BF16),
                   jax.ShapeDtypeStruct((batch, NSA_KV_GROUPS, seq, nslc), BF16)],
        compiler_params=_cparams(("parallel", "parallel", "parallel")),
        name="cmp_select",
    )(qn, kc, vc, ov, z)


def _head_rms(x, g):
    return x * lax.rsqrt(jnp.mean(x * x, axis=-1, keepdims=True) + EPS) * g


def _mem_kernel(zq_ref, kv_ref, qg_ref, kg_ref, o_ref, kn_scr, vb_scr):
    @pl.when(pl.program_id(1) == 0)
    def _prep():
        for h in range(MEM_HEADS):
            kh = kv_ref[0, :, h * MEM_HEAD_DIM:(h + 1) * MEM_HEAD_DIM]
            kn_scr[h] = _head_rms(kh, kg_ref[...]).astype(BF16)
            vb_scr[h] = kv_ref[0, :, MEM_WIDTH + h * MEM_HEAD_DIM:
                               MEM_WIDTH + (h + 1) * MEM_HEAD_DIM].astype(BF16)

    for h in range(MEM_HEADS):
        sl = slice(h * MEM_HEAD_DIM, (h + 1) * MEM_HEAD_DIM)
        qh = (_head_rms(zq_ref[0, :, sl].astype(F32), qg_ref[...])
              * ((MEM_HEAD_DIM ** -0.5) * LOG2E)).astype(BF16)
        s = lax.dot_general(qh, kn_scr[h], (((1,), (1,)), ((), ())), preferred_element_type=F32)
        m = jnp.max(s, axis=1, keepdims=True)
        e = jnp.exp2(s - m)
        p = (e / jnp.sum(e, axis=1, keepdims=True)).astype(BF16)
        o_ref[0, :, sl] = jnp.dot(p, vb_scr[h], preferred_element_type=F32).astype(o_ref.dtype)


def _mem_attention(z, mem_kv, q_g, k_g, *, batch, seq):
    mlen = mem_kv.shape[1]
    tq = TQ
    return pl.pallas_call(
        _mem_kernel,
        grid=(batch, seq // tq),
        in_specs=[pl.BlockSpec((1, tq, MEM_WIDTH), lambda b, i: (b, i, C_MEM_Q // MEM_WIDTH)),
                  pl.BlockSpec((1, mlen, 2 * MEM_WIDTH), lambda b, i: (b, 0, 0)),
                  pl.BlockSpec((1, MEM_HEAD_DIM), lambda b, i: (0, 0)),
                  pl.BlockSpec((1, MEM_HEAD_DIM), lambda b, i: (0, 0))],
        out_specs=pl.BlockSpec((1, tq, MEM_WIDTH), lambda b, i: (b, i, 0)),
        out_shape=jax.ShapeDtypeStruct((batch, seq, MEM_WIDTH), BF16),
        scratch_shapes=[pltpu.VMEM((MEM_HEADS, mlen, MEM_HEAD_DIM), BF16),
                        pltpu.VMEM((MEM_HEADS, mlen, MEM_HEAD_DIM), BF16)],
        compiler_params=_cparams(("parallel", "arbitrary")),
        name="mem_attention",
    )(z, mem_kv, q_g.reshape(1, MEM_HEAD_DIM), k_g.reshape(1, MEM_HEAD_DIM))


def _sigmoid(x):
    return 0.5 * jnp.tanh(0.5 * x) + 0.5


def _out_kernel(x_ref, oc_ref, os_ref, ow_ref, ob_ref, om_ref, sa_ref, sb_ref, sm_ref,
                g0_ref, g1_ref, g2_ref, wa_ref, wb_ref, wm_ref, wo_ref, y_ref):
    silu = lambda ref: (lambda x: x * _sigmoid(x))(ref[...].astype(F32))
    oa = (oc_ref[...].astype(F32) + os_ref[...].astype(F32) + ow_ref[...].astype(F32)) * silu(sa_ref)
    ob = ob_ref[...].astype(F32) * silu(sb_ref)
    om = om_ref[...].astype(F32) * silu(sm_ref)
    gate = lambda ref: _sigmoid(ref[...].astype(F32))
    u = (gate(g0_ref) * jnp.dot(oa.astype(BF16), wa_ref[...], preferred_element_type=F32)
         + gate(g1_ref) * jnp.dot(ob.astype(BF16), wb_ref[...], preferred_element_type=F32)
         + gate(g2_ref) * jnp.dot(om.astype(BF16), wm_ref[...], preferred_element_type=F32))
    y_ref[...] = x_ref[...] + jnp.dot(u.astype(BF16), wo_ref[...], preferred_element_type=F32)


def _out_proj(x2d, o_c, o_s, o_w, o_b, o_m, z2d, wa, wb, wm, wo, layer):
    m = x2d.shape[0]
    tm = 256
    w512 = 512
    row512 = lambda c: pl.BlockSpec((tm, w512), lambda i, c=c: (i, c))
    row1024 = lambda c: pl.BlockSpec((tm, D_MODEL), lambda i, c=c: (i, c))
    full = lambda shape: pl.BlockSpec((None,) + shape, lambda i: (layer, 0, 0))
    return pl.pallas_call(
        _out_kernel,
        grid=(m // tm,),
        in_specs=[row1024(0), row512(0), row512(0), row512(0), row512(0), row512(0),
                  row512(C_NSA_SILU // w512), row512(C_FOX_SILU // w512), row512(C_MEM_SILU // w512),
                  row1024(C_MERGE // D_MODEL), row1024(C_MERGE // D_MODEL + 1),
                  row1024(C_MERGE // D_MODEL + 2),
                  full((w512, D_MODEL)), full((w512, D_MODEL)), full((w512, D_MODEL)),
                  full((D_MODEL, D_MODEL))],
        out_specs=row1024(0),
        out_shape=jax.ShapeDtypeStruct((m, D_MODEL), F32),
        compiler_params=_cparams(("parallel",)),
        name="out_proj",
    )(x2d, o_c, o_s, o_w, o_b, o_m, z2d, z2d, z2d, z2d, z2d, z2d, wa, wb, wm, wo)


def _permute_w_in(w_in):
    o = np.cumsum([0, NSA_WIDTH, 6 * NSA_KV_GROUPS * HEAD_DIM, 3 * NSA_HEADS, NSA_WIDTH,
                   3 * FOX_WIDTH, FOX_HEADS, FOX_WIDTH, MEM_WIDTH, MEM_WIDTH, 3 * D_MODEL])
    nsa_q, nsa_kv, nsa_gate, nsa_silu, fox_qkv, fox_f, fox_silu, mem_q, mem_silu, merge = [
        w_in[..., o[i]:o[i + 1]] for i in range(10)]
    pad = jnp.zeros(w_in.shape[:2] + (LANES - 3 * NSA_HEADS - FOX_HEADS,), w_in.dtype)
    return jnp.concatenate([nsa_q, nsa_silu, fox_qkv, fox_silu, mem_q, mem_silu, merge, nsa_kv,
                            nsa_gate, fox_f, pad], axis=-1).astype(BF16)


def _pad_lanes(x, width=LANES):
    return jnp.pad(x, [(0, 0)] * (x.ndim - 1) + [(0, width - x.shape[-1])])


def _tile2(g):
    return jnp.concatenate([g, g], axis=-1)[:, None, :]


def _overlap(n_cmp_pad, n_slc):
    cs = np.arange(n_cmp_pad)[:, None] * NSA_CMP_STRIDE
    ss = np.arange(NSLC_PAD)[None, :] * NSA_SEL_LEN
    ov = np.clip(np.minimum(cs + NSA_CMP_LEN, ss + NSA_SEL_LEN) - np.maximum(cs, ss), 0, None)
    ov = ov / NSA_CMP_LEN
    ov[-1] = 0.0
    ov[:, n_slc:] = 0.0
    return jnp.asarray(ov, BF16)


def _aug_permutations():
    pq = np.zeros((6 * LANES, FOX_HEADS * LANES), np.float32)
    pk = np.zeros((6 * LANES, FOX_HEADS * LANES), np.float32)
    for h in range(FOX_HEADS):
        for i in range(3):
            pq[i * LANES + F_LANE + h, h * LANES + AUG + i] = 1.0
            pk[i * LANES + F_LANE + h, h * LANES + AUG + 3 + i] = -1.0
            pk[(3 + i) * LANES + F_LANE + h, h * LANES + AUG + 6 + i] = -1.0
    return jnp.asarray(pq, BF16), jnp.asarray(pk, BF16)


def _compress_w1(w1):
    nl = w1.shape[0]
    halves = w1.reshape(nl, 2, NSA_CMP_STRIDE, 1, HEAD_DIM, NSA_CMP_HIDDEN)
    eye = jnp.eye(NSA_KV_GROUPS, dtype=w1.dtype)
    out = jnp.einsum("pg,zhldn->zlpdghn", eye, halves[:, :, :, 0])
    return out.reshape(nl, NSA_CMP_STRIDE * NSA_KV_GROUPS * HEAD_DIM,
                       NSA_KV_GROUPS * 2 * NSA_CMP_HIDDEN).astype(BF16)


def _layer(x, mem, p, big, layer, consts):
    batch, seq, _ = x.shape
    m_rows = batch * seq
    cos4, sin_signed, tri, head_ones, perm_q, perm_k, ov = consts
    z2d = _norm_matmul(x.reshape(m_rows, D_MODEL), p["norm_g"], big["w_in"], layer, 512, Z_WIDTH // 3, BF16)
    z = z2d.reshape(batch, seq, Z_WIDTH)

    (qn, kcr, vcr, ks, vs, kw, vw, fqd, fqo, fka, base) = _prep(
        z, cos4, sin_signed, tri, head_ones, perm_q, perm_k,
        p["gq"], p["gks"], p["gkw"], p["gfq"], p["gfk"], p["fb"],
        p["s_sel"], p["s_win"], p["s_fox"], batch=batch, seq=seq)

    nchunk = seq // NSA_CMP_STRIDE
    kc, vc = _compress(kcr.reshape(batch, nchunk, NSA_CMP_STRIDE * LANES),
                       vcr.reshape(batch, nchunk, NSA_CMP_STRIDE * LANES),
                       p["w1k"], p["w1v"], p["bk"], p["bv"], p["w2k"], p["w2v"], p["gkc"], p["s_cmp"],
                       batch=batch)

    def attend(fixed_shift):
        o_c, unsel = _cmp_select(qn, kc, vc, ov, z, batch=batch, seq=seq, fixed_shift=fixed_shift)
        o_s = _flash(qn, ks, vs, batch=batch, seq=seq, shared_kv=True, fixed_shift=fixed_shift,
                     u=unsel, z=z, gate_branch=1)
        o_w = _flash(qn, kw, vw, batch=batch, seq=seq, shared_kv=True, fixed_shift=fixed_shift,
                     window=NSA_WINDOW, z=z, gate_branch=2)
        o_b = _flash(fqd, fka, z, batch=batch, seq=seq, shared_kv=False, fixed_shift=fixed_shift,
                     base=base.reshape(-1), q_off=fqo, v_col=C_FOX_V // FOX_WIDTH)
        return o_c, o_s, o_w, o_b

    o_c, o_s, o_w, o_b = lax.cond(p["bound_ok"], lambda: attend(True), lambda: attend(False))

    mlen = mem.shape[1]
    mem_kv = _norm_matmul(mem.reshape(batch * mlen, D_MODEL), p["mem_norm_g"], big["w_mem_kv"], layer,
                          min(512, batch * mlen), 2 * MEM_WIDTH, F32).reshape(batch, mlen, 2 * MEM_WIDTH)
    o_m = _mem_attention(z, mem_kv, p["mem_q_norm"], p["mem_k_norm"], batch=batch, seq=seq)

    y = _out_proj(x.reshape(m_rows, D_MODEL), o_c.reshape(m_rows, -1), o_s.reshape(m_rows, -1),
                  o_w.reshape(m_rows, -1), o_b.reshape(m_rows, -1), o_m.reshape(m_rows, -1), z2d,
                  big["w_branch_a"], big["w_branch_b"], big["w_branch_m"], big["w_out"], layer)
    return y.reshape(batch, seq, D_MODEL)


def kernel(x, mem, norm_g, mem_norm_g, w_in, nsa_q_norm, nsa_k_norm, cmp_pe_k, cmp_w1_k, cmp_w2_k,
           cmp_pe_v, cmp_w1_v, cmp_w2_v, fox_q_norm, fox_k_norm, fox_f_bias, mem_q_norm, mem_k_norm,
           w_mem_kv, w_branch_a, w_branch_b, w_branch_m, w_out):
    batch, seq, _ = x.shape
    depth = w_in.shape[0]
    n_slc = seq // NSA_SEL_LEN
    assert n_slc <= NSLC_PAD and seq % TQ == 0 and NSA_WINDOW == TQ

    half = HEAD_DIM // 2
    inv_freq = ROPE_THETA ** (-jnp.arange(half, dtype=F32) / half)
    ang = jnp.arange(seq).astype(F32)[:, None] * inv_freq[None, :]
    cos, sin = jnp.cos(ang), jnp.sin(ang)
    cos4 = jnp.concatenate([cos, cos, cos, cos], axis=-1)
    sin_signed = jnp.concatenate([-sin, sin, -sin, sin], axis=-1)
    tri = jnp.asarray(np.tril(np.ones((TQ, TQ), np.float32)), BF16)
    head_ones = jnp.asarray(np.kron(np.eye(2), np.ones((HEAD_DIM, HEAD_DIM))), BF16)
    consts = (cos4, sin_signed, tri, head_ones, *_aug_permutations(),
              _overlap(seq // NSA_CMP_STRIDE, n_slc))

    hp = lax.Precision.HIGHEST
    pe_bias = lambda pe, w1: jnp.einsum("lk,lkn->ln", pe.reshape(depth, -1), w1, precision=hp)[:, None, :]
    fb = jnp.zeros((depth, 1, LANES), F32).at[:, 0, F_LANE:F_LANE + FOX_HEADS].set(fox_f_bias)

    def logit_bound(gq, gk):
        bound = (BOUND_MARGIN * HEAD_DIM * QK_SCALE
                 * jnp.max(jnp.abs(gq), axis=-1) * jnp.max(jnp.abs(gk), axis=-1))
        return bound.astype(BF16).astype(F32)

    bounds = [logit_bound(nsa_q_norm, nsa_k_norm[:, 1]), logit_bound(nsa_q_norm, nsa_k_norm[:, 2]),
              logit_bound(fox_q_norm, fox_k_norm), logit_bound(nsa_q_norm, nsa_k_norm[:, 0])]
    bound_ok = functools.reduce(jnp.maximum, bounds) <= MAX_FIXED_BOUND
    bound_row = lambda s: jnp.broadcast_to(s[:, None, None], (depth, 1, LANES))
    stacked = dict(
        s_sel=bound_row(bounds[0]), s_win=bound_row(bounds[1]), s_fox=bound_row(bounds[2]),
        s_cmp=bound_row(bounds[3]),
        bound_ok=bound_ok,
        norm_g=norm_g, mem_norm_g=mem_norm_g,
        gq=_tile2(nsa_q_norm), gks=_tile2(nsa_k_norm[:, 1]), gkw=_tile2(nsa_k_norm[:, 2]),
        gkc=_pad_lanes(nsa_k_norm[:, 0])[:, None, :], gfq=_tile2(fox_q_norm), gfk=_tile2(fox_k_norm),
        fb=fb, w1k=_compress_w1(cmp_w1_k), w1v=_compress_w1(cmp_w1_v),
        bk=pe_bias(cmp_pe_k, cmp_w1_k), bv=pe_bias(cmp_pe_v, cmp_w1_v),
        w2k=_pad_lanes(cmp_w2_k).astype(BF16),
        w2v=jnp.concatenate([cmp_w2_v, cmp_w2_v], axis=-1).astype(BF16),
        mem_q_norm=mem_q_norm, mem_k_norm=mem_k_norm)
    big = dict(w_in=_permute_w_in(w_in), w_mem_kv=w_mem_kv.astype(BF16),
               w_branch_a=w_branch_a.astype(BF16), w_branch_b=w_branch_b.astype(BF16),
               w_branch_m=w_branch_m.astype(BF16), w_out=w_out.astype(BF16))
    for l in range(depth):
        x = _layer(x, mem, {k: v[l] for k, v in stacked.items()}, big, l, consts)
    return x
```

```python
import functools

import numpy as np
import jax
import jax.numpy as jnp
from jax import lax
from jax.experimental import pallas as pl
from jax.experimental.pallas import tpu as pltpu

F32 = jnp.float32
BF16 = jnp.bfloat16

D_MODEL = 1024
HEAD_DIM = 64
ROPE_THETA = 10000.0
EPS = 1e-6
NSA_HEADS = 8
NSA_KV_GROUPS = 2
NSA_GROUP_HEADS = NSA_HEADS // NSA_KV_GROUPS
NSA_CMP_LEN = 32
NSA_CMP_STRIDE = 16
NSA_CMP_HIDDEN = 128
NSA_SEL_LEN = 64
NSA_SEL_TOPK = 16
NSA_WINDOW = 512
FORCE_SCORE = 1e9
NSA_WIDTH = NSA_HEADS * HEAD_DIM
FOX_HEADS = 8
FOX_WIDTH = FOX_HEADS * HEAD_DIM
MEM_HEADS = 4
MEM_HEAD_DIM = 128
MEM_WIDTH = MEM_HEADS * MEM_HEAD_DIM

LANES = 128
VMEM_LIMIT = 56 * 1024 * 1024
NEG = -1e30
SEL_MASK = 32768.0
SEL_SHIFT = 6
NSLC_PAD = LANES
LOG2E = 1.4426950408889634
QK_SCALE = (HEAD_DIM ** -0.5) * LOG2E
BOUND_MARGIN = 1.02
MAX_FIXED_BOUND = 60.0
ZERO_WEIGHT_LOG2 = -160.0

C_NSA_Q = 0
C_NSA_SILU = 512
C_FOX_Q = 1024
C_FOX_K = 1536
C_FOX_V = 2048
C_FOX_SILU = 2560
C_MEM_Q = 3072
C_MEM_SILU = 3584
C_MERGE = 4096
C_NSA_KV = 7168
C_SMALL = 7936
Z_WIDTH = 8064
F_LANE = 3 * NSA_HEADS
AUG = HEAD_DIM

TQ = 512
FLASH_ROWS = 4
SAFE_FLASH_ROWS = 2
CMP_TQ = 512


def _cparams(sem):
    return pltpu.CompilerParams(dimension_semantics=sem, vmem_limit_bytes=VMEM_LIMIT)


def _norm_matmul_kernel(x_ref, g_ref, w_ref, o_ref, *, tn):
    x = x_ref[...]
    ms = jnp.mean(x * x, axis=-1, keepdims=True)
    h = (x * lax.rsqrt(ms + EPS) * g_ref[...]).astype(BF16)
    for c0 in range(0, w_ref.shape[1], tn):
        o_ref[:, c0:c0 + tn] = jnp.dot(h, w_ref[:, c0:c0 + tn],
                                       preferred_element_type=F32).astype(o_ref.dtype)


def _norm_matmul(x2d, g, w, layer, tm, tn, out_dtype):
    m, k = x2d.shape
    n = w.shape[2]
    return pl.pallas_call(
        functools.partial(_norm_matmul_kernel, tn=tn),
        grid=(m // tm,),
        in_specs=[pl.BlockSpec((tm, k), lambda i: (i, 0)),
                  pl.BlockSpec((1, k), lambda i: (0, 0)),
                  pl.BlockSpec((None, k, n), lambda i: (layer, 0, 0), pipeline_mode=pl.Buffered(1))],
        out_specs=pl.BlockSpec((tm, n), lambda i: (i, 0)),
        out_shape=jax.ShapeDtypeStruct((m, n), out_dtype),
        compiler_params=_cparams(("parallel",)),
        name="norm_matmul",
    )(x2d, g.reshape(1, k), w)


def _pair_rms(x, g2, head_ones):
    sq = x * x
    hi = sq.astype(BF16)
    low = (sq - hi.astype(F32)).astype(BF16)
    ssq = (jnp.dot(hi, head_ones, preferred_element_type=F32)
           + jnp.dot(low, head_ones, preferred_element_type=F32))
    return x * lax.rsqrt(ssq * (1.0 / HEAD_DIM) + EPS) * g2


def _pair_rope(y, cos4, sin_signed, first_half):
    rot = jnp.where(first_half, pltpu.roll(y, LANES - HEAD_DIM // 2, 1), pltpu.roll(y, HEAD_DIM // 2, 1))
    return y * cos4 + rot * sin_signed


def _head_a(y, lo, tail=0.0):
    return jnp.where(lo, y, tail)


def _head_b(y, lo, tail=0.0):
    return jnp.where(lo, pltpu.roll(y, HEAD_DIM, 1), tail)


def _split3_f32(c):
    hi = c.astype(BF16).astype(F32)
    r = c - hi
    mid = r.astype(BF16).astype(F32)
    return hi, mid, r - mid


def _prep_kernel(zq_ref, zc_ref, zs_ref, zw_ref, fq_ref, fk_ref, sm_ref, cos_ref, sin_ref,
                 tri_ref, ones_ref, pq_ref, pk_ref, gq_ref, gks_ref, gkw_ref, gfq_ref, gfk_ref, fb_ref,
                 ss_ref, sw_ref, sf_ref,
                 qn_ref, kcr_ref, vcr_ref, ks_ref, vs_ref, kw_ref, vw_ref, fqd_ref, fqo_ref, fka_ref,
                 base_ref, run_scr):
    ti = pl.program_id(1)
    tt = zq_ref.shape[1]
    lane = lax.broadcasted_iota(jnp.int32, (tt, LANES), 1)
    lo = lane < HEAD_DIM
    first_half = (lane & (HEAD_DIM // 2)) == 0
    cos4 = cos_ref[...]
    sin_signed = sin_ref[...]
    head_ones = ones_ref[...]
    one_at_aug = jnp.where(lane == AUG, 1.0, 0.0)

    for p in range(NSA_HEADS // 2):
        y = _pair_rope(_pair_rms(zq_ref[0, :, p * LANES:(p + 1) * LANES].astype(F32), gq_ref[...], head_ones),
                       cos4, sin_signed, first_half) * QK_SCALE
        qn_ref[0, 2 * p] = _head_a(y, lo, one_at_aug).astype(BF16)
        qn_ref[0, 2 * p + 1] = _head_b(y, lo, one_at_aug).astype(BF16)

    kcr_ref[0] = _pair_rope(zc_ref[0, :, :LANES].astype(F32), cos4, sin_signed, first_half).astype(BF16)
    vcr_ref[0] = zc_ref[0, :, LANES:].astype(BF16)

    sel_lane = jnp.right_shift(ti * tt + lax.broadcasted_iota(jnp.int32, (tt, LANES), 0), SEL_SHIFT)
    onehot = jnp.where(lane == sel_lane, -SEL_MASK, 0.0).astype(BF16)
    for z_ref, g_ref, k_ref, v_ref, bound_ref, with_onehot in (
            (zs_ref, gks_ref, ks_ref, vs_ref, ss_ref, True),
            (zw_ref, gkw_ref, kw_ref, vw_ref, sw_ref, False)):
        y = _pair_rope(_pair_rms(z_ref[0, :, :LANES].astype(F32), g_ref[...], head_ones),
                       cos4, sin_signed, first_half)
        v = z_ref[0, :, LANES:].astype(F32)
        vr = pltpu.roll(v, HEAD_DIM, 1)
        neg_bound = jnp.where(lane == AUG, -bound_ref[...], 0.0)
        k_ref[0, 0, :, :LANES] = _head_a(y, lo, neg_bound).astype(BF16)
        k_ref[0, 1, :, :LANES] = _head_b(y, lo, neg_bound).astype(BF16)
        if with_onehot:
            k_ref[0, 0, :, LANES:] = onehot
            k_ref[0, 1, :, LANES:] = onehot
        v_ref[0, 0] = jnp.where(lo, v, vr).astype(BF16)
        v_ref[0, 1] = jnp.where(lo, vr, v).astype(BF16)

    @pl.when(ti == 0)
    def _reset():
        run_scr[...] = jnp.zeros(run_scr.shape, F32)

    xf = sm_ref[0].astype(F32) + fb_ref[...]
    log_f = (jnp.minimum(xf, 0.0) - jnp.log1p(jnp.exp(-jnp.abs(xf)))) * LOG2E
    tri = tri_ref[...]
    parts = _split3_f32(log_f)
    cum = sum(jnp.dot(tri, part.astype(BF16), preferred_element_type=F32) for part in parts)
    base_ref[0, 0] = run_scr[...]
    run_scr[...] = run_scr[...] + cum[tt - 1:tt, :]

    parts6 = _split3_f32(cum) + _split3_f32(cum - cum[tt - 1:tt, :])
    x6 = jnp.concatenate([part.astype(BF16) for part in parts6], axis=1)
    aug_q = jnp.dot(x6, pq_ref[...], preferred_element_type=F32)
    aug_k = jnp.dot(x6, pk_ref[...], preferred_element_type=F32)

    lane1 = lax.broadcasted_iota(jnp.int32, (1, LANES), 1)
    span = lambda a, b: jnp.where((lane1 >= a) & (lane1 < b), 1.0, 0.0)
    q_tail_diag = span(AUG + 3, AUG + 6) + span(AUG + 9, AUG + 10)
    q_tail_off = span(AUG + 6, AUG + 10)
    k_tail = span(AUG, AUG + 3) + jnp.where(lane1 == AUG + 9, -sf_ref[...], 0.0)
    for p in range(FOX_HEADS // 2):
        sl = slice(p * LANES, (p + 1) * LANES)
        yq = _pair_rms(fq_ref[0, :, sl].astype(F32), gfq_ref[...], head_ones) * QK_SCALE
        yk = _pair_rms(fk_ref[0, :, sl].astype(F32), gfk_ref[...], head_ones)
        for e, split in ((0, _head_a), (1, _head_b)):
            h = 2 * p + e
            hs = slice(h * LANES, (h + 1) * LANES)
            qh = split(yq, lo) + aug_q[:, hs]
            fqd_ref[0, h] = (qh + q_tail_diag).astype(BF16)
            fqo_ref[0, h] = (qh + q_tail_off).astype(BF16)
            fka_ref[0, h] = (split(yk, lo) + aug_k[:, hs] + k_tail).astype(BF16)


def _prep(z, cos4, sin_signed, tri, head_ones, perm_q, perm_k, gq, gks, gkw, gfq, gfk, fb,
          s_sel, s_win, s_fox, *, batch, seq):
    nblk = seq // TQ
    whole = lambda a: pl.BlockSpec(a.shape, lambda b, i: (0, 0))
    zspec = lambda w, c: pl.BlockSpec((1, TQ, w), lambda b, i, c=c: (b, i, c))
    row = pl.BlockSpec((1, LANES), lambda b, i: (0, 0))
    tab = pl.BlockSpec((TQ, LANES), lambda b, i: (i, 0))
    heads = lambda n, w: pl.BlockSpec((1, n, TQ, w), lambda b, i: (b, 0, i, 0))
    tok = lambda w: pl.BlockSpec((1, TQ, w), lambda b, i: (b, i, 0))
    bf = lambda *shape: jax.ShapeDtypeStruct(shape, BF16)
    return pl.pallas_call(
        _prep_kernel,
        grid=(batch, nblk),
        in_specs=[zspec(NSA_WIDTH, C_NSA_Q // NSA_WIDTH),
                  zspec(2 * LANES, C_NSA_KV // (2 * LANES)),
                  zspec(2 * LANES, C_NSA_KV // (2 * LANES) + 1),
                  zspec(2 * LANES, C_NSA_KV // (2 * LANES) + 2),
                  zspec(FOX_WIDTH, C_FOX_Q // FOX_WIDTH),
                  zspec(FOX_WIDTH, C_FOX_K // FOX_WIDTH),
                  zspec(LANES, C_SMALL // LANES),
                  tab, tab, whole(tri), whole(head_ones), whole(perm_q), whole(perm_k),
                  row, row, row, row, row, row, row, row, row],
        out_specs=[heads(NSA_HEADS, LANES), tok(LANES), tok(LANES),
                   heads(NSA_KV_GROUPS, 2 * LANES), heads(NSA_KV_GROUPS, LANES),
                   heads(NSA_KV_GROUPS, LANES), heads(NSA_KV_GROUPS, LANES),
                   heads(FOX_HEADS, LANES), heads(FOX_HEADS, LANES), heads(FOX_HEADS, LANES),
                   pl.BlockSpec((1, 1, 1, LANES), lambda b, i: (b, i, 0, 0))],
        out_shape=[bf(batch, NSA_HEADS, seq, LANES), bf(batch, seq, LANES), bf(batch, seq, LANES),
                   bf(batch, NSA_KV_GROUPS, seq, 2 * LANES), bf(batch, NSA_KV_GROUPS, seq, LANES),
                   bf(batch, NSA_KV_GROUPS, seq, LANES), bf(batch, NSA_KV_GROUPS, seq, LANES),
                   bf(batch, FOX_HEADS, seq, LANES), bf(batch, FOX_HEADS, seq, LANES),
                   bf(batch, FOX_HEADS, seq, LANES),
                   jax.ShapeDtypeStruct((batch, nblk, 1, LANES), F32)],
        scratch_shapes=[pltpu.VMEM((1, LANES), F32)],
        compiler_params=_cparams(("parallel", "arbitrary")),
        name="prep",
    )(z, z, z, z, z, z, z, cos4, sin_signed, tri, head_ones, perm_q, perm_k,
      gq, gks, gkw, gfq, gfk, fb, s_sel, s_win, s_fox)


def _compress_kernel(xk_ref, xv_ref, w1k_ref, w1v_ref, bk_ref, bv_ref, w2k_ref, w2v_ref, kg_ref,
                     sc_ref, kc_ref, vc_ref):
    nchunk = xk_ref.shape[1]
    lane = lax.broadcasted_iota(jnp.int32, (nchunk, LANES), 1)
    for x_ref, w1_ref, b_ref, w2_ref, o_ref, is_key in ((xk_ref, w1k_ref, bk_ref, w2k_ref, kc_ref, True),
                                                        (xv_ref, w1v_ref, bv_ref, w2v_ref, vc_ref, False)):
        h = jnp.dot(x_ref[0], w1_ref[...], preferred_element_type=F32)
        for g in range(NSA_KV_GROUPS):
            c0 = g * 2 * NSA_CMP_HIDDEN
            top = h[:, c0:c0 + NSA_CMP_HIDDEN]
            bot = h[:, c0 + NSA_CMP_HIDDEN:c0 + 2 * NSA_CMP_HIDDEN]
            hid = top + pltpu.roll(bot, nchunk - 1, 0) + b_ref[...]
            act = (hid * jax.nn.sigmoid(hid)).astype(BF16)
            o = jnp.dot(act, w2_ref[...], preferred_element_type=F32)
            if is_key:
                ms = jnp.sum(o * o, axis=1, keepdims=True) * (1.0 / HEAD_DIM)
                o = o * lax.rsqrt(ms + EPS) * kg_ref[...]
                o = jnp.where(lane == AUG, -sc_ref[...], o)
            o_ref[0, g] = o.astype(BF16)


def _compress(xk, xv, w1k, w1v, bk, bv, w2k, w2v, kg, s_cmp, *, batch):
    nchunk, kin = xk.shape[1], xk.shape[2]
    xspec = pl.BlockSpec((1, nchunk, kin), lambda b: (b, 0, 0))
    full = lambda a: pl.BlockSpec(a.shape, lambda b: (0,) * a.ndim)
    ospec = pl.BlockSpec((1, NSA_KV_GROUPS, nchunk, LANES), lambda b: (b, 0, 0, 0))
    oshape = jax.ShapeDtypeStruct((batch, NSA_KV_GROUPS, nchunk, LANES), BF16)
    return pl.pallas_call(
        _compress_kernel,
        grid=(batch,),
        in_specs=[xspec, xspec, full(w1k), full(w1v), full(bk), full(bv), full(w2k), full(w2v), full(kg),
                  full(s_cmp)],
        out_specs=[ospec, ospec],
        out_shape=[oshape, oshape],
        compiler_params=_cparams(("parallel",)),
        name="compress",
    )(xk, xv, w1k, w1v, bk, bv, w2k, w2v, kg, s_cmp)


def _gate_column(gate_blk, lane, idx):
    col = jnp.sum(jnp.where(lane == idx, gate_blk, 0.0), axis=1, keepdims=True)
    return jax.nn.sigmoid(col)


def _flash_kernel(qi_tab, ki_tab, fl_tab, *refs, nheads, shared_kv, window, has_delta,
                  has_u, gate_branch, nblk, fixed_shift, nb):
    refs = list(refs)
    base_ref = refs.pop(0) if has_delta else None
    q_ref = refs.pop(0)
    qoff_ref = refs.pop(0) if has_delta else None
    u_ref = refs.pop(0) if has_u else None
    k_ref = refs.pop(0)
    v_ref = refs.pop(0)
    gate_ref = refs.pop(0) if gate_branch is not None else None
    o_ref = refs.pop(0)
    m_scr = refs.pop(0)
    l_scr = refs.pop(0)
    acc_scr = refs.pop(0)
    qcat_scr = refs.pop(0) if has_u else None

    b = pl.program_id(0)
    g = pl.program_id(1)
    step = (b * pl.num_programs(1) + g) * pl.num_programs(2) + pl.program_id(2)
    qi = qi_tab[step]
    ki = ki_tab[step]
    fl = fl_tab[step]
    active = (fl & 4) == 4
    tq = q_ref.shape[2]
    tk = k_ref.shape[2]

    @pl.when((fl & 1) == 1)
    def _init():
        m_scr[...] = jnp.full(m_scr.shape, NEG, F32)
        l_scr[...] = jnp.zeros(l_scr.shape, F32)
        acc_scr[...] = jnp.zeros(acc_scr.shape, F32)
        if has_u:
            for bb in range(nb):
                for r in range(nheads):
                    qcat_scr[bb * nheads + r, :, :LANES] = q_ref[bb, r]
                    qcat_scr[bb * nheads + r, :, LANES:] = u_ref[bb, r // NSA_GROUP_HEADS]

    lane = lax.broadcasted_iota(jnp.int32, (tq, LANES), 1)
    lo = lane < HEAD_DIM

    def tile(diagonal):
        masked = diagonal or window is not None
        if masked:
            row = lax.broadcasted_iota(jnp.int32, (tq, tk), 0)
            col = lax.broadcasted_iota(jnp.int32, (tq, tk), 1)
            if diagonal:
                valid = row >= col
            else:
                valid = (tq + row - col) < window
        for bb in range(nb):
            slot0, pair0, brow = bb * nheads, bb * (nheads // 2), b * nb + bb
            for p in range(nheads // 2):
                pvs, alphas = [], []
                for e in range(2):
                    hh = 2 * p + e
                    if has_u:
                        q = qcat_scr[slot0 + hh]
                    elif has_delta and not diagonal:
                        q = qoff_ref[bb, hh]
                    else:
                        q = q_ref[bb, hh]
                    k = k_ref[bb, hh // NSA_GROUP_HEADS] if shared_kv else k_ref[bb, hh]
                    v = (v_ref[bb, hh // NSA_GROUP_HEADS] if shared_kv
                         else v_ref[bb, :, p * LANES:(p + 1) * LANES])
                    s = lax.dot_general(q, k, (((1,), (1,)), ((), ())), preferred_element_type=F32)
                    if masked:
                        s = jnp.where(valid, s, NEG)
                    delta = None
                    if has_delta and not diagonal:
                        hoff = F_LANE + g * nheads + hh
                        delta = (base_ref[(brow * nblk + qi) * LANES + hoff]
                                 - base_ref[(brow * nblk + ki + 1) * LANES + hoff])
                    if fixed_shift:
                        pm = jnp.exp2(s)
                        rowsum = pm[:, :LANES]
                        for j in range(1, tk // LANES):
                            rowsum = rowsum + pm[:, j * LANES:(j + 1) * LANES]
                        pv = jnp.dot(pm.astype(BF16), v, preferred_element_type=F32)
                        if delta is not None:
                            w = jnp.exp2(jnp.full((1, LANES), delta, F32))
                            rowsum = rowsum * w
                            pv = pv * w
                        l_scr[slot0 + hh] = l_scr[slot0 + hh] + rowsum
                        pvs.append(pv)
                    else:
                        m_prev = m_scr[slot0 + hh]
                        m_cur = jnp.max(s, axis=1, keepdims=True)
                        if delta is not None:
                            m_cur = m_cur + delta
                        m_new = jnp.maximum(m_prev, m_cur)
                        alpha = jnp.exp2(m_prev - m_new)
                        shift = m_new[:, :1]
                        if delta is not None:
                            shift = shift - delta
                        pm = jnp.exp2(s - shift)
                        l_scr[slot0 + hh] = alpha * l_scr[slot0 + hh] + jnp.sum(pm, axis=1, keepdims=True)
                        m_scr[slot0 + hh] = m_new
                        pvs.append(jnp.dot(pm.astype(BF16), v, preferred_element_type=F32))
                        alphas.append(alpha)
                if fixed_shift:
                    acc_scr[pair0 + p] = acc_scr[pair0 + p] + jnp.where(lo, pvs[0], pvs[1])
                else:
                    acc_scr[pair0 + p] = (acc_scr[pair0 + p] * jnp.where(lo, alphas[0], alphas[1])
                                          + jnp.where(lo, pvs[0], pvs[1]))

    pl.when((qi == ki) & active)(lambda: tile(True))
    pl.when((qi != ki) & active)(lambda: tile(False))

    @pl.when((fl & 2) == 2)
    def _finish():
        for bb in range(nb):
            slot0, pair0 = bb * nheads, bb * (nheads // 2)
            for p in range(nheads // 2):
                if fixed_shift:
                    la = jnp.sum(l_scr[slot0 + 2 * p], axis=1, keepdims=True)
                    lb = jnp.sum(l_scr[slot0 + 2 * p + 1], axis=1, keepdims=True)
                else:
                    la, lb = l_scr[slot0 + 2 * p], l_scr[slot0 + 2 * p + 1]
                inv = jnp.where(lo, 1.0 / la, 1.0 / lb)
                if gate_branch is not None:
                    h0 = g * nheads + 2 * p
                    src = lax.broadcasted_iota(jnp.int32, (LANES, LANES), 0)
                    dst_lo = lax.broadcasted_iota(jnp.int32, (LANES, LANES), 1) < HEAD_DIM
                    pick = jnp.where(src == jnp.where(dst_lo, 3 * h0 + gate_branch,
                                                      3 * (h0 + 1) + gate_branch), 1.0, 0.0)
                    logits = jnp.dot(gate_ref[bb].astype(BF16), pick.astype(BF16),
                                     preferred_element_type=F32)
                    inv = inv * jax.nn.sigmoid(logits)
                o_ref[bb, :, p * LANES:(p + 1) * LANES] = (acc_scr[pair0 + p] * inv).astype(o_ref.dtype)


def _tile_tables(nq, window_tiles, batch):
    qi, ki, fl = [], [], []
    for i in range(nq):
        lo = 0 if window_tiles is None else max(0, i - window_tiles)
        for j in range(lo, i + 1):
            qi.append(i)
            ki.append(j)
            fl.append((1 if j == lo else 0) | (2 if j == i else 0) | 4)
    rows = lambda a: jnp.tile(jnp.asarray(a, jnp.int32), batch)
    return rows(qi), rows(ki), rows(fl)


def _decayed_tile_tables(nq, batch, nb, ngroups, base):
    qi = np.repeat(np.arange(nq), np.arange(1, nq + 1))
    ki = np.concatenate([np.arange(i + 1) for i in range(nq)])
    nsteps = qi.shape[0]
    diag = jnp.asarray(qi == ki)
    gate = base.reshape(batch, nq, LANES)[:, :, F_LANE:F_LANE + FOX_HEADS]
    delta = gate[:, qi, :] - gate[:, np.minimum(ki + 1, nq - 1), :]
    top = jnp.max(delta.reshape(batch // nb, nb, nsteps, ngroups, FOX_HEADS // ngroups), axis=(1, 4))
    top = top.transpose(0, 2, 1).reshape(batch // nb * ngroups, nsteps)
    active = diag[None, :] | (top > ZERO_WEIGHT_LOG2)
    same_q = jnp.asarray(qi[:, None] == np.arange(nq)[None, :])
    first_ki = jnp.min(jnp.where(active[:, :, None] & same_q[None], ki[None, :, None], nq), axis=1)
    first = active & (jnp.asarray(ki)[None, :] == first_ki[:, qi])
    flags = first.astype(jnp.int32) + 2 * diag[None, :].astype(jnp.int32) + 4 * active.astype(jnp.int32)
    order = jnp.argsort(jnp.logical_not(active), axis=1, stable=True)
    n_active = jnp.sum(active, axis=1, keepdims=True)
    slot = jnp.arange(nsteps)[None, :]
    src = jnp.take_along_axis(order, jnp.minimum(slot, n_active - 1), axis=1)
    kept = slot < n_active
    take = lambda a: jnp.asarray(a, jnp.int32)[src].reshape(-1)
    fl = jnp.where(kept, jnp.take_along_axis(flags, src, axis=1), 0).reshape(-1)
    return take(qi), take(ki), fl.astype(jnp.int32)


def _flash(q, k, v, *, batch, seq, shared_kv, fixed_shift, nheads=NSA_HEADS, window=None, base=None,
           q_off=None, u=None, z=None, gate_branch=None, v_offset=0):
    nb = FLASH_ROWS if fixed_shift else SAFE_FLASH_ROWS
    nb = nb if batch % nb == 0 else 1
    nq = seq // TQ
    ngroups = NSA_HEADS // nheads
    nkv = NSA_KV_GROUPS
    da = k.shape[-1]
    has_delta = base is not None
    has_u = u is not None
    out_w = LANES * (nheads // 2)
    v_col = v_offset // out_w
    table_rows = batch // nb * ngroups
    if has_delta and fixed_shift:
        tabs = _decayed_tile_tables(nq, batch, nb, ngroups, base)
    else:
        tabs = _tile_tables(nq, None if window is None else window // TQ, table_rows)
    nsteps = int(tabs[0].shape[0]) // table_rows

    at = lambda tab, b, g, s: tab[(b * ngroups + g) * nsteps + s]
    by_q_heads = lambda b, g, s, qt, kt, ft: (b, g, at(qt, b, g, s), 0)
    by_k_heads = lambda b, g, s, qt, kt, ft: (b, g, at(kt, b, g, s), 0)
    by_q_tokens = lambda col: (lambda b, g, s, qt, kt, ft: (b, at(qt, b, g, s), g if col is None else col))
    by_k_tokens = lambda b, g, s, qt, kt, ft: (b, at(kt, b, g, s), g + v_col)

    in_specs, args = [], []
    if has_delta:
        in_specs.append(pl.BlockSpec(memory_space=pltpu.SMEM))
        args.append(base)
    in_specs.append(pl.BlockSpec((nb, nheads, TQ, LANES), by_q_heads))
    args.append(q)
    if has_delta:
        in_specs.append(pl.BlockSpec((nb, nheads, TQ, LANES), by_q_heads))
        args.append(q_off)
    if has_u:
        in_specs.append(pl.BlockSpec((nb, nkv, TQ, LANES), by_q_heads))
        args.append(u)
    if shared_kv:
        in_specs.append(pl.BlockSpec((nb, nkv, TQ, da), by_k_heads))
        in_specs.append(pl.BlockSpec((nb, nkv, TQ, LANES), by_k_heads))
    else:
        in_specs.append(pl.BlockSpec((nb, nheads, TQ, da), by_k_heads))
        in_specs.append(pl.BlockSpec((nb, TQ, out_w), by_k_tokens))
    args += [k, v]
    if gate_branch is not None:
        in_specs.append(pl.BlockSpec((nb, TQ, LANES), by_q_tokens(C_SMALL // LANES)))
        args.append(z)

    m_rows = 8 if fixed_shift else TQ
    scratch = [pltpu.VMEM((nb * nheads, m_rows, LANES), F32),
               pltpu.VMEM((nb * nheads, TQ, LANES), F32),
               pltpu.VMEM((nb * nheads // 2, TQ, LANES), F32)]
    if has_u:
        scratch.append(pltpu.VMEM((nb * nheads, TQ, 2 * LANES), BF16))

    kern = functools.partial(_flash_kernel, nheads=nheads, shared_kv=shared_kv, window=window,
                             has_delta=has_delta, has_u=has_u, gate_branch=gate_branch, nblk=nq,
                             fixed_shift=fixed_shift, nb=nb)
    return pl.pallas_call(
        kern,
        grid_spec=pltpu.PrefetchScalarGridSpec(
            num_scalar_prefetch=3,
            grid=(batch // nb, ngroups, nsteps),
            in_specs=in_specs,
            out_specs=pl.BlockSpec((nb, TQ, out_w), by_q_tokens(None)),
            scratch_shapes=scratch),
        out_shape=jax.ShapeDtypeStruct((batch, seq, 4 * LANES), BF16),
        compiler_params=_cparams(("parallel", "parallel", "arbitrary")),
        name="flash_" + ("fox" if has_delta else ("sel" if has_u else "win")),
    )(*tabs, *args)


def _cmp_kernel(q_ref, kc_ref, vc_ref, ov_ref, gate_ref, o_ref, u_ref):
    g = pl.program_id(1)
    qi = pl.program_id(2)
    tq = q_ref.shape[2]
    ncmp = kc_ref.shape[2]
    nslc = ov_ref.shape[1]

    t = qi * tq + lax.broadcasted_iota(jnp.int32, (tq, ncmp), 0)
    n = lax.broadcasted_iota(jnp.int32, (tq, ncmp), 1)
    valid = (n * NSA_CMP_STRIDE + (NSA_CMP_LEN - 1)) <= t
    kc = kc_ref[0, 0]
    vc = vc_ref[0, 0]
    ov = ov_ref[...]
    lane = lax.broadcasted_iota(jnp.int32, (tq, LANES), 1)
    lo = lane < HEAD_DIM
    gate_blk = gate_ref[0].astype(F32)

    imp = jnp.zeros((tq, nslc), F32)
    outs = []
    for r in range(NSA_GROUP_HEADS):
        s = lax.dot_general(q_ref[0, r], kc, (((1,), (1,)), ((), ())), preferred_element_type=F32)
        s = jnp.where(valid, s, NEG)
        m = jnp.max(s, axis=1, keepdims=True)
        e = jnp.where(valid, jnp.exp2(s - m), 0.0)
        d = jnp.sum(e, axis=1, keepdims=True)
        pb = (e / jnp.where(d > 0.0, d, 1.0)).astype(BF16)
        o = jnp.dot(pb, vc, preferred_element_type=F32)
        gcol = _gate_column(gate_blk, lane, 3 * (g * NSA_GROUP_HEADS + r))
        outs.append(o * gcol)
        imp = imp + jnp.dot(pb, ov, preferred_element_type=F32)
    for p in range(NSA_GROUP_HEADS // 2):
        o_ref[0, :, p * LANES:(p + 1) * LANES] = jnp.where(lo, outs[2 * p], outs[2 * p + 1]).astype(o_ref.dtype)
    _write_unselected(u_ref, imp, qi, tq, nslc)


def _cmp_fixed_kernel(q_ref, kc_ref, vc_ref, ov_ref, gate_ref, o_ref, u_ref, d_scr, acc_scr, imp_scr,
                      *, chunk):
    g = pl.program_id(1)
    qi = pl.program_id(2)
    tq = q_ref.shape[2]
    ncmp = kc_ref.shape[2]
    nslc = ov_ref.shape[1]
    t0 = qi * tq
    d_scr[...] = jnp.zeros(d_scr.shape, F32)
    acc_scr[...] = jnp.zeros(acc_scr.shape, F32)
    imp_scr[...] = jnp.zeros(imp_scr.shape, F32)

    for c in range(ncmp // chunk):
        first_end = c * chunk * NSA_CMP_STRIDE + NSA_CMP_LEN - 1
        last_end = ((c + 1) * chunk - 1) * NSA_CMP_STRIDE + NSA_CMP_LEN - 1

        def body(masked, c=c):
            rows = slice(c * chunk, (c + 1) * chunk)
            kc = kc_ref[0, 0, rows, :]
            vo = jnp.concatenate([vc_ref[0, 0, rows, :], ov_ref[rows, :]], axis=1)
            if masked:
                t = t0 + lax.broadcasted_iota(jnp.int32, (tq, chunk), 0)
                n = c * chunk + lax.broadcasted_iota(jnp.int32, (tq, chunk), 1)
                valid = (n * NSA_CMP_STRIDE + (NSA_CMP_LEN - 1)) <= t
            for r in range(NSA_GROUP_HEADS):
                s = lax.dot_general(q_ref[0, r], kc, (((1,), (1,)), ((), ())), preferred_element_type=F32)
                if masked:
                    s = jnp.where(valid, s, NEG)
                e = jnp.exp2(s)
                part = e[:, :LANES]
                for j in range(1, chunk // LANES):
                    part = part + e[:, j * LANES:(j + 1) * LANES]
                d_scr[r] = d_scr[r] + part
                both = jnp.dot(e.astype(BF16), vo, preferred_element_type=F32)
                acc_scr[r] = acc_scr[r] + both[:, :LANES]
                imp_scr[r] = imp_scr[r] + both[:, LANES:]

        pl.when((first_end <= t0 + (tq - 1)) & (last_end > t0))(functools.partial(body, True))
        pl.when(last_end <= t0)(functools.partial(body, False))

    lane = lax.broadcasted_iota(jnp.int32, (tq, LANES), 1)
    lo = lane < HEAD_DIM
    src = lax.broadcasted_iota(jnp.int32, (LANES, LANES), 0)
    dst_lo = lax.broadcasted_iota(jnp.int32, (LANES, LANES), 1) < HEAD_DIM
    imp = jnp.zeros((tq, nslc), F32)
    for p in range(NSA_GROUP_HEADS // 2):
        normed = []
        for r in (2 * p, 2 * p + 1):
            d = jnp.sum(d_scr[r], axis=1, keepdims=True)
            inv = 1.0 / jnp.where(d > 0.0, d, 1.0)
            normed.append(acc_scr[r] * inv)
            imp = imp + imp_scr[r] * inv
        h0 = g * NSA_GROUP_HEADS + 2 * p
        pick = jnp.where(src == jnp.where(dst_lo, 3 * h0, 3 * (h0 + 1)), 1.0, 0.0).astype(BF16)
        logits = jnp.dot(gate_ref[0].astype(BF16), pick, preferred_element_type=F32)
        o_ref[0, :, p * LANES:(p + 1) * LANES] = (jnp.where(lo, normed[0], normed[1])
                                                  * jax.nn.sigmoid(logits)).astype(o_ref.dtype)
    _write_unselected(u_ref, imp, qi, tq, nslc)


N_FORCED = 3


def _write_unselected(u_ref, imp, qi, tq, nslc):
    tpos = qi * tq + lax.broadcasted_iota(jnp.int32, (tq, nslc), 0)
    jblk = lax.broadcasted_iota(jnp.int32, (tq, nslc), 1)
    cur = jnp.right_shift(tpos, SEL_SHIFT)
    forced = (jblk == 0) | (jblk == cur) | (jblk == cur - 1)
    visible = jblk <= cur
    score_t = jnp.where(visible & jnp.logical_not(forced), imp, -jnp.inf).T

    def run(nrows):
        sc = score_t[:nrows]
        jt = lax.broadcasted_iota(jnp.int32, (nrows, tq), 0).astype(F32)
        for _ in range(NSA_SEL_TOPK - N_FORCED):
            mx = jnp.max(sc, axis=0, keepdims=True)
            first = jnp.min(jnp.where(sc == mx, jt, float(nslc)), axis=0, keepdims=True)
            sc = jnp.where(jt == first, -jnp.inf, sc)
        taken = jnp.where(sc == -jnp.inf, 1.0, 0.0)
        if nrows < nslc:
            taken = jnp.concatenate([taken, jnp.zeros((nslc - nrows, tq), F32)], axis=0)
        chosen = visible & (forced | (taken.T > 0.5))
        u_ref[0, 0] = jnp.where(chosen, 0.0, 1.0).astype(BF16)

    half = nslc // 2
    last_block = jnp.right_shift(qi * tq + (tq - 1), SEL_SHIFT)
    pl.when(last_block < half)(functools.partial(run, half))
    pl.when(last_block >= half)(functools.partial(run, nslc))


def _cmp_select(qn, kc, vc, ov, z, *, batch, seq, fixed_shift):
    ncmp = kc.shape[2]
    nslc = ov.shape[1]
    if fixed_shift:
        chunk = 2 * LANES if ncmp % (2 * LANES) == 0 else ncmp
        kern = functools.partial(_cmp_fixed_kernel, chunk=chunk)
        scratch = [pltpu.VMEM((NSA_GROUP_HEADS, CMP_TQ, LANES), F32) for _ in range(3)]
    else:
        kern, scratch = _cmp_kernel, []
    return pl.pallas_call(
        kern,
        scratch_shapes=scratch,
        grid=(batch, NSA_KV_GROUPS, seq // CMP_TQ),
        in_specs=[pl.BlockSpec((1, NSA_GROUP_HEADS, CMP_TQ, LANES), lambda b, g, i: (b, g, i, 0)),
                  pl.BlockSpec((1, 1, ncmp, LANES), lambda b, g, i: (b, g, 0, 0)),
                  pl.BlockSpec((1, 1, ncmp, LANES), lambda b, g, i: (b, g, 0, 0)),
                  pl.BlockSpec((ncmp, nslc), lambda b, g, i: (0, 0)),
                  pl.BlockSpec((1, CMP_TQ, LANES), lambda b, g, i: (b, i, C_SMALL // LANES))],
        out_specs=[pl.BlockSpec((1, CMP_TQ, 2 * LANES), lambda b, g, i: (b, i, g)),
                   pl.BlockSpec((1, 1, CMP_TQ, nslc), lambda b, g, i: (b, g, i, 0))],
        out_shape=[jax.ShapeDtypeStruct((batch, seq, NSA_WIDTH), BF16),
                   jax.ShapeDtypeStruct((batch, NSA_KV_GROUPS, seq, nslc), BF16)],
        compiler_params=_cparams(("parallel", "parallel", "parallel")),
        name="cmp_select",
    )(qn, kc, vc, ov, z)


def _head_rms(x, g):
    return x * lax.rsqrt(jnp.mean(x * x, axis=-1, keepdims=True) + EPS) * g


def _mem_kernel(zq_ref, kv_ref, qg_ref, kg_ref, o_ref, kn_scr, vb_scr):
    @pl.when(pl.program_id(1) == 0)
    def _prep():
        for h in range(MEM_HEADS):
            kh = kv_ref[0, :, h * MEM_HEAD_DIM:(h + 1) * MEM_HEAD_DIM]
            kn_scr[h] = _head_rms(kh, kg_ref[...]).astype(BF16)
            vb_scr[h] = kv_ref[0, :, MEM_WIDTH + h * MEM_HEAD_DIM:
                               MEM_WIDTH + (h + 1) * MEM_HEAD_DIM].astype(BF16)

    for h in range(MEM_HEADS):
        sl = slice(h * MEM_HEAD_DIM, (h + 1) * MEM_HEAD_DIM)
        qh = (_head_rms(zq_ref[0, :, sl].astype(F32), qg_ref[...])
              * ((MEM_HEAD_DIM ** -0.5) * LOG2E)).astype(BF16)
        s = lax.dot_general(qh, kn_scr[h], (((1,), (1,)), ((), ())), preferred_element_type=F32)
        m = jnp.max(s, axis=1, keepdims=True)
        e = jnp.exp2(s - m)
        p = (e / jnp.sum(e, axis=1, keepdims=True)).astype(BF16)
        o_ref[0, :, sl] = jnp.dot(p, vb_scr[h], preferred_element_type=F32).astype(o_ref.dtype)


def _mem_attention(z, mem_kv, q_g, k_g, *, batch, seq):
    mlen = mem_kv.shape[1]
    tq = TQ
    return pl.pallas_call(
        _mem_kernel,
        grid=(batch, seq // tq),
        in_specs=[pl.BlockSpec((1, tq, MEM_WIDTH), lambda b, i: (b, i, C_MEM_Q // MEM_WIDTH)),
                  pl.BlockSpec((1, mlen, 2 * MEM_WIDTH), lambda b, i: (b, 0, 0)),
                  pl.BlockSpec((1, MEM_HEAD_DIM), lambda b, i: (0, 0)),
                  pl.BlockSpec((1, MEM_HEAD_DIM), lambda b, i: (0, 0))],
        out_specs=pl.BlockSpec((1, tq, MEM_WIDTH), lambda b, i: (b, i, 0)),
        out_shape=jax.ShapeDtypeStruct((batch, seq, MEM_WIDTH), BF16),
        scratch_shapes=[pltpu.VMEM((MEM_HEADS, mlen, MEM_HEAD_DIM), BF16),
                        pltpu.VMEM((MEM_HEADS, mlen, MEM_HEAD_DIM), BF16)],
        compiler_params=_cparams(("parallel", "arbitrary")),
        name="mem_attention",
    )(z, mem_kv, q_g.reshape(1, MEM_HEAD_DIM), k_g.reshape(1, MEM_HEAD_DIM))


def _sigmoid(x):
    return 0.5 * jnp.tanh(0.5 * x) + 0.5


def _out_kernel(x_ref, oc_ref, os_ref, ow_ref, ob_ref, om_ref, sa_ref, sb_ref, sm_ref,
                g0_ref, g1_ref, g2_ref, wa_ref, wb_ref, wm_ref, wo_ref, y_ref):
    silu = lambda ref: (lambda x: x * _sigmoid(x))(ref[...].astype(F32))
    oa = (oc_ref[...].astype(F32) + os_ref[...].astype(F32) + ow_ref[...].astype(F32)) * silu(sa_ref)
    ob = ob_ref[...].astype(F32) * silu(sb_ref)
    om = om_ref[...].astype(F32) * silu(sm_ref)
    gate = lambda ref: _sigmoid(ref[...].astype(F32))
    u = (gate(g0_ref) * jnp.dot(oa.astype(BF16), wa_ref[...], preferred_element_type=F32)
         + gate(g1_ref) * jnp.dot(ob.astype(BF16), wb_ref[...], preferred_element_type=F32)
         + gate(g2_ref) * jnp.dot(om.astype(BF16), wm_ref[...], preferred_element_type=F32))
    y_ref[...] = x_ref[...] + jnp.dot(u.astype(BF16), wo_ref[...], preferred_element_type=F32)


def _out_proj(x2d, o_c, o_s, o_w, o_b, o_m, z2d, wa, wb, wm, wo, layer):
    m = x2d.shape[0]
    tm = 256
    w512 = 512
    row512 = lambda c: pl.BlockSpec((tm, w512), lambda i, c=c: (i, c))
    row1024 = lambda c: pl.BlockSpec((tm, D_MODEL), lambda i, c=c: (i, c))
    full = lambda shape: pl.BlockSpec((None,) + shape, lambda i: (layer, 0, 0))
    return pl.pallas_call(
        _out_kernel,
        grid=(m // tm,),
        in_specs=[row1024(0), row512(0), row512(0), row512(0), row512(0), row512(0),
                  row512(C_NSA_SILU // w512), row512(C_FOX_SILU // w512), row512(C_MEM_SILU // w512),
                  row1024(C_MERGE // D_MODEL), row1024(C_MERGE // D_MODEL + 1),
                  row1024(C_MERGE // D_MODEL + 2),
                  full((w512, D_MODEL)), full((w512, D_MODEL)), full((w512, D_MODEL)),
                  full((D_MODEL, D_MODEL))],
        out_specs=row1024(0),
        out_shape=jax.ShapeDtypeStruct((m, D_MODEL), F32),
        compiler_params=_cparams(("parallel",)),
        name="out_proj",
    )(x2d, o_c, o_s, o_w, o_b, o_m, z2d, z2d, z2d, z2d, z2d, z2d, wa, wb, wm, wo)


def _fox_head_order(a, order, sections):
    nl, d, _ = a.shape
    a = a.reshape(nl, d, sections, FOX_HEADS, HEAD_DIM)
    return jnp.take_along_axis(a, order[:, None, None, :, None], axis=3).reshape(nl, d, -1)


def _permute_w_in(w_in, fox_order):
    o = np.cumsum([0, NSA_WIDTH, 6 * NSA_KV_GROUPS * HEAD_DIM, 3 * NSA_HEADS, NSA_WIDTH,
                   3 * FOX_WIDTH, FOX_HEADS, FOX_WIDTH, MEM_WIDTH, MEM_WIDTH, 3 * D_MODEL])
    nsa_q, nsa_kv, nsa_gate, nsa_silu, fox_qkv, fox_f, fox_silu, mem_q, mem_silu, merge = [
        w_in[..., o[i]:o[i + 1]] for i in range(10)]
    fox_qkv = _fox_head_order(fox_qkv, fox_order, 3)
    fox_silu = _fox_head_order(fox_silu, fox_order, 1)
    fox_f = jnp.take_along_axis(fox_f, fox_order[:, None, :], axis=2)
    pad = jnp.zeros(w_in.shape[:2] + (LANES - 3 * NSA_HEADS - FOX_HEADS,), w_in.dtype)
    return jnp.concatenate([nsa_q, nsa_silu, fox_qkv, fox_silu, mem_q, mem_silu, merge, nsa_kv,
                            nsa_gate, fox_f, pad], axis=-1).astype(BF16)


def _pad_lanes(x, width=LANES):
    return jnp.pad(x, [(0, 0)] * (x.ndim - 1) + [(0, width - x.shape[-1])])


def _tile2(g):
    return jnp.concatenate([g, g], axis=-1)[:, None, :]


def _overlap(n_cmp_pad, n_slc):
    cs = np.arange(n_cmp_pad)[:, None] * NSA_CMP_STRIDE
    ss = np.arange(NSLC_PAD)[None, :] * NSA_SEL_LEN
    ov = np.clip(np.minimum(cs + NSA_CMP_LEN, ss + NSA_SEL_LEN) - np.maximum(cs, ss), 0, None)
    ov = ov / NSA_CMP_LEN
    ov[-1] = 0.0
    ov[:, n_slc:] = 0.0
    return jnp.asarray(ov, BF16)


def _aug_permutations():
    pq = np.zeros((6 * LANES, FOX_HEADS * LANES), np.float32)
    pk = np.zeros((6 * LANES, FOX_HEADS * LANES), np.float32)
    for h in range(FOX_HEADS):
        for i in range(3):
            pq[i * LANES + F_LANE + h, h * LANES + AUG + i] = 1.0
            pk[i * LANES + F_LANE + h, h * LANES + AUG + 3 + i] = -1.0
            pk[(3 + i) * LANES + F_LANE + h, h * LANES + AUG + 6 + i] = -1.0
    return jnp.asarray(pq, BF16), jnp.asarray(pk, BF16)


def _compress_w1(w1):
    nl = w1.shape[0]
    halves = w1.reshape(nl, 2, NSA_CMP_STRIDE, 1, HEAD_DIM, NSA_CMP_HIDDEN)
    eye = jnp.eye(NSA_KV_GROUPS, dtype=w1.dtype)
    out = jnp.einsum("pg,zhldn->zlpdghn", eye, halves[:, :, :, 0])
    return out.reshape(nl, NSA_CMP_STRIDE * NSA_KV_GROUPS * HEAD_DIM,
                       NSA_KV_GROUPS * 2 * NSA_CMP_HIDDEN).astype(BF16)


def _layer(x, mem, p, big, layer, consts):
    batch, seq, _ = x.shape
    m_rows = batch * seq
    cos4, sin_signed, tri, head_ones, perm_q, perm_k, ov = consts
    z2d = _norm_matmul(x.reshape(m_rows, D_MODEL), p["norm_g"], big["w_in"], layer, 512, Z_WIDTH // 3, BF16)
    z = z2d.reshape(batch, seq, Z_WIDTH)

    (qn, kcr, vcr, ks, vs, kw, vw, fqd, fqo, fka, base) = _prep(
        z, cos4, sin_signed, tri, head_ones, perm_q, perm_k,
        p["gq"], p["gks"], p["gkw"], p["gfq"], p["gfk"], p["fb"],
        p["s_sel"], p["s_win"], p["s_fox"], batch=batch, seq=seq)

    nchunk = seq // NSA_CMP_STRIDE
    kc, vc = _compress(kcr.reshape(batch, nchunk, NSA_CMP_STRIDE * LANES),
                       vcr.reshape(batch, nchunk, NSA_CMP_STRIDE * LANES),
                       p["w1k"], p["w1v"], p["bk"], p["bv"], p["w2k"], p["w2v"], p["gkc"], p["s_cmp"],
                       batch=batch)

    def attend(fixed_shift):
        o_c, unsel = _cmp_select(qn, kc, vc, ov, z, batch=batch, seq=seq, fixed_shift=fixed_shift)
        o_s = _flash(qn, ks, vs, batch=batch, seq=seq, shared_kv=True, fixed_shift=fixed_shift,
                     u=unsel, z=z, gate_branch=1)
        o_w = _flash(qn, kw, vw, batch=batch, seq=seq, shared_kv=True, fixed_shift=fixed_shift,
                     window=NSA_WINDOW, z=z, gate_branch=2)
        o_b = _flash(fqd, fka, z, batch=batch, seq=seq, shared_kv=False, fixed_shift=fixed_shift,
                     nheads=FOX_HEADS // 2, base=base.reshape(-1), q_off=fqo, v_offset=C_FOX_V)
        return o_c, o_s, o_w, o_b

    o_c, o_s, o_w, o_b = lax.cond(p["bound_ok"], lambda: attend(True), lambda: attend(False))

    mlen = mem.shape[1]
    mem_kv = _norm_matmul(mem.reshape(batch * mlen, D_MODEL), p["mem_norm_g"], big["w_mem_kv"], layer,
                          min(512, batch * mlen), 2 * MEM_WIDTH, F32).reshape(batch, mlen, 2 * MEM_WIDTH)
    o_m = _mem_attention(z, mem_kv, p["mem_q_norm"], p["mem_k_norm"], batch=batch, seq=seq)

    y = _out_proj(x.reshape(m_rows, D_MODEL), o_c.reshape(m_rows, -1), o_s.reshape(m_rows, -1),
                  o_w.reshape(m_rows, -1), o_b.reshape(m_rows, -1), o_m.reshape(m_rows, -1), z2d,
                  big["w_branch_a"], big["w_branch_b"], big["w_branch_m"], big["w_out"], layer)
    return y.reshape(batch, seq, D_MODEL)


def kernel(x, mem, norm_g, mem_norm_g, w_in, nsa_q_norm, nsa_k_norm, cmp_pe_k, cmp_w1_k, cmp_w2_k,
           cmp_pe_v, cmp_w1_v, cmp_w2_v, fox_q_norm, fox_k_norm, fox_f_bias, mem_q_norm, mem_k_norm,
           w_mem_kv, w_branch_a, w_branch_b, w_branch_m, w_out):
    batch, seq, _ = x.shape
    depth = w_in.shape[0]
    n_slc = seq // NSA_SEL_LEN
    assert n_slc <= NSLC_PAD and seq % TQ == 0 and NSA_WINDOW == TQ

    half = HEAD_DIM // 2
    inv_freq = ROPE_THETA ** (-jnp.arange(half, dtype=F32) / half)
    ang = jnp.arange(seq).astype(F32)[:, None] * inv_freq[None, :]
    cos, sin = jnp.cos(ang), jnp.sin(ang)
    cos4 = jnp.concatenate([cos, cos, cos, cos], axis=-1)
    sin_signed = jnp.concatenate([-sin, sin, -sin, sin], axis=-1)
    tri = jnp.asarray(np.tril(np.ones((TQ, TQ), np.float32)), BF16)
    head_ones = jnp.asarray(np.kron(np.eye(2), np.ones((HEAD_DIM, HEAD_DIM))), BF16)
    consts = (cos4, sin_signed, tri, head_ones, *_aug_permutations(),
              _overlap(seq // NSA_CMP_STRIDE, n_slc))

    hp = lax.Precision.HIGHEST
    pe_bias = lambda pe, w1: jnp.einsum("lk,lkn->ln", pe.reshape(depth, -1), w1, precision=hp)[:, None, :]
    fox_order = jnp.argsort(fox_f_bias, axis=-1)
    fox_bias_sorted = jnp.take_along_axis(fox_f_bias, fox_order, axis=-1)
    w_branch_b_sorted = jnp.take_along_axis(
        w_branch_b.reshape(depth, FOX_HEADS, HEAD_DIM, D_MODEL), fox_order[:, :, None, None], axis=1
    ).reshape(depth, FOX_WIDTH, D_MODEL)
    fb = jnp.zeros((depth, 1, LANES), F32).at[:, 0, F_LANE:F_LANE + FOX_HEADS].set(fox_bias_sorted)

    def logit_bound(gq, gk):
        bound = (BOUND_MARGIN * HEAD_DIM * QK_SCALE
                 * jnp.max(jnp.abs(gq), axis=-1) * jnp.max(jnp.abs(gk), axis=-1))
        return bound.astype(BF16).astype(F32)

    bounds = [logit_bound(nsa_q_norm, nsa_k_norm[:, 1]), logit_bound(nsa_q_norm, nsa_k_norm[:, 2]),
              logit_bound(fox_q_norm, fox_k_norm), logit_bound(nsa_q_norm, nsa_k_norm[:, 0])]
    bound_ok = functools.reduce(jnp.maximum, bounds) <= MAX_FIXED_BOUND
    bound_row = lambda s: jnp.broadcast_to(s[:, None, None], (depth, 1, LANES))
    stacked = dict(
        s_sel=bound_row(bounds[0]), s_win=bound_row(bounds[1]), s_fox=bound_row(bounds[2]),
        s_cmp=bound_row(bounds[3]),
        bound_ok=bound_ok,
        norm_g=norm_g, mem_norm_g=mem_norm_g,
        gq=_tile2(nsa_q_norm), gks=_tile2(nsa_k_norm[:, 1]), gkw=_tile2(nsa_k_norm[:, 2]),
        gkc=_pad_lanes(nsa_k_norm[:, 0])[:, None, :], gfq=_tile2(fox_q_norm), gfk=_tile2(fox_k_norm),
        fb=fb, w1k=_compress_w1(cmp_w1_k), w1v=_compress_w1(cmp_w1_v),
        bk=pe_bias(cmp_pe_k, cmp_w1_k), bv=pe_bias(cmp_pe_v, cmp_w1_v),
        w2k=_pad_lanes(cmp_w2_k).astype(BF16),
        w2v=jnp.concatenate([cmp_w2_v, cmp_w2_v], axis=-1).astype(BF16),
        mem_q_norm=mem_q_norm, mem_k_norm=mem_k_norm)
    big = dict(w_in=_permute_w_in(w_in, fox_order), w_mem_kv=w_mem_kv.astype(BF16),
               w_branch_a=w_branch_a.astype(BF16), w_branch_b=w_branch_b_sorted.astype(BF16),
               w_branch_m=w_branch_m.astype(BF16), w_out=w_out.astype(BF16))
    for l in range(depth):
        x = _layer(x, mem, {k: v[l] for k, v in stacked.items()}, big, l, consts)
    return x
```

```python
import functools

import numpy as np
import jax
import jax.numpy as jnp
from jax import lax
from jax.experimental import pallas as pl
from jax.experimental.pallas import tpu as pltpu

F32 = jnp.float32
BF16 = jnp.bfloat16

D_MODEL = 1024
HEAD_DIM = 64
ROPE_THETA = 10000.0
EPS = 1e-6
NSA_HEADS = 8
NSA_KV_GROUPS = 2
NSA_GROUP_HEADS = NSA_HEADS // NSA_KV_GROUPS
NSA_CMP_LEN = 32
NSA_CMP_STRIDE = 16
NSA_CMP_HIDDEN = 128
NSA_SEL_LEN = 64
NSA_SEL_TOPK = 16
NSA_WINDOW = 512
FORCE_SCORE = 1e9
NSA_WIDTH = NSA_HEADS * HEAD_DIM
FOX_HEADS = 8
FOX_WIDTH = FOX_HEADS * HEAD_DIM
MEM_HEADS = 4
MEM_HEAD_DIM = 128
MEM_WIDTH = MEM_HEADS * MEM_HEAD_DIM

LANES = 128
VMEM_LIMIT = 56 * 1024 * 1024
NEG = -1e30
SEL_MASK = 32768.0
SEL_SHIFT = 6
NSLC_PAD = LANES
LOG2E = 1.4426950408889634
QK_SCALE = (HEAD_DIM ** -0.5) * LOG2E
BOUND_MARGIN = 1.02
MAX_FIXED_BOUND = 60.0
ZERO_WEIGHT_LOG2 = -160.0

C_NSA_Q = 0
C_NSA_SILU = 512
C_FOX_Q = 1024
C_FOX_K = 1536
C_FOX_V = 2048
C_FOX_SILU = 2560
C_MEM_Q = 3072
C_MEM_SILU = 3584
C_MERGE = 4096
C_NSA_KV = 7168
C_SMALL = 7936
Z_WIDTH = 8064
F_LANE = 3 * NSA_HEADS
AUG = HEAD_DIM

TQ = 512
FLASH_ROWS = 4
SAFE_FLASH_ROWS = 2
FOX_GROUP_HEADS = 2
CMP_TQ = 512


def _cparams(sem):
    return pltpu.CompilerParams(dimension_semantics=sem, vmem_limit_bytes=VMEM_LIMIT)


def _norm_matmul_kernel(x_ref, g_ref, w_ref, o_ref, *, tn):
    x = x_ref[...]
    ms = jnp.mean(x * x, axis=-1, keepdims=True)
    h = (x * lax.rsqrt(ms + EPS) * g_ref[...]).astype(BF16)
    for c0 in range(0, w_ref.shape[1], tn):
        o_ref[:, c0:c0 + tn] = jnp.dot(h, w_ref[:, c0:c0 + tn],
                                       preferred_element_type=F32).astype(o_ref.dtype)


def _norm_matmul(x2d, g, w, layer, tm, tn, out_dtype):
    m, k = x2d.shape
    n = w.shape[2]
    return pl.pallas_call(
        functools.partial(_norm_matmul_kernel, tn=tn),
        grid=(m // tm,),
        in_specs=[pl.BlockSpec((tm, k), lambda i: (i, 0)),
                  pl.BlockSpec((1, k), lambda i: (0, 0)),
                  pl.BlockSpec((None, k, n), lambda i: (layer, 0, 0), pipeline_mode=pl.Buffered(1))],
        out_specs=pl.BlockSpec((tm, n), lambda i: (i, 0)),
        out_shape=jax.ShapeDtypeStruct((m, n), out_dtype),
        compiler_params=_cparams(("parallel",)),
        name="norm_matmul",
    )(x2d, g.reshape(1, k), w)


def _pair_rms(x, g2, head_ones):
    sq = x * x
    hi = sq.astype(BF16)
    low = (sq - hi.astype(F32)).astype(BF16)
    ssq = (jnp.dot(hi, head_ones, preferred_element_type=F32)
           + jnp.dot(low, head_ones, preferred_element_type=F32))
    return x * lax.rsqrt(ssq * (1.0 / HEAD_DIM) + EPS) * g2


def _pair_rope(y, cos4, sin_signed, first_half):
    rot = jnp.where(first_half, pltpu.roll(y, LANES - HEAD_DIM // 2, 1), pltpu.roll(y, HEAD_DIM // 2, 1))
    return y * cos4 + rot * sin_signed


def _head_a(y, lo, tail=0.0):
    return jnp.where(lo, y, tail)


def _head_b(y, lo, tail=0.0):
    return jnp.where(lo, pltpu.roll(y, HEAD_DIM, 1), tail)


def _split3_f32(c):
    hi = c.astype(BF16).astype(F32)
    r = c - hi
    mid = r.astype(BF16).astype(F32)
    return hi, mid, r - mid


def _prep_kernel(zq_ref, zc_ref, zs_ref, zw_ref, fq_ref, fk_ref, sm_ref, cos_ref, sin_ref,
                 tri_ref, ones_ref, pq_ref, pk_ref, gq_ref, gks_ref, gkw_ref, gfq_ref, gfk_ref, fb_ref,
                 ss_ref, sw_ref, sf_ref,
                 qn_ref, kcr_ref, vcr_ref, ks_ref, vs_ref, kw_ref, vw_ref, fqd_ref, fqo_ref, fka_ref,
                 base_ref, run_scr):
    ti = pl.program_id(1)
    tt = zq_ref.shape[1]
    lane = lax.broadcasted_iota(jnp.int32, (tt, LANES), 1)
    lo = lane < HEAD_DIM
    first_half = (lane & (HEAD_DIM // 2)) == 0
    cos4 = cos_ref[...]
    sin_signed = sin_ref[...]
    head_ones = ones_ref[...]
    one_at_aug = jnp.where(lane == AUG, 1.0, 0.0)

    for p in range(NSA_HEADS // 2):
        y = _pair_rope(_pair_rms(zq_ref[0, :, p * LANES:(p + 1) * LANES].astype(F32), gq_ref[...], head_ones),
                       cos4, sin_signed, first_half) * QK_SCALE
        qn_ref[0, 2 * p] = _head_a(y, lo, one_at_aug).astype(BF16)
        qn_ref[0, 2 * p + 1] = _head_b(y, lo, one_at_aug).astype(BF16)

    kcr_ref[0] = _pair_rope(zc_ref[0, :, :LANES].astype(F32), cos4, sin_signed, first_half).astype(BF16)
    vcr_ref[0] = zc_ref[0, :, LANES:].astype(BF16)

    sel_lane = jnp.right_shift(ti * tt + lax.broadcasted_iota(jnp.int32, (tt, LANES), 0), SEL_SHIFT)
    onehot = jnp.where(lane == sel_lane, -SEL_MASK, 0.0).astype(BF16)
    for z_ref, g_ref, k_ref, v_ref, bound_ref, with_onehot in (
            (zs_ref, gks_ref, ks_ref, vs_ref, ss_ref, True),
            (zw_ref, gkw_ref, kw_ref, vw_ref, sw_ref, False)):
        y = _pair_rope(_pair_rms(z_ref[0, :, :LANES].astype(F32), g_ref[...], head_ones),
                       cos4, sin_signed, first_half)
        v = z_ref[0, :, LANES:].astype(F32)
        vr = pltpu.roll(v, HEAD_DIM, 1)
        neg_bound = jnp.where(lane == AUG, -bound_ref[...], 0.0)
        k_ref[0, 0, :, :LANES] = _head_a(y, lo, neg_bound).astype(BF16)
        k_ref[0, 1, :, :LANES] = _head_b(y, lo, neg_bound).astype(BF16)
        if with_onehot:
            k_ref[0, 0, :, LANES:] = onehot
            k_ref[0, 1, :, LANES:] = onehot
        v_ref[0, 0] = jnp.where(lo, v, vr).astype(BF16)
        v_ref[0, 1] = jnp.where(lo, vr, v).astype(BF16)

    @pl.when(ti == 0)
    def _reset():
        run_scr[...] = jnp.zeros(run_scr.shape, F32)

    xf = sm_ref[0].astype(F32) + fb_ref[...]
    log_f = (jnp.minimum(xf, 0.0) - jnp.log1p(jnp.exp(-jnp.abs(xf)))) * LOG2E
    tri = tri_ref[...]
    parts = _split3_f32(log_f)
    cum = sum(jnp.dot(tri, part.astype(BF16), preferred_element_type=F32) for part in parts)
    base_ref[0, 0] = run_scr[...]
    run_scr[...] = run_scr[...] + cum[tt - 1:tt, :]

    parts6 = _split3_f32(cum) + _split3_f32(cum - cum[tt - 1:tt, :])
    x6 = jnp.concatenate([part.astype(BF16) for part in parts6], axis=1)
    aug_q = jnp.dot(x6, pq_ref[...], preferred_element_type=F32)
    aug_k = jnp.dot(x6, pk_ref[...], preferred_element_type=F32)

    lane1 = lax.broadcasted_iota(jnp.int32, (1, LANES), 1)
    span = lambda a, b: jnp.where((lane1 >= a) & (lane1 < b), 1.0, 0.0)
    q_tail_diag = span(AUG + 3, AUG + 6) + span(AUG + 9, AUG + 10)
    q_tail_off = span(AUG + 6, AUG + 10)
    k_tail = span(AUG, AUG + 3) + jnp.where(lane1 == AUG + 9, -sf_ref[...], 0.0)
    for p in range(FOX_HEADS // 2):
        sl = slice(p * LANES, (p + 1) * LANES)
        yq = _pair_rms(fq_ref[0, :, sl].astype(F32), gfq_ref[...], head_ones) * QK_SCALE
        yk = _pair_rms(fk_ref[0, :, sl].astype(F32), gfk_ref[...], head_ones)
        for e, split in ((0, _head_a), (1, _head_b)):
            h = 2 * p + e
            hs = slice(h * LANES, (h + 1) * LANES)
            qh = split(yq, lo) + aug_q[:, hs]
            fqd_ref[0, h] = (qh + q_tail_diag).astype(BF16)
            fqo_ref[0, h] = (qh + q_tail_off).astype(BF16)
            fka_ref[0, h] = (split(yk, lo) + aug_k[:, hs] + k_tail).astype(BF16)


def _prep(z, cos4, sin_signed, tri, head_ones, perm_q, perm_k, gq, gks, gkw, gfq, gfk, fb,
          s_sel, s_win, s_fox, *, batch, seq):
    nblk = seq // TQ
    whole = lambda a: pl.BlockSpec(a.shape, lambda b, i: (0, 0))
    zspec = lambda w, c: pl.BlockSpec((1, TQ, w), lambda b, i, c=c: (b, i, c))
    row = pl.BlockSpec((1, LANES), lambda b, i: (0, 0))
    tab = pl.BlockSpec((TQ, LANES), lambda b, i: (i, 0))
    heads = lambda n, w: pl.BlockSpec((1, n, TQ, w), lambda b, i: (b, 0, i, 0))
    tok = lambda w: pl.BlockSpec((1, TQ, w), lambda b, i: (b, i, 0))
    bf = lambda *shape: jax.ShapeDtypeStruct(shape, BF16)
    return pl.pallas_call(
        _prep_kernel,
        grid=(batch, nblk),
        in_specs=[zspec(NSA_WIDTH, C_NSA_Q // NSA_WIDTH),
                  zspec(2 * LANES, C_NSA_KV // (2 * LANES)),
                  zspec(2 * LANES, C_NSA_KV // (2 * LANES) + 1),
                  zspec(2 * LANES, C_NSA_KV // (2 * LANES) + 2),
                  zspec(FOX_WIDTH, C_FOX_Q // FOX_WIDTH),
                  zspec(FOX_WIDTH, C_FOX_K // FOX_WIDTH),
                  zspec(LANES, C_SMALL // LANES),
                  tab, tab, whole(tri), whole(head_ones), whole(perm_q), whole(perm_k),
                  row, row, row, row, row, row, row, row, row],
        out_specs=[heads(NSA_HEADS, LANES), tok(LANES), tok(LANES),
                   heads(NSA_KV_GROUPS, 2 * LANES), heads(NSA_KV_GROUPS, LANES),
                   heads(NSA_KV_GROUPS, LANES), heads(NSA_KV_GROUPS, LANES),
                   heads(FOX_HEADS, LANES), heads(FOX_HEADS, LANES), heads(FOX_HEADS, LANES),
                   pl.BlockSpec((1, 1, 1, LANES), lambda b, i: (b, i, 0, 0))],
        out_shape=[bf(batch, NSA_HEADS, seq, LANES), bf(batch, seq, LANES), bf(batch, seq, LANES),
                   bf(batch, NSA_KV_GROUPS, seq, 2 * LANES), bf(batch, NSA_KV_GROUPS, seq, LANES),
                   bf(batch, NSA_KV_GROUPS, seq, LANES), bf(batch, NSA_KV_GROUPS, seq, LANES),
                   bf(batch, FOX_HEADS, seq, LANES), bf(batch, FOX_HEADS, seq, LANES),
                   bf(batch, FOX_HEADS, seq, LANES),
                   jax.ShapeDtypeStruct((batch, nblk, 1, LANES), F32)],
        scratch_shapes=[pltpu.VMEM((1, LANES), F32)],
        compiler_params=_cparams(("parallel", "arbitrary")),
        name="prep",
    )(z, z, z, z, z, z, z, cos4, sin_signed, tri, head_ones, perm_q, perm_k,
      gq, gks, gkw, gfq, gfk, fb, s_sel, s_win, s_fox)


def _compress_kernel(xk_ref, xv_ref, w1k_ref, w1v_ref, bk_ref, bv_ref, w2k_ref, w2v_ref, kg_ref,
                     sc_ref, kc_ref, vc_ref):
    nchunk = xk_ref.shape[1]
    lane = lax.broadcasted_iota(jnp.int32, (nchunk, LANES), 1)
    for x_ref, w1_ref, b_ref, w2_ref, o_ref, is_key in ((xk_ref, w1k_ref, bk_ref, w2k_ref, kc_ref, True),
                                                        (xv_ref, w1v_ref, bv_ref, w2v_ref, vc_ref, False)):
        h = jnp.dot(x_ref[0], w1_ref[...], preferred_element_type=F32)
        for g in range(NSA_KV_GROUPS):
            c0 = g * 2 * NSA_CMP_HIDDEN
            top = h[:, c0:c0 + NSA_CMP_HIDDEN]
            bot = h[:, c0 + NSA_CMP_HIDDEN:c0 + 2 * NSA_CMP_HIDDEN]
            hid = top + pltpu.roll(bot, nchunk - 1, 0) + b_ref[...]
            act = (hid * jax.nn.sigmoid(hid)).astype(BF16)
            o = jnp.dot(act, w2_ref[...], preferred_element_type=F32)
            if is_key:
                ms = jnp.sum(o * o, axis=1, keepdims=True) * (1.0 / HEAD_DIM)
                o = o * lax.rsqrt(ms + EPS) * kg_ref[...]
                o = jnp.where(lane == AUG, -sc_ref[...], o)
            o_ref[0, g] = o.astype(BF16)


def _compress(xk, xv, w1k, w1v, bk, bv, w2k, w2v, kg, s_cmp, *, batch):
    nchunk, kin = xk.shape[1], xk.shape[2]
    xspec = pl.BlockSpec((1, nchunk, kin), lambda b: (b, 0, 0))
    full = lambda a: pl.BlockSpec(a.shape, lambda b: (0,) * a.ndim)
    ospec = pl.BlockSpec((1, NSA_KV_GROUPS, nchunk, LANES), lambda b: (b, 0, 0, 0))
    oshape = jax.ShapeDtypeStruct((batch, NSA_KV_GROUPS, nchunk, LANES), BF16)
    return pl.pallas_call(
        _compress_kernel,
        grid=(batch,),
        in_specs=[xspec, xspec, full(w1k), full(w1v), full(bk), full(bv), full(w2k), full(w2v), full(kg),
                  full(s_cmp)],
        out_specs=[ospec, ospec],
        out_shape=[oshape, oshape],
        compiler_params=_cparams(("parallel",)),
        name="compress",
    )(xk, xv, w1k, w1v, bk, bv, w2k, w2v, kg, s_cmp)


def _gate_column(gate_blk, lane, idx):
    col = jnp.sum(jnp.where(lane == idx, gate_blk, 0.0), axis=1, keepdims=True)
    return jax.nn.sigmoid(col)


def _flash_kernel(qi_tab, ki_tab, fl_tab, *refs, nheads, shared_kv, window, has_delta,
                  has_u, gate_branch, nblk, fixed_shift, nb):
    refs = list(refs)
    base_ref = refs.pop(0) if has_delta else None
    q_ref = refs.pop(0)
    qoff_ref = refs.pop(0) if has_delta else None
    u_ref = refs.pop(0) if has_u else None
    k_ref = refs.pop(0)
    v_ref = refs.pop(0)
    gate_ref = refs.pop(0) if gate_branch is not None else None
    o_ref = refs.pop(0)
    m_scr = refs.pop(0)
    l_scr = refs.pop(0)
    acc_scr = refs.pop(0)
    qcat_scr = refs.pop(0) if has_u else None

    b = pl.program_id(0)
    g = pl.program_id(1)
    step = (b * pl.num_programs(1) + g) * pl.num_programs(2) + pl.program_id(2)
    qi = qi_tab[step]
    ki = ki_tab[step]
    fl = fl_tab[step]
    active = (fl & 4) == 4
    tq = q_ref.shape[2]
    tk = k_ref.shape[2]

    @pl.when((fl & 1) == 1)
    def _init():
        m_scr[...] = jnp.full(m_scr.shape, NEG, F32)
        l_scr[...] = jnp.zeros(l_scr.shape, F32)
        acc_scr[...] = jnp.zeros(acc_scr.shape, F32)
        if has_u:
            for bb in range(nb):
                for r in range(nheads):
                    qcat_scr[bb * nheads + r, :, :LANES] = q_ref[bb, r]
                    qcat_scr[bb * nheads + r, :, LANES:] = u_ref[bb, r // NSA_GROUP_HEADS]

    lane = lax.broadcasted_iota(jnp.int32, (tq, LANES), 1)
    lo = lane < HEAD_DIM

    def tile(diagonal):
        masked = diagonal or window is not None
        if masked:
            row = lax.broadcasted_iota(jnp.int32, (tq, tk), 0)
            col = lax.broadcasted_iota(jnp.int32, (tq, tk), 1)
            if diagonal:
                valid = row >= col
            else:
                valid = (tq + row - col) < window
        for bb in range(nb):
            slot0, pair0, brow = bb * nheads, bb * (nheads // 2), b * nb + bb
            for p in range(nheads // 2):
                pvs, alphas = [], []
                for e in range(2):
                    hh = 2 * p + e
                    if has_u:
                        q = qcat_scr[slot0 + hh]
                    elif has_delta and not diagonal:
                        q = qoff_ref[bb, hh]
                    else:
                        q = q_ref[bb, hh]
                    k = k_ref[bb, hh // NSA_GROUP_HEADS] if shared_kv else k_ref[bb, hh]
                    v = (v_ref[bb, hh // NSA_GROUP_HEADS] if shared_kv
                         else v_ref[bb, :, p * LANES:(p + 1) * LANES])
                    s = lax.dot_general(q, k, (((1,), (1,)), ((), ())), preferred_element_type=F32)
                    if masked:
                        s = jnp.where(valid, s, NEG)
                    delta = None
                    if has_delta and not diagonal:
                        hoff = F_LANE + g * nheads + hh
                        delta = (base_ref[(brow * nblk + qi) * LANES + hoff]
                                 - base_ref[(brow * nblk + ki + 1) * LANES + hoff])
                    if fixed_shift:
                        pm = jnp.exp2(s)
                        rowsum = pm[:, :LANES]
                        for j in range(1, tk // LANES):
                            rowsum = rowsum + pm[:, j * LANES:(j + 1) * LANES]
                        pv = jnp.dot(pm.astype(BF16), v, preferred_element_type=F32)
                        if delta is not None:
                            w = jnp.exp2(jnp.full((1, LANES), delta, F32))
                            rowsum = rowsum * w
                            pv = pv * w
                        l_scr[slot0 + hh] = l_scr[slot0 + hh] + rowsum
                        pvs.append(pv)
                    else:
                        m_prev = m_scr[slot0 + hh]
                        m_cur = jnp.max(s, axis=1, keepdims=True)
                        if delta is not None:
                            m_cur = m_cur + delta
                        m_new = jnp.maximum(m_prev, m_cur)
                        alpha = jnp.exp2(m_prev - m_new)
                        shift = m_new[:, :1]
                        if delta is not None:
                            shift = shift - delta
                        pm = jnp.exp2(s - shift)
                        l_scr[slot0 + hh] = alpha * l_scr[slot0 + hh] + jnp.sum(pm, axis=1, keepdims=True)
                        m_scr[slot0 + hh] = m_new
                        pvs.append(jnp.dot(pm.astype(BF16), v, preferred_element_type=F32))
                        alphas.append(alpha)
                if fixed_shift:
                    acc_scr[pair0 + p] = acc_scr[pair0 + p] + jnp.where(lo, pvs[0], pvs[1])
                else:
                    acc_scr[pair0 + p] = (acc_scr[pair0 + p] * jnp.where(lo, alphas[0], alphas[1])
                                          + jnp.where(lo, pvs[0], pvs[1]))

    pl.when((qi == ki) & active)(lambda: tile(True))
    pl.when((qi != ki) & active)(lambda: tile(False))

    @pl.when((fl & 2) == 2)
    def _finish():
        for bb in range(nb):
            slot0, pair0 = bb * nheads, bb * (nheads // 2)
            for p in range(nheads // 2):
                if fixed_shift:
                    la = jnp.sum(l_scr[slot0 + 2 * p], axis=1, keepdims=True)
                    lb = jnp.sum(l_scr[slot0 + 2 * p + 1], axis=1, keepdims=True)
                else:
                    la, lb = l_scr[slot0 + 2 * p], l_scr[slot0 + 2 * p + 1]
                inv = jnp.where(lo, 1.0 / la, 1.0 / lb)
                if gate_branch is not None:
                    h0 = g * nheads + 2 * p
                    src = lax.broadcasted_iota(jnp.int32, (LANES, LANES), 0)
                    dst_lo = lax.broadcasted_iota(jnp.int32, (LANES, LANES), 1) < HEAD_DIM
                    pick = jnp.where(src == jnp.where(dst_lo, 3 * h0 + gate_branch,
                                                      3 * (h0 + 1) + gate_branch), 1.0, 0.0)
                    logits = jnp.dot(gate_ref[bb].astype(BF16), pick.astype(BF16),
                                     preferred_element_type=F32)
                    inv = inv * jax.nn.sigmoid(logits)
                o_ref[bb, :, p * LANES:(p + 1) * LANES] = (acc_scr[pair0 + p] * inv).astype(o_ref.dtype)


def _tile_tables(nq, window_tiles, batch):
    qi, ki, fl = [], [], []
    for i in range(nq):
        lo = 0 if window_tiles is None else max(0, i - window_tiles)
        for j in range(lo, i + 1):
            qi.append(i)
            ki.append(j)
            fl.append((1 if j == lo else 0) | (2 if j == i else 0) | 4)
    rows = lambda a: jnp.tile(jnp.asarray(a, jnp.int32), batch)
    return rows(qi), rows(ki), rows(fl)


def _decayed_tile_tables(nq, batch, nb, ngroups, base):
    qi = np.repeat(np.arange(nq), np.arange(1, nq + 1))
    ki = np.concatenate([np.arange(i + 1) for i in range(nq)])
    nsteps = qi.shape[0]
    diag = jnp.asarray(qi == ki)
    gate = base.reshape(batch, nq, LANES)[:, :, F_LANE:F_LANE + FOX_HEADS]
    delta = gate[:, qi, :] - gate[:, np.minimum(ki + 1, nq - 1), :]
    top = jnp.max(delta.reshape(batch // nb, nb, nsteps, ngroups, FOX_HEADS // ngroups), axis=(1, 4))
    top = top.transpose(0, 2, 1).reshape(batch // nb * ngroups, nsteps)
    active = diag[None, :] | (top > ZERO_WEIGHT_LOG2)
    same_q = jnp.asarray(qi[:, None] == np.arange(nq)[None, :])
    first_ki = jnp.min(jnp.where(active[:, :, None] & same_q[None], ki[None, :, None], nq), axis=1)
    first = active & (jnp.asarray(ki)[None, :] == first_ki[:, qi])
    flags = first.astype(jnp.int32) + 2 * diag[None, :].astype(jnp.int32) + 4 * active.astype(jnp.int32)
    order = jnp.argsort(jnp.logical_not(active), axis=1, stable=True)
    n_active = jnp.sum(active, axis=1, keepdims=True)
    slot = jnp.arange(nsteps)[None, :]
    src = jnp.take_along_axis(order, jnp.minimum(slot, n_active - 1), axis=1)
    kept = slot < n_active
    take = lambda a: jnp.asarray(a, jnp.int32)[src].reshape(-1)
    fl = jnp.where(kept, jnp.take_along_axis(flags, src, axis=1), 0).reshape(-1)
    return take(qi), take(ki), fl.astype(jnp.int32)


def _flash(q, k, v, *, batch, seq, shared_kv, fixed_shift, nheads=NSA_HEADS, window=None, base=None,
           q_off=None, u=None, z=None, gate_branch=None, v_offset=0):
    nb = FLASH_ROWS if fixed_shift else SAFE_FLASH_ROWS
    nb = nb if batch % nb == 0 else 1
    nq = seq // TQ
    ngroups = NSA_HEADS // nheads
    nkv = NSA_KV_GROUPS
    da = k.shape[-1]
    has_delta = base is not None
    has_u = u is not None
    out_w = LANES * (nheads // 2)
    v_col = v_offset // out_w
    table_rows = batch // nb * ngroups
    if has_delta and fixed_shift:
        tabs = _decayed_tile_tables(nq, batch, nb, ngroups, base)
    else:
        tabs = _tile_tables(nq, None if window is None else window // TQ, table_rows)
    nsteps = int(tabs[0].shape[0]) // table_rows

    at = lambda tab, b, g, s: tab[(b * ngroups + g) * nsteps + s]
    by_q_heads = lambda b, g, s, qt, kt, ft: (b, g, at(qt, b, g, s), 0)
    by_k_heads = lambda b, g, s, qt, kt, ft: (b, g, at(kt, b, g, s), 0)
    by_q_tokens = lambda col: (lambda b, g, s, qt, kt, ft: (b, at(qt, b, g, s), g if col is None else col))
    by_k_tokens = lambda b, g, s, qt, kt, ft: (b, at(kt, b, g, s), g + v_col)

    in_specs, args = [], []
    if has_delta:
        in_specs.append(pl.BlockSpec(memory_space=pltpu.SMEM))
        args.append(base)
    in_specs.append(pl.BlockSpec((nb, nheads, TQ, LANES), by_q_heads))
    args.append(q)
    if has_delta:
        in_specs.append(pl.BlockSpec((nb, nheads, TQ, LANES), by_q_heads))
        args.append(q_off)
    if has_u:
        in_specs.append(pl.BlockSpec((nb, nkv, TQ, LANES), by_q_heads))
        args.append(u)
    if shared_kv:
        in_specs.append(pl.BlockSpec((nb, nkv, TQ, da), by_k_heads))
        in_specs.append(pl.BlockSpec((nb, nkv, TQ, LANES), by_k_heads))
    else:
        in_specs.append(pl.BlockSpec((nb, nheads, TQ, da), by_k_heads))
        in_specs.append(pl.BlockSpec((nb, TQ, out_w), by_k_tokens))
    args += [k, v]
    if gate_branch is not None:
        in_specs.append(pl.BlockSpec((nb, TQ, LANES), by_q_tokens(C_SMALL // LANES)))
        args.append(z)

    m_rows = 8 if fixed_shift else TQ
    scratch = [pltpu.VMEM((nb * nheads, m_rows, LANES), F32),
               pltpu.VMEM((nb * nheads, TQ, LANES), F32),
               pltpu.VMEM((nb * nheads // 2, TQ, LANES), F32)]
    if has_u:
        scratch.append(pltpu.VMEM((nb * nheads, TQ, 2 * LANES), BF16))

    kern = functools.partial(_flash_kernel, nheads=nheads, shared_kv=shared_kv, window=window,
                             has_delta=has_delta, has_u=has_u, gate_branch=gate_branch, nblk=nq,
                             fixed_shift=fixed_shift, nb=nb)
    return pl.pallas_call(
        kern,
        grid_spec=pltpu.PrefetchScalarGridSpec(
            num_scalar_prefetch=3,
            grid=(batch // nb, ngroups, nsteps),
            in_specs=in_specs,
            out_specs=pl.BlockSpec((nb, TQ, out_w), by_q_tokens(None)),
            scratch_shapes=scratch),
        out_shape=jax.ShapeDtypeStruct((batch, seq, 4 * LANES), BF16),
        compiler_params=_cparams(("parallel", "parallel", "arbitrary")),
        name="flash_" + ("fox" if has_delta else ("sel" if has_u else "win")),
    )(*tabs, *args)


def _cmp_kernel(q_ref, kc_ref, vc_ref, ov_ref, gate_ref, o_ref, u_ref):
    g = pl.program_id(1)
    qi = pl.program_id(2)
    tq = q_ref.shape[2]
    ncmp = kc_ref.shape[2]
    nslc = ov_ref.shape[1]

    t = qi * tq + lax.broadcasted_iota(jnp.int32, (tq, ncmp), 0)
    n = lax.broadcasted_iota(jnp.int32, (tq, ncmp), 1)
    valid = (n * NSA_CMP_STRIDE + (NSA_CMP_LEN - 1)) <= t
    kc = kc_ref[0, 0]
    vc = vc_ref[0, 0]
    ov = ov_ref[...]
    lane = lax.broadcasted_iota(jnp.int32, (tq, LANES), 1)
    lo = lane < HEAD_DIM
    gate_blk = gate_ref[0].astype(F32)

    imp = jnp.zeros((tq, nslc), F32)
    outs = []
    for r in range(NSA_GROUP_HEADS):
        s = lax.dot_general(q_ref[0, r], kc, (((1,), (1,)), ((), ())), preferred_element_type=F32)
        s = jnp.where(valid, s, NEG)
        m = jnp.max(s, axis=1, keepdims=True)
        e = jnp.where(valid, jnp.exp2(s - m), 0.0)
        d = jnp.sum(e, axis=1, keepdims=True)
        pb = (e / jnp.where(d > 0.0, d, 1.0)).astype(BF16)
        o = jnp.dot(pb, vc, preferred_element_type=F32)
        gcol = _gate_column(gate_blk, lane, 3 * (g * NSA_GROUP_HEADS + r))
        outs.append(o * gcol)
        imp = imp + jnp.dot(pb, ov, preferred_element_type=F32)
    for p in range(NSA_GROUP_HEADS // 2):
        o_ref[0, :, p * LANES:(p + 1) * LANES] = jnp.where(lo, outs[2 * p], outs[2 * p + 1]).astype(o_ref.dtype)
    _write_unselected(u_ref, imp, qi, tq, nslc)


def _cmp_fixed_kernel(q_ref, kc_ref, vc_ref, ov_ref, gate_ref, o_ref, u_ref, d_scr, acc_scr, imp_scr,
                      *, chunk):
    g = pl.program_id(1)
    qi = pl.program_id(2)
    tq = q_ref.shape[2]
    ncmp = kc_ref.shape[2]
    nslc = ov_ref.shape[1]
    t0 = qi * tq
    d_scr[...] = jnp.zeros(d_scr.shape, F32)
    acc_scr[...] = jnp.zeros(acc_scr.shape, F32)
    imp_scr[...] = jnp.zeros(imp_scr.shape, F32)

    for c in range(ncmp // chunk):
        first_end = c * chunk * NSA_CMP_STRIDE + NSA_CMP_LEN - 1
        last_end = ((c + 1) * chunk - 1) * NSA_CMP_STRIDE + NSA_CMP_LEN - 1

        def body(masked, c=c):
            rows = slice(c * chunk, (c + 1) * chunk)
            kc = kc_ref[0, 0, rows, :]
            vo = jnp.concatenate([vc_ref[0, 0, rows, :], ov_ref[rows, :]], axis=1)
            if masked:
                t = t0 + lax.broadcasted_iota(jnp.int32, (tq, chunk), 0)
                n = c * chunk + lax.broadcasted_iota(jnp.int32, (tq, chunk), 1)
                valid = (n * NSA_CMP_STRIDE + (NSA_CMP_LEN - 1)) <= t
            for r in range(NSA_GROUP_HEADS):
                s = lax.dot_general(q_ref[0, r], kc, (((1,), (1,)), ((), ())), preferred_element_type=F32)
                if masked:
                    s = jnp.where(valid, s, NEG)
                e = jnp.exp2(s)
                part = e[:, :LANES]
                for j in range(1, chunk // LANES):
                    part = part + e[:, j * LANES:(j + 1) * LANES]
                d_scr[r] = d_scr[r] + part
                both = jnp.dot(e.astype(BF16), vo, preferred_element_type=F32)
                acc_scr[r] = acc_scr[r] + both[:, :LANES]
                imp_scr[r] = imp_scr[r] + both[:, LANES:]

        pl.when((first_end <= t0 + (tq - 1)) & (last_end > t0))(functools.partial(body, True))
        pl.when(last_end <= t0)(functools.partial(body, False))

    lane = lax.broadcasted_iota(jnp.int32, (tq, LANES), 1)
    lo = lane < HEAD_DIM
    src = lax.broadcasted_iota(jnp.int32, (LANES, LANES), 0)
    dst_lo = lax.broadcasted_iota(jnp.int32, (LANES, LANES), 1) < HEAD_DIM
    imp = jnp.zeros((tq, nslc), F32)
    for p in range(NSA_GROUP_HEADS // 2):
        normed = []
        for r in (2 * p, 2 * p + 1):
            d = jnp.sum(d_scr[r], axis=1, keepdims=True)
            inv = 1.0 / jnp.where(d > 0.0, d, 1.0)
            normed.append(acc_scr[r] * inv)
            imp = imp + imp_scr[r] * inv
        h0 = g * NSA_GROUP_HEADS + 2 * p
        pick = jnp.where(src == jnp.where(dst_lo, 3 * h0, 3 * (h0 + 1)), 1.0, 0.0).astype(BF16)
        logits = jnp.dot(gate_ref[0].astype(BF16), pick, preferred_element_type=F32)
        o_ref[0, :, p * LANES:(p + 1) * LANES] = (jnp.where(lo, normed[0], normed[1])
                                                  * jax.nn.sigmoid(logits)).astype(o_ref.dtype)
    _write_unselected(u_ref, imp, qi, tq, nslc)


N_FORCED = 3


def _write_unselected(u_ref, imp, qi, tq, nslc):
    tpos = qi * tq + lax.broadcasted_iota(jnp.int32, (tq, nslc), 0)
    jblk = lax.broadcasted_iota(jnp.int32, (tq, nslc), 1)
    cur = jnp.right_shift(tpos, SEL_SHIFT)
    forced = (jblk == 0) | (jblk == cur) | (jblk == cur - 1)
    visible = jblk <= cur
    score_t = jnp.where(visible & jnp.logical_not(forced), imp, -jnp.inf).T

    def run(nrows):
        sc = score_t[:nrows]
        jt = lax.broadcasted_iota(jnp.int32, (nrows, tq), 0).astype(F32)
        for _ in range(NSA_SEL_TOPK - N_FORCED):
            mx = jnp.max(sc, axis=0, keepdims=True)
            first = jnp.min(jnp.where(sc == mx, jt, float(nslc)), axis=0, keepdims=True)
            sc = jnp.where(jt == first, -jnp.inf, sc)
        taken = jnp.where(sc == -jnp.inf, 1.0, 0.0)
        if nrows < nslc:
            taken = jnp.concatenate([taken, jnp.zeros((nslc - nrows, tq), F32)], axis=0)
        chosen = visible & (forced | (taken.T > 0.5))
        u_ref[0, 0] = jnp.where(chosen, 0.0, 1.0).astype(BF16)

    half = nslc // 2
    last_block = jnp.right_shift(qi * tq + (tq - 1), SEL_SHIFT)
    pl.when(last_block < half)(functools.partial(run, half))
    pl.when(last_block >= half)(functools.partial(run, nslc))


def _cmp_select(qn, kc, vc, ov, z, *, batch, seq, fixed_shift):
    ncmp = kc.shape[2]
    nslc = ov.shape[1]
    if fixed_shift:
        chunk = 2 * LANES if ncmp % (2 * LANES) == 0 else ncmp
        kern = functools.partial(_cmp_fixed_kernel, chunk=chunk)
        scratch = [pltpu.VMEM((NSA_GROUP_HEADS, CMP_TQ, LANES), F32) for _ in range(3)]
    else:
        kern, scratch = _cmp_kernel, []
    return pl.pallas_call(
        kern,
        scratch_shapes=scratch,
        grid=(batch, NSA_KV_GROUPS, seq // CMP_TQ),
        in_specs=[pl.BlockSpec((1, NSA_GROUP_HEADS, CMP_TQ, LANES), lambda b, g, i: (b, g, i, 0)),
                  pl.BlockSpec((1, 1, ncmp, LANES), lambda b, g, i: (b, g, 0, 0)),
                  pl.BlockSpec((1, 1, ncmp, LANES), lambda b, g, i: (b, g, 0, 0)),
                  pl.BlockSpec((ncmp, nslc), lambda b, g, i: (0, 0)),
                  pl.BlockSpec((1, CMP_TQ, LANES), lambda b, g, i: (b, i, C_SMALL // LANES))],
        out_specs=[pl.BlockSpec((1, CMP_TQ, 2 * LANES), lambda b, g, i: (b, i, g)),
                   pl.BlockSpec((1, 1, CMP_TQ, nslc), lambda b, g, i: (b, g, i, 0))],
        out_shape=[jax.ShapeDtypeStruct((batch, seq, NSA_WIDTH), BF16),
                   jax.ShapeDtypeStruct((batch, NSA_KV_GROUPS, seq, nslc), BF16)],
        compiler_params=_cparams(("parallel", "parallel", "parallel")),
        name="cmp_select",
    )(qn, kc, vc, ov, z)


def _head_rms(x, g):
    return x * lax.rsqrt(jnp.mean(x * x, axis=-1, keepdims=True) + EPS) * g


def _mem_kernel(zq_ref, kv_ref, qg_ref, kg_ref, o_ref, kn_scr, vb_scr):
    @pl.when(pl.program_id(1) == 0)
    def _prep():
        for h in range(MEM_HEADS):
            kh = kv_ref[0, :, h * MEM_HEAD_DIM:(h + 1) * MEM_HEAD_DIM]
            kn_scr[h] = _head_rms(kh, kg_ref[...]).astype(BF16)
            vb_scr[h] = kv_ref[0, :, MEM_WIDTH + h * MEM_HEAD_DIM:
                               MEM_WIDTH + (h + 1) * MEM_HEAD_DIM].astype(BF16)

    for h in range(MEM_HEADS):
        sl = slice(h * MEM_HEAD_DIM, (h + 1) * MEM_HEAD_DIM)
        qh = (_head_rms(zq_ref[0, :, sl].astype(F32), qg_ref[...])
              * ((MEM_HEAD_DIM ** -0.5) * LOG2E)).astype(BF16)
        s = lax.dot_general(qh, kn_scr[h], (((1,), (1,)), ((), ())), preferred_element_type=F32)
        m = jnp.max(s, axis=1, keepdims=True)
        e = jnp.exp2(s - m)
        p = (e / jnp.sum(e, axis=1, keepdims=True)).astype(BF16)
        o_ref[0, :, sl] = jnp.dot(p, vb_scr[h], preferred_element_type=F32).astype(o_ref.dtype)


def _mem_attention(z, mem_kv, q_g, k_g, *, batch, seq):
    mlen = mem_kv.shape[1]
    tq = TQ
    return pl.pallas_call(
        _mem_kernel,
        grid=(batch, seq // tq),
        in_specs=[pl.BlockSpec((1, tq, MEM_WIDTH), lambda b, i: (b, i, C_MEM_Q // MEM_WIDTH)),
                  pl.BlockSpec((1, mlen, 2 * MEM_WIDTH), lambda b, i: (b, 0, 0)),
                  pl.BlockSpec((1, MEM_HEAD_DIM), lambda b, i: (0, 0)),
                  pl.BlockSpec((1, MEM_HEAD_DIM), lambda b, i: (0, 0))],
        out_specs=pl.BlockSpec((1, tq, MEM_WIDTH), lambda b, i: (b, i, 0)),
        out_shape=jax.ShapeDtypeStruct((batch, seq, MEM_WIDTH), BF16),
        scratch_shapes=[pltpu.VMEM((MEM_HEADS, mlen, MEM_HEAD_DIM), BF16),
                        pltpu.VMEM((MEM_HEADS, mlen, MEM_HEAD_DIM), BF16)],
        compiler_params=_cparams(("parallel", "arbitrary")),
        name="mem_attention",
    )(z, mem_kv, q_g.reshape(1, MEM_HEAD_DIM), k_g.reshape(1, MEM_HEAD_DIM))


def _sigmoid(x):
    return 0.5 * jnp.tanh(0.5 * x) + 0.5


def _out_kernel(x_ref, oc_ref, os_ref, ow_ref, ob_ref, om_ref, sa_ref, sb_ref, sm_ref,
                g0_ref, g1_ref, g2_ref, wa_ref, wb_ref, wm_ref, wo_ref, y_ref):
    silu = lambda ref: (lambda x: x * _sigmoid(x))(ref[...].astype(F32))
    oa = (oc_ref[...].astype(F32) + os_ref[...].astype(F32) + ow_ref[...].astype(F32)) * silu(sa_ref)
    ob = ob_ref[...].astype(F32) * silu(sb_ref)
    om = om_ref[...].astype(F32) * silu(sm_ref)
    gate = lambda ref: _sigmoid(ref[...].astype(F32))
    u = (gate(g0_ref) * jnp.dot(oa.astype(BF16), wa_ref[...], preferred_element_type=F32)
         + gate(g1_ref) * jnp.dot(ob.astype(BF16), wb_ref[...], preferred_element_type=F32)
         + gate(g2_ref) * jnp.dot(om.astype(BF16), wm_ref[...], preferred_element_type=F32))
    y_ref[...] = x_ref[...] + jnp.dot(u.astype(BF16), wo_ref[...], preferred_element_type=F32)


def _out_proj(x2d, o_c, o_s, o_w, o_b, o_m, z2d, wa, wb, wm, wo, layer):
    m = x2d.shape[0]
    tm = 256
    w512 = 512
    row512 = lambda c: pl.BlockSpec((tm, w512), lambda i, c=c: (i, c))
    row1024 = lambda c: pl.BlockSpec((tm, D_MODEL), lambda i, c=c: (i, c))
    full = lambda shape: pl.BlockSpec((None,) + shape, lambda i: (layer, 0, 0))
    return pl.pallas_call(
        _out_kernel,
        grid=(m // tm,),
        in_specs=[row1024(0), row512(0), row512(0), row512(0), row512(0), row512(0),
                  row512(C_NSA_SILU // w512), row512(C_FOX_SILU // w512), row512(C_MEM_SILU // w512),
                  row1024(C_MERGE // D_MODEL), row1024(C_MERGE // D_MODEL + 1),
                  row1024(C_MERGE // D_MODEL + 2),
                  full((w512, D_MODEL)), full((w512, D_MODEL)), full((w512, D_MODEL)),
                  full((D_MODEL, D_MODEL))],
        out_specs=row1024(0),
        out_shape=jax.ShapeDtypeStruct((m, D_MODEL), F32),
        compiler_params=_cparams(("parallel",)),
        name="out_proj",
    )(x2d, o_c, o_s, o_w, o_b, o_m, z2d, z2d, z2d, z2d, z2d, z2d, wa, wb, wm, wo)


def _fox_head_order(a, order, sections):
    nl, d, _ = a.shape
    a = a.reshape(nl, d, sections, FOX_HEADS, HEAD_DIM)
    return jnp.take_along_axis(a, order[:, None, None, :, None], axis=3).reshape(nl, d, -1)


def _permute_w_in(w_in, fox_order):
    o = np.cumsum([0, NSA_WIDTH, 6 * NSA_KV_GROUPS * HEAD_DIM, 3 * NSA_HEADS, NSA_WIDTH,
                   3 * FOX_WIDTH, FOX_HEADS, FOX_WIDTH, MEM_WIDTH, MEM_WIDTH, 3 * D_MODEL])
    nsa_q, nsa_kv, nsa_gate, nsa_silu, fox_qkv, fox_f, fox_silu, mem_q, mem_silu, merge = [
        w_in[..., o[i]:o[i + 1]] for i in range(10)]
    fox_qkv = _fox_head_order(fox_qkv, fox_order, 3)
    fox_silu = _fox_head_order(fox_silu, fox_order, 1)
    fox_f = jnp.take_along_axis(fox_f, fox_order[:, None, :], axis=2)
    pad = jnp.zeros(w_in.shape[:2] + (LANES - 3 * NSA_HEADS - FOX_HEADS,), w_in.dtype)
    return jnp.concatenate([nsa_q, nsa_silu, fox_qkv, fox_silu, mem_q, mem_silu, merge, nsa_kv,
                            nsa_gate, fox_f, pad], axis=-1).astype(BF16)


def _pad_lanes(x, width=LANES):
    return jnp.pad(x, [(0, 0)] * (x.ndim - 1) + [(0, width - x.shape[-1])])


def _tile2(g):
    return jnp.concatenate([g, g], axis=-1)[:, None, :]


def _overlap(n_cmp_pad, n_slc):
    cs = np.arange(n_cmp_pad)[:, None] * NSA_CMP_STRIDE
    ss = np.arange(NSLC_PAD)[None, :] * NSA_SEL_LEN
    ov = np.clip(np.minimum(cs + NSA_CMP_LEN, ss + NSA_SEL_LEN) - np.maximum(cs, ss), 0, None)
    ov = ov / NSA_CMP_LEN
    ov[-1] = 0.0
    ov[:, n_slc:] = 0.0
    return jnp.asarray(ov, BF16)


def _aug_permutations():
    pq = np.zeros((6 * LANES, FOX_HEADS * LANES), np.float32)
    pk = np.zeros((6 * LANES, FOX_HEADS * LANES), np.float32)
    for h in range(FOX_HEADS):
        for i in range(3):
            pq[i * LANES + F_LANE + h, h * LANES + AUG + i] = 1.0
            pk[i * LANES + F_LANE + h, h * LANES + AUG + 3 + i] = -1.0
            pk[(3 + i) * LANES + F_LANE + h, h * LANES + AUG + 6 + i] = -1.0
    return jnp.asarray(pq, BF16), jnp.asarray(pk, BF16)


def _compress_w1(w1):
    nl = w1.shape[0]
    halves = w1.reshape(nl, 2, NSA_CMP_STRIDE, 1, HEAD_DIM, NSA_CMP_HIDDEN)
    eye = jnp.eye(NSA_KV_GROUPS, dtype=w1.dtype)
    out = jnp.einsum("pg,zhldn->zlpdghn", eye, halves[:, :, :, 0])
    return out.reshape(nl, NSA_CMP_STRIDE * NSA_KV_GROUPS * HEAD_DIM,
                       NSA_KV_GROUPS * 2 * NSA_CMP_HIDDEN).astype(BF16)


def _layer(x, mem, p, big, layer, consts):
    batch, seq, _ = x.shape
    m_rows = batch * seq
    cos4, sin_signed, tri, head_ones, perm_q, perm_k, ov = consts
    z2d = _norm_matmul(x.reshape(m_rows, D_MODEL), p["norm_g"], big["w_in"], layer, 512, Z_WIDTH // 3, BF16)
    z = z2d.reshape(batch, seq, Z_WIDTH)

    (qn, kcr, vcr, ks, vs, kw, vw, fqd, fqo, fka, base) = _prep(
        z, cos4, sin_signed, tri, head_ones, perm_q, perm_k,
        p["gq"], p["gks"], p["gkw"], p["gfq"], p["gfk"], p["fb"],
        p["s_sel"], p["s_win"], p["s_fox"], batch=batch, seq=seq)

    nchunk = seq // NSA_CMP_STRIDE
    kc, vc = _compress(kcr.reshape(batch, nchunk, NSA_CMP_STRIDE * LANES),
                       vcr.reshape(batch, nchunk, NSA_CMP_STRIDE * LANES),
                       p["w1k"], p["w1v"], p["bk"], p["bv"], p["w2k"], p["w2v"], p["gkc"], p["s_cmp"],
                       batch=batch)

    def attend(fixed_shift):
        o_c, unsel = _cmp_select(qn, kc, vc, ov, z, batch=batch, seq=seq, fixed_shift=fixed_shift)
        o_s = _flash(qn, ks, vs, batch=batch, seq=seq, shared_kv=True, fixed_shift=fixed_shift,
                     u=unsel, z=z, gate_branch=1)
        o_w = _flash(qn, kw, vw, batch=batch, seq=seq, shared_kv=True, fixed_shift=fixed_shift,
                     window=NSA_WINDOW, z=z, gate_branch=2)
        o_b = _flash(fqd, fka, z, batch=batch, seq=seq, shared_kv=False, fixed_shift=fixed_shift,
                     nheads=FOX_GROUP_HEADS, base=base.reshape(-1), q_off=fqo, v_offset=C_FOX_V)
        return o_c, o_s, o_w, o_b

    o_c, o_s, o_w, o_b = lax.cond(p["bound_ok"], lambda: attend(True), lambda: attend(False))

    mlen = mem.shape[1]
    mem_kv = _norm_matmul(mem.reshape(batch * mlen, D_MODEL), p["mem_norm_g"], big["w_mem_kv"], layer,
                          min(512, batch * mlen), 2 * MEM_WIDTH, F32).reshape(batch, mlen, 2 * MEM_WIDTH)
    o_m = _mem_attention(z, mem_kv, p["mem_q_norm"], p["mem_k_norm"], batch=batch, seq=seq)

    y = _out_proj(x.reshape(m_rows, D_MODEL), o_c.reshape(m_rows, -1), o_s.reshape(m_rows, -1),
                  o_w.reshape(m_rows, -1), o_b.reshape(m_rows, -1), o_m.reshape(m_rows, -1), z2d,
                  big["w_branch_a"], big["w_branch_b"], big["w_branch_m"], big["w_out"], layer)
    return y.reshape(batch, seq, D_MODEL)


def kernel(x, mem, norm_g, mem_norm_g, w_in, nsa_q_norm, nsa_k_norm, cmp_pe_k, cmp_w1_k, cmp_w2_k,
           cmp_pe_v, cmp_w1_v, cmp_w2_v, fox_q_norm, fox_k_norm, fox_f_bias, mem_q_norm, mem_k_norm,
           w_mem_kv, w_branch_a, w_branch_b, w_branch_m, w_out):
    batch, seq, _ = x.shape
    depth = w_in.shape[0]
    n_slc = seq // NSA_SEL_LEN
    assert n_slc <= NSLC_PAD and seq % TQ == 0 and NSA_WINDOW == TQ

    half = HEAD_DIM // 2
    inv_freq = ROPE_THETA ** (-jnp.arange(half, dtype=F32) / half)
    ang = jnp.arange(seq).astype(F32)[:, None] * inv_freq[None, :]
    cos, sin = jnp.cos(ang), jnp.sin(ang)
    cos4 = jnp.concatenate([cos, cos, cos, cos], axis=-1)
    sin_signed = jnp.concatenate([-sin, sin, -sin, sin], axis=-1)
    tri = jnp.asarray(np.tril(np.ones((TQ, TQ), np.float32)), BF16)
    head_ones = jnp.asarray(np.kron(np.eye(2), np.ones((HEAD_DIM, HEAD_DIM))), BF16)
    consts = (cos4, sin_signed, tri, head_ones, *_aug_permutations(),
              _overlap(seq // NSA_CMP_STRIDE, n_slc))

    hp = lax.Precision.HIGHEST
    pe_bias = lambda pe, w1: jnp.einsum("lk,lkn->ln", pe.reshape(depth, -1), w1, precision=hp)[:, None, :]
    fox_order = jnp.argsort(fox_f_bias, axis=-1)
    fox_bias_sorted = jnp.take_along_axis(fox_f_bias, fox_order, axis=-1)
    w_branch_b_sorted = jnp.take_along_axis(
        w_branch_b.reshape(depth, FOX_HEADS, HEAD_DIM, D_MODEL), fox_order[:, :, None, None], axis=1
    ).reshape(depth, FOX_WIDTH, D_MODEL)
    fb = jnp.zeros((depth, 1, LANES), F32).at[:, 0, F_LANE:F_LANE + FOX_HEADS].set(fox_bias_sorted)

    def logit_bound(gq, gk):
        bound = (BOUND_MARGIN * HEAD_DIM * QK_SCALE
                 * jnp.max(jnp.abs(gq), axis=-1) * jnp.max(jnp.abs(gk), axis=-1))
        return bound.astype(BF16).astype(F32)

    bounds = [logit_bound(nsa_q_norm, nsa_k_norm[:, 1]), logit_bound(nsa_q_norm, nsa_k_norm[:, 2]),
              logit_bound(fox_q_norm, fox_k_norm), logit_bound(nsa_q_norm, nsa_k_norm[:, 0])]
    bound_ok = functools.reduce(jnp.maximum, bounds) <= MAX_FIXED_BOUND
    bound_row = lambda s: jnp.broadcast_to(s[:, None, None], (depth, 1, LANES))
    stacked = dict(
        s_sel=bound_row(bounds[0]), s_win=bound_row(bounds[1]), s_fox=bound_row(bounds[2]),
        s_cmp=bound_row(bounds[3]),
        bound_ok=bound_ok,
        norm_g=norm_g, mem_norm_g=mem_norm_g,
        gq=_tile2(nsa_q_norm), gks=_tile2(nsa_k_norm[:, 1]), gkw=_tile2(nsa_k_norm[:, 2]),
        gkc=_pad_lanes(nsa_k_norm[:, 0])[:, None, :], gfq=_tile2(fox_q_norm), gfk=_tile2(fox_k_norm),
        fb=fb, w1k=_compress_w1(cmp_w1_k), w1v=_compress_w1(cmp_w1_v),
        bk=pe_bias(cmp_pe_k, cmp_w1_k), bv=pe_bias(cmp_pe_v, cmp_w1_v),
        w2k=_pad_lanes(cmp_w2_k).astype(BF16),
        w2v=jnp.concatenate([cmp_w2_v, cmp_w2_v], axis=-1).astype(BF16),
        mem_q_norm=mem_q_norm, mem_k_norm=mem_k_norm)
    big = dict(w_in=_permute_w_in(w_in, fox_order), w_mem_kv=w_mem_kv.astype(BF16),
               w_branch_a=w_branch_a.astype(BF16), w_branch_b=w_branch_b_sorted.astype(BF16),
               w_branch_m=w_branch_m.astype(BF16), w_out=w_out.astype(BF16))
    for l in range(depth):
        x = _layer(x, mem, {k: v[l] for k, v in stacked.items()}, big, l, consts)
    return x
```

```python
import functools

import numpy as np
import jax
import jax.numpy as jnp
from jax import lax
from jax.experimental import pallas as pl
from jax.experimental.pallas import tpu as pltpu

F32 = jnp.float32
BF16 = jnp.bfloat16

D_MODEL = 1024
HEAD_DIM = 64
ROPE_THETA = 10000.0
EPS = 1e-6
NSA_HEADS = 8
NSA_KV_GROUPS = 2
NSA_GROUP_HEADS = NSA_HEADS // NSA_KV_GROUPS
NSA_CMP_LEN = 32
NSA_CMP_STRIDE = 16
NSA_CMP_HIDDEN = 128
NSA_SEL_LEN = 64
NSA_SEL_TOPK = 16
NSA_WINDOW = 512
FORCE_SCORE = 1e9
NSA_WIDTH = NSA_HEADS * HEAD_DIM
FOX_HEADS = 8
FOX_WIDTH = FOX_HEADS * HEAD_DIM
MEM_HEADS = 4
MEM_HEAD_DIM = 128
MEM_WIDTH = MEM_HEADS * MEM_HEAD_DIM

LANES = 128
VMEM_LIMIT = 56 * 1024 * 1024
NEG = -1e30
SEL_MASK = 32768.0
SEL_SHIFT = 6
NSLC_PAD = LANES
LOG2E = 1.4426950408889634
QK_SCALE = (HEAD_DIM ** -0.5) * LOG2E
BOUND_MARGIN = 1.02
MAX_FIXED_BOUND = 60.0
ZERO_WEIGHT_LOG2 = -160.0

C_NSA_Q = 0
C_NSA_SILU = 512
C_FOX_Q = 1024
C_FOX_K = 1536
C_FOX_V = 2048
C_FOX_SILU = 2560
C_MEM_Q = 3072
C_MEM_SILU = 3584
C_MERGE = 4096
C_NSA_KV = 7168
C_SMALL = 7936
Z_WIDTH = 8064
F_LANE = 3 * NSA_HEADS
AUG = HEAD_DIM

TQ = 512
FLASH_ROWS = 4
SAFE_FLASH_ROWS = 2
FOX_GROUP_HEADS = 4
CMP_TQ = 512


def _cparams(sem):
    return pltpu.CompilerParams(dimension_semantics=sem, vmem_limit_bytes=VMEM_LIMIT)


def _norm_matmul_kernel(x_ref, g_ref, w_ref, o_ref, *, tn):
    x = x_ref[...]
    ms = jnp.mean(x * x, axis=-1, keepdims=True)
    h = (x * lax.rsqrt(ms + EPS) * g_ref[...]).astype(BF16)
    for c0 in range(0, w_ref.shape[1], tn):
        o_ref[:, c0:c0 + tn] = jnp.dot(h, w_ref[:, c0:c0 + tn],
                                       preferred_element_type=F32).astype(o_ref.dtype)


def _norm_matmul(x2d, g, w, layer, tm, tn, out_dtype):
    m, k = x2d.shape
    n = w.shape[2]
    return pl.pallas_call(
        functools.partial(_norm_matmul_kernel, tn=tn),
        grid=(m // tm,),
        in_specs=[pl.BlockSpec((tm, k), lambda i: (i, 0)),
                  pl.BlockSpec((1, k), lambda i: (0, 0)),
                  pl.BlockSpec((None, k, n), lambda i: (layer, 0, 0), pipeline_mode=pl.Buffered(1))],
        out_specs=pl.BlockSpec((tm, n), lambda i: (i, 0)),
        out_shape=jax.ShapeDtypeStruct((m, n), out_dtype),
        compiler_params=_cparams(("parallel",)),
        name="norm_matmul",
    )(x2d, g.reshape(1, k), w)


def _pair_rms(x, g2, head_ones):
    sq = x * x
    hi = sq.astype(BF16)
    low = (sq - hi.astype(F32)).astype(BF16)
    ssq = (jnp.dot(hi, head_ones, preferred_element_type=F32)
           + jnp.dot(low, head_ones, preferred_element_type=F32))
    return x * lax.rsqrt(ssq * (1.0 / HEAD_DIM) + EPS) * g2


def _pair_rope(y, cos4, sin_signed, first_half):
    rot = jnp.where(first_half, pltpu.roll(y, LANES - HEAD_DIM // 2, 1), pltpu.roll(y, HEAD_DIM // 2, 1))
    return y * cos4 + rot * sin_signed


def _head_a(y, lo, tail=0.0):
    return jnp.where(lo, y, tail)


def _head_b(y, lo, tail=0.0):
    return jnp.where(lo, pltpu.roll(y, HEAD_DIM, 1), tail)


def _split3_f32(c):
    hi = c.astype(BF16).astype(F32)
    r = c - hi
    mid = r.astype(BF16).astype(F32)
    return hi, mid, r - mid


def _prep_kernel(zq_ref, zc_ref, zs_ref, zw_ref, fq_ref, fk_ref, sm_ref, cos_ref, sin_ref,
                 tri_ref, ones_ref, pq_ref, pk_ref, gq_ref, gks_ref, gkw_ref, gfq_ref, gfk_ref, fb_ref,
                 ss_ref, sw_ref, sf_ref,
                 qn_ref, kcr_ref, vcr_ref, ks_ref, vs_ref, kw_ref, vw_ref, fqd_ref, fqo_ref, fka_ref,
                 base_ref, run_scr):
    ti = pl.program_id(1)
    tt = zq_ref.shape[1]
    lane = lax.broadcasted_iota(jnp.int32, (tt, LANES), 1)
    lo = lane < HEAD_DIM
    first_half = (lane & (HEAD_DIM // 2)) == 0
    cos4 = cos_ref[...]
    sin_signed = sin_ref[...]
    head_ones = ones_ref[...]
    one_at_aug = jnp.where(lane == AUG, 1.0, 0.0)

    for p in range(NSA_HEADS // 2):
        y = _pair_rope(_pair_rms(zq_ref[0, :, p * LANES:(p + 1) * LANES].astype(F32), gq_ref[...], head_ones),
                       cos4, sin_signed, first_half) * QK_SCALE
        qn_ref[0, 2 * p] = _head_a(y, lo, one_at_aug).astype(BF16)
        qn_ref[0, 2 * p + 1] = _head_b(y, lo, one_at_aug).astype(BF16)

    kcr_ref[0] = _pair_rope(zc_ref[0, :, :LANES].astype(F32), cos4, sin_signed, first_half).astype(BF16)
    vcr_ref[0] = zc_ref[0, :, LANES:].astype(BF16)

    sel_lane = jnp.right_shift(ti * tt + lax.broadcasted_iota(jnp.int32, (tt, LANES), 0), SEL_SHIFT)
    onehot = jnp.where(lane == sel_lane, -SEL_MASK, 0.0).astype(BF16)
    for z_ref, g_ref, k_ref, v_ref, bound_ref, with_onehot in (
            (zs_ref, gks_ref, ks_ref, vs_ref, ss_ref, True),
            (zw_ref, gkw_ref, kw_ref, vw_ref, sw_ref, False)):
        y = _pair_rope(_pair_rms(z_ref[0, :, :LANES].astype(F32), g_ref[...], head_ones),
                       cos4, sin_signed, first_half)
        v = z_ref[0, :, LANES:].astype(F32)
        vr = pltpu.roll(v, HEAD_DIM, 1)
        neg_bound = jnp.where(lane == AUG, -bound_ref[...], 0.0)
        k_ref[0, 0, :, :LANES] = _head_a(y, lo, neg_bound).astype(BF16)
        k_ref[0, 1, :, :LANES] = _head_b(y, lo, neg_bound).astype(BF16)
        if with_onehot:
            k_ref[0, 0, :, LANES:] = onehot
            k_ref[0, 1, :, LANES:] = onehot
        v_ref[0, 0] = jnp.where(lo, v, vr).astype(BF16)
        v_ref[0, 1] = jnp.where(lo, vr, v).astype(BF16)

    @pl.when(ti == 0)
    def _reset():
        run_scr[...] = jnp.zeros(run_scr.shape, F32)

    xf = sm_ref[0].astype(F32) + fb_ref[...]
    log_f = (jnp.minimum(xf, 0.0) - jnp.log1p(jnp.exp(-jnp.abs(xf)))) * LOG2E
    tri = tri_ref[...]
    parts = _split3_f32(log_f)
    cum = sum(jnp.dot(tri, part.astype(BF16), preferred_element_type=F32) for part in parts)
    base_ref[0, 0] = run_scr[...]
    run_scr[...] = run_scr[...] + cum[tt - 1:tt, :]

    parts6 = _split3_f32(cum) + _split3_f32(cum - cum[tt - 1:tt, :])
    x6 = jnp.concatenate([part.astype(BF16) for part in parts6], axis=1)
    aug_q = jnp.dot(x6, pq_ref[...], preferred_element_type=F32)
    aug_k = jnp.dot(x6, pk_ref[...], preferred_element_type=F32)

    lane1 = lax.broadcasted_iota(jnp.int32, (1, LANES), 1)
    span = lambda a, b: jnp.where((lane1 >= a) & (lane1 < b), 1.0, 0.0)
    q_tail_diag = span(AUG + 3, AUG + 6) + span(AUG + 9, AUG + 10)
    q_tail_off = span(AUG + 6, AUG + 10)
    k_tail = span(AUG, AUG + 3) + jnp.where(lane1 == AUG + 9, -sf_ref[...], 0.0)
    for p in range(FOX_HEADS // 2):
        sl = slice(p * LANES, (p + 1) * LANES)
        yq = _pair_rms(fq_ref[0, :, sl].astype(F32), gfq_ref[...], head_ones) * QK_SCALE
        yk = _pair_rms(fk_ref[0, :, sl].astype(F32), gfk_ref[...], head_ones)
        for e, split in ((0, _head_a), (1, _head_b)):
            h = 2 * p + e
            hs = slice(h * LANES, (h + 1) * LANES)
            qh = split(yq, lo) + aug_q[:, hs]
            fqd_ref[0, h] = (qh + q_tail_diag).astype(BF16)
            fqo_ref[0, h] = (qh + q_tail_off).astype(BF16)
            fka_ref[0, h] = (split(yk, lo) + aug_k[:, hs] + k_tail).astype(BF16)


def _prep(z, cos4, sin_signed, tri, head_ones, perm_q, perm_k, gq, gks, gkw, gfq, gfk, fb,
          s_sel, s_win, s_fox, *, batch, seq):
    nblk = seq // TQ
    whole = lambda a: pl.BlockSpec(a.shape, lambda b, i: (0, 0))
    zspec = lambda w, c: pl.BlockSpec((1, TQ, w), lambda b, i, c=c: (b, i, c))
    row = pl.BlockSpec((1, LANES), lambda b, i: (0, 0))
    tab = pl.BlockSpec((TQ, LANES), lambda b, i: (i, 0))
    heads = lambda n, w: pl.BlockSpec((1, n, TQ, w), lambda b, i: (b, 0, i, 0))
    tok = lambda w: pl.BlockSpec((1, TQ, w), lambda b, i: (b, i, 0))
    bf = lambda *shape: jax.ShapeDtypeStruct(shape, BF16)
    return pl.pallas_call(
        _prep_kernel,
        grid=(batch, nblk),
        in_specs=[zspec(NSA_WIDTH, C_NSA_Q // NSA_WIDTH),
                  zspec(2 * LANES, C_NSA_KV // (2 * LANES)),
                  zspec(2 * LANES, C_NSA_KV // (2 * LANES) + 1),
                  zspec(2 * LANES, C_NSA_KV // (2 * LANES) + 2),
                  zspec(FOX_WIDTH, C_FOX_Q // FOX_WIDTH),
                  zspec(FOX_WIDTH, C_FOX_K // FOX_WIDTH),
                  zspec(LANES, C_SMALL // LANES),
                  tab, tab, whole(tri), whole(head_ones), whole(perm_q), whole(perm_k),
                  row, row, row, row, row, row, row, row, row],
        out_specs=[heads(NSA_HEADS, LANES), tok(LANES), tok(LANES),
                   heads(NSA_KV_GROUPS, 2 * LANES), heads(NSA_KV_GROUPS, LANES),
                   heads(NSA_KV_GROUPS, LANES), heads(NSA_KV_GROUPS, LANES),
                   heads(FOX_HEADS, LANES), heads(FOX_HEADS, LANES), heads(FOX_HEADS, LANES),
                   pl.BlockSpec((1, 1, 1, LANES), lambda b, i: (b, i, 0, 0))],
        out_shape=[bf(batch, NSA_HEADS, seq, LANES), bf(batch, seq, LANES), bf(batch, seq, LANES),
                   bf(batch, NSA_KV_GROUPS, seq, 2 * LANES), bf(batch, NSA_KV_GROUPS, seq, LANES),
                   bf(batch, NSA_KV_GROUPS, seq, LANES), bf(batch, NSA_KV_GROUPS, seq, LANES),
                   bf(batch, FOX_HEADS, seq, LANES), bf(batch, FOX_HEADS, seq, LANES),
                   bf(batch, FOX_HEADS, seq, LANES),
                   jax.ShapeDtypeStruct((batch, nblk, 1, LANES), F32)],
        scratch_shapes=[pltpu.VMEM((1, LANES), F32)],
        compiler_params=_cparams(("parallel", "arbitrary")),
        name="prep",
    )(z, z, z, z, z, z, z, cos4, sin_signed, tri, head_ones, perm_q, perm_k,
      gq, gks, gkw, gfq, gfk, fb, s_sel, s_win, s_fox)


def _compress_kernel(xk_ref, xv_ref, w1k_ref, w1v_ref, bk_ref, bv_ref, w2k_ref, w2v_ref, kg_ref,
                     sc_ref, kc_ref, vc_ref):
    nchunk = xk_ref.shape[1]
    lane = lax.broadcasted_iota(jnp.int32, (nchunk, LANES), 1)
    for x_ref, w1_ref, b_ref, w2_ref, o_ref, is_key in ((xk_ref, w1k_ref, bk_ref, w2k_ref, kc_ref, True),
                                                        (xv_ref, w1v_ref, bv_ref, w2v_ref, vc_ref, False)):
        h = jnp.dot(x_ref[0], w1_ref[...], preferred_element_type=F32)
        for g in range(NSA_KV_GROUPS):
            c0 = g * 2 * NSA_CMP_HIDDEN
            top = h[:, c0:c0 + NSA_CMP_HIDDEN]
            bot = h[:, c0 + NSA_CMP_HIDDEN:c0 + 2 * NSA_CMP_HIDDEN]
            hid = top + pltpu.roll(bot, nchunk - 1, 0) + b_ref[...]
            act = (hid * jax.nn.sigmoid(hid)).astype(BF16)
            o = jnp.dot(act, w2_ref[...], preferred_element_type=F32)
            if is_key:
                ms = jnp.sum(o * o, axis=1, keepdims=True) * (1.0 / HEAD_DIM)
                o = o * lax.rsqrt(ms + EPS) * kg_ref[...]
                o = jnp.where(lane == AUG, -sc_ref[...], o)
            o_ref[0, g] = o.astype(BF16)


def _compress(xk, xv, w1k, w1v, bk, bv, w2k, w2v, kg, s_cmp, *, batch):
    nchunk, kin = xk.shape[1], xk.shape[2]
    xspec = pl.BlockSpec((1, nchunk, kin), lambda b: (b, 0, 0))
    full = lambda a: pl.BlockSpec(a.shape, lambda b: (0,) * a.ndim)
    ospec = pl.BlockSpec((1, NSA_KV_GROUPS, nchunk, LANES), lambda b: (b, 0, 0, 0))
    oshape = jax.ShapeDtypeStruct((batch, NSA_KV_GROUPS, nchunk, LANES), BF16)
    return pl.pallas_call(
        _compress_kernel,
        grid=(batch,),
        in_specs=[xspec, xspec, full(w1k), full(w1v), full(bk), full(bv), full(w2k), full(w2v), full(kg),
                  full(s_cmp)],
        out_specs=[ospec, ospec],
        out_shape=[oshape, oshape],
        compiler_params=_cparams(("parallel",)),
        name="compress",
    )(xk, xv, w1k, w1v, bk, bv, w2k, w2v, kg, s_cmp)


def _gate_column(gate_blk, lane, idx):
    col = jnp.sum(jnp.where(lane == idx, gate_blk, 0.0), axis=1, keepdims=True)
    return jax.nn.sigmoid(col)


def _flash_kernel(qi_tab, ki_tab, fl_tab, *refs, nheads, shared_kv, window, has_delta,
                  has_u, gate_branch, nblk, fixed_shift, nb):
    refs = list(refs)
    base_ref = refs.pop(0) if has_delta else None
    q_ref = refs.pop(0)
    qoff_ref = refs.pop(0) if has_delta else None
    u_ref = refs.pop(0) if has_u else None
    k_ref = refs.pop(0)
    v_ref = refs.pop(0)
    gate_ref = refs.pop(0) if gate_branch is not None else None
    o_ref = refs.pop(0)
    m_scr = refs.pop(0)
    l_scr = refs.pop(0)
    acc_scr = refs.pop(0)
    qcat_scr = refs.pop(0) if has_u else None

    b = pl.program_id(0)
    g = pl.program_id(1)
    step = (b * pl.num_programs(1) + g) * pl.num_programs(2) + pl.program_id(2)
    qi = qi_tab[step]
    ki = ki_tab[step]
    fl = fl_tab[step]
    active = (fl & 4) == 4
    tq = q_ref.shape[2]
    tk = k_ref.shape[2]

    @pl.when((fl & 1) == 1)
    def _init():
        m_scr[...] = jnp.full(m_scr.shape, NEG, F32)
        l_scr[...] = jnp.zeros(l_scr.shape, F32)
        acc_scr[...] = jnp.zeros(acc_scr.shape, F32)
        if has_u:
            for bb in range(nb):
                for r in range(nheads):
                    qcat_scr[bb * nheads + r, :, :LANES] = q_ref[bb, r]
                    qcat_scr[bb * nheads + r, :, LANES:] = u_ref[bb, r // NSA_GROUP_HEADS]

    lane = lax.broadcasted_iota(jnp.int32, (tq, LANES), 1)
    lo = lane < HEAD_DIM

    def tile(diagonal):
        masked = diagonal or window is not None
        if masked:
            row = lax.broadcasted_iota(jnp.int32, (tq, tk), 0)
            col = lax.broadcasted_iota(jnp.int32, (tq, tk), 1)
            if diagonal:
                valid = row >= col
            else:
                valid = (tq + row - col) < window
        for bb in range(nb):
            slot0, pair0, brow = bb * nheads, bb * (nheads // 2), b * nb + bb
            for p in range(nheads // 2):
                pvs, alphas = [], []
                for e in range(2):
                    hh = 2 * p + e
                    if has_u:
                        q = qcat_scr[slot0 + hh]
                    elif has_delta and not diagonal:
                        q = qoff_ref[bb, hh]
                    else:
                        q = q_ref[bb, hh]
                    k = k_ref[bb, hh // NSA_GROUP_HEADS] if shared_kv else k_ref[bb, hh]
                    v = (v_ref[bb, hh // NSA_GROUP_HEADS] if shared_kv
                         else v_ref[bb, :, p * LANES:(p + 1) * LANES])
                    s = lax.dot_general(q, k, (((1,), (1,)), ((), ())), preferred_element_type=F32)
                    if masked:
                        s = jnp.where(valid, s, NEG)
                    delta = None
                    if has_delta and not diagonal:
                        hoff = F_LANE + g * nheads + hh
                        delta = (base_ref[(brow * nblk + qi) * LANES + hoff]
                                 - base_ref[(brow * nblk + ki + 1) * LANES + hoff])
                    if fixed_shift:
                        pm = jnp.exp2(s)
                        rowsum = pm[:, :LANES]
                        for j in range(1, tk // LANES):
                            rowsum = rowsum + pm[:, j * LANES:(j + 1) * LANES]
                        pv = jnp.dot(pm.astype(BF16), v, preferred_element_type=F32)
                        if delta is not None:
                            w = jnp.exp2(jnp.full((1, LANES), delta, F32))
                            rowsum = rowsum * w
                            pv = pv * w
                        l_scr[slot0 + hh] = l_scr[slot0 + hh] + rowsum
                        pvs.append(pv)
                    else:
                        m_prev = m_scr[slot0 + hh]
                        m_cur = jnp.max(s, axis=1, keepdims=True)
                        if delta is not None:
                            m_cur = m_cur + delta
                        m_new = jnp.maximum(m_prev, m_cur)
                        alpha = jnp.exp2(m_prev - m_new)
                        shift = m_new[:, :1]
                        if delta is not None:
                            shift = shift - delta
                        pm = jnp.exp2(s - shift)
                        l_scr[slot0 + hh] = alpha * l_scr[slot0 + hh] + jnp.sum(pm, axis=1, keepdims=True)
                        m_scr[slot0 + hh] = m_new
                        pvs.append(jnp.dot(pm.astype(BF16), v, preferred_element_type=F32))
                        alphas.append(alpha)
                if fixed_shift:
                    acc_scr[pair0 + p] = acc_scr[pair0 + p] + jnp.where(lo, pvs[0], pvs[1])
                else:
                    acc_scr[pair0 + p] = (acc_scr[pair0 + p] * jnp.where(lo, alphas[0], alphas[1])
                                          + jnp.where(lo, pvs[0], pvs[1]))

    pl.when((qi == ki) & active)(lambda: tile(True))
    pl.when((qi != ki) & active)(lambda: tile(False))

    @pl.when((fl & 2) == 2)
    def _finish():
        for bb in range(nb):
            slot0, pair0 = bb * nheads, bb * (nheads // 2)
            for p in range(nheads // 2):
                if fixed_shift:
                    la = jnp.sum(l_scr[slot0 + 2 * p], axis=1, keepdims=True)
                    lb = jnp.sum(l_scr[slot0 + 2 * p + 1], axis=1, keepdims=True)
                else:
                    la, lb = l_scr[slot0 + 2 * p], l_scr[slot0 + 2 * p + 1]
                inv = jnp.where(lo, 1.0 / la, 1.0 / lb)
                if gate_branch is not None:
                    h0 = g * nheads + 2 * p
                    src = lax.broadcasted_iota(jnp.int32, (LANES, LANES), 0)
                    dst_lo = lax.broadcasted_iota(jnp.int32, (LANES, LANES), 1) < HEAD_DIM
                    pick = jnp.where(src == jnp.where(dst_lo, 3 * h0 + gate_branch,
                                                      3 * (h0 + 1) + gate_branch), 1.0, 0.0)
                    logits = jnp.dot(gate_ref[bb].astype(BF16), pick.astype(BF16),
                                     preferred_element_type=F32)
                    inv = inv * jax.nn.sigmoid(logits)
                o_ref[bb, :, p * LANES:(p + 1) * LANES] = (acc_scr[pair0 + p] * inv).astype(o_ref.dtype)


def _tile_tables(nq, window_tiles, batch):
    qi, ki, fl = [], [], []
    for i in range(nq):
        lo = 0 if window_tiles is None else max(0, i - window_tiles)
        for j in range(lo, i + 1):
            qi.append(i)
            ki.append(j)
            fl.append((1 if j == lo else 0) | (2 if j == i else 0) | 4)
    rows = lambda a: jnp.tile(jnp.asarray(a, jnp.int32), batch)
    return rows(qi), rows(ki), rows(fl)


def _decayed_tile_tables(nq, batch, nb, ngroups, base):
    qi = np.repeat(np.arange(nq), np.arange(1, nq + 1))
    ki = np.concatenate([np.arange(i + 1) for i in range(nq)])
    nsteps = qi.shape[0]
    diag = jnp.asarray(qi == ki)
    gate = base.reshape(batch, nq, LANES)[:, :, F_LANE:F_LANE + FOX_HEADS]
    delta = gate[:, qi, :] - gate[:, np.minimum(ki + 1, nq - 1), :]
    top = jnp.max(delta.reshape(batch // nb, nb, nsteps, ngroups, FOX_HEADS // ngroups), axis=(1, 4))
    top = top.transpose(0, 2, 1).reshape(batch // nb * ngroups, nsteps)
    active = diag[None, :] | (top > ZERO_WEIGHT_LOG2)
    same_q = jnp.asarray(qi[:, None] == np.arange(nq)[None, :])
    first_ki = jnp.min(jnp.where(active[:, :, None] & same_q[None], ki[None, :, None], nq), axis=1)
    first = active & (jnp.asarray(ki)[None, :] == first_ki[:, qi])
    flags = first.astype(jnp.int32) + 2 * diag[None, :].astype(jnp.int32) + 4 * active.astype(jnp.int32)
    order = jnp.argsort(jnp.logical_not(active), axis=1, stable=True)
    n_active = jnp.sum(active, axis=1, keepdims=True)
    slot = jnp.arange(nsteps)[None, :]
    src = jnp.take_along_axis(order, jnp.minimum(slot, n_active - 1), axis=1)
    kept = slot < n_active
    take = lambda a: jnp.asarray(a, jnp.int32)[src].reshape(-1)
    fl = jnp.where(kept, jnp.take_along_axis(flags, src, axis=1), 0).reshape(-1)
    return take(qi), take(ki), fl.astype(jnp.int32)


def _flash(q, k, v, *, batch, seq, shared_kv, fixed_shift, nheads=NSA_HEADS, window=None, base=None,
           q_off=None, u=None, z=None, gate_branch=None, v_offset=0):
    nb = FLASH_ROWS if fixed_shift else SAFE_FLASH_ROWS
    nb = nb if batch % nb == 0 else 1
    nq = seq // TQ
    ngroups = NSA_HEADS // nheads
    nkv = NSA_KV_GROUPS
    da = k.shape[-1]
    has_delta = base is not None
    has_u = u is not None
    out_w = LANES * (nheads // 2)
    v_col = v_offset // out_w
    table_rows = batch // nb * ngroups
    if has_delta and fixed_shift:
        tabs = _decayed_tile_tables(nq, batch, nb, ngroups, base)
    else:
        tabs = _tile_tables(nq, None if window is None else window // TQ, table_rows)
    nsteps = int(tabs[0].shape[0]) // table_rows

    at = lambda tab, b, g, s: tab[(b * ngroups + g) * nsteps + s]
    by_q_heads = lambda b, g, s, qt, kt, ft: (b, g, at(qt, b, g, s), 0)
    by_k_heads = lambda b, g, s, qt, kt, ft: (b, g, at(kt, b, g, s), 0)
    by_q_tokens = lambda col: (lambda b, g, s, qt, kt, ft: (b, at(qt, b, g, s), g if col is None else col))
    by_k_tokens = lambda b, g, s, qt, kt, ft: (b, at(kt, b, g, s), g + v_col)

    in_specs, args = [], []
    if has_delta:
        in_specs.append(pl.BlockSpec(memory_space=pltpu.SMEM))
        args.append(base)
    in_specs.append(pl.BlockSpec((nb, nheads, TQ, LANES), by_q_heads))
    args.append(q)
    if has_delta:
        in_specs.append(pl.BlockSpec((nb, nheads, TQ, LANES), by_q_heads))
        args.append(q_off)
    if has_u:
        in_specs.append(pl.BlockSpec((nb, nkv, TQ, LANES), by_q_heads))
        args.append(u)
    if shared_kv:
        in_specs.append(pl.BlockSpec((nb, nkv, TQ, da), by_k_heads))
        in_specs.append(pl.BlockSpec((nb, nkv, TQ, LANES), by_k_heads))
    else:
        in_specs.append(pl.BlockSpec((nb, nheads, TQ, da), by_k_heads))
        in_specs.append(pl.BlockSpec((nb, TQ, out_w), by_k_tokens))
    args += [k, v]
    if gate_branch is not None:
        in_specs.append(pl.BlockSpec((nb, TQ, LANES), by_q_tokens(C_SMALL // LANES)))
        args.append(z)

    m_rows = 8 if fixed_shift else TQ
    scratch = [pltpu.VMEM((nb * nheads, m_rows, LANES), F32),
               pltpu.VMEM((nb * nheads, TQ, LANES), F32),
               pltpu.VMEM((nb * nheads // 2, TQ, LANES), F32)]
    if has_u:
        scratch.append(pltpu.VMEM((nb * nheads, TQ, 2 * LANES), BF16))

    kern = functools.partial(_flash_kernel, nheads=nheads, shared_kv=shared_kv, window=window,
                             has_delta=has_delta, has_u=has_u, gate_branch=gate_branch, nblk=nq,
                             fixed_shift=fixed_shift, nb=nb)
    return pl.pallas_call(
        kern,
        grid_spec=pltpu.PrefetchScalarGridSpec(
            num_scalar_prefetch=3,
            grid=(batch // nb, ngroups, nsteps),
            in_specs=in_specs,
            out_specs=pl.BlockSpec((nb, TQ, out_w), by_q_tokens(None)),
            scratch_shapes=scratch),
        out_shape=jax.ShapeDtypeStruct((batch, seq, 4 * LANES), BF16),
        compiler_params=_cparams(("parallel", "parallel", "arbitrary")),
        name="flash_" + ("fox" if has_delta else ("sel" if has_u else "win")),
    )(*tabs, *args)


def _cmp_kernel(q_ref, kc_ref, vc_ref, ov_ref, gate_ref, o_ref, u_ref):
    g = pl.program_id(1)
    qi = pl.program_id(2)
    tq = q_ref.shape[2]
    ncmp = kc_ref.shape[2]
    nslc = ov_ref.shape[1]

    t = qi * tq + lax.broadcasted_iota(jnp.int32, (tq, ncmp), 0)
    n = lax.broadcasted_iota(jnp.int32, (tq, ncmp), 1)
    valid = (n * NSA_CMP_STRIDE + (NSA_CMP_LEN - 1)) <= t
    kc = kc_ref[0, 0]
    vc = vc_ref[0, 0]
    ov = ov_ref[...]
    lane = lax.broadcasted_iota(jnp.int32, (tq, LANES), 1)
    lo = lane < HEAD_DIM
    gate_blk = gate_ref[0].astype(F32)

    imp = jnp.zeros((tq, nslc), F32)
    outs = []
    for r in range(NSA_GROUP_HEADS):
        s = lax.dot_general(q_ref[0, r], kc, (((1,), (1,)), ((), ())), preferred_element_type=F32)
        s = jnp.where(valid, s, NEG)
        m = jnp.max(s, axis=1, keepdims=True)
        e = jnp.where(valid, jnp.exp2(s - m), 0.0)
        d = jnp.sum(e, axis=1, keepdims=True)
        pb = (e / jnp.where(d > 0.0, d, 1.0)).astype(BF16)
        o = jnp.dot(pb, vc, preferred_element_type=F32)
        gcol = _gate_column(gate_blk, lane, 3 * (g * NSA_GROUP_HEADS + r))
        outs.append(o * gcol)
        imp = imp + jnp.dot(pb, ov, preferred_element_type=F32)
    for p in range(NSA_GROUP_HEADS // 2):
        o_ref[0, :, p * LANES:(p + 1) * LANES] = jnp.where(lo, outs[2 * p], outs[2 * p + 1]).astype(o_ref.dtype)
    _write_unselected(u_ref, imp, qi, tq, nslc)


def _cmp_fixed_kernel(q_ref, kc_ref, vc_ref, ov_ref, gate_ref, o_ref, u_ref, d_scr, acc_scr, imp_scr,
                      *, chunk):
    g = pl.program_id(1)
    qi = pl.program_id(2)
    tq = q_ref.shape[2]
    ncmp = kc_ref.shape[2]
    nslc = ov_ref.shape[1]
    t0 = qi * tq
    d_scr[...] = jnp.zeros(d_scr.shape, F32)
    acc_scr[...] = jnp.zeros(acc_scr.shape, F32)
    imp_scr[...] = jnp.zeros(imp_scr.shape, F32)

    for c in range(ncmp // chunk):
        first_end = c * chunk * NSA_CMP_STRIDE + NSA_CMP_LEN - 1
        last_end = ((c + 1) * chunk - 1) * NSA_CMP_STRIDE + NSA_CMP_LEN - 1

        def body(masked, c=c):
            rows = slice(c * chunk, (c + 1) * chunk)
            kc = kc_ref[0, 0, rows, :]
            vo = jnp.concatenate([vc_ref[0, 0, rows, :], ov_ref[rows, :]], axis=1)
            if masked:
                t = t0 + lax.broadcasted_iota(jnp.int32, (tq, chunk), 0)
                n = c * chunk + lax.broadcasted_iota(jnp.int32, (tq, chunk), 1)
                valid = (n * NSA_CMP_STRIDE + (NSA_CMP_LEN - 1)) <= t
            for r in range(NSA_GROUP_HEADS):
                s = lax.dot_general(q_ref[0, r], kc, (((1,), (1,)), ((), ())), preferred_element_type=F32)
                if masked:
                    s = jnp.where(valid, s, NEG)
                e = jnp.exp2(s)
                part = e[:, :LANES]
                for j in range(1, chunk // LANES):
                    part = part + e[:, j * LANES:(j + 1) * LANES]
                d_scr[r] = d_scr[r] + part
                both = jnp.dot(e.astype(BF16), vo, preferred_element_type=F32)
                acc_scr[r] = acc_scr[r] + both[:, :LANES]
                imp_scr[r] = imp_scr[r] + both[:, LANES:]

        pl.when((first_end <= t0 + (tq - 1)) & (last_end > t0))(functools.partial(body, True))
        pl.when(last_end <= t0)(functools.partial(body, False))

    lane = lax.broadcasted_iota(jnp.int32, (tq, LANES), 1)
    lo = lane < HEAD_DIM
    src = lax.broadcasted_iota(jnp.int32, (LANES, LANES), 0)
    dst_lo = lax.broadcasted_iota(jnp.int32, (LANES, LANES), 1) < HEAD_DIM
    imp = jnp.zeros((tq, nslc), F32)
    for p in range(NSA_GROUP_HEADS // 2):
        normed = []
        for r in (2 * p, 2 * p + 1):
            d = jnp.sum(d_scr[r], axis=1, keepdims=True)
            inv = 1.0 / jnp.where(d > 0.0, d, 1.0)
            normed.append(acc_scr[r] * inv)
            imp = imp + imp_scr[r] * inv
        h0 = g * NSA_GROUP_HEADS + 2 * p
        pick = jnp.where(src == jnp.where(dst_lo, 3 * h0, 3 * (h0 + 1)), 1.0, 0.0).astype(BF16)
        logits = jnp.dot(gate_ref[0].astype(BF16), pick, preferred_element_type=F32)
        o_ref[0, :, p * LANES:(p + 1) * LANES] = (jnp.where(lo, normed[0], normed[1])
                                                  * jax.nn.sigmoid(logits)).astype(o_ref.dtype)
    _write_unselected(u_ref, imp, qi, tq, nslc)


N_FORCED = 3


def _write_unselected(u_ref, imp, qi, tq, nslc):
    tpos = qi * tq + lax.broadcasted_iota(jnp.int32, (tq, nslc), 0)
    jblk = lax.broadcasted_iota(jnp.int32, (tq, nslc), 1)
    cur = jnp.right_shift(tpos, SEL_SHIFT)
    forced = (jblk == 0) | (jblk == cur) | (jblk == cur - 1)
    visible = jblk <= cur
    score_t = jnp.where(visible & jnp.logical_not(forced), imp, -jnp.inf).T

    def run(nrows):
        sc = score_t[:nrows]
        jt = lax.broadcasted_iota(jnp.int32, (nrows, tq), 0).astype(F32)
        for _ in range(NSA_SEL_TOPK - N_FORCED):
            mx = jnp.max(sc, axis=0, keepdims=True)
            first = jnp.min(jnp.where(sc == mx, jt, float(nslc)), axis=0, keepdims=True)
            sc = jnp.where(jt == first, -jnp.inf, sc)
        taken = jnp.where(sc == -jnp.inf, 1.0, 0.0)
        if nrows < nslc:
            taken = jnp.concatenate([taken, jnp.zeros((nslc - nrows, tq), F32)], axis=0)
        chosen = visible & (forced | (taken.T > 0.5))
        u_ref[0, 0] = jnp.where(chosen, 0.0, 1.0).astype(BF16)

    half = nslc // 2
    last_block = jnp.right_shift(qi * tq + (tq - 1), SEL_SHIFT)
    pl.when(last_block < half)(functools.partial(run, half))
    pl.when(last_block >= half)(functools.partial(run, nslc))


def _cmp_select(qn, kc, vc, ov, z, *, batch, seq, fixed_shift):
    ncmp = kc.shape[2]
    nslc = ov.shape[1]
    if fixed_shift:
        chunk = 2 * LANES if ncmp % (2 * LANES) == 0 else ncmp
        kern = functools.partial(_cmp_fixed_kernel, chunk=chunk)
        scratch = [pltpu.VMEM((NSA_GROUP_HEADS, CMP_TQ, LANES), F32) for _ in range(3)]
    else:
        kern, scratch = _cmp_kernel, []
    return pl.pallas_call(
        kern,
        scratch_shapes=scratch,
        grid=(batch, NSA_KV_GROUPS, seq // CMP_TQ),
        in_specs=[pl.BlockSpec((1, NSA_GROUP_HEADS, CMP_TQ, LANES), lambda b, g, i: (b, g, i, 0)),
                  pl.BlockSpec((1, 1, ncmp, LANES), lambda b, g, i: (b, g, 0, 0)),
                  pl.BlockSpec((1, 1, ncmp, LANES), lambda b, g, i: (b, g, 0, 0)),
                  pl.BlockSpec((ncmp, nslc), lambda b, g, i: (0, 0)),
                  pl.BlockSpec((1, CMP_TQ, LANES), lambda b, g, i: (b, i, C_SMALL // LANES))],
        out_specs=[pl.BlockSpec((1, CMP_TQ, 2 * LANES), lambda b, g, i: (b, i, g)),
                   pl.BlockSpec((1, 1, CMP_TQ, nslc), lambda b, g, i: (b, g, i, 0))],
        out_shape=[jax.ShapeDtypeStruct((batch, seq, NSA_WIDTH), BF16),
                   jax.ShapeDtypeStruct((batch, NSA_KV_GROUPS, seq, nslc), BF16)],
        compiler_params=_cparams(("parallel", "parallel", "parallel")),
        name="cmp_select",
    )(qn, kc, vc, ov, z)


def _head_rms(x, g):
    return x * lax.rsqrt(jnp.mean(x * x, axis=-1, keepdims=True) + EPS) * g


def _mem_kernel(zq_ref, kv_ref, qg_ref, kg_ref, o_ref, kn_scr, vb_scr):
    @pl.when(pl.program_id(1) == 0)
    def _prep():
        for h in range(MEM_HEADS):
            kh = kv_ref[0, :, h * MEM_HEAD_DIM:(h + 1) * MEM_HEAD_DIM]
            kn_scr[h] = _head_rms(kh, kg_ref[...]).astype(BF16)
            vb_scr[h] = kv_ref[0, :, MEM_WIDTH + h * MEM_HEAD_DIM:
                               MEM_WIDTH + (h + 1) * MEM_HEAD_DIM].astype(BF16)

    for h in range(MEM_HEADS):
        sl = slice(h * MEM_HEAD_DIM, (h + 1) * MEM_HEAD_DIM)
        qh = (_head_rms(zq_ref[0, :, sl].astype(F32), qg_ref[...])
              * ((MEM_HEAD_DIM ** -0.5) * LOG2E)).astype(BF16)
        s = lax.dot_general(qh, kn_scr[h], (((1,), (1,)), ((), ())), preferred_element_type=F32)
        m = jnp.max(s, axis=1, keepdims=True)
        e = jnp.exp2(s - m)
        p = (e / jnp.sum(e, axis=1, keepdims=True)).astype(BF16)
        o_ref[0, :, sl] = jnp.dot(p, vb_scr[h], preferred_element_type=F32).astype(o_ref.dtype)


def _mem_attention(z, mem_kv, q_g, k_g, *, batch, seq):
    mlen = mem_kv.shape[1]
    tq = TQ
    return pl.pallas_call(
        _mem_kernel,
        grid=(batch, seq // tq),
        in_specs=[pl.BlockSpec((1, tq, MEM_WIDTH), lambda b, i: (b, i, C_MEM_Q // MEM_WIDTH)),
                  pl.BlockSpec((1, mlen, 2 * MEM_WIDTH), lambda b, i: (b, 0, 0)),
                  pl.BlockSpec((1, MEM_HEAD_DIM), lambda b, i: (0, 0)),
                  pl.BlockSpec((1, MEM_HEAD_DIM), lambda b, i: (0, 0))],
        out_specs=pl.BlockSpec((1, tq, MEM_WIDTH), lambda b, i: (b, i, 0)),
        out_shape=jax.ShapeDtypeStruct((batch, seq, MEM_WIDTH), BF16),
        scratch_shapes=[pltpu.VMEM((MEM_HEADS, mlen, MEM_HEAD_DIM), BF16),
                        pltpu.VMEM((MEM_HEADS, mlen, MEM_HEAD_DIM), BF16)],
        compiler_params=_cparams(("parallel", "arbitrary")),
        name="mem_attention",
    )(z, mem_kv, q_g.reshape(1, MEM_HEAD_DIM), k_g.reshape(1, MEM_HEAD_DIM))


def _sigmoid(x):
    return 0.5 * jnp.tanh(0.5 * x) + 0.5


def _out_kernel(x_ref, oc_ref, os_ref, ow_ref, ob_ref, om_ref, sa_ref, sb_ref, sm_ref,
                g0_ref, g1_ref, g2_ref, wa_ref, wb_ref, wm_ref, wo_ref, y_ref):
    silu = lambda ref: (lambda x: x * _sigmoid(x))(ref[...])
    oa = ((oc_ref[...].astype(F32) + os_ref[...].astype(F32) + ow_ref[...].astype(F32)).astype(BF16)
          * silu(sa_ref))
    ob = ob_ref[...] * silu(sb_ref)
    om = om_ref[...] * silu(sm_ref)
    gate = lambda ref: _sigmoid(ref[...]).astype(F32)
    u = (gate(g0_ref) * jnp.dot(oa, wa_ref[...], preferred_element_type=F32)
         + gate(g1_ref) * jnp.dot(ob, wb_ref[...], preferred_element_type=F32)
         + gate(g2_ref) * jnp.dot(om, wm_ref[...], preferred_element_type=F32))
    y_ref[...] = x_ref[...] + jnp.dot(u.astype(BF16), wo_ref[...], preferred_element_type=F32)


def _out_proj(x2d, o_c, o_s, o_w, o_b, o_m, z2d, wa, wb, wm, wo, layer):
    m = x2d.shape[0]
    tm = 256
    w512 = 512
    row512 = lambda c: pl.BlockSpec((tm, w512), lambda i, c=c: (i, c))
    row1024 = lambda c: pl.BlockSpec((tm, D_MODEL), lambda i, c=c: (i, c))
    full = lambda shape: pl.BlockSpec((None,) + shape, lambda i: (layer, 0, 0))
    return pl.pallas_call(
        _out_kernel,
        grid=(m // tm,),
        in_specs=[row1024(0), row512(0), row512(0), row512(0), row512(0), row512(0),
                  row512(C_NSA_SILU // w512), row512(C_FOX_SILU // w512), row512(C_MEM_SILU // w512),
                  row1024(C_MERGE // D_MODEL), row1024(C_MERGE // D_MODEL + 1),
                  row1024(C_MERGE // D_MODEL + 2),
                  full((w512, D_MODEL)), full((w512, D_MODEL)), full((w512, D_MODEL)),
                  full((D_MODEL, D_MODEL))],
        out_specs=row1024(0),
        out_shape=jax.ShapeDtypeStruct((m, D_MODEL), F32),
        compiler_params=_cparams(("parallel",)),
        name="out_proj",
    )(x2d, o_c, o_s, o_w, o_b, o_m, z2d, z2d, z2d, z2d, z2d, z2d, wa, wb, wm, wo)


def _fox_head_order(a, order, sections):
    nl, d, _ = a.shape
    a = a.reshape(nl, d, sections, FOX_HEADS, HEAD_DIM)
    return jnp.take_along_axis(a, order[:, None, None, :, None], axis=3).reshape(nl, d, -1)


def _permute_w_in(w_in, fox_order):
    o = np.cumsum([0, NSA_WIDTH, 6 * NSA_KV_GROUPS * HEAD_DIM, 3 * NSA_HEADS, NSA_WIDTH,
                   3 * FOX_WIDTH, FOX_HEADS, FOX_WIDTH, MEM_WIDTH, MEM_WIDTH, 3 * D_MODEL])
    nsa_q, nsa_kv, nsa_gate, nsa_silu, fox_qkv, fox_f, fox_silu, mem_q, mem_silu, merge = [
        w_in[..., o[i]:o[i + 1]] for i in range(10)]
    fox_qkv = _fox_head_order(fox_qkv, fox_order, 3)
    fox_silu = _fox_head_order(fox_silu, fox_order, 1)
    fox_f = jnp.take_along_axis(fox_f, fox_order[:, None, :], axis=2)
    pad = jnp.zeros(w_in.shape[:2] + (LANES - 3 * NSA_HEADS - FOX_HEADS,), w_in.dtype)
    return jnp.concatenate([nsa_q, nsa_silu, fox_qkv, fox_silu, mem_q, mem_silu, merge, nsa_kv,
                            nsa_gate, fox_f, pad], axis=-1).astype(BF16)


def _pad_lanes(x, width=LANES):
    return jnp.pad(x, [(0, 0)] * (x.ndim - 1) + [(0, width - x.shape[-1])])


def _tile2(g):
    return jnp.concatenate([g, g], axis=-1)[:, None, :]


def _overlap(n_cmp_pad, n_slc):
    cs = np.arange(n_cmp_pad)[:, None] * NSA_CMP_STRIDE
    ss = np.arange(NSLC_PAD)[None, :] * NSA_SEL_LEN
    ov = np.clip(np.minimum(cs + NSA_CMP_LEN, ss + NSA_SEL_LEN) - np.maximum(cs, ss), 0, None)
    ov = ov / NSA_CMP_LEN
    ov[-1] = 0.0
    ov[:, n_slc:] = 0.0
    return jnp.asarray(ov, BF16)


def _aug_permutations():
    pq = np.zeros((6 * LANES, FOX_HEADS * LANES), np.float32)
    pk = np.zeros((6 * LANES, FOX_HEADS * LANES), np.float32)
    for h in range(FOX_HEADS):
        for i in range(3):
            pq[i * LANES + F_LANE + h, h * LANES + AUG + i] = 1.0
            pk[i * LANES + F_LANE + h, h * LANES + AUG + 3 + i] = -1.0
            pk[(3 + i) * LANES + F_LANE + h, h * LANES + AUG + 6 + i] = -1.0
    return jnp.asarray(pq, BF16), jnp.asarray(pk, BF16)


def _compress_w1(w1):
    nl = w1.shape[0]
    halves = w1.reshape(nl, 2, NSA_CMP_STRIDE, 1, HEAD_DIM, NSA_CMP_HIDDEN)
    eye = jnp.eye(NSA_KV_GROUPS, dtype=w1.dtype)
    out = jnp.einsum("pg,zhldn->zlpdghn", eye, halves[:, :, :, 0])
    return out.reshape(nl, NSA_CMP_STRIDE * NSA_KV_GROUPS * HEAD_DIM,
                       NSA_KV_GROUPS * 2 * NSA_CMP_HIDDEN).astype(BF16)


def _layer(x, mem, p, big, layer, consts):
    batch, seq, _ = x.shape
    m_rows = batch * seq
    cos4, sin_signed, tri, head_ones, perm_q, perm_k, ov = consts
    z2d = _norm_matmul(x.reshape(m_rows, D_MODEL), p["norm_g"], big["w_in"], layer, 512, Z_WIDTH // 3, BF16)
    z = z2d.reshape(batch, seq, Z_WIDTH)

    (qn, kcr, vcr, ks, vs, kw, vw, fqd, fqo, fka, base) = _prep(
        z, cos4, sin_signed, tri, head_ones, perm_q, perm_k,
        p["gq"], p["gks"], p["gkw"], p["gfq"], p["gfk"], p["fb"],
        p["s_sel"], p["s_win"], p["s_fox"], batch=batch, seq=seq)

    nchunk = seq // NSA_CMP_STRIDE
    kc, vc = _compress(kcr.reshape(batch, nchunk, NSA_CMP_STRIDE * LANES),
                       vcr.reshape(batch, nchunk, NSA_CMP_STRIDE * LANES),
                       p["w1k"], p["w1v"], p["bk"], p["bv"], p["w2k"], p["w2v"], p["gkc"], p["s_cmp"],
                       batch=batch)

    def attend(fixed_shift):
        o_c, unsel = _cmp_select(qn, kc, vc, ov, z, batch=batch, seq=seq, fixed_shift=fixed_shift)
        o_s = _flash(qn, ks, vs, batch=batch, seq=seq, shared_kv=True, fixed_shift=fixed_shift,
                     u=unsel, z=z, gate_branch=1)
        o_w = _flash(qn, kw, vw, batch=batch, seq=seq, shared_kv=True, fixed_shift=fixed_shift,
                     window=NSA_WINDOW, z=z, gate_branch=2)
        o_b = _flash(fqd, fka, z, batch=batch, seq=seq, shared_kv=False, fixed_shift=fixed_shift,
                     nheads=FOX_GROUP_HEADS, base=base.reshape(-1), q_off=fqo, v_offset=C_FOX_V)
        return o_c, o_s, o_w, o_b

    o_c, o_s, o_w, o_b = lax.cond(p["bound_ok"], lambda: attend(True), lambda: attend(False))

    mlen = mem.shape[1]
    mem_kv = _norm_matmul(mem.reshape(batch * mlen, D_MODEL), p["mem_norm_g"], big["w_mem_kv"], layer,
                          min(512, batch * mlen), 2 * MEM_WIDTH, F32).reshape(batch, mlen, 2 * MEM_WIDTH)
    o_m = _mem_attention(z, mem_kv, p["mem_q_norm"], p["mem_k_norm"], batch=batch, seq=seq)

    y = _out_proj(x.reshape(m_rows, D_MODEL), o_c.reshape(m_rows, -1), o_s.reshape(m_rows, -1),
                  o_w.reshape(m_rows, -1), o_b.reshape(m_rows, -1), o_m.reshape(m_rows, -1), z2d,
                  big["w_branch_a"], big["w_branch_b"], big["w_branch_m"], big["w_out"], layer)
    return y.reshape(batch, seq, D_MODEL)


def kernel(x, mem, norm_g, mem_norm_g, w_in, nsa_q_norm, nsa_k_norm, cmp_pe_k, cmp_w1_k, cmp_w2_k,
           cmp_pe_v, cmp_w1_v, cmp_w2_v, fox_q_norm, fox_k_norm, fox_f_bias, mem_q_norm, mem_k_norm,
           w_mem_kv, w_branch_a, w_branch_b, w_branch_m, w_out):
    batch, seq, _ = x.shape
    depth = w_in.shape[0]
    n_slc = seq // NSA_SEL_LEN
    assert n_slc <= NSLC_PAD and seq % TQ == 0 and NSA_WINDOW == TQ

    half = HEAD_DIM // 2
    inv_freq = ROPE_THETA ** (-jnp.arange(half, dtype=F32) / half)
    ang = jnp.arange(seq).astype(F32)[:, None] * inv_freq[None, :]
    cos, sin = jnp.cos(ang), jnp.sin(ang)
    cos4 = jnp.concatenate([cos, cos, cos, cos], axis=-1)
    sin_signed = jnp.concatenate([-sin, sin, -sin, sin], axis=-1)
    tri = jnp.asarray(np.tril(np.ones((TQ, TQ), np.float32)), BF16)
    head_ones = jnp.asarray(np.kron(np.eye(2), np.ones((HEAD_DIM, HEAD_DIM))), BF16)
    consts = (cos4, sin_signed, tri, head_ones, *_aug_permutations(),
              _overlap(seq // NSA_CMP_STRIDE, n_slc))

    hp = lax.Precision.HIGHEST
    pe_bias = lambda pe, w1: jnp.einsum("lk,lkn->ln", pe.reshape(depth, -1), w1, precision=hp)[:, None, :]
    fox_order = jnp.argsort(fox_f_bias, axis=-1)
    fox_bias_sorted = jnp.take_along_axis(fox_f_bias, fox_order, axis=-1)
    w_branch_b_sorted = jnp.take_along_axis(
        w_branch_b.reshape(depth, FOX_HEADS, HEAD_DIM, D_MODEL), fox_order[:, :, None, None], axis=1
    ).reshape(depth, FOX_WIDTH, D_MODEL)
    fb = jnp.zeros((depth, 1, LANES), F32).at[:, 0, F_LANE:F_LANE + FOX_HEADS].set(fox_bias_sorted)

    def logit_bound(gq, gk):
        bound = (BOUND_MARGIN * HEAD_DIM * QK_SCALE
                 * jnp.max(jnp.abs(gq), axis=-1) * jnp.max(jnp.abs(gk), axis=-1))
        return bound.astype(BF16).astype(F32)

    bounds = [logit_bound(nsa_q_norm, nsa_k_norm[:, 1]), logit_bound(nsa_q_norm, nsa_k_norm[:, 2]),
              logit_bound(fox_q_norm, fox_k_norm), logit_bound(nsa_q_norm, nsa_k_norm[:, 0])]
    bound_ok = functools.reduce(jnp.maximum, bounds) <= MAX_FIXED_BOUND
    bound_row = lambda s: jnp.broadcast_to(s[:, None, None], (depth, 1, LANES))
    stacked = dict(
        s_sel=bound_row(bounds[0]), s_win=bound_row(bounds[1]), s_fox=bound_row(bounds[2]),
        s_cmp=bound_row(bounds[3]),
        bound_ok=bound_ok,
        norm_g=norm_g, mem_norm_g=mem_norm_g,
        gq=_tile2(nsa_q_norm), gks=_tile2(nsa_k_norm[:, 1]), gkw=_tile2(nsa_k_norm[:, 2]),
        gkc=_pad_lanes(nsa_k_norm[:, 0])[:, None, :], gfq=_tile2(fox_q_norm), gfk=_tile2(fox_k_norm),
        fb=fb, w1k=_compress_w1(cmp_w1_k), w1v=_compress_w1(cmp_w1_v),
        bk=pe_bias(cmp_pe_k, cmp_w1_k), bv=pe_bias(cmp_pe_v, cmp_w1_v),
        w2k=_pad_lanes(cmp_w2_k).astype(BF16),
        w2v=jnp.concatenate([cmp_w2_v, cmp_w2_v], axis=-1).astype(BF16),
        mem_q_norm=mem_q_norm, mem_k_norm=mem_k_norm)
    big = dict(w_in=_permute_w_in(w_in, fox_order), w_mem_kv=w_mem_kv.astype(BF16),
               w_branch_a=w_branch_a.astype(BF16), w_branch_b=w_branch_b_sorted.astype(BF16),
               w_branch_m=w_branch_m.astype(BF16), w_out=w_out.astype(BF16))
    for l in range(depth):
        x = _layer(x, mem, {k: v[l] for k, v in stacked.items()}, big, l, consts)
    return x
```

```python
import functools

import numpy as np
import jax
import jax.numpy as jnp
from jax import lax
from jax.experimental import pallas as pl
from jax.experimental.pallas import tpu as pltpu

F32 = jnp.float32
BF16 = jnp.bfloat16

D_MODEL = 1024
HEAD_DIM = 64
ROPE_THETA = 10000.0
EPS = 1e-6
NSA_HEADS = 8
NSA_KV_GROUPS = 2
NSA_GROUP_HEADS = NSA_HEADS // NSA_KV_GROUPS
NSA_CMP_LEN = 32
NSA_CMP_STRIDE = 16
NSA_CMP_HIDDEN = 128
NSA_SEL_LEN = 64
NSA_SEL_TOPK = 16
NSA_WINDOW = 512
FORCE_SCORE = 1e9
NSA_WIDTH = NSA_HEADS * HEAD_DIM
FOX_HEADS = 8
FOX_WIDTH = FOX_HEADS * HEAD_DIM
MEM_HEADS = 4
MEM_HEAD_DIM = 128
MEM_WIDTH = MEM_HEADS * MEM_HEAD_DIM

LANES = 128
VMEM_LIMIT = 56 * 1024 * 1024
NEG = -1e30
SEL_MASK = 32768.0
SEL_SHIFT = 6
NSLC_PAD = LANES
LOG2E = 1.4426950408889634
QK_SCALE = (HEAD_DIM ** -0.5) * LOG2E
BOUND_MARGIN = 1.02
MAX_FIXED_BOUND = 60.0
ZERO_WEIGHT_LOG2 = -160.0

C_NSA_Q = 0
C_NSA_SILU = 512
C_FOX_Q = 1024
C_FOX_K = 1536
C_FOX_V = 2048
C_FOX_SILU = 2560
C_MEM_Q = 3072
C_MEM_SILU = 3584
C_MERGE = 4096
C_NSA_KV = 7168
C_SMALL = 7936
Z_WIDTH = 8064
F_LANE = 3 * NSA_HEADS
AUG = HEAD_DIM

TQ = 512
FLASH_ROWS = 4
SAFE_FLASH_ROWS = 2
CMP_TQ = 1024


def _cparams(sem):
    return pltpu.CompilerParams(dimension_semantics=sem, vmem_limit_bytes=VMEM_LIMIT)


def _norm_matmul_kernel(x_ref, g_ref, w_ref, o_ref, *, tn):
    x = x_ref[...]
    ms = jnp.mean(x * x, axis=-1, keepdims=True)
    h = (x * lax.rsqrt(ms + EPS) * g_ref[...]).astype(BF16)
    for c0 in range(0, w_ref.shape[1], tn):
        o_ref[:, c0:c0 + tn] = jnp.dot(h, w_ref[:, c0:c0 + tn],
                                       preferred_element_type=F32).astype(o_ref.dtype)


def _norm_matmul(x2d, g, w, layer, tm, tn, out_dtype):
    m, k = x2d.shape
    n = w.shape[2]
    return pl.pallas_call(
        functools.partial(_norm_matmul_kernel, tn=tn),
        grid=(m // tm,),
        in_specs=[pl.BlockSpec((tm, k), lambda i: (i, 0)),
                  pl.BlockSpec((1, k), lambda i: (0, 0)),
                  pl.BlockSpec((None, k, n), lambda i: (layer, 0, 0), pipeline_mode=pl.Buffered(1))],
        out_specs=pl.BlockSpec((tm, n), lambda i: (i, 0)),
        out_shape=jax.ShapeDtypeStruct((m, n), out_dtype),
        compiler_params=_cparams(("parallel",)),
        name="norm_matmul",
    )(x2d, g.reshape(1, k), w)


def _pair_rms(x, g2, head_ones):
    sq = x * x
    hi = sq.astype(BF16)
    low = (sq - hi.astype(F32)).astype(BF16)
    ssq = (jnp.dot(hi, head_ones, preferred_element_type=F32)
           + jnp.dot(low, head_ones, preferred_element_type=F32))
    return x * lax.rsqrt(ssq * (1.0 / HEAD_DIM) + EPS) * g2


def _pair_rope(y, cos4, sin_signed, first_half):
    rot = jnp.where(first_half, pltpu.roll(y, LANES - HEAD_DIM // 2, 1), pltpu.roll(y, HEAD_DIM // 2, 1))
    return y * cos4 + rot * sin_signed


def _head_a(y, lo, tail=0.0):
    return jnp.where(lo, y, tail)


def _head_b(y, lo, tail=0.0):
    return jnp.where(lo, pltpu.roll(y, HEAD_DIM, 1), tail)


def _split3_f32(c):
    hi = c.astype(BF16).astype(F32)
    r = c - hi
    mid = r.astype(BF16).astype(F32)
    return hi, mid, r - mid


def _prep_kernel(zq_ref, zc_ref, zs_ref, zw_ref, fq_ref, fk_ref, sm_ref, cos_ref, sin_ref,
                 tri_ref, ones_ref, pq_ref, pk_ref, gq_ref, gks_ref, gkw_ref, gfq_ref, gfk_ref, fb_ref,
                 ss_ref, sw_ref, sf_ref,
                 qn_ref, kcr_ref, vcr_ref, ks_ref, vs_ref, kw_ref, vw_ref, fqd_ref, fqo_ref, fka_ref,
                 base_ref, run_scr):
    ti = pl.program_id(1)
    tt = zq_ref.shape[1]
    lane = lax.broadcasted_iota(jnp.int32, (tt, LANES), 1)
    lo = lane < HEAD_DIM
    first_half = (lane & (HEAD_DIM // 2)) == 0
    cos4 = cos_ref[...]
    sin_signed = sin_ref[...]
    head_ones = ones_ref[...]
    one_at_aug = jnp.where(lane == AUG, 1.0, 0.0)

    for p in range(NSA_HEADS // 2):
        y = _pair_rope(_pair_rms(zq_ref[0, :, p * LANES:(p + 1) * LANES].astype(F32), gq_ref[...], head_ones),
                       cos4, sin_signed, first_half) * QK_SCALE
        qn_ref[0, 2 * p] = _head_a(y, lo, one_at_aug).astype(BF16)
        qn_ref[0, 2 * p + 1] = _head_b(y, lo, one_at_aug).astype(BF16)

    kcr_ref[0] = _pair_rope(zc_ref[0, :, :LANES].astype(F32), cos4, sin_signed, first_half).astype(BF16)
    vcr_ref[0] = zc_ref[0, :, LANES:].astype(BF16)

    sel_lane = jnp.right_shift(ti * tt + lax.broadcasted_iota(jnp.int32, (tt, LANES), 0), SEL_SHIFT)
    onehot = jnp.where(lane == sel_lane, -SEL_MASK, 0.0).astype(BF16)
    for z_ref, g_ref, k_ref, v_ref, bound_ref, with_onehot in (
            (zs_ref, gks_ref, ks_ref, vs_ref, ss_ref, True),
            (zw_ref, gkw_ref, kw_ref, vw_ref, sw_ref, False)):
        y = _pair_rope(_pair_rms(z_ref[0, :, :LANES].astype(F32), g_ref[...], head_ones),
                       cos4, sin_signed, first_half)
        v = z_ref[0, :, LANES:].astype(F32)
        vr = pltpu.roll(v, HEAD_DIM, 1)
        neg_bound = jnp.where(lane == AUG, -bound_ref[...], 0.0)
        k_ref[0, 0, :, :LANES] = _head_a(y, lo, neg_bound).astype(BF16)
        k_ref[0, 1, :, :LANES] = _head_b(y, lo, neg_bound).astype(BF16)
        if with_onehot:
            k_ref[0, 0, :, LANES:] = onehot
            k_ref[0, 1, :, LANES:] = onehot
        v_ref[0, 0] = jnp.where(lo, v, vr).astype(BF16)
        v_ref[0, 1] = jnp.where(lo, vr, v).astype(BF16)

    @pl.when(ti == 0)
    def _reset():
        run_scr[...] = jnp.zeros(run_scr.shape, F32)

    xf = sm_ref[0].astype(F32) + fb_ref[...]
    log_f = (jnp.minimum(xf, 0.0) - jnp.log1p(jnp.exp(-jnp.abs(xf)))) * LOG2E
    tri = tri_ref[...]
    parts = _split3_f32(log_f)
    cum = sum(jnp.dot(tri, part.astype(BF16), preferred_element_type=F32) for part in parts)
    base_ref[0, 0] = run_scr[...]
    run_scr[...] = run_scr[...] + cum[tt - 1:tt, :]

    parts6 = _split3_f32(cum) + _split3_f32(cum - cum[tt - 1:tt, :])
    x6 = jnp.concatenate([part.astype(BF16) for part in parts6], axis=1)
    aug_q = jnp.dot(x6, pq_ref[...], preferred_element_type=F32)
    aug_k = jnp.dot(x6, pk_ref[...], preferred_element_type=F32)

    lane1 = lax.broadcasted_iota(jnp.int32, (1, LANES), 1)
    span = lambda a, b: jnp.where((lane1 >= a) & (lane1 < b), 1.0, 0.0)
    q_tail_diag = span(AUG + 3, AUG + 6) + span(AUG + 9, AUG + 10)
    q_tail_off = span(AUG + 6, AUG + 10)
    k_tail = span(AUG, AUG + 3) + jnp.where(lane1 == AUG + 9, -sf_ref[...], 0.0)
    for p in range(FOX_HEADS // 2):
        sl = slice(p * LANES, (p + 1) * LANES)
        yq = _pair_rms(fq_ref[0, :, sl].astype(F32), gfq_ref[...], head_ones) * QK_SCALE
        yk = _pair_rms(fk_ref[0, :, sl].astype(F32), gfk_ref[...], head_ones)
        for e, split in ((0, _head_a), (1, _head_b)):
            h = 2 * p + e
            hs = slice(h * LANES, (h + 1) * LANES)
            qh = split(yq, lo) + aug_q[:, hs]
            fqd_ref[0, h] = (qh + q_tail_diag).astype(BF16)
            fqo_ref[0, h] = (qh + q_tail_off).astype(BF16)
            fka_ref[0, h] = (split(yk, lo) + aug_k[:, hs] + k_tail).astype(BF16)


def _prep(z, cos4, sin_signed, tri, head_ones, perm_q, perm_k, gq, gks, gkw, gfq, gfk, fb,
          s_sel, s_win, s_fox, *, batch, seq):
    nblk = seq // TQ
    whole = lambda a: pl.BlockSpec(a.shape, lambda b, i: (0, 0))
    zspec = lambda w, c: pl.BlockSpec((1, TQ, w), lambda b, i, c=c: (b, i, c))
    row = pl.BlockSpec((1, LANES), lambda b, i: (0, 0))
    tab = pl.BlockSpec((TQ, LANES), lambda b, i: (i, 0))
    heads = lambda n, w: pl.BlockSpec((1, n, TQ, w), lambda b, i: (b, 0, i, 0))
    tok = lambda w: pl.BlockSpec((1, TQ, w), lambda b, i: (b, i, 0))
    bf = lambda *shape: jax.ShapeDtypeStruct(shape, BF16)
    return pl.pallas_call(
        _prep_kernel,
        grid=(batch, nblk),
        in_specs=[zspec(NSA_WIDTH, C_NSA_Q // NSA_WIDTH),
                  zspec(2 * LANES, C_NSA_KV // (2 * LANES)),
                  zspec(2 * LANES, C_NSA_KV // (2 * LANES) + 1),
                  zspec(2 * LANES, C_NSA_KV // (2 * LANES) + 2),
                  zspec(FOX_WIDTH, C_FOX_Q // FOX_WIDTH),
                  zspec(FOX_WIDTH, C_FOX_K // FOX_WIDTH),
                  zspec(LANES, C_SMALL // LANES),
                  tab, tab, whole(tri), whole(head_ones), whole(perm_q), whole(perm_k),
                  row, row, row, row, row, row, row, row, row],
        out_specs=[heads(NSA_HEADS, LANES), tok(LANES), tok(LANES),
                   heads(NSA_KV_GROUPS, 2 * LANES), heads(NSA_KV_GROUPS, LANES),
                   heads(NSA_KV_GROUPS, LANES), heads(NSA_KV_GROUPS, LANES),
                   heads(FOX_HEADS, LANES), heads(FOX_HEADS, LANES), heads(FOX_HEADS, LANES),
                   pl.BlockSpec((1, 1, 1, LANES), lambda b, i: (b, i, 0, 0))],
        out_shape=[bf(batch, NSA_HEADS, seq, LANES), bf(batch, seq, LANES), bf(batch, seq, LANES),
                   bf(batch, NSA_KV_GROUPS, seq, 2 * LANES), bf(batch, NSA_KV_GROUPS, seq, LANES),
                   bf(batch, NSA_KV_GROUPS, seq, LANES), bf(batch, NSA_KV_GROUPS, seq, LANES),
                   bf(batch, FOX_HEADS, seq, LANES), bf(batch, FOX_HEADS, seq, LANES),
                   bf(batch, FOX_HEADS, seq, LANES),
                   jax.ShapeDtypeStruct((batch, nblk, 1, LANES), F32)],
        scratch_shapes=[pltpu.VMEM((1, LANES), F32)],
        compiler_params=_cparams(("parallel", "arbitrary")),
        name="prep",
    )(z, z, z, z, z, z, z, cos4, sin_signed, tri, head_ones, perm_q, perm_k,
      gq, gks, gkw, gfq, gfk, fb, s_sel, s_win, s_fox)


def _compress_kernel(xk_ref, xv_ref, w1k_ref, w1v_ref, bk_ref, bv_ref, w2k_ref, w2v_ref, kg_ref,
                     sc_ref, kc_ref, vc_ref):
    nchunk = xk_ref.shape[1]
    lane = lax.broadcasted_iota(jnp.int32, (nchunk, LANES), 1)
    for x_ref, w1_ref, b_ref, w2_ref, o_ref, is_key in ((xk_ref, w1k_ref, bk_ref, w2k_ref, kc_ref, True),
                                                        (xv_ref, w1v_ref, bv_ref, w2v_ref, vc_ref, False)):
        h = jnp.dot(x_ref[0], w1_ref[...], preferred_element_type=F32)
        for g in range(NSA_KV_GROUPS):
            c0 = g * 2 * NSA_CMP_HIDDEN
            top = h[:, c0:c0 + NSA_CMP_HIDDEN]
            bot = h[:, c0 + NSA_CMP_HIDDEN:c0 + 2 * NSA_CMP_HIDDEN]
            hid = top + pltpu.roll(bot, nchunk - 1, 0) + b_ref[...]
            act = (hid * jax.nn.sigmoid(hid)).astype(BF16)
            o = jnp.dot(act, w2_ref[...], preferred_element_type=F32)
            if is_key:
                ms = jnp.sum(o * o, axis=1, keepdims=True) * (1.0 / HEAD_DIM)
                o = o * lax.rsqrt(ms + EPS) * kg_ref[...]
                o = jnp.where(lane == AUG, -sc_ref[...], o)
            o_ref[0, g] = o.astype(BF16)


def _compress(xk, xv, w1k, w1v, bk, bv, w2k, w2v, kg, s_cmp, *, batch):
    nchunk, kin = xk.shape[1], xk.shape[2]
    xspec = pl.BlockSpec((1, nchunk, kin), lambda b: (b, 0, 0))
    full = lambda a: pl.BlockSpec(a.shape, lambda b: (0,) * a.ndim)
    ospec = pl.BlockSpec((1, NSA_KV_GROUPS, nchunk, LANES), lambda b: (b, 0, 0, 0))
    oshape = jax.ShapeDtypeStruct((batch, NSA_KV_GROUPS, nchunk, LANES), BF16)
    return pl.pallas_call(
        _compress_kernel,
        grid=(batch,),
        in_specs=[xspec, xspec, full(w1k), full(w1v), full(bk), full(bv), full(w2k), full(w2v), full(kg),
                  full(s_cmp)],
        out_specs=[ospec, ospec],
        out_shape=[oshape, oshape],
        compiler_params=_cparams(("parallel",)),
        name="compress",
    )(xk, xv, w1k, w1v, bk, bv, w2k, w2v, kg, s_cmp)


def _gate_column(gate_blk, lane, idx):
    col = jnp.sum(jnp.where(lane == idx, gate_blk, 0.0), axis=1, keepdims=True)
    return jax.nn.sigmoid(col)


def _flash_kernel(qi_tab, ki_tab, fl_tab, *refs, nheads, shared_kv, window, has_delta,
                  has_u, gate_branch, nblk, fixed_shift, nb):
    refs = list(refs)
    base_ref = refs.pop(0) if has_delta else None
    q_ref = refs.pop(0)
    qoff_ref = refs.pop(0) if has_delta else None
    u_ref = refs.pop(0) if has_u else None
    k_ref = refs.pop(0)
    v_ref = refs.pop(0)
    gate_ref = refs.pop(0) if gate_branch is not None else None
    o_ref = refs.pop(0)
    m_scr = refs.pop(0)
    l_scr = refs.pop(0)
    acc_scr = refs.pop(0)
    qcat_scr = refs.pop(0) if has_u else None

    b = pl.program_id(0)
    g = pl.program_id(1)
    step = (b * pl.num_programs(1) + g) * pl.num_programs(2) + pl.program_id(2)
    qi = qi_tab[step]
    ki = ki_tab[step]
    fl = fl_tab[step]
    active = (fl & 4) == 4
    tq = q_ref.shape[2]
    tk = k_ref.shape[2]

    @pl.when((fl & 1) == 1)
    def _init():
        m_scr[...] = jnp.full(m_scr.shape, NEG, F32)
        l_scr[...] = jnp.zeros(l_scr.shape, F32)
        acc_scr[...] = jnp.zeros(acc_scr.shape, F32)
        if has_u:
            for bb in range(nb):
                for r in range(nheads):
                    qcat_scr[bb * nheads + r, :, :LANES] = q_ref[bb, r]
                    qcat_scr[bb * nheads + r, :, LANES:] = u_ref[bb, r // NSA_GROUP_HEADS]

    lane = lax.broadcasted_iota(jnp.int32, (tq, LANES), 1)
    lo = lane < HEAD_DIM

    def tile(diagonal):
        masked = diagonal or window is not None
        if masked:
            row = lax.broadcasted_iota(jnp.int32, (tq, tk), 0)
            col = lax.broadcasted_iota(jnp.int32, (tq, tk), 1)
            if diagonal:
                valid = row >= col
            else:
                valid = (tq + row - col) < window
        for bb in range(nb):
            slot0, pair0, brow = bb * nheads, bb * (nheads // 2), b * nb + bb
            for p in range(nheads // 2):
                pvs, alphas = [], []
                for e in range(2):
                    hh = 2 * p + e
                    if has_u:
                        q = qcat_scr[slot0 + hh]
                    elif has_delta and not diagonal:
                        q = qoff_ref[bb, hh]
                    else:
                        q = q_ref[bb, hh]
                    k = k_ref[bb, hh // NSA_GROUP_HEADS] if shared_kv else k_ref[bb, hh]
                    v = (v_ref[bb, hh // NSA_GROUP_HEADS] if shared_kv
                         else v_ref[bb, :, p * LANES:(p + 1) * LANES])
                    s = lax.dot_general(q, k, (((1,), (1,)), ((), ())), preferred_element_type=F32)
                    if masked:
                        s = jnp.where(valid, s, NEG)
                    delta = None
                    if has_delta and not diagonal:
                        hoff = F_LANE + g * nheads + hh
                        delta = (base_ref[(brow * nblk + qi) * LANES + hoff]
                                 - base_ref[(brow * nblk + ki + 1) * LANES + hoff])
                    if fixed_shift:
                        pm = jnp.exp2(s)
                        rowsum = pm[:, :LANES]
                        for j in range(1, tk // LANES):
                            rowsum = rowsum + pm[:, j * LANES:(j + 1) * LANES]
                        pv = jnp.dot(pm.astype(BF16), v, preferred_element_type=F32)
                        if delta is not None:
                            w = jnp.exp2(jnp.full((1, LANES), delta, F32))
                            rowsum = rowsum * w
                            pv = pv * w
                        l_scr[slot0 + hh] = l_scr[slot0 + hh] + rowsum
                        pvs.append(pv)
                    else:
                        m_prev = m_scr[slot0 + hh]
                        m_cur = jnp.max(s, axis=1, keepdims=True)
                        if delta is not None:
                            m_cur = m_cur + delta
                        m_new = jnp.maximum(m_prev, m_cur)
                        alpha = jnp.exp2(m_prev - m_new)
                        shift = m_new[:, :1]
                        if delta is not None:
                            shift = shift - delta
                        pm = jnp.exp2(s - shift)
                        l_scr[slot0 + hh] = alpha * l_scr[slot0 + hh] + jnp.sum(pm, axis=1, keepdims=True)
                        m_scr[slot0 + hh] = m_new
                        pvs.append(jnp.dot(pm.astype(BF16), v, preferred_element_type=F32))
                        alphas.append(alpha)
                if fixed_shift:
                    acc_scr[pair0 + p] = acc_scr[pair0 + p] + jnp.where(lo, pvs[0], pvs[1])
                else:
                    acc_scr[pair0 + p] = (acc_scr[pair0 + p] * jnp.where(lo, alphas[0], alphas[1])
                                          + jnp.where(lo, pvs[0], pvs[1]))

    pl.when((qi == ki) & active)(lambda: tile(True))
    pl.when((qi != ki) & active)(lambda: tile(False))

    @pl.when((fl & 2) == 2)
    def _finish():
        for bb in range(nb):
            slot0, pair0 = bb * nheads, bb * (nheads // 2)
            for p in range(nheads // 2):
                if fixed_shift:
                    la = jnp.sum(l_scr[slot0 + 2 * p], axis=1, keepdims=True)
                    lb = jnp.sum(l_scr[slot0 + 2 * p + 1], axis=1, keepdims=True)
                else:
                    la, lb = l_scr[slot0 + 2 * p], l_scr[slot0 + 2 * p + 1]
                inv = jnp.where(lo, 1.0 / la, 1.0 / lb)
                if gate_branch is not None:
                    h0 = g * nheads + 2 * p
                    src = lax.broadcasted_iota(jnp.int32, (LANES, LANES), 0)
                    dst_lo = lax.broadcasted_iota(jnp.int32, (LANES, LANES), 1) < HEAD_DIM
                    pick = jnp.where(src == jnp.where(dst_lo, 3 * h0 + gate_branch,
                                                      3 * (h0 + 1) + gate_branch), 1.0, 0.0)
                    logits = jnp.dot(gate_ref[bb].astype(BF16), pick.astype(BF16),
                                     preferred_element_type=F32)
                    inv = inv * jax.nn.sigmoid(logits)
                o_ref[bb, :, p * LANES:(p + 1) * LANES] = (acc_scr[pair0 + p] * inv).astype(o_ref.dtype)


def _tile_tables(nq, window_tiles, batch):
    qi, ki, fl = [], [], []
    for i in range(nq):
        lo = 0 if window_tiles is None else max(0, i - window_tiles)
        for j in range(lo, i + 1):
            qi.append(i)
            ki.append(j)
            fl.append((1 if j == lo else 0) | (2 if j == i else 0) | 4)
    rows = lambda a: jnp.tile(jnp.asarray(a, jnp.int32), batch)
    return rows(qi), rows(ki), rows(fl)


def _decayed_tile_tables(nq, batch, nb, ngroups, base):
    qi = np.repeat(np.arange(nq), np.arange(1, nq + 1))
    ki = np.concatenate([np.arange(i + 1) for i in range(nq)])
    nsteps = qi.shape[0]
    diag = jnp.asarray(qi == ki)
    gate = base.reshape(batch, nq, LANES)[:, :, F_LANE:F_LANE + FOX_HEADS]
    delta = gate[:, qi, :] - gate[:, np.minimum(ki + 1, nq - 1), :]
    top = jnp.max(delta.reshape(batch // nb, nb, nsteps, ngroups, FOX_HEADS // ngroups), axis=(1, 4))
    top = top.transpose(0, 2, 1).reshape(batch // nb * ngroups, nsteps)
    active = diag[None, :] | (top > ZERO_WEIGHT_LOG2)
    same_q = jnp.asarray(qi[:, None] == np.arange(nq)[None, :])
    first_ki = jnp.min(jnp.where(active[:, :, None] & same_q[None], ki[None, :, None], nq), axis=1)
    first = active & (jnp.asarray(ki)[None, :] == first_ki[:, qi])
    flags = first.astype(jnp.int32) + 2 * diag[None, :].astype(jnp.int32) + 4 * active.astype(jnp.int32)
    order = jnp.argsort(jnp.logical_not(active), axis=1, stable=True)
    n_active = jnp.sum(active, axis=1, keepdims=True)
    slot = jnp.arange(nsteps)[None, :]
    src = jnp.take_along_axis(order, jnp.minimum(slot, n_active - 1), axis=1)
    kept = slot < n_active
    take = lambda a: jnp.asarray(a, jnp.int32)[src].reshape(-1)
    fl = jnp.where(kept, jnp.take_along_axis(flags, src, axis=1), 0).reshape(-1)
    return take(qi), take(ki), fl.astype(jnp.int32)


def _flash(q, k, v, *, batch, seq, shared_kv, fixed_shift, nheads=NSA_HEADS, window=None, base=None,
           q_off=None, u=None, z=None, gate_branch=None, v_offset=0):
    nb = FLASH_ROWS if fixed_shift else SAFE_FLASH_ROWS
    nb = nb if batch % nb == 0 else 1
    nq = seq // TQ
    ngroups = NSA_HEADS // nheads
    nkv = NSA_KV_GROUPS
    da = k.shape[-1]
    has_delta = base is not None
    has_u = u is not None
    out_w = LANES * (nheads // 2)
    v_col = v_offset // out_w
    table_rows = batch // nb * ngroups
    if has_delta and fixed_shift:
        tabs = _decayed_tile_tables(nq, batch, nb, ngroups, base)
    else:
        tabs = _tile_tables(nq, None if window is None else window // TQ, table_rows)
    nsteps = int(tabs[0].shape[0]) // table_rows

    at = lambda tab, b, g, s: tab[(b * ngroups + g) * nsteps + s]
    by_q_heads = lambda b, g, s, qt, kt, ft: (b, g, at(qt, b, g, s), 0)
    by_k_heads = lambda b, g, s, qt, kt, ft: (b, g, at(kt, b, g, s), 0)
    by_q_tokens = lambda col: (lambda b, g, s, qt, kt, ft: (b, at(qt, b, g, s), g if col is None else col))
    by_k_tokens = lambda b, g, s, qt, kt, ft: (b, at(kt, b, g, s), g + v_col)

    in_specs, args = [], []
    if has_delta:
        in_specs.append(pl.BlockSpec(memory_space=pltpu.SMEM))
        args.append(base)
    in_specs.append(pl.BlockSpec((nb, nheads, TQ, LANES), by_q_heads))
    args.append(q)
    if has_delta:
        in_specs.append(pl.BlockSpec((nb, nheads, TQ, LANES), by_q_heads))
        args.append(q_off)
    if has_u:
        in_specs.append(pl.BlockSpec((nb, nkv, TQ, LANES), by_q_heads))
        args.append(u)
    if shared_kv:
        in_specs.append(pl.BlockSpec((nb, nkv, TQ, da), by_k_heads))
        in_specs.append(pl.BlockSpec((nb, nkv, TQ, LANES), by_k_heads))
    else:
        in_specs.append(pl.BlockSpec((nb, nheads, TQ, da), by_k_heads))
        in_specs.append(pl.BlockSpec((nb, TQ, out_w), by_k_tokens))
    args += [k, v]
    if gate_branch is not None:
        in_specs.append(pl.BlockSpec((nb, TQ, LANES), by_q_tokens(C_SMALL // LANES)))
        args.append(z)

    m_rows = 8 if fixed_shift else TQ
    scratch = [pltpu.VMEM((nb * nheads, m_rows, LANES), F32),
               pltpu.VMEM((nb * nheads, TQ, LANES), F32),
               pltpu.VMEM((nb * nheads // 2, TQ, LANES), F32)]
    if has_u:
        scratch.append(pltpu.VMEM((nb * nheads, TQ, 2 * LANES), BF16))

    kern = functools.partial(_flash_kernel, nheads=nheads, shared_kv=shared_kv, window=window,
                             has_delta=has_delta, has_u=has_u, gate_branch=gate_branch, nblk=nq,
                             fixed_shift=fixed_shift, nb=nb)
    return pl.pallas_call(
        kern,
        grid_spec=pltpu.PrefetchScalarGridSpec(
            num_scalar_prefetch=3,
            grid=(batch // nb, ngroups, nsteps),
            in_specs=in_specs,
            out_specs=pl.BlockSpec((nb, TQ, out_w), by_q_tokens(None)),
            scratch_shapes=scratch),
        out_shape=jax.ShapeDtypeStruct((batch, seq, 4 * LANES), BF16),
        compiler_params=_cparams(("parallel", "parallel", "arbitrary")),
        name="flash_" + ("fox" if has_delta else ("sel" if has_u else "win")),
    )(*tabs, *args)


def _cmp_kernel(q_ref, kc_ref, vc_ref, ov_ref, gate_ref, o_ref, u_ref):
    g = pl.program_id(1)
    qi = pl.program_id(2)
    tq = q_ref.shape[2]
    ncmp = kc_ref.shape[2]
    nslc = ov_ref.shape[1]

    t = qi * tq + lax.broadcasted_iota(jnp.int32, (tq, ncmp), 0)
    n = lax.broadcasted_iota(jnp.int32, (tq, ncmp), 1)
    valid = (n * NSA_CMP_STRIDE + (NSA_CMP_LEN - 1)) <= t
    kc = kc_ref[0, 0]
    vc = vc_ref[0, 0]
    ov = ov_ref[...]
    lane = lax.broadcasted_iota(jnp.int32, (tq, LANES), 1)
    lo = lane < HEAD_DIM
    gate_blk = gate_ref[0].astype(F32)

    imp = jnp.zeros((tq, nslc), F32)
    outs = []
    for r in range(NSA_GROUP_HEADS):
        s = lax.dot_general(q_ref[0, r], kc, (((1,), (1,)), ((), ())), preferred_element_type=F32)
        s = jnp.where(valid, s, NEG)
        m = jnp.max(s, axis=1, keepdims=True)
        e = jnp.where(valid, jnp.exp2(s - m), 0.0)
        d = jnp.sum(e, axis=1, keepdims=True)
        pb = (e / jnp.where(d > 0.0, d, 1.0)).astype(BF16)
        o = jnp.dot(pb, vc, preferred_element_type=F32)
        gcol = _gate_column(gate_blk, lane, 3 * (g * NSA_GROUP_HEADS + r))
        outs.append(o * gcol)
        imp = imp + jnp.dot(pb, ov, preferred_element_type=F32)
    for p in range(NSA_GROUP_HEADS // 2):
        o_ref[0, :, p * LANES:(p + 1) * LANES] = jnp.where(lo, outs[2 * p], outs[2 * p + 1]).astype(o_ref.dtype)
    _write_unselected(u_ref, imp, qi, tq, nslc)


def _cmp_fixed_kernel(q_ref, kc_ref, vc_ref, ov_ref, gate_ref, o_ref, u_ref, d_scr, acc_scr, imp_scr,
                      *, chunk):
    g = pl.program_id(1)
    qi = pl.program_id(2)
    tq = q_ref.shape[2]
    ncmp = kc_ref.shape[2]
    nslc = ov_ref.shape[1]
    t0 = qi * tq
    d_scr[...] = jnp.zeros(d_scr.shape, F32)
    acc_scr[...] = jnp.zeros(acc_scr.shape, F32)
    imp_scr[...] = jnp.zeros(imp_scr.shape, F32)

    for c in range(ncmp // chunk):
        first_end = c * chunk * NSA_CMP_STRIDE + NSA_CMP_LEN - 1
        last_end = ((c + 1) * chunk - 1) * NSA_CMP_STRIDE + NSA_CMP_LEN - 1

        def body(masked, c=c):
            rows = slice(c * chunk, (c + 1) * chunk)
            kc = kc_ref[0, 0, rows, :]
            vo = jnp.concatenate([vc_ref[0, 0, rows, :], ov_ref[rows, :]], axis=1)
            if masked:
                t = t0 + lax.broadcasted_iota(jnp.int32, (tq, chunk), 0)
                n = c * chunk + lax.broadcasted_iota(jnp.int32, (tq, chunk), 1)
                valid = (n * NSA_CMP_STRIDE + (NSA_CMP_LEN - 1)) <= t
            for r in range(NSA_GROUP_HEADS):
                s = lax.dot_general(q_ref[0, r], kc, (((1,), (1,)), ((), ())), preferred_element_type=F32)
                if masked:
                    s = jnp.where(valid, s, NEG)
                e = jnp.exp2(s)
                part = e[:, :LANES]
                for j in range(1, chunk // LANES):
                    part = part + e[:, j * LANES:(j + 1) * LANES]
                d_scr[r] = d_scr[r] + part
                both = jnp.dot(e.astype(BF16), vo, preferred_element_type=F32)
                acc_scr[r] = acc_scr[r] + both[:, :LANES]
                imp_scr[r] = imp_scr[r] + both[:, LANES:]

        pl.when((first_end <= t0 + (tq - 1)) & (last_end > t0))(functools.partial(body, True))
        pl.when(last_end <= t0)(functools.partial(body, False))

    lane = lax.broadcasted_iota(jnp.int32, (tq, LANES), 1)
    lo = lane < HEAD_DIM
    src = lax.broadcasted_iota(jnp.int32, (LANES, LANES), 0)
    dst_lo = lax.broadcasted_iota(jnp.int32, (LANES, LANES), 1) < HEAD_DIM
    imp = jnp.zeros((tq, nslc), F32)
    for p in range(NSA_GROUP_HEADS // 2):
        normed = []
        for r in (2 * p, 2 * p + 1):
            d = jnp.sum(d_scr[r], axis=1, keepdims=True)
            inv = 1.0 / jnp.where(d > 0.0, d, 1.0)
            normed.append(acc_scr[r] * inv)
            imp = imp + imp_scr[r] * inv
        h0 = g * NSA_GROUP_HEADS + 2 * p
        pick = jnp.where(src == jnp.where(dst_lo, 3 * h0, 3 * (h0 + 1)), 1.0, 0.0).astype(BF16)
        logits = jnp.dot(gate_ref[0].astype(BF16), pick, preferred_element_type=F32)
        o_ref[0, :, p * LANES:(p + 1) * LANES] = (jnp.where(lo, normed[0], normed[1])
                                                  * jax.nn.sigmoid(logits)).astype(o_ref.dtype)
    _write_unselected(u_ref, imp, qi, tq, nslc)


N_FORCED = 3


def _write_unselected(u_ref, imp, qi, tq, nslc):
    tpos = qi * tq + lax.broadcasted_iota(jnp.int32, (tq, nslc), 0)
    jblk = lax.broadcasted_iota(jnp.int32, (tq, nslc), 1)
    cur = jnp.right_shift(tpos, SEL_SHIFT)
    forced = (jblk == 0) | (jblk == cur) | (jblk == cur - 1)
    visible = jblk <= cur
    score_t = jnp.where(visible & jnp.logical_not(forced), imp, -jnp.inf).T

    def run(nrows):
        sc = score_t[:nrows]
        jt = lax.broadcasted_iota(jnp.int32, (nrows, tq), 0).astype(F32)
        for _ in range(NSA_SEL_TOPK - N_FORCED):
            mx = jnp.max(sc, axis=0, keepdims=True)
            first = jnp.min(jnp.where(sc == mx, jt, float(nslc)), axis=0, keepdims=True)
            sc = jnp.where(jt == first, -jnp.inf, sc)
        taken = jnp.where(sc == -jnp.inf, 1.0, 0.0)
        if nrows < nslc:
            taken = jnp.concatenate([taken, jnp.zeros((nslc - nrows, tq), F32)], axis=0)
        chosen = visible & (forced | (taken.T > 0.5))
        u_ref[0, 0] = jnp.where(chosen, 0.0, 1.0).astype(BF16)

    half = nslc // 2
    last_block = jnp.right_shift(qi * tq + (tq - 1), SEL_SHIFT)
    pl.when(last_block < half)(functools.partial(run, half))
    pl.when(last_block >= half)(functools.partial(run, nslc))


def _cmp_select(qn, kc, vc, ov, z, *, batch, seq, fixed_shift):
    ncmp = kc.shape[2]
    nslc = ov.shape[1]
    if fixed_shift:
        chunk = 2 * LANES if ncmp % (2 * LANES) == 0 else ncmp
        kern = functools.partial(_cmp_fixed_kernel, chunk=chunk)
        scratch = [pltpu.VMEM((NSA_GROUP_HEADS, CMP_TQ, LANES), F32) for _ in range(3)]
    else:
        kern, scratch = _cmp_kernel, []
    return pl.pallas_call(
        kern,
        scratch_shapes=scratch,
        grid=(batch, NSA_KV_GROUPS, seq // CMP_TQ),
        in_specs=[pl.BlockSpec((1, NSA_GROUP_HEADS, CMP_TQ, LANES), lambda b, g, i: (b, g, i, 0)),
                  pl.BlockSpec((1, 1, ncmp, LANES), lambda b, g, i: (b, g, 0, 0)),
                  pl.BlockSpec((1, 1, ncmp, LANES), lambda b, g, i: (b, g, 0, 0)),
                  pl.BlockSpec((ncmp, nslc), lambda b, g, i: (0, 0)),
                  pl.BlockSpec((1, CMP_TQ, LANES), lambda b, g, i: (b, i, C_SMALL // LANES))],
        out_specs=[pl.BlockSpec((1, CMP_TQ, 2 * LANES), lambda b, g, i: (b, i, g)),
                   pl.BlockSpec((1, 1, CMP_TQ, nslc), lambda b, g, i: (b, g, i, 0))],
        out_shape=[jax.ShapeDtypeStruct((batch, seq, NSA_WIDTH), BF16),
                   jax.ShapeDtypeStruct((batch, NSA_KV_GROUPS, seq, nslc), BF16)],
        compiler_params=_cparams(("parallel", "parallel", "parallel")),
        name="cmp_select",
    )(qn, kc, vc, ov, z)


def _head_rms(x, g):
    return x * lax.rsqrt(jnp.mean(x * x, axis=-1, keepdims=True) + EPS) * g


def _mem_kernel(zq_ref, kv_ref, qg_ref, kg_ref, o_ref, kn_scr, vb_scr):
    @pl.when(pl.program_id(1) == 0)
    def _prep():
        for h in range(MEM_HEADS):
            kh = kv_ref[0, :, h * MEM_HEAD_DIM:(h + 1) * MEM_HEAD_DIM]
            kn_scr[h] = _head_rms(kh, kg_ref[...]).astype(BF16)
            vb_scr[h] = kv_ref[0, :, MEM_WIDTH + h * MEM_HEAD_DIM:
                               MEM_WIDTH + (h + 1) * MEM_HEAD_DIM].astype(BF16)

    for h in range(MEM_HEADS):
        sl = slice(h * MEM_HEAD_DIM, (h + 1) * MEM_HEAD_DIM)
        qh = (_head_rms(zq_ref[0, :, sl].astype(F32), qg_ref[...])
              * ((MEM_HEAD_DIM ** -0.5) * LOG2E)).astype(BF16)
        s = lax.dot_general(qh, kn_scr[h], (((1,), (1,)), ((), ())), preferred_element_type=F32)
        m = jnp.max(s, axis=1, keepdims=True)
        e = jnp.exp2(s - m)
        p = (e / jnp.sum(e, axis=1, keepdims=True)).astype(BF16)
        o_ref[0, :, sl] = jnp.dot(p, vb_scr[h], preferred_element_type=F32).astype(o_ref.dtype)


def _mem_attention(z, mem_kv, q_g, k_g, *, batch, seq):
    mlen = mem_kv.shape[1]
    tq = TQ
    return pl.pallas_call(
        _mem_kernel,
        grid=(batch, seq // tq),
        in_specs=[pl.BlockSpec((1, tq, MEM_WIDTH), lambda b, i: (b, i, C_MEM_Q // MEM_WIDTH)),
                  pl.BlockSpec((1, mlen, 2 * MEM_WIDTH), lambda b, i: (b, 0, 0)),
                  pl.BlockSpec((1, MEM_HEAD_DIM), lambda b, i: (0, 0)),
                  pl.BlockSpec((1, MEM_HEAD_DIM), lambda b, i: (0, 0))],
        out_specs=pl.BlockSpec((1, tq, MEM_WIDTH), lambda b, i: (b, i, 0)),
        out_shape=jax.ShapeDtypeStruct((batch, seq, MEM_WIDTH), BF16),
        scratch_shapes=[pltpu.VMEM((MEM_HEADS, mlen, MEM_HEAD_DIM), BF16),
                        pltpu.VMEM((MEM_HEADS, mlen, MEM_HEAD_DIM), BF16)],
        compiler_params=_cparams(("parallel", "arbitrary")),
        name="mem_attention",
    )(z, mem_kv, q_g.reshape(1, MEM_HEAD_DIM), k_g.reshape(1, MEM_HEAD_DIM))


def _sigmoid(x):
    return 0.5 * jnp.tanh(0.5 * x) + 0.5


def _out_kernel(x_ref, oc_ref, os_ref, ow_ref, ob_ref, om_ref, sa_ref, sb_ref, sm_ref,
                g0_ref, g1_ref, g2_ref, wa_ref, wb_ref, wm_ref, wo_ref, y_ref):
    silu = lambda ref: (lambda x: x * _sigmoid(x))(ref[...].astype(F32))
    oa = (oc_ref[...].astype(F32) + os_ref[...].astype(F32) + ow_ref[...].astype(F32)) * silu(sa_ref)
    ob = ob_ref[...].astype(F32) * silu(sb_ref)
    om = om_ref[...].astype(F32) * silu(sm_ref)
    gate = lambda ref: _sigmoid(ref[...].astype(F32))
    u = (gate(g0_ref) * jnp.dot(oa.astype(BF16), wa_ref[...], preferred_element_type=F32)
         + gate(g1_ref) * jnp.dot(ob.astype(BF16), wb_ref[...], preferred_element_type=F32)
         + gate(g2_ref) * jnp.dot(om.astype(BF16), wm_ref[...], preferred_element_type=F32))
    y_ref[...] = x_ref[...] + jnp.dot(u.astype(BF16), wo_ref[...], preferred_element_type=F32)


def _out_proj(x2d, o_c, o_s, o_w, o_b, o_m, z2d, wa, wb, wm, wo, layer):
    m = x2d.shape[0]
    tm = 256
    w512 = 512
    row512 = lambda c: pl.BlockSpec((tm, w512), lambda i, c=c: (i, c))
    row1024 = lambda c: pl.BlockSpec((tm, D_MODEL), lambda i, c=c: (i, c))
    full = lambda shape: pl.BlockSpec((None,) + shape, lambda i: (layer, 0, 0))
    return pl.pallas_call(
        _out_kernel,
        grid=(m // tm,),
        in_specs=[row1024(0), row512(0), row512(0), row512(0), row512(0), row512(0),
                  row512(C_NSA_SILU // w512), row512(C_FOX_SILU // w512), row512(C_MEM_SILU // w512),
                  row1024(C_MERGE // D_MODEL), row1024(C_MERGE // D_MODEL + 1),
                  row1024(C_MERGE // D_MODEL + 2),
                  full((w512, D_MODEL)), full((w512, D_MODEL)), full((w512, D_MODEL)),
                  full((D_MODEL, D_MODEL))],
        out_specs=row1024(0),
        out_shape=jax.ShapeDtypeStruct((m, D_MODEL), F32),
        compiler_params=_cparams(("parallel",)),
        name="out_proj",
    )(x2d, o_c, o_s, o_w, o_b, o_m, z2d, z2d, z2d, z2d, z2d, z2d, wa, wb, wm, wo)


def _fox_head_order(a, order, sections):
    nl, d, _ = a.shape
    a = a.reshape(nl, d, sections, FOX_HEADS, HEAD_DIM)
    return jnp.take_along_axis(a, order[:, None, None, :, None], axis=3).reshape(nl, d, -1)


def _permute_w_in(w_in, fox_order):
    o = np.cumsum([0, NSA_WIDTH, 6 * NSA_KV_GROUPS * HEAD_DIM, 3 * NSA_HEADS, NSA_WIDTH,
                   3 * FOX_WIDTH, FOX_HEADS, FOX_WIDTH, MEM_WIDTH, MEM_WIDTH, 3 * D_MODEL])
    nsa_q, nsa_kv, nsa_gate, nsa_silu, fox_qkv, fox_f, fox_silu, mem_q, mem_silu, merge = [
        w_in[..., o[i]:o[i + 1]] for i in range(10)]
    fox_qkv = _fox_head_order(fox_qkv, fox_order, 3)
    fox_silu = _fox_head_order(fox_silu, fox_order, 1)
    fox_f = jnp.take_along_axis(fox_f, fox_order[:, None, :], axis=2)
    pad = jnp.zeros(w_in.shape[:2] + (LANES - 3 * NSA_HEADS - FOX_HEADS,), w_in.dtype)
    return jnp.concatenate([nsa_q, nsa_silu, fox_qkv, fox_silu, mem_q, mem_silu, merge, nsa_kv,
                            nsa_gate, fox_f, pad], axis=-1).astype(BF16)


def _pad_lanes(x, width=LANES):
    return jnp.pad(x, [(0, 0)] * (x.ndim - 1) + [(0, width - x.shape[-1])])


def _tile2(g):
    return jnp.concatenate([g, g], axis=-1)[:, None, :]


def _overlap(n_cmp_pad, n_slc):
    cs = np.arange(n_cmp_pad)[:, None] * NSA_CMP_STRIDE
    ss = np.arange(NSLC_PAD)[None, :] * NSA_SEL_LEN
    ov = np.clip(np.minimum(cs + NSA_CMP_LEN, ss + NSA_SEL_LEN) - np.maximum(cs, ss), 0, None)
    ov = ov / NSA_CMP_LEN
    ov[-1] = 0.0
    ov[:, n_slc:] = 0.0
    return jnp.asarray(ov, BF16)


def _aug_permutations():
    pq = np.zeros((6 * LANES, FOX_HEADS * LANES), np.float32)
    pk = np.zeros((6 * LANES, FOX_HEADS * LANES), np.float32)
    for h in range(FOX_HEADS):
        for i in range(3):
            pq[i * LANES + F_LANE + h, h * LANES + AUG + i] = 1.0
            pk[i * LANES + F_LANE + h, h * LANES + AUG + 3 + i] = -1.0
            pk[(3 + i) * LANES + F_LANE + h, h * LANES + AUG + 6 + i] = -1.0
    return jnp.asarray(pq, BF16), jnp.asarray(pk, BF16)


def _compress_w1(w1):
    nl = w1.shape[0]
    halves = w1.reshape(nl, 2, NSA_CMP_STRIDE, 1, HEAD_DIM, NSA_CMP_HIDDEN)
    eye = jnp.eye(NSA_KV_GROUPS, dtype=w1.dtype)
    out = jnp.einsum("pg,zhldn->zlpdghn", eye, halves[:, :, :, 0])
    return out.reshape(nl, NSA_CMP_STRIDE * NSA_KV_GROUPS * HEAD_DIM,
                       NSA_KV_GROUPS * 2 * NSA_CMP_HIDDEN).astype(BF16)


def _layer(x, mem, p, big, layer, consts):
    batch, seq, _ = x.shape
    m_rows = batch * seq
    cos4, sin_signed, tri, head_ones, perm_q, perm_k, ov = consts
    z2d = _norm_matmul(x.reshape(m_rows, D_MODEL), p["norm_g"], big["w_in"], layer, 512, Z_WIDTH // 3, BF16)
    z = z2d.reshape(batch, seq, Z_WIDTH)

    (qn, kcr, vcr, ks, vs, kw, vw, fqd, fqo, fka, base) = _prep(
        z, cos4, sin_signed, tri, head_ones, perm_q, perm_k,
        p["gq"], p["gks"], p["gkw"], p["gfq"], p["gfk"], p["fb"],
        p["s_sel"], p["s_win"], p["s_fox"], batch=batch, seq=seq)

    nchunk = seq // NSA_CMP_STRIDE
    kc, vc = _compress(kcr.reshape(batch, nchunk, NSA_CMP_STRIDE * LANES),
                       vcr.reshape(batch, nchunk, NSA_CMP_STRIDE * LANES),
                       p["w1k"], p["w1v"], p["bk"], p["bv"], p["w2k"], p["w2v"], p["gkc"], p["s_cmp"],
                       batch=batch)

    def attend(fixed_shift):
        o_c, unsel = _cmp_select(qn, kc, vc, ov, z, batch=batch, seq=seq, fixed_shift=fixed_shift)
        o_s = _flash(qn, ks, vs, batch=batch, seq=seq, shared_kv=True, fixed_shift=fixed_shift,
                     u=unsel, z=z, gate_branch=1)
        o_w = _flash(qn, kw, vw, batch=batch, seq=seq, shared_kv=True, fixed_shift=fixed_shift,
                     window=NSA_WINDOW, z=z, gate_branch=2)
        o_b = _flash(fqd, fka, z, batch=batch, seq=seq, shared_kv=False, fixed_shift=fixed_shift,
                     nheads=FOX_HEADS // 2, base=base.reshape(-1), q_off=fqo, v_offset=C_FOX_V)
        return o_c, o_s, o_w, o_b

    o_c, o_s, o_w, o_b = lax.cond(p["bound_ok"], lambda: attend(True), lambda: attend(False))

    mlen = mem.shape[1]
    mem_kv = _norm_matmul(mem.reshape(batch * mlen, D_MODEL), p["mem_norm_g"], big["w_mem_kv"], layer,
                          min(512, batch * mlen), 2 * MEM_WIDTH, F32).reshape(batch, mlen, 2 * MEM_WIDTH)
    o_m = _mem_attention(z, mem_kv, p["mem_q_norm"], p["mem_k_norm"], batch=batch, seq=seq)

    y = _out_proj(x.reshape(m_rows, D_MODEL), o_c.reshape(m_rows, -1), o_s.reshape(m_rows, -1),
                  o_w.reshape(m_rows, -1), o_b.reshape(m_rows, -1), o_m.reshape(m_rows, -1), z2d,
                  big["w_branch_a"], big["w_branch_b"], big["w_branch_m"], big["w_out"], layer)
    return y.reshape(batch, seq, D_MODEL)


def kernel(x, mem, norm_g, mem_norm_g, w_in, nsa_q_norm, nsa_k_norm, cmp_pe_k, cmp_w1_k, cmp_w2_k,
           cmp_pe_v, cmp_w1_v, cmp_w2_v, fox_q_norm, fox_k_norm, fox_f_bias, mem_q_norm, mem_k_norm,
           w_mem_kv, w_branch_a, w_branch_b, w_branch_m, w_out):
    batch, seq, _ = x.shape
    depth = w_in.shape[0]
    n_slc = seq // NSA_SEL_LEN
    assert n_slc <= NSLC_PAD and seq % TQ == 0 and NSA_WINDOW == TQ

    half = HEAD_DIM // 2
    inv_freq = ROPE_THETA ** (-jnp.arange(half, dtype=F32) / half)
    ang = jnp.arange(seq).astype(F32)[:, None] * inv_freq[None, :]
    cos, sin = jnp.cos(ang), jnp.sin(ang)
    cos4 = jnp.concatenate([cos, cos, cos, cos], axis=-1)
    sin_signed = jnp.concatenate([-sin, sin, -sin, sin], axis=-1)
    tri = jnp.asarray(np.tril(np.ones((TQ, TQ), np.float32)), BF16)
    head_ones = jnp.asarray(np.kron(np.eye(2), np.ones((HEAD_DIM, HEAD_DIM))), BF16)
    consts = (cos4, sin_signed, tri, head_ones, *_aug_permutations(),
              _overlap(seq // NSA_CMP_STRIDE, n_slc))

    hp = lax.Precision.HIGHEST
    pe_bias = lambda pe, w1: jnp.einsum("lk,lkn->ln", pe.reshape(depth, -1), w1, precision=hp)[:, None, :]
    fox_order = jnp.argsort(fox_f_bias, axis=-1)
    fox_bias_sorted = jnp.take_along_axis(fox_f_bias, fox_order, axis=-1)
    w_branch_b_sorted = jnp.take_along_axis(
        w_branch_b.reshape(depth, FOX_HEADS, HEAD_DIM, D_MODEL), fox_order[:, :, None, None], axis=1
    ).reshape(depth, FOX_WIDTH, D_MODEL)
    fb = jnp.zeros((depth, 1, LANES), F32).at[:, 0, F_LANE:F_LANE + FOX_HEADS].set(fox_bias_sorted)

    def logit_bound(gq, gk):
        bound = (BOUND_MARGIN * HEAD_DIM * QK_SCALE
                 * jnp.max(jnp.abs(gq), axis=-1) * jnp.max(jnp.abs(gk), axis=-1))
        return bound.astype(BF16).astype(F32)

    bounds = [logit_bound(nsa_q_norm, nsa_k_norm[:, 1]), logit_bound(nsa_q_norm, nsa_k_norm[:, 2]),
              logit_bound(fox_q_norm, fox_k_norm), logit_bound(nsa_q_norm, nsa_k_norm[:, 0])]
    bound_ok = functools.reduce(jnp.maximum, bounds) <= MAX_FIXED_BOUND
    bound_row = lambda s: jnp.broadcast_to(s[:, None, None], (depth, 1, LANES))
    stacked = dict(
        s_sel=bound_row(bounds[0]), s_win=bound_row(bounds[1]), s_fox=bound_row(bounds[2]),
        s_cmp=bound_row(bounds[3]),
        bound_ok=bound_ok,
        norm_g=norm_g, mem_norm_g=mem_norm_g,
        gq=_tile2(nsa_q_norm), gks=_tile2(nsa_k_norm[:, 1]), gkw=_tile2(nsa_k_norm[:, 2]),
        gkc=_pad_lanes(nsa_k_norm[:, 0])[:, None, :], gfq=_tile2(fox_q_norm), gfk=_tile2(fox_k_norm),
        fb=fb, w1k=_compress_w1(cmp_w1_k), w1v=_compress_w1(cmp_w1_v),
        bk=pe_bias(cmp_pe_k, cmp_w1_k), bv=pe_bias(cmp_pe_v, cmp_w1_v),
        w2k=_pad_lanes(cmp_w2_k).astype(BF16),
        w2v=jnp.concatenate([cmp_w2_v, cmp_w2_v], axis=-1).astype(BF16),
        mem_q_norm=mem_q_norm, mem_k_norm=mem_k_norm)
    big = dict(w_in=_permute_w_in(w_in, fox_order), w_mem_kv=w_mem_kv.astype(BF16),
               w_branch_a=w_branch_a.astype(BF16), w_branch_b=w_branch_b_sorted.astype(BF16),
               w_branch_m=w_branch_m.astype(BF16), w_out=w_out.astype(BF16))
    for l in range(depth):
        x = _layer(x, mem, {k: v[l] for k, v in stacked.items()}, big, l, consts)
    return x
```

```python
import functools

import numpy as np
import jax
import jax.numpy as jnp
from jax import lax
from jax.experimental import pallas as pl
from jax.experimental.pallas import tpu as pltpu

F32 = jnp.float32
BF16 = jnp.bfloat16

D_MODEL = 1024
HEAD_DIM = 64
ROPE_THETA = 10000.0
EPS = 1e-6
NSA_HEADS = 8
NSA_KV_GROUPS = 2
NSA_GROUP_HEADS = NSA_HEADS // NSA_KV_GROUPS
NSA_CMP_LEN = 32
NSA_CMP_STRIDE = 16
NSA_CMP_HIDDEN = 128
NSA_SEL_LEN = 64
NSA_SEL_TOPK = 16
NSA_WINDOW = 512
FORCE_SCORE = 1e9
NSA_WIDTH = NSA_HEADS * HEAD_DIM
FOX_HEADS = 8
FOX_WIDTH = FOX_HEADS * HEAD_DIM
MEM_HEADS = 4
MEM_HEAD_DIM = 128
MEM_WIDTH = MEM_HEADS * MEM_HEAD_DIM

LANES = 128
VMEM_LIMIT = 56 * 1024 * 1024
NEG = -1e30
SEL_MASK = 32768.0
SEL_SHIFT = 6
NSLC_PAD = LANES
LOG2E = 1.4426950408889634
QK_SCALE = (HEAD_DIM ** -0.5) * LOG2E
BOUND_MARGIN = 1.02
MAX_FIXED_BOUND = 60.0
ZERO_WEIGHT_LOG2 = -160.0

C_NSA_Q = 0
C_NSA_SILU = 512
C_FOX_Q = 1024
C_FOX_K = 1536
C_FOX_V = 2048
C_FOX_SILU = 2560
C_MEM_Q = 3072
C_MEM_SILU = 3584
C_MERGE = 4096
C_NSA_KV = 7168
C_SMALL = 7936
Z_WIDTH = 8064
F_LANE = 3 * NSA_HEADS
AUG = HEAD_DIM

TQ = 512
FLASH_ROWS = 4
SAFE_FLASH_ROWS = 2
CMP_TQ = 1024


def _cparams(sem):
    return pltpu.CompilerParams(dimension_semantics=sem, vmem_limit_bytes=VMEM_LIMIT)


def _norm_matmul_kernel(x_ref, g_ref, w_ref, o_ref, *, tn):
    x = x_ref[...]
    ms = jnp.mean(x * x, axis=-1, keepdims=True)
    h = (x * lax.rsqrt(ms + EPS) * g_ref[...]).astype(BF16)
    for c0 in range(0, w_ref.shape[1], tn):
        o_ref[:, c0:c0 + tn] = jnp.dot(h, w_ref[:, c0:c0 + tn],
                                       preferred_element_type=F32).astype(o_ref.dtype)


def _norm_matmul(x2d, g, w, layer, tm, tn, out_dtype):
    m, k = x2d.shape
    n = w.shape[2]
    return pl.pallas_call(
        functools.partial(_norm_matmul_kernel, tn=tn),
        grid=(m // tm,),
        in_specs=[pl.BlockSpec((tm, k), lambda i: (i, 0)),
                  pl.BlockSpec((1, k), lambda i: (0, 0)),
                  pl.BlockSpec((None, k, n), lambda i: (layer, 0, 0), pipeline_mode=pl.Buffered(1))],
        out_specs=pl.BlockSpec((tm, n), lambda i: (i, 0)),
        out_shape=jax.ShapeDtypeStruct((m, n), out_dtype),
        compiler_params=_cparams(("parallel",)),
        name="norm_matmul",
    )(x2d, g.reshape(1, k), w)


def _pair_rms(x, g2, head_ones):
    sq = x * x
    hi = sq.astype(BF16)
    low = (sq - hi.astype(F32)).astype(BF16)
    ssq = (jnp.dot(hi, head_ones, preferred_element_type=F32)
           + jnp.dot(low, head_ones, preferred_element_type=F32))
    return x * lax.rsqrt(ssq * (1.0 / HEAD_DIM) + EPS) * g2


def _pair_rope(y, cos4, sin_signed, first_half):
    rot = jnp.where(first_half, pltpu.roll(y, LANES - HEAD_DIM // 2, 1), pltpu.roll(y, HEAD_DIM // 2, 1))
    return y * cos4 + rot * sin_signed


def _head_a(y, lo, tail=0.0):
    return jnp.where(lo, y, tail)


def _head_b(y, lo, tail=0.0):
    return jnp.where(lo, pltpu.roll(y, HEAD_DIM, 1), tail)


def _split3_f32(c):
    hi = c.astype(BF16).astype(F32)
    r = c - hi
    mid = r.astype(BF16).astype(F32)
    return hi, mid, r - mid


def _prep_kernel(zq_ref, zc_ref, zs_ref, zw_ref, fq_ref, fk_ref, sm_ref, cos_ref, sin_ref,
                 tri_ref, ones_ref, pq_ref, pk_ref, gq_ref, gks_ref, gkw_ref, gfq_ref, gfk_ref, fb_ref,
                 ss_ref, sw_ref, sf_ref,
                 qn_ref, kcr_ref, vcr_ref, ks_ref, vs_ref, kw_ref, vw_ref, fqd_ref, fqo_ref, fka_ref,
                 base_ref, run_scr):
    ti = pl.program_id(1)
    tt = zq_ref.shape[1]
    lane = lax.broadcasted_iota(jnp.int32, (tt, LANES), 1)
    lo = lane < HEAD_DIM
    first_half = (lane & (HEAD_DIM // 2)) == 0
    cos4 = cos_ref[...]
    sin_signed = sin_ref[...]
    head_ones = ones_ref[...]
    one_at_aug = jnp.where(lane == AUG, 1.0, 0.0)

    for p in range(NSA_HEADS // 2):
        y = _pair_rope(_pair_rms(zq_ref[0, :, p * LANES:(p + 1) * LANES].astype(F32), gq_ref[...], head_ones),
                       cos4, sin_signed, first_half) * QK_SCALE
        qn_ref[0, 2 * p] = _head_a(y, lo, one_at_aug).astype(BF16)
        qn_ref[0, 2 * p + 1] = _head_b(y, lo, one_at_aug).astype(BF16)

    kcr_ref[0] = _pair_rope(zc_ref[0, :, :LANES].astype(F32), cos4, sin_signed, first_half).astype(BF16)
    vcr_ref[0] = zc_ref[0, :, LANES:].astype(BF16)

    sel_lane = jnp.right_shift(ti * tt + lax.broadcasted_iota(jnp.int32, (tt, LANES), 0), SEL_SHIFT)
    onehot = jnp.where(lane == sel_lane, -SEL_MASK, 0.0).astype(BF16)
    for z_ref, g_ref, k_ref, v_ref, bound_ref, with_onehot in (
            (zs_ref, gks_ref, ks_ref, vs_ref, ss_ref, True),
            (zw_ref, gkw_ref, kw_ref, vw_ref, sw_ref, False)):
        y = _pair_rope(_pair_rms(z_ref[0, :, :LANES].astype(F32), g_ref[...], head_ones),
                       cos4, sin_signed, first_half)
        v = z_ref[0, :, LANES:].astype(F32)
        vr = pltpu.roll(v, HEAD_DIM, 1)
        neg_bound = jnp.where(lane == AUG, -bound_ref[...], 0.0)
        k_ref[0, 0, :, :LANES] = _head_a(y, lo, neg_bound).astype(BF16)
        k_ref[0, 1, :, :LANES] = _head_b(y, lo, neg_bound).astype(BF16)
        if with_onehot:
            k_ref[0, 0, :, LANES:] = onehot
            k_ref[0, 1, :, LANES:] = onehot
        v_ref[0, 0] = jnp.where(lo, v, vr).astype(BF16)
        v_ref[0, 1] = jnp.where(lo, vr, v).astype(BF16)

    @pl.when(ti == 0)
    def _reset():
        run_scr[...] = jnp.zeros(run_scr.shape, F32)

    xf = sm_ref[0].astype(F32) + fb_ref[...]
    log_f = (jnp.minimum(xf, 0.0) - jnp.log1p(jnp.exp(-jnp.abs(xf)))) * LOG2E
    tri = tri_ref[...]
    parts = _split3_f32(log_f)
    cum = sum(jnp.dot(tri, part.astype(BF16), preferred_element_type=F32) for part in parts)
    base_ref[0, 0] = run_scr[...]
    run_scr[...] = run_scr[...] + cum[tt - 1:tt, :]

    parts6 = _split3_f32(cum) + _split3_f32(cum - cum[tt - 1:tt, :])
    x6 = jnp.concatenate([part.astype(BF16) for part in parts6], axis=1)
    aug_q = jnp.dot(x6, pq_ref[...], preferred_element_type=F32)
    aug_k = jnp.dot(x6, pk_ref[...], preferred_element_type=F32)

    lane1 = lax.broadcasted_iota(jnp.int32, (1, LANES), 1)
    span = lambda a, b: jnp.where((lane1 >= a) & (lane1 < b), 1.0, 0.0)
    q_tail_diag = span(AUG + 3, AUG + 6) + span(AUG + 9, AUG + 10)
    q_tail_off = span(AUG + 6, AUG + 10)
    k_tail = span(AUG, AUG + 3) + jnp.where(lane1 == AUG + 9, -sf_ref[...], 0.0)
    for p in range(FOX_HEADS // 2):
        sl = slice(p * LANES, (p + 1) * LANES)
        yq = _pair_rms(fq_ref[0, :, sl].astype(F32), gfq_ref[...], head_ones) * QK_SCALE
        yk = _pair_rms(fk_ref[0, :, sl].astype(F32), gfk_ref[...], head_ones)
        for e, split in ((0, _head_a), (1, _head_b)):
            h = 2 * p + e
            hs = slice(h * LANES, (h + 1) * LANES)
            qh = split(yq, lo) + aug_q[:, hs]
            fqd_ref[0, h] = (qh + q_tail_diag).astype(BF16)
            fqo_ref[0, h] = (qh + q_tail_off).astype(BF16)
            fka_ref[0, h] = (split(yk, lo) + aug_k[:, hs] + k_tail).astype(BF16)


def _prep(z, cos4, sin_signed, tri, head_ones, perm_q, perm_k, gq, gks, gkw, gfq, gfk, fb,
          s_sel, s_win, s_fox, *, batch, seq):
    nblk = seq // TQ
    whole = lambda a: pl.BlockSpec(a.shape, lambda b, i: (0, 0))
    zspec = lambda w, c: pl.BlockSpec((1, TQ, w), lambda b, i, c=c: (b, i, c))
    row = pl.BlockSpec((1, LANES), lambda b, i: (0, 0))
    tab = pl.BlockSpec((TQ, LANES), lambda b, i: (i, 0))
    heads = lambda n, w: pl.BlockSpec((1, n, TQ, w), lambda b, i: (b, 0, i, 0))
    tok = lambda w: pl.BlockSpec((1, TQ, w), lambda b, i: (b, i, 0))
    bf = lambda *shape: jax.ShapeDtypeStruct(shape, BF16)
    return pl.pallas_call(
        _prep_kernel,
        grid=(batch, nblk),
        in_specs=[zspec(NSA_WIDTH, C_NSA_Q // NSA_WIDTH),
                  zspec(2 * LANES, C_NSA_KV // (2 * LANES)),
                  zspec(2 * LANES, C_NSA_KV // (2 * LANES) + 1),
                  zspec(2 * LANES, C_NSA_KV // (2 * LANES) + 2),
                  zspec(FOX_WIDTH, C_FOX_Q // FOX_WIDTH),
                  zspec(FOX_WIDTH, C_FOX_K // FOX_WIDTH),
                  zspec(LANES, C_SMALL // LANES),
                  tab, tab, whole(tri), whole(head_ones), whole(perm_q), whole(perm_k),
                  row, row, row, row, row, row, row, row, row],
        out_specs=[heads(NSA_HEADS, LANES), tok(LANES), tok(LANES),
                   heads(NSA_KV_GROUPS, 2 * LANES), heads(NSA_KV_GROUPS, LANES),
                   heads(NSA_KV_GROUPS, LANES), heads(NSA_KV_GROUPS, LANES),
                   heads(FOX_HEADS, LANES), heads(FOX_HEADS, LANES), heads(FOX_HEADS, LANES),
                   pl.BlockSpec((1, 1, 1, LANES), lambda b, i: (b, i, 0, 0))],
        out_shape=[bf(batch, NSA_HEADS, seq, LANES), bf(batch, seq, LANES), bf(batch, seq, LANES),
                   bf(batch, NSA_KV_GROUPS, seq, 2 * LANES), bf(batch, NSA_KV_GROUPS, seq, LANES),
                   bf(batch, NSA_KV_GROUPS, seq, LANES), bf(batch, NSA_KV_GROUPS, seq, LANES),
                   bf(batch, FOX_HEADS, seq, LANES), bf(batch, FOX_HEADS, seq, LANES),
                   bf(batch, FOX_HEADS, seq, LANES),
                   jax.ShapeDtypeStruct((batch, nblk, 1, LANES), F32)],
        scratch_shapes=[pltpu.VMEM((1, LANES), F32)],
        compiler_params=_cparams(("parallel", "arbitrary")),
        name="prep",
    )(z, z, z, z, z, z, z, cos4, sin_signed, tri, head_ones, perm_q, perm_k,
      gq, gks, gkw, gfq, gfk, fb, s_sel, s_win, s_fox)


def _compress_kernel(xk_ref, xv_ref, w1k_ref, w1v_ref, bk_ref, bv_ref, w2k_ref, w2v_ref, kg_ref,
                     sc_ref, kc_ref, vc_ref):
    nchunk = xk_ref.shape[1]
    lane = lax.broadcasted_iota(jnp.int32, (nchunk, LANES), 1)
    for x_ref, w1_ref, b_ref, w2_ref, o_ref, is_key in ((xk_ref, w1k_ref, bk_ref, w2k_ref, kc_ref, True),
                                                        (xv_ref, w1v_ref, bv_ref, w2v_ref, vc_ref, False)):
        h = jnp.dot(x_ref[0], w1_ref[...], preferred_element_type=F32)
        for g in range(NSA_KV_GROUPS):
            c0 = g * 2 * NSA_CMP_HIDDEN
            top = h[:, c0:c0 + NSA_CMP_HIDDEN]
            bot = h[:, c0 + NSA_CMP_HIDDEN:c0 + 2 * NSA_CMP_HIDDEN]
            hid = top + pltpu.roll(bot, nchunk - 1, 0) + b_ref[...]
            act = (hid * jax.nn.sigmoid(hid)).astype(BF16)
            o = jnp.dot(act, w2_ref[...], preferred_element_type=F32)
            if is_key:
                ms = jnp.sum(o * o, axis=1, keepdims=True) * (1.0 / HEAD_DIM)
                o = o * lax.rsqrt(ms + EPS) * kg_ref[...]
                o = jnp.where(lane == AUG, -sc_ref[...], o)
            o_ref[0, g] = o.astype(BF16)


def _compress(xk, xv, w1k, w1v, bk, bv, w2k, w2v, kg, s_cmp, *, batch):
    nchunk, kin = xk.shape[1], xk.shape[2]
    xspec = pl.BlockSpec((1, nchunk, kin), lambda b: (b, 0, 0))
    full = lambda a: pl.BlockSpec(a.shape, lambda b: (0,) * a.ndim)
    ospec = pl.BlockSpec((1, NSA_KV_GROUPS, nchunk, LANES), lambda b: (b, 0, 0, 0))
    oshape = jax.ShapeDtypeStruct((batch, NSA_KV_GROUPS, nchunk, LANES), BF16)
    return pl.pallas_call(
        _compress_kernel,
        grid=(batch,),
        in_specs=[xspec, xspec, full(w1k), full(w1v), full(bk), full(bv), full(w2k), full(w2v), full(kg),
                  full(s_cmp)],
        out_specs=[ospec, ospec],
        out_shape=[oshape, oshape],
        compiler_params=_cparams(("parallel",)),
        name="compress",
    )(xk, xv, w1k, w1v, bk, bv, w2k, w2v, kg, s_cmp)


def _gate_column(gate_blk, lane, idx):
    col = jnp.sum(jnp.where(lane == idx, gate_blk, 0.0), axis=1, keepdims=True)
    return jax.nn.sigmoid(col)


def _flash_kernel(qi_tab, ki_tab, fl_tab, *refs, nheads, shared_kv, window, has_delta,
                  has_u, gate_branch, nblk, fixed_shift, nb):
    refs = list(refs)
    base_ref = refs.pop(0) if has_delta else None
    q_ref = refs.pop(0)
    qoff_ref = refs.pop(0) if has_delta else None
    u_ref = refs.pop(0) if has_u else None
    k_ref = refs.pop(0)
    v_ref = refs.pop(0)
    gate_ref = refs.pop(0) if gate_branch is not None else None
    o_ref = refs.pop(0)
    m_scr = refs.pop(0)
    l_scr = refs.pop(0)
    acc_scr = refs.pop(0)
    qcat_scr = refs.pop(0) if has_u else None

    b = pl.program_id(0)
    g = pl.program_id(1)
    step = (b * pl.num_programs(1) + g) * pl.num_programs(2) + pl.program_id(2)
    qi = qi_tab[step]
    ki = ki_tab[step]
    fl = fl_tab[step]
    active = (fl & 4) == 4
    tq = q_ref.shape[2]
    tk = k_ref.shape[2]

    @pl.when((fl & 1) == 1)
    def _init():
        m_scr[...] = jnp.full(m_scr.shape, NEG, F32)
        l_scr[...] = jnp.zeros(l_scr.shape, F32)
        acc_scr[...] = jnp.zeros(acc_scr.shape, F32)
        if has_u:
            for bb in range(nb):
                for r in range(nheads):
                    qcat_scr[bb * nheads + r, :, :LANES] = q_ref[bb, r]
                    qcat_scr[bb * nheads + r, :, LANES:] = u_ref[bb, r // NSA_GROUP_HEADS]

    lane = lax.broadcasted_iota(jnp.int32, (tq, LANES), 1)
    lo = lane < HEAD_DIM

    def tile(diagonal):
        masked = diagonal or window is not None
        if masked:
            row = lax.broadcasted_iota(jnp.int32, (tq, tk), 0)
            col = lax.broadcasted_iota(jnp.int32, (tq, tk), 1)
            if diagonal:
                valid = row >= col
            else:
                valid = (tq + row - col) < window
        for bb in range(nb):
            slot0, pair0, brow = bb * nheads, bb * (nheads // 2), b * nb + bb
            for p in range(nheads // 2):
                pvs, alphas = [], []
                for e in range(2):
                    hh = 2 * p + e
                    if has_u:
                        q = qcat_scr[slot0 + hh]
                    elif has_delta and not diagonal:
                        q = qoff_ref[bb, hh]
                    else:
                        q = q_ref[bb, hh]
                    k = k_ref[bb, hh // NSA_GROUP_HEADS] if shared_kv else k_ref[bb, hh]
                    v = (v_ref[bb, hh // NSA_GROUP_HEADS] if shared_kv
                         else v_ref[bb, :, p * LANES:(p + 1) * LANES])
                    s = lax.dot_general(q, k, (((1,), (1,)), ((), ())), preferred_element_type=F32)
                    if masked:
                        s = jnp.where(valid, s, NEG)
                    delta = None
                    if has_delta and not diagonal:
                        hoff = F_LANE + g * nheads + hh
                        delta = (base_ref[(brow * nblk + qi) * LANES + hoff]
                                 - base_ref[(brow * nblk + ki + 1) * LANES + hoff])
                    if fixed_shift:
                        pm = jnp.exp2(s)
                        rowsum = pm[:, :LANES]
                        for j in range(1, tk // LANES):
                            rowsum = rowsum + pm[:, j * LANES:(j + 1) * LANES]
                        pv = jnp.dot(pm.astype(BF16), v, preferred_element_type=F32)
                        if delta is not None:
                            w = jnp.exp2(jnp.full((1, LANES), delta, F32))
                            rowsum = rowsum * w
                            pv = pv * w
                        l_scr[slot0 + hh] = l_scr[slot0 + hh] + rowsum
                        pvs.append(pv)
                    else:
                        m_prev = m_scr[slot0 + hh]
                        m_cur = jnp.max(s, axis=1, keepdims=True)
                        if delta is not None:
                            m_cur = m_cur + delta
                        m_new = jnp.maximum(m_prev, m_cur)
                        alpha = jnp.exp2(m_prev - m_new)
                        shift = m_new[:, :1]
                        if delta is not None:
                            shift = shift - delta
                        pm = jnp.exp2(s - shift)
                        l_scr[slot0 + hh] = alpha * l_scr[slot0 + hh] + jnp.sum(pm, axis=1, keepdims=True)
                        m_scr[slot0 + hh] = m_new
                        pvs.append(jnp.dot(pm.astype(BF16), v, preferred_element_type=F32))
                        alphas.append(alpha)
                if fixed_shift:
                    acc_scr[pair0 + p] = acc_scr[pair0 + p] + jnp.where(lo, pvs[0], pvs[1])
                else:
                    acc_scr[pair0 + p] = (acc_scr[pair0 + p] * jnp.where(lo, alphas[0], alphas[1])
                                          + jnp.where(lo, pvs[0], pvs[1]))

    pl.when((qi == ki) & active)(lambda: tile(True))
    pl.when((qi != ki) & active)(lambda: tile(False))

    @pl.when((fl & 2) == 2)
    def _finish():
        for bb in range(nb):
            slot0, pair0 = bb * nheads, bb * (nheads // 2)
            for p in range(nheads // 2):
                if fixed_shift:
                    la = jnp.sum(l_scr[slot0 + 2 * p], axis=1, keepdims=True)
                    lb = jnp.sum(l_scr[slot0 + 2 * p + 1], axis=1, keepdims=True)
                else:
                    la, lb = l_scr[slot0 + 2 * p], l_scr[slot0 + 2 * p + 1]
                inv = jnp.where(lo, 1.0 / la, 1.0 / lb)
                if gate_branch is not None:
                    h0 = g * nheads + 2 * p
                    src = lax.broadcasted_iota(jnp.int32, (LANES, LANES), 0)
                    dst_lo = lax.broadcasted_iota(jnp.int32, (LANES, LANES), 1) < HEAD_DIM
                    pick = jnp.where(src == jnp.where(dst_lo, 3 * h0 + gate_branch,
                                                      3 * (h0 + 1) + gate_branch), 1.0, 0.0)
                    logits = jnp.dot(gate_ref[bb].astype(BF16), pick.astype(BF16),
                                     preferred_element_type=F32)
                    inv = inv * jax.nn.sigmoid(logits)
                o_ref[bb, :, p * LANES:(p + 1) * LANES] = (acc_scr[pair0 + p] * inv).astype(o_ref.dtype)


def _tile_tables(nq, window_tiles, batch):
    qi, ki, fl = [], [], []
    for i in range(nq):
        lo = 0 if window_tiles is None else max(0, i - window_tiles)
        for j in range(lo, i + 1):
            qi.append(i)
            ki.append(j)
            fl.append((1 if j == lo else 0) | (2 if j == i else 0) | 4)
    rows = lambda a: jnp.tile(jnp.asarray(a, jnp.int32), batch)
    return rows(qi), rows(ki), rows(fl)


def _decayed_tile_tables(nq, batch, nb, ngroups, base):
    qi = np.repeat(np.arange(nq), np.arange(1, nq + 1))
    ki = np.concatenate([np.arange(i + 1) for i in range(nq)])
    nsteps = qi.shape[0]
    diag = jnp.asarray(qi == ki)
    gate = base.reshape(batch, nq, LANES)[:, :, F_LANE:F_LANE + FOX_HEADS]
    delta = gate[:, qi, :] - gate[:, np.minimum(ki + 1, nq - 1), :]
    top = jnp.max(delta.reshape(batch // nb, nb, nsteps, ngroups, FOX_HEADS // ngroups), axis=(1, 4))
    top = top.transpose(0, 2, 1).reshape(batch // nb * ngroups, nsteps)
    active = diag[None, :] | (top > ZERO_WEIGHT_LOG2)
    same_q = jnp.asarray(qi[:, None] == np.arange(nq)[None, :])
    first_ki = jnp.min(jnp.where(active[:, :, None] & same_q[None], ki[None, :, None], nq), axis=1)
    first = active & (jnp.asarray(ki)[None, :] == first_ki[:, qi])
    flags = first.astype(jnp.int32) + 2 * diag[None, :].astype(jnp.int32) + 4 * active.astype(jnp.int32)
    order = jnp.argsort(jnp.logical_not(active), axis=1, stable=True)
    n_active = jnp.sum(active, axis=1, keepdims=True)
    slot = jnp.arange(nsteps)[None, :]
    src = jnp.take_along_axis(order, jnp.minimum(slot, n_active - 1), axis=1)
    kept = slot < n_active
    take = lambda a: jnp.asarray(a, jnp.int32)[src].reshape(-1)
    fl = jnp.where(kept, jnp.take_along_axis(flags, src, axis=1), 0).reshape(-1)
    return take(qi), take(ki), fl.astype(jnp.int32)


def _flash(q, k, v, *, batch, seq, shared_kv, fixed_shift, nheads=NSA_HEADS, window=None, base=None,
           q_off=None, u=None, z=None, gate_branch=None, v_offset=0):
    nb = FLASH_ROWS if fixed_shift else SAFE_FLASH_ROWS
    nb = nb if batch % nb == 0 else 1
    nq = seq // TQ
    ngroups = NSA_HEADS // nheads
    nkv = NSA_KV_GROUPS
    da = k.shape[-1]
    has_delta = base is not None
    has_u = u is not None
    out_w = LANES * (nheads // 2)
    v_col = v_offset // out_w
    table_rows = batch // nb * ngroups
    if has_delta and fixed_shift:
        tabs = _decayed_tile_tables(nq, batch, nb, ngroups, base)
    else:
        tabs = _tile_tables(nq, None if window is None else window // TQ, table_rows)
    nsteps = int(tabs[0].shape[0]) // table_rows

    at = lambda tab, b, g, s: tab[(b * ngroups + g) * nsteps + s]
    by_q_heads = lambda b, g, s, qt, kt, ft: (b, g, at(qt, b, g, s), 0)
    by_k_heads = lambda b, g, s, qt, kt, ft: (b, g, at(kt, b, g, s), 0)
    by_q_tokens = lambda col: (lambda b, g, s, qt, kt, ft: (b, at(qt, b, g, s), g if col is None else col))
    by_k_tokens = lambda b, g, s, qt, kt, ft: (b, at(kt, b, g, s), g + v_col)

    in_specs, args = [], []
    if has_delta:
        in_specs.append(pl.BlockSpec(memory_space=pltpu.SMEM))
        args.append(base)
    in_specs.append(pl.BlockSpec((nb, nheads, TQ, LANES), by_q_heads))
    args.append(q)
    if has_delta:
        in_specs.append(pl.BlockSpec((nb, nheads, TQ, LANES), by_q_heads))
        args.append(q_off)
    if has_u:
        in_specs.append(pl.BlockSpec((nb, nkv, TQ, LANES), by_q_heads))
        args.append(u)
    if shared_kv:
        in_specs.append(pl.BlockSpec((nb, nkv, TQ, da), by_k_heads))
        in_specs.append(pl.BlockSpec((nb, nkv, TQ, LANES), by_k_heads))
    else:
        in_specs.append(pl.BlockSpec((nb, nheads, TQ, da), by_k_heads))
        in_specs.append(pl.BlockSpec((nb, TQ, out_w), by_k_tokens))
    args += [k, v]
    if gate_branch is not None:
        in_specs.append(pl.BlockSpec((nb, TQ, LANES), by_q_tokens(C_SMALL // LANES)))
        args.append(z)

    m_rows = 8 if fixed_shift else TQ
    scratch = [pltpu.VMEM((nb * nheads, m_rows, LANES), F32),
               pltpu.VMEM((nb * nheads, TQ, LANES), F32),
               pltpu.VMEM((nb * nheads // 2, TQ, LANES), F32)]
    if has_u:
        scratch.append(pltpu.VMEM((nb * nheads, TQ, 2 * LANES), BF16))

    kern = functools.partial(_flash_kernel, nheads=nheads, shared_kv=shared_kv, window=window,
                             has_delta=has_delta, has_u=has_u, gate_branch=gate_branch, nblk=nq,
                             fixed_shift=fixed_shift, nb=nb)
    return pl.pallas_call(
        kern,
        grid_spec=pltpu.PrefetchScalarGridSpec(
            num_scalar_prefetch=3,
            grid=(batch // nb, ngroups, nsteps),
            in_specs=in_specs,
            out_specs=pl.BlockSpec((nb, TQ, out_w), by_q_tokens(None)),
            scratch_shapes=scratch),
        out_shape=jax.ShapeDtypeStruct((batch, seq, 4 * LANES), BF16),
        compiler_params=_cparams(("parallel", "parallel", "arbitrary")),
        name="flash_" + ("fox" if has_delta else ("sel" if has_u else "win")),
    )(*tabs, *args)


def _cmp_kernel(q_ref, kc_ref, vc_ref, ov_ref, gate_ref, o_ref, u_ref):
    g = pl.program_id(1)
    qi = pl.program_id(2)
    tq = q_ref.shape[2]
    ncmp = kc_ref.shape[2]
    nslc = ov_ref.shape[1]

    t = qi * tq + lax.broadcasted_iota(jnp.int32, (tq, ncmp), 0)
    n = lax.broadcasted_iota(jnp.int32, (tq, ncmp), 1)
    valid = (n * NSA_CMP_STRIDE + (NSA_CMP_LEN - 1)) <= t
    kc = kc_ref[0, 0]
    vc = vc_ref[0, 0]
    ov = ov_ref[...]
    lane = lax.broadcasted_iota(jnp.int32, (tq, LANES), 1)
    lo = lane < HEAD_DIM
    gate_blk = gate_ref[0].astype(F32)

    imp = jnp.zeros((tq, nslc), F32)
    outs = []
    for r in range(NSA_GROUP_HEADS):
        s = lax.dot_general(q_ref[0, r], kc, (((1,), (1,)), ((), ())), preferred_element_type=F32)
        s = jnp.where(valid, s, NEG)
        m = jnp.max(s, axis=1, keepdims=True)
        e = jnp.where(valid, jnp.exp2(s - m), 0.0)
        d = jnp.sum(e, axis=1, keepdims=True)
        pb = (e / jnp.where(d > 0.0, d, 1.0)).astype(BF16)
        o = jnp.dot(pb, vc, preferred_element_type=F32)
        gcol = _gate_column(gate_blk, lane, 3 * (g * NSA_GROUP_HEADS + r))
        outs.append(o * gcol)
        imp = imp + jnp.dot(pb, ov, preferred_element_type=F32)
    for p in range(NSA_GROUP_HEADS // 2):
        o_ref[0, :, p * LANES:(p + 1) * LANES] = jnp.where(lo, outs[2 * p], outs[2 * p + 1]).astype(o_ref.dtype)
    _write_unselected(u_ref, imp, qi, tq, nslc)


def _cmp_fixed_kernel(q_ref, kc_ref, vc_ref, ov_ref, gate_ref, o_ref, u_ref, d_scr, acc_scr, imp_scr,
                      *, chunk):
    g = pl.program_id(1)
    qi = pl.program_id(2)
    tq = q_ref.shape[2]
    ncmp = kc_ref.shape[2]
    nslc = ov_ref.shape[1]
    t0 = qi * tq
    d_scr[...] = jnp.zeros(d_scr.shape, F32)
    acc_scr[...] = jnp.zeros(acc_scr.shape, F32)
    imp_scr[...] = jnp.zeros(imp_scr.shape, F32)

    for c in range(ncmp // chunk):
        first_end = c * chunk * NSA_CMP_STRIDE + NSA_CMP_LEN - 1
        last_end = ((c + 1) * chunk - 1) * NSA_CMP_STRIDE + NSA_CMP_LEN - 1

        def body(masked, c=c):
            rows = slice(c * chunk, (c + 1) * chunk)
            kc = kc_ref[0, 0, rows, :]
            vo = jnp.concatenate([vc_ref[0, 0, rows, :], ov_ref[rows, :]], axis=1)
            if masked:
                t = t0 + lax.broadcasted_iota(jnp.int32, (tq, chunk), 0)
                n = c * chunk + lax.broadcasted_iota(jnp.int32, (tq, chunk), 1)
                valid = (n * NSA_CMP_STRIDE + (NSA_CMP_LEN - 1)) <= t
            for r in range(NSA_GROUP_HEADS):
                s = lax.dot_general(q_ref[0, r], kc, (((1,), (1,)), ((), ())), preferred_element_type=F32)
                if masked:
                    s = jnp.where(valid, s, NEG)
                e = jnp.exp2(s)
                part = e[:, :LANES]
                for j in range(1, chunk // LANES):
                    part = part + e[:, j * LANES:(j + 1) * LANES]
                d_scr[r] = d_scr[r] + part
                both = jnp.dot(e.astype(BF16), vo, preferred_element_type=F32)
                acc_scr[r] = acc_scr[r] + both[:, :LANES]
                imp_scr[r] = imp_scr[r] + both[:, LANES:]

        pl.when((first_end <= t0 + (tq - 1)) & (last_end > t0))(functools.partial(body, True))
        pl.when(last_end <= t0)(functools.partial(body, False))

    lane = lax.broadcasted_iota(jnp.int32, (tq, LANES), 1)
    lo = lane < HEAD_DIM
    src = lax.broadcasted_iota(jnp.int32, (LANES, LANES), 0)
    dst_lo = lax.broadcasted_iota(jnp.int32, (LANES, LANES), 1) < HEAD_DIM
    imp = jnp.zeros((tq, nslc), F32)
    for p in range(NSA_GROUP_HEADS // 2):
        normed = []
        for r in (2 * p, 2 * p + 1):
            d = jnp.sum(d_scr[r], axis=1, keepdims=True)
            inv = 1.0 / jnp.where(d > 0.0, d, 1.0)
            normed.append(acc_scr[r] * inv)
            imp = imp + imp_scr[r] * inv
        h0 = g * NSA_GROUP_HEADS + 2 * p
        pick = jnp.where(src == jnp.where(dst_lo, 3 * h0, 3 * (h0 + 1)), 1.0, 0.0).astype(BF16)
        logits = jnp.dot(gate_ref[0].astype(BF16), pick, preferred_element_type=F32)
        o_ref[0, :, p * LANES:(p + 1) * LANES] = (jnp.where(lo, normed[0], normed[1])
                                                  * jax.nn.sigmoid(logits)).astype(o_ref.dtype)
    _write_unselected(u_ref, imp, qi, tq, nslc)


N_FORCED = 3


def _write_unselected(u_ref, imp, qi, tq, nslc):
    tpos = qi * tq + lax.broadcasted_iota(jnp.int32, (tq, nslc), 0)
    jblk = lax.broadcasted_iota(jnp.int32, (tq, nslc), 1)
    cur = jnp.right_shift(tpos, SEL_SHIFT)
    forced = (jblk == 0) | (jblk == cur) | (jblk == cur - 1)
    visible = jblk <= cur
    score_t = jnp.where(visible & jnp.logical_not(forced), imp, -jnp.inf).T

    def run(nrows):
        sc = score_t[:nrows]
        jt = lax.broadcasted_iota(jnp.int32, (nrows, tq), 0).astype(F32)
        for _ in range(NSA_SEL_TOPK - N_FORCED):
            mx = jnp.max(sc, axis=0, keepdims=True)
            first = jnp.min(jnp.where(sc == mx, jt, float(nslc)), axis=0, keepdims=True)
            sc = jnp.where(jt == first, -jnp.inf, sc)
        taken = jnp.where(sc == -jnp.inf, 1.0, 0.0)
        if nrows < nslc:
            taken = jnp.concatenate([taken, jnp.zeros((nslc - nrows, tq), F32)], axis=0)
        chosen = visible & (forced | (taken.T > 0.5))
        u_ref[0, 0] = jnp.where(chosen, 0.0, 1.0).astype(BF16)

    half = nslc // 2
    last_block = jnp.right_shift(qi * tq + (tq - 1), SEL_SHIFT)
    pl.when(last_block < half)(functools.partial(run, half))
    pl.when(last_block >= half)(functools.partial(run, nslc))


def _cmp_select(qn, kc, vc, ov, z, *, batch, seq, fixed_shift):
    ncmp = kc.shape[2]
    nslc = ov.shape[1]
    if fixed_shift:
        chunk = 2 * LANES if ncmp % (2 * LANES) == 0 else ncmp
        kern = functools.partial(_cmp_fixed_kernel, chunk=chunk)
        scratch = [pltpu.VMEM((NSA_GROUP_HEADS, CMP_TQ, LANES), F32) for _ in range(3)]
    else:
        kern, scratch = _cmp_kernel, []
    return pl.pallas_call(
        kern,
        scratch_shapes=scratch,
        grid=(batch, NSA_KV_GROUPS, seq // CMP_TQ),
        in_specs=[pl.BlockSpec((1, NSA_GROUP_HEADS, CMP_TQ, LANES), lambda b, g, i: (b, g, i, 0)),
                  pl.BlockSpec((1, 1, ncmp, LANES), lambda b, g, i: (b, g, 0, 0)),
                  pl.BlockSpec((1, 1, ncmp, LANES), lambda b, g, i: (b, g, 0, 0)),
                  pl.BlockSpec((ncmp, nslc), lambda b, g, i: (0, 0)),
                  pl.BlockSpec((1, CMP_TQ, LANES), lambda b, g, i: (b, i, C_SMALL // LANES))],
        out_specs=[pl.BlockSpec((1, CMP_TQ, 2 * LANES), lambda b, g, i: (b, i, g)),
                   pl.BlockSpec((1, 1, CMP_TQ, nslc), lambda b, g, i: (b, g, i, 0))],
        out_shape=[jax.ShapeDtypeStruct((batch, seq, NSA_WIDTH), BF16),
                   jax.ShapeDtypeStruct((batch, NSA_KV_GROUPS, seq, nslc), BF16)],
        compiler_params=_cparams(("parallel", "parallel", "parallel")),
        name="cmp_select",
    )(qn, kc, vc, ov, z)


def _head_rms(x, g):
    return x * lax.rsqrt(jnp.mean(x * x, axis=-1, keepdims=True) + EPS) * g


def _mem_kernel(zq_ref, kv_ref, qg_ref, kg_ref, o_ref, kn_scr, vb_scr):
    @pl.when(pl.program_id(1) == 0)
    def _prep():
        for h in range(MEM_HEADS):
            kh = kv_ref[0, :, h * MEM_HEAD_DIM:(h + 1) * MEM_HEAD_DIM]
            kn_scr[h] = _head_rms(kh, kg_ref[...]).astype(BF16)
            vb_scr[h] = kv_ref[0, :, MEM_WIDTH + h * MEM_HEAD_DIM:
                               MEM_WIDTH + (h + 1) * MEM_HEAD_DIM].astype(BF16)

    for h in range(MEM_HEADS):
        sl = slice(h * MEM_HEAD_DIM, (h + 1) * MEM_HEAD_DIM)
        qh = (_head_rms(zq_ref[0, :, sl].astype(F32), qg_ref[...])
              * ((MEM_HEAD_DIM ** -0.5) * LOG2E)).astype(BF16)
        s = lax.dot_general(qh, kn_scr[h], (((1,), (1,)), ((), ())), preferred_element_type=F32)
        m = jnp.max(s, axis=1, keepdims=True)
        e = jnp.exp2(s - m)
        p = (e / jnp.sum(e, axis=1, keepdims=True)).astype(BF16)
        o_ref[0, :, sl] = jnp.dot(p, vb_scr[h], preferred_element_type=F32).astype(o_ref.dtype)


def _mem_attention(z, mem_kv, q_g, k_g, *, batch, seq):
    mlen = mem_kv.shape[1]
    tq = TQ
    return pl.pallas_call(
        _mem_kernel,
        grid=(batch, seq // tq),
        in_specs=[pl.BlockSpec((1, tq, MEM_WIDTH), lambda b, i: (b, i, C_MEM_Q // MEM_WIDTH)),
                  pl.BlockSpec((1, mlen, 2 * MEM_WIDTH), lambda b, i: (b, 0, 0)),
                  pl.BlockSpec((1, MEM_HEAD_DIM), lambda b, i: (0, 0)),
                  pl.BlockSpec((1, MEM_HEAD_DIM), lambda b, i: (0, 0))],
        out_specs=pl.BlockSpec((1, tq, MEM_WIDTH), lambda b, i: (b, i, 0)),
        out_shape=jax.ShapeDtypeStruct((batch, seq, MEM_WIDTH), BF16),
        scratch_shapes=[pltpu.VMEM((MEM_HEADS, mlen, MEM_HEAD_DIM), BF16),
                        pltpu.VMEM((MEM_HEADS, mlen, MEM_HEAD_DIM), BF16)],
        compiler_params=_cparams(("parallel", "arbitrary")),
        name="mem_attention",
    )(z, mem_kv, q_g.reshape(1, MEM_HEAD_DIM), k_g.reshape(1, MEM_HEAD_DIM))


def _sigmoid(x):
    return 0.5 * jnp.tanh(0.5 * x) + 0.5


def _out_kernel(x_ref, oc_ref, os_ref, ow_ref, ob_ref, om_ref, sa_ref, sb_ref, sm_ref,
                g0_ref, g1_ref, g2_ref, wa_ref, wb_ref, wm_ref, wo_ref, y_ref):
    silu = lambda ref: (lambda x: x * _sigmoid(x))(ref[...].astype(F32))
    oa = (oc_ref[...].astype(F32) + os_ref[...].astype(F32) + ow_ref[...].astype(F32)) * silu(sa_ref)
    ob = ob_ref[...].astype(F32) * silu(sb_ref)
    om = om_ref[...].astype(F32) * silu(sm_ref)
    gate = lambda ref: _sigmoid(ref[...].astype(F32))
    u = (gate(g0_ref) * jnp.dot(oa.astype(BF16), wa_ref[...], preferred_element_type=F32)
         + gate(g1_ref) * jnp.dot(ob.astype(BF16), wb_ref[...], preferred_element_type=F32)
         + gate(g2_ref) * jnp.dot(om.astype(BF16), wm_ref[...], preferred_element_type=F32))
    y_ref[...] = x_ref[...] + jnp.dot(u.astype(BF16), wo_ref[...], preferred_element_type=F32)


def _out_proj(x2d, o_c, o_s, o_w, o_b, o_m, z2d, wa, wb, wm, wo, layer):
    m = x2d.shape[0]
    tm = 512
    w512 = 512
    row512 = lambda c: pl.BlockSpec((tm, w512), lambda i, c=c: (i, c))
    row1024 = lambda c: pl.BlockSpec((tm, D_MODEL), lambda i, c=c: (i, c))
    full = lambda shape: pl.BlockSpec((None,) + shape, lambda i: (layer, 0, 0))
    return pl.pallas_call(
        _out_kernel,
        grid=(m // tm,),
        in_specs=[row1024(0), row512(0), row512(0), row512(0), row512(0), row512(0),
                  row512(C_NSA_SILU // w512), row512(C_FOX_SILU // w512), row512(C_MEM_SILU // w512),
                  row1024(C_MERGE // D_MODEL), row1024(C_MERGE // D_MODEL + 1),
                  row1024(C_MERGE // D_MODEL + 2),
                  full((w512, D_MODEL)), full((w512, D_MODEL)), full((w512, D_MODEL)),
                  full((D_MODEL, D_MODEL))],
        out_specs=row1024(0),
        out_shape=jax.ShapeDtypeStruct((m, D_MODEL), F32),
        compiler_params=_cparams(("parallel",)),
        name="out_proj",
    )(x2d, o_c, o_s, o_w, o_b, o_m, z2d, z2d, z2d, z2d, z2d, z2d, wa, wb, wm, wo)


def _fox_head_order(a, order, sections):
    nl, d, _ = a.shape
    a = a.reshape(nl, d, sections, FOX_HEADS, HEAD_DIM)
    return jnp.take_along_axis(a, order[:, None, None, :, None], axis=3).reshape(nl, d, -1)


def _permute_w_in(w_in, fox_order):
    o = np.cumsum([0, NSA_WIDTH, 6 * NSA_KV_GROUPS * HEAD_DIM, 3 * NSA_HEADS, NSA_WIDTH,
                   3 * FOX_WIDTH, FOX_HEADS, FOX_WIDTH, MEM_WIDTH, MEM_WIDTH, 3 * D_MODEL])
    nsa_q, nsa_kv, nsa_gate, nsa_silu, fox_qkv, fox_f, fox_silu, mem_q, mem_silu, merge = [
        w_in[..., o[i]:o[i + 1]] for i in range(10)]
    fox_qkv = _fox_head_order(fox_qkv, fox_order, 3)
    fox_silu = _fox_head_order(fox_silu, fox_order, 1)
    fox_f = jnp.take_along_axis(fox_f, fox_order[:, None, :], axis=2)
    pad = jnp.zeros(w_in.shape[:2] + (LANES - 3 * NSA_HEADS - FOX_HEADS,), w_in.dtype)
    return jnp.concatenate([nsa_q, nsa_silu, fox_qkv, fox_silu, mem_q, mem_silu, merge, nsa_kv,
                            nsa_gate, fox_f, pad], axis=-1).astype(BF16)


def _pad_lanes(x, width=LANES):
    return jnp.pad(x, [(0, 0)] * (x.ndim - 1) + [(0, width - x.shape[-1])])


def _tile2(g):
    return jnp.concatenate([g, g], axis=-1)[:, None, :]


def _overlap(n_cmp_pad, n_slc):
    cs = np.arange(n_cmp_pad)[:, None] * NSA_CMP_STRIDE
    ss = np.arange(NSLC_PAD)[None, :] * NSA_SEL_LEN
    ov = np.clip(np.minimum(cs + NSA_CMP_LEN, ss + NSA_SEL_LEN) - np.maximum(cs, ss), 0, None)
    ov = ov / NSA_CMP_LEN
    ov[-1] = 0.0
    ov[:, n_slc:] = 0.0
    return jnp.asarray(ov, BF16)


def _aug_permutations():
    pq = np.zeros((6 * LANES, FOX_HEADS * LANES), np.float32)
    pk = np.zeros((6 * LANES, FOX_HEADS * LANES), np.float32)
    for h in range(FOX_HEADS):
        for i in range(3):
            pq[i * LANES + F_LANE + h, h * LANES + AUG + i] = 1.0
            pk[i * LANES + F_LANE + h, h * LANES + AUG + 3 + i] = -1.0
            pk[(3 + i) * LANES + F_LANE + h, h * LANES + AUG + 6 + i] = -1.0
    return jnp.asarray(pq, BF16), jnp.asarray(pk, BF16)


def _compress_w1(w1):
    nl = w1.shape[0]
    halves = w1.reshape(nl, 2, NSA_CMP_STRIDE, 1, HEAD_DIM, NSA_CMP_HIDDEN)
    eye = jnp.eye(NSA_KV_GROUPS, dtype=w1.dtype)
    out = jnp.einsum("pg,zhldn->zlpdghn", eye, halves[:, :, :, 0])
    return out.reshape(nl, NSA_CMP_STRIDE * NSA_KV_GROUPS * HEAD_DIM,
                       NSA_KV_GROUPS * 2 * NSA_CMP_HIDDEN).astype(BF16)


def _layer(x, mem, p, big, layer, consts):
    batch, seq, _ = x.shape
    m_rows = batch * seq
    cos4, sin_signed, tri, head_ones, perm_q, perm_k, ov = consts
    z2d = _norm_matmul(x.reshape(m_rows, D_MODEL), p["norm_g"], big["w_in"], layer, 512, Z_WIDTH // 3, BF16)
    z = z2d.reshape(batch, seq, Z_WIDTH)

    (qn, kcr, vcr, ks, vs, kw, vw, fqd, fqo, fka, base) = _prep(
        z, cos4, sin_signed, tri, head_ones, perm_q, perm_k,
        p["gq"], p["gks"], p["gkw"], p["gfq"], p["gfk"], p["fb"],
        p["s_sel"], p["s_win"], p["s_fox"], batch=batch, seq=seq)

    nchunk = seq // NSA_CMP_STRIDE
    kc, vc = _compress(kcr.reshape(batch, nchunk, NSA_CMP_STRIDE * LANES),
                       vcr.reshape(batch, nchunk, NSA_CMP_STRIDE * LANES),
                       p["w1k"], p["w1v"], p["bk"], p["bv"], p["w2k"], p["w2v"], p["gkc"], p["s_cmp"],
                       batch=batch)

    def attend(fixed_shift):
        o_c, unsel = _cmp_select(qn, kc, vc, ov, z, batch=batch, seq=seq, fixed_shift=fixed_shift)
        o_s = _flash(qn, ks, vs, batch=batch, seq=seq, shared_kv=True, fixed_shift=fixed_shift,
                     u=unsel, z=z, gate_branch=1)
        o_w = _flash(qn, kw, vw, batch=batch, seq=seq, shared_kv=True, fixed_shift=fixed_shift,
                     window=NSA_WINDOW, z=z, gate_branch=2)
        o_b = _flash(fqd, fka, z, batch=batch, seq=seq, shared_kv=False, fixed_shift=fixed_shift,
                     nheads=FOX_HEADS // 2, base=base.reshape(-1), q_off=fqo, v_offset=C_FOX_V)
        return o_c, o_s, o_w, o_b

    o_c, o_s, o_w, o_b = lax.cond(p["bound_ok"], lambda: attend(True), lambda: attend(False))

    mlen = mem.shape[1]
    mem_kv = _norm_matmul(mem.reshape(batch * mlen, D_MODEL), p["mem_norm_g"], big["w_mem_kv"], layer,
                          min(512, batch * mlen), 2 * MEM_WIDTH, F32).reshape(batch, mlen, 2 * MEM_WIDTH)
    o_m = _mem_attention(z, mem_kv, p["mem_q_norm"], p["mem_k_norm"], batch=batch, seq=seq)

    y = _out_proj(x.reshape(m_rows, D_MODEL), o_c.reshape(m_rows, -1), o_s.reshape(m_rows, -1),
                  o_w.reshape(m_rows, -1), o_b.reshape(m_rows, -1), o_m.reshape(m_rows, -1), z2d,
                  big["w_branch_a"], big["w_branch_b"], big["w_branch_m"], big["w_out"], layer)
    return y.reshape(batch, seq, D_MODEL)


def kernel(x, mem, norm_g, mem_norm_g, w_in, nsa_q_norm, nsa_k_norm, cmp_pe_k, cmp_w1_k, cmp_w2_k,
           cmp_pe_v, cmp_w1_v, cmp_w2_v, fox_q_norm, fox_k_norm, fox_f_bias, mem_q_norm, mem_k_norm,
           w_mem_kv, w_branch_a, w_branch_b, w_branch_m, w_out):
    batch, seq, _ = x.shape
    depth = w_in.shape[0]
    n_slc = seq // NSA_SEL_LEN
    assert n_slc <= NSLC_PAD and seq % TQ == 0 and NSA_WINDOW == TQ

    half = HEAD_DIM // 2
    inv_freq = ROPE_THETA ** (-jnp.arange(half, dtype=F32) / half)
    ang = jnp.arange(seq).astype(F32)[:, None] * inv_freq[None, :]
    cos, sin = jnp.cos(ang), jnp.sin(ang)
    cos4 = jnp.concatenate([cos, cos, cos, cos], axis=-1)
    sin_signed = jnp.concatenate([-sin, sin, -sin, sin], axis=-1)
    tri = jnp.asarray(np.tril(np.ones((TQ, TQ), np.float32)), BF16)
    head_ones = jnp.asarray(np.kron(np.eye(2), np.ones((HEAD_DIM, HEAD_DIM))), BF16)
    consts = (cos4, sin_signed, tri, head_ones, *_aug_permutations(),
              _overlap(seq // NSA_CMP_STRIDE, n_slc))

    hp = lax.Precision.HIGHEST
    pe_bias = lambda pe, w1: jnp.einsum("lk,lkn->ln", pe.reshape(depth, -1), w1, precision=hp)[:, None, :]
    fox_order = jnp.argsort(fox_f_bias, axis=-1)
    fox_bias_sorted = jnp.take_along_axis(fox_f_bias, fox_order, axis=-1)
    w_branch_b_sorted = jnp.take_along_axis(
        w_branch_b.reshape(depth, FOX_HEADS, HEAD_DIM, D_MODEL), fox_order[:, :, None, None], axis=1
    ).reshape(depth, FOX_WIDTH, D_MODEL)
    fb = jnp.zeros((depth, 1, LANES), F32).at[:, 0, F_LANE:F_LANE + FOX_HEADS].set(fox_bias_sorted)

    def logit_bound(gq, gk):
        bound = (BOUND_MARGIN * HEAD_DIM * QK_SCALE
                 * jnp.max(jnp.abs(gq), axis=-1) * jnp.max(jnp.abs(gk), axis=-1))
        return bound.astype(BF16).astype(F32)

    bounds = [logit_bound(nsa_q_norm, nsa_k_norm[:, 1]), logit_bound(nsa_q_norm, nsa_k_norm[:, 2]),
              logit_bound(fox_q_norm, fox_k_norm), logit_bound(nsa_q_norm, nsa_k_norm[:, 0])]
    bound_ok = functools.reduce(jnp.maximum, bounds) <= MAX_FIXED_BOUND
    bound_row = lambda s: jnp.broadcast_to(s[:, None, None], (depth, 1, LANES))
    stacked = dict(
        s_sel=bound_row(bounds[0]), s_win=bound_row(bounds[1]), s_fox=bound_row(bounds[2]),
        s_cmp=bound_row(bounds[3]),
        bound_ok=bound_ok,
        norm_g=norm_g, mem_norm_g=mem_norm_g,
        gq=_tile2(nsa_q_norm), gks=_tile2(nsa_k_norm[:, 1]), gkw=_tile2(nsa_k_norm[:, 2]),
        gkc=_pad_lanes(nsa_k_norm[:, 0])[:, None, :], gfq=_tile2(fox_q_norm), gfk=_tile2(fox_k_norm),
        fb=fb, w1k=_compress_w1(cmp_w1_k), w1v=_compress_w1(cmp_w1_v),
        bk=pe_bias(cmp_pe_k, cmp_w1_k), bv=pe_bias(cmp_pe_v, cmp_w1_v),
        w2k=_pad_lanes(cmp_w2_k).astype(BF16),
        w2v=jnp.concatenate([cmp_w2_v, cmp_w2_v], axis=-1).astype(BF16),
        mem_q_norm=mem_q_norm, mem_k_norm=mem_k_norm)
    big = dict(w_in=_permute_w_in(w_in, fox_order), w_mem_kv=w_mem_kv.astype(BF16),
               w_branch_a=w_branch_a.astype(BF16), w_branch_b=w_branch_b_sorted.astype(BF16),
               w_branch_m=w_branch_m.astype(BF16), w_out=w_out.astype(BF16))
    for l in range(depth):
        x = _layer(x, mem, {k: v[l] for k, v in stacked.items()}, big, l, consts)
    return x
```

```python
import functools

import numpy as np
import jax
import jax.numpy as jnp
from jax import lax
from jax.experimental import pallas as pl
from jax.experimental.pallas import tpu as pltpu

F32 = jnp.float32
BF16 = jnp.bfloat16

D_MODEL = 1024
HEAD_DIM = 64
ROPE_THETA = 10000.0
EPS = 1e-6
NSA_HEADS = 8
NSA_KV_GROUPS = 2
NSA_GROUP_HEADS = NSA_HEADS // NSA_KV_GROUPS
NSA_CMP_LEN = 32
NSA_CMP_STRIDE = 16
NSA_CMP_HIDDEN = 128
NSA_SEL_LEN = 64
NSA_SEL_TOPK = 16
NSA_WINDOW = 512
FORCE_SCORE = 1e9
NSA_WIDTH = NSA_HEADS * HEAD_DIM
FOX_HEADS = 8
FOX_WIDTH = FOX_HEADS * HEAD_DIM
MEM_HEADS = 4
MEM_HEAD_DIM = 128
MEM_WIDTH = MEM_HEADS * MEM_HEAD_DIM

LANES = 128
VMEM_LIMIT = 56 * 1024 * 1024
NEG = -1e30
SEL_MASK = 32768.0
SEL_SHIFT = 6
NSLC_PAD = LANES
LOG2E = 1.4426950408889634
QK_SCALE = (HEAD_DIM ** -0.5) * LOG2E
BOUND_MARGIN = 1.02
MAX_FIXED_BOUND = 60.0
ZERO_WEIGHT_LOG2 = -160.0

C_NSA_Q = 0
C_NSA_SILU = 512
C_FOX_Q = 1024
C_FOX_K = 1536
C_FOX_V = 2048
C_FOX_SILU = 2560
C_MEM_Q = 3072
C_MEM_SILU = 3584
C_MERGE = 4096
C_NSA_KV = 7168
C_SMALL = 7936
Z_WIDTH = 8064
F_LANE = 3 * NSA_HEADS
AUG = HEAD_DIM

TQ = 512
FLASH_ROWS = 4
SAFE_FLASH_ROWS = 2
CMP_TQ = 2048


def _cparams(sem):
    return pltpu.CompilerParams(dimension_semantics=sem, vmem_limit_bytes=VMEM_LIMIT)


def _norm_matmul_kernel(x_ref, g_ref, w_ref, o_ref, *, tn):
    x = x_ref[...]
    ms = jnp.mean(x * x, axis=-1, keepdims=True)
    h = (x * lax.rsqrt(ms + EPS) * g_ref[...]).astype(BF16)
    for c0 in range(0, w_ref.shape[1], tn):
        o_ref[:, c0:c0 + tn] = jnp.dot(h, w_ref[:, c0:c0 + tn],
                                       preferred_element_type=F32).astype(o_ref.dtype)


def _norm_matmul(x2d, g, w, layer, tm, tn, out_dtype):
    m, k = x2d.shape
    n = w.shape[2]
    return pl.pallas_call(
        functools.partial(_norm_matmul_kernel, tn=tn),
        grid=(m // tm,),
        in_specs=[pl.BlockSpec((tm, k), lambda i: (i, 0)),
                  pl.BlockSpec((1, k), lambda i: (0, 0)),
                  pl.BlockSpec((None, k, n), lambda i: (layer, 0, 0), pipeline_mode=pl.Buffered(1))],
        out_specs=pl.BlockSpec((tm, n), lambda i: (i, 0)),
        out_shape=jax.ShapeDtypeStruct((m, n), out_dtype),
        compiler_params=_cparams(("parallel",)),
        name="norm_matmul",
    )(x2d, g.reshape(1, k), w)


def _pair_rms(x, g2, head_ones):
    sq = x * x
    hi = sq.astype(BF16)
    low = (sq - hi.astype(F32)).astype(BF16)
    ssq = (jnp.dot(hi, head_ones, preferred_element_type=F32)
           + jnp.dot(low, head_ones, preferred_element_type=F32))
    return x * lax.rsqrt(ssq * (1.0 / HEAD_DIM) + EPS) * g2


def _pair_rope(y, cos4, sin_signed, first_half):
    rot = jnp.where(first_half, pltpu.roll(y, LANES - HEAD_DIM // 2, 1), pltpu.roll(y, HEAD_DIM // 2, 1))
    return y * cos4 + rot * sin_signed


def _head_a(y, lo, tail=0.0):
    return jnp.where(lo, y, tail)


def _head_b(y, lo, tail=0.0):
    return jnp.where(lo, pltpu.roll(y, HEAD_DIM, 1), tail)


def _split3_f32(c):
    hi = c.astype(BF16).astype(F32)
    r = c - hi
    mid = r.astype(BF16).astype(F32)
    return hi, mid, r - mid


def _prep_kernel(zq_ref, zc_ref, zs_ref, zw_ref, fq_ref, fk_ref, sm_ref, cos_ref, sin_ref,
                 tri_ref, ones_ref, pq_ref, pk_ref, gq_ref, gks_ref, gkw_ref, gfq_ref, gfk_ref, fb_ref,
                 ss_ref, sw_ref, sf_ref,
                 qn_ref, kcr_ref, vcr_ref, ks_ref, vs_ref, kw_ref, vw_ref, fqd_ref, fqo_ref, fka_ref,
                 base_ref, run_scr):
    ti = pl.program_id(1)
    tt = zq_ref.shape[1]
    lane = lax.broadcasted_iota(jnp.int32, (tt, LANES), 1)
    lo = lane < HEAD_DIM
    first_half = (lane & (HEAD_DIM // 2)) == 0
    cos4 = cos_ref[...]
    sin_signed = sin_ref[...]
    head_ones = ones_ref[...]
    one_at_aug = jnp.where(lane == AUG, 1.0, 0.0)

    for p in range(NSA_HEADS // 2):
        y = _pair_rope(_pair_rms(zq_ref[0, :, p * LANES:(p + 1) * LANES].astype(F32), gq_ref[...], head_ones),
                       cos4, sin_signed, first_half) * QK_SCALE
        qn_ref[0, 2 * p] = _head_a(y, lo, one_at_aug).astype(BF16)
        qn_ref[0, 2 * p + 1] = _head_b(y, lo, one_at_aug).astype(BF16)

    kcr_ref[0] = _pair_rope(zc_ref[0, :, :LANES].astype(F32), cos4, sin_signed, first_half).astype(BF16)
    vcr_ref[0] = zc_ref[0, :, LANES:].astype(BF16)

    sel_lane = jnp.right_shift(ti * tt + lax.broadcasted_iota(jnp.int32, (tt, LANES), 0), SEL_SHIFT)
    onehot = jnp.where(lane == sel_lane, -SEL_MASK, 0.0).astype(BF16)
    for z_ref, g_ref, k_ref, v_ref, bound_ref, with_onehot in (
            (zs_ref, gks_ref, ks_ref, vs_ref, ss_ref, True),
            (zw_ref, gkw_ref, kw_ref, vw_ref, sw_ref, False)):
        y = _pair_rope(_pair_rms(z_ref[0, :, :LANES].astype(F32), g_ref[...], head_ones),
                       cos4, sin_signed, first_half)
        v = z_ref[0, :, LANES:].astype(F32)
        vr = pltpu.roll(v, HEAD_DIM, 1)
        neg_bound = jnp.where(lane == AUG, -bound_ref[...], 0.0)
        k_ref[0, 0, :, :LANES] = _head_a(y, lo, neg_bound).astype(BF16)
        k_ref[0, 1, :, :LANES] = _head_b(y, lo, neg_bound).astype(BF16)
        if with_onehot:
            k_ref[0, 0, :, LANES:] = onehot
            k_ref[0, 1, :, LANES:] = onehot
        v_ref[0, 0] = jnp.where(lo, v, vr).astype(BF16)
        v_ref[0, 1] = jnp.where(lo, vr, v).astype(BF16)

    @pl.when(ti == 0)
    def _reset():
        run_scr[...] = jnp.zeros(run_scr.shape, F32)

    xf = sm_ref[0].astype(F32) + fb_ref[...]
    log_f = (jnp.minimum(xf, 0.0) - jnp.log1p(jnp.exp(-jnp.abs(xf)))) * LOG2E
    tri = tri_ref[...]
    parts = _split3_f32(log_f)
    cum = sum(jnp.dot(tri, part.astype(BF16), preferred_element_type=F32) for part in parts)
    base_ref[0, 0] = run_scr[...]
    run_scr[...] = run_scr[...] + cum[tt - 1:tt, :]

    parts6 = _split3_f32(cum) + _split3_f32(cum - cum[tt - 1:tt, :])
    x6 = jnp.concatenate([part.astype(BF16) for part in parts6], axis=1)
    aug_q = jnp.dot(x6, pq_ref[...], preferred_element_type=F32)
    aug_k = jnp.dot(x6, pk_ref[...], preferred_element_type=F32)

    lane1 = lax.broadcasted_iota(jnp.int32, (1, LANES), 1)
    span = lambda a, b: jnp.where((lane1 >= a) & (lane1 < b), 1.0, 0.0)
    q_tail_diag = span(AUG + 3, AUG + 6) + span(AUG + 9, AUG + 10)
    q_tail_off = span(AUG + 6, AUG + 10)
    k_tail = span(AUG, AUG + 3) + jnp.where(lane1 == AUG + 9, -sf_ref[...], 0.0)
    for p in range(FOX_HEADS // 2):
        sl = slice(p * LANES, (p + 1) * LANES)
        yq = _pair_rms(fq_ref[0, :, sl].astype(F32), gfq_ref[...], head_ones) * QK_SCALE
        yk = _pair_rms(fk_ref[0, :, sl].astype(F32), gfk_ref[...], head_ones)
        for e, split in ((0, _head_a), (1, _head_b)):
            h = 2 * p + e
            hs = slice(h * LANES, (h + 1) * LANES)
            qh = split(yq, lo) + aug_q[:, hs]
            fqd_ref[0, h] = (qh + q_tail_diag).astype(BF16)
            fqo_ref[0, h] = (qh + q_tail_off).astype(BF16)
            fka_ref[0, h] = (split(yk, lo) + aug_k[:, hs] + k_tail).astype(BF16)


def _prep(z, cos4, sin_signed, tri, head_ones, perm_q, perm_k, gq, gks, gkw, gfq, gfk, fb,
          s_sel, s_win, s_fox, *, batch, seq):
    nblk = seq // TQ
    whole = lambda a: pl.BlockSpec(a.shape, lambda b, i: (0, 0))
    zspec = lambda w, c: pl.BlockSpec((1, TQ, w), lambda b, i, c=c: (b, i, c))
    row = pl.BlockSpec((1, LANES), lambda b, i: (0, 0))
    tab = pl.BlockSpec((TQ, LANES), lambda b, i: (i, 0))
    heads = lambda n, w: pl.BlockSpec((1, n, TQ, w), lambda b, i: (b, 0, i, 0))
    tok = lambda w: pl.BlockSpec((1, TQ, w), lambda b, i: (b, i, 0))
    bf = lambda *shape: jax.ShapeDtypeStruct(shape, BF16)
    return pl.pallas_call(
        _prep_kernel,
        grid=(batch, nblk),
        in_specs=[zspec(NSA_WIDTH, C_NSA_Q // NSA_WIDTH),
                  zspec(2 * LANES, C_NSA_KV // (2 * LANES)),
                  zspec(2 * LANES, C_NSA_KV // (2 * LANES) + 1),
                  zspec(2 * LANES, C_NSA_KV // (2 * LANES) + 2),
                  zspec(FOX_WIDTH, C_FOX_Q // FOX_WIDTH),
                  zspec(FOX_WIDTH, C_FOX_K // FOX_WIDTH),
                  zspec(LANES, C_SMALL // LANES),
                  tab, tab, whole(tri), whole(head_ones), whole(perm_q), whole(perm_k),
                  row, row, row, row, row, row, row, row, row],
        out_specs=[heads(NSA_HEADS, LANES), tok(LANES), tok(LANES),
                   heads(NSA_KV_GROUPS, 2 * LANES), heads(NSA_KV_GROUPS, LANES),
                   heads(NSA_KV_GROUPS, LANES), heads(NSA_KV_GROUPS, LANES),
                   heads(FOX_HEADS, LANES), heads(FOX_HEADS, LANES), heads(FOX_HEADS, LANES),
                   pl.BlockSpec((1, 1, 1, LANES), lambda b, i: (b, i, 0, 0))],
        out_shape=[bf(batch, NSA_HEADS, seq, LANES), bf(batch, seq, LANES), bf(batch, seq, LANES),
                   bf(batch, NSA_KV_GROUPS, seq, 2 * LANES), bf(batch, NSA_KV_GROUPS, seq, LANES),
                   bf(batch, NSA_KV_GROUPS, seq, LANES), bf(batch, NSA_KV_GROUPS, seq, LANES),
                   bf(batch, FOX_HEADS, seq, LANES), bf(batch, FOX_HEADS, seq, LANES),
                   bf(batch, FOX_HEADS, seq, LANES),
                   jax.ShapeDtypeStruct((batch, nblk, 1, LANES), F32)],
        scratch_shapes=[pltpu.VMEM((1, LANES), F32)],
        compiler_params=_cparams(("parallel", "arbitrary")),
        name="prep",
    )(z, z, z, z, z, z, z, cos4, sin_signed, tri, head_ones, perm_q, perm_k,
      gq, gks, gkw, gfq, gfk, fb, s_sel, s_win, s_fox)


def _compress_kernel(xk_ref, xv_ref, w1k_ref, w1v_ref, bk_ref, bv_ref, w2k_ref, w2v_ref, kg_ref,
                     sc_ref, kc_ref, vc_ref):
    nchunk = xk_ref.shape[1]
    lane = lax.broadcasted_iota(jnp.int32, (nchunk, LANES), 1)
    for x_ref, w1_ref, b_ref, w2_ref, o_ref, is_key in ((xk_ref, w1k_ref, bk_ref, w2k_ref, kc_ref, True),
                                                        (xv_ref, w1v_ref, bv_ref, w2v_ref, vc_ref, False)):
        h = jnp.dot(x_ref[0], w1_ref[...], preferred_element_type=F32)
        for g in range(NSA_KV_GROUPS):
            c0 = g * 2 * NSA_CMP_HIDDEN
            top = h[:, c0:c0 + NSA_CMP_HIDDEN]
            bot = h[:, c0 + NSA_CMP_HIDDEN:c0 + 2 * NSA_CMP_HIDDEN]
            hid = top + pltpu.roll(bot, nchunk - 1, 0) + b_ref[...]
            act = (hid * jax.nn.sigmoid(hid)).astype(BF16)
            o = jnp.dot(act, w2_ref[...], preferred_element_type=F32)
            if is_key:
                ms = jnp.sum(o * o, axis=1, keepdims=True) * (1.0 / HEAD_DIM)
                o = o * lax.rsqrt(ms + EPS) * kg_ref[...]
                o = jnp.where(lane == AUG, -sc_ref[...], o)
            o_ref[0, g] = o.astype(BF16)


def _compress(xk, xv, w1k, w1v, bk, bv, w2k, w2v, kg, s_cmp, *, batch):
    nchunk, kin = xk.shape[1], xk.shape[2]
    xspec = pl.BlockSpec((1, nchunk, kin), lambda b: (b, 0, 0))
    full = lambda a: pl.BlockSpec(a.shape, lambda b: (0,) * a.ndim)
    ospec = pl.BlockSpec((1, NSA_KV_GROUPS, nchunk, LANES), lambda b: (b, 0, 0, 0))
    oshape = jax.ShapeDtypeStruct((batch, NSA_KV_GROUPS, nchunk, LANES), BF16)
    return pl.pallas_call(
        _compress_kernel,
        grid=(batch,),
        in_specs=[xspec, xspec, full(w1k), full(w1v), full(bk), full(bv), full(w2k), full(w2v), full(kg),
                  full(s_cmp)],
        out_specs=[ospec, ospec],
        out_shape=[oshape, oshape],
        compiler_params=_cparams(("parallel",)),
        name="compress",
    )(xk, xv, w1k, w1v, bk, bv, w2k, w2v, kg, s_cmp)


def _gate_column(gate_blk, lane, idx):
    col = jnp.sum(jnp.where(lane == idx, gate_blk, 0.0), axis=1, keepdims=True)
    return jax.nn.sigmoid(col)


def _flash_kernel(qi_tab, ki_tab, fl_tab, *refs, nheads, shared_kv, window, has_delta,
                  has_u, gate_branch, nblk, fixed_shift, nb):
    refs = list(refs)
    base_ref = refs.pop(0) if has_delta else None
    q_ref = refs.pop(0)
    qoff_ref = refs.pop(0) if has_delta else None
    u_ref = refs.pop(0) if has_u else None
    k_ref = refs.pop(0)
    v_ref = refs.pop(0)
    gate_ref = refs.pop(0) if gate_branch is not None else None
    o_ref = refs.pop(0)
    m_scr = refs.pop(0)
    l_scr = refs.pop(0)
    acc_scr = refs.pop(0)
    qcat_scr = refs.pop(0) if has_u else None

    b = pl.program_id(0)
    g = pl.program_id(1)
    step = (b * pl.num_programs(1) + g) * pl.num_programs(2) + pl.program_id(2)
    qi = qi_tab[step]
    ki = ki_tab[step]
    fl = fl_tab[step]
    active = (fl & 4) == 4
    tq = q_ref.shape[2]
    tk = k_ref.shape[2]

    @pl.when((fl & 1) == 1)
    def _init():
        m_scr[...] = jnp.full(m_scr.shape, NEG, F32)
        l_scr[...] = jnp.zeros(l_scr.shape, F32)
        acc_scr[...] = jnp.zeros(acc_scr.shape, F32)
        if has_u:
            for bb in range(nb):
                for r in range(nheads):
                    qcat_scr[bb * nheads + r, :, :LANES] = q_ref[bb, r]
                    qcat_scr[bb * nheads + r, :, LANES:] = u_ref[bb, r // NSA_GROUP_HEADS]

    lane = lax.broadcasted_iota(jnp.int32, (tq, LANES), 1)
    lo = lane < HEAD_DIM

    def tile(diagonal):
        masked = diagonal or window is not None
        if masked:
            row = lax.broadcasted_iota(jnp.int32, (tq, tk), 0)
            col = lax.broadcasted_iota(jnp.int32, (tq, tk), 1)
            if diagonal:
                valid = row >= col
            else:
                valid = (tq + row - col) < window
        for bb in range(nb):
            slot0, pair0, brow = bb * nheads, bb * (nheads // 2), b * nb + bb
            for p in range(nheads // 2):
                pvs, alphas = [], []
                for e in range(2):
                    hh = 2 * p + e
                    if has_u:
                        q = qcat_scr[slot0 + hh]
                    elif has_delta and not diagonal:
                        q = qoff_ref[bb, hh]
                    else:
                        q = q_ref[bb, hh]
                    k = k_ref[bb, hh // NSA_GROUP_HEADS] if shared_kv else k_ref[bb, hh]
                    v = (v_ref[bb, hh // NSA_GROUP_HEADS] if shared_kv
                         else v_ref[bb, :, p * LANES:(p + 1) * LANES])
                    s = lax.dot_general(q, k, (((1,), (1,)), ((), ())), preferred_element_type=F32)
                    if masked:
                        s = jnp.where(valid, s, NEG)
                    delta = None
                    if has_delta and not diagonal:
                        hoff = F_LANE + g * nheads + hh
                        delta = (base_ref[(brow * nblk + qi) * LANES + hoff]
                                 - base_ref[(brow * nblk + ki + 1) * LANES + hoff])
                    if fixed_shift:
                        pm = jnp.exp2(s)
                        rowsum = pm[:, :LANES]
                        for j in range(1, tk // LANES):
                            rowsum = rowsum + pm[:, j * LANES:(j + 1) * LANES]
                        pv = jnp.dot(pm.astype(BF16), v, preferred_element_type=F32)
                        if delta is not None:
                            w = jnp.exp2(jnp.full((1, LANES), delta, F32))
                            rowsum = rowsum * w
                            pv = pv * w
                        l_scr[slot0 + hh] = l_scr[slot0 + hh] + rowsum
                        pvs.append(pv)
                    else:
                        m_prev = m_scr[slot0 + hh]
                        m_cur = jnp.max(s, axis=1, keepdims=True)
                        if delta is not None:
                            m_cur = m_cur + delta
                        m_new = jnp.maximum(m_prev, m_cur)
                        alpha = jnp.exp2(m_prev - m_new)
                        shift = m_new[:, :1]
                        if delta is not None:
                            shift = shift - delta
                        pm = jnp.exp2(s - shift)
                        l_scr[slot0 + hh] = alpha * l_scr[slot0 + hh] + jnp.sum(pm, axis=1, keepdims=True)
                        m_scr[slot0 + hh] = m_new
                        pvs.append(jnp.dot(pm.astype(BF16), v, preferred_element_type=F32))
                        alphas.append(alpha)
                if fixed_shift:
                    acc_scr[pair0 + p] = acc_scr[pair0 + p] + jnp.where(lo, pvs[0], pvs[1])
                else:
                    acc_scr[pair0 + p] = (acc_scr[pair0 + p] * jnp.where(lo, alphas[0], alphas[1])
                                          + jnp.where(lo, pvs[0], pvs[1]))

    pl.when((qi == ki) & active)(lambda: tile(True))
    pl.when((qi != ki) & active)(lambda: tile(False))

    @pl.when((fl & 2) == 2)
    def _finish():
        for bb in range(nb):
            slot0, pair0 = bb * nheads, bb * (nheads // 2)
            for p in range(nheads // 2):
                if fixed_shift:
                    la = jnp.sum(l_scr[slot0 + 2 * p], axis=1, keepdims=True)
                    lb = jnp.sum(l_scr[slot0 + 2 * p + 1], axis=1, keepdims=True)
                else:
                    la, lb = l_scr[slot0 + 2 * p], l_scr[slot0 + 2 * p + 1]
                inv = jnp.where(lo, 1.0 / la, 1.0 / lb)
                if gate_branch is not None:
                    h0 = g * nheads + 2 * p
                    src = lax.broadcasted_iota(jnp.int32, (LANES, LANES), 0)
                    dst_lo = lax.broadcasted_iota(jnp.int32, (LANES, LANES), 1) < HEAD_DIM
                    pick = jnp.where(src == jnp.where(dst_lo, 3 * h0 + gate_branch,
                                                      3 * (h0 + 1) + gate_branch), 1.0, 0.0)
                    logits = jnp.dot(gate_ref[bb].astype(BF16), pick.astype(BF16),
                                     preferred_element_type=F32)
                    inv = inv * jax.nn.sigmoid(logits)
                o_ref[bb, :, p * LANES:(p + 1) * LANES] = (acc_scr[pair0 + p] * inv).astype(o_ref.dtype)


def _tile_tables(nq, window_tiles, batch):
    qi, ki, fl = [], [], []
    for i in range(nq):
        lo = 0 if window_tiles is None else max(0, i - window_tiles)
        for j in range(lo, i + 1):
            qi.append(i)
            ki.append(j)
            fl.append((1 if j == lo else 0) | (2 if j == i else 0) | 4)
    rows = lambda a: jnp.tile(jnp.asarray(a, jnp.int32), batch)
    return rows(qi), rows(ki), rows(fl)


def _decayed_tile_tables(nq, batch, nb, ngroups, base):
    qi = np.repeat(np.arange(nq), np.arange(1, nq + 1))
    ki = np.concatenate([np.arange(i + 1) for i in range(nq)])
    nsteps = qi.shape[0]
    diag = jnp.asarray(qi == ki)
    gate = base.reshape(batch, nq, LANES)[:, :, F_LANE:F_LANE + FOX_HEADS]
    delta = gate[:, qi, :] - gate[:, np.minimum(ki + 1, nq - 1), :]
    top = jnp.max(delta.reshape(batch // nb, nb, nsteps, ngroups, FOX_HEADS // ngroups), axis=(1, 4))
    top = top.transpose(0, 2, 1).reshape(batch // nb * ngroups, nsteps)
    active = diag[None, :] | (top > ZERO_WEIGHT_LOG2)
    same_q = jnp.asarray(qi[:, None] == np.arange(nq)[None, :])
    first_ki = jnp.min(jnp.where(active[:, :, None] & same_q[None], ki[None, :, None], nq), axis=1)
    first = active & (jnp.asarray(ki)[None, :] == first_ki[:, qi])
    flags = first.astype(jnp.int32) + 2 * diag[None, :].astype(jnp.int32) + 4 * active.astype(jnp.int32)
    order = jnp.argsort(jnp.logical_not(active), axis=1, stable=True)
    n_active = jnp.sum(active, axis=1, keepdims=True)
    slot = jnp.arange(nsteps)[None, :]
    src = jnp.take_along_axis(order, jnp.minimum(slot, n_active - 1), axis=1)
    kept = slot < n_active
    take = lambda a: jnp.asarray(a, jnp.int32)[src].reshape(-1)
    fl = jnp.where(kept, jnp.take_along_axis(flags, src, axis=1), 0).reshape(-1)
    return take(qi), take(ki), fl.astype(jnp.int32)


def _flash(q, k, v, *, batch, seq, shared_kv, fixed_shift, nheads=NSA_HEADS, window=None, base=None,
           q_off=None, u=None, z=None, gate_branch=None, v_offset=0):
    nb = FLASH_ROWS if fixed_shift else SAFE_FLASH_ROWS
    nb = nb if batch % nb == 0 else 1
    nq = seq // TQ
    ngroups = NSA_HEADS // nheads
    nkv = NSA_KV_GROUPS
    da = k.shape[-1]
    has_delta = base is not None
    has_u = u is not None
    out_w = LANES * (nheads // 2)
    v_col = v_offset // out_w
    table_rows = batch // nb * ngroups
    if has_delta and fixed_shift:
        tabs = _decayed_tile_tables(nq, batch, nb, ngroups, base)
    else:
        tabs = _tile_tables(nq, None if window is None else window // TQ, table_rows)
    nsteps = int(tabs[0].shape[0]) // table_rows

    at = lambda tab, b, g, s: tab[(b * ngroups + g) * nsteps + s]
    by_q_heads = lambda b, g, s, qt, kt, ft: (b, g, at(qt, b, g, s), 0)
    by_k_heads = lambda b, g, s, qt, kt, ft: (b, g, at(kt, b, g, s), 0)
    by_q_tokens = lambda col: (lambda b, g, s, qt, kt, ft: (b, at(qt, b, g, s), g if col is None else col))
    by_k_tokens = lambda b, g, s, qt, kt, ft: (b, at(kt, b, g, s), g + v_col)

    in_specs, args = [], []
    if has_delta:
        in_specs.append(pl.BlockSpec(memory_space=pltpu.SMEM))
        args.append(base)
    in_specs.append(pl.BlockSpec((nb, nheads, TQ, LANES), by_q_heads))
    args.append(q)
    if has_delta:
        in_specs.append(pl.BlockSpec((nb, nheads, TQ, LANES), by_q_heads))
        args.append(q_off)
    if has_u:
        in_specs.append(pl.BlockSpec((nb, nkv, TQ, LANES), by_q_heads))
        args.append(u)
    if shared_kv:
        in_specs.append(pl.BlockSpec((nb, nkv, TQ, da), by_k_heads))
        in_specs.append(pl.BlockSpec((nb, nkv, TQ, LANES), by_k_heads))
    else:
        in_specs.append(pl.BlockSpec((nb, nheads, TQ, da), by_k_heads))
        in_specs.append(pl.BlockSpec((nb, TQ, out_w), by_k_tokens))
    args += [k, v]
    if gate_branch is not None:
        in_specs.append(pl.BlockSpec((nb, TQ, LANES), by_q_tokens(C_SMALL // LANES)))
        args.append(z)

    m_rows = 8 if fixed_shift else TQ
    scratch = [pltpu.VMEM((nb * nheads, m_rows, LANES), F32),
               pltpu.VMEM((nb * nheads, TQ, LANES), F32),
               pltpu.VMEM((nb * nheads // 2, TQ, LANES), F32)]
    if has_u:
        scratch.append(pltpu.VMEM((nb * nheads, TQ, 2 * LANES), BF16))

    kern = functools.partial(_flash_kernel, nheads=nheads, shared_kv=shared_kv, window=window,
                             has_delta=has_delta, has_u=has_u, gate_branch=gate_branch, nblk=nq,
                             fixed_shift=fixed_shift, nb=nb)
    return pl.pallas_call(
        kern,
        grid_spec=pltpu.PrefetchScalarGridSpec(
            num_scalar_prefetch=3,
            grid=(batch // nb, ngroups, nsteps),
            in_specs=in_specs,
            out_specs=pl.BlockSpec((nb, TQ, out_w), by_q_tokens(None)),
            scratch_shapes=scratch),
        out_shape=jax.ShapeDtypeStruct((batch, seq, 4 * LANES), BF16),
        compiler_params=_cparams(("parallel", "parallel", "arbitrary")),
        name="flash_" + ("fox" if has_delta else ("sel" if has_u else "win")),
    )(*tabs, *args)


def _cmp_kernel(q_ref, kc_ref, vc_ref, ov_ref, gate_ref, o_ref, u_ref):
    g = pl.program_id(1)
    qi = pl.program_id(2)
    tq = q_ref.shape[2]
    ncmp = kc_ref.shape[2]
    nslc = ov_ref.shape[1]

    t = qi * tq + lax.broadcasted_iota(jnp.int32, (tq, ncmp), 0)
    n = lax.broadcasted_iota(jnp.int32, (tq, ncmp), 1)
    valid = (n * NSA_CMP_STRIDE + (NSA_CMP_LEN - 1)) <= t
    kc = kc_ref[0, 0]
    vc = vc_ref[0, 0]
    ov = ov_ref[...]
    lane = lax.broadcasted_iota(jnp.int32, (tq, LANES), 1)
    lo = lane < HEAD_DIM
    gate_blk = gate_ref[0].astype(F32)

    imp = jnp.zeros((tq, nslc), F32)
    outs = []
    for r in range(NSA_GROUP_HEADS):
        s = lax.dot_general(q_ref[0, r], kc, (((1,), (1,)), ((), ())), preferred_element_type=F32)
        s = jnp.where(valid, s, NEG)
        m = jnp.max(s, axis=1, keepdims=True)
        e = jnp.where(valid, jnp.exp2(s - m), 0.0)
        d = jnp.sum(e, axis=1, keepdims=True)
        pb = (e / jnp.where(d > 0.0, d, 1.0)).astype(BF16)
        o = jnp.dot(pb, vc, preferred_element_type=F32)
        gcol = _gate_column(gate_blk, lane, 3 * (g * NSA_GROUP_HEADS + r))
        outs.append(o * gcol)
        imp = imp + jnp.dot(pb, ov, preferred_element_type=F32)
    for p in range(NSA_GROUP_HEADS // 2):
        o_ref[0, :, p * LANES:(p + 1) * LANES] = jnp.where(lo, outs[2 * p], outs[2 * p + 1]).astype(o_ref.dtype)
    _write_unselected(u_ref, imp, qi, tq, nslc)


def _cmp_fixed_kernel(q_ref, kc_ref, vc_ref, ov_ref, gate_ref, o_ref, u_ref, d_scr, acc_scr, imp_scr,
                      *, chunk):
    g = pl.program_id(1)
    qi = pl.program_id(2)
    tq = q_ref.shape[2]
    ncmp = kc_ref.shape[2]
    nslc = ov_ref.shape[1]
    t0 = qi * tq
    d_scr[...] = jnp.zeros(d_scr.shape, F32)
    acc_scr[...] = jnp.zeros(acc_scr.shape, F32)
    imp_scr[...] = jnp.zeros(imp_scr.shape, F32)

    for c in range(ncmp // chunk):
        first_end = c * chunk * NSA_CMP_STRIDE + NSA_CMP_LEN - 1
        last_end = ((c + 1) * chunk - 1) * NSA_CMP_STRIDE + NSA_CMP_LEN - 1

        def body(masked, c=c):
            rows = slice(c * chunk, (c + 1) * chunk)
            kc = kc_ref[0, 0, rows, :]
            vo = jnp.concatenate([vc_ref[0, 0, rows, :], ov_ref[rows, :]], axis=1)
            if masked:
                t = t0 + lax.broadcasted_iota(jnp.int32, (tq, chunk), 0)
                n = c * chunk + lax.broadcasted_iota(jnp.int32, (tq, chunk), 1)
                valid = (n * NSA_CMP_STRIDE + (NSA_CMP_LEN - 1)) <= t
            for r in range(NSA_GROUP_HEADS):
                s = lax.dot_general(q_ref[0, r], kc, (((1,), (1,)), ((), ())), preferred_element_type=F32)
                if masked:
                    s = jnp.where(valid, s, NEG)
                e = jnp.exp2(s)
                part = e[:, :LANES]
                for j in range(1, chunk // LANES):
                    part = part + e[:, j * LANES:(j + 1) * LANES]
                d_scr[r] = d_scr[r] + part
                both = jnp.dot(e.astype(BF16), vo, preferred_element_type=F32)
                acc_scr[r] = acc_scr[r] + both[:, :LANES]
                imp_scr[r] = imp_scr[r] + both[:, LANES:]

        pl.when((first_end <= t0 + (tq - 1)) & (last_end > t0))(functools.partial(body, True))
        pl.when(last_end <= t0)(functools.partial(body, False))

    lane = lax.broadcasted_iota(jnp.int32, (tq, LANES), 1)
    lo = lane < HEAD_DIM
    src = lax.broadcasted_iota(jnp.int32, (LANES, LANES), 0)
    dst_lo = lax.broadcasted_iota(jnp.int32, (LANES, LANES), 1) < HEAD_DIM
    imp = jnp.zeros((tq, nslc), F32)
    for p in range(NSA_GROUP_HEADS // 2):
        normed = []
        for r in (2 * p, 2 * p + 1):
            d = jnp.sum(d_scr[r], axis=1, keepdims=True)
            inv = 1.0 / jnp.where(d > 0.0, d, 1.0)
            normed.append(acc_scr[r] * inv)
            imp = imp + imp_scr[r] * inv
        h0 = g * NSA_GROUP_HEADS + 2 * p
        pick = jnp.where(src == jnp.where(dst_lo, 3 * h0, 3 * (h0 + 1)), 1.0, 0.0).astype(BF16)
        logits = jnp.dot(gate_ref[0].astype(BF16), pick, preferred_element_type=F32)
        o_ref[0, :, p * LANES:(p + 1) * LANES] = (jnp.where(lo, normed[0], normed[1])
                                                  * jax.nn.sigmoid(logits)).astype(o_ref.dtype)
    _write_unselected(u_ref, imp, qi, tq, nslc)


N_FORCED = 3


def _write_unselected(u_ref, imp, qi, tq, nslc):
    tpos = qi * tq + lax.broadcasted_iota(jnp.int32, (tq, nslc), 0)
    jblk = lax.broadcasted_iota(jnp.int32, (tq, nslc), 1)
    cur = jnp.right_shift(tpos, SEL_SHIFT)
    forced = (jblk == 0) | (jblk == cur) | (jblk == cur - 1)
    visible = jblk <= cur
    score_t = jnp.where(visible & jnp.logical_not(forced), imp, -jnp.inf).T

    def run(nrows):
        sc = score_t[:nrows]
        jt = lax.broadcasted_iota(jnp.int32, (nrows, tq), 0).astype(F32)
        for _ in range(NSA_SEL_TOPK - N_FORCED):
            mx = jnp.max(sc, axis=0, keepdims=True)
            first = jnp.min(jnp.where(sc == mx, jt, float(nslc)), axis=0, keepdims=True)
            sc = jnp.where(jt == first, -jnp.inf, sc)
        taken = jnp.where(sc == -jnp.inf, 1.0, 0.0)
        if nrows < nslc:
            taken = jnp.concatenate([taken, jnp.zeros((nslc - nrows, tq), F32)], axis=0)
        chosen = visible & (forced | (taken.T > 0.5))
        u_ref[0, 0] = jnp.where(chosen, 0.0, 1.0).astype(BF16)

    half = nslc // 2
    last_block = jnp.right_shift(qi * tq + (tq - 1), SEL_SHIFT)
    pl.when(last_block < half)(functools.partial(run, half))
    pl.when(last_block >= half)(functools.partial(run, nslc))


def _cmp_select(qn, kc, vc, ov, z, *, batch, seq, fixed_shift):
    ncmp = kc.shape[2]
    nslc = ov.shape[1]
    if fixed_shift:
        chunk = 2 * LANES if ncmp % (2 * LANES) == 0 else ncmp
        kern = functools.partial(_cmp_fixed_kernel, chunk=chunk)
        scratch = [pltpu.VMEM((NSA_GROUP_HEADS, CMP_TQ, LANES), F32) for _ in range(3)]
    else:
        kern, scratch = _cmp_kernel, []
    return pl.pallas_call(
        kern,
        scratch_shapes=scratch,
        grid=(batch, NSA_KV_GROUPS, seq // CMP_TQ),
        in_specs=[pl.BlockSpec((1, NSA_GROUP_HEADS, CMP_TQ, LANES), lambda b, g, i: (b, g, i, 0)),
                  pl.BlockSpec((1, 1, ncmp, LANES), lambda b, g, i: (b, g, 0, 0)),
                  pl.BlockSpec((1, 1, ncmp, LANES), lambda b, g, i: (b, g, 0, 0)),
                  pl.BlockSpec((ncmp, nslc), lambda b, g, i: (0, 0)),
                  pl.BlockSpec((1, CMP_TQ, LANES), lambda b, g, i: (b, i, C_SMALL // LANES))],
        out_specs=[pl.BlockSpec((1, CMP_TQ, 2 * LANES), lambda b, g, i: (b, i, g)),
                   pl.BlockSpec((1, 1, CMP_TQ, nslc), lambda b, g, i: (b, g, i, 0))],
        out_shape=[jax.ShapeDtypeStruct((batch, seq, NSA_WIDTH), BF16),
                   jax.ShapeDtypeStruct((batch, NSA_KV_GROUPS, seq, nslc), BF16)],
        compiler_params=_cparams(("parallel", "parallel", "parallel")),
        name="cmp_select",
    )(qn, kc, vc, ov, z)


def _head_rms(x, g):
    return x * lax.rsqrt(jnp.mean(x * x, axis=-1, keepdims=True) + EPS) * g


def _mem_kernel(zq_ref, kv_ref, qg_ref, kg_ref, o_ref, kn_scr, vb_scr):
    @pl.when(pl.program_id(1) == 0)
    def _prep():
        for h in range(MEM_HEADS):
            kh = kv_ref[0, :, h * MEM_HEAD_DIM:(h + 1) * MEM_HEAD_DIM]
            kn_scr[h] = _head_rms(kh, kg_ref[...]).astype(BF16)
            vb_scr[h] = kv_ref[0, :, MEM_WIDTH + h * MEM_HEAD_DIM:
                               MEM_WIDTH + (h + 1) * MEM_HEAD_DIM].astype(BF16)

    for h in range(MEM_HEADS):
        sl = slice(h * MEM_HEAD_DIM, (h + 1) * MEM_HEAD_DIM)
        qh = (_head_rms(zq_ref[0, :, sl].astype(F32), qg_ref[...])
              * ((MEM_HEAD_DIM ** -0.5) * LOG2E)).astype(BF16)
        s = lax.dot_general(qh, kn_scr[h], (((1,), (1,)), ((), ())), preferred_element_type=F32)
        m = jnp.max(s, axis=1, keepdims=True)
        e = jnp.exp2(s - m)
        p = (e / jnp.sum(e, axis=1, keepdims=True)).astype(BF16)
        o_ref[0, :, sl] = jnp.dot(p, vb_scr[h], preferred_element_type=F32).astype(o_ref.dtype)


def _mem_attention(z, mem_kv, q_g, k_g, *, batch, seq):
    mlen = mem_kv.shape[1]
    tq = TQ
    return pl.pallas_call(
        _mem_kernel,
        grid=(batch, seq // tq),
        in_specs=[pl.BlockSpec((1, tq, MEM_WIDTH), lambda b, i: (b, i, C_MEM_Q // MEM_WIDTH)),
                  pl.BlockSpec((1, mlen, 2 * MEM_WIDTH), lambda b, i: (b, 0, 0)),
                  pl.BlockSpec((1, MEM_HEAD_DIM), lambda b, i: (0, 0)),
                  pl.BlockSpec((1, MEM_HEAD_DIM), lambda b, i: (0, 0))],
        out_specs=pl.BlockSpec((1, tq, MEM_WIDTH), lambda b, i: (b, i, 0)),
        out_shape=jax.ShapeDtypeStruct((batch, seq, MEM_WIDTH), BF16),
        scratch_shapes=[pltpu.VMEM((MEM_HEADS, mlen, MEM_HEAD_DIM), BF16),
                        pltpu.VMEM((MEM_HEADS, mlen, MEM_HEAD_DIM), BF16)],
        compiler_params=_cparams(("parallel", "arbitrary")),
        name="mem_attention",
    )(z, mem_kv, q_g.reshape(1, MEM_HEAD_DIM), k_g.reshape(1, MEM_HEAD_DIM))


def _sigmoid(x):
    return 0.5 * jnp.tanh(0.5 * x) + 0.5


def _out_kernel(x_ref, oc_ref, os_ref, ow_ref, ob_ref, om_ref, sa_ref, sb_ref, sm_ref,
                g0_ref, g1_ref, g2_ref, wa_ref, wb_ref, wm_ref, wo_ref, y_ref):
    silu = lambda ref: (lambda x: x * _sigmoid(x))(ref[...].astype(F32))
    oa = (oc_ref[...].astype(F32) + os_ref[...].astype(F32) + ow_ref[...].astype(F32)) * silu(sa_ref)
    ob = ob_ref[...].astype(F32) * silu(sb_ref)
    om = om_ref[...].astype(F32) * silu(sm_ref)
    gate = lambda ref: _sigmoid(ref[...].astype(F32))
    u = (gate(g0_ref) * jnp.dot(oa.astype(BF16), wa_ref[...], preferred_element_type=F32)
         + gate(g1_ref) * jnp.dot(ob.astype(BF16), wb_ref[...], preferred_element_type=F32)
         + gate(g2_ref) * jnp.dot(om.astype(BF16), wm_ref[...], preferred_element_type=F32))
    y_ref[...] = x_ref[...] + jnp.dot(u.astype(BF16), wo_ref[...], preferred_element_type=F32)


def _out_proj(x2d, o_c, o_s, o_w, o_b, o_m, z2d, wa, wb, wm, wo, layer):
    m = x2d.shape[0]
    tm = 512
    w512 = 512
    row512 = lambda c: pl.BlockSpec((tm, w512), lambda i, c=c: (i, c))
    row1024 = lambda c: pl.BlockSpec((tm, D_MODEL), lambda i, c=c: (i, c))
    full = lambda shape: pl.BlockSpec((None,) + shape, lambda i: (layer, 0, 0))
    return pl.pallas_call(
        _out_kernel,
        grid=(m // tm,),
        in_specs=[row1024(0), row512(0), row512(0), row512(0), row512(0), row512(0),
                  row512(C_NSA_SILU // w512), row512(C_FOX_SILU // w512), row512(C_MEM_SILU // w512),
                  row1024(C_MERGE // D_MODEL), row1024(C_MERGE // D_MODEL + 1),
                  row1024(C_MERGE // D_MODEL + 2),
                  full((w512, D_MODEL)), full((w512, D_MODEL)), full((w512, D_MODEL)),
                  full((D_MODEL, D_MODEL))],
        out_specs=row1024(0),
        out_shape=jax.ShapeDtypeStruct((m, D_MODEL), F32),
        compiler_params=_cparams(("parallel",)),
        name="out_proj",
    )(x2d, o_c, o_s, o_w, o_b, o_m, z2d, z2d, z2d, z2d, z2d, z2d, wa, wb, wm, wo)


def _fox_head_order(a, order, sections):
    nl, d, _ = a.shape
    a = a.reshape(nl, d, sections, FOX_HEADS, HEAD_DIM)
    return jnp.take_along_axis(a, order[:, None, None, :, None], axis=3).reshape(nl, d, -1)


def _permute_w_in(w_in, fox_order):
    o = np.cumsum([0, NSA_WIDTH, 6 * NSA_KV_GROUPS * HEAD_DIM, 3 * NSA_HEADS, NSA_WIDTH,
                   3 * FOX_WIDTH, FOX_HEADS, FOX_WIDTH, MEM_WIDTH, MEM_WIDTH, 3 * D_MODEL])
    nsa_q, nsa_kv, nsa_gate, nsa_silu, fox_qkv, fox_f, fox_silu, mem_q, mem_silu, merge = [
        w_in[..., o[i]:o[i + 1]] for i in range(10)]
    fox_qkv = _fox_head_order(fox_qkv, fox_order, 3)
    fox_silu = _fox_head_order(fox_silu, fox_order, 1)
    fox_f = jnp.take_along_axis(fox_f, fox_order[:, None, :], axis=2)
    pad = jnp.zeros(w_in.shape[:2] + (LANES - 3 * NSA_HEADS - FOX_HEADS,), w_in.dtype)
    return jnp.concatenate([nsa_q, nsa_silu, fox_qkv, fox_silu, mem_q, mem_silu, merge, nsa_kv,
                            nsa_gate, fox_f, pad], axis=-1).astype(BF16)


def _pad_lanes(x, width=LANES):
    return jnp.pad(x, [(0, 0)] * (x.ndim - 1) + [(0, width - x.shape[-1])])


def _tile2(g):
    return jnp.concatenate([g, g], axis=-1)[:, None, :]


def _overlap(n_cmp_pad, n_slc):
    cs = np.arange(n_cmp_pad)[:, None] * NSA_CMP_STRIDE
    ss = np.arange(NSLC_PAD)[None, :] * NSA_SEL_LEN
    ov = np.clip(np.minimum(cs + NSA_CMP_LEN, ss + NSA_SEL_LEN) - np.maximum(cs, ss), 0, None)
    ov = ov / NSA_CMP_LEN
    ov[-1] = 0.0
    ov[:, n_slc:] = 0.0
    return jnp.asarray(ov, BF16)


def _aug_permutations():
    pq = np.zeros((6 * LANES, FOX_HEADS * LANES), np.float32)
    pk = np.zeros((6 * LANES, FOX_HEADS * LANES), np.float32)
    for h in range(FOX_HEADS):
        for i in range(3):
            pq[i * LANES + F_LANE + h, h * LANES + AUG + i] = 1.0
            pk[i * LANES + F_LANE + h, h * LANES + AUG + 3 + i] = -1.0
            pk[(3 + i) * LANES + F_LANE + h, h * LANES + AUG + 6 + i] = -1.0
    return jnp.asarray(pq, BF16), jnp.asarray(pk, BF16)


def _compress_w1(w1):
    nl = w1.shape[0]
    halves = w1.reshape(nl, 2, NSA_CMP_STRIDE, 1, HEAD_DIM, NSA_CMP_HIDDEN)
    eye = jnp.eye(NSA_KV_GROUPS, dtype=w1.dtype)
    out = jnp.einsum("pg,zhldn->zlpdghn", eye, halves[:, :, :, 0])
    return out.reshape(nl, NSA_CMP_STRIDE * NSA_KV_GROUPS * HEAD_DIM,
                       NSA_KV_GROUPS * 2 * NSA_CMP_HIDDEN).astype(BF16)


def _layer(x, mem, p, big, layer, consts):
    batch, seq, _ = x.shape
    m_rows = batch * seq
    cos4, sin_signed, tri, head_ones, perm_q, perm_k, ov = consts
    z2d = _norm_matmul(x.reshape(m_rows, D_MODEL), p["norm_g"], big["w_in"], layer, 512, Z_WIDTH // 3, BF16)
    z = z2d.reshape(batch, seq, Z_WIDTH)

    (qn, kcr, vcr, ks, vs, kw, vw, fqd, fqo, fka, base) = _prep(
        z, cos4, sin_signed, tri, head_ones, perm_q, perm_k,
        p["gq"], p["gks"], p["gkw"], p["gfq"], p["gfk"], p["fb"],
        p["s_sel"], p["s_win"], p["s_fox"], batch=batch, seq=seq)

    nchunk = seq // NSA_CMP_STRIDE
    kc, vc = _compress(kcr.reshape(batch, nchunk, NSA_CMP_STRIDE * LANES),
                       vcr.reshape(batch, nchunk, NSA_CMP_STRIDE * LANES),
                       p["w1k"], p["w1v"], p["bk"], p["bv"], p["w2k"], p["w2v"], p["gkc"], p["s_cmp"],
                       batch=batch)

    def attend(fixed_shift):
        o_c, unsel = _cmp_select(qn, kc, vc, ov, z, batch=batch, seq=seq, fixed_shift=fixed_shift)
        o_s = _flash(qn, ks, vs, batch=batch, seq=seq, shared_kv=True, fixed_shift=fixed_shift,
                     u=unsel, z=z, gate_branch=1)
        o_w = _flash(qn, kw, vw, batch=batch, seq=seq, shared_kv=True, fixed_shift=fixed_shift,
                     window=NSA_WINDOW, z=z, gate_branch=2)
        o_b = _flash(fqd, fka, z, batch=batch, seq=seq, shared_kv=False, fixed_shift=fixed_shift,
                     nheads=FOX_HEADS // 2, base=base.reshape(-1), q_off=fqo, v_offset=C_FOX_V)
        return o_c, o_s, o_w, o_b

    o_c, o_s, o_w, o_b = lax.cond(p["bound_ok"], lambda: attend(True), lambda: attend(False))

    mlen = mem.shape[1]
    mem_kv = _norm_matmul(mem.reshape(batch * mlen, D_MODEL), p["mem_norm_g"], big["w_mem_kv"], layer,
                          min(512, batch * mlen), 2 * MEM_WIDTH, F32).reshape(batch, mlen, 2 * MEM_WIDTH)
    o_m = _mem_attention(z, mem_kv, p["mem_q_norm"], p["mem_k_norm"], batch=batch, seq=seq)

    y = _out_proj(x.reshape(m_rows, D_MODEL), o_c.reshape(m_rows, -1), o_s.reshape(m_rows, -1),
                  o_w.reshape(m_rows, -1), o_b.reshape(m_rows, -1), o_m.reshape(m_rows, -1), z2d,
                  big["w_branch_a"], big["w_branch_b"], big["w_branch_m"], big["w_out"], layer)
    return y.reshape(batch, seq, D_MODEL)


def kernel(x, mem, norm_g, mem_norm_g, w_in, nsa_q_norm, nsa_k_norm, cmp_pe_k, cmp_w1_k, cmp_w2_k,
           cmp_pe_v, cmp_w1_v, cmp_w2_v, fox_q_norm, fox_k_norm, fox_f_bias, mem_q_norm, mem_k_norm,
           w_mem_kv, w_branch_a, w_branch_b, w_branch_m, w_out):
    batch, seq, _ = x.shape
    depth = w_in.shape[0]
    n_slc = seq // NSA_SEL_LEN
    assert n_slc <= NSLC_PAD and seq % TQ == 0 and NSA_WINDOW == TQ

    half = HEAD_DIM // 2
    inv_freq = ROPE_THETA ** (-jnp.arange(half, dtype=F32) / half)
    ang = jnp.arange(seq).astype(F32)[:, None] * inv_freq[None, :]
    cos, sin = jnp.cos(ang), jnp.sin(ang)
    cos4 = jnp.concatenate([cos, cos, cos, cos], axis=-1)
    sin_signed = jnp.concatenate([-sin, sin, -sin, sin], axis=-1)
    tri = jnp.asarray(np.tril(np.ones((TQ, TQ), np.float32)), BF16)
    head_ones = jnp.asarray(np.kron(np.eye(2), np.ones((HEAD_DIM, HEAD_DIM))), BF16)
    consts = (cos4, sin_signed, tri, head_ones, *_aug_permutations(),
              _overlap(seq // NSA_CMP_STRIDE, n_slc))

    hp = lax.Precision.HIGHEST
    pe_bias = lambda pe, w1: jnp.einsum("lk,lkn->ln", pe.reshape(depth, -1), w1, precision=hp)[:, None, :]
    fox_order = jnp.argsort(fox_f_bias, axis=-1)
    fox_bias_sorted = jnp.take_along_axis(fox_f_bias, fox_order, axis=-1)
    w_branch_b_sorted = jnp.take_along_axis(
        w_branch_b.reshape(depth, FOX_HEADS, HEAD_DIM, D_MODEL), fox_order[:, :, None, None], axis=1
    ).reshape(depth, FOX_WIDTH, D_MODEL)
    fb = jnp.zeros((depth, 1, LANES), F32).at[:, 0, F_LANE:F_LANE + FOX_HEADS].set(fox_bias_sorted)

    def logit_bound(gq, gk):
        bound = (BOUND_MARGIN * HEAD_DIM * QK_SCALE
                 * jnp.max(jnp.abs(gq), axis=-1) * jnp.max(jnp.abs(gk), axis=-1))
        return bound.astype(BF16).astype(F32)

    bounds = [logit_bound(nsa_q_norm, nsa_k_norm[:, 1]), logit_bound(nsa_q_norm, nsa_k_norm[:, 2]),
              logit_bound(fox_q_norm, fox_k_norm), logit_bound(nsa_q_norm, nsa_k_norm[:, 0])]
    bound_ok = functools.reduce(jnp.maximum, bounds) <= MAX_FIXED_BOUND
    bound_row = lambda s: jnp.broadcast_to(s[:, None, None], (depth, 1, LANES))
    stacked = dict(
        s_sel=bound_row(bounds[0]), s_win=bound_row(bounds[1]), s_fox=bound_row(bounds[2]),
        s_cmp=bound_row(bounds[3]),
        bound_ok=bound_ok,
        norm_g=norm_g, mem_norm_g=mem_norm_g,
        gq=_tile2(nsa_q_norm), gks=_tile2(nsa_k_norm[:, 1]), gkw=_tile2(nsa_k_norm[:, 2]),
        gkc=_pad_lanes(nsa_k_norm[:, 0])[:, None, :], gfq=_tile2(fox_q_norm), gfk=_tile2(fox_k_norm),
        fb=fb, w1k=_compress_w1(cmp_w1_k), w1v=_compress_w1(cmp_w1_v),
        bk=pe_bias(cmp_pe_k, cmp_w1_k), bv=pe_bias(cmp_pe_v, cmp_w1_v),
        w2k=_pad_lanes(cmp_w2_k).astype(BF16),
        w2v=jnp.concatenate([cmp_w2_v, cmp_w2_v], axis=-1).astype(BF16),
        mem_q_norm=mem_q_norm, mem_k_norm=mem_k_norm)
    big = dict(w_in=_permute_w_in(w_in, fox_order), w_mem_kv=w_mem_kv.astype(BF16),
               w_branch_a=w_branch_a.astype(BF16), w_branch_b=w_branch_b_sorted.astype(BF16),
               w_branch_m=w_branch_m.astype(BF16), w_out=w_out.astype(BF16))
    for l in range(depth):
        x = _layer(x, mem, {k: v[l] for k, v in stacked.items()}, big, l, consts)
    return x
```

```python
import functools

import numpy as np
import jax
import jax.numpy as jnp
from jax import lax
from jax.experimental import pallas as pl
from jax.experimental.pallas import tpu as pltpu

F32 = jnp.float32
BF16 = jnp.bfloat16

D_MODEL = 1024
HEAD_DIM = 64
ROPE_THETA = 10000.0
EPS = 1e-6
NSA_HEADS = 8
NSA_KV_GROUPS = 2
NSA_GROUP_HEADS = NSA_HEADS // NSA_KV_GROUPS
NSA_CMP_LEN = 32
NSA_CMP_STRIDE = 16
NSA_CMP_HIDDEN = 128
NSA_SEL_LEN = 64
NSA_SEL_TOPK = 16
NSA_WINDOW = 512
FORCE_SCORE = 1e9
NSA_WIDTH = NSA_HEADS * HEAD_DIM
FOX_HEADS = 8
FOX_WIDTH = FOX_HEADS * HEAD_DIM
MEM_HEADS = 4
MEM_HEAD_DIM = 128
MEM_WIDTH = MEM_HEADS * MEM_HEAD_DIM

LANES = 128
VMEM_LIMIT = 56 * 1024 * 1024
NEG = -1e30
SEL_MASK = 32768.0
SEL_SHIFT = 6
NSLC_PAD = LANES
LOG2E = 1.4426950408889634
QK_SCALE = (HEAD_DIM ** -0.5) * LOG2E
BOUND_MARGIN = 1.02
MAX_FIXED_BOUND = 60.0
ZERO_WEIGHT_LOG2 = -160.0

C_NSA_Q = 0
C_NSA_SILU = 512
C_FOX_Q = 1024
C_FOX_K = 1536
C_FOX_V = 2048
C_FOX_SILU = 2560
C_MEM_Q = 3072
C_MEM_SILU = 3584
C_MERGE = 4096
C_NSA_KV = 7168
C_SMALL = 7936
Z_WIDTH = 8064
F_LANE = 3 * NSA_HEADS
AUG = HEAD_DIM

TQ = 512
FLASH_ROWS = 4
SAFE_FLASH_ROWS = 2
CMP_TQ = 1024


def _cparams(sem):
    return pltpu.CompilerParams(dimension_semantics=sem, vmem_limit_bytes=VMEM_LIMIT)


def _norm_matmul_kernel(x_ref, g_ref, w_ref, o_ref, *, tn):
    x = x_ref[...]
    ms = jnp.mean(x * x, axis=-1, keepdims=True)
    h = (x * lax.rsqrt(ms + EPS) * g_ref[...]).astype(BF16)
    for c0 in range(0, w_ref.shape[1], tn):
        o_ref[:, c0:c0 + tn] = jnp.dot(h, w_ref[:, c0:c0 + tn],
                                       preferred_element_type=F32).astype(o_ref.dtype)


def _norm_matmul(x2d, g, w, layer, tm, tn, out_dtype):
    m, k = x2d.shape
    n = w.shape[2]
    return pl.pallas_call(
        functools.partial(_norm_matmul_kernel, tn=tn),
        grid=(m // tm,),
        in_specs=[pl.BlockSpec((tm, k), lambda i: (i, 0)),
                  pl.BlockSpec((1, k), lambda i: (0, 0)),
                  pl.BlockSpec((None, k, n), lambda i: (layer, 0, 0), pipeline_mode=pl.Buffered(1))],
        out_specs=pl.BlockSpec((tm, n), lambda i: (i, 0)),
        out_shape=jax.ShapeDtypeStruct((m, n), out_dtype),
        compiler_params=_cparams(("parallel",)),
        name="norm_matmul",
    )(x2d, g.reshape(1, k), w)


def _pair_rms(x, g2, head_ones):
    sq = x * x
    hi = sq.astype(BF16)
    low = (sq - hi.astype(F32)).astype(BF16)
    ssq = (jnp.dot(hi, head_ones, preferred_element_type=F32)
           + jnp.dot(low, head_ones, preferred_element_type=F32))
    return x * lax.rsqrt(ssq * (1.0 / HEAD_DIM) + EPS) * g2


def _pair_rope(y, cos4, sin_signed, first_half):
    rot = jnp.where(first_half, pltpu.roll(y, LANES - HEAD_DIM // 2, 1), pltpu.roll(y, HEAD_DIM // 2, 1))
    return y * cos4 + rot * sin_signed


def _head_a(y, lo, tail=0.0):
    return jnp.where(lo, y, tail)


def _head_b(y, lo, tail=0.0):
    return jnp.where(lo, pltpu.roll(y, HEAD_DIM, 1), tail)


def _split3_f32(c):
    hi = c.astype(BF16).astype(F32)
    r = c - hi
    mid = r.astype(BF16).astype(F32)
    return hi, mid, r - mid


def _prep_kernel(zq_ref, zc_ref, zs_ref, zw_ref, fq_ref, fk_ref, sm_ref, cos_ref, sin_ref,
                 tri_ref, ones_ref, pq_ref, pk_ref, gq_ref, gks_ref, gkw_ref, gfq_ref, gfk_ref, fb_ref,
                 ss_ref, sw_ref, sf_ref,
                 qn_ref, kcr_ref, vcr_ref, ks_ref, vs_ref, kw_ref, vw_ref, fqd_ref, fqo_ref, fka_ref,
                 base_ref, run_scr):
    ti = pl.program_id(1)
    tt = zq_ref.shape[1]
    lane = lax.broadcasted_iota(jnp.int32, (tt, LANES), 1)
    lo = lane < HEAD_DIM
    first_half = (lane & (HEAD_DIM // 2)) == 0
    cos4 = cos_ref[...]
    sin_signed = sin_ref[...]
    head_ones = ones_ref[...]
    one_at_aug = jnp.where(lane == AUG, 1.0, 0.0)

    for p in range(NSA_HEADS // 2):
        y = _pair_rope(_pair_rms(zq_ref[0, :, p * LANES:(p + 1) * LANES].astype(F32), gq_ref[...], head_ones),
                       cos4, sin_signed, first_half) * QK_SCALE
        qn_ref[0, 2 * p] = _head_a(y, lo, one_at_aug).astype(BF16)
        qn_ref[0, 2 * p + 1] = _head_b(y, lo, one_at_aug).astype(BF16)

    kcr_ref[0] = _pair_rope(zc_ref[0, :, :LANES].astype(F32), cos4, sin_signed, first_half).astype(BF16)
    vcr_ref[0] = zc_ref[0, :, LANES:].astype(BF16)

    sel_lane = jnp.right_shift(ti * tt + lax.broadcasted_iota(jnp.int32, (tt, LANES), 0), SEL_SHIFT)
    onehot = jnp.where(lane == sel_lane, -SEL_MASK, 0.0).astype(BF16)
    for z_ref, g_ref, k_ref, v_ref, bound_ref, with_onehot in (
            (zs_ref, gks_ref, ks_ref, vs_ref, ss_ref, True),
            (zw_ref, gkw_ref, kw_ref, vw_ref, sw_ref, False)):
        y = _pair_rope(_pair_rms(z_ref[0, :, :LANES].astype(F32), g_ref[...], head_ones),
                       cos4, sin_signed, first_half)
        v = z_ref[0, :, LANES:].astype(F32)
        vr = pltpu.roll(v, HEAD_DIM, 1)
        neg_bound = jnp.where(lane == AUG, -bound_ref[...], 0.0)
        k_ref[0, 0, :, :LANES] = _head_a(y, lo, neg_bound).astype(BF16)
        k_ref[0, 1, :, :LANES] = _head_b(y, lo, neg_bound).astype(BF16)
        if with_onehot:
            k_ref[0, 0, :, LANES:] = onehot
            k_ref[0, 1, :, LANES:] = onehot
        v_ref[0, 0] = jnp.where(lo, v, vr).astype(BF16)
        v_ref[0, 1] = jnp.where(lo, vr, v).astype(BF16)

    @pl.when(ti == 0)
    def _reset():
        run_scr[...] = jnp.zeros(run_scr.shape, F32)

    xf = sm_ref[0].astype(F32) + fb_ref[...]
    log_f = (jnp.minimum(xf, 0.0) - jnp.log1p(jnp.exp(-jnp.abs(xf)))) * LOG2E
    tri = tri_ref[...]
    parts = _split3_f32(log_f)
    cum = sum(jnp.dot(tri, part.astype(BF16), preferred_element_type=F32) for part in parts)
    base_ref[0, 0] = run_scr[...]
    run_scr[...] = run_scr[...] + cum[tt - 1:tt, :]

    parts6 = _split3_f32(cum) + _split3_f32(cum - cum[tt - 1:tt, :])
    x6 = jnp.concatenate([part.astype(BF16) for part in parts6], axis=1)
    aug_q = jnp.dot(x6, pq_ref[...], preferred_element_type=F32)
    aug_k = jnp.dot(x6, pk_ref[...], preferred_element_type=F32)

    lane1 = lax.broadcasted_iota(jnp.int32, (1, LANES), 1)
    span = lambda a, b: jnp.where((lane1 >= a) & (lane1 < b), 1.0, 0.0)
    q_tail_diag = span(AUG + 3, AUG + 6) + span(AUG + 9, AUG + 10)
    q_tail_off = span(AUG + 6, AUG + 10)
    k_tail = span(AUG, AUG + 3) + jnp.where(lane1 == AUG + 9, -sf_ref[...], 0.0)
    for p in range(FOX_HEADS // 2):
        sl = slice(p * LANES, (p + 1) * LANES)
        yq = _pair_rms(fq_ref[0, :, sl].astype(F32), gfq_ref[...], head_ones) * QK_SCALE
        yk = _pair_rms(fk_ref[0, :, sl].astype(F32), gfk_ref[...], head_ones)
        for e, split in ((0, _head_a), (1, _head_b)):
            h = 2 * p + e
            hs = slice(h * LANES, (h + 1) * LANES)
            qh = split(yq, lo) + aug_q[:, hs]
            fqd_ref[0, h] = (qh + q_tail_diag).astype(BF16)
            fqo_ref[0, h] = (qh + q_tail_off).astype(BF16)
            fka_ref[0, h] = (split(yk, lo) + aug_k[:, hs] + k_tail).astype(BF16)


def _prep(z, cos4, sin_signed, tri, head_ones, perm_q, perm_k, gq, gks, gkw, gfq, gfk, fb,
          s_sel, s_win, s_fox, *, batch, seq):
    nblk = seq // TQ
    whole = lambda a: pl.BlockSpec(a.shape, lambda b, i: (0, 0))
    zspec = lambda w, c: pl.BlockSpec((1, TQ, w), lambda b, i, c=c: (b, i, c))
    row = pl.BlockSpec((1, LANES), lambda b, i: (0, 0))
    tab = pl.BlockSpec((TQ, LANES), lambda b, i: (i, 0))
    heads = lambda n, w: pl.BlockSpec((1, n, TQ, w), lambda b, i: (b, 0, i, 0))
    tok = lambda w: pl.BlockSpec((1, TQ, w), lambda b, i: (b, i, 0))
    bf = lambda *shape: jax.ShapeDtypeStruct(shape, BF16)
    return pl.pallas_call(
        _prep_kernel,
        grid=(batch, nblk),
        in_specs=[zspec(NSA_WIDTH, C_NSA_Q // NSA_WIDTH),
                  zspec(2 * LANES, C_NSA_KV // (2 * LANES)),
                  zspec(2 * LANES, C_NSA_KV // (2 * LANES) + 1),
                  zspec(2 * LANES, C_NSA_KV // (2 * LANES) + 2),
                  zspec(FOX_WIDTH, C_FOX_Q // FOX_WIDTH),
                  zspec(FOX_WIDTH, C_FOX_K // FOX_WIDTH),
                  zspec(LANES, C_SMALL // LANES),
                  tab, tab, whole(tri), whole(head_ones), whole(perm_q), whole(perm_k),
                  row, row, row, row, row, row, row, row, row],
        out_specs=[heads(NSA_HEADS, LANES), tok(LANES), tok(LANES),
                   heads(NSA_KV_GROUPS, 2 * LANES), heads(NSA_KV_GROUPS, LANES),
                   heads(NSA_KV_GROUPS, LANES), heads(NSA_KV_GROUPS, LANES),
                   heads(FOX_HEADS, LANES), heads(FOX_HEADS, LANES), heads(FOX_HEADS, LANES),
                   pl.BlockSpec((1, 1, 1, LANES), lambda b, i: (b, i, 0, 0))],
        out_shape=[bf(batch, NSA_HEADS, seq, LANES), bf(batch, seq, LANES), bf(batch, seq, LANES),
                   bf(batch, NSA_KV_GROUPS, seq, 2 * LANES), bf(batch, NSA_KV_GROUPS, seq, LANES),
                   bf(batch, NSA_KV_GROUPS, seq, LANES), bf(batch, NSA_KV_GROUPS, seq, LANES),
                   bf(batch, FOX_HEADS, seq, LANES), bf(batch, FOX_HEADS, seq, LANES),
                   bf(batch, FOX_HEADS, seq, LANES),
                   jax.ShapeDtypeStruct((batch, nblk, 1, LANES), F32)],
        scratch_shapes=[pltpu.VMEM((1, LANES), F32)],
        compiler_params=_cparams(("parallel", "arbitrary")),
        name="prep",
    )(z, z, z, z, z, z, z, cos4, sin_signed, tri, head_ones, perm_q, perm_k,
      gq, gks, gkw, gfq, gfk, fb, s_sel, s_win, s_fox)


def _compress_kernel(xk_ref, xv_ref, w1k_ref, w1v_ref, bk_ref, bv_ref, w2k_ref, w2v_ref, kg_ref,
                     sc_ref, kc_ref, vc_ref):
    nchunk = xk_ref.shape[1]
    lane = lax.broadcasted_iota(jnp.int32, (nchunk, LANES), 1)
    for x_ref, w1_ref, b_ref, w2_ref, o_ref, is_key in ((xk_ref, w1k_ref, bk_ref, w2k_ref, kc_ref, True),
                                                        (xv_ref, w1v_ref, bv_ref, w2v_ref, vc_ref, False)):
        h = jnp.dot(x_ref[0], w1_ref[...], preferred_element_type=F32)
        for g in range(NSA_KV_GROUPS):
            c0 = g * 2 * NSA_CMP_HIDDEN
            top = h[:, c0:c0 + NSA_CMP_HIDDEN]
            bot = h[:, c0 + NSA_CMP_HIDDEN:c0 + 2 * NSA_CMP_HIDDEN]
            hid = top + pltpu.roll(bot, nchunk - 1, 0) + b_ref[...]
            act = (hid * jax.nn.sigmoid(hid)).astype(BF16)
            o = jnp.dot(act, w2_ref[...], preferred_element_type=F32)
            if is_key:
                ms = jnp.sum(o * o, axis=1, keepdims=True) * (1.0 / HEAD_DIM)
                o = o * lax.rsqrt(ms + EPS) * kg_ref[...]
                o = jnp.where(lane == AUG, -sc_ref[...], o)
            o_ref[0, g] = o.astype(BF16)


def _compress(xk, xv, w1k, w1v, bk, bv, w2k, w2v, kg, s_cmp, *, batch):
    nchunk, kin = xk.shape[1], xk.shape[2]
    xspec = pl.BlockSpec((1, nchunk, kin), lambda b: (b, 0, 0))
    full = lambda a: pl.BlockSpec(a.shape, lambda b: (0,) * a.ndim)
    ospec = pl.BlockSpec((1, NSA_KV_GROUPS, nchunk, LANES), lambda b: (b, 0, 0, 0))
    oshape = jax.ShapeDtypeStruct((batch, NSA_KV_GROUPS, nchunk, LANES), BF16)
    return pl.pallas_call(
        _compress_kernel,
        grid=(batch,),
        in_specs=[xspec, xspec, full(w1k), full(w1v), full(bk), full(bv), full(w2k), full(w2v), full(kg),
                  full(s_cmp)],
        out_specs=[ospec, ospec],
        out_shape=[oshape, oshape],
        compiler_params=_cparams(("parallel",)),
        name="compress",
    )(xk, xv, w1k, w1v, bk, bv, w2k, w2v, kg, s_cmp)


def _gate_column(gate_blk, lane, idx):
    col = jnp.sum(jnp.where(lane == idx, gate_blk, 0.0), axis=1, keepdims=True)
    return jax.nn.sigmoid(col)


def _flash_kernel(qi_tab, ki_tab, fl_tab, *refs, nheads, shared_kv, window, has_delta,
                  has_u, gate_branch, nblk, fixed_shift, nb):
    refs = list(refs)
    base_ref = refs.pop(0) if has_delta else None
    q_ref = refs.pop(0)
    qoff_ref = refs.pop(0) if has_delta else None
    u_ref = refs.pop(0) if has_u else None
    k_ref = refs.pop(0)
    v_ref = refs.pop(0)
    gate_ref = refs.pop(0) if gate_branch is not None else None
    o_ref = refs.pop(0)
    m_scr = refs.pop(0)
    l_scr = refs.pop(0)
    acc_scr = refs.pop(0)
    qcat_scr = refs.pop(0) if has_u else None

    b = pl.program_id(0)
    g = pl.program_id(1)
    step = (b * pl.num_programs(1) + g) * pl.num_programs(2) + pl.program_id(2)
    qi = qi_tab[step]
    ki = ki_tab[step]
    fl = fl_tab[step]
    active = (fl & 4) == 4
    tq = q_ref.shape[2]
    tk = k_ref.shape[2]

    @pl.when((fl & 1) == 1)
    def _init():
        m_scr[...] = jnp.full(m_scr.shape, NEG, F32)
        l_scr[...] = jnp.zeros(l_scr.shape, F32)
        acc_scr[...] = jnp.zeros(acc_scr.shape, F32)
        if has_u:
            for bb in range(nb):
                for r in range(nheads):
                    qcat_scr[bb * nheads + r, :, :LANES] = q_ref[bb, r]
                    qcat_scr[bb * nheads + r, :, LANES:] = u_ref[bb, r // NSA_GROUP_HEADS]

    lane = lax.broadcasted_iota(jnp.int32, (tq, LANES), 1)
    lo = lane < HEAD_DIM

    def tile(diagonal):
        masked = diagonal or window is not None
        if masked:
            row = lax.broadcasted_iota(jnp.int32, (tq, tk), 0)
            col = lax.broadcasted_iota(jnp.int32, (tq, tk), 1)
            if diagonal:
                valid = row >= col
            else:
                valid = (tq + row - col) < window
        for bb in range(nb):
            slot0, pair0, brow = bb * nheads, bb * (nheads // 2), b * nb + bb
            for p in range(nheads // 2):
                pvs, alphas = [], []
                for e in range(2):
                    hh = 2 * p + e
                    if has_u:
                        q = qcat_scr[slot0 + hh]
                    elif has_delta and not diagonal:
                        q = qoff_ref[bb, hh]
                    else:
                        q = q_ref[bb, hh]
                    k = k_ref[bb, hh // NSA_GROUP_HEADS] if shared_kv else k_ref[bb, hh]
                    v = (v_ref[bb, hh // NSA_GROUP_HEADS] if shared_kv
                         else v_ref[bb, :, p * LANES:(p + 1) * LANES])
                    s = lax.dot_general(q, k, (((1,), (1,)), ((), ())), preferred_element_type=F32)
                    if masked:
                        s = jnp.where(valid, s, NEG)
                    delta = None
                    if has_delta and not diagonal:
                        hoff = F_LANE + g * nheads + hh
                        delta = (base_ref[(brow * nblk + qi) * LANES + hoff]
                                 - base_ref[(brow * nblk + ki + 1) * LANES + hoff])
                    if fixed_shift:
                        pm = jnp.exp2(s)
                        rowsum = pm[:, :LANES]
                        for j in range(1, tk // LANES):
                            rowsum = rowsum + pm[:, j * LANES:(j + 1) * LANES]
                        pv = jnp.dot(pm.astype(BF16), v, preferred_element_type=F32)
                        if delta is not None:
                            w = jnp.exp2(jnp.full((1, LANES), delta, F32))
                            rowsum = rowsum * w
                            pv = pv * w
                        l_scr[slot0 + hh] = l_scr[slot0 + hh] + rowsum
                        pvs.append(pv)
                    else:
                        m_prev = m_scr[slot0 + hh]
                        m_cur = jnp.max(s, axis=1, keepdims=True)
                        if delta is not None:
                            m_cur = m_cur + delta
                        m_new = jnp.maximum(m_prev, m_cur)
                        alpha = jnp.exp2(m_prev - m_new)
                        shift = m_new[:, :1]
                        if delta is not None:
                            shift = shift - delta
                        pm = jnp.exp2(s - shift)
                        l_scr[slot0 + hh] = alpha * l_scr[slot0 + hh] + jnp.sum(pm, axis=1, keepdims=True)
                        m_scr[slot0 + hh] = m_new
                        pvs.append(jnp.dot(pm.astype(BF16), v, preferred_element_type=F32))
                        alphas.append(alpha)
                if fixed_shift:
                    acc_scr[pair0 + p] = acc_scr[pair0 + p] + jnp.where(lo, pvs[0], pvs[1])
                else:
                    acc_scr[pair0 + p] = (acc_scr[pair0 + p] * jnp.where(lo, alphas[0], alphas[1])
                                          + jnp.where(lo, pvs[0], pvs[1]))

    pl.when((qi == ki) & active)(lambda: tile(True))
    pl.when((qi != ki) & active)(lambda: tile(False))

    @pl.when((fl & 2) == 2)
    def _finish():
        for bb in range(nb):
            slot0, pair0 = bb * nheads, bb * (nheads // 2)
            for p in range(nheads // 2):
                if fixed_shift:
                    la = jnp.sum(l_scr[slot0 + 2 * p], axis=1, keepdims=True)
                    lb = jnp.sum(l_scr[slot0 + 2 * p + 1], axis=1, keepdims=True)
                else:
                    la, lb = l_scr[slot0 + 2 * p], l_scr[slot0 + 2 * p + 1]
                inv = jnp.where(lo, 1.0 / la, 1.0 / lb)
                if gate_branch is not None:
                    h0 = g * nheads + 2 * p
                    src = lax.broadcasted_iota(jnp.int32, (LANES, LANES), 0)
                    dst_lo = lax.broadcasted_iota(jnp.int32, (LANES, LANES), 1) < HEAD_DIM
                    pick = jnp.where(src == jnp.where(dst_lo, 3 * h0 + gate_branch,
                                                      3 * (h0 + 1) + gate_branch), 1.0, 0.0)
                    logits = jnp.dot(gate_ref[bb].astype(BF16), pick.astype(BF16),
                                     preferred_element_type=F32)
                    inv = inv * jax.nn.sigmoid(logits)
                o_ref[bb, :, p * LANES:(p + 1) * LANES] = (acc_scr[pair0 + p] * inv).astype(o_ref.dtype)


def _tile_tables(nq, window_tiles, batch):
    qi, ki, fl = [], [], []
    for i in range(nq):
        lo = 0 if window_tiles is None else max(0, i - window_tiles)
        for j in range(lo, i + 1):
            qi.append(i)
            ki.append(j)
            fl.append((1 if j == lo else 0) | (2 if j == i else 0) | 4)
    rows = lambda a: jnp.tile(jnp.asarray(a, jnp.int32), batch)
    return rows(qi), rows(ki), rows(fl)


def _decayed_tile_tables(nq, batch, nb, ngroups, base):
    qi = np.repeat(np.arange(nq), np.arange(1, nq + 1))
    ki = np.concatenate([np.arange(i + 1) for i in range(nq)])
    nsteps = qi.shape[0]
    diag = jnp.asarray(qi == ki)
    gate = base.reshape(batch, nq, LANES)[:, :, F_LANE:F_LANE + FOX_HEADS]
    delta = gate[:, qi, :] - gate[:, np.minimum(ki + 1, nq - 1), :]
    top = jnp.max(delta.reshape(batch // nb, nb, nsteps, ngroups, FOX_HEADS // ngroups), axis=(1, 4))
    top = top.transpose(0, 2, 1).reshape(batch // nb * ngroups, nsteps)
    active = diag[None, :] | (top > ZERO_WEIGHT_LOG2)
    same_q = jnp.asarray(qi[:, None] == np.arange(nq)[None, :])
    first_ki = jnp.min(jnp.where(active[:, :, None] & same_q[None], ki[None, :, None], nq), axis=1)
    first = active & (jnp.asarray(ki)[None, :] == first_ki[:, qi])
    flags = first.astype(jnp.int32) + 2 * diag[None, :].astype(jnp.int32) + 4 * active.astype(jnp.int32)
    order = jnp.argsort(jnp.logical_not(active), axis=1, stable=True)
    n_active = jnp.sum(active, axis=1, keepdims=True)
    slot = jnp.arange(nsteps)[None, :]
    src = jnp.take_along_axis(order, jnp.minimum(slot, n_active - 1), axis=1)
    kept = slot < n_active
    take = lambda a: jnp.asarray(a, jnp.int32)[src].reshape(-1)
    fl = jnp.where(kept, jnp.take_along_axis(flags, src, axis=1), 0).reshape(-1)
    return take(qi), take(ki), fl.astype(jnp.int32)


def _flash(q, k, v, *, batch, seq, shared_kv, fixed_shift, nheads=NSA_HEADS, window=None, base=None,
           q_off=None, u=None, z=None, gate_branch=None, v_offset=0):
    nb = FLASH_ROWS if fixed_shift else SAFE_FLASH_ROWS
    nb = nb if batch % nb == 0 else 1
    nq = seq // TQ
    ngroups = NSA_HEADS // nheads
    nkv = NSA_KV_GROUPS
    da = k.shape[-1]
    has_delta = base is not None
    has_u = u is not None
    out_w = LANES * (nheads // 2)
    v_col = v_offset // out_w
    table_rows = batch // nb * ngroups
    if has_delta and fixed_shift:
        tabs = _decayed_tile_tables(nq, batch, nb, ngroups, base)
    else:
        tabs = _tile_tables(nq, None if window is None else window // TQ, table_rows)
    nsteps = int(tabs[0].shape[0]) // table_rows

    at = lambda tab, b, g, s: tab[(b * ngroups + g) * nsteps + s]
    by_q_heads = lambda b, g, s, qt, kt, ft: (b, g, at(qt, b, g, s), 0)
    by_k_heads = lambda b, g, s, qt, kt, ft: (b, g, at(kt, b, g, s), 0)
    by_q_tokens = lambda col: (lambda b, g, s, qt, kt, ft: (b, at(qt, b, g, s), g if col is None else col))
    by_k_tokens = lambda b, g, s, qt, kt, ft: (b, at(kt, b, g, s), g + v_col)

    in_specs, args = [], []
    if has_delta:
        in_specs.append(pl.BlockSpec(memory_space=pltpu.SMEM))
        args.append(base)
    in_specs.append(pl.BlockSpec((nb, nheads, TQ, LANES), by_q_heads))
    args.append(q)
    if has_delta:
        in_specs.append(pl.BlockSpec((nb, nheads, TQ, LANES), by_q_heads))
        args.append(q_off)
    if has_u:
        in_specs.append(pl.BlockSpec((nb, nkv, TQ, LANES), by_q_heads))
        args.append(u)
    if shared_kv:
        in_specs.append(pl.BlockSpec((nb, nkv, TQ, da), by_k_heads))
        in_specs.append(pl.BlockSpec((nb, nkv, TQ, LANES), by_k_heads))
    else:
        in_specs.append(pl.BlockSpec((nb, nheads, TQ, da), by_k_heads))
        in_specs.append(pl.BlockSpec((nb, TQ, out_w), by_k_tokens))
    args += [k, v]
    if gate_branch is not None:
        in_specs.append(pl.BlockSpec((nb, TQ, LANES), by_q_tokens(C_SMALL // LANES)))
        args.append(z)

    m_rows = 8 if fixed_shift else TQ
    scratch = [pltpu.VMEM((nb * nheads, m_rows, LANES), F32),
               pltpu.VMEM((nb * nheads, TQ, LANES), F32),
               pltpu.VMEM((nb * nheads // 2, TQ, LANES), F32)]
    if has_u:
        scratch.append(pltpu.VMEM((nb * nheads, TQ, 2 * LANES), BF16))

    kern = functools.partial(_flash_kernel, nheads=nheads, shared_kv=shared_kv, window=window,
                             has_delta=has_delta, has_u=has_u, gate_branch=gate_branch, nblk=nq,
                             fixed_shift=fixed_shift, nb=nb)
    return pl.pallas_call(
        kern,
        grid_spec=pltpu.PrefetchScalarGridSpec(
            num_scalar_prefetch=3,
            grid=(batch // nb, ngroups, nsteps),
            in_specs=in_specs,
            out_specs=pl.BlockSpec((nb, TQ, out_w), by_q_tokens(None)),
            scratch_shapes=scratch),
        out_shape=jax.ShapeDtypeStruct((batch, seq, 4 * LANES), BF16),
        compiler_params=_cparams(("parallel", "parallel", "arbitrary")),
        name="flash_" + ("fox" if has_delta else ("sel" if has_u else "win")),
    )(*tabs, *args)


def _cmp_kernel(q_ref, kc_ref, vc_ref, ov_ref, gate_ref, o_ref, u_ref):
    g = pl.program_id(1)
    qi = pl.program_id(2)
    tq = q_ref.shape[2]
    ncmp = kc_ref.shape[2]
    nslc = ov_ref.shape[1]

    t = qi * tq + lax.broadcasted_iota(jnp.int32, (tq, ncmp), 0)
    n = lax.broadcasted_iota(jnp.int32, (tq, ncmp), 1)
    valid = (n * NSA_CMP_STRIDE + (NSA_CMP_LEN - 1)) <= t
    kc = kc_ref[0, 0]
    vc = vc_ref[0, 0]
    ov = ov_ref[...]
    lane = lax.broadcasted_iota(jnp.int32, (tq, LANES), 1)
    lo = lane < HEAD_DIM
    gate_blk = gate_ref[0].astype(F32)

    imp = jnp.zeros((tq, nslc), F32)
    outs = []
    for r in range(NSA_GROUP_HEADS):
        s = lax.dot_general(q_ref[0, r], kc, (((1,), (1,)), ((), ())), preferred_element_type=F32)
        s = jnp.where(valid, s, NEG)
        m = jnp.max(s, axis=1, keepdims=True)
        e = jnp.where(valid, jnp.exp2(s - m), 0.0)
        d = jnp.sum(e, axis=1, keepdims=True)
        pb = (e / jnp.where(d > 0.0, d, 1.0)).astype(BF16)
        o = jnp.dot(pb, vc, preferred_element_type=F32)
        gcol = _gate_column(gate_blk, lane, 3 * (g * NSA_GROUP_HEADS + r))
        outs.append(o * gcol)
        imp = imp + jnp.dot(pb, ov, preferred_element_type=F32)
    for p in range(NSA_GROUP_HEADS // 2):
        o_ref[0, :, p * LANES:(p + 1) * LANES] = jnp.where(lo, outs[2 * p], outs[2 * p + 1]).astype(o_ref.dtype)
    _write_unselected(u_ref, imp, qi, tq, nslc)


def _cmp_fixed_kernel(q_ref, kc_ref, vc_ref, ov_ref, gate_ref, o_ref, u_ref, d_scr, acc_scr, imp_scr,
                      *, chunk):
    g = pl.program_id(1)
    qi = pl.program_id(2)
    tq = q_ref.shape[2]
    ncmp = kc_ref.shape[2]
    nslc = ov_ref.shape[1]
    t0 = qi * tq
    d_scr[...] = jnp.zeros(d_scr.shape, F32)
    acc_scr[...] = jnp.zeros(acc_scr.shape, F32)
    imp_scr[...] = jnp.zeros(imp_scr.shape, F32)

    for c in range(ncmp // chunk):
        first_end = c * chunk * NSA_CMP_STRIDE + NSA_CMP_LEN - 1
        last_end = ((c + 1) * chunk - 1) * NSA_CMP_STRIDE + NSA_CMP_LEN - 1

        def body(masked, c=c):
            rows = slice(c * chunk, (c + 1) * chunk)
            kc = kc_ref[0, 0, rows, :]
            vo = jnp.concatenate([vc_ref[0, 0, rows, :], ov_ref[rows, :]], axis=1)
            if masked:
                t = t0 + lax.broadcasted_iota(jnp.int32, (tq, chunk), 0)
                n = c * chunk + lax.broadcasted_iota(jnp.int32, (tq, chunk), 1)
                valid = (n * NSA_CMP_STRIDE + (NSA_CMP_LEN - 1)) <= t
            for r in range(NSA_GROUP_HEADS):
                s = lax.dot_general(q_ref[0, r], kc, (((1,), (1,)), ((), ())), preferred_element_type=F32)
                if masked:
                    s = jnp.where(valid, s, NEG)
                e = jnp.exp2(s)
                part = e[:, :LANES]
                for j in range(1, chunk // LANES):
                    part = part + e[:, j * LANES:(j + 1) * LANES]
                d_scr[r] = d_scr[r] + part
                both = jnp.dot(e.astype(BF16), vo, preferred_element_type=F32)
                acc_scr[r] = acc_scr[r] + both[:, :LANES]
                imp_scr[r] = imp_scr[r] + both[:, LANES:]

        pl.when((first_end <= t0 + (tq - 1)) & (last_end > t0))(functools.partial(body, True))
        pl.when(last_end <= t0)(functools.partial(body, False))

    lane = lax.broadcasted_iota(jnp.int32, (tq, LANES), 1)
    lo = lane < HEAD_DIM
    src = lax.broadcasted_iota(jnp.int32, (LANES, LANES), 0)
    dst_lo = lax.broadcasted_iota(jnp.int32, (LANES, LANES), 1) < HEAD_DIM
    imp = jnp.zeros((tq, nslc), F32)
    for p in range(NSA_GROUP_HEADS // 2):
        normed = []
        for r in (2 * p, 2 * p + 1):
            d = jnp.sum(d_scr[r], axis=1, keepdims=True)
            inv = 1.0 / jnp.where(d > 0.0, d, 1.0)
            normed.append(acc_scr[r] * inv)
            imp = imp + imp_scr[r] * inv
        h0 = g * NSA_GROUP_HEADS + 2 * p
        pick = jnp.where(src == jnp.where(dst_lo, 3 * h0, 3 * (h0 + 1)), 1.0, 0.0).astype(BF16)
        logits = jnp.dot(gate_ref[0].astype(BF16), pick, preferred_element_type=F32)
        o_ref[0, :, p * LANES:(p + 1) * LANES] = (jnp.where(lo, normed[0], normed[1])
                                                  * jax.nn.sigmoid(logits)).astype(o_ref.dtype)
    _write_unselected(u_ref, imp, qi, tq, nslc)


N_FORCED = 3


def _write_unselected(u_ref, imp, qi, tq, nslc):
    tpos = qi * tq + lax.broadcasted_iota(jnp.int32, (tq, nslc), 0)
    jblk = lax.broadcasted_iota(jnp.int32, (tq, nslc), 1)
    cur = jnp.right_shift(tpos, SEL_SHIFT)
    forced = (jblk == 0) | (jblk == cur) | (jblk == cur - 1)
    visible = jblk <= cur
    score_t = jnp.where(visible & jnp.logical_not(forced), imp, -jnp.inf).T

    def run(nrows):
        sc = score_t[:nrows]
        jt = lax.broadcasted_iota(jnp.int32, (nrows, tq), 0).astype(F32)
        for _ in range(NSA_SEL_TOPK - N_FORCED):
            mx = jnp.max(sc, axis=0, keepdims=True)
            first = jnp.min(jnp.where(sc == mx, jt, float(nslc)), axis=0, keepdims=True)
            sc = jnp.where(jt == first, -jnp.inf, sc)
        taken = jnp.where(sc == -jnp.inf, 1.0, 0.0)
        if nrows < nslc:
            taken = jnp.concatenate([taken, jnp.zeros((nslc - nrows, tq), F32)], axis=0)
        chosen = visible & (forced | (taken.T > 0.5))
        u_ref[0, 0] = jnp.where(chosen, 0.0, 1.0).astype(BF16)

    half = nslc // 2
    last_block = jnp.right_shift(qi * tq + (tq - 1), SEL_SHIFT)
    pl.when(last_block < half)(functools.partial(run, half))
    pl.when(last_block >= half)(functools.partial(run, nslc))


def _cmp_select(qn, kc, vc, ov, z, *, batch, seq, fixed_shift):
    ncmp = kc.shape[2]
    nslc = ov.shape[1]
    if fixed_shift:
        chunk = 2 * LANES if ncmp % (2 * LANES) == 0 else ncmp
        kern = functools.partial(_cmp_fixed_kernel, chunk=chunk)
        scratch = [pltpu.VMEM((NSA_GROUP_HEADS, CMP_TQ, LANES), F32) for _ in range(3)]
    else:
        kern, scratch = _cmp_kernel, []
    return pl.pallas_call(
        kern,
        scratch_shapes=scratch,
        grid=(batch, NSA_KV_GROUPS, seq // CMP_TQ),
        in_specs=[pl.BlockSpec((1, NSA_GROUP_HEADS, CMP_TQ, LANES), lambda b, g, i: (b, g, i, 0)),
                  pl.BlockSpec((1, 1, ncmp, LANES), lambda b, g, i: (b, g, 0, 0)),
                  pl.BlockSpec((1, 1, ncmp, LANES), lambda b, g, i: (b, g, 0, 0)),
                  pl.BlockSpec((ncmp, nslc), lambda b, g, i: (0, 0)),
                  pl.BlockSpec((1, CMP_TQ, LANES), lambda b, g, i: (b, i, C_SMALL // LANES))],
        out_specs=[pl.BlockSpec((1, CMP_TQ, 2 * LANES), lambda b, g, i: (b, i, g)),
                   pl.BlockSpec((1, 1, CMP_TQ, nslc), lambda b, g, i: (b, g, i, 0))],
        out_shape=[jax.ShapeDtypeStruct((batch, seq, NSA_WIDTH), BF16),
                   jax.ShapeDtypeStruct((batch, NSA_KV_GROUPS, seq, nslc), BF16)],
        compiler_params=_cparams(("parallel", "parallel", "parallel")),
        name="cmp_select",
    )(qn, kc, vc, ov, z)


def _head_rms(x, g):
    return x * lax.rsqrt(jnp.mean(x * x, axis=-1, keepdims=True) + EPS) * g


def _mem_kernel(zq_ref, kv_ref, qg_ref, kg_ref, o_ref, kn_scr, vb_scr):
    @pl.when(pl.program_id(1) == 0)
    def _prep():
        for h in range(MEM_HEADS):
            kh = kv_ref[0, :, h * MEM_HEAD_DIM:(h + 1) * MEM_HEAD_DIM]
            kn_scr[h] = _head_rms(kh, kg_ref[...]).astype(BF16)
            vb_scr[h] = kv_ref[0, :, MEM_WIDTH + h * MEM_HEAD_DIM:
                               MEM_WIDTH + (h + 1) * MEM_HEAD_DIM].astype(BF16)

    for h in range(MEM_HEADS):
        sl = slice(h * MEM_HEAD_DIM, (h + 1) * MEM_HEAD_DIM)
        qh = (_head_rms(zq_ref[0, :, sl].astype(F32), qg_ref[...])
              * ((MEM_HEAD_DIM ** -0.5) * LOG2E)).astype(BF16)
        s = lax.dot_general(qh, kn_scr[h], (((1,), (1,)), ((), ())), preferred_element_type=F32)
        m = jnp.max(s, axis=1, keepdims=True)
        e = jnp.exp2(s - m)
        p = (e / jnp.sum(e, axis=1, keepdims=True)).astype(BF16)
        o_ref[0, :, sl] = jnp.dot(p, vb_scr[h], preferred_element_type=F32).astype(o_ref.dtype)


def _mem_attention(z, mem_kv, q_g, k_g, *, batch, seq):
    mlen = mem_kv.shape[1]
    tq = TQ
    return pl.pallas_call(
        _mem_kernel,
        grid=(batch, seq // tq),
        in_specs=[pl.BlockSpec((1, tq, MEM_WIDTH), lambda b, i: (b, i, C_MEM_Q // MEM_WIDTH)),
                  pl.BlockSpec((1, mlen, 2 * MEM_WIDTH), lambda b, i: (b, 0, 0)),
                  pl.BlockSpec((1, MEM_HEAD_DIM), lambda b, i: (0, 0)),
                  pl.BlockSpec((1, MEM_HEAD_DIM), lambda b, i: (0, 0))],
        out_specs=pl.BlockSpec((1, tq, MEM_WIDTH), lambda b, i: (b, i, 0)),
        out_shape=jax.ShapeDtypeStruct((batch, seq, MEM_WIDTH), BF16),
        scratch_shapes=[pltpu.VMEM((MEM_HEADS, mlen, MEM_HEAD_DIM), BF16),
                        pltpu.VMEM((MEM_HEADS, mlen, MEM_HEAD_DIM), BF16)],
        compiler_params=_cparams(("parallel", "arbitrary")),
        name="mem_attention",
    )(z, mem_kv, q_g.reshape(1, MEM_HEAD_DIM), k_g.reshape(1, MEM_HEAD_DIM))


def _sigmoid(x):
    return 0.5 * jnp.tanh(0.5 * x) + 0.5


def _out_kernel(x_ref, oc_ref, os_ref, ow_ref, ob_ref, om_ref, sa_ref, sb_ref, sm_ref,
                g0_ref, g1_ref, g2_ref, wa_ref, wb_ref, wm_ref, wo_ref, y_ref):
    silu = lambda ref: (lambda x: x * _sigmoid(x))(ref[...].astype(F32))
    oa = (oc_ref[...].astype(F32) + os_ref[...].astype(F32) + ow_ref[...].astype(F32)) * silu(sa_ref)
    ob = ob_ref[...].astype(F32) * silu(sb_ref)
    om = om_ref[...].astype(F32) * silu(sm_ref)
    gate = lambda ref: _sigmoid(ref[...].astype(F32))
    u = (gate(g0_ref) * jnp.dot(oa.astype(BF16), wa_ref[...], preferred_element_type=F32)
         + gate(g1_ref) * jnp.dot(ob.astype(BF16), wb_ref[...], preferred_element_type=F32)
         + gate(g2_ref) * jnp.dot(om.astype(BF16), wm_ref[...], preferred_element_type=F32))
    y_ref[...] = x_ref[...] + jnp.dot(u.astype(BF16), wo_ref[...], preferred_element_type=F32)


def _out_proj(x2d, o_c, o_s, o_w, o_b, o_m, z2d, wa, wb, wm, wo, layer):
    m = x2d.shape[0]
    tm = 512
    w512 = 512
    row512 = lambda c: pl.BlockSpec((tm, w512), lambda i, c=c: (i, c))
    row1024 = lambda c: pl.BlockSpec((tm, D_MODEL), lambda i, c=c: (i, c))
    full = lambda shape: pl.BlockSpec((None,) + shape, lambda i: (layer, 0, 0))
    return pl.pallas_call(
        _out_kernel,
        grid=(m // tm,),
        in_specs=[row1024(0), row512(0), row512(0), row512(0), row512(0), row512(0),
                  row512(C_NSA_SILU // w512), row512(C_FOX_SILU // w512), row512(C_MEM_SILU // w512),
                  row1024(C_MERGE // D_MODEL), row1024(C_MERGE // D_MODEL + 1),
                  row1024(C_MERGE // D_MODEL + 2),
                  full((w512, D_MODEL)), full((w512, D_MODEL)), full((w512, D_MODEL)),
                  full((D_MODEL, D_MODEL))],
        out_specs=row1024(0),
        out_shape=jax.ShapeDtypeStruct((m, D_MODEL), F32),
        compiler_params=_cparams(("parallel",)),
        name="out_proj",
    )(x2d, o_c, o_s, o_w, o_b, o_m, z2d, z2d, z2d, z2d, z2d, z2d, wa, wb, wm, wo)


def _fox_head_order(a, order, sections):
    nl, d, _ = a.shape
    a = a.reshape(nl, d, sections, FOX_HEADS, HEAD_DIM)
    return jnp.take_along_axis(a, order[:, None, None, :, None], axis=3).reshape(nl, d, -1)


def _permute_w_in(w_in, fox_order):
    o = np.cumsum([0, NSA_WIDTH, 6 * NSA_KV_GROUPS * HEAD_DIM, 3 * NSA_HEADS, NSA_WIDTH,
                   3 * FOX_WIDTH, FOX_HEADS, FOX_WIDTH, MEM_WIDTH, MEM_WIDTH, 3 * D_MODEL])
    nsa_q, nsa_kv, nsa_gate, nsa_silu, fox_qkv, fox_f, fox_silu, mem_q, mem_silu, merge = [
        w_in[..., o[i]:o[i + 1]] for i in range(10)]
    fox_qkv = _fox_head_order(fox_qkv, fox_order, 3)
    fox_silu = _fox_head_order(fox_silu, fox_order, 1)
    fox_f = jnp.take_along_axis(fox_f, fox_order[:, None, :], axis=2)
    pad = jnp.zeros(w_in.shape[:2] + (LANES - 3 * NSA_HEADS - FOX_HEADS,), w_in.dtype)
    return jnp.concatenate([nsa_q, nsa_silu, fox_qkv, fox_silu, mem_q, mem_silu, merge, nsa_kv,
                            nsa_gate, fox_f, pad], axis=-1).astype(BF16)


def _pad_lanes(x, width=LANES):
    return jnp.pad(x, [(0, 0)] * (x.ndim - 1) + [(0, width - x.shape[-1])])


def _tile2(g):
    return jnp.concatenate([g, g], axis=-1)[:, None, :]


def _overlap(n_cmp_pad, n_slc):
    cs = np.arange(n_cmp_pad)[:, None] * NSA_CMP_STRIDE
    ss = np.arange(NSLC_PAD)[None, :] * NSA_SEL_LEN
    ov = np.clip(np.minimum(cs + NSA_CMP_LEN, ss + NSA_SEL_LEN) - np.maximum(cs, ss), 0, None)
    ov = ov / NSA_CMP_LEN
    ov[-1] = 0.0
    ov[:, n_slc:] = 0.0
    return jnp.asarray(ov, BF16)


def _aug_permutations():
    pq = np.zeros((6 * LANES, FOX_HEADS * LANES), np.float32)
    pk = np.zeros((6 * LANES, FOX_HEADS * LANES), np.float32)
    for h in range(FOX_HEADS):
        for i in range(3):
            pq[i * LANES + F_LANE + h, h * LANES + AUG + i] = 1.0
            pk[i * LANES + F_LANE + h, h * LANES + AUG + 3 + i] = -1.0
            pk[(3 + i) * LANES + F_LANE + h, h * LANES + AUG + 6 + i] = -1.0
    return jnp.asarray(pq, BF16), jnp.asarray(pk, BF16)


def _compress_w1(w1):
    nl = w1.shape[0]
    halves = w1.reshape(nl, 2, NSA_CMP_STRIDE, 1, HEAD_DIM, NSA_CMP_HIDDEN)
    eye = jnp.eye(NSA_KV_GROUPS, dtype=w1.dtype)
    out = jnp.einsum("pg,zhldn->zlpdghn", eye, halves[:, :, :, 0])
    return out.reshape(nl, NSA_CMP_STRIDE * NSA_KV_GROUPS * HEAD_DIM,
                       NSA_KV_GROUPS * 2 * NSA_CMP_HIDDEN).astype(BF16)


def _layer(x, mem, p, big, layer, consts):
    batch, seq, _ = x.shape
    m_rows = batch * seq
    cos4, sin_signed, tri, head_ones, perm_q, perm_k, ov = consts
    z2d = _norm_matmul(x.reshape(m_rows, D_MODEL), p["norm_g"], big["w_in"], layer, 512, Z_WIDTH, BF16)
    z = z2d.reshape(batch, seq, Z_WIDTH)

    (qn, kcr, vcr, ks, vs, kw, vw, fqd, fqo, fka, base) = _prep(
        z, cos4, sin_signed, tri, head_ones, perm_q, perm_k,
        p["gq"], p["gks"], p["gkw"], p["gfq"], p["gfk"], p["fb"],
        p["s_sel"], p["s_win"], p["s_fox"], batch=batch, seq=seq)

    nchunk = seq // NSA_CMP_STRIDE
    kc, vc = _compress(kcr.reshape(batch, nchunk, NSA_CMP_STRIDE * LANES),
                       vcr.reshape(batch, nchunk, NSA_CMP_STRIDE * LANES),
                       p["w1k"], p["w1v"], p["bk"], p["bv"], p["w2k"], p["w2v"], p["gkc"], p["s_cmp"],
                       batch=batch)

    def attend(fixed_shift):
        o_c, unsel = _cmp_select(qn, kc, vc, ov, z, batch=batch, seq=seq, fixed_shift=fixed_shift)
        o_s = _flash(qn, ks, vs, batch=batch, seq=seq, shared_kv=True, fixed_shift=fixed_shift,
                     u=unsel, z=z, gate_branch=1)
        o_w = _flash(qn, kw, vw, batch=batch, seq=seq, shared_kv=True, fixed_shift=fixed_shift,
                     window=NSA_WINDOW, z=z, gate_branch=2)
        o_b = _flash(fqd, fka, z, batch=batch, seq=seq, shared_kv=False, fixed_shift=fixed_shift,
                     nheads=FOX_HEADS // 2, base=base.reshape(-1), q_off=fqo, v_offset=C_FOX_V)
        return o_c, o_s, o_w, o_b

    o_c, o_s, o_w, o_b = lax.cond(p["bound_ok"], lambda: attend(True), lambda: attend(False))

    mlen = mem.shape[1]
    mem_kv = _norm_matmul(mem.reshape(batch * mlen, D_MODEL), p["mem_norm_g"], big["w_mem_kv"], layer,
                          min(512, batch * mlen), 2 * MEM_WIDTH, F32).reshape(batch, mlen, 2 * MEM_WIDTH)
    o_m = _mem_attention(z, mem_kv, p["mem_q_norm"], p["mem_k_norm"], batch=batch, seq=seq)

    y = _out_proj(x.reshape(m_rows, D_MODEL), o_c.reshape(m_rows, -1), o_s.reshape(m_rows, -1),
                  o_w.reshape(m_rows, -1), o_b.reshape(m_rows, -1), o_m.reshape(m_rows, -1), z2d,
                  big["w_branch_a"], big["w_branch_b"], big["w_branch_m"], big["w_out"], layer)
    return y.reshape(batch, seq, D_MODEL)


def kernel(x, mem, norm_g, mem_norm_g, w_in, nsa_q_norm, nsa_k_norm, cmp_pe_k, cmp_w1_k, cmp_w2_k,
           cmp_pe_v, cmp_w1_v, cmp_w2_v, fox_q_norm, fox_k_norm, fox_f_bias, mem_q_norm, mem_k_norm,
           w_mem_kv, w_branch_a, w_branch_b, w_branch_m, w_out):
    batch, seq, _ = x.shape
    depth = w_in.shape[0]
    n_slc = seq // NSA_SEL_LEN
    assert n_slc <= NSLC_PAD and seq % TQ == 0 and NSA_WINDOW == TQ

    half = HEAD_DIM // 2
    inv_freq = ROPE_THETA ** (-jnp.arange(half, dtype=F32) / half)
    ang = jnp.arange(seq).astype(F32)[:, None] * inv_freq[None, :]
    cos, sin = jnp.cos(ang), jnp.sin(ang)
    cos4 = jnp.concatenate([cos, cos, cos, cos], axis=-1)
    sin_signed = jnp.concatenate([-sin, sin, -sin, sin], axis=-1)
    tri = jnp.asarray(np.tril(np.ones((TQ, TQ), np.float32)), BF16)
    head_ones = jnp.asarray(np.kron(np.eye(2), np.ones((HEAD_DIM, HEAD_DIM))), BF16)
    consts = (cos4, sin_signed, tri, head_ones, *_aug_permutations(),
              _overlap(seq // NSA_CMP_STRIDE, n_slc))

    hp = lax.Precision.HIGHEST
    pe_bias = lambda pe, w1: jnp.einsum("lk,lkn->ln", pe.reshape(depth, -1), w1, precision=hp)[:, None, :]
    fox_order = jnp.argsort(fox_f_bias, axis=-1)
    fox_bias_sorted = jnp.take_along_axis(fox_f_bias, fox_order, axis=-1)
    w_branch_b_sorted = jnp.take_along_axis(
        w_branch_b.reshape(depth, FOX_HEADS, HEAD_DIM, D_MODEL), fox_order[:, :, None, None], axis=1
    ).reshape(depth, FOX_WIDTH, D_MODEL)
    fb = jnp.zeros((depth, 1, LANES), F32).at[:, 0, F_LANE:F_LANE + FOX_HEADS].set(fox_bias_sorted)

    def logit_bound(gq, gk):
        bound = (BOUND_MARGIN * HEAD_DIM * QK_SCALE
                 * jnp.max(jnp.abs(gq), axis=-1) * jnp.max(jnp.abs(gk), axis=-1))
        return bound.astype(BF16).astype(F32)

    bounds = [logit_bound(nsa_q_norm, nsa_k_norm[:, 1]), logit_bound(nsa_q_norm, nsa_k_norm[:, 2]),
              logit_bound(fox_q_norm, fox_k_norm), logit_bound(nsa_q_norm, nsa_k_norm[:, 0])]
    bound_ok = functools.reduce(jnp.maximum, bounds) <= MAX_FIXED_BOUND
    bound_row = lambda s: jnp.broadcast_to(s[:, None, None], (depth, 1, LANES))
    stacked = dict(
        s_sel=bound_row(bounds[0]), s_win=bound_row(bounds[1]), s_fox=bound_row(bounds[2]),
        s_cmp=bound_row(bounds[3]),
        bound_ok=bound_ok,
        norm_g=norm_g, mem_norm_g=mem_norm_g,
        gq=_tile2(nsa_q_norm), gks=_tile2(nsa_k_norm[:, 1]), gkw=_tile2(nsa_k_norm[:, 2]),
        gkc=_pad_lanes(nsa_k_norm[:, 0])[:, None, :], gfq=_tile2(fox_q_norm), gfk=_tile2(fox_k_norm),
        fb=fb, w1k=_compress_w1(cmp_w1_k), w1v=_compress_w1(cmp_w1_v),
        bk=pe_bias(cmp_pe_k, cmp_w1_k), bv=pe_bias(cmp_pe_v, cmp_w1_v),
        w2k=_pad_lanes(cmp_w2_k).astype(BF16),
        w2v=jnp.concatenate([cmp_w2_v, cmp_w2_v], axis=-1).astype(BF16),
        mem_q_norm=mem_q_norm, mem_k_norm=mem_k_norm)
    big = dict(w_in=_permute_w_in(w_in, fox_order), w_mem_kv=w_mem_kv.astype(BF16),
               w_branch_a=w_branch_a.astype(BF16), w_branch_b=w_branch_b_sorted.astype(BF16),
               w_branch_m=w_branch_m.astype(BF16), w_out=w_out.astype(BF16))
    for l in range(depth):
        x = _layer(x, mem, {k: v[l] for k, v in stacked.items()}, big, l, consts)
    return x
```
